```python
import math
import jax, jax.numpy as jnp
from jax import lax
import numpy as np

D_MODEL = 1024
BATCH = 16
SEQ = 256
DEPTH = 2
DEC_BATCH = 2
DEC_SEQ = 4096
PAST_LEN = 512

GRID_W = 64
HEAD_DIM = 64
GQA_HEADS = 8
GQA_KV_HEADS = 2
GQA_GROUP = GQA_HEADS // GQA_KV_HEADS
GLA_HEADS = 4
GLA_DK = 64
GLA_DV = 128
GLA_RANK = 16
GLA_TAU = 16.0
GLA_CHUNK = 64
DIFF_HEADS = 4
DIFF_DV = 2 * HEAD_DIM
N_BRANCH = 3
BRANCH_W = 512
FFN_HIDDEN = ((8 * D_MODEL + 3 * 256 - 1) // (3 * 256)) * 256
ROPE_THETA = 10000.0
Q_BLOCK = 128
EPS = 1e-6
SPLIT_SIZES = (
    GQA_HEADS * HEAD_DIM, GQA_KV_HEADS * HEAD_DIM, GQA_KV_HEADS * HEAD_DIM,
    GLA_HEADS * GLA_DK, GLA_HEADS * GLA_DK, GLA_HEADS * GLA_DV, 2 * GLA_RANK, GLA_HEADS * GLA_DV,
    DIFF_HEADS * 2 * HEAD_DIM, DIFF_HEADS * 2 * HEAD_DIM, DIFF_HEADS * DIFF_DV,
    N_BRANCH * D_MODEL,
)
N_IN = sum(SPLIT_SIZES)

kernel_name = 'hybrid_diffusion_gqa_gla_diffattn_step'


def rms_norm(x, g):
    x32 = x.astype(jnp.float32)
    y = x32 * lax.rsqrt(jnp.mean(x32 * x32, axis=-1, keepdims=True) + EPS)
    return (y * g.astype(jnp.float32)).astype(x.dtype)


def split_columns(z):
    points, acc = [], 0
    for s in SPLIT_SIZES[:-1]:
        acc += s
        points.append(acc)
    return jnp.split(z, points, axis=-1)


def axial_rope_angles(n_tokens):
    n_rows = n_tokens // GRID_W
    row = jnp.repeat(jnp.arange(n_rows, dtype=jnp.float32), GRID_W)
    col = jnp.tile(jnp.arange(GRID_W, dtype=jnp.float32), n_rows)
    n_freq = HEAD_DIM // 4
    freqs = ROPE_THETA ** (-jnp.arange(n_freq, dtype=jnp.float32) / n_freq)
    return row[:, None] * freqs, col[:, None] * freqs


def rope_half(x, ang):
    ang = ang.reshape((1, ang.shape[0]) + (1,) * (x.ndim - 3) + (ang.shape[-1],))
    cos, sin = jnp.cos(ang), jnp.sin(ang)
    x1, x2 = jnp.split(x, 2, axis=-1)
    return jnp.concatenate([x1 * cos - x2 * sin, x2 * cos + x1 * sin], axis=-1)


def apply_axial_rope(x, angles):
    ang_row, ang_col = angles
    x32 = x.astype(jnp.float32)
    half = HEAD_DIM // 2
    out = jnp.concatenate([rope_half(x32[..., :half], ang_row), rope_half(x32[..., half:], ang_col)], axis=-1)
    return out.astype(x.dtype)


def sweep_query_blocks(fn, q):
    b, n = q.shape[:2]
    nb = n // Q_BLOCK
    blocks = jnp.moveaxis(q.reshape((b, nb, Q_BLOCK) + q.shape[2:]), 1, 0)
    out = lax.map(fn, blocks)
    return jnp.moveaxis(out, 0, 1).reshape((b, n) + out.shape[3:])


def gqa_block(q, k, v):
    s = jnp.einsum('bqkgd,bskd->bkgqs', q, k).astype(jnp.float32) * (HEAD_DIM ** -0.5)
    p = jax.nn.softmax(s, axis=-1).astype(v.dtype)
    return jnp.einsum('bkgqs,bskd->bqkgd', p, v)


def diff_block(q, k, v, lam):
    s = jnp.einsum('bqhmd,bshmd->bhmqs', q, k).astype(jnp.float32) * (HEAD_DIM ** -0.5)
    p = jax.nn.softmax(s, axis=-1)
    p = p[:, :, 0] - lam * p[:, :, 1]
    return jnp.einsum('bhqs,bshe->bqhe', p.astype(v.dtype), v)


def gla_chunked(q, k, v, log_a, s0):
    b, n, h, dk = q.shape
    dv = v.shape[-1]
    nc = n // GLA_CHUNK
    f32 = jnp.float32
    q = q.astype(f32).reshape(b, nc, GLA_CHUNK, h, dk)
    k = k.astype(f32).reshape(b, nc, GLA_CHUNK, h, dk)
    v = v.astype(f32).reshape(b, nc, GLA_CHUNK, h, dv)
    cum = jnp.cumsum(log_a.astype(f32).reshape(b, nc, GLA_CHUNK, h, dk), axis=2)
    last = cum[:, :, -1]
    q_dec = q * jnp.exp(cum)
    k_inv = k * jnp.exp(-cum)
    k_to_end = k * jnp.exp(last[:, :, None] - cum)
    mask = jnp.tril(jnp.ones((GLA_CHUNK, GLA_CHUNK), dtype=bool))
    a = jnp.where(mask, jnp.einsum('bcihd,bcjhd->bchij', q_dec, k_inv), 0.0)
    o_intra = jnp.einsum('bchij,bcjhe->bcihe', a, v)
    ds = jnp.einsum('bcjhd,bcjhe->bchde', k_to_end, v)

    def step(s, inp):
        decay, d = inp
        return decay[..., None] * s + d, s

    s_final, s_prev = lax.scan(step, s0.astype(f32), (jnp.moveaxis(jnp.exp(last), 1, 0), jnp.moveaxis(ds, 1, 0)))
    s_prev = jnp.moveaxis(s_prev, 0, 1)
    o_inter = jnp.einsum('bcihd,bchde->bcihe', q_dec, s_prev)
    return (o_intra + o_inter).reshape(b, n, h, dv), s_final


def gla_bidirectional(q, k, v, log_a, s0):
    o_f, s_f = gla_chunked(q, k, v, log_a[:, :, 0], s0[:, 0])
    flip = lambda t: jnp.flip(t, axis=1)
    o_b, s_b = gla_chunked(flip(q), flip(k), flip(v), flip(log_a[:, :, 1]), s0[:, 1])
    return o_f + flip(o_b), jnp.stack([s_f, s_b], axis=1)


def token_mixers(h, p, l, angles, ctx):
    b, n = h.shape[:2]
    (gq, gk, gv, lq, lk, lv, lr, lgo, dq, dk, dv, merge) = split_columns(h @ p['w_in'][l])
    gq = rms_norm(gq.reshape(b, n, GQA_HEADS, HEAD_DIM), p['gqa_q_norm'][l])
    gk = rms_norm(gk.reshape(b, n, GQA_KV_HEADS, HEAD_DIM), p['gqa_k_norm'][l])
    gv = gv.reshape(b, n, GQA_KV_HEADS, HEAD_DIM)
    dq = rms_norm(dq.reshape(b, n, DIFF_HEADS, 2, HEAD_DIM), p['diff_q_norm'][l])
    dk = rms_norm(dk.reshape(b, n, DIFF_HEADS, 2, HEAD_DIM), p['diff_k_norm'][l])
    dv = dv.reshape(b, n, DIFF_HEADS, DIFF_DV)
    if angles is not None:
        gq, gk = apply_axial_rope(gq, angles), apply_axial_rope(gk, angles)
        dq, dk = apply_axial_rope(dq, angles), apply_axial_rope(dk, angles)
    if ctx is None:
        gk_all, gv_all, dk_all, dv_all = gk, gv, dk, dv
        s0 = jnp.zeros((b, 2, GLA_HEADS, GLA_DK, GLA_DV), jnp.float32)
    else:
        c_gk, c_gv, c_state, c_dk, c_dv = ctx
        gk_all = jnp.concatenate([gk, c_gk.astype(gk.dtype)], axis=1)
        gv_all = jnp.concatenate([gv, c_gv.astype(gv.dtype)], axis=1)
        dk_all = jnp.concatenate([dk, c_dk.astype(dk.dtype)], axis=1)
        dv_all = jnp.concatenate([dv, c_dv.astype(dv.dtype)], axis=1)
        s0 = c_state
    gqa_out = sweep_query_blocks(lambda qb: gqa_block(qb, gk_all, gv_all),
                                 gq.reshape(b, n, GQA_KV_HEADS, GQA_GROUP, HEAD_DIM)).reshape(b, n, BRANCH_W)
    lq = lq.reshape(b, n, GLA_HEADS, GLA_DK) * (GLA_DK ** -0.5)
    lk = lk.reshape(b, n, GLA_HEADS, GLA_DK)
    lv = lv.reshape(b, n, GLA_HEADS, GLA_DV)
    z = jnp.einsum('bnzr,zrf->bnzf', lr.reshape(b, n, 2, GLA_RANK), p['gla_alpha_up'][l]) + p['gla_alpha_bias'][l]
    log_a = (jax.nn.log_sigmoid(z.astype(jnp.float32)) / GLA_TAU).reshape(b, n, 2, GLA_HEADS, GLA_DK)
    o_gla, s_new = gla_bidirectional(lq, lk, lv, log_a, s0)
    gla_out = rms_norm(o_gla.astype(h.dtype), p['gla_out_norm'][l]) * jax.nn.silu(lgo.reshape(b, n, GLA_HEADS, GLA_DV))
    gla_out = gla_out.reshape(b, n, BRANCH_W)
    lam_init = 0.8 - 0.6 * math.exp(-0.3 * l)
    lp = p['diff_lambda'][l].astype(jnp.float32)
    lam = jnp.exp(jnp.sum(lp[0] * lp[1])) - jnp.exp(jnp.sum(lp[2] * lp[3])) + lam_init
    d_out = sweep_query_blocks(lambda qb: diff_block(qb, dk_all, dv_all, lam), dq)
    diff_out = (rms_norm(d_out, p['diff_sub_norm'][l]) * (1.0 - lam_init)).reshape(b, n, BRANCH_W)
    outs = jnp.stack([gqa_out, gla_out, diff_out], axis=2)
    y = jnp.einsum('bnie,ied->bnid', outs, p['w_branch'][l])
    gates = jax.nn.sigmoid(merge.reshape(b, n, N_BRANCH, D_MODEL))
    mixed = jnp.sum(gates * y, axis=2) @ p['w_out'][l]
    return mixed, (gk, gv, s_new.astype(h.dtype), dk, dv)


def layer(x, cond, p, l, angles, ctx):
    mod = jax.nn.silu(cond) @ p['w_mod'][l] + p['b_mod'][l]
    sh1, sc1, g1, sh2, sc2, g2 = jnp.split(mod, 6, axis=-1)
    h = rms_norm(x, p['norm1'][l]) * (1.0 + sc1) + sh1
    mixed, ctx_out = token_mixers(h, p, l, angles, ctx)
    x = x + g1 * mixed
    h = rms_norm(x, p['norm2'][l]) * (1.0 + sc2) + sh2
    a, u = jnp.split(h @ p['w_ffn_in'][l], 2, axis=-1)
    x = x + g2 * ((jax.nn.silu(a) * u) @ p['w_ffn_out'][l])
    return x, ctx_out


def setup_inputs(seed: int = 0) -> dict:
    key = jax.random.key(seed)
    ks = jax.random.split(key, 32)
    nrm = lambda k, shape, s=1.0: jax.random.normal(k, shape, jnp.float32) * s
    gain = lambda k, shape: 1.0 + 0.02 * jax.random.normal(k, shape, jnp.float32)
    return {
        'x_prompt': nrm(ks[0], (BATCH, SEQ, D_MODEL)),
        'x_sample': nrm(ks[1], (DEC_BATCH, DEC_SEQ, D_MODEL)),
        'c': nrm(ks[2], (DEC_BATCH, D_MODEL)),
        'cache_gqa_k': nrm(ks[3], (DEC_BATCH, DEPTH, PAST_LEN, GQA_KV_HEADS, HEAD_DIM)),
        'cache_gqa_v': nrm(ks[4], (DEC_BATCH, DEPTH, PAST_LEN, GQA_KV_HEADS, HEAD_DIM)),
        'state_gla': nrm(ks[5], (DEC_BATCH, DEPTH, 2, GLA_HEADS, GLA_DK, GLA_DV)),
        'cache_diff_k': nrm(ks[6], (DEC_BATCH, DEPTH, PAST_LEN, DIFF_HEADS, 2, HEAD_DIM)),
        'cache_diff_v': nrm(ks[7], (DEC_BATCH, DEPTH, PAST_LEN, DIFF_HEADS, DIFF_DV)),
        'c_ctx': nrm(ks[8], (D_MODEL,)),
        'w_mod': nrm(ks[9], (DEPTH, D_MODEL, 6 * D_MODEL), 0.5 * D_MODEL ** -0.5),
        'b_mod': nrm(ks[10], (DEPTH, 6 * D_MODEL), 0.01),
        'norm1': gain(ks[11], (DEPTH, D_MODEL)),
        'norm2': gain(ks[12], (DEPTH, D_MODEL)),
        'w_in': nrm(ks[13], (DEPTH, D_MODEL, N_IN), D_MODEL ** -0.5),
        'gqa_q_norm': gain(ks[14], (DEPTH, HEAD_DIM)),
        'gqa_k_norm': gain(ks[15], (DEPTH, HEAD_DIM)),
        'gla_alpha_up': nrm(ks[16], (DEPTH, 2, GLA_RANK, GLA_HEADS * GLA_DK), GLA_RANK ** -0.5),
        'gla_alpha_bias': nrm(ks[17], (DEPTH, 2, GLA_HEADS * GLA_DK), 0.1),
        'gla_out_norm': gain(ks[18], (DEPTH, GLA_DV)),
        'diff_q_norm': gain(ks[19], (DEPTH, HEAD_DIM)),
        'diff_k_norm': gain(ks[20], (DEPTH, HEAD_DIM)),
        'diff_lambda': nrm(ks[21], (DEPTH, 4, HEAD_DIM), 0.1),
        'diff_sub_norm': gain(ks[22], (DEPTH, DIFF_DV)),
        'w_branch': nrm(ks[23], (DEPTH, N_BRANCH, BRANCH_W, D_MODEL), BRANCH_W ** -0.5),
        'w_out': nrm(ks[24], (DEPTH, D_MODEL, D_MODEL), D_MODEL ** -0.5),
        'w_ffn_in': nrm(ks[25], (DEPTH, D_MODEL, 2 * FFN_HIDDEN), D_MODEL ** -0.5),
        'w_ffn_out': nrm(ks[26], (DEPTH, FFN_HIDDEN, D_MODEL), FFN_HIDDEN ** -0.5),
    }


def reference(x_prompt, x_sample, c, cache_gqa_k, cache_gqa_v, state_gla, cache_diff_k, cache_diff_v,
              c_ctx, w_mod, b_mod, norm1, norm2, w_in, gqa_q_norm, gqa_k_norm, gla_alpha_up, gla_alpha_bias,
              gla_out_norm, diff_q_norm, diff_k_norm, diff_lambda, diff_sub_norm, w_branch, w_out,
              w_ffn_in, w_ffn_out):
    p = {
        'w_mod': w_mod, 'b_mod': b_mod, 'norm1': norm1, 'norm2': norm2, 'w_in': w_in,
        'gqa_q_norm': gqa_q_norm, 'gqa_k_norm': gqa_k_norm, 'gla_alpha_up': gla_alpha_up,
        'gla_alpha_bias': gla_alpha_bias, 'gla_out_norm': gla_out_norm, 'diff_q_norm': diff_q_norm,
        'diff_k_norm': diff_k_norm, 'diff_lambda': diff_lambda, 'diff_sub_norm': diff_sub_norm,
        'w_branch': w_branch, 'w_out': w_out, 'w_ffn_in': w_ffn_in, 'w_ffn_out': w_ffn_out,
    }
    y_prompt = x_prompt
    cond_ctx = c_ctx[None, None, :]
    ctx_tensors = []
    for l in range(DEPTH):
        y_prompt, ctx_t = layer(y_prompt, cond_ctx, p, l, None, None)
        ctx_tensors.append(ctx_t)
    new_gqa_k = jnp.stack([t[0] for t in ctx_tensors], axis=1)
    new_gqa_v = jnp.stack([t[1] for t in ctx_tensors], axis=1)
    new_state_gla = jnp.stack([t[2] for t in ctx_tensors], axis=1)
    new_diff_k = jnp.stack([t[3] for t in ctx_tensors], axis=1)
    new_diff_v = jnp.stack([t[4] for t in ctx_tensors], axis=1)
    angles = axial_rope_angles(x_sample.shape[1])
    y_sample = x_sample
    cond_lat = c[:, None, :]
    for l in range(DEPTH):
        ctx = (cache_gqa_k[:, l], cache_gqa_v[:, l], state_gla[:, l], cache_diff_k[:, l], cache_diff_v[:, l])
        y_sample, _ = layer(y_sample, cond_lat, p, l, angles, ctx)
    return (y_prompt, y_sample, new_gqa_k, new_gqa_v, new_state_gla, new_diff_k, new_diff_v)
```

```python
import functools
import math

import jax
import jax.numpy as jnp
from jax import lax
from jax.experimental import pallas as pl
from jax.experimental.pallas import tpu as pltpu

D_MODEL = 1024
DEPTH = 2
PAST_LEN = 512
GRID_W = 64
HEAD_DIM = 64
GQA_HEADS = 8
GQA_KV_HEADS = 2
GQA_GROUP = GQA_HEADS // GQA_KV_HEADS
GLA_HEADS = 4
GLA_DK = 64
GLA_DV = 128
GLA_RANK = 16
GLA_TAU = 16.0
GLA_CHUNK = 64
DIFF_HEADS = 4
DIFF_DV = 2 * HEAD_DIM
N_BRANCH = 3
BRANCH_W = 512
FFN_HIDDEN = ((8 * D_MODEL + 3 * 256 - 1) // (3 * 256)) * 256
ROPE_THETA = 10000.0
EPS = 1e-6

LANES = 128
BF16_ROWS = 16
VMEM_LIMIT = 56 * 1024 * 1024

F32 = jnp.float32
BF16 = jnp.bfloat16
HIGHEST = lax.Precision.HIGHEST

C_GQ = 0
C_DQ = C_GQ + GQA_HEADS * HEAD_DIM
C_GK = C_DQ + DIFF_HEADS * 2 * HEAD_DIM
C_DK = C_GK + GQA_KV_HEADS * HEAD_DIM
C_GV = C_DK + DIFF_HEADS * 2 * HEAD_DIM
C_DV = C_GV + GQA_KV_HEADS * HEAD_DIM
C_LQ = C_DV + DIFF_HEADS * DIFF_DV
C_LK = C_LQ + GLA_HEADS * GLA_DK
C_LV = C_LK + GLA_HEADS * GLA_DK
C_LGO = C_LV + GLA_HEADS * GLA_DV
C_LR = C_LGO + GLA_HEADS * GLA_DV
C_MG = C_LR + LANES
N_IN_PAD = C_MG + N_BRANCH * D_MODEL
N_Q = C_GK - C_GQ
N_K = C_GV - C_GK
N_V = C_LQ - C_GV
GLA_W = GLA_HEADS * GLA_DK
GQA_VT_ROWS = HEAD_DIM + BF16_ROWS
DIFF_VT_ROWS = DIFF_DV + BF16_ROWS


def _sigmoid(x):
    return 1.0 / (1.0 + jnp.exp(-x))


def _silu(x):
    return x * _sigmoid(x)


def _full_spec(shape):
    n = len(shape)
    return pl.BlockSpec(shape, lambda *_: (0,) * n, pipeline_mode=pl.Buffered(1))


def _mod_kernel(cond_ref, w_ref, b_ref, o_ref):
    s = _silu(cond_ref[...])
    o_ref[0] = jnp.dot(s, w_ref[0], precision=HIGHEST, preferred_element_type=F32) + b_ref[0]


def _modulation(cond_rows, w_mod, b_mod):
    tn = 1536
    n_out = 6 * D_MODEL
    return pl.pallas_call(
        _mod_kernel,
        grid=(DEPTH, n_out // tn),
        in_specs=[
            pl.BlockSpec((8, D_MODEL), lambda l, j: (0, 0)),
            pl.BlockSpec((1, D_MODEL, tn), lambda l, j: (l, 0, j)),
            pl.BlockSpec((1, 1, tn), lambda l, j: (l, 0, j)),
        ],
        out_specs=pl.BlockSpec((1, 8, tn), lambda l, j: (l, 0, j)),
        out_shape=jax.ShapeDtypeStruct((DEPTH, 8, n_out), F32),
        compiler_params=pltpu.CompilerParams(
            dimension_semantics=("parallel", "parallel"), vmem_limit_bytes=VMEM_LIMIT),
        name="modulation",
    )(cond_rows, w_mod, b_mod.reshape(DEPTH, 1, n_out))


def _head_norm(z, gain, gmat):
    outs = []
    n = z.shape[1]
    for c0 in range(0, n, 2 * LANES):
        w = min(2 * LANES, n - c0)
        zz = z[:, c0:c0 + w]
        ms = jnp.dot((zz * zz).astype(BF16), gmat[0:w, 0:w], preferred_element_type=F32)
        outs.append(zz * lax.rsqrt(ms + EPS) * gain[:, c0:c0 + w])
    return outs


def _rope_block(zb, cos, sin, first):
    partner = jnp.where(first, pltpu.roll(zb, LANES - 16, 1), pltpu.roll(zb, 16, 1))
    return zb * cos + partner * sin


def _in_kernel(*refs, rope, emit_cache):
    it = iter(refs)
    x_ref, mod_ref, g1_ref, w_ref, gq_ref, gk_ref, gmat_ref, up_ref, ub_ref = (next(it) for _ in range(9))
    if rope:
        cos_ref, sin_ref = next(it), next(it)
    (qg_ref, qd_ref, kdg_ref, kdd_ref, vtg_ref, vtd_ref,
     lq_ref, lk_ref, lv_ref, lgo_ref, la_ref, gate_ref) = (next(it) for _ in range(12))
    if emit_cache:
        ck_g_ref, ck_d_ref, cv_g_ref, cv_d_ref = (next(it) for _ in range(4))

    tm = x_ref.shape[0]
    x = x_ref[...]
    ms = jnp.mean(x * x, axis=-1, keepdims=True)
    h = x * lax.rsqrt(ms + EPS) * g1_ref[...]
    h = h * (1.0 + mod_ref[0, 1:2, :]) + mod_ref[0, 0:1, :]
    hb = h.astype(BF16)

    def proj(c0, n):
        return jnp.dot(hb, w_ref[:, c0:c0 + n], preferred_element_type=F32)

    lane = lax.broadcasted_iota(jnp.int32, (tm, LANES), 1)
    first16 = (lane % 32) < 16
    lo_half = lane < HEAD_DIM
    gmat = gmat_ref[...]
    if rope:
        cos, sin = cos_ref[...], sin_ref[...]

    def blocks128(parts):
        out = []
        for p in parts:
            for c in range(0, p.shape[1], LANES):
                out.append(p[:, c:c + LANES])
        return out

    qb = blocks128(_head_norm(proj(C_GQ, N_Q), gq_ref[...], gmat))
    for j, zb in enumerate(qb):
        if rope:
            zb = _rope_block(zb, cos, sin, first16)
        dst = qg_ref if j < 4 else qd_ref
        c = (j % 4) * LANES
        dst[:, c:c + LANES] = zb.astype(BF16)

    kb = blocks128(_head_norm(proj(C_GK, N_K), gk_ref[...], gmat))
    for j, zb in enumerate(kb):
        if rope:
            zb = _rope_block(zb, cos, sin, first16)
        if emit_cache:
            if j == 0:
                ck_g_ref[...] = zb
            else:
                ck_d_ref[:, (j - 1) * LANES:j * LANES] = zb
        sw = pltpu.roll(zb, HEAD_DIM, 1)
        dup_lo = jnp.where(lo_half, zb, sw).astype(BF16)
        dup_hi = jnp.where(lo_half, sw, zb).astype(BF16)
        if j == 0:
            kdg_ref[0, 0] = dup_lo
            kdg_ref[0, 1] = dup_hi
        else:
            kdd_ref[0, 2 * (j - 1)] = dup_lo
            kdd_ref[0, 2 * (j - 1) + 1] = dup_hi

    ones_rows = jnp.where(lax.broadcasted_iota(jnp.int32, (BF16_ROWS, tm), 0) == 0, 1.0, 0.0).astype(BF16)
    zv = proj(C_GV, N_V)
    if emit_cache:
        cv_g_ref[...] = zv[:, 0:LANES]
        cv_d_ref[...] = zv[:, LANES:]
    vt = zv[:, 0:LANES].T
    for hh in range(GQA_KV_HEADS):
        vtg_ref[0, hh, 0:HEAD_DIM, :] = vt[hh * HEAD_DIM:(hh + 1) * HEAD_DIM, :].astype(BF16)
        vtg_ref[0, hh, HEAD_DIM:GQA_VT_ROWS, :] = ones_rows
    for hh in range(DIFF_HEADS):
        vt = zv[:, (hh + 1) * LANES:(hh + 2) * LANES].T
        vtd_ref[0, hh, 0:DIFF_DV, :] = vt.astype(BF16)
        vtd_ref[0, hh, DIFF_DV:DIFF_VT_ROWS, :] = ones_rows

    zl = proj(C_LQ, C_MG - C_LQ)
    o = 0
    lq_ref[...] = (zl[:, o:o + GLA_W] * (GLA_DK ** -0.5)).astype(BF16)
    o += GLA_W
    lk_ref[...] = zl[:, o:o + GLA_W].astype(BF16)
    o += GLA_W
    lv_ref[...] = zl[:, o:o + BRANCH_W].astype(BF16)
    o += BRANCH_W
    lgo_ref[...] = zl[:, o:o + BRANCH_W].astype(BF16)
    o += BRANCH_W
    lr = zl[:, o:o + LANES]
    z = jnp.dot(lr, up_ref[...], precision=HIGHEST, preferred_element_type=F32) + ub_ref[...]
    log_sig = jnp.minimum(z, 0.0) - jnp.log(1.0 + jnp.exp(-jnp.abs(z)))
    la_ref[...] = log_sig * (1.0 / GLA_TAU)

    for b in range(N_BRANCH):
        gate_ref[:, b * D_MODEL:(b + 1) * D_MODEL] = _sigmoid(proj(C_MG + b * D_MODEL, D_MODEL)).astype(BF16)


def _in_projection(x2d, mod, lw, n_batch, n_tok, rope_tabs, emit_cache, tm):
    t_total = n_batch * n_tok
    tpb = n_tok // tm
    rope = rope_tabs is not None
    tile = lambda c: pl.BlockSpec((tm, c), lambda i: (i, 0))
    in_specs = [
        tile(D_MODEL),
        pl.BlockSpec((1, 6, D_MODEL), lambda i: (i // tpb if mod.shape[0] > 1 else 0, 0, 0)),
        _full_spec((1, D_MODEL)),
        _full_spec((D_MODEL, N_IN_PAD)),
        _full_spec((1, N_Q)),
        _full_spec((1, N_K)),
        _full_spec((2 * LANES, 2 * LANES)),
        _full_spec((LANES, 2 * GLA_W)),
        _full_spec((1, 2 * GLA_W)),
    ]
    args = [x2d, mod, lw["norm1"], lw["w_in"], lw["gq"], lw["gk"], lw["gmat"], lw["up"], lw["ub"]]
    if rope:
        in_specs += [pl.BlockSpec((tm, LANES), lambda i: (i % tpb, 0))] * 2
        args += list(rope_tabs)
    sd = jax.ShapeDtypeStruct
    out_shape = [
        sd((t_total, BRANCH_W), BF16), sd((t_total, BRANCH_W), BF16),
        sd((n_batch, GQA_KV_HEADS, n_tok, LANES), BF16), sd((n_batch, 2 * DIFF_HEADS, n_tok, LANES), BF16),
        sd((n_batch, GQA_KV_HEADS, GQA_VT_ROWS, n_tok), BF16), sd((n_batch, DIFF_HEADS, DIFF_VT_ROWS, n_tok), BF16),
        sd((t_total, GLA_W), BF16), sd((t_total, GLA_W), BF16),
        sd((t_total, BRANCH_W), BF16), sd((t_total, BRANCH_W), BF16),
        sd((t_total, 2 * GLA_W), F32), sd((t_total, N_BRANCH * D_MODEL), BF16),
    ]
    out_specs = [
        tile(BRANCH_W), tile(BRANCH_W),
        pl.BlockSpec((1, GQA_KV_HEADS, tm, LANES), lambda i: (i // tpb, 0, i % tpb, 0)),
        pl.BlockSpec((1, 2 * DIFF_HEADS, tm, LANES), lambda i: (i // tpb, 0, i % tpb, 0)),
        pl.BlockSpec((1, GQA_KV_HEADS, GQA_VT_ROWS, tm), lambda i: (i // tpb, 0, 0, i % tpb)),
        pl.BlockSpec((1, DIFF_HEADS, DIFF_VT_ROWS, tm), lambda i: (i // tpb, 0, 0, i % tpb)),
        tile(GLA_W), tile(GLA_W), tile(BRANCH_W), tile(BRANCH_W), tile(2 * GLA_W), tile(N_BRANCH * D_MODEL),
    ]
    if emit_cache:
        out_shape += [sd((t_total, LANES), F32), sd((t_total, BRANCH_W), F32),
                      sd((t_total, LANES), F32), sd((t_total, BRANCH_W), F32)]
        out_specs += [tile(LANES), tile(BRANCH_W), tile(LANES), tile(BRANCH_W)]
    return pl.pallas_call(
        functools.partial(_in_kernel, rope=rope, emit_cache=emit_cache),
        grid=(t_total // tm,),
        in_specs=in_specs,
        out_specs=out_specs,
        out_shape=out_shape,
        compiler_params=pltpu.CompilerParams(
            dimension_semantics=("parallel",), vmem_limit_bytes=VMEM_LIMIT),
        name="in_projection",
    )(*args)


def _masked_queries(q_blk, keep_low):
    lane = lax.broadcasted_iota(jnp.int32, q_blk.shape, 1)
    keep = (lane < HEAD_DIM) if keep_low else (lane >= HEAD_DIM)
    return jnp.where(keep, q_blk.astype(F32), 0.0).astype(BF16)


def _flash_step(k, vt, qm, m_ref, acc_ref):
    s = lax.dot_general(k, qm, (((1,), (1,)), ((), ())), preferred_element_type=F32)
    m_prev = m_ref[...]
    m_new = jnp.maximum(m_prev, jnp.max(s, axis=0, keepdims=True))
    alpha = jnp.exp(m_prev - m_new)
    p = jnp.exp(s - m_new).astype(BF16)
    acc_ref[...] = acc_ref[...] * alpha + jnp.dot(vt, p, preferred_element_type=F32)
    m_ref[...] = m_new


def _gqa_kernel(*refs, nk_main, has_cache, tq):
    if has_cache:
        q_ref, k_ref, vt_ref, kc_ref, vtc_ref, o_ref, qm_ref, m_ref, acc_ref, ot_ref = refs
    else:
        q_ref, k_ref, vt_ref, o_ref, qm_ref, m_ref, acc_ref, ot_ref = refs
    kt = pl.program_id(3)

    @pl.when(kt == 0)
    def _():
        for g in range(GQA_GROUP):
            blk = q_ref[0, :, (g // 2) * LANES:(g // 2 + 1) * LANES]
            qm_ref[g * tq:(g + 1) * tq, :] = _masked_queries(blk, g % 2 == 0)
        m_ref[...] = jnp.full(m_ref.shape, -jnp.inf, F32)
        acc_ref[...] = jnp.zeros(acc_ref.shape, F32)

    if has_cache:
        @pl.when(kt < nk_main)
        def _():
            _flash_step(k_ref[0, 0], vt_ref[0, 0], qm_ref[...], m_ref, acc_ref)

        @pl.when(kt >= nk_main)
        def _():
            _flash_step(kc_ref[0, 0], vtc_ref[0, 0], qm_ref[...], m_ref, acc_ref)
    else:
        _flash_step(k_ref[0, 0], vt_ref[0, 0], qm_ref[...], m_ref, acc_ref)

    @pl.when(kt == pl.num_programs(3) - 1)
    def _():
        acc = acc_ref[...]
        o = acc[0:HEAD_DIM, :] / acc[HEAD_DIM:HEAD_DIM + 1, :]
        for g in range(GQA_GROUP):
            ot_ref[g * HEAD_DIM:(g + 1) * HEAD_DIM, :] = o[:, g * tq:(g + 1) * tq]
        o_ref[0] = ot_ref[...].T.astype(BF16)


def _kv_specs(n_heads_blk, vt_rows, tk, nk_main, cache):
    if cache:
        k_map = lambda b, h, qi, kt: (b, h, jnp.maximum(kt - nk_main, 0), 0)
        v_map = lambda b, h, qi, kt: (b, h, 0, jnp.maximum(kt - nk_main, 0))
    else:
        k_map = lambda b, h, qi, kt: (b, h, jnp.minimum(kt, nk_main - 1), 0)
        v_map = lambda b, h, qi, kt: (b, h, 0, jnp.minimum(kt, nk_main - 1))
    return [pl.BlockSpec((1, n_heads_blk, tk, LANES), k_map), pl.BlockSpec((1, 1, vt_rows, tk), v_map)]


def _gqa_attention(q, kd, vt, kd_c, vt_c, n_batch, n_tok, tq, tk):
    has_cache = kd_c is not None
    nk_main = n_tok // tk
    nk = nk_main + (PAST_LEN // tk if has_cache else 0)
    rows = GQA_GROUP * tq
    in_specs = [pl.BlockSpec((1, tq, 2 * LANES), lambda b, h, qi, kt: (b, qi, h))]
    in_specs += _kv_specs(1, GQA_VT_ROWS, tk, nk_main, False)
    args = [q.reshape(n_batch, n_tok, BRANCH_W), kd, vt]
    if has_cache:
        in_specs += _kv_specs(1, GQA_VT_ROWS, tk, nk_main, True)
        args += [kd_c, vt_c]
    out = pl.pallas_call(
        functools.partial(_gqa_kernel, nk_main=nk_main, has_cache=has_cache, tq=tq),
        grid=(n_batch, GQA_KV_HEADS, n_tok // tq, nk),
        in_specs=in_specs,
        out_specs=pl.BlockSpec((1, tq, 2 * LANES), lambda b, h, qi, kt: (b, qi, h)),
        out_shape=jax.ShapeDtypeStruct((n_batch, n_tok, BRANCH_W), BF16),
        scratch_shapes=[
            pltpu.VMEM((rows, LANES), BF16),
            pltpu.VMEM((1, rows), F32),
            pltpu.VMEM((GQA_VT_ROWS, rows), F32),
            pltpu.VMEM((GQA_GROUP * HEAD_DIM, tq), F32),
        ],
        compiler_params=pltpu.CompilerParams(
            dimension_semantics=("parallel", "parallel", "parallel", "arbitrary"),
            vmem_limit_bytes=VMEM_LIMIT),
        name="gqa_attention",
    )(*args)
    return out.reshape(n_batch * n_tok, BRANCH_W)


def _diff_kernel(*refs, nk_main, has_cache, lam_init):
    if has_cache:
        (q_ref, lam_ref, gsub_ref, k_ref, vt_ref, kc_ref, vtc_ref, o_ref,
         qm_ref, m_ref, acc_ref) = refs
    else:
        q_ref, lam_ref, gsub_ref, k_ref, vt_ref, o_ref, qm_ref, m_ref, acc_ref = refs
    kt = pl.program_id(3)

    @pl.when(kt == 0)
    def _():
        blk = q_ref[0]
        for mm in range(2):
            qm_ref[mm] = _masked_queries(blk, mm == 0)
        m_ref[...] = jnp.full(m_ref.shape, -jnp.inf, F32)
        acc_ref[...] = jnp.zeros(acc_ref.shape, F32)

    def both_maps(kr, vr):
        vt_blk = vr[0, 0]
        for mm in range(2):
            _flash_step(kr[0, mm], vt_blk, qm_ref[mm], m_ref.at[mm], acc_ref.at[mm])

    if has_cache:
        @pl.when(kt < nk_main)
        def _():
            both_maps(k_ref, vt_ref)

        @pl.when(kt >= nk_main)
        def _():
            both_maps(kc_ref, vtc_ref)
    else:
        both_maps(k_ref, vt_ref)

    @pl.when(kt == pl.num_programs(3) - 1)
    def _():
        lp = lam_ref[...]
        lam = (jnp.exp(jnp.sum(lp[0:1] * lp[1:2], axis=-1, keepdims=True))
               - jnp.exp(jnp.sum(lp[2:3] * lp[3:4], axis=-1, keepdims=True)) + lam_init)
        a0, a1 = acc_ref[0], acc_ref[1]
        o0 = a0[0:DIFF_DV, :] / a0[DIFF_DV:DIFF_DV + 1, :]
        o1 = a1[0:DIFF_DV, :] / a1[DIFF_DV:DIFF_DV + 1, :]
        d = (o0 - lam * o1).T
        ms = jnp.mean(d * d, axis=-1, keepdims=True)
        o_ref[0] = (d * lax.rsqrt(ms + EPS) * gsub_ref[...] * (1.0 - lam_init)).astype(BF16)


def _diff_attention(q, lam_p, gsub, kd, vt, kd_c, vt_c, n_batch, n_tok, tq, tk, lam_init):
    has_cache = kd_c is not None
    nk_main = n_tok // tk
    nk = nk_main + (PAST_LEN // tk if has_cache else 0)
    in_specs = [
        pl.BlockSpec((1, tq, LANES), lambda b, h, qi, kt: (b, qi, h)),
        pl.BlockSpec((4, HEAD_DIM), lambda b, h, qi, kt: (0, 0)),
        pl.BlockSpec((1, DIFF_DV), lambda b, h, qi, kt: (0, 0)),
    ]
    in_specs += _kv_specs(2, DIFF_VT_ROWS, tk, nk_main, False)
    args = [q.reshape(n_batch, n_tok, BRANCH_W), lam_p, gsub, kd, vt]
    if has_cache:
        in_specs += _kv_specs(2, DIFF_VT_ROWS, tk, nk_main, True)
        args += [kd_c, vt_c]
    out = pl.pallas_call(
        functools.partial(_diff_kernel, nk_main=nk_main, has_cache=has_cache, lam_init=lam_init),
        grid=(n_batch, DIFF_HEADS, n_tok // tq, nk),
        in_specs=in_specs,
        out_specs=pl.BlockSpec((1, tq, LANES), lambda b, h, qi, kt: (b, qi, h)),
        out_shape=jax.ShapeDtypeStruct((n_batch, n_tok, BRANCH_W), BF16),
        scratch_shapes=[
            pltpu.VMEM((2, tq, LANES), BF16),
            pltpu.VMEM((2, 1, tq), F32),
            pltpu.VMEM((2, DIFF_VT_ROWS, tq), F32),
        ],
        compiler_params=pltpu.CompilerParams(
            dimension_semantics=("parallel", "parallel", "parallel", "arbitrary"),
            vmem_limit_bytes=VMEM_LIMIT),
        name="diff_attention",
    )(*args)
    return out.reshape(n_batch * n_tok, BRANCH_W)


def _gla_kernel(*refs, has_s0, n_chunk):
    if has_s0:
        qf, kf, vf, laf, qb, kb, vb, lab, s0_ref, of_ref, ob_ref, sfin_ref, st_ref = refs
    else:
        qf, kf, vf, laf, qb, kb, vb, lab, of_ref, ob_ref, sfin_ref, st_ref = refs
    i = pl.program_id(1)
    ck = GLA_CHUNK

    @pl.when(i == 0)
    def _():
        for d in range(2):
            for hd in range(GLA_HEADS):
                if has_s0:
                    s = s0_ref[0, d, hd]
                    z = jnp.zeros_like(s)
                    padded = jnp.concatenate([s, z] if hd % 2 == 0 else [z, s], axis=0)
                    st_ref[d, hd] = padded.T
                else:
                    st_ref[d, hd] = jnp.zeros((GLA_DV, LANES), F32)

    r = lax.broadcasted_iota(jnp.int32, (ck, ck), 0)
    c = lax.broadcasted_iota(jnp.int32, (ck, ck), 1)
    lane = lax.broadcasted_iota(jnp.int32, (ck, LANES), 1)
    streams = ((qf, kf, vf, laf, of_ref, c <= r), (qb, kb, vb, lab, ob_ref, c >= r))
    for d, (q_r, k_r, v_r, la_r, o_r, tri) in enumerate(streams):
        tri_f = tri.astype(F32)
        for cc in range(n_chunk):
            ci = cc if d == 0 else n_chunk - 1 - cc
            rows = slice(ci * ck, (ci + 1) * ck)
            la = la_r[rows, :]
            cum = jnp.dot(tri_f, la, precision=HIGHEST, preferred_element_type=F32)
            tot = cum[ck - 1:ck, :] if d == 0 else cum[0:1, :]
            q = q_r[rows, :].astype(F32)
            k = k_r[rows, :].astype(F32)
            v = v_r[rows, :]
            q_dec = q * jnp.exp(cum)
            k_inv = k * jnp.exp(-cum)
            k_end = k * jnp.exp(tot - cum)
            decay = jnp.exp(tot)
            for hd in range(GLA_HEADS):
                pair = slice((hd // 2) * LANES, (hd // 2 + 1) * LANES)
                keep = (lane < GLA_DK) if hd % 2 == 0 else (lane >= GLA_DK)
                qd_m = jnp.where(keep, q_dec[:, pair], 0.0).astype(BF16)
                ke_m = jnp.where(keep, k_end[:, pair], 0.0).astype(BF16)
                ki_p = k_inv[:, pair].astype(BF16)
                a = lax.dot_general(qd_m, ki_p, (((1,), (1,)), ((), ())), preferred_element_type=F32)
                a = jnp.where(tri, a, 0.0).astype(BF16)
                hv = slice(hd * GLA_DV, (hd + 1) * GLA_DV)
                s_t = st_ref[d, hd]
                o = jnp.dot(a, v[:, hv], preferred_element_type=F32)
                o = o + lax.dot_general(qd_m, s_t.astype(BF16), (((1,), (1,)), ((), ())),
                                        preferred_element_type=F32)
                o_r[rows, hv] = o
                ds_t = lax.dot_general(v[:, hv], ke_m, (((0,), (0,)), ((), ())), preferred_element_type=F32)
                st_ref[d, hd] = s_t * decay[:, pair] + ds_t

    @pl.when(i == pl.num_programs(1) - 1)
    def _():
        for d in range(2):
            for hd in range(GLA_HEADS):
                t = st_ref[d, hd].T
                sfin_ref[0, d, hd] = t[(hd % 2) * GLA_DK:(hd % 2 + 1) * GLA_DK, :]


def _gla(lq, lk, lv, la, s0, n_batch, n_tok, tb):
    nb = n_tok // tb
    t_total = n_batch * n_tok
    has_s0 = s0 is not None
    fwd = lambda b, i: (b * nb + i, 0)
    bwd = lambda b, i: (b * nb + nb - 1 - i, 0)
    bwd_la = lambda b, i: (b * nb + nb - 1 - i, 1)
    in_specs = [
        pl.BlockSpec((tb, GLA_W), fwd), pl.BlockSpec((tb, GLA_W), fwd),
        pl.BlockSpec((tb, BRANCH_W), fwd), pl.BlockSpec((tb, GLA_W), fwd),
        pl.BlockSpec((tb, GLA_W), bwd), pl.BlockSpec((tb, GLA_W), bwd),
        pl.BlockSpec((tb, BRANCH_W), bwd), pl.BlockSpec((tb, GLA_W), bwd_la),
    ]
    args = [lq, lk, lv, la, lq, lk, lv, la]
    state_spec = pl.BlockSpec((1, 2, GLA_HEADS, GLA_DK, GLA_DV), lambda b, i: (b, 0, 0, 0, 0))
    if has_s0:
        in_specs.append(state_spec)
        args.append(s0)
    return pl.pallas_call(
        functools.partial(_gla_kernel, has_s0=has_s0, n_chunk=tb // GLA_CHUNK),
        grid=(n_batch, nb),
        in_specs=in_specs,
        out_specs=[pl.BlockSpec((tb, BRANCH_W), fwd), pl.BlockSpec((tb, BRANCH_W), bwd), state_spec],
        out_shape=[
            jax.ShapeDtypeStruct((t_total, BRANCH_W), F32),
            jax.ShapeDtypeStruct((t_total, BRANCH_W), F32),
            jax.ShapeDtypeStruct((n_batch, 2, GLA_HEADS, GLA_DK, GLA_DV), F32),
        ],
        scratch_shapes=[pltpu.VMEM((2, GLA_HEADS, GLA_DV, LANES), F32)],
        compiler_params=pltpu.CompilerParams(
            dimension_semantics=("parallel", "arbitrary"), vmem_limit_bytes=VMEM_LIMIT),
        name="gla",
    )(*args)


def _merge_kernel(x_ref, mod_ref, og_ref, of_ref, ob_ref, lgo_ref, od_ref, gate_ref,
                  gout_ref, wb_ref, wo_ref, y_ref):
    o_gla = of_ref[...] + ob_ref[...]
    gla_parts = []
    for hd in range(GLA_HEADS):
        blk = o_gla[:, hd * GLA_DV:(hd + 1) * GLA_DV]
        ms = jnp.mean(blk * blk, axis=-1, keepdims=True)
        gla_parts.append(blk * lax.rsqrt(ms + EPS) * gout_ref[...])
    gla = jnp.concatenate(gla_parts, axis=1) * _silu(lgo_ref[...].astype(F32))
    branches = (og_ref[...], gla.astype(BF16), od_ref[...])
    mixed = None
    for b, ob in enumerate(branches):
        y = jnp.dot(ob, wb_ref[b], preferred_element_type=F32)
        y = y * gate_ref[:, b * D_MODEL:(b + 1) * D_MODEL].astype(F32)
        mixed = y if mixed is None else mixed + y
    out = jnp.dot(mixed.astype(BF16), wo_ref[...], preferred_element_type=F32)
    y_ref[...] = x_ref[...] + mod_ref[0, 2:3, :] * out


def _merge(x2d, mod, og, o_f, o_b, lgo, od, gates, lw, tiles_per_mod, tm):
    t_total = x2d.shape[0]
    tile = lambda c: pl.BlockSpec((tm, c), lambda i: (i, 0))
    return pl.pallas_call(
        _merge_kernel,
        grid=(t_total // tm,),
        in_specs=[
            tile(D_MODEL),
            pl.BlockSpec((1, 6, D_MODEL), lambda i: (i // tiles_per_mod, 0, 0)),
            tile(BRANCH_W), tile(BRANCH_W), tile(BRANCH_W), tile(BRANCH_W), tile(BRANCH_W),
            tile(N_BRANCH * D_MODEL),
            _full_spec((1, GLA_DV)),
            _full_spec((N_BRANCH, BRANCH_W, D_MODEL)),
            _full_spec((D_MODEL, D_MODEL)),
        ],
        out_specs=tile(D_MODEL),
        out_shape=jax.ShapeDtypeStruct((t_total, D_MODEL), F32),
        compiler_params=pltpu.CompilerParams(
            dimension_semantics=("parallel",), vmem_limit_bytes=VMEM_LIMIT),
        name="merge",
    )(x2d, mod, og, o_f, o_b, lgo, od, gates, lw["gout"], lw["w_branch"], lw["w_out"])


FFN_CHUNK = 256


def _ffn_kernel(x_ref, mod_ref, g2_ref, wi_ref, wo_ref, y_ref):
    x = x_ref[...]
    ms = jnp.mean(x * x, axis=-1, keepdims=True)
    h = x * lax.rsqrt(ms + EPS) * g2_ref[...]
    h = h * (1.0 + mod_ref[0, 4:5, :]) + mod_ref[0, 3:4, :]
    hb = h.astype(BF16)
    acc = None
    for c0 in range(0, FFN_HIDDEN, FFN_CHUNK):
        a = jnp.dot(hb, wi_ref[:, c0:c0 + FFN_CHUNK], preferred_element_type=F32)
        u = jnp.dot(hb, wi_ref[:, FFN_HIDDEN + c0:FFN_HIDDEN + c0 + FFN_CHUNK], preferred_element_type=F32)
        act = (_silu(a) * u).astype(BF16)
        part = jnp.dot(act, wo_ref[c0:c0 + FFN_CHUNK, :], preferred_element_type=F32)
        acc = part if acc is None else acc + part
    y_ref[...] = x + mod_ref[0, 5:6, :] * acc


def _ffn(x2d, mod, lw, tiles_per_mod, tm):
    t_total = x2d.shape[0]
    tile = lambda c: pl.BlockSpec((tm, c), lambda i: (i, 0))
    return pl.pallas_call(
        _ffn_kernel,
        grid=(t_total // tm,),
        in_specs=[
            tile(D_MODEL),
            pl.BlockSpec((1, 6, D_MODEL), lambda i: (i // tiles_per_mod, 0, 0)),
            _full_spec((1, D_MODEL)),
            _full_spec((D_MODEL, 2 * FFN_HIDDEN)),
            _full_spec((FFN_HIDDEN, D_MODEL)),
        ],
        out_specs=tile(D_MODEL),
        out_shape=jax.ShapeDtypeStruct((t_total, D_MODEL), F32),
        compiler_params=pltpu.CompilerParams(
            dimension_semantics=("parallel",), vmem_limit_bytes=VMEM_LIMIT),
        name="ffn",
    )(x2d, mod, lw["norm2"], lw["w_ffn_in"], lw["w_ffn_out"])


def _layer_weights(l, p):
    w = p["w_in"][l]
    s = (GQA_HEADS * HEAD_DIM, GQA_KV_HEADS * HEAD_DIM, GQA_KV_HEADS * HEAD_DIM,
         GLA_HEADS * GLA_DK, GLA_HEADS * GLA_DK, GLA_HEADS * GLA_DV, 2 * GLA_RANK, GLA_HEADS * GLA_DV,
         DIFF_HEADS * 2 * HEAD_DIM, DIFF_HEADS * 2 * HEAD_DIM, DIFF_HEADS * DIFF_DV, N_BRANCH * D_MODEL)
    offs = [0]
    for n in s:
        offs.append(offs[-1] + n)
    col = lambda j: w[:, offs[j]:offs[j + 1]]
    gq, gk, gv, lq, lk, lv, lr, lgo, dq, dk, dv, mg = (col(j) for j in range(12))
    lr_pad = jnp.zeros((D_MODEL, LANES - 2 * GLA_RANK), w.dtype)
    w_in = jnp.concatenate([gq, dq, gk, dk, gv, dv, lq, lk, lv, lgo, lr, lr_pad, mg], axis=1).astype(BF16)
    scale = HEAD_DIM ** -0.5
    gq_row = jnp.concatenate([jnp.tile(p["gqa_q_norm"][l], GQA_HEADS),
                              jnp.tile(p["diff_q_norm"][l], 2 * DIFF_HEADS)]) * scale
    gk_row = jnp.concatenate([jnp.tile(p["gqa_k_norm"][l], GQA_KV_HEADS),
                              jnp.tile(p["diff_k_norm"][l], 2 * DIFF_HEADS)])
    idx = jnp.arange(2 * LANES) // HEAD_DIM
    gmat = jnp.where(idx[:, None] == idx[None, :], 1.0 / HEAD_DIM, 0.0).astype(BF16)
    up = jnp.zeros((LANES, 2 * GLA_W), F32)
    up = up.at[0:GLA_RANK, 0:GLA_W].set(p["gla_alpha_up"][l, 0])
    up = up.at[GLA_RANK:2 * GLA_RANK, GLA_W:].set(p["gla_alpha_up"][l, 1])
    ub = p["gla_alpha_bias"][l].reshape(1, 2 * GLA_W)
    return {
        "w_in": w_in, "gq": gq_row.reshape(1, N_Q), "gk": gk_row.reshape(1, N_K), "gmat": gmat,
        "up": up, "ub": ub,
        "norm1": p["norm1"][l].reshape(1, D_MODEL), "norm2": p["norm2"][l].reshape(1, D_MODEL),
        "gout": p["gla_out_norm"][l].reshape(1, GLA_DV),
        "gsub": p["diff_sub_norm"][l].reshape(1, DIFF_DV),
        "lam": p["diff_lambda"][l],
        "w_branch": p["w_branch"][l].astype(BF16), "w_out": p["w_out"][l].astype(BF16),
        "w_ffn_in": p["w_ffn_in"][l].astype(BF16), "w_ffn_out": p["w_ffn_out"][l].astype(BF16),
    }


def _rope_tables(n_tokens):
    n_rows = n_tokens // GRID_W
    row = jnp.repeat(jnp.arange(n_rows, dtype=F32), GRID_W)
    col = jnp.tile(jnp.arange(GRID_W, dtype=F32), n_rows)
    n_freq = HEAD_DIM // 4
    freqs = ROPE_THETA ** (-jnp.arange(n_freq, dtype=F32) / n_freq)
    ar, ac = row[:, None] * freqs, col[:, None] * freqs
    cos = jnp.concatenate([jnp.cos(ar), jnp.cos(ar), jnp.cos(ac), jnp.cos(ac)], axis=1)
    sin = jnp.concatenate([-jnp.sin(ar), jnp.sin(ar), -jnp.sin(ac), jnp.sin(ac)], axis=1)
    return jnp.tile(cos, (1, LANES // HEAD_DIM)), jnp.tile(sin, (1, LANES // HEAD_DIM))


def _with_ones_rows(v_t):
    lead = v_t.shape[:-2]
    s = v_t.shape[-1]
    ones = jnp.ones(lead + (1, s), v_t.dtype)
    zeros = jnp.zeros(lead + (BF16_ROWS - 1, s), v_t.dtype)
    return jnp.concatenate([v_t, ones, zeros], axis=-2).astype(BF16)


def _cache_layouts(l, cache_gqa_k, cache_gqa_v, cache_diff_k, cache_diff_v):
    b = cache_gqa_k.shape[0]
    gk = jnp.transpose(cache_gqa_k[:, l], (0, 2, 1, 3))
    kd_g = jnp.concatenate([gk, gk], axis=-1).astype(BF16)
    vt_g = _with_ones_rows(jnp.transpose(cache_gqa_v[:, l], (0, 2, 3, 1)))
    dk = jnp.transpose(cache_diff_k[:, l], (0, 2, 3, 1, 4)).reshape(b, 2 * DIFF_HEADS, PAST_LEN, HEAD_DIM)
    kd_d = jnp.concatenate([dk, dk], axis=-1).astype(BF16)
    vt_d = _with_ones_rows(jnp.transpose(cache_diff_v[:, l], (0, 2, 3, 1)))
    return kd_g, vt_g, kd_d, vt_d


def _run_layer(x2d, mod, lw, l, n_batch, n_tok, rope_tabs, ctx, emit_cache, cfg):
    tiles_per_mod = lambda tm: (n_tok // tm) if mod.shape[0] > 1 else (n_batch * n_tok // tm)
    outs = _in_projection(x2d, mod, lw, n_batch, n_tok, rope_tabs, emit_cache, cfg["tm_in"])
    qg, qd, kdg, kdd, vtg, vtd, lq, lk, lv, lgo, la, gates = outs[:12]
    if ctx is None:
        kd_gc = vt_gc = kd_dc = vt_dc = s0 = None
    else:
        kd_gc, vt_gc, kd_dc, vt_dc, s0 = ctx
    og = _gqa_attention(qg, kdg, vtg, kd_gc, vt_gc, n_batch, n_tok, cfg["tq_gqa"], cfg["tk"])
    lam_init = 0.8 - 0.6 * math.exp(-0.3 * l)
    od = _diff_attention(qd, lw["lam"], lw["gsub"], kdd, vtd, kd_dc, vt_dc, n_batch, n_tok,
                         cfg["tq_diff"], cfg["tk"], lam_init)
    o_f, o_b, s_fin = _gla(lq, lk, lv, la, s0, n_batch, n_tok, cfg["tb_gla"])
    x2d = _merge(x2d, mod, og, o_f, o_b, lgo, od, gates, lw, tiles_per_mod(cfg["tm"]), cfg["tm"])
    x2d = _ffn(x2d, mod, lw, tiles_per_mod(cfg["tm"]), cfg["tm"])
    cache = tuple(outs[12:]) + (s_fin,) if emit_cache else None
    return x2d, cache


PROMPT_CFG = dict(tm_in=256, tq_gqa=256, tq_diff=256, tk=256, tb_gla=256, tm=512)
SAMPLE_CFG = dict(tm_in=256, tq_gqa=512, tq_diff=1024, tk=512, tb_gla=256, tm=512)


def kernel(x_prompt, x_sample, c, cache_gqa_k, cache_gqa_v, state_gla, cache_diff_k, cache_diff_v, c_ctx, w_mod, b_mod, norm1, norm2, w_in, gqa_q_norm, gqa_k_norm, gla_alpha_up, gla_alpha_bias, gla_out_norm, diff_q_norm, diff_k_norm, diff_lambda, diff_sub_norm, w_branch, w_out, w_ffn_in, w_ffn_out):
    p = {
        "norm1": norm1, "norm2": norm2, "w_in": w_in, "gqa_q_norm": gqa_q_norm, "gqa_k_norm": gqa_k_norm,
        "gla_alpha_up": gla_alpha_up, "gla_alpha_bias": gla_alpha_bias, "gla_out_norm": gla_out_norm,
        "diff_q_norm": diff_q_norm, "diff_k_norm": diff_k_norm, "diff_lambda": diff_lambda,
        "diff_sub_norm": diff_sub_norm, "w_branch": w_branch, "w_out": w_out,
        "w_ffn_in": w_ffn_in, "w_ffn_out": w_ffn_out,
    }
    n_ctx_b, n_ctx = x_prompt.shape[:2]
    n_lat_b, n_lat = x_sample.shape[:2]
    cond_rows = jnp.concatenate(
        [c_ctx[None, :], c, jnp.zeros((8 - 1 - n_lat_b, D_MODEL), F32)], axis=0)
    mod_all = _modulation(cond_rows, w_mod, b_mod)
    weights = [_layer_weights(l, p) for l in range(DEPTH)]

    y = x_prompt.reshape(n_ctx_b * n_ctx, D_MODEL)
    caches = []
    for l in range(DEPTH):
        mod = mod_all[l, 0:1].reshape(1, 6, D_MODEL)
        y, cache = _run_layer(y, mod, weights[l], l, n_ctx_b, n_ctx, None, None, True, PROMPT_CFG)
        caches.append(cache)
    y_prompt = y.reshape(n_ctx_b, n_ctx, D_MODEL)
    stack = lambda j, shape: jnp.stack([cc[j].reshape(shape) for cc in caches], axis=1)
    new_gqa_k = stack(0, (n_ctx_b, n_ctx, GQA_KV_HEADS, HEAD_DIM))
    new_diff_k = stack(1, (n_ctx_b, n_ctx, DIFF_HEADS, 2, HEAD_DIM))
    new_gqa_v = stack(2, (n_ctx_b, n_ctx, GQA_KV_HEADS, HEAD_DIM))
    new_diff_v = stack(3, (n_ctx_b, n_ctx, DIFF_HEADS, DIFF_DV))
    new_state_gla = jnp.stack([cc[4] for cc in caches], axis=1)

    rope_tabs = _rope_tables(n_lat)
    y = x_sample.reshape(n_lat_b * n_lat, D_MODEL)
    for l in range(DEPTH):
        mod = mod_all[l, 1:1 + n_lat_b].reshape(n_lat_b, 6, D_MODEL)
        ctx = _cache_layouts(l, cache_gqa_k, cache_gqa_v, cache_diff_k, cache_diff_v) + (state_gla[:, l],)
        y, _ = _run_layer(y, mod, weights[l], l, n_lat_b, n_lat, rope_tabs, ctx, False, SAMPLE_CFG)
    y_sample = y.reshape(n_lat_b, n_lat, D_MODEL)
    return (y_prompt, y_sample, new_gqa_k, new_gqa_v, new_state_gla, new_diff_k, new_diff_v)
```

```python
import functools
import math

import jax
import jax.numpy as jnp
from jax import lax
from jax.experimental import pallas as pl
from jax.experimental.pallas import tpu as pltpu

D_MODEL = 1024
DEPTH = 2
PAST_LEN = 512
GRID_W = 64
HEAD_DIM = 64
GQA_HEADS = 8
GQA_KV_HEADS = 2
GQA_GROUP = GQA_HEADS // GQA_KV_HEADS
GLA_HEADS = 4
GLA_DK = 64
GLA_DV = 128
GLA_RANK = 16
GLA_TAU = 16.0
GLA_CHUNK = 64
DIFF_HEADS = 4
DIFF_DV = 2 * HEAD_DIM
N_BRANCH = 3
BRANCH_W = 512
FFN_HIDDEN = ((8 * D_MODEL + 3 * 256 - 1) // (3 * 256)) * 256
ROPE_THETA = 10000.0
EPS = 1e-6

LANES = 128
BF16_ROWS = 16
VMEM_LIMIT = 56 * 1024 * 1024

F32 = jnp.float32
BF16 = jnp.bfloat16
HIGHEST = lax.Precision.HIGHEST

C_GQ = 0
C_DQ = C_GQ + GQA_HEADS * HEAD_DIM
C_GK = C_DQ + DIFF_HEADS * 2 * HEAD_DIM
C_DK = C_GK + GQA_KV_HEADS * HEAD_DIM
C_GV = C_DK + DIFF_HEADS * 2 * HEAD_DIM
C_DV = C_GV + GQA_KV_HEADS * HEAD_DIM
C_LQ = C_DV + DIFF_HEADS * DIFF_DV
C_LK = C_LQ + GLA_HEADS * GLA_DK
C_LV = C_LK + GLA_HEADS * GLA_DK
C_LGO = C_LV + GLA_HEADS * GLA_DV
C_LR = C_LGO + GLA_HEADS * GLA_DV
C_MG = C_LR + LANES
N_IN_PAD = C_MG + N_BRANCH * D_MODEL
N_Q = C_GK - C_GQ
N_K = C_GV - C_GK
N_V = C_LQ - C_GV
GLA_W = GLA_HEADS * GLA_DK
GQA_VT_ROWS = HEAD_DIM + BF16_ROWS
DIFF_VT_ROWS = DIFF_DV + BF16_ROWS


def _sigmoid(x):
    return 1.0 / (1.0 + jnp.exp(-x))


def _silu(x):
    return x * _sigmoid(x)


def _full_spec(shape):
    n = len(shape)
    return pl.BlockSpec(shape, lambda *_: (0,) * n, pipeline_mode=pl.Buffered(1))


def _mod_kernel(cond_ref, w_ref, b_ref, o_ref):
    s = _silu(cond_ref[...])
    o_ref[0] = jnp.dot(s, w_ref[0], precision=HIGHEST, preferred_element_type=F32) + b_ref[0]


def _modulation(cond_rows, w_mod, b_mod):
    tn = 1536
    n_out = 6 * D_MODEL
    return pl.pallas_call(
        _mod_kernel,
        grid=(DEPTH, n_out // tn),
        in_specs=[
            pl.BlockSpec((8, D_MODEL), lambda l, j: (0, 0)),
            pl.BlockSpec((1, D_MODEL, tn), lambda l, j: (l, 0, j)),
            pl.BlockSpec((1, 1, tn), lambda l, j: (l, 0, j)),
        ],
        out_specs=pl.BlockSpec((1, 8, tn), lambda l, j: (l, 0, j)),
        out_shape=jax.ShapeDtypeStruct((DEPTH, 8, n_out), F32),
        compiler_params=pltpu.CompilerParams(
            dimension_semantics=("parallel", "parallel"), vmem_limit_bytes=VMEM_LIMIT),
        name="modulation",
    )(cond_rows, w_mod, b_mod.reshape(DEPTH, 1, n_out))


def _head_norm(z, gain, gmat):
    outs = []
    n = z.shape[1]
    for c0 in range(0, n, 2 * LANES):
        w = min(2 * LANES, n - c0)
        zz = z[:, c0:c0 + w]
        ms = jnp.dot((zz * zz).astype(BF16), gmat[0:w, 0:w], preferred_element_type=F32)
        outs.append(zz * lax.rsqrt(ms + EPS) * gain[:, c0:c0 + w])
    return outs


def _rope_block(zb, cos, sin, first):
    partner = jnp.where(first, pltpu.roll(zb, LANES - 16, 1), pltpu.roll(zb, 16, 1))
    return zb * cos + partner * sin


def _in_kernel(*refs, rope, emit_cache):
    it = iter(refs)
    x_ref, mod_ref, g1_ref, w_ref, gq_ref, gk_ref, gmat_ref, up_ref, ub_ref = (next(it) for _ in range(9))
    if rope:
        cos_ref, sin_ref = next(it), next(it)
    (qg_ref, qd_ref, kdg_ref, kdd_ref, vtg_ref, vtd_ref,
     lq_ref, lk_ref, lv_ref, lgo_ref, la_ref, gate_ref) = (next(it) for _ in range(12))
    if emit_cache:
        ck_g_ref, ck_d_ref, cv_g_ref, cv_d_ref = (next(it) for _ in range(4))

    tm = x_ref.shape[0]
    x = x_ref[...]
    ms = jnp.mean(x * x, axis=-1, keepdims=True)
    h = x * lax.rsqrt(ms + EPS) * g1_ref[...]
    h = h * (1.0 + mod_ref[0, 1:2, :]) + mod_ref[0, 0:1, :]
    hb = h.astype(BF16)

    def proj(c0, n):
        return jnp.dot(hb, w_ref[:, c0:c0 + n], preferred_element_type=F32)

    lane = lax.broadcasted_iota(jnp.int32, (tm, LANES), 1)
    first16 = (lane % 32) < 16
    lo_half = lane < HEAD_DIM
    gmat = gmat_ref[...]
    if rope:
        cos, sin = cos_ref[...], sin_ref[...]

    def blocks128(parts):
        out = []
        for p in parts:
            for c in range(0, p.shape[1], LANES):
                out.append(p[:, c:c + LANES])
        return out

    qb = blocks128(_head_norm(proj(C_GQ, N_Q), gq_ref[...], gmat))
    for j, zb in enumerate(qb):
        if rope:
            zb = _rope_block(zb, cos, sin, first16)
        dst = qg_ref if j < 4 else qd_ref
        c = (j % 4) * LANES
        dst[:, c:c + LANES] = zb.astype(BF16)

    kb = blocks128(_head_norm(proj(C_GK, N_K), gk_ref[...], gmat))
    for j, zb in enumerate(kb):
        if rope:
            zb = _rope_block(zb, cos, sin, first16)
        if emit_cache:
            if j == 0:
                ck_g_ref[...] = zb
            else:
                ck_d_ref[:, (j - 1) * LANES:j * LANES] = zb
        sw = pltpu.roll(zb, HEAD_DIM, 1)
        dup_lo = jnp.where(lo_half, zb, sw).astype(BF16)
        dup_hi = jnp.where(lo_half, sw, zb).astype(BF16)
        if j == 0:
            kdg_ref[0, 0] = dup_lo
            kdg_ref[0, 1] = dup_hi
        else:
            kdd_ref[0, 2 * (j - 1)] = dup_lo
            kdd_ref[0, 2 * (j - 1) + 1] = dup_hi

    ones_rows = jnp.where(lax.broadcasted_iota(jnp.int32, (BF16_ROWS, tm), 0) == 0, 1.0, 0.0).astype(BF16)
    zv = proj(C_GV, N_V)
    if emit_cache:
        cv_g_ref[...] = zv[:, 0:LANES]
        cv_d_ref[...] = zv[:, LANES:]
    vt = zv[:, 0:LANES].T
    for hh in range(GQA_KV_HEADS):
        vtg_ref[0, hh, 0:HEAD_DIM, :] = vt[hh * HEAD_DIM:(hh + 1) * HEAD_DIM, :].astype(BF16)
        vtg_ref[0, hh, HEAD_DIM:GQA_VT_ROWS, :] = ones_rows
    for hh in range(DIFF_HEADS):
        vt = zv[:, (hh + 1) * LANES:(hh + 2) * LANES].T
        vtd_ref[0, hh, 0:DIFF_DV, :] = vt.astype(BF16)
        vtd_ref[0, hh, DIFF_DV:DIFF_VT_ROWS, :] = ones_rows

    zl = proj(C_LQ, C_MG - C_LQ)
    o = 0
    lq_ref[...] = (zl[:, o:o + GLA_W] * (GLA_DK ** -0.5)).astype(BF16)
    o += GLA_W
    lk_ref[...] = zl[:, o:o + GLA_W].astype(BF16)
    o += GLA_W
    lv_ref[...] = zl[:, o:o + BRANCH_W].astype(BF16)
    o += BRANCH_W
    lgo_ref[...] = zl[:, o:o + BRANCH_W].astype(BF16)
    o += BRANCH_W
    lr = zl[:, o:o + LANES]
    z = jnp.dot(lr, up_ref[...], precision=HIGHEST, preferred_element_type=F32) + ub_ref[...]
    log_sig = jnp.minimum(z, 0.0) - jnp.log(1.0 + jnp.exp(-jnp.abs(z)))
    la_ref[...] = log_sig * (1.0 / GLA_TAU)

    for b in range(N_BRANCH):
        gate_ref[:, b * D_MODEL:(b + 1) * D_MODEL] = _sigmoid(proj(C_MG + b * D_MODEL, D_MODEL)).astype(BF16)


def _in_projection(x2d, mod, lw, n_batch, n_tok, rope_tabs, emit_cache, tm):
    t_total = n_batch * n_tok
    tpb = n_tok // tm
    rope = rope_tabs is not None
    tile = lambda c: pl.BlockSpec((tm, c), lambda i: (i, 0))
    in_specs = [
        tile(D_MODEL),
        pl.BlockSpec((1, 6, D_MODEL), lambda i: (i // tpb if mod.shape[0] > 1 else 0, 0, 0)),
        _full_spec((1, D_MODEL)),
        _full_spec((D_MODEL, N_IN_PAD)),
        _full_spec((1, N_Q)),
        _full_spec((1, N_K)),
        _full_spec((2 * LANES, 2 * LANES)),
        _full_spec((LANES, 2 * GLA_W)),
        _full_spec((1, 2 * GLA_W)),
    ]
    args = [x2d, mod, lw["norm1"], lw["w_in"], lw["gq"], lw["gk"], lw["gmat"], lw["up"], lw["ub"]]
    if rope:
        in_specs += [pl.BlockSpec((tm, LANES), lambda i: (i % tpb, 0))] * 2
        args += list(rope_tabs)
    sd = jax.ShapeDtypeStruct
    out_shape = [
        sd((t_total, BRANCH_W), BF16), sd((t_total, BRANCH_W), BF16),
        sd((n_batch, GQA_KV_HEADS, n_tok, LANES), BF16), sd((n_batch, 2 * DIFF_HEADS, n_tok, LANES), BF16),
        sd((n_batch, GQA_KV_HEADS, GQA_VT_ROWS, n_tok), BF16), sd((n_batch, DIFF_HEADS, DIFF_VT_ROWS, n_tok), BF16),
        sd((t_total, GLA_W), BF16), sd((t_total, GLA_W), BF16),
        sd((t_total, BRANCH_W), BF16), sd((t_total, BRANCH_W), BF16),
        sd((t_total, 2 * GLA_W), F32), sd((t_total, N_BRANCH * D_MODEL), BF16),
    ]
    out_specs = [
        tile(BRANCH_W), tile(BRANCH_W),
        pl.BlockSpec((1, GQA_KV_HEADS, tm, LANES), lambda i: (i // tpb, 0, i % tpb, 0)),
        pl.BlockSpec((1, 2 * DIFF_HEADS, tm, LANES), lambda i: (i // tpb, 0, i % tpb, 0)),
        pl.BlockSpec((1, GQA_KV_HEADS, GQA_VT_ROWS, tm), lambda i: (i // tpb, 0, 0, i % tpb)),
        pl.BlockSpec((1, DIFF_HEADS, DIFF_VT_ROWS, tm), lambda i: (i // tpb, 0, 0, i % tpb)),
        tile(GLA_W), tile(GLA_W), tile(BRANCH_W), tile(BRANCH_W), tile(2 * GLA_W), tile(N_BRANCH * D_MODEL),
    ]
    if emit_cache:
        out_shape += [sd((t_total, LANES), F32), sd((t_total, BRANCH_W), F32),
                      sd((t_total, LANES), F32), sd((t_total, BRANCH_W), F32)]
        out_specs += [tile(LANES), tile(BRANCH_W), tile(LANES), tile(BRANCH_W)]
    return pl.pallas_call(
        functools.partial(_in_kernel, rope=rope, emit_cache=emit_cache),
        grid=(t_total // tm,),
        in_specs=in_specs,
        out_specs=out_specs,
        out_shape=out_shape,
        compiler_params=pltpu.CompilerParams(
            dimension_semantics=("parallel",), vmem_limit_bytes=VMEM_LIMIT),
        name="in_projection",
    )(*args)


MAX_COL = 4 * LANES
NT_DIMS = (((1,), (1,)), ((), ()))
SCORE_BOUND = 45.0
K_NORM_MARGIN = 1.05


def _masked_queries(q_blk, keep_low):
    lane = lax.broadcasted_iota(jnp.int32, q_blk.shape, 1)
    keep = (lane < HEAD_DIM) if keep_low else (lane >= HEAD_DIM)
    qm = jnp.where(keep, q_blk.astype(F32), 0.0)
    return qm.astype(BF16), jnp.max(jnp.sum(qm * qm, axis=1, keepdims=True))


def _dup_key_norm2(kd):
    kf = kd.astype(F32)
    return 0.5 * jnp.max(jnp.sum(kf * kf, axis=1, keepdims=True))


def _needs_no_stabiliser(q_norm2, k_norm2):
    return (q_norm2 * k_norm2 <= SCORE_BOUND * SCORE_BOUND).astype(jnp.int32)


def _plain_cols(k, vt, qm_ref, acc_ref, cols):
    col = acc_ref.shape[-1]
    for c in cols:
        s = lax.dot_general(k, qm_ref[c * col:(c + 1) * col, :], NT_DIMS, preferred_element_type=F32)
        p = jnp.exp2(s).astype(BF16)
        acc_ref[c] += jnp.dot(vt, p, preferred_element_type=F32)


def _online_cols(k, vt, qm_ref, m_ref, acc_ref, cols):
    col = acc_ref.shape[-1]
    for c in cols:
        s = lax.dot_general(k, qm_ref[c * col:(c + 1) * col, :], NT_DIMS, preferred_element_type=F32)
        m_prev = m_ref[c]
        m_new = jnp.maximum(m_prev, jnp.max(s, axis=0, keepdims=True))
        alpha = jnp.exp2(m_prev - m_new)
        p = jnp.exp2(s - m_new).astype(BF16)
        acc_ref[c] = acc_ref[c] * alpha + jnp.dot(vt, p, preferred_element_type=F32)
        m_ref[c] = m_new


def _gqa_kernel(*refs, nk_main, has_cache, tq):
    if has_cache:
        kb_ref, q_ref, k_ref, vt_ref, kc_ref, vtc_ref, o_ref, qm_ref, m_ref, acc_ref, plain_ref = refs
    else:
        kb_ref, q_ref, k_ref, vt_ref, o_ref, qm_ref, m_ref, acc_ref, plain_ref = refs
    kt = pl.program_id(3)
    col = acc_ref.shape[-1]
    per_head = tq // col
    cols = range(GQA_GROUP * per_head)

    @pl.when(kt == 0)
    def _():
        q_norm2 = None
        for g in range(GQA_GROUP):
            blk = q_ref[0, :, (g // 2) * LANES:(g // 2 + 1) * LANES]
            qm, n2 = _masked_queries(blk, g % 2 == 0)
            qm_ref[g * tq:(g + 1) * tq, :] = qm
            q_norm2 = n2 if q_norm2 is None else jnp.maximum(q_norm2, n2)
        k_norm2 = kb_ref[0]
        if has_cache:
            k_norm2 = jnp.maximum(k_norm2, _dup_key_norm2(kc_ref[0, 0]))
        plain_ref[0] = _needs_no_stabiliser(q_norm2, k_norm2)
        m_ref[...] = jnp.full(m_ref.shape, -jnp.inf, F32)
        acc_ref[...] = jnp.zeros(acc_ref.shape, F32)

    def step(kr, vr):
        @pl.when(plain_ref[0] == 1)
        def _():
            _plain_cols(kr[0, 0], vr[0, 0], qm_ref, acc_ref, cols)

        @pl.when(plain_ref[0] != 1)
        def _():
            _online_cols(kr[0, 0], vr[0, 0], qm_ref, m_ref, acc_ref, cols)

    if has_cache:
        @pl.when(kt < nk_main)
        def _():
            step(k_ref, vt_ref)

        @pl.when(kt >= nk_main)
        def _():
            step(kc_ref, vtc_ref)
    else:
        step(k_ref, vt_ref)

    @pl.when(kt == pl.num_programs(3) - 1)
    def _():
        for part in range(per_head):
            heads = []
            for g in range(GQA_GROUP):
                acc = acc_ref[g * per_head + part]
                heads.append(acc[0:HEAD_DIM, :] / acc[HEAD_DIM:HEAD_DIM + 1, :])
            o_ref[0, part * col:(part + 1) * col, :] = jnp.concatenate(heads, axis=0).T.astype(BF16)


def _kv_specs(n_heads_blk, vt_rows, tk, nk_main, cache):
    if cache:
        k_map = lambda b, h, qi, kt: (b, h, jnp.maximum(kt - nk_main, 0), 0)
        v_map = lambda b, h, qi, kt: (b, h, 0, jnp.maximum(kt - nk_main, 0))
    else:
        k_map = lambda b, h, qi, kt: (b, h, jnp.minimum(kt, nk_main - 1), 0)
        v_map = lambda b, h, qi, kt: (b, h, 0, jnp.minimum(kt, nk_main - 1))
    return [pl.BlockSpec((1, n_heads_blk, tk, LANES), k_map), pl.BlockSpec((1, 1, vt_rows, tk), v_map)]


def _gqa_attention(q, k_bound, kd, vt, kd_c, vt_c, n_batch, n_tok, tq, tk):
    has_cache = kd_c is not None
    assert not has_cache or tk == PAST_LEN
    nk_main = n_tok // tk
    nk = nk_main + (PAST_LEN // tk if has_cache else 0)
    rows = GQA_GROUP * tq
    col = min(MAX_COL, tq)
    in_specs = [pl.BlockSpec(memory_space=pltpu.SMEM),
                pl.BlockSpec((1, tq, 2 * LANES), lambda b, h, qi, kt: (b, qi, h))]
    in_specs += _kv_specs(1, GQA_VT_ROWS, tk, nk_main, False)
    args = [k_bound, q.reshape(n_batch, n_tok, BRANCH_W), kd, vt]
    if has_cache:
        in_specs += _kv_specs(1, GQA_VT_ROWS, tk, nk_main, True)
        args += [kd_c, vt_c]
    out = pl.pallas_call(
        functools.partial(_gqa_kernel, nk_main=nk_main, has_cache=has_cache, tq=tq),
        grid=(n_batch, GQA_KV_HEADS, n_tok // tq, nk),
        in_specs=in_specs,
        out_specs=pl.BlockSpec((1, tq, 2 * LANES), lambda b, h, qi, kt: (b, qi, h)),
        out_shape=jax.ShapeDtypeStruct((n_batch, n_tok, BRANCH_W), BF16),
        scratch_shapes=[
            pltpu.VMEM((rows, LANES), BF16),
            pltpu.VMEM((rows // col, 1, col), F32),
            pltpu.VMEM((rows // col, GQA_VT_ROWS, col), F32),
            pltpu.SMEM((1,), jnp.int32),
        ],
        compiler_params=pltpu.CompilerParams(
            dimension_semantics=("parallel", "parallel", "parallel", "arbitrary"),
            vmem_limit_bytes=VMEM_LIMIT),
        name="gqa_attention",
    )(*args)
    return out.reshape(n_batch * n_tok, BRANCH_W)


def _diff_kernel(*refs, nk_main, has_cache, lam_init):
    if has_cache:
        (kb_ref, q_ref, lam_ref, gsub_ref, k_ref, vt_ref, kc_ref, vtc_ref, o_ref,
         qm_ref, m_ref, acc_ref, plain_ref) = refs
    else:
        kb_ref, q_ref, lam_ref, gsub_ref, k_ref, vt_ref, o_ref, qm_ref, m_ref, acc_ref, plain_ref = refs
    kt = pl.program_id(3)
    tq = q_ref.shape[1]
    col = acc_ref.shape[-1]
    per_map = tq // col

    @pl.when(kt == 0)
    def _():
        blk = q_ref[0]
        q_norm2 = None
        for mm in range(2):
            qm, n2 = _masked_queries(blk, mm == 0)
            qm_ref[mm * tq:(mm + 1) * tq, :] = qm
            q_norm2 = n2 if q_norm2 is None else jnp.maximum(q_norm2, n2)
        k_norm2 = kb_ref[0]
        if has_cache:
            for mm in range(2):
                k_norm2 = jnp.maximum(k_norm2, _dup_key_norm2(kc_ref[0, mm]))
        plain_ref[0] = _needs_no_stabiliser(q_norm2, k_norm2)
        m_ref[...] = jnp.full(m_ref.shape, -jnp.inf, F32)
        acc_ref[...] = jnp.zeros(acc_ref.shape, F32)

    def step(kr, vr):
        @pl.when(plain_ref[0] == 1)
        def _():
            for mm in range(2):
                _plain_cols(kr[0, mm], vr[0, 0], qm_ref, acc_ref, range(mm * per_map, (mm + 1) * per_map))

        @pl.when(plain_ref[0] != 1)
        def _():
            for mm in range(2):
                _online_cols(kr[0, mm], vr[0, 0], qm_ref, m_ref, acc_ref,
                             range(mm * per_map, (mm + 1) * per_map))

    if has_cache:
        @pl.when(kt < nk_main)
        def _():
            step(k_ref, vt_ref)

        @pl.when(kt >= nk_main)
        def _():
            step(kc_ref, vtc_ref)
    else:
        step(k_ref, vt_ref)

    @pl.when(kt == pl.num_programs(3) - 1)
    def _():
        lp = lam_ref[...]
        lam = (jnp.exp(jnp.sum(lp[0:1] * lp[1:2], axis=-1, keepdims=True))
               - jnp.exp(jnp.sum(lp[2:3] * lp[3:4], axis=-1, keepdims=True)) + lam_init)
        for part in range(per_map):
            a0, a1 = acc_ref[part], acc_ref[per_map + part]
            o0 = a0[0:DIFF_DV, :] / a0[DIFF_DV:DIFF_DV + 1, :]
            o1 = a1[0:DIFF_DV, :] / a1[DIFF_DV:DIFF_DV + 1, :]
            d = (o0 - lam * o1).T
            ms = jnp.mean(d * d, axis=-1, keepdims=True)
            o_ref[0, part * col:(part + 1) * col, :] = (
                d * lax.rsqrt(ms + EPS) * gsub_ref[...] * (1.0 - lam_init)).astype(BF16)


def _diff_attention(q, k_bound, lam_p, gsub, kd, vt, kd_c, vt_c, n_batch, n_tok, tq, tk, lam_init):
    has_cache = kd_c is not None
    assert not has_cache or tk == PAST_LEN
    nk_main = n_tok // tk
    nk = nk_main + (PAST_LEN // tk if has_cache else 0)
    col = min(MAX_COL, tq)
    in_specs = [
        pl.BlockSpec(memory_space=pltpu.SMEM),
        pl.BlockSpec((1, tq, LANES), lambda b, h, qi, kt: (b, qi, h)),
        pl.BlockSpec((4, HEAD_DIM), lambda b, h, qi, kt: (0, 0)),
        pl.BlockSpec((1, DIFF_DV), lambda b, h, qi, kt: (0, 0)),
    ]
    in_specs += _kv_specs(2, DIFF_VT_ROWS, tk, nk_main, False)
    args = [k_bound, q.reshape(n_batch, n_tok, BRANCH_W), lam_p, gsub, kd, vt]
    if has_cache:
        in_specs += _kv_specs(2, DIFF_VT_ROWS, tk, nk_main, True)
        args += [kd_c, vt_c]
    out = pl.pallas_call(
        functools.partial(_diff_kernel, nk_main=nk_main, has_cache=has_cache, lam_init=lam_init),
        grid=(n_batch, DIFF_HEADS, n_tok // tq, nk),
        in_specs=in_specs,
        out_specs=pl.BlockSpec((1, tq, LANES), lambda b, h, qi, kt: (b, qi, h)),
        out_shape=jax.ShapeDtypeStruct((n_batch, n_tok, BRANCH_W), BF16),
        scratch_shapes=[
            pltpu.VMEM((2 * tq, LANES), BF16),
            pltpu.VMEM((2 * tq // col, 1, col), F32),
            pltpu.VMEM((2 * tq // col, DIFF_VT_ROWS, col), F32),
            pltpu.SMEM((1,), jnp.int32),
        ],
        compiler_params=pltpu.CompilerParams(
            dimension_semantics=("parallel", "parallel", "parallel", "arbitrary"),
            vmem_limit_bytes=VMEM_LIMIT),
        name="diff_attention",
    )(*args)
    return out.reshape(n_batch * n_tok, BRANCH_W)


def _gla_kernel(*refs, has_s0, n_chunk):
    if has_s0:
        qf, kf, vf, laf, qb, kb, vb, lab, s0_ref, of_ref, ob_ref, sfin_ref, st_ref = refs
    else:
        qf, kf, vf, laf, qb, kb, vb, lab, of_ref, ob_ref, sfin_ref, st_ref = refs
    i = pl.program_id(1)
    ck = GLA_CHUNK

    @pl.when(i == 0)
    def _():
        for d in range(2):
            for hd in range(GLA_HEADS):
                if has_s0:
                    s = s0_ref[0, d, hd]
                    z = jnp.zeros_like(s)
                    padded = jnp.concatenate([s, z] if hd % 2 == 0 else [z, s], axis=0)
                    st_ref[d, hd] = padded.T
                else:
                    st_ref[d, hd] = jnp.zeros((GLA_DV, LANES), F32)

    r = lax.broadcasted_iota(jnp.int32, (ck, ck), 0)
    c = lax.broadcasted_iota(jnp.int32, (ck, ck), 1)
    lane = lax.broadcasted_iota(jnp.int32, (ck, LANES), 1)
    streams = ((qf, kf, vf, laf, of_ref, c <= r), (qb, kb, vb, lab, ob_ref, c >= r))
    for d, (q_r, k_r, v_r, la_r, o_r, tri) in enumerate(streams):
        tri_f = tri.astype(F32)
        for cc in range(n_chunk):
            ci = cc if d == 0 else n_chunk - 1 - cc
            rows = slice(ci * ck, (ci + 1) * ck)
            la = la_r[rows, :]
            cum = jnp.dot(tri_f, la, precision=HIGHEST, preferred_element_type=F32)
            tot = cum[ck - 1:ck, :] if d == 0 else cum[0:1, :]
            q = q_r[rows, :].astype(F32)
            k = k_r[rows, :].astype(F32)
            v = v_r[rows, :]
            q_dec = q * jnp.exp(cum)
            k_inv = k * jnp.exp(-cum)
            k_end = k * jnp.exp(tot - cum)
            decay = jnp.exp(tot)
            for hd in range(GLA_HEADS):
                pair = slice((hd // 2) * LANES, (hd // 2 + 1) * LANES)
                keep = (lane < GLA_DK) if hd % 2 == 0 else (lane >= GLA_DK)
                qd_m = jnp.where(keep, q_dec[:, pair], 0.0).astype(BF16)
                ke_m = jnp.where(keep, k_end[:, pair], 0.0).astype(BF16)
                ki_p = k_inv[:, pair].astype(BF16)
                a = lax.dot_general(qd_m, ki_p, (((1,), (1,)), ((), ())), preferred_element_type=F32)
                a = jnp.where(tri, a, 0.0).astype(BF16)
                hv = slice(hd * GLA_DV, (hd + 1) * GLA_DV)
                s_t = st_ref[d, hd]
                o = jnp.dot(a, v[:, hv], preferred_element_type=F32)
                o = o + lax.dot_general(qd_m, s_t.astype(BF16), (((1,), (1,)), ((), ())),
                                        preferred_element_type=F32)
                o_r[rows, hv] = o
                ds_t = lax.dot_general(v[:, hv], ke_m, (((0,), (0,)), ((), ())), preferred_element_type=F32)
                st_ref[d, hd] = s_t * decay[:, pair] + ds_t

    @pl.when(i == pl.num_programs(1) - 1)
    def _():
        for d in range(2):
            for hd in range(GLA_HEADS):
                t = st_ref[d, hd].T
                sfin_ref[0, d, hd] = t[(hd % 2) * GLA_DK:(hd % 2 + 1) * GLA_DK, :]


def _gla(lq, lk, lv, la, s0, n_batch, n_tok, tb):
    nb = n_tok // tb
    t_total = n_batch * n_tok
    has_s0 = s0 is not None
    fwd = lambda b, i: (b * nb + i, 0)
    bwd = lambda b, i: (b * nb + nb - 1 - i, 0)
    bwd_la = lambda b, i: (b * nb + nb - 1 - i, 1)
    in_specs = [
        pl.BlockSpec((tb, GLA_W), fwd), pl.BlockSpec((tb, GLA_W), fwd),
        pl.BlockSpec((tb, BRANCH_W), fwd), pl.BlockSpec((tb, GLA_W), fwd),
        pl.BlockSpec((tb, GLA_W), bwd), pl.BlockSpec((tb, GLA_W), bwd),
        pl.BlockSpec((tb, BRANCH_W), bwd), pl.BlockSpec((tb, GLA_W), bwd_la),
    ]
    args = [lq, lk, lv, la, lq, lk, lv, la]
    state_spec = pl.BlockSpec((1, 2, GLA_HEADS, GLA_DK, GLA_DV), lambda b, i: (b, 0, 0, 0, 0))
    if has_s0:
        in_specs.append(state_spec)
        args.append(s0)
    return pl.pallas_call(
        functools.partial(_gla_kernel, has_s0=has_s0, n_chunk=tb // GLA_CHUNK),
        grid=(n_batch, nb),
        in_specs=in_specs,
        out_specs=[pl.BlockSpec((tb, BRANCH_W), fwd), pl.BlockSpec((tb, BRANCH_W), bwd), state_spec],
        out_shape=[
            jax.ShapeDtypeStruct((t_total, BRANCH_W), F32),
            jax.ShapeDtypeStruct((t_total, BRANCH_W), F32),
            jax.ShapeDtypeStruct((n_batch, 2, GLA_HEADS, GLA_DK, GLA_DV), F32),
        ],
        scratch_shapes=[pltpu.VMEM((2, GLA_HEADS, GLA_DV, LANES), F32)],
        compiler_params=pltpu.CompilerParams(
            dimension_semantics=("parallel", "arbitrary"), vmem_limit_bytes=VMEM_LIMIT),
        name="gla",
    )(*args)


def _merge_kernel(x_ref, mod_ref, og_ref, of_ref, ob_ref, lgo_ref, od_ref, gate_ref,
                  gout_ref, wb_ref, wo_ref, y_ref):
    o_gla = of_ref[...] + ob_ref[...]
    gla_parts = []
    for hd in range(GLA_HEADS):
        blk = o_gla[:, hd * GLA_DV:(hd + 1) * GLA_DV]
        ms = jnp.mean(blk * blk, axis=-1, keepdims=True)
        gla_parts.append(blk * lax.rsqrt(ms + EPS) * gout_ref[...])
    gla = jnp.concatenate(gla_parts, axis=1) * _silu(lgo_ref[...].astype(F32))
    branches = (og_ref[...], gla.astype(BF16), od_ref[...])
    mixed = None
    for b, ob in enumerate(branches):
        y = jnp.dot(ob, wb_ref[b], preferred_element_type=F32)
        y = y * gate_ref[:, b * D_MODEL:(b + 1) * D_MODEL].astype(F32)
        mixed = y if mixed is None else mixed + y
    out = jnp.dot(mixed.astype(BF16), wo_ref[...], preferred_element_type=F32)
    y_ref[...] = x_ref[...] + mod_ref[0, 2:3, :] * out


def _merge(x2d, mod, og, o_f, o_b, lgo, od, gates, lw, tiles_per_mod, tm):
    t_total = x2d.shape[0]
    tile = lambda c: pl.BlockSpec((tm, c), lambda i: (i, 0))
    return pl.pallas_call(
        _merge_kernel,
        grid=(t_total // tm,),
        in_specs=[
            tile(D_MODEL),
            pl.BlockSpec((1, 6, D_MODEL), lambda i: (i // tiles_per_mod, 0, 0)),
            tile(BRANCH_W), tile(BRANCH_W), tile(BRANCH_W), tile(BRANCH_W), tile(BRANCH_W),
            tile(N_BRANCH * D_MODEL),
            _full_spec((1, GLA_DV)),
            _full_spec((N_BRANCH, BRANCH_W, D_MODEL)),
            _full_spec((D_MODEL, D_MODEL)),
        ],
        out_specs=tile(D_MODEL),
        out_shape=jax.ShapeDtypeStruct((t_total, D_MODEL), F32),
        compiler_params=pltpu.CompilerParams(
            dimension_semantics=("parallel",), vmem_limit_bytes=VMEM_LIMIT),
        name="merge",
    )(x2d, mod, og, o_f, o_b, lgo, od, gates, lw["gout"], lw["w_branch"], lw["w_out"])


FFN_CHUNK = 256


def _ffn_kernel(x_ref, mod_ref, g2_ref, wi_ref, wo_ref, y_ref):
    x = x_ref[...]
    ms = jnp.mean(x * x, axis=-1, keepdims=True)
    h = x * lax.rsqrt(ms + EPS) * g2_ref[...]
    h = h * (1.0 + mod_ref[0, 4:5, :]) + mod_ref[0, 3:4, :]
    hb = h.astype(BF16)
    acc = None
    for c0 in range(0, FFN_HIDDEN, FFN_CHUNK):
        a = jnp.dot(hb, wi_ref[:, c0:c0 + FFN_CHUNK], preferred_element_type=F32)
        u = jnp.dot(hb, wi_ref[:, FFN_HIDDEN + c0:FFN_HIDDEN + c0 + FFN_CHUNK], preferred_element_type=F32)
        act = (_silu(a) * u).astype(BF16)
        part = jnp.dot(act, wo_ref[c0:c0 + FFN_CHUNK, :], preferred_element_type=F32)
        acc = part if acc is None else acc + part
    y_ref[...] = x + mod_ref[0, 5:6, :] * acc


def _ffn(x2d, mod, lw, tiles_per_mod, tm):
    t_total = x2d.shape[0]
    tile = lambda c: pl.BlockSpec((tm, c), lambda i: (i, 0))
    return pl.pallas_call(
        _ffn_kernel,
        grid=(t_total // tm,),
        in_specs=[
            tile(D_MODEL),
            pl.BlockSpec((1, 6, D_MODEL), lambda i: (i // tiles_per_mod, 0, 0)),
            _full_spec((1, D_MODEL)),
            _full_spec((D_MODEL, 2 * FFN_HIDDEN)),
            _full_spec((FFN_HIDDEN, D_MODEL)),
        ],
        out_specs=tile(D_MODEL),
        out_shape=jax.ShapeDtypeStruct((t_total, D_MODEL), F32),
        compiler_params=pltpu.CompilerParams(
            dimension_semantics=("parallel",), vmem_limit_bytes=VMEM_LIMIT),
        name="ffn",
    )(x2d, mod, lw["norm2"], lw["w_ffn_in"], lw["w_ffn_out"])


def _layer_weights(l, p):
    w = p["w_in"][l]
    s = (GQA_HEADS * HEAD_DIM, GQA_KV_HEADS * HEAD_DIM, GQA_KV_HEADS * HEAD_DIM,
         GLA_HEADS * GLA_DK, GLA_HEADS * GLA_DK, GLA_HEADS * GLA_DV, 2 * GLA_RANK, GLA_HEADS * GLA_DV,
         DIFF_HEADS * 2 * HEAD_DIM, DIFF_HEADS * 2 * HEAD_DIM, DIFF_HEADS * DIFF_DV, N_BRANCH * D_MODEL)
    offs = [0]
    for n in s:
        offs.append(offs[-1] + n)
    col = lambda j: w[:, offs[j]:offs[j + 1]]
    gq, gk, gv, lq, lk, lv, lr, lgo, dq, dk, dv, mg = (col(j) for j in range(12))
    lr_pad = jnp.zeros((D_MODEL, LANES - 2 * GLA_RANK), w.dtype)
    w_in = jnp.concatenate([gq, dq, gk, dk, gv, dv, lq, lk, lv, lgo, lr, lr_pad, mg], axis=1).astype(BF16)
    scale = HEAD_DIM ** -0.5 * math.log2(math.e)
    key_bound = lambda g: (HEAD_DIM * K_NORM_MARGIN ** 2 * jnp.max(g * g)).reshape(1)
    gq_row = jnp.concatenate([jnp.tile(p["gqa_q_norm"][l], GQA_HEADS),
                              jnp.tile(p["diff_q_norm"][l], 2 * DIFF_HEADS)]) * scale
    gk_row = jnp.concatenate([jnp.tile(p["gqa_k_norm"][l], GQA_KV_HEADS),
                              jnp.tile(p["diff_k_norm"][l], 2 * DIFF_HEADS)])
    idx = jnp.arange(2 * LANES) // HEAD_DIM
    gmat = jnp.where(idx[:, None] == idx[None, :], 1.0 / HEAD_DIM, 0.0).astype(BF16)
    up = jnp.zeros((LANES, 2 * GLA_W), F32)
    up = up.at[0:GLA_RANK, 0:GLA_W].set(p["gla_alpha_up"][l, 0])
    up = up.at[GLA_RANK:2 * GLA_RANK, GLA_W:].set(p["gla_alpha_up"][l, 1])
    ub = p["gla_alpha_bias"][l].reshape(1, 2 * GLA_W)
    return {
        "w_in": w_in, "gq": gq_row.reshape(1, N_Q), "gk": gk_row.reshape(1, N_K), "gmat": gmat,
        "up": up, "ub": ub,
        "kb_g": key_bound(p["gqa_k_norm"][l]), "kb_d": key_bound(p["diff_k_norm"][l]),
        "norm1": p["norm1"][l].reshape(1, D_MODEL), "norm2": p["norm2"][l].reshape(1, D_MODEL),
        "gout": p["gla_out_norm"][l].reshape(1, GLA_DV),
        "gsub": p["diff_sub_norm"][l].reshape(1, DIFF_DV),
        "lam": p["diff_lambda"][l],
        "w_branch": p["w_branch"][l].astype(BF16), "w_out": p["w_out"][l].astype(BF16),
        "w_ffn_in": p["w_ffn_in"][l].astype(BF16), "w_ffn_out": p["w_ffn_out"][l].astype(BF16),
    }


def _rope_tables(n_tokens):
    n_rows = n_tokens // GRID_W
    row = jnp.repeat(jnp.arange(n_rows, dtype=F32), GRID_W)
    col = jnp.tile(jnp.arange(GRID_W, dtype=F32), n_rows)
    n_freq = HEAD_DIM // 4
    freqs = ROPE_THETA ** (-jnp.arange(n_freq, dtype=F32) / n_freq)
    ar, ac = row[:, None] * freqs, col[:, None] * freqs
    cos = jnp.concatenate([jnp.cos(ar), jnp.cos(ar), jnp.cos(ac), jnp.cos(ac)], axis=1)
    sin = jnp.concatenate([-jnp.sin(ar), jnp.sin(ar), -jnp.sin(ac), jnp.sin(ac)], axis=1)
    return jnp.tile(cos, (1, LANES // HEAD_DIM)), jnp.tile(sin, (1, LANES // HEAD_DIM))


def _with_ones_rows(v_t):
    lead = v_t.shape[:-2]
    s = v_t.shape[-1]
    ones = jnp.ones(lead + (1, s), v_t.dtype)
    zeros = jnp.zeros(lead + (BF16_ROWS - 1, s), v_t.dtype)
    return jnp.concatenate([v_t, ones, zeros], axis=-2).astype(BF16)


def _cache_layouts(l, cache_gqa_k, cache_gqa_v, cache_diff_k, cache_diff_v):
    b = cache_gqa_k.shape[0]
    gk = jnp.transpose(cache_gqa_k[:, l], (0, 2, 1, 3))
    kd_g = jnp.concatenate([gk, gk], axis=-1).astype(BF16)
    vt_g = _with_ones_rows(jnp.transpose(cache_gqa_v[:, l], (0, 2, 3, 1)))
    dk = jnp.transpose(cache_diff_k[:, l], (0, 2, 3, 1, 4)).reshape(b, 2 * DIFF_HEADS, PAST_LEN, HEAD_DIM)
    kd_d = jnp.concatenate([dk, dk], axis=-1).astype(BF16)
    vt_d = _with_ones_rows(jnp.transpose(cache_diff_v[:, l], (0, 2, 3, 1)))
    return kd_g, vt_g, kd_d, vt_d


def _run_layer(x2d, mod, lw, l, n_batch, n_tok, rope_tabs, ctx, emit_cache, cfg):
    tiles_per_mod = lambda tm: (n_tok // tm) if mod.shape[0] > 1 else (n_batch * n_tok // tm)
    outs = _in_projection(x2d, mod, lw, n_batch, n_tok, rope_tabs, emit_cache, cfg["tm_in"])
    qg, qd, kdg, kdd, vtg, vtd, lq, lk, lv, lgo, la, gates = outs[:12]
    if ctx is None:
        kd_gc = vt_gc = kd_dc = vt_dc = s0 = None
    else:
        kd_gc, vt_gc, kd_dc, vt_dc, s0 = ctx
    og = _gqa_attention(qg, lw["kb_g"], kdg, vtg, kd_gc, vt_gc, n_batch, n_tok, cfg["tq_gqa"], cfg["tk"])
    lam_init = 0.8 - 0.6 * math.exp(-0.3 * l)
    od = _diff_attention(qd, lw["kb_d"], lw["lam"], lw["gsub"], kdd, vtd, kd_dc, vt_dc, n_batch, n_tok,
                         cfg["tq_diff"], cfg["tk"], lam_init)
    o_f, o_b, s_fin = _gla(lq, lk, lv, la, s0, n_batch, n_tok, cfg["tb_gla"])
    x2d = _merge(x2d, mod, og, o_f, o_b, lgo, od, gates, lw, tiles_per_mod(cfg["tm"]), cfg["tm"])
    x2d = _ffn(x2d, mod, lw, tiles_per_mod(cfg["tm"]), cfg["tm"])
    cache = tuple(outs[12:]) + (s_fin,) if emit_cache else None
    return x2d, cache


PROMPT_CFG = dict(tm_in=256, tq_gqa=256, tq_diff=256, tk=256, tb_gla=256, tm=512)
SAMPLE_CFG = dict(tm_in=256, tq_gqa=512, tq_diff=1024, tk=512, tb_gla=256, tm=512)


def kernel(x_prompt, x_sample, c, cache_gqa_k, cache_gqa_v, state_gla, cache_diff_k, cache_diff_v, c_ctx, w_mod, b_mod, norm1, norm2, w_in, gqa_q_norm, gqa_k_norm, gla_alpha_up, gla_alpha_bias, gla_out_norm, diff_q_norm, diff_k_norm, diff_lambda, diff_sub_norm, w_branch, w_out, w_ffn_in, w_ffn_out):
    p = {
        "norm1": norm1, "norm2": norm2, "w_in": w_in, "gqa_q_norm": gqa_q_norm, "gqa_k_norm": gqa_k_norm,
        "gla_alpha_up": gla_alpha_up, "gla_alpha_bias": gla_alpha_bias, "gla_out_norm": gla_out_norm,
        "diff_q_norm": diff_q_norm, "diff_k_norm": diff_k_norm, "diff_lambda": diff_lambda,
        "diff_sub_norm": diff_sub_norm, "w_branch": w_branch, "w_out": w_out,
        "w_ffn_in": w_ffn_in, "w_ffn_out": w_ffn_out,
    }
    n_ctx_b, n_ctx = x_prompt.shape[:2]
    n_lat_b, n_lat = x_sample.shape[:2]
    cond_rows = jnp.concatenate(
        [c_ctx[None, :], c, jnp.zeros((8 - 1 - n_lat_b, D_MODEL), F32)], axis=0)
    mod_all = _modulation(cond_rows, w_mod, b_mod)
    weights = [_layer_weights(l, p) for l in range(DEPTH)]

    y = x_prompt.reshape(n_ctx_b * n_ctx, D_MODEL)
    caches = []
    for l in range(DEPTH):
        mod = mod_all[l, 0:1].reshape(1, 6, D_MODEL)
        y, cache = _run_layer(y, mod, weights[l], l, n_ctx_b, n_ctx, None, None, True, PROMPT_CFG)
        caches.append(cache)
    y_prompt = y.reshape(n_ctx_b, n_ctx, D_MODEL)
    stack = lambda j, shape: jnp.stack([cc[j].reshape(shape) for cc in caches], axis=1)
    new_gqa_k = stack(0, (n_ctx_b, n_ctx, GQA_KV_HEADS, HEAD_DIM))
    new_diff_k = stack(1, (n_ctx_b, n_ctx, DIFF_HEADS, 2, HEAD_DIM))
    new_gqa_v = stack(2, (n_ctx_b, n_ctx, GQA_KV_HEADS, HEAD_DIM))
    new_diff_v = stack(3, (n_ctx_b, n_ctx, DIFF_HEADS, DIFF_DV))
    new_state_gla = jnp.stack([cc[4] for cc in caches], axis=1)

    rope_tabs = _rope_tables(n_lat)
    y = x_sample.reshape(n_lat_b * n_lat, D_MODEL)
    for l in range(DEPTH):
        mod = mod_all[l, 1:1 + n_lat_b].reshape(n_lat_b, 6, D_MODEL)
        ctx = _cache_layouts(l, cache_gqa_k, cache_gqa_v, cache_diff_k, cache_diff_v) + (state_gla[:, l],)
        y, _ = _run_layer(y, mod, weights[l], l, n_lat_b, n_lat, rope_tabs, ctx, False, SAMPLE_CFG)
    y_sample = y.reshape(n_lat_b, n_lat, D_MODEL)
    return (y_prompt, y_sample, new_gqa_k, new_gqa_v, new_state_gla, new_diff_k, new_diff_v)
```

```python
import functools
import math

import jax
import jax.numpy as jnp
from jax import lax
from jax.experimental import pallas as pl
from jax.experimental.pallas import tpu as pltpu

D_MODEL = 1024
DEPTH = 2
PAST_LEN = 512
GRID_W = 64
HEAD_DIM = 64
GQA_HEADS = 8
GQA_KV_HEADS = 2
GQA_GROUP = GQA_HEADS // GQA_KV_HEADS
GLA_HEADS = 4
GLA_DK = 64
GLA_DV = 128
GLA_RANK = 16
GLA_TAU = 16.0
GLA_CHUNK = 64
DIFF_HEADS = 4
DIFF_DV = 2 * HEAD_DIM
N_BRANCH = 3
BRANCH_W = 512
FFN_HIDDEN = ((8 * D_MODEL + 3 * 256 - 1) // (3 * 256)) * 256
ROPE_THETA = 10000.0
EPS = 1e-6

LANES = 128
BF16_ROWS = 16
VMEM_LIMIT = 56 * 1024 * 1024

F32 = jnp.float32
BF16 = jnp.bfloat16
HIGHEST = lax.Precision.HIGHEST

C_GQ = 0
C_DQ = C_GQ + GQA_HEADS * HEAD_DIM
C_GK = C_DQ + DIFF_HEADS * 2 * HEAD_DIM
C_DK = C_GK + GQA_KV_HEADS * HEAD_DIM
C_GV = C_DK + DIFF_HEADS * 2 * HEAD_DIM
C_DV = C_GV + GQA_KV_HEADS * HEAD_DIM
C_LQ = C_DV + DIFF_HEADS * DIFF_DV
C_LK = C_LQ + GLA_HEADS * GLA_DK
C_LV = C_LK + GLA_HEADS * GLA_DK
C_LGO = C_LV + GLA_HEADS * GLA_DV
C_LR = C_LGO + GLA_HEADS * GLA_DV
C_MG = C_LR + LANES
N_IN_PAD = C_MG + N_BRANCH * D_MODEL
N_Q = C_GK - C_GQ
N_K = C_GV - C_GK
N_V = C_LQ - C_GV
GLA_W = GLA_HEADS * GLA_DK
GQA_VT_ROWS = HEAD_DIM + BF16_ROWS
DIFF_VT_ROWS = DIFF_DV + BF16_ROWS


def _sigmoid(x):
    return 1.0 / (1.0 + jnp.exp(-x))


def _silu(x):
    return x * _sigmoid(x)


def _bf16_terms(x):
    hi = x.astype(BF16)
    r1 = x - hi.astype(F32)
    mid = r1.astype(BF16)
    lo = (r1 - mid.astype(F32)).astype(BF16)
    return hi, mid, lo


def _full_spec(shape):
    n = len(shape)
    return pl.BlockSpec(shape, lambda *_: (0,) * n, pipeline_mode=pl.Buffered(1))


def _mod_kernel(cond_ref, w_ref, b_ref, o_ref):
    s = _silu(cond_ref[...])
    o_ref[0] = jnp.dot(s, w_ref[0], precision=HIGHEST, preferred_element_type=F32) + b_ref[0]


def _modulation(cond_rows, w_mod, b_mod):
    tn = 1536
    n_out = 6 * D_MODEL
    return pl.pallas_call(
        _mod_kernel,
        grid=(DEPTH, n_out // tn),
        in_specs=[
            pl.BlockSpec((8, D_MODEL), lambda l, j: (0, 0)),
            pl.BlockSpec((1, D_MODEL, tn), lambda l, j: (l, 0, j)),
            pl.BlockSpec((1, 1, tn), lambda l, j: (l, 0, j)),
        ],
        out_specs=pl.BlockSpec((1, 8, tn), lambda l, j: (l, 0, j)),
        out_shape=jax.ShapeDtypeStruct((DEPTH, 8, n_out), F32),
        compiler_params=pltpu.CompilerParams(
            dimension_semantics=("parallel", "parallel"), vmem_limit_bytes=VMEM_LIMIT),
        name="modulation",
    )(cond_rows, w_mod, b_mod.reshape(DEPTH, 1, n_out))


def _head_norm(z, gain, gmat):
    outs = []
    n = z.shape[1]
    for c0 in range(0, n, 2 * LANES):
        w = min(2 * LANES, n - c0)
        zz = z[:, c0:c0 + w]
        ms = jnp.dot((zz * zz).astype(BF16), gmat[0:w, 0:w], preferred_element_type=F32)
        outs.append(zz * lax.rsqrt(ms + EPS) * gain[:, c0:c0 + w])
    return outs


def _rope_block(zb, cos, sin, first):
    partner = jnp.where(first, pltpu.roll(zb, LANES - 16, 1), pltpu.roll(zb, 16, 1))
    return zb * cos + partner * sin


def _in_kernel(*refs, rope, emit_cache):
    it = iter(refs)
    x_ref, mod_ref, g1_ref, w_ref, gq_ref, gk_ref, gmat_ref, up_ref, ub_ref = (next(it) for _ in range(9))
    if rope:
        cos_ref, sin_ref = next(it), next(it)
    (qg_ref, qd_ref, kdg_ref, kdd_ref, vtg_ref, vtd_ref,
     lq_ref, lk_ref, lv_ref, lgo_ref, la_ref, gate_ref) = (next(it) for _ in range(12))
    if emit_cache:
        ck_g_ref, ck_d_ref, cv_g_ref, cv_d_ref = (next(it) for _ in range(4))

    tm = x_ref.shape[0]
    x = x_ref[...]
    ms = jnp.mean(x * x, axis=-1, keepdims=True)
    h = x * lax.rsqrt(ms + EPS) * g1_ref[...]
    h = h * (1.0 + mod_ref[0, 1:2, :]) + mod_ref[0, 0:1, :]
    hb = h.astype(BF16)

    def proj(c0, n):
        return jnp.dot(hb, w_ref[:, c0:c0 + n], preferred_element_type=F32)

    lane = lax.broadcasted_iota(jnp.int32, (tm, LANES), 1)
    first16 = (lane % 32) < 16
    lo_half = lane < HEAD_DIM
    gmat = gmat_ref[...]
    if rope:
        cos, sin = cos_ref[...], sin_ref[...]

    def blocks128(parts):
        out = []
        for p in parts:
            for c in range(0, p.shape[1], LANES):
                out.append(p[:, c:c + LANES])
        return out

    qb = blocks128(_head_norm(proj(C_GQ, N_Q), gq_ref[...], gmat))
    for j, zb in enumerate(qb):
        if rope:
            zb = _rope_block(zb, cos, sin, first16)
        dst = qg_ref if j < 4 else qd_ref
        c = (j % 4) * LANES
        dst[:, c:c + LANES] = zb.astype(BF16)

    kb = blocks128(_head_norm(proj(C_GK, N_K), gk_ref[...], gmat))
    for j, zb in enumerate(kb):
        if rope:
            zb = _rope_block(zb, cos, sin, first16)
        if emit_cache:
            if j == 0:
                ck_g_ref[...] = zb
            else:
                ck_d_ref[:, (j - 1) * LANES:j * LANES] = zb
        sw = pltpu.roll(zb, HEAD_DIM, 1)
        dup_lo = jnp.where(lo_half, zb, sw).astype(BF16)
        dup_hi = jnp.where(lo_half, sw, zb).astype(BF16)
        if j == 0:
            kdg_ref[0, 0] = dup_lo
            kdg_ref[0, 1] = dup_hi
        else:
            kdd_ref[0, 2 * (j - 1)] = dup_lo
            kdd_ref[0, 2 * (j - 1) + 1] = dup_hi

    ones_rows = jnp.where(lax.broadcasted_iota(jnp.int32, (BF16_ROWS, tm), 0) == 0, 1.0, 0.0).astype(BF16)
    zv = proj(C_GV, N_V)
    if emit_cache:
        cv_g_ref[...] = zv[:, 0:LANES]
        cv_d_ref[...] = zv[:, LANES:]
    vt = zv[:, 0:LANES].T
    for hh in range(GQA_KV_HEADS):
        vtg_ref[0, hh, 0:HEAD_DIM, :] = vt[hh * HEAD_DIM:(hh + 1) * HEAD_DIM, :].astype(BF16)
        vtg_ref[0, hh, HEAD_DIM:GQA_VT_ROWS, :] = ones_rows
    for hh in range(DIFF_HEADS):
        vt = zv[:, (hh + 1) * LANES:(hh + 2) * LANES].T
        vtd_ref[0, hh, 0:DIFF_DV, :] = vt.astype(BF16)
        vtd_ref[0, hh, DIFF_DV:DIFF_VT_ROWS, :] = ones_rows

    zl = proj(C_LQ, C_MG - C_LQ)
    o = 0
    lq_ref[...] = (zl[:, o:o + GLA_W] * (GLA_DK ** -0.5)).astype(BF16)
    o += GLA_W
    lk_ref[...] = zl[:, o:o + GLA_W].astype(BF16)
    o += GLA_W
    lv_ref[...] = zl[:, o:o + BRANCH_W].astype(BF16)
    o += BRANCH_W
    lgo_ref[...] = zl[:, o:o + BRANCH_W].astype(BF16)
    o += BRANCH_W
    lr = zl[:, o:o + LANES]
    lr_hi, lr_mid, _ = _bf16_terms(lr)
    z = (jnp.dot(lr_hi, up_ref[0], preferred_element_type=F32)
         + jnp.dot(lr_mid, up_ref[0], preferred_element_type=F32)
         + jnp.dot(lr_hi, up_ref[1], preferred_element_type=F32)) + ub_ref[...]
    log_sig = jnp.minimum(z, 0.0) - jnp.log(1.0 + jnp.exp(-jnp.abs(z)))
    la_ref[...] = log_sig * (1.0 / GLA_TAU)

    for b in range(N_BRANCH):
        gate_ref[:, b * D_MODEL:(b + 1) * D_MODEL] = _sigmoid(proj(C_MG + b * D_MODEL, D_MODEL)).astype(BF16)


def _in_projection(x2d, mod, lw, n_batch, n_tok, rope_tabs, emit_cache, tm):
    t_total = n_batch * n_tok
    tpb = n_tok // tm
    rope = rope_tabs is not None
    tile = lambda c: pl.BlockSpec((tm, c), lambda i: (i, 0))
    in_specs = [
        tile(D_MODEL),
        pl.BlockSpec((1, 6, D_MODEL), lambda i: (i // tpb if mod.shape[0] > 1 else 0, 0, 0)),
        _full_spec((1, D_MODEL)),
        _full_spec((D_MODEL, N_IN_PAD)),
        _full_spec((1, N_Q)),
        _full_spec((1, N_K)),
        _full_spec((2 * LANES, 2 * LANES)),
        _full_spec((2, LANES, 2 * GLA_W)),
        _full_spec((1, 2 * GLA_W)),
    ]
    args = [x2d, mod, lw["norm1"], lw["w_in"], lw["gq"], lw["gk"], lw["gmat"], lw["up"], lw["ub"]]
    if rope:
        in_specs += [pl.BlockSpec((tm, LANES), lambda i: (i % tpb, 0))] * 2
        args += list(rope_tabs)
    sd = jax.ShapeDtypeStruct
    out_shape = [
        sd((t_total, BRANCH_W), BF16), sd((t_total, BRANCH_W), BF16),
        sd((n_batch, GQA_KV_HEADS, n_tok, LANES), BF16), sd((n_batch, 2 * DIFF_HEADS, n_tok, LANES), BF16),
        sd((n_batch, GQA_KV_HEADS, GQA_VT_ROWS, n_tok), BF16), sd((n_batch, DIFF_HEADS, DIFF_VT_ROWS, n_tok), BF16),
        sd((t_total, GLA_W), BF16), sd((t_total, GLA_W), BF16),
        sd((t_total, BRANCH_W), BF16), sd((t_total, BRANCH_W), BF16),
        sd((t_total, 2 * GLA_W), F32), sd((t_total, N_BRANCH * D_MODEL), BF16),
    ]
    out_specs = [
        tile(BRANCH_W), tile(BRANCH_W),
        pl.BlockSpec((1, GQA_KV_HEADS, tm, LANES), lambda i: (i // tpb, 0, i % tpb, 0)),
        pl.BlockSpec((1, 2 * DIFF_HEADS, tm, LANES), lambda i: (i // tpb, 0, i % tpb, 0)),
        pl.BlockSpec((1, GQA_KV_HEADS, GQA_VT_ROWS, tm), lambda i: (i // tpb, 0, 0, i % tpb)),
        pl.BlockSpec((1, DIFF_HEADS, DIFF_VT_ROWS, tm), lambda i: (i // tpb, 0, 0, i % tpb)),
        tile(GLA_W), tile(GLA_W), tile(BRANCH_W), tile(BRANCH_W), tile(2 * GLA_W), tile(N_BRANCH * D_MODEL),
    ]
    if emit_cache:
        out_shape += [sd((t_total, LANES), F32), sd((t_total, BRANCH_W), F32),
                      sd((t_total, LANES), F32), sd((t_total, BRANCH_W), F32)]
        out_specs += [tile(LANES), tile(BRANCH_W), tile(LANES), tile(BRANCH_W)]
    return pl.pallas_call(
        functools.partial(_in_kernel, rope=rope, emit_cache=emit_cache),
        grid=(t_total // tm,),
        in_specs=in_specs,
        out_specs=out_specs,
        out_shape=out_shape,
        compiler_params=pltpu.CompilerParams(
            dimension_semantics=("parallel",), vmem_limit_bytes=VMEM_LIMIT),
        name="in_projection",
    )(*args)


MAX_COL = 4 * LANES
NT_DIMS = (((1,), (1,)), ((), ()))
SCORE_BOUND = 45.0
K_NORM_MARGIN = 1.05

def _masked_queries(q_blk, keep_low):
    lane = lax.broadcasted_iota(jnp.int32, q_blk.shape, 1)
    keep = (lane < HEAD_DIM) if keep_low else (lane >= HEAD_DIM)
    qm = jnp.where(keep, q_blk.astype(F32), 0.0)
    return qm.astype(BF16), jnp.max(jnp.sum(qm * qm, axis=1, keepdims=True))


def _dup_key_norm2(kd):
    kf = kd.astype(F32)
    return 0.5 * jnp.max(jnp.sum(kf * kf, axis=1, keepdims=True))


def _needs_no_stabiliser(q_norm2, k_norm2):
    return (q_norm2 * k_norm2 <= SCORE_BOUND * SCORE_BOUND).astype(jnp.int32)


def _plain_cols(k, vt, qm_ref, acc_ref, cols):
    col = acc_ref.shape[-1]
    for c in cols:
        s = lax.dot_general(k, qm_ref[c * col:(c + 1) * col, :], NT_DIMS, preferred_element_type=F32)
        p = jnp.exp2(s).astype(BF16)
        acc_ref[c] += jnp.dot(vt, p, preferred_element_type=F32)


def _online_cols(k, vt, qm_ref, m_ref, acc_ref, cols):
    col = acc_ref.shape[-1]
    for c in cols:
        s = lax.dot_general(k, qm_ref[c * col:(c + 1) * col, :], NT_DIMS, preferred_element_type=F32)
        m_prev = m_ref[c]
        m_new = jnp.maximum(m_prev, jnp.max(s, axis=0, keepdims=True))
        alpha = jnp.exp2(m_prev - m_new)
        p = jnp.exp2(s - m_new).astype(BF16)
        acc_ref[c] = acc_ref[c] * alpha + jnp.dot(vt, p, preferred_element_type=F32)
        m_ref[c] = m_new


def _gqa_kernel(*refs, nk_main, has_cache, tq):
    if has_cache:
        kb_ref, q_ref, k_ref, vt_ref, kc_ref, vtc_ref, o_ref, qm_ref, m_ref, acc_ref, plain_ref = refs
    else:
        kb_ref, q_ref, k_ref, vt_ref, o_ref, qm_ref, m_ref, acc_ref, plain_ref = refs
    kt = pl.program_id(3)
    col = acc_ref.shape[-1]
    per_head = tq // col
    cols = range(GQA_GROUP * per_head)

    @pl.when(kt == 0)
    def _():
        q_norm2 = None
        for g in range(GQA_GROUP):
            blk = q_ref[0, :, (g // 2) * LANES:(g // 2 + 1) * LANES]
            qm, n2 = _masked_queries(blk, g % 2 == 0)
            qm_ref[g * tq:(g + 1) * tq, :] = qm
            q_norm2 = n2 if q_norm2 is None else jnp.maximum(q_norm2, n2)
        k_norm2 = kb_ref[0]
        if has_cache:
            k_norm2 = jnp.maximum(k_norm2, _dup_key_norm2(kc_ref[0, 0]))
        plain_ref[0] = _needs_no_stabiliser(q_norm2, k_norm2)
        m_ref[...] = jnp.full(m_ref.shape, -jnp.inf, F32)
        acc_ref[...] = jnp.zeros(acc_ref.shape, F32)

    def step(kr, vr):
        @pl.when(plain_ref[0] == 1)
        def _():
            _plain_cols(kr[0, 0], vr[0, 0], qm_ref, acc_ref, cols)

        @pl.when(plain_ref[0] != 1)
        def _():
            _online_cols(kr[0, 0], vr[0, 0], qm_ref, m_ref, acc_ref, cols)

    if has_cache:
        @pl.when(kt < nk_main)
        def _():
            step(k_ref, vt_ref)

        @pl.when(kt >= nk_main)
        def _():
            step(kc_ref, vtc_ref)
    else:
        step(k_ref, vt_ref)

    @pl.when(kt == pl.num_programs(3) - 1)
    def _():
        for part in range(per_head):
            heads = []
            for g in range(GQA_GROUP):
                acc = acc_ref[g * per_head + part]
                heads.append(acc[0:HEAD_DIM, :] / acc[HEAD_DIM:HEAD_DIM + 1, :])
            o_ref[0, part * col:(part + 1) * col, :] = jnp.concatenate(heads, axis=0).T.astype(BF16)


def _kv_specs(n_heads_blk, vt_rows, tk, nk_main, cache):
    if cache:
        k_map = lambda b, h, qi, kt: (b, h, jnp.maximum(kt - nk_main, 0), 0)
        v_map = lambda b, h, qi, kt: (b, h, 0, jnp.maximum(kt - nk_main, 0))
    else:
        k_map = lambda b, h, qi, kt: (b, h, jnp.minimum(kt, nk_main - 1), 0)
        v_map = lambda b, h, qi, kt: (b, h, 0, jnp.minimum(kt, nk_main - 1))
    return [pl.BlockSpec((1, n_heads_blk, tk, LANES), k_map), pl.BlockSpec((1, 1, vt_rows, tk), v_map)]


def _gqa_attention(q, k_bound, kd, vt, kd_c, vt_c, n_batch, n_tok, tq, tk):
    has_cache = kd_c is not None
    assert not has_cache or tk == PAST_LEN
    nk_main = n_tok // tk
    nk = nk_main + (PAST_LEN // tk if has_cache else 0)
    rows = GQA_GROUP * tq
    col = min(MAX_COL, tq)
    in_specs = [pl.BlockSpec(memory_space=pltpu.SMEM),
                pl.BlockSpec((1, tq, 2 * LANES), lambda b, h, qi, kt: (b, qi, h))]
    in_specs += _kv_specs(1, GQA_VT_ROWS, tk, nk_main, False)
    args = [k_bound, q.reshape(n_batch, n_tok, BRANCH_W), kd, vt]
    if has_cache:
        in_specs += _kv_specs(1, GQA_VT_ROWS, tk, nk_main, True)
        args += [kd_c, vt_c]
    out = pl.pallas_call(
        functools.partial(_gqa_kernel, nk_main=nk_main, has_cache=has_cache, tq=tq),
        grid=(n_batch, GQA_KV_HEADS, n_tok // tq, nk),
        in_specs=in_specs,
        out_specs=pl.BlockSpec((1, tq, 2 * LANES), lambda b, h, qi, kt: (b, qi, h)),
        out_shape=jax.ShapeDtypeStruct((n_batch, n_tok, BRANCH_W), BF16),
        scratch_shapes=[
            pltpu.VMEM((rows, LANES), BF16),
            pltpu.VMEM((rows // col, 1, col), F32),
            pltpu.VMEM((rows // col, GQA_VT_ROWS, col), F32),
            pltpu.SMEM((1,), jnp.int32),
        ],
        compiler_params=pltpu.CompilerParams(
            dimension_semantics=("parallel", "parallel", "parallel", "arbitrary"),
            vmem_limit_bytes=VMEM_LIMIT),
        name="gqa_attention",
    )(*args)
    return out.reshape(n_batch * n_tok, BRANCH_W)


def _diff_kernel(*refs, nk_main, has_cache, lam_init):
    if has_cache:
        (kb_ref, q_ref, lam_ref, gsub_ref, k_ref, vt_ref, kc_ref, vtc_ref, o_ref,
         qm_ref, m_ref, acc_ref, plain_ref) = refs
    else:
        kb_ref, q_ref, lam_ref, gsub_ref, k_ref, vt_ref, o_ref, qm_ref, m_ref, acc_ref, plain_ref = refs
    kt = pl.program_id(3)
    tq = q_ref.shape[1]
    col = acc_ref.shape[-1]
    per_map = tq // col

    @pl.when(kt == 0)
    def _():
        blk = q_ref[0]
        q_norm2 = None
        for mm in range(2):
            qm, n2 = _masked_queries(blk, mm == 0)
            qm_ref[mm * tq:(mm + 1) * tq, :] = qm
            q_norm2 = n2 if q_norm2 is None else jnp.maximum(q_norm2, n2)
        k_norm2 = kb_ref[0]
        if has_cache:
            for mm in range(2):
                k_norm2 = jnp.maximum(k_norm2, _dup_key_norm2(kc_ref[0, mm]))
        plain_ref[0] = _needs_no_stabiliser(q_norm2, k_norm2)
        m_ref[...] = jnp.full(m_ref.shape, -jnp.inf, F32)
        acc_ref[...] = jnp.zeros(acc_ref.shape, F32)

    def step(kr, vr):
        @pl.when(plain_ref[0] == 1)
        def _():
            for mm in range(2):
                _plain_cols(kr[0, mm], vr[0, 0], qm_ref, acc_ref, range(mm * per_map, (mm + 1) * per_map))

        @pl.when(plain_ref[0] != 1)
        def _():
            for mm in range(2):
                _online_cols(kr[0, mm], vr[0, 0], qm_ref, m_ref, acc_ref,
                             range(mm * per_map, (mm + 1) * per_map))

    if has_cache:
        @pl.when(kt < nk_main)
        def _():
            step(k_ref, vt_ref)

        @pl.when(kt >= nk_main)
        def _():
            step(kc_ref, vtc_ref)
    else:
        step(k_ref, vt_ref)

    @pl.when(kt == pl.num_programs(3) - 1)
    def _():
        lp = lam_ref[...]
        lam = (jnp.exp(jnp.sum(lp[0:1] * lp[1:2], axis=-1, keepdims=True))
               - jnp.exp(jnp.sum(lp[2:3] * lp[3:4], axis=-1, keepdims=True)) + lam_init)
        for part in range(per_map):
            a0, a1 = acc_ref[part], acc_ref[per_map + part]
            o0 = a0[0:DIFF_DV, :] / a0[DIFF_DV:DIFF_DV + 1, :]
            o1 = a1[0:DIFF_DV, :] / a1[DIFF_DV:DIFF_DV + 1, :]
            d = (o0 - lam * o1).T
            ms = jnp.mean(d * d, axis=-1, keepdims=True)
            o_ref[0, part * col:(part + 1) * col, :] = (
                d * lax.rsqrt(ms + EPS) * gsub_ref[...] * (1.0 - lam_init)).astype(BF16)


def _diff_attention(q, k_bound, lam_p, gsub, kd, vt, kd_c, vt_c, n_batch, n_tok, tq, tk, lam_init):
    has_cache = kd_c is not None
    assert not has_cache or tk == PAST_LEN
    nk_main = n_tok // tk
    nk = nk_main + (PAST_LEN // tk if has_cache else 0)
    col = min(MAX_COL, tq)
    in_specs = [
        pl.BlockSpec(memory_space=pltpu.SMEM),
        pl.BlockSpec((1, tq, LANES), lambda b, h, qi, kt: (b, qi, h)),
        pl.BlockSpec((4, HEAD_DIM), lambda b, h, qi, kt: (0, 0)),
        pl.BlockSpec((1, DIFF_DV), lambda b, h, qi, kt: (0, 0)),
    ]
    in_specs += _kv_specs(2, DIFF_VT_ROWS, tk, nk_main, False)
    args = [k_bound, q.reshape(n_batch, n_tok, BRANCH_W), lam_p, gsub, kd, vt]
    if has_cache:
        in_specs += _kv_specs(2, DIFF_VT_ROWS, tk, nk_main, True)
        args += [kd_c, vt_c]
    out = pl.pallas_call(
        functools.partial(_diff_kernel, nk_main=nk_main, has_cache=has_cache, lam_init=lam_init),
        grid=(n_batch, DIFF_HEADS, n_tok // tq, nk),
        in_specs=in_specs,
        out_specs=pl.BlockSpec((1, tq, LANES), lambda b, h, qi, kt: (b, qi, h)),
        out_shape=jax.ShapeDtypeStruct((n_batch, n_tok, BRANCH_W), BF16),
        scratch_shapes=[
            pltpu.VMEM((2 * tq, LANES), BF16),
            pltpu.VMEM((2 * tq // col, 1, col), F32),
            pltpu.VMEM((2 * tq // col, DIFF_VT_ROWS, col), F32),
            pltpu.SMEM((1,), jnp.int32),
        ],
        compiler_params=pltpu.CompilerParams(
            dimension_semantics=("parallel", "parallel", "parallel", "arbitrary"),
            vmem_limit_bytes=VMEM_LIMIT),
        name="diff_attention",
    )(*args)
    return out.reshape(n_batch * n_tok, BRANCH_W)


def _gla_kernel(*refs, has_s0, n_chunk):
    if has_s0:
        qf, kf, vf, laf, qb, kb, vb, lab, s0_ref, of_ref, ob_ref, sfin_ref, st_ref = refs
    else:
        qf, kf, vf, laf, qb, kb, vb, lab, of_ref, ob_ref, sfin_ref, st_ref = refs
    i = pl.program_id(1)
    ck = GLA_CHUNK

    @pl.when(i == 0)
    def _():
        for d in range(2):
            for hd in range(GLA_HEADS):
                if has_s0:
                    s = s0_ref[0, d, hd]
                    z = jnp.zeros_like(s)
                    padded = jnp.concatenate([s, z] if hd % 2 == 0 else [z, s], axis=0)
                    st_ref[d, hd] = padded.T
                else:
                    st_ref[d, hd] = jnp.zeros((GLA_DV, LANES), F32)

    tb = qf.shape[0]
    r = lax.broadcasted_iota(jnp.int32, (tb, tb), 0)
    c = lax.broadcasted_iota(jnp.int32, (tb, tb), 1)
    lane = lax.broadcasted_iota(jnp.int32, (tb, LANES), 1)
    chunk_id = lax.broadcasted_iota(jnp.int32, (tb, GLA_W), 0) // ck
    zero_row = jnp.zeros((1, GLA_W), F32)
    streams = ((qf, kf, vf, laf, of_ref, c <= r), (qb, kb, vb, lab, ob_ref, c >= r))
    for d, (q_r, k_r, v_r, la_r, o_r, tri) in enumerate(streams):
        tri_b = tri.astype(F32).astype(BF16)
        g = sum(jnp.dot(tri_b, part, preferred_element_type=F32) for part in _bf16_terms(la_r[...]))
        if d == 0:
            bounds = [zero_row] + [g[ck * j - 1:ck * j, :] for j in range(1, n_chunk)]
            g_end = g[tb - 1:tb, :]
        else:
            bounds = [g[ck * (j + 1):ck * (j + 1) + 1, :] for j in range(n_chunk - 1)] + [zero_row]
            g_end = g[0:1, :]
        b_rows = jnp.concatenate([jnp.broadcast_to(b, (ck, GLA_W)) for b in bounds], axis=0)
        q = q_r[...].astype(F32)
        k = k_r[...].astype(F32)
        v = v_r[...]
        q_dec = q * jnp.exp(g - b_rows)
        q_glob = q * jnp.exp(g)
        k_end = k * jnp.exp(g_end - g)
        decay = jnp.exp(g_end)
        k_rel = []
        for j in range(n_chunk):
            reach = (chunk_id <= j) if d == 0 else (chunk_id >= j)
            k_rel.append(k * jnp.exp(jnp.where(reach, bounds[j] - g, 0.0)))
        for hd in range(GLA_HEADS):
            pair = slice((hd // 2) * LANES, (hd // 2 + 1) * LANES)
            keep = (lane < GLA_DK) if hd % 2 == 0 else (lane >= GLA_DK)
            qd_m = jnp.where(keep, q_dec[:, pair], 0.0).astype(BF16)
            qg_m = jnp.where(keep, q_glob[:, pair], 0.0).astype(BF16)
            ke_m = jnp.where(keep, k_end[:, pair], 0.0).astype(BF16)
            a_rows = [
                lax.dot_general(qd_m[ck * j:ck * (j + 1), :], k_rel[j][:, pair].astype(BF16),
                                (((1,), (1,)), ((), ())), preferred_element_type=F32)
                for j in range(n_chunk)
            ]
            a = jnp.where(tri, jnp.concatenate(a_rows, axis=0), 0.0).astype(BF16)
            hv = slice(hd * GLA_DV, (hd + 1) * GLA_DV)
            s_t = st_ref[d, hd]
            o = jnp.dot(a, v[:, hv], preferred_element_type=F32)
            o = o + lax.dot_general(qg_m, s_t.astype(BF16), (((1,), (1,)), ((), ())),
                                    preferred_element_type=F32)
            o_r[:, hv] = o
            ds_t = lax.dot_general(v[:, hv], ke_m, (((0,), (0,)), ((), ())), preferred_element_type=F32)
            st_ref[d, hd] = s_t * decay[:, pair] + ds_t

    @pl.when(i == pl.num_programs(1) - 1)
    def _():
        for d in range(2):
            for hd in range(GLA_HEADS):
                t = st_ref[d, hd].T
                sfin_ref[0, d, hd] = t[(hd % 2) * GLA_DK:(hd % 2 + 1) * GLA_DK, :]


def _gla(lq, lk, lv, la, s0, n_batch, n_tok, tb):
    nb = n_tok // tb
    t_total = n_batch * n_tok
    has_s0 = s0 is not None
    fwd = lambda b, i: (b * nb + i, 0)
    bwd = lambda b, i: (b * nb + nb - 1 - i, 0)
    bwd_la = lambda b, i: (b * nb + nb - 1 - i, 1)
    in_specs = [
        pl.BlockSpec((tb, GLA_W), fwd), pl.BlockSpec((tb, GLA_W), fwd),
        pl.BlockSpec((tb, BRANCH_W), fwd), pl.BlockSpec((tb, GLA_W), fwd),
        pl.BlockSpec((tb, GLA_W), bwd), pl.BlockSpec((tb, GLA_W), bwd),
        pl.BlockSpec((tb, BRANCH_W), bwd), pl.BlockSpec((tb, GLA_W), bwd_la),
    ]
    args = [lq, lk, lv, la, lq, lk, lv, la]
    state_spec = pl.BlockSpec((1, 2, GLA_HEADS, GLA_DK, GLA_DV), lambda b, i: (b, 0, 0, 0, 0))
    if has_s0:
        in_specs.append(state_spec)
        args.append(s0)
    return pl.pallas_call(
        functools.partial(_gla_kernel, has_s0=has_s0, n_chunk=tb // GLA_CHUNK),
        grid=(n_batch, nb),
        in_specs=in_specs,
        out_specs=[pl.BlockSpec((tb, BRANCH_W), fwd), pl.BlockSpec((tb, BRANCH_W), bwd), state_spec],
        out_shape=[
            jax.ShapeDtypeStruct((t_total, BRANCH_W), F32),
            jax.ShapeDtypeStruct((t_total, BRANCH_W), F32),
            jax.ShapeDtypeStruct((n_batch, 2, GLA_HEADS, GLA_DK, GLA_DV), F32),
        ],
        scratch_shapes=[pltpu.VMEM((2, GLA_HEADS, GLA_DV, LANES), F32)],
        compiler_params=pltpu.CompilerParams(
            dimension_semantics=("parallel", "arbitrary"), vmem_limit_bytes=VMEM_LIMIT),
        name="gla",
    )(*args)


def _merge_kernel(x_ref, mod_ref, og_ref, of_ref, ob_ref, lgo_ref, od_ref, gate_ref,
                  gout_ref, wb_ref, wo_ref, y_ref):
    o_gla = of_ref[...] + ob_ref[...]
    gla_parts = []
    for hd in range(GLA_HEADS):
        blk = o_gla[:, hd * GLA_DV:(hd + 1) * GLA_DV]
        ms = jnp.mean(blk * blk, axis=-1, keepdims=True)
        gla_parts.append(blk * lax.rsqrt(ms + EPS) * gout_ref[...])
    gla = jnp.concatenate(gla_parts, axis=1) * _silu(lgo_ref[...].astype(F32))
    branches = (og_ref[...], gla.astype(BF16), od_ref[...])
    mixed = None
    for b, ob in enumerate(branches):
        y = jnp.dot(ob, wb_ref[b], preferred_element_type=F32)
        y = y * gate_ref[:, b * D_MODEL:(b + 1) * D_MODEL].astype(F32)
        mixed = y if mixed is None else mixed + y
    out = jnp.dot(mixed.astype(BF16), wo_ref[...], preferred_element_type=F32)
    y_ref[...] = x_ref[...] + mod_ref[0, 2:3, :] * out


def _merge(x2d, mod, og, o_f, o_b, lgo, od, gates, lw, tiles_per_mod, tm):
    t_total = x2d.shape[0]
    tile = lambda c: pl.BlockSpec((tm, c), lambda i: (i, 0))
    return pl.pallas_call(
        _merge_kernel,
        grid=(t_total // tm,),
        in_specs=[
            tile(D_MODEL),
            pl.BlockSpec((1, 6, D_MODEL), lambda i: (i // tiles_per_mod, 0, 0)),
            tile(BRANCH_W), tile(BRANCH_W), tile(BRANCH_W), tile(BRANCH_W), tile(BRANCH_W),
            tile(N_BRANCH * D_MODEL),
            _full_spec((1, GLA_DV)),
            _full_spec((N_BRANCH, BRANCH_W, D_MODEL)),
            _full_spec((D_MODEL, D_MODEL)),
        ],
        out_specs=tile(D_MODEL),
        out_shape=jax.ShapeDtypeStruct((t_total, D_MODEL), F32),
        compiler_params=pltpu.CompilerParams(
            dimension_semantics=("parallel",), vmem_limit_bytes=VMEM_LIMIT),
        name="merge",
    )(x2d, mod, og, o_f, o_b, lgo, od, gates, lw["gout"], lw["w_branch"], lw["w_out"])


FFN_CHUNK = 256


def _ffn_kernel(x_ref, mod_ref, g2_ref, wi_ref, wo_ref, y_ref):
    x = x_ref[...]
    ms = jnp.mean(x * x, axis=-1, keepdims=True)
    h = x * lax.rsqrt(ms + EPS) * g2_ref[...]
    h = h * (1.0 + mod_ref[0, 4:5, :]) + mod_ref[0, 3:4, :]
    hb = h.astype(BF16)
    acc = None
    for c0 in range(0, FFN_HIDDEN, FFN_CHUNK):
        a = jnp.dot(hb, wi_ref[:, c0:c0 + FFN_CHUNK], preferred_element_type=F32)
        u = jnp.dot(hb, wi_ref[:, FFN_HIDDEN + c0:FFN_HIDDEN + c0 + FFN_CHUNK], preferred_element_type=F32)
        act = (_silu(a) * u).astype(BF16)
        part = jnp.dot(act, wo_ref[c0:c0 + FFN_CHUNK, :], preferred_element_type=F32)
        acc = part if acc is None else acc + part
    y_ref[...] = x + mod_ref[0, 5:6, :] * acc


def _ffn(x2d, mod, lw, tiles_per_mod, tm):
    t_total = x2d.shape[0]
    tile = lambda c: pl.BlockSpec((tm, c), lambda i: (i, 0))
    return pl.pallas_call(
        _ffn_kernel,
        grid=(t_total // tm,),
        in_specs=[
            tile(D_MODEL),
            pl.BlockSpec((1, 6, D_MODEL), lambda i: (i // tiles_per_mod, 0, 0)),
            _full_spec((1, D_MODEL)),
            _full_spec((D_MODEL, 2 * FFN_HIDDEN)),
            _full_spec((FFN_HIDDEN, D_MODEL)),
        ],
        out_specs=tile(D_MODEL),
        out_shape=jax.ShapeDtypeStruct((t_total, D_MODEL), F32),
        compiler_params=pltpu.CompilerParams(
            dimension_semantics=("parallel",), vmem_limit_bytes=VMEM_LIMIT),
        name="ffn",
    )(x2d, mod, lw["norm2"], lw["w_ffn_in"], lw["w_ffn_out"])


def _layer_weights(l, p):
    w = p["w_in"][l]
    s = (GQA_HEADS * HEAD_DIM, GQA_KV_HEADS * HEAD_DIM, GQA_KV_HEADS * HEAD_DIM,
         GLA_HEADS * GLA_DK, GLA_HEADS * GLA_DK, GLA_HEADS * GLA_DV, 2 * GLA_RANK, GLA_HEADS * GLA_DV,
         DIFF_HEADS * 2 * HEAD_DIM, DIFF_HEADS * 2 * HEAD_DIM, DIFF_HEADS * DIFF_DV, N_BRANCH * D_MODEL)
    offs = [0]
    for n in s:
        offs.append(offs[-1] + n)
    col = lambda j: w[:, offs[j]:offs[j + 1]].astype(BF16)
    gq, gk, gv, lq, lk, lv, lr, lgo, dq, dk, dv, mg = (col(j) for j in range(12))
    lr_pad = jnp.zeros((D_MODEL, LANES - 2 * GLA_RANK), BF16)
    w_in = jnp.concatenate([gq, dq, gk, dk, gv, dv, lq, lk, lv, lgo, lr, lr_pad, mg], axis=1)
    scale = HEAD_DIM ** -0.5 * math.log2(math.e)
    key_bound = lambda g: (HEAD_DIM * K_NORM_MARGIN ** 2 * jnp.max(g * g)).reshape(1)
    gq_row = jnp.concatenate([jnp.tile(p["gqa_q_norm"][l], GQA_HEADS),
                              jnp.tile(p["diff_q_norm"][l], 2 * DIFF_HEADS)]) * scale
    gk_row = jnp.concatenate([jnp.tile(p["gqa_k_norm"][l], GQA_KV_HEADS),
                              jnp.tile(p["diff_k_norm"][l], 2 * DIFF_HEADS)])
    idx = jnp.arange(2 * LANES) // HEAD_DIM
    gmat = jnp.where(idx[:, None] == idx[None, :], 1.0 / HEAD_DIM, 0.0).astype(BF16)
    up = jnp.zeros((LANES, 2 * GLA_W), F32)
    up = up.at[0:GLA_RANK, 0:GLA_W].set(p["gla_alpha_up"][l, 0])
    up = up.at[GLA_RANK:2 * GLA_RANK, GLA_W:].set(p["gla_alpha_up"][l, 1])
    ub = p["gla_alpha_bias"][l].reshape(1, 2 * GLA_W)
    up_hi = up.astype(BF16)
    up = jnp.stack([up_hi, (up - up_hi.astype(F32)).astype(BF16)])
    return {
        "w_in": w_in, "gq": gq_row.reshape(1, N_Q), "gk": gk_row.reshape(1, N_K), "gmat": gmat,
        "up": up, "ub": ub,
        "kb_g": key_bound(p["gqa_k_norm"][l]), "kb_d": key_bound(p["diff_k_norm"][l]),
        "norm1": p["norm1"][l].reshape(1, D_MODEL), "norm2": p["norm2"][l].reshape(1, D_MODEL),
        "gout": p["gla_out_norm"][l].reshape(1, GLA_DV),
        "gsub": p["diff_sub_norm"][l].reshape(1, DIFF_DV),
        "lam": p["diff_lambda"][l],
        "w_branch": p["w_branch"][l].astype(BF16), "w_out": p["w_out"][l].astype(BF16),
        "w_ffn_in": p["w_ffn_in"][l].astype(BF16), "w_ffn_out": p["w_ffn_out"][l].astype(BF16),
    }


def _rope_tables(n_tokens):
    n_rows = n_tokens // GRID_W
    row = jnp.repeat(jnp.arange(n_rows, dtype=F32), GRID_W)
    col = jnp.tile(jnp.arange(GRID_W, dtype=F32), n_rows)
    n_freq = HEAD_DIM // 4
    freqs = ROPE_THETA ** (-jnp.arange(n_freq, dtype=F32) / n_freq)
    ar, ac = row[:, None] * freqs, col[:, None] * freqs
    cos = jnp.concatenate([jnp.cos(ar), jnp.cos(ar), jnp.cos(ac), jnp.cos(ac)], axis=1)
    sin = jnp.concatenate([-jnp.sin(ar), jnp.sin(ar), -jnp.sin(ac), jnp.sin(ac)], axis=1)
    return jnp.tile(cos, (1, LANES // HEAD_DIM)), jnp.tile(sin, (1, LANES // HEAD_DIM))


def _with_ones_rows(v_t):
    lead = v_t.shape[:-2]
    s = v_t.shape[-1]
    ones = jnp.ones(lead + (1, s), v_t.dtype)
    zeros = jnp.zeros(lead + (BF16_ROWS - 1, s), v_t.dtype)
    return jnp.concatenate([v_t, ones, zeros], axis=-2).astype(BF16)


def _cache_layouts(l, cache_gqa_k, cache_gqa_v, cache_diff_k, cache_diff_v):
    b = cache_gqa_k.shape[0]
    gk = jnp.transpose(cache_gqa_k[:, l], (0, 2, 1, 3))
    kd_g = jnp.concatenate([gk, gk], axis=-1).astype(BF16)
    vt_g = _with_ones_rows(jnp.transpose(cache_gqa_v[:, l], (0, 2, 3, 1)))
    dk = jnp.transpose(cache_diff_k[:, l], (0, 2, 3, 1, 4)).reshape(b, 2 * DIFF_HEADS, PAST_LEN, HEAD_DIM)
    kd_d = jnp.concatenate([dk, dk], axis=-1).astype(BF16)
    vt_d = _with_ones_rows(jnp.transpose(cache_diff_v[:, l], (0, 2, 3, 1)))
    return kd_g, vt_g, kd_d, vt_d


def _run_layer(x2d, mod, lw, l, n_batch, n_tok, rope_tabs, ctx, emit_cache, cfg):
    tiles_per_mod = lambda tm: (n_tok // tm) if mod.shape[0] > 1 else (n_batch * n_tok // tm)
    outs = _in_projection(x2d, mod, lw, n_batch, n_tok, rope_tabs, emit_cache, cfg["tm_in"])
    qg, qd, kdg, kdd, vtg, vtd, lq, lk, lv, lgo, la, gates = outs[:12]
    if ctx is None:
        kd_gc = vt_gc = kd_dc = vt_dc = s0 = None
    else:
        kd_gc, vt_gc, kd_dc, vt_dc, s0 = ctx
    og = _gqa_attention(qg, lw["kb_g"], kdg, vtg, kd_gc, vt_gc, n_batch, n_tok, cfg["tq_gqa"], cfg["tk"])
    lam_init = 0.8 - 0.6 * math.exp(-0.3 * l)
    od = _diff_attention(qd, lw["kb_d"], lw["lam"], lw["gsub"], kdd, vtd, kd_dc, vt_dc, n_batch, n_tok,
                         cfg["tq_diff"], cfg["tk"], lam_init)
    o_f, o_b, s_fin = _gla(lq, lk, lv, la, s0, n_batch, n_tok, cfg["tb_gla"])
    x2d = _merge(x2d, mod, og, o_f, o_b, lgo, od, gates, lw, tiles_per_mod(cfg["tm"]), cfg["tm"])
    x2d = _ffn(x2d, mod, lw, tiles_per_mod(cfg["tm"]), cfg["tm"])
    cache = tuple(outs[12:]) + (s_fin,) if emit_cache else None
    return x2d, cache


PROMPT_CFG = dict(tm_in=256, tq_gqa=256, tq_diff=256, tk=256, tb_gla=256, tm=512)
SAMPLE_CFG = dict(tm_in=256, tq_gqa=1024, tq_diff=2048, tk=512, tb_gla=256, tm=512)


def kernel(x_prompt, x_sample, c, cache_gqa_k, cache_gqa_v, state_gla, cache_diff_k, cache_diff_v, c_ctx, w_mod, b_mod, norm1, norm2, w_in, gqa_q_norm, gqa_k_norm, gla_alpha_up, gla_alpha_bias, gla_out_norm, diff_q_norm, diff_k_norm, diff_lambda, diff_sub_norm, w_branch, w_out, w_ffn_in, w_ffn_out):
    p = {
        "norm1": norm1, "norm2": norm2, "w_in": w_in, "gqa_q_norm": gqa_q_norm, "gqa_k_norm": gqa_k_norm,
        "gla_alpha_up": gla_alpha_up, "gla_alpha_bias": gla_alpha_bias, "gla_out_norm": gla_out_norm,
        "diff_q_norm": diff_q_norm, "diff_k_norm": diff_k_norm, "diff_lambda": diff_lambda,
        "diff_sub_norm": diff_sub_norm, "w_branch": w_branch, "w_out": w_out,
        "w_ffn_in": w_ffn_in, "w_ffn_out": w_ffn_out,
    }
    n_ctx_b, n_ctx = x_prompt.shape[:2]
    n_lat_b, n_lat = x_sample.shape[:2]
    cond_rows = jnp.concatenate(
        [c_ctx[None, :], c, jnp.zeros((8 - 1 - n_lat_b, D_MODEL), F32)], axis=0)
    mod_all = _modulation(cond_rows, w_mod, b_mod)
    weights = [_layer_weights(l, p) for l in range(DEPTH)]

    y = x_prompt.reshape(n_ctx_b * n_ctx, D_MODEL)
    caches = []
    for l in range(DEPTH):
        mod = mod_all[l, 0:1].reshape(1, 6, D_MODEL)
        y, cache = _run_layer(y, mod, weights[l], l, n_ctx_b, n_ctx, None, None, True, PROMPT_CFG)
        caches.append(cache)
    y_prompt = y.reshape(n_ctx_b, n_ctx, D_MODEL)
    stack = lambda j, shape: jnp.stack([cc[j].reshape(shape) for cc in caches], axis=1)
    new_gqa_k = stack(0, (n_ctx_b, n_ctx, GQA_KV_HEADS, HEAD_DIM))
    new_diff_k = stack(1, (n_ctx_b, n_ctx, DIFF_HEADS, 2, HEAD_DIM))
    new_gqa_v = stack(2, (n_ctx_b, n_ctx, GQA_KV_HEADS, HEAD_DIM))
    new_diff_v = stack(3, (n_ctx_b, n_ctx, DIFF_HEADS, DIFF_DV))
    new_state_gla = jnp.stack([cc[4] for cc in caches], axis=1)

    rope_tabs = _rope_tables(n_lat)
    y = x_sample.reshape(n_lat_b * n_lat, D_MODEL)
    for l in range(DEPTH):
        mod = mod_all[l, 1:1 + n_lat_b].reshape(n_lat_b, 6, D_MODEL)
        ctx = _cache_layouts(l, cache_gqa_k, cache_gqa_v, cache_diff_k, cache_diff_v) + (state_gla[:, l],)
        y, _ = _run_layer(y, mod, weights[l], l, n_lat_b, n_lat, rope_tabs, ctx, False, SAMPLE_CFG)
    y_sample = y.reshape(n_lat_b, n_lat, D_MODEL)
    return (y_prompt, y_sample, new_gqa_k, new_gqa_v, new_state_gla, new_diff_k, new_diff_v)
```

```python
import functools
import math

import jax
import jax.numpy as jnp
from jax import lax
from jax.experimental import pallas as pl
from jax.experimental.pallas import tpu as pltpu

D_MODEL = 1024
DEPTH = 2
PAST_LEN = 512
GRID_W = 64
HEAD_DIM = 64
GQA_HEADS = 8
GQA_KV_HEADS = 2
GQA_GROUP = GQA_HEADS // GQA_KV_HEADS
GLA_HEADS = 4
GLA_DK = 64
GLA_DV = 128
GLA_RANK = 16
GLA_TAU = 16.0
GLA_CHUNK = 64
DIFF_HEADS = 4
DIFF_DV = 2 * HEAD_DIM
N_BRANCH = 3
BRANCH_W = 512
FFN_HIDDEN = ((8 * D_MODEL + 3 * 256 - 1) // (3 * 256)) * 256
ROPE_THETA = 10000.0
EPS = 1e-6

LANES = 128
BF16_ROWS = 16
VMEM_LIMIT = 56 * 1024 * 1024

F32 = jnp.float32
BF16 = jnp.bfloat16
HIGHEST = lax.Precision.HIGHEST

C_GQ = 0
C_DQ = C_GQ + GQA_HEADS * HEAD_DIM
C_GK = C_DQ + DIFF_HEADS * 2 * HEAD_DIM
C_DK = C_GK + GQA_KV_HEADS * HEAD_DIM
C_GV = C_DK + DIFF_HEADS * 2 * HEAD_DIM
C_DV = C_GV + GQA_KV_HEADS * HEAD_DIM
C_LQ = C_DV + DIFF_HEADS * DIFF_DV
C_LK = C_LQ + GLA_HEADS * GLA_DK
C_LV = C_LK + GLA_HEADS * GLA_DK
C_LGO = C_LV + GLA_HEADS * GLA_DV
C_LR = C_LGO + GLA_HEADS * GLA_DV
C_MG = C_LR + LANES
N_IN_PAD = C_MG + N_BRANCH * D_MODEL
N_Q = C_GK - C_GQ
N_K = C_GV - C_GK
N_V = C_LQ - C_GV
GLA_W = GLA_HEADS * GLA_DK
GQA_VT_ROWS = HEAD_DIM + BF16_ROWS
DIFF_VT_ROWS = DIFF_DV + BF16_ROWS


def _sigmoid(x):
    return 1.0 / (1.0 + jnp.exp(-x))


def _silu(x):
    return x * _sigmoid(x)


def _bf16_terms(x):
    hi = x.astype(BF16)
    r1 = x - hi.astype(F32)
    mid = r1.astype(BF16)
    lo = (r1 - mid.astype(F32)).astype(BF16)
    return hi, mid, lo


def _full_spec(shape):
    n = len(shape)
    return pl.BlockSpec(shape, lambda *_: (0,) * n, pipeline_mode=pl.Buffered(1))


def _mod_kernel(cond_ref, w_ref, b_ref, o_ref):
    s = _silu(cond_ref[...])
    o_ref[0] = jnp.dot(s, w_ref[0], precision=HIGHEST, preferred_element_type=F32) + b_ref[0]


def _modulation(cond_rows, w_mod, b_mod):
    tn = 1536
    n_out = 6 * D_MODEL
    return pl.pallas_call(
        _mod_kernel,
        grid=(DEPTH, n_out // tn),
        in_specs=[
            pl.BlockSpec((8, D_MODEL), lambda l, j: (0, 0)),
            pl.BlockSpec((1, D_MODEL, tn), lambda l, j: (l, 0, j)),
            pl.BlockSpec((1, 1, tn), lambda l, j: (l, 0, j)),
        ],
        out_specs=pl.BlockSpec((1, 8, tn), lambda l, j: (l, 0, j)),
        out_shape=jax.ShapeDtypeStruct((DEPTH, 8, n_out), F32),
        compiler_params=pltpu.CompilerParams(
            dimension_semantics=("parallel", "parallel"), vmem_limit_bytes=VMEM_LIMIT),
        name="modulation",
    )(cond_rows, w_mod, b_mod.reshape(DEPTH, 1, n_out))


def _head_norm(z, gain, gmat):
    outs = []
    n = z.shape[1]
    for c0 in range(0, n, 2 * LANES):
        w = min(2 * LANES, n - c0)
        zz = z[:, c0:c0 + w]
        ms = jnp.dot((zz * zz).astype(BF16), gmat[0:w, 0:w], preferred_element_type=F32)
        outs.append(zz * lax.rsqrt(ms + EPS) * gain[:, c0:c0 + w])
    return outs


def _rope_block(zb, cos, sin, first):
    partner = jnp.where(first, pltpu.roll(zb, LANES - 16, 1), pltpu.roll(zb, 16, 1))
    return zb * cos + partner * sin


def _in_kernel(*refs, rope, emit_cache):
    it = iter(refs)
    x_ref, mod_ref, g1_ref, w_ref, gq_ref, gk_ref, gmat_ref, up_ref, ub_ref = (next(it) for _ in range(9))
    if rope:
        cos_ref, sin_ref = next(it), next(it)
    (qg_ref, qd_ref, kdg_ref, kdd_ref, vtg_ref, vtd_ref,
     lq_ref, lk_ref, lv_ref, lgo_ref, la_ref, gate_ref) = (next(it) for _ in range(12))
    if emit_cache:
        ck_g_ref, ck_d_ref, cv_g_ref, cv_d_ref = (next(it) for _ in range(4))

    tm = x_ref.shape[0]
    x = x_ref[...]
    ms = jnp.mean(x * x, axis=-1, keepdims=True)
    h = x * lax.rsqrt(ms + EPS) * g1_ref[...]
    h = h * (1.0 + mod_ref[0, 1:2, :]) + mod_ref[0, 0:1, :]
    hb = h.astype(BF16)

    def proj(c0, n):
        return jnp.dot(hb, w_ref[:, c0:c0 + n], preferred_element_type=F32)

    lane = lax.broadcasted_iota(jnp.int32, (tm, LANES), 1)
    first16 = (lane % 32) < 16
    lo_half = lane < HEAD_DIM
    gmat = gmat_ref[...]
    if rope:
        cos, sin = cos_ref[...], sin_ref[...]

    def blocks128(parts):
        out = []
        for p in parts:
            for c in range(0, p.shape[1], LANES):
                out.append(p[:, c:c + LANES])
        return out

    qb = blocks128(_head_norm(proj(C_GQ, N_Q), gq_ref[...], gmat))
    for j, zb in enumerate(qb):
        if rope:
            zb = _rope_block(zb, cos, sin, first16)
        dst = qg_ref if j < 4 else qd_ref
        c = (j % 4) * LANES
        dst[:, c:c + LANES] = zb.astype(BF16)

    kb = blocks128(_head_norm(proj(C_GK, N_K), gk_ref[...], gmat))
    for j, zb in enumerate(kb):
        if rope:
            zb = _rope_block(zb, cos, sin, first16)
        if emit_cache:
            if j == 0:
                ck_g_ref[...] = zb
            else:
                ck_d_ref[:, (j - 1) * LANES:j * LANES] = zb
        sw = pltpu.roll(zb, HEAD_DIM, 1)
        dup_lo = jnp.where(lo_half, zb, sw).astype(BF16)
        dup_hi = jnp.where(lo_half, sw, zb).astype(BF16)
        if j == 0:
            kdg_ref[0, 0] = dup_lo
            kdg_ref[0, 1] = dup_hi
        else:
            kdd_ref[0, 2 * (j - 1)] = dup_lo
            kdd_ref[0, 2 * (j - 1) + 1] = dup_hi

    ones_rows = jnp.where(lax.broadcasted_iota(jnp.int32, (BF16_ROWS, tm), 0) == 0, 1.0, 0.0).astype(BF16)
    zv = proj(C_GV, N_V)
    if emit_cache:
        cv_g_ref[...] = zv[:, 0:LANES]
        cv_d_ref[...] = zv[:, LANES:]
    vt = zv[:, 0:LANES].T
    for hh in range(GQA_KV_HEADS):
        vtg_ref[0, hh, 0:HEAD_DIM, :] = vt[hh * HEAD_DIM:(hh + 1) * HEAD_DIM, :].astype(BF16)
        vtg_ref[0, hh, HEAD_DIM:GQA_VT_ROWS, :] = ones_rows
    for hh in range(DIFF_HEADS):
        vt = zv[:, (hh + 1) * LANES:(hh + 2) * LANES].T
        vtd_ref[0, hh, 0:DIFF_DV, :] = vt.astype(BF16)
        vtd_ref[0, hh, DIFF_DV:DIFF_VT_ROWS, :] = ones_rows

    zl = proj(C_LQ, C_MG - C_LQ)
    o = 0
    lq_ref[...] = (zl[:, o:o + GLA_W] * (GLA_DK ** -0.5)).astype(BF16)
    o += GLA_W
    lk_ref[...] = zl[:, o:o + GLA_W].astype(BF16)
    o += GLA_W
    lv_ref[...] = zl[:, o:o + BRANCH_W].astype(BF16)
    o += BRANCH_W
    lgo_ref[...] = zl[:, o:o + BRANCH_W].astype(BF16)
    o += BRANCH_W
    lr = zl[:, o:o + LANES]
    lr_hi, lr_mid, _ = _bf16_terms(lr)
    z = (jnp.dot(lr_hi, up_ref[0], preferred_element_type=F32)
         + jnp.dot(lr_mid, up_ref[0], preferred_element_type=F32)
         + jnp.dot(lr_hi, up_ref[1], preferred_element_type=F32)) + ub_ref[...]
    log_sig = jnp.minimum(z, 0.0) - jnp.log(1.0 + jnp.exp(-jnp.abs(z)))
    la_ref[...] = log_sig * (1.0 / GLA_TAU)

    for b in range(N_BRANCH):
        gate_ref[:, b * D_MODEL:(b + 1) * D_MODEL] = _sigmoid(proj(C_MG + b * D_MODEL, D_MODEL)).astype(BF16)


def _in_projection(x2d, mod, lw, n_batch, n_tok, rope_tabs, emit_cache, tm):
    t_total = n_batch * n_tok
    tpb = n_tok // tm
    rope = rope_tabs is not None
    tile = lambda c: pl.BlockSpec((tm, c), lambda i: (i, 0))
    in_specs = [
        tile(D_MODEL),
        pl.BlockSpec((1, 6, D_MODEL), lambda i: (i // tpb if mod.shape[0] > 1 else 0, 0, 0)),
        _full_spec((1, D_MODEL)),
        _full_spec((D_MODEL, N_IN_PAD)),
        _full_spec((1, N_Q)),
        _full_spec((1, N_K)),
        _full_spec((2 * LANES, 2 * LANES)),
        _full_spec((2, LANES, 2 * GLA_W)),
        _full_spec((1, 2 * GLA_W)),
    ]
    args = [x2d, mod, lw["norm1"], lw["w_in"], lw["gq"], lw["gk"], lw["gmat"], lw["up"], lw["ub"]]
    if rope:
        in_specs += [pl.BlockSpec((tm, LANES), lambda i: (i % tpb, 0))] * 2
        args += list(rope_tabs)
    sd = jax.ShapeDtypeStruct
    out_shape = [
        sd((t_total, BRANCH_W), BF16), sd((t_total, BRANCH_W), BF16),
        sd((n_batch, GQA_KV_HEADS, n_tok, LANES), BF16), sd((n_batch, 2 * DIFF_HEADS, n_tok, LANES), BF16),
        sd((n_batch, GQA_KV_HEADS, GQA_VT_ROWS, n_tok), BF16), sd((n_batch, DIFF_HEADS, DIFF_VT_ROWS, n_tok), BF16),
        sd((t_total, GLA_W), BF16), sd((t_total, GLA_W), BF16),
        sd((t_total, BRANCH_W), BF16), sd((t_total, BRANCH_W), BF16),
        sd((t_total, 2 * GLA_W), F32), sd((t_total, N_BRANCH * D_MODEL), BF16),
    ]
    out_specs = [
        tile(BRANCH_W), tile(BRANCH_W),
        pl.BlockSpec((1, GQA_KV_HEADS, tm, LANES), lambda i: (i // tpb, 0, i % tpb, 0)),
        pl.BlockSpec((1, 2 * DIFF_HEADS, tm, LANES), lambda i: (i // tpb, 0, i % tpb, 0)),
        pl.BlockSpec((1, GQA_KV_HEADS, GQA_VT_ROWS, tm), lambda i: (i // tpb, 0, 0, i % tpb)),
        pl.BlockSpec((1, DIFF_HEADS, DIFF_VT_ROWS, tm), lambda i: (i // tpb, 0, 0, i % tpb)),
        tile(GLA_W), tile(GLA_W), tile(BRANCH_W), tile(BRANCH_W), tile(2 * GLA_W), tile(N_BRANCH * D_MODEL),
    ]
    if emit_cache:
        out_shape += [sd((t_total, LANES), F32), sd((t_total, BRANCH_W), F32),
                      sd((t_total, LANES), F32), sd((t_total, BRANCH_W), F32)]
        out_specs += [tile(LANES), tile(BRANCH_W), tile(LANES), tile(BRANCH_W)]
    return pl.pallas_call(
        functools.partial(_in_kernel, rope=rope, emit_cache=emit_cache),
        grid=(t_total // tm,),
        in_specs=in_specs,
        out_specs=out_specs,
        out_shape=out_shape,
        compiler_params=pltpu.CompilerParams(
            dimension_semantics=("parallel",), vmem_limit_bytes=VMEM_LIMIT),
        name="in_projection",
    )(*args)


MAX_COL = 4 * LANES
NT_DIMS = (((1,), (1,)), ((), ()))
SCORE_BOUND = 45.0
NORM_MARGIN = 1.05


def _masked_queries(q_blk, keep_low):
    lane = lax.broadcasted_iota(jnp.int32, q_blk.shape, 1)
    keep = (lane < HEAD_DIM) if keep_low else (lane >= HEAD_DIM)
    return jnp.where(keep, q_blk.astype(F32), 0.0).astype(BF16)


def _dup_key_norm2(kd):
    kf = kd.astype(F32)
    return 0.5 * jnp.max(jnp.sum(kf * kf, axis=1, keepdims=True))


def _needs_no_stabiliser(q_norm2, k_norm2):
    return (q_norm2 * k_norm2 <= SCORE_BOUND * SCORE_BOUND).astype(jnp.int32)


def _plain_cols(k, vt, qm_ref, acc_ref, cols):
    col = acc_ref.shape[-1]
    cols = list(cols)
    scores = lambda c: lax.dot_general(k, qm_ref[c * col:(c + 1) * col, :], NT_DIMS,
                                       preferred_element_type=F32)
    s_next = scores(cols[0])
    for i, c in enumerate(cols):
        s = s_next
        if i + 1 < len(cols):
            s_next = scores(cols[i + 1])
        p = jnp.exp2(s).astype(BF16)
        acc_ref[c] += jnp.dot(vt, p, preferred_element_type=F32)


def _online_cols(k, vt, qm_ref, m_ref, acc_ref, cols):
    col = acc_ref.shape[-1]
    for c in cols:
        s = lax.dot_general(k, qm_ref[c * col:(c + 1) * col, :], NT_DIMS, preferred_element_type=F32)
        m_prev = m_ref[c]
        m_new = jnp.maximum(m_prev, jnp.max(s, axis=0, keepdims=True))
        alpha = jnp.exp2(m_prev - m_new)
        p = jnp.exp2(s - m_new).astype(BF16)
        acc_ref[c] = acc_ref[c] * alpha + jnp.dot(vt, p, preferred_element_type=F32)
        m_ref[c] = m_new


def _gqa_kernel(*refs, nk_main, has_cache, tq):
    if has_cache:
        kb_ref, q_ref, k_ref, vt_ref, kc_ref, vtc_ref, o_ref, qm_ref, m_ref, acc_ref, plain_ref = refs
    else:
        kb_ref, q_ref, k_ref, vt_ref, o_ref, qm_ref, m_ref, acc_ref, plain_ref = refs
    kt = pl.program_id(3)
    col = acc_ref.shape[-1]
    per_head = tq // col
    cols = range(GQA_GROUP * per_head)

    @pl.when(kt == 0)
    def _():
        for g in range(GQA_GROUP):
            blk = q_ref[0, :, (g // 2) * LANES:(g // 2 + 1) * LANES]
            qm_ref[g * tq:(g + 1) * tq, :] = _masked_queries(blk, g % 2 == 0)
        k_norm2 = kb_ref[1]
        if has_cache:
            k_norm2 = jnp.maximum(k_norm2, _dup_key_norm2(kc_ref[0, 0]))
        plain_ref[0] = _needs_no_stabiliser(kb_ref[0], k_norm2)
        m_ref[...] = jnp.full(m_ref.shape, -jnp.inf, F32)
        acc_ref[...] = jnp.zeros(acc_ref.shape, F32)

    def step(kr, vr):
        @pl.when(plain_ref[0] == 1)
        def _():
            _plain_cols(kr[0, 0], vr[0, 0], qm_ref, acc_ref, cols)

        @pl.when(plain_ref[0] != 1)
        def _():
            _online_cols(kr[0, 0], vr[0, 0], qm_ref, m_ref, acc_ref, cols)

    if has_cache:
        @pl.when(kt < nk_main)
        def _():
            step(k_ref, vt_ref)

        @pl.when(kt >= nk_main)
        def _():
            step(kc_ref, vtc_ref)
    else:
        step(k_ref, vt_ref)

    @pl.when(kt == pl.num_programs(3) - 1)
    def _():
        for part in range(per_head):
            heads = []
            for g in range(GQA_GROUP):
                acc = acc_ref[g * per_head + part]
                heads.append(acc[0:HEAD_DIM, :] / acc[HEAD_DIM:HEAD_DIM + 1, :])
            o_ref[0, part * col:(part + 1) * col, :] = jnp.concatenate(heads, axis=0).T.astype(BF16)


def _kv_specs(n_heads_blk, vt_rows, tk, nk_main, cache):
    if cache:
        k_map = lambda b, h, qi, kt: (b, h, jnp.maximum(kt - nk_main, 0), 0)
        v_map = lambda b, h, qi, kt: (b, h, 0, jnp.maximum(kt - nk_main, 0))
    else:
        k_map = lambda b, h, qi, kt: (b, h, jnp.minimum(kt, nk_main - 1), 0)
        v_map = lambda b, h, qi, kt: (b, h, 0, jnp.minimum(kt, nk_main - 1))
    return [pl.BlockSpec((1, n_heads_blk, tk, LANES), k_map), pl.BlockSpec((1, 1, vt_rows, tk), v_map)]


def _gqa_attention(q, k_bound, kd, vt, kd_c, vt_c, n_batch, n_tok, tq, tk):
    has_cache = kd_c is not None
    assert not has_cache or tk == PAST_LEN
    nk_main = n_tok // tk
    nk = nk_main + (PAST_LEN // tk if has_cache else 0)
    rows = GQA_GROUP * tq
    col = min(MAX_COL, tq)
    in_specs = [pl.BlockSpec(memory_space=pltpu.SMEM),
                pl.BlockSpec((1, tq, 2 * LANES), lambda b, h, qi, kt: (b, qi, h))]
    in_specs += _kv_specs(1, GQA_VT_ROWS, tk, nk_main, False)
    args = [k_bound, q.reshape(n_batch, n_tok, BRANCH_W), kd, vt]
    if has_cache:
        in_specs += _kv_specs(1, GQA_VT_ROWS, tk, nk_main, True)
        args += [kd_c, vt_c]
    out = pl.pallas_call(
        functools.partial(_gqa_kernel, nk_main=nk_main, has_cache=has_cache, tq=tq),
        grid=(n_batch, GQA_KV_HEADS, n_tok // tq, nk),
        in_specs=in_specs,
        out_specs=pl.BlockSpec((1, tq, 2 * LANES), lambda b, h, qi, kt: (b, qi, h)),
        out_shape=jax.ShapeDtypeStruct((n_batch, n_tok, BRANCH_W), BF16),
        scratch_shapes=[
            pltpu.VMEM((rows, LANES), BF16),
            pltpu.VMEM((rows // col, 1, col), F32),
            pltpu.VMEM((rows // col, GQA_VT_ROWS, col), F32),
            pltpu.SMEM((1,), jnp.int32),
        ],
        compiler_params=pltpu.CompilerParams(
            dimension_semantics=("parallel", "parallel", "parallel", "arbitrary"),
            vmem_limit_bytes=VMEM_LIMIT),
        name="gqa_attention",
    )(*args)
    return out.reshape(n_batch * n_tok, BRANCH_W)


def _diff_kernel(*refs, nk_main, has_cache, lam_init):
    if has_cache:
        (kb_ref, q_ref, lam_ref, gsub_ref, k_ref, vt_ref, kc_ref, vtc_ref, o_ref,
         qm_ref, m_ref, acc_ref, plain_ref) = refs
    else:
        kb_ref, q_ref, lam_ref, gsub_ref, k_ref, vt_ref, o_ref, qm_ref, m_ref, acc_ref, plain_ref = refs
    kt = pl.program_id(3)
    tq = q_ref.shape[1]
    col = acc_ref.shape[-1]
    per_map = tq // col

    @pl.when(kt == 0)
    def _():
        blk = q_ref[0]
        for mm in range(2):
            qm_ref[mm * tq:(mm + 1) * tq, :] = _masked_queries(blk, mm == 0)
        k_norm2 = kb_ref[1]
        if has_cache:
            for mm in range(2):
                k_norm2 = jnp.maximum(k_norm2, _dup_key_norm2(kc_ref[0, mm]))
        plain_ref[0] = _needs_no_stabiliser(kb_ref[0], k_norm2)
        m_ref[...] = jnp.full(m_ref.shape, -jnp.inf, F32)
        acc_ref[...] = jnp.zeros(acc_ref.shape, F32)

    def step(kr, vr):
        @pl.when(plain_ref[0] == 1)
        def _():
            for mm in range(2):
                _plain_cols(kr[0, mm], vr[0, 0], qm_ref, acc_ref, range(mm * per_map, (mm + 1) * per_map))

        @pl.when(plain_ref[0] != 1)
        def _():
            for mm in range(2):
                _online_cols(kr[0, mm], vr[0, 0], qm_ref, m_ref, acc_ref,
                             range(mm * per_map, (mm + 1) * per_map))

    if has_cache:
        @pl.when(kt < nk_main)
        def _():
            step(k_ref, vt_ref)

        @pl.when(kt >= nk_main)
        def _():
            step(kc_ref, vtc_ref)
    else:
        step(k_ref, vt_ref)

    @pl.when(kt == pl.num_programs(3) - 1)
    def _():
        lp = lam_ref[...]
        lam = (jnp.exp(jnp.sum(lp[0:1] * lp[1:2], axis=-1, keepdims=True))
               - jnp.exp(jnp.sum(lp[2:3] * lp[3:4], axis=-1, keepdims=True)) + lam_init)
        for part in range(per_map):
            a0, a1 = acc_ref[part], acc_ref[per_map + part]
            o0 = a0[0:DIFF_DV, :] / a0[DIFF_DV:DIFF_DV + 1, :]
            o1 = a1[0:DIFF_DV, :] / a1[DIFF_DV:DIFF_DV + 1, :]
            d = (o0 - lam * o1).T
            ms = jnp.mean(d * d, axis=-1, keepdims=True)
            o_ref[0, part * col:(part + 1) * col, :] = (
                d * lax.rsqrt(ms + EPS) * gsub_ref[...] * (1.0 - lam_init)).astype(BF16)


def _diff_attention(q, k_bound, lam_p, gsub, kd, vt, kd_c, vt_c, n_batch, n_tok, tq, tk, lam_init):
    has_cache = kd_c is not None
    assert not has_cache or tk == PAST_LEN
    nk_main = n_tok // tk
    nk = nk_main + (PAST_LEN // tk if has_cache else 0)
    col = min(MAX_COL, tq)
    in_specs = [
        pl.BlockSpec(memory_space=pltpu.SMEM),
        pl.BlockSpec((1, tq, LANES), lambda b, h, qi, kt: (b, qi, h)),
        pl.BlockSpec((4, HEAD_DIM), lambda b, h, qi, kt: (0, 0)),
        pl.BlockSpec((1, DIFF_DV), lambda b, h, qi, kt: (0, 0)),
    ]
    in_specs += _kv_specs(2, DIFF_VT_ROWS, tk, nk_main, False)
    args = [k_bound, q.reshape(n_batch, n_tok, BRANCH_W), lam_p, gsub, kd, vt]
    if has_cache:
        in_specs += _kv_specs(2, DIFF_VT_ROWS, tk, nk_main, True)
        args += [kd_c, vt_c]
    out = pl.pallas_call(
        functools.partial(_diff_kernel, nk_main=nk_main, has_cache=has_cache, lam_init=lam_init),
        grid=(n_batch, DIFF_HEADS, n_tok // tq, nk),
        in_specs=in_specs,
        out_specs=pl.BlockSpec((1, tq, LANES), lambda b, h, qi, kt: (b, qi, h)),
        out_shape=jax.ShapeDtypeStruct((n_batch, n_tok, BRANCH_W), BF16),
        scratch_shapes=[
            pltpu.VMEM((2 * tq, LANES), BF16),
            pltpu.VMEM((2 * tq // col, 1, col), F32),
            pltpu.VMEM((2 * tq // col, DIFF_VT_ROWS, col), F32),
            pltpu.SMEM((1,), jnp.int32),
        ],
        compiler_params=pltpu.CompilerParams(
            dimension_semantics=("parallel", "parallel", "parallel", "arbitrary"),
            vmem_limit_bytes=VMEM_LIMIT),
        name="diff_attention",
    )(*args)
    return out.reshape(n_batch * n_tok, BRANCH_W)


def _gla_kernel(*refs, has_s0, n_chunk):
    if has_s0:
        qf, kf, vf, laf, qb, kb, vb, lab, s0_ref, of_ref, ob_ref, sfin_ref, st_ref = refs
    else:
        qf, kf, vf, laf, qb, kb, vb, lab, of_ref, ob_ref, sfin_ref, st_ref = refs
    i = pl.program_id(1)
    ck = GLA_CHUNK

    @pl.when(i == 0)
    def _():
        for d in range(2):
            for hd in range(GLA_HEADS):
                if has_s0:
                    s = s0_ref[0, d, hd]
                    z = jnp.zeros_like(s)
                    padded = jnp.concatenate([s, z] if hd % 2 == 0 else [z, s], axis=0)
                    st_ref[d, hd] = padded.T
                else:
                    st_ref[d, hd] = jnp.zeros((GLA_DV, LANES), F32)

    tb = qf.shape[0]
    r = lax.broadcasted_iota(jnp.int32, (tb, tb), 0)
    c = lax.broadcasted_iota(jnp.int32, (tb, tb), 1)
    lane = lax.broadcasted_iota(jnp.int32, (tb, LANES), 1)
    chunk_id = lax.broadcasted_iota(jnp.int32, (tb, GLA_W), 0) // ck
    zero_row = jnp.zeros((1, GLA_W), F32)
    streams = ((qf, kf, vf, laf, of_ref, c <= r), (qb, kb, vb, lab, ob_ref, c >= r))
    for d, (q_r, k_r, v_r, la_r, o_r, tri) in enumerate(streams):
        tri_b = tri.astype(F32).astype(BF16)
        g = sum(jnp.dot(tri_b, part, preferred_element_type=F32) for part in _bf16_terms(la_r[...]))
        if d == 0:
            bounds = [zero_row] + [g[ck * j - 1:ck * j, :] for j in range(1, n_chunk)]
            g_end = g[tb - 1:tb, :]
        else:
            bounds = [g[ck * (j + 1):ck * (j + 1) + 1, :] for j in range(n_chunk - 1)] + [zero_row]
            g_end = g[0:1, :]
        b_rows = jnp.concatenate([jnp.broadcast_to(b, (ck, GLA_W)) for b in bounds], axis=0)
        q = q_r[...].astype(F32)
        k = k_r[...].astype(F32)
        v = v_r[...]
        q_dec = q * jnp.exp(g - b_rows)
        q_glob = q * jnp.exp(g)
        k_end = k * jnp.exp(g_end - g)
        decay = jnp.exp(g_end)
        k_rel = []
        for j in range(n_chunk):
            reach = (chunk_id <= j) if d == 0 else (chunk_id >= j)
            k_rel.append(k * jnp.exp(jnp.where(reach, bounds[j] - g, 0.0)))
        for hd in range(GLA_HEADS):
            pair = slice((hd // 2) * LANES, (hd // 2 + 1) * LANES)
            keep = (lane < GLA_DK) if hd % 2 == 0 else (lane >= GLA_DK)
            qd_m = jnp.where(keep, q_dec[:, pair], 0.0).astype(BF16)
            qg_m = jnp.where(keep, q_glob[:, pair], 0.0).astype(BF16)
            ke_m = jnp.where(keep, k_end[:, pair], 0.0).astype(BF16)
            a_rows = [
                lax.dot_general(qd_m[ck * j:ck * (j + 1), :], k_rel[j][:, pair].astype(BF16),
                                (((1,), (1,)), ((), ())), preferred_element_type=F32)
                for j in range(n_chunk)
            ]
            a = jnp.where(tri, jnp.concatenate(a_rows, axis=0), 0.0).astype(BF16)
            hv = slice(hd * GLA_DV, (hd + 1) * GLA_DV)
            s_t = st_ref[d, hd]
            o = jnp.dot(a, v[:, hv], preferred_element_type=F32)
            o = o + lax.dot_general(qg_m, s_t.astype(BF16), (((1,), (1,)), ((), ())),
                                    preferred_element_type=F32)
            o_r[:, hv] = o
            ds_t = lax.dot_general(v[:, hv], ke_m, (((0,), (0,)), ((), ())), preferred_element_type=F32)
            st_ref[d, hd] = s_t * decay[:, pair] + ds_t

    @pl.when(i == pl.num_programs(1) - 1)
    def _():
        for d in range(2):
            for hd in range(GLA_HEADS):
                t = st_ref[d, hd].T
                sfin_ref[0, d, hd] = t[(hd % 2) * GLA_DK:(hd % 2 + 1) * GLA_DK, :]


def _gla(lq, lk, lv, la, s0, n_batch, n_tok, tb):
    nb = n_tok // tb
    t_total = n_batch * n_tok
    has_s0 = s0 is not None
    fwd = lambda b, i: (b * nb + i, 0)
    bwd = lambda b, i: (b * nb + nb - 1 - i, 0)
    bwd_la = lambda b, i: (b * nb + nb - 1 - i, 1)
    in_specs = [
        pl.BlockSpec((tb, GLA_W), fwd), pl.BlockSpec((tb, GLA_W), fwd),
        pl.BlockSpec((tb, BRANCH_W), fwd), pl.BlockSpec((tb, GLA_W), fwd),
        pl.BlockSpec((tb, GLA_W), bwd), pl.BlockSpec((tb, GLA_W), bwd),
        pl.BlockSpec((tb, BRANCH_W), bwd), pl.BlockSpec((tb, GLA_W), bwd_la),
    ]
    args = [lq, lk, lv, la, lq, lk, lv, la]
    state_spec = pl.BlockSpec((1, 2, GLA_HEADS, GLA_DK, GLA_DV), lambda b, i: (b, 0, 0, 0, 0))
    if has_s0:
        in_specs.append(state_spec)
        args.append(s0)
    return pl.pallas_call(
        functools.partial(_gla_kernel, has_s0=has_s0, n_chunk=tb // GLA_CHUNK),
        grid=(n_batch, nb),
        in_specs=in_specs,
        out_specs=[pl.BlockSpec((tb, BRANCH_W), fwd), pl.BlockSpec((tb, BRANCH_W), bwd), state_spec],
        out_shape=[
            jax.ShapeDtypeStruct((t_total, BRANCH_W), F32),
            jax.ShapeDtypeStruct((t_total, BRANCH_W), F32),
            jax.ShapeDtypeStruct((n_batch, 2, GLA_HEADS, GLA_DK, GLA_DV), F32),
        ],
        scratch_shapes=[pltpu.VMEM((2, GLA_HEADS, GLA_DV, LANES), F32)],
        compiler_params=pltpu.CompilerParams(
            dimension_semantics=("parallel", "arbitrary"), vmem_limit_bytes=VMEM_LIMIT),
        name="gla",
    )(*args)


def _merge_kernel(x_ref, mod_ref, og_ref, of_ref, ob_ref, lgo_ref, od_ref, gate_ref,
                  gout_ref, wb_ref, wo_ref, y_ref):
    o_gla = of_ref[...] + ob_ref[...]
    gla_parts = []
    for hd in range(GLA_HEADS):
        blk = o_gla[:, hd * GLA_DV:(hd + 1) * GLA_DV]
        ms = jnp.mean(blk * blk, axis=-1, keepdims=True)
        gla_parts.append(blk * lax.rsqrt(ms + EPS) * gout_ref[...])
    gla = jnp.concatenate(gla_parts, axis=1) * _silu(lgo_ref[...].astype(F32))
    branches = (og_ref[...], gla.astype(BF16), od_ref[...])
    mixed = None
    for b, ob in enumerate(branches):
        y = jnp.dot(ob, wb_ref[b], preferred_element_type=F32)
        y = y * gate_ref[:, b * D_MODEL:(b + 1) * D_MODEL].astype(F32)
        mixed = y if mixed is None else mixed + y
    out = jnp.dot(mixed.astype(BF16), wo_ref[...], preferred_element_type=F32)
    y_ref[...] = x_ref[...] + mod_ref[0, 2:3, :] * out


def _merge(x2d, mod, og, o_f, o_b, lgo, od, gates, lw, tiles_per_mod, tm):
    t_total = x2d.shape[0]
    tile = lambda c: pl.BlockSpec((tm, c), lambda i: (i, 0))
    return pl.pallas_call(
        _merge_kernel,
        grid=(t_total // tm,),
        in_specs=[
            tile(D_MODEL),
            pl.BlockSpec((1, 6, D_MODEL), lambda i: (i // tiles_per_mod, 0, 0)),
            tile(BRANCH_W), tile(BRANCH_W), tile(BRANCH_W), tile(BRANCH_W), tile(BRANCH_W),
            tile(N_BRANCH * D_MODEL),
            _full_spec((1, GLA_DV)),
            _full_spec((N_BRANCH, BRANCH_W, D_MODEL)),
            _full_spec((D_MODEL, D_MODEL)),
        ],
        out_specs=tile(D_MODEL),
        out_shape=jax.ShapeDtypeStruct((t_total, D_MODEL), F32),
        compiler_params=pltpu.CompilerParams(
            dimension_semantics=("parallel",), vmem_limit_bytes=VMEM_LIMIT),
        name="merge",
    )(x2d, mod, og, o_f, o_b, lgo, od, gates, lw["gout"], lw["w_branch"], lw["w_out"])


FFN_CHUNK = 256


def _ffn_kernel(x_ref, mod_ref, g2_ref, wi_ref, wo_ref, y_ref):
    x = x_ref[...]
    ms = jnp.mean(x * x, axis=-1, keepdims=True)
    h = x * lax.rsqrt(ms + EPS) * g2_ref[...]
    h = h * (1.0 + mod_ref[0, 4:5, :]) + mod_ref[0, 3:4, :]
    hb = h.astype(BF16)
    def up_proj(c0):
        a = jnp.dot(hb, wi_ref[:, c0:c0 + FFN_CHUNK], preferred_element_type=F32)
        u = jnp.dot(hb, wi_ref[:, FFN_HIDDEN + c0:FFN_HIDDEN + c0 + FFN_CHUNK], preferred_element_type=F32)
        return a, u

    acc = None
    chunks = list(range(0, FFN_HIDDEN, FFN_CHUNK))
    nxt = up_proj(chunks[0])
    for i, c0 in enumerate(chunks):
        a, u = nxt
        if i + 1 < len(chunks):
            nxt = up_proj(chunks[i + 1])
        act = (_silu(a) * u).astype(BF16)
        part = jnp.dot(act, wo_ref[c0:c0 + FFN_CHUNK, :], preferred_element_type=F32)
        acc = part if acc is None else acc + part
    y_ref[...] = x + mod_ref[0, 5:6, :] * acc


def _ffn(x2d, mod, lw, tiles_per_mod, tm):
    t_total = x2d.shape[0]
    tile = lambda c: pl.BlockSpec((tm, c), lambda i: (i, 0))
    return pl.pallas_call(
        _ffn_kernel,
        grid=(t_total // tm,),
        in_specs=[
            tile(D_MODEL),
            pl.BlockSpec((1, 6, D_MODEL), lambda i: (i // tiles_per_mod, 0, 0)),
            _full_spec((1, D_MODEL)),
            _full_spec((D_MODEL, 2 * FFN_HIDDEN)),
            _full_spec((FFN_HIDDEN, D_MODEL)),
        ],
        out_specs=tile(D_MODEL),
        out_shape=jax.ShapeDtypeStruct((t_total, D_MODEL), F32),
        compiler_params=pltpu.CompilerParams(
            dimension_semantics=("parallel",), vmem_limit_bytes=VMEM_LIMIT),
        name="ffn",
    )(x2d, mod, lw["norm2"], lw["w_ffn_in"], lw["w_ffn_out"])


def _layer_weights(l, p):
    w = p["w_in"][l]
    s = (GQA_HEADS * HEAD_DIM, GQA_KV_HEADS * HEAD_DIM, GQA_KV_HEADS * HEAD_DIM,
         GLA_HEADS * GLA_DK, GLA_HEADS * GLA_DK, GLA_HEADS * GLA_DV, 2 * GLA_RANK, GLA_HEADS * GLA_DV,
         DIFF_HEADS * 2 * HEAD_DIM, DIFF_HEADS * 2 * HEAD_DIM, DIFF_HEADS * DIFF_DV, N_BRANCH * D_MODEL)
    offs = [0]
    for n in s:
        offs.append(offs[-1] + n)
    col = lambda j: w[:, offs[j]:offs[j + 1]].astype(BF16)
    gq, gk, gv, lq, lk, lv, lr, lgo, dq, dk, dv, mg = (col(j) for j in range(12))
    lr_pad = jnp.zeros((D_MODEL, LANES - 2 * GLA_RANK), BF16)
    w_in = jnp.concatenate([gq, dq, gk, dk, gv, dv, lq, lk, lv, lgo, lr, lr_pad, mg], axis=1)
    scale = HEAD_DIM ** -0.5 * math.log2(math.e)
    head_bound = lambda g, s: HEAD_DIM * (NORM_MARGIN * s) ** 2 * jnp.max(g * g)
    norm_bounds = lambda gq_, gk_: jnp.stack([head_bound(gq_, scale), head_bound(gk_, 1.0)])
    gq_row = jnp.concatenate([jnp.tile(p["gqa_q_norm"][l], GQA_HEADS),
                              jnp.tile(p["diff_q_norm"][l], 2 * DIFF_HEADS)]) * scale
    gk_row = jnp.concatenate([jnp.tile(p["gqa_k_norm"][l], GQA_KV_HEADS),
                              jnp.tile(p["diff_k_norm"][l], 2 * DIFF_HEADS)])
    idx = jnp.arange(2 * LANES) // HEAD_DIM
    gmat = jnp.where(idx[:, None] == idx[None, :], 1.0 / HEAD_DIM, 0.0).astype(BF16)
    up = jnp.zeros((LANES, 2 * GLA_W), F32)
    up = up.at[0:GLA_RANK, 0:GLA_W].set(p["gla_alpha_up"][l, 0])
    up = up.at[GLA_RANK:2 * GLA_RANK, GLA_W:].set(p["gla_alpha_up"][l, 1])
    ub = p["gla_alpha_bias"][l].reshape(1, 2 * GLA_W)
    up_hi = up.astype(BF16)
    up = jnp.stack([up_hi, (up - up_hi.astype(F32)).astype(BF16)])
    return {
        "w_in": w_in, "gq": gq_row.reshape(1, N_Q), "gk": gk_row.reshape(1, N_K), "gmat": gmat,
        "up": up, "ub": ub,
        "kb_g": norm_bounds(p["gqa_q_norm"][l], p["gqa_k_norm"][l]),
        "kb_d": norm_bounds(p["diff_q_norm"][l], p["diff_k_norm"][l]),
        "norm1": p["norm1"][l].reshape(1, D_MODEL), "norm2": p["norm2"][l].reshape(1, D_MODEL),
        "gout": p["gla_out_norm"][l].reshape(1, GLA_DV),
        "gsub": p["diff_sub_norm"][l].reshape(1, DIFF_DV),
        "lam": p["diff_lambda"][l],
        "w_branch": p["w_branch"][l].astype(BF16), "w_out": p["w_out"][l].astype(BF16),
        "w_ffn_in": p["w_ffn_in"][l].astype(BF16), "w_ffn_out": p["w_ffn_out"][l].astype(BF16),
    }


def _rope_tables(n_tokens):
    n_rows = n_tokens // GRID_W
    row = jnp.repeat(jnp.arange(n_rows, dtype=F32), GRID_W)
    col = jnp.tile(jnp.arange(GRID_W, dtype=F32), n_rows)
    n_freq = HEAD_DIM // 4
    freqs = ROPE_THETA ** (-jnp.arange(n_freq, dtype=F32) / n_freq)
    ar, ac = row[:, None] * freqs, col[:, None] * freqs
    cos = jnp.concatenate([jnp.cos(ar), jnp.cos(ar), jnp.cos(ac), jnp.cos(ac)], axis=1)
    sin = jnp.concatenate([-jnp.sin(ar), jnp.sin(ar), -jnp.sin(ac), jnp.sin(ac)], axis=1)
    return jnp.tile(cos, (1, LANES // HEAD_DIM)), jnp.tile(sin, (1, LANES // HEAD_DIM))


def _with_ones_rows(v_t):
    lead = v_t.shape[:-2]
    s = v_t.shape[-1]
    ones = jnp.ones(lead + (1, s), v_t.dtype)
    zeros = jnp.zeros(lead + (BF16_ROWS - 1, s), v_t.dtype)
    return jnp.concatenate([v_t, ones, zeros], axis=-2).astype(BF16)


def _cache_layouts(l, cache_gqa_k, cache_gqa_v, cache_diff_k, cache_diff_v):
    b = cache_gqa_k.shape[0]
    gk = jnp.transpose(cache_gqa_k[:, l], (0, 2, 1, 3))
    kd_g = jnp.concatenate([gk, gk], axis=-1).astype(BF16)
    vt_g = _with_ones_rows(jnp.transpose(cache_gqa_v[:, l], (0, 2, 3, 1)))
    dk = jnp.transpose(cache_diff_k[:, l], (0, 2, 3, 1, 4)).reshape(b, 2 * DIFF_HEADS, PAST_LEN, HEAD_DIM)
    kd_d = jnp.concatenate([dk, dk], axis=-1).astype(BF16)
    vt_d = _with_ones_rows(jnp.transpose(cache_diff_v[:, l], (0, 2, 3, 1)))
    return kd_g, vt_g, kd_d, vt_d


def _run_layer(x2d, mod, lw, l, n_batch, n_tok, rope_tabs, ctx, emit_cache, cfg):
    tiles_per_mod = lambda tm: (n_tok // tm) if mod.shape[0] > 1 else (n_batch * n_tok // tm)
    outs = _in_projection(x2d, mod, lw, n_batch, n_tok, rope_tabs, emit_cache, cfg["tm_in"])
    qg, qd, kdg, kdd, vtg, vtd, lq, lk, lv, lgo, la, gates = outs[:12]
    if ctx is None:
        kd_gc = vt_gc = kd_dc = vt_dc = s0 = None
    else:
        kd_gc, vt_gc, kd_dc, vt_dc, s0 = ctx
    og = _gqa_attention(qg, lw["kb_g"], kdg, vtg, kd_gc, vt_gc, n_batch, n_tok, cfg["tq_gqa"], cfg["tk"])
    lam_init = 0.8 - 0.6 * math.exp(-0.3 * l)
    od = _diff_attention(qd, lw["kb_d"], lw["lam"], lw["gsub"], kdd, vtd, kd_dc, vt_dc, n_batch, n_tok,
                         cfg["tq_diff"], cfg["tk"], lam_init)
    o_f, o_b, s_fin = _gla(lq, lk, lv, la, s0, n_batch, n_tok, cfg["tb_gla"])
    x2d = _merge(x2d, mod, og, o_f, o_b, lgo, od, gates, lw, tiles_per_mod(cfg["tm"]), cfg["tm"])
    x2d = _ffn(x2d, mod, lw, tiles_per_mod(cfg["tm"]), cfg["tm"])
    cache = tuple(outs[12:]) + (s_fin,) if emit_cache else None
    return x2d, cache


PROMPT_CFG = dict(tm_in=256, tq_gqa=256, tq_diff=256, tk=256, tb_gla=256, tm=512)
SAMPLE_CFG = dict(tm_in=256, tq_gqa=1024, tq_diff=2048, tk=512, tb_gla=256, tm=512)


def kernel(x_prompt, x_sample, c, cache_gqa_k, cache_gqa_v, state_gla, cache_diff_k, cache_diff_v, c_ctx, w_mod, b_mod, norm1, norm2, w_in, gqa_q_norm, gqa_k_norm, gla_alpha_up, gla_alpha_bias, gla_out_norm, diff_q_norm, diff_k_norm, diff_lambda, diff_sub_norm, w_branch, w_out, w_ffn_in, w_ffn_out):
    p = {
        "norm1": norm1, "norm2": norm2, "w_in": w_in, "gqa_q_norm": gqa_q_norm, "gqa_k_norm": gqa_k_norm,
        "gla_alpha_up": gla_alpha_up, "gla_alpha_bias": gla_alpha_bias, "gla_out_norm": gla_out_norm,
        "diff_q_norm": diff_q_norm, "diff_k_norm": diff_k_norm, "diff_lambda": diff_lambda,
        "diff_sub_norm": diff_sub_norm, "w_branch": w_branch, "w_out": w_out,
        "w_ffn_in": w_ffn_in, "w_ffn_out": w_ffn_out,
    }
    n_ctx_b, n_ctx = x_prompt.shape[:2]
    n_lat_b, n_lat = x_sample.shape[:2]
    cond_rows = jnp.concatenate(
        [c_ctx[None, :], c, jnp.zeros((8 - 1 - n_lat_b, D_MODEL), F32)], axis=0)
    mod_all = _modulation(cond_rows, w_mod, b_mod)
    weights = [_layer_weights(l, p) for l in range(DEPTH)]

    y = x_prompt.reshape(n_ctx_b * n_ctx, D_MODEL)
    caches = []
    for l in range(DEPTH):
        mod = mod_all[l, 0:1].reshape(1, 6, D_MODEL)
        y, cache = _run_layer(y, mod, weights[l], l, n_ctx_b, n_ctx, None, None, True, PROMPT_CFG)
        caches.append(cache)
    y_prompt = y.reshape(n_ctx_b, n_ctx, D_MODEL)
    stack = lambda j, shape: jnp.stack([cc[j].reshape(shape) for cc in caches], axis=1)
    new_gqa_k = stack(0, (n_ctx_b, n_ctx, GQA_KV_HEADS, HEAD_DIM))
    new_diff_k = stack(1, (n_ctx_b, n_ctx, DIFF_HEADS, 2, HEAD_DIM))
    new_gqa_v = stack(2, (n_ctx_b, n_ctx, GQA_KV_HEADS, HEAD_DIM))
    new_diff_v = stack(3, (n_ctx_b, n_ctx, DIFF_HEADS, DIFF_DV))
    new_state_gla = jnp.stack([cc[4] for cc in caches], axis=1)

    rope_tabs = _rope_tables(n_lat)
    y = x_sample.reshape(n_lat_b * n_lat, D_MODEL)
    for l in range(DEPTH):
        mod = mod_all[l, 1:1 + n_lat_b].reshape(n_lat_b, 6, D_MODEL)
        ctx = _cache_layouts(l, cache_gqa_k, cache_gqa_v, cache_diff_k, cache_diff_v) + (state_gla[:, l],)
        y, _ = _run_layer(y, mod, weights[l], l, n_lat_b, n_lat, rope_tabs, ctx, False, SAMPLE_CFG)
    y_sample = y.reshape(n_lat_b, n_lat, D_MODEL)
    return (y_prompt, y_sample, new_gqa_k, new_gqa_v, new_state_gla, new_diff_k, new_diff_v)
```

```python
import functools
import math

import jax
import jax.numpy as jnp
from jax import lax
from jax.experimental import pallas as pl
from jax.experimental.pallas import tpu as pltpu

D_MODEL = 1024
DEPTH = 2
PAST_LEN = 512
GRID_W = 64
HEAD_DIM = 64
GQA_HEADS = 8
GQA_KV_HEADS = 2
GQA_GROUP = GQA_HEADS // GQA_KV_HEADS
GLA_HEADS = 4
GLA_DK = 64
GLA_DV = 128
GLA_RANK = 16
GLA_TAU = 16.0
GLA_CHUNK = 64
DIFF_HEADS = 4
DIFF_DV = 2 * HEAD_DIM
N_BRANCH = 3
BRANCH_W = 512
FFN_HIDDEN = ((8 * D_MODEL + 3 * 256 - 1) // (3 * 256)) * 256
ROPE_THETA = 10000.0
EPS = 1e-6

LANES = 128
BF16_ROWS = 16
VMEM_LIMIT = 56 * 1024 * 1024

F32 = jnp.float32
BF16 = jnp.bfloat16
HIGHEST = lax.Precision.HIGHEST

A_GQ = 0
A_KV = A_GQ + GQA_HEADS * HEAD_DIM
A_GLA = A_KV + 2 * GQA_KV_HEADS * HEAD_DIM
SPLIT_AB = A_GLA + 2 * GLA_HEADS * GLA_DK + GLA_HEADS * GLA_DV + 2 * GLA_RANK
N_A = -(-SPLIT_AB // LANES) * LANES
B_LGO = 0
B_DQ = B_LGO + GLA_HEADS * GLA_DV
B_DK = B_DQ + DIFF_HEADS * 2 * HEAD_DIM
B_DV = B_DK + DIFF_HEADS * 2 * HEAD_DIM
B_MG = B_DV + DIFF_HEADS * DIFF_DV
N_B = B_MG + N_BRANCH * D_MODEL
N_QG = GQA_HEADS * HEAD_DIM
GLA_W = GLA_HEADS * GLA_DK
GQA_VT_ROWS = HEAD_DIM + BF16_ROWS
DIFF_VT_ROWS = DIFF_DV + BF16_ROWS


def _sigmoid(x):
    return 1.0 / (1.0 + jnp.exp(-x))


def _silu(x):
    return x * _sigmoid(x)


def _bf16_terms(x):
    hi = x.astype(BF16)
    r1 = x - hi.astype(F32)
    mid = r1.astype(BF16)
    lo = (r1 - mid.astype(F32)).astype(BF16)
    return hi, mid, lo


def _full_spec(shape):
    n = len(shape)
    return pl.BlockSpec(shape, lambda *_: (0,) * n, pipeline_mode=pl.Buffered(1))


def _layer_spec(shape, l):
    n = len(shape)
    return pl.BlockSpec((None,) + tuple(shape), lambda *_: (l,) + (0,) * n, pipeline_mode=pl.Buffered(1))


def _mod_kernel(cond_ref, w_ref, b_ref, o_ref):
    s = _silu(cond_ref[...])
    o_ref[0] = jnp.dot(s, w_ref[0], precision=HIGHEST, preferred_element_type=F32) + b_ref[0]


def _modulation(cond_rows, w_mod, b_mod):
    tn = 1536
    n_out = 6 * D_MODEL
    return pl.pallas_call(
        _mod_kernel,
        grid=(DEPTH, n_out // tn),
        in_specs=[
            pl.BlockSpec((8, D_MODEL), lambda l, j: (0, 0)),
            pl.BlockSpec((1, D_MODEL, tn), lambda l, j: (l, 0, j)),
            pl.BlockSpec((1, 1, tn), lambda l, j: (l, 0, j)),
        ],
        out_specs=pl.BlockSpec((1, 8, tn), lambda l, j: (l, 0, j)),
        out_shape=jax.ShapeDtypeStruct((DEPTH, 8, n_out), F32),
        compiler_params=pltpu.CompilerParams(
            dimension_semantics=("parallel", "parallel"), vmem_limit_bytes=VMEM_LIMIT),
        name="modulation",
    )(cond_rows, w_mod, b_mod.reshape(DEPTH, 1, n_out))


def _head_norm(z, gain, gmat):
    outs = []
    n = z.shape[1]
    for c0 in range(0, n, 2 * LANES):
        w = min(2 * LANES, n - c0)
        zz = z[:, c0:c0 + w]
        ms = jnp.dot((zz * zz).astype(BF16), gmat[0:w, 0:w], preferred_element_type=F32)
        outs.append(zz * lax.rsqrt(ms + EPS) * gain[:, c0:c0 + w])
    return outs


def _rope_block(zb, cos, sin, first):
    partner = jnp.where(first, pltpu.roll(zb, LANES - 16, 1), pltpu.roll(zb, 16, 1))
    return zb * cos + partner * sin


def _in_kernel(*refs, rope, emit_cache):
    it = iter(refs)
    (x_ref, mod_ref, g1_ref, wa_ref, wb_ref, gq_ref, gk_ref, gmat_ref, up_ref, ub_ref) = (
        next(it) for _ in range(10))
    if rope:
        cos_ref, sin_ref = next(it), next(it)
    (qg_ref, qd_ref, kdg_ref, kdd_ref, vtg_ref, vtd_ref,
     lq_ref, lk_ref, lv_ref, lgo_ref, la_ref, gate_ref) = (next(it) for _ in range(12))
    if emit_cache:
        ck_g_ref, ck_d_ref, cv_g_ref, cv_d_ref = (next(it) for _ in range(4))

    tm = x_ref.shape[0]
    x = x_ref[...]
    ms = jnp.mean(x * x, axis=-1, keepdims=True)
    h = x * lax.rsqrt(ms + EPS) * g1_ref[...]
    h = h * (1.0 + mod_ref[0, 1:2, :]) + mod_ref[0, 0:1, :]
    hb = h.astype(BF16)

    def proj(w_ref, c0, n):
        return jnp.dot(hb, w_ref[:, c0:c0 + n], preferred_element_type=F32)

    lane = lax.broadcasted_iota(jnp.int32, (tm, LANES), 1)
    first16 = (lane % 32) < 16
    lo_half = lane < HEAD_DIM
    gmat = gmat_ref[...]
    if rope:
        cos, sin = cos_ref[...], sin_ref[...]

    def blocks128(parts):
        out = []
        for p in parts:
            for c in range(0, p.shape[1], LANES):
                out.append(p[:, c:c + LANES])
        return out

    def store_queries(z, gain, dst):
        for j, zb in enumerate(blocks128(_head_norm(z, gain, gmat))):
            if rope:
                zb = _rope_block(zb, cos, sin, first16)
            dst[:, j * LANES:(j + 1) * LANES] = zb.astype(BF16)

    def store_keys(z, gain, dst, cache_dst):
        for j, zb in enumerate(blocks128(_head_norm(z, gain, gmat))):
            if rope:
                zb = _rope_block(zb, cos, sin, first16)
            if emit_cache:
                cache_dst[:, j * LANES:(j + 1) * LANES] = zb
            sw = pltpu.roll(zb, HEAD_DIM, 1)
            dst[0, 2 * j] = jnp.where(lo_half, zb, sw).astype(BF16)
            dst[0, 2 * j + 1] = jnp.where(lo_half, sw, zb).astype(BF16)

    z_gq = proj(wa_ref, A_GQ, N_QG)
    z_dq = proj(wb_ref, B_DQ, N_QG)
    store_queries(z_gq, gq_ref[:, 0:N_QG], qg_ref)
    z_kv = proj(wa_ref, A_KV, 2 * LANES)
    store_queries(z_dq, gq_ref[:, N_QG:2 * N_QG], qd_ref)
    z_dk = proj(wb_ref, B_DK, BRANCH_W)
    store_keys(z_kv[:, 0:LANES], gk_ref[:, 0:LANES], kdg_ref, ck_g_ref if emit_cache else None)
    z_dv = proj(wb_ref, B_DV, BRANCH_W)
    store_keys(z_dk, gk_ref[:, LANES:LANES + BRANCH_W], kdd_ref, ck_d_ref if emit_cache else None)
    zl = proj(wa_ref, A_GLA, N_A - A_GLA)

    ones_rows = jnp.where(lax.broadcasted_iota(jnp.int32, (BF16_ROWS, tm), 0) == 0, 1.0, 0.0).astype(BF16)
    z_gv = z_kv[:, LANES:2 * LANES]
    if emit_cache:
        cv_g_ref[...] = z_gv
        cv_d_ref[...] = z_dv
    vt = z_gv.T
    for hh in range(GQA_KV_HEADS):
        vtg_ref[0, hh, 0:HEAD_DIM, :] = vt[hh * HEAD_DIM:(hh + 1) * HEAD_DIM, :].astype(BF16)
        vtg_ref[0, hh, HEAD_DIM:GQA_VT_ROWS, :] = ones_rows
    for hh in range(DIFF_HEADS):
        vt = z_dv[:, hh * LANES:(hh + 1) * LANES].T
        vtd_ref[0, hh, 0:DIFF_DV, :] = vt.astype(BF16)
        vtd_ref[0, hh, DIFF_DV:DIFF_VT_ROWS, :] = ones_rows

    z_lgo = proj(wb_ref, B_LGO, BRANCH_W)
    o = 0
    lq_ref[...] = (zl[:, o:o + GLA_W] * (GLA_DK ** -0.5)).astype(BF16)
    o += GLA_W
    lk_ref[...] = zl[:, o:o + GLA_W].astype(BF16)
    o += GLA_W
    lv_ref[...] = zl[:, o:o + BRANCH_W].astype(BF16)
    o += BRANCH_W
    lgo_ref[...] = z_lgo.astype(BF16)
    lr = zl[:, o:o + LANES]
    lr_hi, lr_mid, _ = _bf16_terms(lr)
    z = (jnp.dot(lr_hi, up_ref[0], preferred_element_type=F32)
         + jnp.dot(lr_mid, up_ref[0], preferred_element_type=F32)
         + jnp.dot(lr_hi, up_ref[1], preferred_element_type=F32)) + ub_ref[...]
    log_sig = jnp.minimum(z, 0.0) - jnp.log(1.0 + jnp.exp(-jnp.abs(z)))
    la_ref[...] = log_sig * (1.0 / GLA_TAU)

    for b in range(N_BRANCH):
        gate_ref[:, b * D_MODEL:(b + 1) * D_MODEL] = _sigmoid(
            proj(wb_ref, B_MG + b * D_MODEL, D_MODEL)).astype(BF16)


def _in_projection(x2d, mod, lw, sw, l, n_batch, n_tok, rope_tabs, emit_cache, tm):
    t_total = n_batch * n_tok
    tpb = n_tok // tm
    rope = rope_tabs is not None
    tile = lambda c: pl.BlockSpec((tm, c), lambda i: (i, 0))
    in_specs = [
        tile(D_MODEL),
        pl.BlockSpec((1, 6, D_MODEL), lambda i: (i // tpb if mod.shape[0] > 1 else 0, 0, 0)),
        _full_spec((1, D_MODEL)),
        _layer_spec((D_MODEL, N_A), l),
        _layer_spec((D_MODEL, N_B), l),
        _full_spec((1, 2 * N_QG)),
        _full_spec((1, LANES + BRANCH_W)),
        _full_spec((2 * LANES, 2 * LANES)),
        _full_spec((2, LANES, 2 * GLA_W)),
        _full_spec((1, 2 * GLA_W)),
    ]
    args = [x2d, mod, lw["norm1"], sw["w_a"], sw["w_b"], lw["gq"], lw["gk"], lw["gmat"], lw["up"], lw["ub"]]
    if rope:
        in_specs += [pl.BlockSpec((tm, LANES), lambda i: (i % tpb, 0))] * 2
        args += list(rope_tabs)
    sd = jax.ShapeDtypeStruct
    out_shape = [
        sd((t_total, BRANCH_W), BF16), sd((t_total, BRANCH_W), BF16),
        sd((n_batch, GQA_KV_HEADS, n_tok, LANES), BF16), sd((n_batch, 2 * DIFF_HEADS, n_tok, LANES), BF16),
        sd((n_batch, GQA_KV_HEADS, GQA_VT_ROWS, n_tok), BF16), sd((n_batch, DIFF_HEADS, DIFF_VT_ROWS, n_tok), BF16),
        sd((t_total, GLA_W), BF16), sd((t_total, GLA_W), BF16),
        sd((t_total, BRANCH_W), BF16), sd((t_total, BRANCH_W), BF16),
        sd((t_total, 2 * GLA_W), F32), sd((t_total, N_BRANCH * D_MODEL), BF16),
    ]
    out_specs = [
        tile(BRANCH_W), tile(BRANCH_W),
        pl.BlockSpec((1, GQA_KV_HEADS, tm, LANES), lambda i: (i // tpb, 0, i % tpb, 0)),
        pl.BlockSpec((1, 2 * DIFF_HEADS, tm, LANES), lambda i: (i // tpb, 0, i % tpb, 0)),
        pl.BlockSpec((1, GQA_KV_HEADS, GQA_VT_ROWS, tm), lambda i: (i // tpb, 0, 0, i % tpb)),
        pl.BlockSpec((1, DIFF_HEADS, DIFF_VT_ROWS, tm), lambda i: (i // tpb, 0, 0, i % tpb)),
        tile(GLA_W), tile(GLA_W), tile(BRANCH_W), tile(BRANCH_W), tile(2 * GLA_W), tile(N_BRANCH * D_MODEL),
    ]
    if emit_cache:
        out_shape += [sd((t_total, LANES), F32), sd((t_total, BRANCH_W), F32),
                      sd((t_total, LANES), F32), sd((t_total, BRANCH_W), F32)]
        out_specs += [tile(LANES), tile(BRANCH_W), tile(LANES), tile(BRANCH_W)]
    return pl.pallas_call(
        functools.partial(_in_kernel, rope=rope, emit_cache=emit_cache),
        grid=(t_total // tm,),
        in_specs=in_specs,
        out_specs=out_specs,
        out_shape=out_shape,
        compiler_params=pltpu.CompilerParams(
            dimension_semantics=("parallel",), vmem_limit_bytes=VMEM_LIMIT),
        name="in_projection",
    )(*args)


MAX_COL = 4 * LANES
NT_DIMS = (((1,), (1,)), ((), ()))
SCORE_BOUND = 45.0
NORM_MARGIN = 1.05


def _masked_queries(q_blk, keep_low):
    lane = lax.broadcasted_iota(jnp.int32, q_blk.shape, 1)
    keep = (lane < HEAD_DIM) if keep_low else (lane >= HEAD_DIM)
    return jnp.where(keep, q_blk.astype(F32), 0.0).astype(BF16)


def _dup_key_norm2(kd):
    kf = kd.astype(F32)
    return 0.5 * jnp.max(jnp.sum(kf * kf, axis=1, keepdims=True))


def _needs_no_stabiliser(q_norm2, k_norm2):
    return (q_norm2 * k_norm2 <= SCORE_BOUND * SCORE_BOUND).astype(jnp.int32)


def _plain_cols(k, vt, qm_ref, acc_ref, cols):
    col = acc_ref.shape[-1]
    cols = list(cols)
    scores = lambda c: lax.dot_general(k, qm_ref[c * col:(c + 1) * col, :], NT_DIMS,
                                       preferred_element_type=F32)
    s_next = scores(cols[0])
    for i, c in enumerate(cols):
        s = s_next
        if i + 1 < len(cols):
            s_next = scores(cols[i + 1])
        p = jnp.exp2(s).astype(BF16)
        acc_ref[c] += jnp.dot(vt, p, preferred_element_type=F32)


def _online_cols(k, vt, qm_ref, m_ref, acc_ref, cols):
    col = acc_ref.shape[-1]
    for c in cols:
        s = lax.dot_general(k, qm_ref[c * col:(c + 1) * col, :], NT_DIMS, preferred_element_type=F32)
        m_prev = m_ref[c]
        m_new = jnp.maximum(m_prev, jnp.max(s, axis=0, keepdims=True))
        alpha = jnp.exp2(m_prev - m_new)
        p = jnp.exp2(s - m_new).astype(BF16)
        acc_ref[c] = acc_ref[c] * alpha + jnp.dot(vt, p, preferred_element_type=F32)
        m_ref[c] = m_new


def _gqa_kernel(*refs, nk_main, has_cache, tq):
    if has_cache:
        kb_ref, q_ref, k_ref, vt_ref, kc_ref, vtc_ref, o_ref, qm_ref, m_ref, acc_ref, plain_ref = refs
    else:
        kb_ref, q_ref, k_ref, vt_ref, o_ref, qm_ref, m_ref, acc_ref, plain_ref = refs
    kt = pl.program_id(3)
    col = acc_ref.shape[-1]
    per_head = tq // col
    cols = range(GQA_GROUP * per_head)

    @pl.when(kt == 0)
    def _():
        for g in range(GQA_GROUP):
            blk = q_ref[0, :, (g // 2) * LANES:(g // 2 + 1) * LANES]
            qm_ref[g * tq:(g + 1) * tq, :] = _masked_queries(blk, g % 2 == 0)
        k_norm2 = kb_ref[1]
        if has_cache:
            k_norm2 = jnp.maximum(k_norm2, _dup_key_norm2(kc_ref[0, 0]))
        plain_ref[0] = _needs_no_stabiliser(kb_ref[0], k_norm2)
        m_ref[...] = jnp.full(m_ref.shape, -jnp.inf, F32)
        acc_ref[...] = jnp.zeros(acc_ref.shape, F32)

    def step(kr, vr):
        @pl.when(plain_ref[0] == 1)
        def _():
            _plain_cols(kr[0, 0], vr[0, 0], qm_ref, acc_ref, cols)

        @pl.when(plain_ref[0] != 1)
        def _():
            _online_cols(kr[0, 0], vr[0, 0], qm_ref, m_ref, acc_ref, cols)

    if has_cache:
        @pl.when(kt < nk_main)
        def _():
            step(k_ref, vt_ref)

        @pl.when(kt >= nk_main)
        def _():
            step(kc_ref, vtc_ref)
    else:
        step(k_ref, vt_ref)

    @pl.when(kt == pl.num_programs(3) - 1)
    def _():
        for part in range(per_head):
            heads = []
            for g in range(GQA_GROUP):
                acc = acc_ref[g * per_head + part]
                heads.append(acc[0:HEAD_DIM, :] / acc[HEAD_DIM:HEAD_DIM + 1, :])
            o_ref[0, part * col:(part + 1) * col, :] = jnp.concatenate(heads, axis=0).T.astype(BF16)


def _kv_specs(n_heads_blk, vt_rows, tk, nk_main, cache):
    if cache:
        k_map = lambda b, h, qi, kt: (b, h, jnp.maximum(kt - nk_main, 0), 0)
        v_map = lambda b, h, qi, kt: (b, h, 0, jnp.maximum(kt - nk_main, 0))
    else:
        k_map = lambda b, h, qi, kt: (b, h, jnp.minimum(kt, nk_main - 1), 0)
        v_map = lambda b, h, qi, kt: (b, h, 0, jnp.minimum(kt, nk_main - 1))
    return [pl.BlockSpec((1, n_heads_blk, tk, LANES), k_map), pl.BlockSpec((1, 1, vt_rows, tk), v_map)]


def _gqa_attention(q, k_bound, kd, vt, kd_c, vt_c, n_batch, n_tok, tq, tk):
    has_cache = kd_c is not None
    assert not has_cache or tk == PAST_LEN
    nk_main = n_tok // tk
    nk = nk_main + (PAST_LEN // tk if has_cache else 0)
    rows = GQA_GROUP * tq
    col = min(MAX_COL, tq)
    in_specs = [pl.BlockSpec(memory_space=pltpu.SMEM),
                pl.BlockSpec((1, tq, 2 * LANES), lambda b, h, qi, kt: (b, qi, h))]
    in_specs += _kv_specs(1, GQA_VT_ROWS, tk, nk_main, False)
    args = [k_bound, q.reshape(n_batch, n_tok, BRANCH_W), kd, vt]
    if has_cache:
        in_specs += _kv_specs(1, GQA_VT_ROWS, tk, nk_main, True)
        args += [kd_c, vt_c]
    out = pl.pallas_call(
        functools.partial(_gqa_kernel, nk_main=nk_main, has_cache=has_cache, tq=tq),
        grid=(n_batch, GQA_KV_HEADS, n_tok // tq, nk),
        in_specs=in_specs,
        out_specs=pl.BlockSpec((1, tq, 2 * LANES), lambda b, h, qi, kt: (b, qi, h)),
        out_shape=jax.ShapeDtypeStruct((n_batch, n_tok, BRANCH_W), BF16),
        scratch_shapes=[
            pltpu.VMEM((rows, LANES), BF16),
            pltpu.VMEM((rows // col, 1, col), F32),
            pltpu.VMEM((rows // col, GQA_VT_ROWS, col), F32),
            pltpu.SMEM((1,), jnp.int32),
        ],
        compiler_params=pltpu.CompilerParams(
            dimension_semantics=("parallel", "parallel", "parallel", "arbitrary"),
            vmem_limit_bytes=VMEM_LIMIT),
        name="gqa_attention",
    )(*args)
    return out.reshape(n_batch * n_tok, BRANCH_W)


def _diff_kernel(*refs, nk_main, has_cache, lam_init):
    if has_cache:
        (kb_ref, q_ref, lam_ref, gsub_ref, k_ref, vt_ref, kc_ref, vtc_ref, o_ref,
         qm_ref, m_ref, acc_ref, plain_ref) = refs
    else:
        kb_ref, q_ref, lam_ref, gsub_ref, k_ref, vt_ref, o_ref, qm_ref, m_ref, acc_ref, plain_ref = refs
    kt = pl.program_id(3)
    tq = q_ref.shape[1]
    col = acc_ref.shape[-1]
    per_map = tq // col

    @pl.when(kt == 0)
    def _():
        blk = q_ref[0]
        for mm in range(2):
            qm_ref[mm * tq:(mm + 1) * tq, :] = _masked_queries(blk, mm == 0)
        k_norm2 = kb_ref[1]
        if has_cache:
            for mm in range(2):
                k_norm2 = jnp.maximum(k_norm2, _dup_key_norm2(kc_ref[0, mm]))
        plain_ref[0] = _needs_no_stabiliser(kb_ref[0], k_norm2)
        m_ref[...] = jnp.full(m_ref.shape, -jnp.inf, F32)
        acc_ref[...] = jnp.zeros(acc_ref.shape, F32)

    def step(kr, vr):
        @pl.when(plain_ref[0] == 1)
        def _():
            for mm in range(2):
                _plain_cols(kr[0, mm], vr[0, 0], qm_ref, acc_ref, range(mm * per_map, (mm + 1) * per_map))

        @pl.when(plain_ref[0] != 1)
        def _():
            for mm in range(2):
                _online_cols(kr[0, mm], vr[0, 0], qm_ref, m_ref, acc_ref,
                             range(mm * per_map, (mm + 1) * per_map))

    if has_cache:
        @pl.when(kt < nk_main)
        def _():
            step(k_ref, vt_ref)

        @pl.when(kt >= nk_main)
        def _():
            step(kc_ref, vtc_ref)
    else:
        step(k_ref, vt_ref)

    @pl.when(kt == pl.num_programs(3) - 1)
    def _():
        lp = lam_ref[...]
        lam = (jnp.exp(jnp.sum(lp[0:1] * lp[1:2], axis=-1, keepdims=True))
               - jnp.exp(jnp.sum(lp[2:3] * lp[3:4], axis=-1, keepdims=True)) + lam_init)
        for part in range(per_map):
            a0, a1 = acc_ref[part], acc_ref[per_map + part]
            o0 = a0[0:DIFF_DV, :] / a0[DIFF_DV:DIFF_DV + 1, :]
            o1 = a1[0:DIFF_DV, :] / a1[DIFF_DV:DIFF_DV + 1, :]
            d = (o0 - lam * o1).T
            ms = jnp.mean(d * d, axis=-1, keepdims=True)
            o_ref[0, part * col:(part + 1) * col, :] = (
                d * lax.rsqrt(ms + EPS) * gsub_ref[...] * (1.0 - lam_init)).astype(BF16)


def _diff_attention(q, k_bound, lam_p, gsub, kd, vt, kd_c, vt_c, n_batch, n_tok, tq, tk, lam_init):
    has_cache = kd_c is not None
    assert not has_cache or tk == PAST_LEN
    nk_main = n_tok // tk
    nk = nk_main + (PAST_LEN // tk if has_cache else 0)
    col = min(MAX_COL, tq)
    in_specs = [
        pl.BlockSpec(memory_space=pltpu.SMEM),
        pl.BlockSpec((1, tq, LANES), lambda b, h, qi, kt: (b, qi, h)),
        pl.BlockSpec((4, HEAD_DIM), lambda b, h, qi, kt: (0, 0)),
        pl.BlockSpec((1, DIFF_DV), lambda b, h, qi, kt: (0, 0)),
    ]
    in_specs += _kv_specs(2, DIFF_VT_ROWS, tk, nk_main, False)
    args = [k_bound, q.reshape(n_batch, n_tok, BRANCH_W), lam_p, gsub, kd, vt]
    if has_cache:
        in_specs += _kv_specs(2, DIFF_VT_ROWS, tk, nk_main, True)
        args += [kd_c, vt_c]
    out = pl.pallas_call(
        functools.partial(_diff_kernel, nk_main=nk_main, has_cache=has_cache, lam_init=lam_init),
        grid=(n_batch, DIFF_HEADS, n_tok // tq, nk),
        in_specs=in_specs,
        out_specs=pl.BlockSpec((1, tq, LANES), lambda b, h, qi, kt: (b, qi, h)),
        out_shape=jax.ShapeDtypeStruct((n_batch, n_tok, BRANCH_W), BF16),
        scratch_shapes=[
            pltpu.VMEM((2 * tq, LANES), BF16),
            pltpu.VMEM((2 * tq // col, 1, col), F32),
            pltpu.VMEM((2 * tq // col, DIFF_VT_ROWS, col), F32),
            pltpu.SMEM((1,), jnp.int32),
        ],
        compiler_params=pltpu.CompilerParams(
            dimension_semantics=("parallel", "parallel", "parallel", "arbitrary"),
            vmem_limit_bytes=VMEM_LIMIT),
        name="diff_attention",
    )(*args)
    return out.reshape(n_batch * n_tok, BRANCH_W)


def _gla_kernel(*refs, has_s0, n_chunk):
    if has_s0:
        qf, kf, vf, laf, qb, kb, vb, lab, s0_ref, of_ref, ob_ref, sfin_ref, st_ref = refs
    else:
        qf, kf, vf, laf, qb, kb, vb, lab, of_ref, ob_ref, sfin_ref, st_ref = refs
    i = pl.program_id(1)
    ck = GLA_CHUNK

    @pl.when(i == 0)
    def _():
        for d in range(2):
            for hd in range(GLA_HEADS):
                if has_s0:
                    s = s0_ref[0, d, hd]
                    z = jnp.zeros_like(s)
                    padded = jnp.concatenate([s, z] if hd % 2 == 0 else [z, s], axis=0)
                    st_ref[d, hd] = padded.T
                else:
                    st_ref[d, hd] = jnp.zeros((GLA_DV, LANES), F32)

    tb = qf.shape[0]
    r = lax.broadcasted_iota(jnp.int32, (tb, tb), 0)
    c = lax.broadcasted_iota(jnp.int32, (tb, tb), 1)
    lane = lax.broadcasted_iota(jnp.int32, (tb, LANES), 1)
    chunk_id = lax.broadcasted_iota(jnp.int32, (tb, GLA_W), 0) // ck
    zero_row = jnp.zeros((1, GLA_W), F32)
    streams = ((qf, kf, vf, laf, of_ref, c <= r), (qb, kb, vb, lab, ob_ref, c >= r))
    for d, (q_r, k_r, v_r, la_r, o_r, tri) in enumerate(streams):
        tri_b = tri.astype(F32).astype(BF16)
        g = sum(jnp.dot(tri_b, part, preferred_element_type=F32) for part in _bf16_terms(la_r[...]))
        if d == 0:
            bounds = [zero_row] + [g[ck * j - 1:ck * j, :] for j in range(1, n_chunk)]
            g_end = g[tb - 1:tb, :]
        else:
            bounds = [g[ck * (j + 1):ck * (j + 1) + 1, :] for j in range(n_chunk - 1)] + [zero_row]
            g_end = g[0:1, :]
        b_rows = jnp.concatenate([jnp.broadcast_to(b, (ck, GLA_W)) for b in bounds], axis=0)
        q = q_r[...].astype(F32)
        k = k_r[...].astype(F32)
        v = v_r[...]
        q_dec = q * jnp.exp(g - b_rows)
        q_glob = q * jnp.exp(g)
        k_end = k * jnp.exp(g_end - g)
        decay = jnp.exp(g_end)
        k_rel = []
        for j in range(n_chunk):
            reach = (chunk_id <= j) if d == 0 else (chunk_id >= j)
            k_rel.append(k * jnp.exp(jnp.where(reach, bounds[j] - g, 0.0)))
        for hd in range(GLA_HEADS):
            pair = slice((hd // 2) * LANES, (hd // 2 + 1) * LANES)
            keep = (lane < GLA_DK) if hd % 2 == 0 else (lane >= GLA_DK)
            qd_m = jnp.where(keep, q_dec[:, pair], 0.0).astype(BF16)
            qg_m = jnp.where(keep, q_glob[:, pair], 0.0).astype(BF16)
            ke_m = jnp.where(keep, k_end[:, pair], 0.0).astype(BF16)
            a_rows = [
                lax.dot_general(qd_m[ck * j:ck * (j + 1), :], k_rel[j][:, pair].astype(BF16),
                                (((1,), (1,)), ((), ())), preferred_element_type=F32)
                for j in range(n_chunk)
            ]
            a = jnp.where(tri, jnp.concatenate(a_rows, axis=0), 0.0).astype(BF16)
            hv = slice(hd * GLA_DV, (hd + 1) * GLA_DV)
            s_t = st_ref[d, hd]
            o = jnp.dot(a, v[:, hv], preferred_element_type=F32)
            o = o + lax.dot_general(qg_m, s_t.astype(BF16), (((1,), (1,)), ((), ())),
                                    preferred_element_type=F32)
            o_r[:, hv] = o.astype(BF16)
            ds_t = lax.dot_general(v[:, hv], ke_m, (((0,), (0,)), ((), ())), preferred_element_type=F32)
            st_ref[d, hd] = s_t * decay[:, pair] + ds_t

    @pl.when(i == pl.num_programs(1) - 1)
    def _():
        for d in range(2):
            for hd in range(GLA_HEADS):
                t = st_ref[d, hd].T
                sfin_ref[0, d, hd] = t[(hd % 2) * GLA_DK:(hd % 2 + 1) * GLA_DK, :]


def _gla(lq, lk, lv, la, s0, n_batch, n_tok, tb):
    nb = n_tok // tb
    t_total = n_batch * n_tok
    has_s0 = s0 is not None
    fwd = lambda b, i: (b * nb + i, 0)
    bwd = lambda b, i: (b * nb + nb - 1 - i, 0)
    bwd_la = lambda b, i: (b * nb + nb - 1 - i, 1)
    in_specs = [
        pl.BlockSpec((tb, GLA_W), fwd), pl.BlockSpec((tb, GLA_W), fwd),
        pl.BlockSpec((tb, BRANCH_W), fwd), pl.BlockSpec((tb, GLA_W), fwd),
        pl.BlockSpec((tb, GLA_W), bwd), pl.BlockSpec((tb, GLA_W), bwd),
        pl.BlockSpec((tb, BRANCH_W), bwd), pl.BlockSpec((tb, GLA_W), bwd_la),
    ]
    args = [lq, lk, lv, la, lq, lk, lv, la]
    state_spec = pl.BlockSpec((1, 2, GLA_HEADS, GLA_DK, GLA_DV), lambda b, i: (b, 0, 0, 0, 0))
    if has_s0:
        in_specs.append(state_spec)
        args.append(s0)
    return pl.pallas_call(
        functools.partial(_gla_kernel, has_s0=has_s0, n_chunk=tb // GLA_CHUNK),
        grid=(n_batch, nb),
        in_specs=in_specs,
        out_specs=[pl.BlockSpec((tb, BRANCH_W), fwd), pl.BlockSpec((tb, BRANCH_W), bwd), state_spec],
        out_shape=[
            jax.ShapeDtypeStruct((t_total, BRANCH_W), BF16),
            jax.ShapeDtypeStruct((t_total, BRANCH_W), BF16),
            jax.ShapeDtypeStruct((n_batch, 2, GLA_HEADS, GLA_DK, GLA_DV), F32),
        ],
        scratch_shapes=[pltpu.VMEM((2, GLA_HEADS, GLA_DV, LANES), F32)],
        compiler_params=pltpu.CompilerParams(
            dimension_semantics=("parallel", "arbitrary"), vmem_limit_bytes=VMEM_LIMIT),
        name="gla",
    )(*args)


def _merge_kernel(x_ref, mod_ref, og_ref, of_ref, ob_ref, lgo_ref, od_ref, gate_ref,
                  gout_ref, wb_ref, wo_ref, y_ref):
    o_gla = of_ref[...].astype(F32) + ob_ref[...].astype(F32)
    gla_parts = []
    for hd in range(GLA_HEADS):
        blk = o_gla[:, hd * GLA_DV:(hd + 1) * GLA_DV]
        ms = jnp.mean(blk * blk, axis=-1, keepdims=True)
        gla_parts.append(blk * lax.rsqrt(ms + EPS) * gout_ref[...])
    gla = jnp.concatenate(gla_parts, axis=1) * _silu(lgo_ref[...].astype(F32))
    branches = (og_ref[...], gla.astype(BF16), od_ref[...])
    mixed = None
    for b, ob in enumerate(branches):
        y = jnp.dot(ob, wb_ref[b], preferred_element_type=F32)
        y = y * gate_ref[:, b * D_MODEL:(b + 1) * D_MODEL].astype(F32)
        mixed = y if mixed is None else mixed + y
    out = jnp.dot(mixed.astype(BF16), wo_ref[...], preferred_element_type=F32)
    y_ref[...] = x_ref[...] + mod_ref[0, 2:3, :] * out


def _merge(x2d, mod, og, o_f, o_b, lgo, od, gates, lw, sw, l, tiles_per_mod, tm):
    t_total = x2d.shape[0]
    tile = lambda c: pl.BlockSpec((tm, c), lambda i: (i, 0))
    return pl.pallas_call(
        _merge_kernel,
        grid=(t_total // tm,),
        in_specs=[
            tile(D_MODEL),
            pl.BlockSpec((1, 6, D_MODEL), lambda i: (i // tiles_per_mod, 0, 0)),
            tile(BRANCH_W), tile(BRANCH_W), tile(BRANCH_W), tile(BRANCH_W), tile(BRANCH_W),
            tile(N_BRANCH * D_MODEL),
            _full_spec((1, GLA_DV)),
            _layer_spec((N_BRANCH, BRANCH_W, D_MODEL), l),
            _layer_spec((D_MODEL, D_MODEL), l),
        ],
        out_specs=tile(D_MODEL),
        out_shape=jax.ShapeDtypeStruct((t_total, D_MODEL), F32),
        compiler_params=pltpu.CompilerParams(
            dimension_semantics=("parallel",), vmem_limit_bytes=VMEM_LIMIT),
        name="merge",
    )(x2d, mod, og, o_f, o_b, lgo, od, gates, lw["gout"], sw["w_branch"], sw["w_out"])


FFN_CHUNK = 256


def _ffn_kernel(x_ref, mod_ref, g2_ref, wi_ref, wo_ref, y_ref):
    x = x_ref[...]
    ms = jnp.mean(x * x, axis=-1, keepdims=True)
    h = x * lax.rsqrt(ms + EPS) * g2_ref[...]
    h = h * (1.0 + mod_ref[0, 4:5, :]) + mod_ref[0, 3:4, :]
    hb = h.astype(BF16)
    def up_proj(c0):
        a = jnp.dot(hb, wi_ref[:, c0:c0 + FFN_CHUNK], preferred_element_type=F32)
        u = jnp.dot(hb, wi_ref[:, FFN_HIDDEN + c0:FFN_HIDDEN + c0 + FFN_CHUNK], preferred_element_type=F32)
        return a, u

    acc = None
    chunks = list(range(0, FFN_HIDDEN, FFN_CHUNK))
    nxt = up_proj(chunks[0])
    for i, c0 in enumerate(chunks):
        a, u = nxt
        if i + 1 < len(chunks):
            nxt = up_proj(chunks[i + 1])
        act = (_silu(a) * u).astype(BF16)
        part = jnp.dot(act, wo_ref[c0:c0 + FFN_CHUNK, :], preferred_element_type=F32)
        acc = part if acc is None else acc + part
    y_ref[...] = x + mod_ref[0, 5:6, :] * acc


def _ffn(x2d, mod, lw, sw, l, tiles_per_mod, tm):
    t_total = x2d.shape[0]
    tile = lambda c: pl.BlockSpec((tm, c), lambda i: (i, 0))
    return pl.pallas_call(
        _ffn_kernel,
        grid=(t_total // tm,),
        in_specs=[
            tile(D_MODEL),
            pl.BlockSpec((1, 6, D_MODEL), lambda i: (i // tiles_per_mod, 0, 0)),
            _full_spec((1, D_MODEL)),
            _layer_spec((D_MODEL, 2 * FFN_HIDDEN), l),
            _layer_spec((FFN_HIDDEN, D_MODEL), l),
        ],
        out_specs=tile(D_MODEL),
        out_shape=jax.ShapeDtypeStruct((t_total, D_MODEL), F32),
        compiler_params=pltpu.CompilerParams(
            dimension_semantics=("parallel",), vmem_limit_bytes=VMEM_LIMIT),
        name="ffn",
    )(x2d, mod, lw["norm2"], sw["w_ffn_in"], sw["w_ffn_out"])


def _shared_weights(p):
    w = p["w_in"]
    pad = jnp.zeros(w.shape[:2] + (N_A - SPLIT_AB,), BF16)
    return {
        "w_a": jnp.concatenate([w[:, :, :SPLIT_AB].astype(BF16), pad], axis=2),
        "w_b": w[:, :, SPLIT_AB:].astype(BF16),
        "w_branch": p["w_branch"].astype(BF16), "w_out": p["w_out"].astype(BF16),
        "w_ffn_in": p["w_ffn_in"].astype(BF16), "w_ffn_out": p["w_ffn_out"].astype(BF16),
    }


def _layer_weights(l, p):
    scale = HEAD_DIM ** -0.5 * math.log2(math.e)
    head_bound = lambda g, s: HEAD_DIM * (NORM_MARGIN * s) ** 2 * jnp.max(g * g)
    norm_bounds = lambda gq_, gk_: jnp.stack([head_bound(gq_, scale), head_bound(gk_, 1.0)])
    gq_row = jnp.concatenate([jnp.tile(p["gqa_q_norm"][l], GQA_HEADS),
                              jnp.tile(p["diff_q_norm"][l], 2 * DIFF_HEADS)]) * scale
    gk_row = jnp.concatenate([jnp.tile(p["gqa_k_norm"][l], GQA_KV_HEADS),
                              jnp.tile(p["diff_k_norm"][l], 2 * DIFF_HEADS)])
    idx = jnp.arange(2 * LANES) // HEAD_DIM
    gmat = jnp.where(idx[:, None] == idx[None, :], 1.0 / HEAD_DIM, 0.0).astype(BF16)
    up = jnp.zeros((LANES, 2 * GLA_W), F32)
    up = up.at[0:GLA_RANK, 0:GLA_W].set(p["gla_alpha_up"][l, 0])
    up = up.at[GLA_RANK:2 * GLA_RANK, GLA_W:].set(p["gla_alpha_up"][l, 1])
    ub = p["gla_alpha_bias"][l].reshape(1, 2 * GLA_W)
    up_hi = up.astype(BF16)
    up = jnp.stack([up_hi, (up - up_hi.astype(F32)).astype(BF16)])
    return {
        "gq": gq_row.reshape(1, 2 * N_QG), "gk": gk_row.reshape(1, LANES + BRANCH_W), "gmat": gmat,
        "up": up, "ub": ub,
        "kb_g": norm_bounds(p["gqa_q_norm"][l], p["gqa_k_norm"][l]),
        "kb_d": norm_bounds(p["diff_q_norm"][l], p["diff_k_norm"][l]),
        "norm1": p["norm1"][l].reshape(1, D_MODEL), "norm2": p["norm2"][l].reshape(1, D_MODEL),
        "gout": p["gla_out_norm"][l].reshape(1, GLA_DV),
        "gsub": p["diff_sub_norm"][l].reshape(1, DIFF_DV),
        "lam": p["diff_lambda"][l],
    }


def _rope_tables(n_tokens):
    n_rows = n_tokens // GRID_W
    row = jnp.repeat(jnp.arange(n_rows, dtype=F32), GRID_W)
    col = jnp.tile(jnp.arange(GRID_W, dtype=F32), n_rows)
    n_freq = HEAD_DIM // 4
    freqs = ROPE_THETA ** (-jnp.arange(n_freq, dtype=F32) / n_freq)
    ar, ac = row[:, None] * freqs, col[:, None] * freqs
    cos = jnp.concatenate([jnp.cos(ar), jnp.cos(ar), jnp.cos(ac), jnp.cos(ac)], axis=1)
    sin = jnp.concatenate([-jnp.sin(ar), jnp.sin(ar), -jnp.sin(ac), jnp.sin(ac)], axis=1)
    return jnp.tile(cos, (1, LANES // HEAD_DIM)), jnp.tile(sin, (1, LANES // HEAD_DIM))


def _with_ones_rows(v_t):
    lead = v_t.shape[:-2]
    s = v_t.shape[-1]
    ones = jnp.ones(lead + (1, s), v_t.dtype)
    zeros = jnp.zeros(lead + (BF16_ROWS - 1, s), v_t.dtype)
    return jnp.concatenate([v_t, ones, zeros], axis=-2).astype(BF16)


def _cache_layouts(l, cache_gqa_k, cache_gqa_v, cache_diff_k, cache_diff_v):
    b = cache_gqa_k.shape[0]
    gk = jnp.transpose(cache_gqa_k[:, l], (0, 2, 1, 3))
    kd_g = jnp.concatenate([gk, gk], axis=-1).astype(BF16)
    vt_g = _with_ones_rows(jnp.transpose(cache_gqa_v[:, l], (0, 2, 3, 1)))
    dk = jnp.transpose(cache_diff_k[:, l], (0, 2, 3, 1, 4)).reshape(b, 2 * DIFF_HEADS, PAST_LEN, HEAD_DIM)
    kd_d = jnp.concatenate([dk, dk], axis=-1).astype(BF16)
    vt_d = _with_ones_rows(jnp.transpose(cache_diff_v[:, l], (0, 2, 3, 1)))
    return kd_g, vt_g, kd_d, vt_d


def _run_layer(x2d, mod, lw, sw, l, n_batch, n_tok, rope_tabs, ctx, emit_cache, cfg):
    tiles_per_mod = lambda tm: (n_tok // tm) if mod.shape[0] > 1 else (n_batch * n_tok // tm)
    outs = _in_projection(x2d, mod, lw, sw, l, n_batch, n_tok, rope_tabs, emit_cache, cfg["tm_in"])
    qg, qd, kdg, kdd, vtg, vtd, lq, lk, lv, lgo, la, gates = outs[:12]
    if ctx is None:
        kd_gc = vt_gc = kd_dc = vt_dc = s0 = None
    else:
        kd_gc, vt_gc, kd_dc, vt_dc, s0 = ctx
    og = _gqa_attention(qg, lw["kb_g"], kdg, vtg, kd_gc, vt_gc, n_batch, n_tok, cfg["tq_gqa"], cfg["tk"])
    lam_init = 0.8 - 0.6 * math.exp(-0.3 * l)
    od = _diff_attention(qd, lw["kb_d"], lw["lam"], lw["gsub"], kdd, vtd, kd_dc, vt_dc, n_batch, n_tok,
                         cfg["tq_diff"], cfg["tk"], lam_init)
    o_f, o_b, s_fin = _gla(lq, lk, lv, la, s0, n_batch, n_tok, cfg["tb_gla"])
    x2d = _merge(x2d, mod, og, o_f, o_b, lgo, od, gates, lw, sw, l, tiles_per_mod(cfg["tm"]), cfg["tm"])
    x2d = _ffn(x2d, mod, lw, sw, l, tiles_per_mod(cfg["tm"]), cfg["tm"])
    cache = tuple(outs[12:]) + (s_fin,) if emit_cache else None
    return x2d, cache


PROMPT_CFG = dict(tm_in=256, tq_gqa=256, tq_diff=256, tk=256, tb_gla=256, tm=512)
SAMPLE_CFG = dict(tm_in=512, tq_gqa=1024, tq_diff=2048, tk=512, tb_gla=256, tm=512)


def kernel(x_prompt, x_sample, c, cache_gqa_k, cache_gqa_v, state_gla, cache_diff_k, cache_diff_v, c_ctx, w_mod, b_mod, norm1, norm2, w_in, gqa_q_norm, gqa_k_norm, gla_alpha_up, gla_alpha_bias, gla_out_norm, diff_q_norm, diff_k_norm, diff_lambda, diff_sub_norm, w_branch, w_out, w_ffn_in, w_ffn_out):
    p = {
        "norm1": norm1, "norm2": norm2, "w_in": w_in, "gqa_q_norm": gqa_q_norm, "gqa_k_norm": gqa_k_norm,
        "gla_alpha_up": gla_alpha_up, "gla_alpha_bias": gla_alpha_bias, "gla_out_norm": gla_out_norm,
        "diff_q_norm": diff_q_norm, "diff_k_norm": diff_k_norm, "diff_lambda": diff_lambda,
        "diff_sub_norm": diff_sub_norm, "w_branch": w_branch, "w_out": w_out,
        "w_ffn_in": w_ffn_in, "w_ffn_out": w_ffn_out,
    }
    n_ctx_b, n_ctx = x_prompt.shape[:2]
    n_lat_b, n_lat = x_sample.shape[:2]
    cond_rows = jnp.concatenate(
        [c_ctx[None, :], c, jnp.zeros((8 - 1 - n_lat_b, D_MODEL), F32)], axis=0)
    mod_all = _modulation(cond_rows, w_mod, b_mod)
    weights = [_layer_weights(l, p) for l in range(DEPTH)]
    shared = _shared_weights(p)

    y = x_prompt.reshape(n_ctx_b * n_ctx, D_MODEL)
    caches = []
    for l in range(DEPTH):
        mod = mod_all[l, 0:1].reshape(1, 6, D_MODEL)
        y, cache = _run_layer(y, mod, weights[l], shared, l, n_ctx_b, n_ctx, None, None, True, PROMPT_CFG)
        caches.append(cache)
    y_prompt = y.reshape(n_ctx_b, n_ctx, D_MODEL)
    stack = lambda j, shape: jnp.stack([cc[j].reshape(shape) for cc in caches], axis=1)
    new_gqa_k = stack(0, (n_ctx_b, n_ctx, GQA_KV_HEADS, HEAD_DIM))
    new_diff_k = stack(1, (n_ctx_b, n_ctx, DIFF_HEADS, 2, HEAD_DIM))
    new_gqa_v = stack(2, (n_ctx_b, n_ctx, GQA_KV_HEADS, HEAD_DIM))
    new_diff_v = stack(3, (n_ctx_b, n_ctx, DIFF_HEADS, DIFF_DV))
    new_state_gla = jnp.stack([cc[4] for cc in caches], axis=1)

    rope_tabs = _rope_tables(n_lat)
    y = x_sample.reshape(n_lat_b * n_lat, D_MODEL)
    for l in range(DEPTH):
        mod = mod_all[l, 1:1 + n_lat_b].reshape(n_lat_b, 6, D_MODEL)
        ctx = _cache_layouts(l, cache_gqa_k, cache_gqa_v, cache_diff_k, cache_diff_v) + (state_gla[:, l],)
        y, _ = _run_layer(y, mod, weights[l], shared, l, n_lat_b, n_lat, rope_tabs, ctx, False, SAMPLE_CFG)
    y_sample = y.reshape(n_lat_b, n_lat, D_MODEL)
    return (y_prompt, y_sample, new_gqa_k, new_gqa_v, new_state_gla, new_diff_k, new_diff_v)
```

```python
import functools
import math

import jax
import jax.numpy as jnp
from jax import lax
from jax.experimental import pallas as pl
from jax.experimental.pallas import tpu as pltpu

D_MODEL = 1024
DEPTH = 2
PAST_LEN = 512
GRID_W = 64
HEAD_DIM = 64
GQA_HEADS = 8
GQA_KV_HEADS = 2
GQA_GROUP = GQA_HEADS // GQA_KV_HEADS
GLA_HEADS = 4
GLA_DK = 64
GLA_DV = 128
GLA_RANK = 16
GLA_TAU = 16.0
GLA_CHUNK = 64
DIFF_HEADS = 4
DIFF_DV = 2 * HEAD_DIM
N_BRANCH = 3
BRANCH_W = 512
FFN_HIDDEN = ((8 * D_MODEL + 3 * 256 - 1) // (3 * 256)) * 256
ROPE_THETA = 10000.0
EPS = 1e-6

LANES = 128
BF16_ROWS = 16
VMEM_LIMIT = 56 * 1024 * 1024

F32 = jnp.float32
BF16 = jnp.bfloat16
HIGHEST = lax.Precision.HIGHEST

A_GQ = 0
A_KV = A_GQ + GQA_HEADS * HEAD_DIM
A_GLA = A_KV + 2 * GQA_KV_HEADS * HEAD_DIM
SPLIT_AB = A_GLA + 2 * GLA_HEADS * GLA_DK + GLA_HEADS * GLA_DV + 2 * GLA_RANK
N_A = -(-SPLIT_AB // LANES) * LANES
B_LGO = 0
B_DQ = B_LGO + GLA_HEADS * GLA_DV
B_DK = B_DQ + DIFF_HEADS * 2 * HEAD_DIM
B_DV = B_DK + DIFF_HEADS * 2 * HEAD_DIM
B_MG = B_DV + DIFF_HEADS * DIFF_DV
N_B = B_MG + N_BRANCH * D_MODEL
N_QG = GQA_HEADS * HEAD_DIM
GLA_W = GLA_HEADS * GLA_DK
GQA_VT_ROWS = HEAD_DIM + BF16_ROWS
DIFF_VT_ROWS = DIFF_DV + BF16_ROWS


def _sigmoid(x):
    return 1.0 / (1.0 + jnp.exp(-x))


def _silu(x):
    return x * _sigmoid(x)


def _bf16_terms(x):
    hi = x.astype(BF16)
    r1 = x - hi.astype(F32)
    mid = r1.astype(BF16)
    lo = (r1 - mid.astype(F32)).astype(BF16)
    return hi, mid, lo


def _full_spec(shape):
    n = len(shape)
    return pl.BlockSpec(shape, lambda *_: (0,) * n, pipeline_mode=pl.Buffered(1))


def _layer_spec(shape, l):
    n = len(shape)
    return pl.BlockSpec((None,) + tuple(shape), lambda *_: (l,) + (0,) * n, pipeline_mode=pl.Buffered(1))


def _mod_kernel(cond_ref, w_ref, b_ref, o_ref):
    s = _silu(cond_ref[...])
    o_ref[0] = jnp.dot(s, w_ref[0], precision=HIGHEST, preferred_element_type=F32) + b_ref[0]


def _modulation(cond_rows, w_mod, b_mod):
    tn = 1536
    n_out = 6 * D_MODEL
    return pl.pallas_call(
        _mod_kernel,
        grid=(DEPTH, n_out // tn),
        in_specs=[
            pl.BlockSpec((8, D_MODEL), lambda l, j: (0, 0)),
            pl.BlockSpec((1, D_MODEL, tn), lambda l, j: (l, 0, j)),
            pl.BlockSpec((1, 1, tn), lambda l, j: (l, 0, j)),
        ],
        out_specs=pl.BlockSpec((1, 8, tn), lambda l, j: (l, 0, j)),
        out_shape=jax.ShapeDtypeStruct((DEPTH, 8, n_out), F32),
        compiler_params=pltpu.CompilerParams(
            dimension_semantics=("parallel", "parallel"), vmem_limit_bytes=VMEM_LIMIT),
        name="modulation",
    )(cond_rows, w_mod, b_mod.reshape(DEPTH, 1, n_out))


def _head_norm(z, gain, gmat):
    outs = []
    n = z.shape[1]
    for c0 in range(0, n, 2 * LANES):
        w = min(2 * LANES, n - c0)
        zz = z[:, c0:c0 + w]
        ms = jnp.dot((zz * zz).astype(BF16), gmat[0:w, 0:w], preferred_element_type=F32)
        outs.append(zz * lax.rsqrt(ms + EPS) * gain[:, c0:c0 + w])
    return outs


def _rope_block(zb, cos, sin, first):
    partner = jnp.where(first, pltpu.roll(zb, LANES - 16, 1), pltpu.roll(zb, 16, 1))
    return zb * cos + partner * sin


N_IN_BASE = 10


def _in_kernel(*refs, rope, emit_cache, tiles_per_batch, cache_tiles):
    if not cache_tiles:
        _in_tile(refs, rope, emit_cache)
        return
    kcg_ref, kcd_ref, vcg_ref, vcd_ref = refs[:4]
    j = pl.program_id(1)

    @pl.when(j < tiles_per_batch)
    def _():
        _in_tile(refs[4:], rope, emit_cache)

    @pl.when(j >= tiles_per_batch)
    def _():
        n_in = 4 + N_IN_BASE + (2 if rope else 0)
        kdg_ref, kdd_ref, vtg_ref, vtd_ref = refs[n_in + 2:n_in + 6]
        kdg_ref[...] = kcg_ref[...]
        kdd_ref[...] = kcd_ref[...]
        vtg_ref[...] = vcg_ref[...]
        vtd_ref[...] = vcd_ref[...]


def _in_tile(refs, rope, emit_cache):
    it = iter(refs)
    (x_ref, mod_ref, g1_ref, wa_ref, wb_ref, gq_ref, gk_ref, gmat_ref, up_ref, ub_ref) = (
        next(it) for _ in range(N_IN_BASE))
    if rope:
        cos_ref, sin_ref = next(it), next(it)
    (qg_ref, qd_ref, kdg_ref, kdd_ref, vtg_ref, vtd_ref,
     lq_ref, lk_ref, lv_ref, lgo_ref, la_ref, gate_ref) = (next(it) for _ in range(12))
    if emit_cache:
        ck_g_ref, ck_d_ref, cv_g_ref, cv_d_ref = (next(it) for _ in range(4))

    tm = x_ref.shape[0]
    x = x_ref[...]
    ms = jnp.mean(x * x, axis=-1, keepdims=True)
    h = x * lax.rsqrt(ms + EPS) * g1_ref[...]
    h = h * (1.0 + mod_ref[0, 1:2, :]) + mod_ref[0, 0:1, :]
    hb = h.astype(BF16)

    def proj(w_ref, c0, n):
        return jnp.dot(hb, w_ref[:, c0:c0 + n], preferred_element_type=F32)

    lane = lax.broadcasted_iota(jnp.int32, (tm, LANES), 1)
    first16 = (lane % 32) < 16
    lo_half = lane < HEAD_DIM
    gmat = gmat_ref[...]
    if rope:
        cos, sin = cos_ref[...], sin_ref[...]

    def blocks128(parts):
        out = []
        for p in parts:
            for c in range(0, p.shape[1], LANES):
                out.append(p[:, c:c + LANES])
        return out

    def store_queries(z, gain, dst):
        for j, zb in enumerate(blocks128(_head_norm(z, gain, gmat))):
            if rope:
                zb = _rope_block(zb, cos, sin, first16)
            dst[:, j * LANES:(j + 1) * LANES] = zb.astype(BF16)

    def store_keys(z, gain, dst, cache_dst):
        for j, zb in enumerate(blocks128(_head_norm(z, gain, gmat))):
            if rope:
                zb = _rope_block(zb, cos, sin, first16)
            if emit_cache:
                cache_dst[:, j * LANES:(j + 1) * LANES] = zb
            sw = pltpu.roll(zb, HEAD_DIM, 1)
            dst[0, 2 * j] = jnp.where(lo_half, zb, sw).astype(BF16)
            dst[0, 2 * j + 1] = jnp.where(lo_half, sw, zb).astype(BF16)

    z_gq = proj(wa_ref, A_GQ, N_QG)
    z_dq = proj(wb_ref, B_DQ, N_QG)
    store_queries(z_gq, gq_ref[:, 0:N_QG], qg_ref)
    z_kv = proj(wa_ref, A_KV, 2 * LANES)
    store_queries(z_dq, gq_ref[:, N_QG:2 * N_QG], qd_ref)
    z_dk = proj(wb_ref, B_DK, BRANCH_W)
    store_keys(z_kv[:, 0:LANES], gk_ref[:, 0:LANES], kdg_ref, ck_g_ref if emit_cache else None)
    z_dv = proj(wb_ref, B_DV, BRANCH_W)
    store_keys(z_dk, gk_ref[:, LANES:LANES + BRANCH_W], kdd_ref, ck_d_ref if emit_cache else None)
    zl = proj(wa_ref, A_GLA, N_A - A_GLA)

    ones_rows = jnp.where(lax.broadcasted_iota(jnp.int32, (BF16_ROWS, tm), 0) == 0, 1.0, 0.0).astype(BF16)
    z_gv = z_kv[:, LANES:2 * LANES]
    if emit_cache:
        cv_g_ref[...] = z_gv
        cv_d_ref[...] = z_dv
    vt = z_gv.T
    for hh in range(GQA_KV_HEADS):
        vtg_ref[0, hh, 0:HEAD_DIM, :] = vt[hh * HEAD_DIM:(hh + 1) * HEAD_DIM, :].astype(BF16)
        vtg_ref[0, hh, HEAD_DIM:GQA_VT_ROWS, :] = ones_rows
    for hh in range(DIFF_HEADS):
        vt = z_dv[:, hh * LANES:(hh + 1) * LANES].T
        vtd_ref[0, hh, 0:DIFF_DV, :] = vt.astype(BF16)
        vtd_ref[0, hh, DIFF_DV:DIFF_VT_ROWS, :] = ones_rows

    z_lgo = proj(wb_ref, B_LGO, BRANCH_W)
    o = 0
    lq_ref[...] = (zl[:, o:o + GLA_W] * (GLA_DK ** -0.5)).astype(BF16)
    o += GLA_W
    lk_ref[...] = zl[:, o:o + GLA_W].astype(BF16)
    o += GLA_W
    lv_ref[...] = zl[:, o:o + BRANCH_W].astype(BF16)
    o += BRANCH_W
    lgo_ref[...] = z_lgo.astype(BF16)
    lr = zl[:, o:o + LANES]
    lr_hi, lr_mid, _ = _bf16_terms(lr)
    z = (jnp.dot(lr_hi, up_ref[0], preferred_element_type=F32)
         + jnp.dot(lr_mid, up_ref[0], preferred_element_type=F32)
         + jnp.dot(lr_hi, up_ref[1], preferred_element_type=F32)) + ub_ref[...]
    log_sig = jnp.minimum(z, 0.0) - jnp.log(1.0 + jnp.exp(-jnp.abs(z)))
    la_ref[...] = log_sig * (1.0 / GLA_TAU)

    for b in range(N_BRANCH):
        gate_ref[:, b * D_MODEL:(b + 1) * D_MODEL] = _sigmoid(
            proj(wb_ref, B_MG + b * D_MODEL, D_MODEL)).astype(BF16)


def _in_projection(x2d, mod, lw, sw, l, n_batch, n_tok, cache_kv, rope_tabs, emit_cache, tm):
    n_cache = 0 if cache_kv is None else cache_kv[0].shape[2]
    assert n_cache % tm == 0
    n_keys = n_tok + n_cache
    t_total = n_batch * n_tok
    tpb = n_tok // tm
    cache_tiles = n_cache // tm
    rope = rope_tabs is not None
    own = lambda j: jnp.minimum(j, tpb - 1)
    tile = lambda c: pl.BlockSpec((tm, c), lambda b, j: (b * tpb + own(j), 0))
    in_specs, args = [], []
    if cache_tiles:
        past = lambda j: jnp.maximum(j - tpb, 0)
        in_specs += [
            pl.BlockSpec((1, GQA_KV_HEADS, tm, LANES), lambda b, j: (b, 0, past(j), 0)),
            pl.BlockSpec((1, 2 * DIFF_HEADS, tm, LANES), lambda b, j: (b, 0, past(j), 0)),
            pl.BlockSpec((1, GQA_KV_HEADS, GQA_VT_ROWS, tm), lambda b, j: (b, 0, 0, past(j))),
            pl.BlockSpec((1, DIFF_HEADS, DIFF_VT_ROWS, tm), lambda b, j: (b, 0, 0, past(j))),
        ]
        args += list(cache_kv)
    in_specs += [
        tile(D_MODEL),
        pl.BlockSpec((1, 6, D_MODEL), lambda b, j: (b if mod.shape[0] > 1 else 0, 0, 0)),
        _full_spec((1, D_MODEL)),
        _layer_spec((D_MODEL, N_A), l),
        _layer_spec((D_MODEL, N_B), l),
        _full_spec((1, 2 * N_QG)),
        _full_spec((1, LANES + BRANCH_W)),
        _full_spec((2 * LANES, 2 * LANES)),
        _full_spec((2, LANES, 2 * GLA_W)),
        _full_spec((1, 2 * GLA_W)),
    ]
    args += [x2d, mod, lw["norm1"], sw["w_a"], sw["w_b"], lw["gq"], lw["gk"], lw["gmat"], lw["up"], lw["ub"]]
    if rope:
        in_specs += [pl.BlockSpec((tm, LANES), lambda b, j: (own(j), 0))] * 2
        args += list(rope_tabs)
    sd = jax.ShapeDtypeStruct
    out_shape = [
        sd((t_total, BRANCH_W), BF16), sd((t_total, BRANCH_W), BF16),
        sd((n_batch, GQA_KV_HEADS, n_keys, LANES), BF16), sd((n_batch, 2 * DIFF_HEADS, n_keys, LANES), BF16),
        sd((n_batch, GQA_KV_HEADS, GQA_VT_ROWS, n_keys), BF16), sd((n_batch, DIFF_HEADS, DIFF_VT_ROWS, n_keys), BF16),
        sd((t_total, GLA_W), BF16), sd((t_total, GLA_W), BF16),
        sd((t_total, BRANCH_W), BF16), sd((t_total, BRANCH_W), BF16),
        sd((t_total, 2 * GLA_W), F32), sd((t_total, N_BRANCH * D_MODEL), BF16),
    ]
    out_specs = [
        tile(BRANCH_W), tile(BRANCH_W),
        pl.BlockSpec((1, GQA_KV_HEADS, tm, LANES), lambda b, j: (b, 0, j, 0)),
        pl.BlockSpec((1, 2 * DIFF_HEADS, tm, LANES), lambda b, j: (b, 0, j, 0)),
        pl.BlockSpec((1, GQA_KV_HEADS, GQA_VT_ROWS, tm), lambda b, j: (b, 0, 0, j)),
        pl.BlockSpec((1, DIFF_HEADS, DIFF_VT_ROWS, tm), lambda b, j: (b, 0, 0, j)),
        tile(GLA_W), tile(GLA_W), tile(BRANCH_W), tile(BRANCH_W), tile(2 * GLA_W), tile(N_BRANCH * D_MODEL),
    ]
    if emit_cache:
        out_shape += [sd((t_total, LANES), F32), sd((t_total, BRANCH_W), F32),
                      sd((t_total, LANES), F32), sd((t_total, BRANCH_W), F32)]
        out_specs += [tile(LANES), tile(BRANCH_W), tile(LANES), tile(BRANCH_W)]
    return pl.pallas_call(
        functools.partial(_in_kernel, rope=rope, emit_cache=emit_cache, tiles_per_batch=tpb,
                          cache_tiles=cache_tiles),
        grid=(n_batch, tpb + cache_tiles),
        in_specs=in_specs,
        out_specs=out_specs,
        out_shape=out_shape,
        compiler_params=pltpu.CompilerParams(
            dimension_semantics=("parallel", "arbitrary"), vmem_limit_bytes=VMEM_LIMIT),
        name="in_projection",
    )(*args)


MAX_COL = 4 * LANES
NT_DIMS = (((1,), (1,)), ((), ()))
SCORE_BOUND = 45.0
NORM_MARGIN = 1.05


def _masked_queries(q_blk, keep_low):
    lane = lax.broadcasted_iota(jnp.int32, q_blk.shape, 1)
    keep = (lane < HEAD_DIM) if keep_low else (lane >= HEAD_DIM)
    return jnp.where(keep, q_blk.astype(F32), 0.0).astype(BF16)


def _dup_key_norm2(kd):
    kf = kd.astype(F32)
    return 0.5 * jnp.max(jnp.sum(kf * kf, axis=1, keepdims=True))


def _needs_no_stabiliser(q_norm2, k_norm2):
    return (q_norm2 * k_norm2 <= SCORE_BOUND * SCORE_BOUND).astype(jnp.int32)


KEY_SUB = 512


def _score_units(k_ref, heads_cols):
    tk = k_ref.shape[2]
    sub = min(KEY_SUB, tk)
    return sub, [(hd, s0, c) for s0 in range(0, tk, sub) for hd, cols in heads_cols for c in cols]


def _tile_scores(k_ref, qm_ref, unit, sub, col):
    hd, s0, c = unit
    return lax.dot_general(k_ref[0, hd, s0:s0 + sub, :], qm_ref[c * col:(c + 1) * col, :], NT_DIMS,
                           preferred_element_type=F32)


def _plain_tiles(k_ref, vt_ref, qm_ref, acc_ref, heads_cols):
    col = acc_ref.shape[-1]
    sub, units = _score_units(k_ref, heads_cols)
    s_next = _tile_scores(k_ref, qm_ref, units[0], sub, col)
    for i, (hd, s0, c) in enumerate(units):
        s = s_next
        if i + 1 < len(units):
            s_next = _tile_scores(k_ref, qm_ref, units[i + 1], sub, col)
        p = jnp.exp2(s).astype(BF16)
        acc_ref[c] += jnp.dot(vt_ref[0, 0, :, s0:s0 + sub], p, preferred_element_type=F32)


def _online_tiles(k_ref, vt_ref, qm_ref, m_ref, acc_ref, heads_cols):
    col = acc_ref.shape[-1]
    sub, units = _score_units(k_ref, heads_cols)
    for unit in units:
        hd, s0, c = unit
        s = _tile_scores(k_ref, qm_ref, unit, sub, col)
        m_prev = m_ref[c]
        m_new = jnp.maximum(m_prev, jnp.max(s, axis=0, keepdims=True))
        alpha = jnp.exp2(m_prev - m_new)
        p = jnp.exp2(s - m_new).astype(BF16)
        acc_ref[c] = acc_ref[c] * alpha + jnp.dot(vt_ref[0, 0, :, s0:s0 + sub], p, preferred_element_type=F32)
        m_ref[c] = m_new


def _gqa_kernel(*refs, has_cache, tq):
    if has_cache:
        kb_ref, q_ref, k_ref, vt_ref, kc_ref, o_ref, qm_ref, m_ref, acc_ref, plain_ref = refs
    else:
        kb_ref, q_ref, k_ref, vt_ref, o_ref, qm_ref, m_ref, acc_ref, plain_ref = refs
    kt = pl.program_id(3)
    col = acc_ref.shape[-1]
    per_head = tq // col
    heads_cols = [(0, range(GQA_GROUP * per_head))]

    @pl.when(kt == 0)
    def _():
        for g in range(GQA_GROUP):
            blk = q_ref[0, :, (g // 2) * LANES:(g // 2 + 1) * LANES]
            qm_ref[g * tq:(g + 1) * tq, :] = _masked_queries(blk, g % 2 == 0)
        k_norm2 = kb_ref[1]
        if has_cache:
            k_norm2 = jnp.maximum(k_norm2, _dup_key_norm2(kc_ref[0, 0]))
        plain_ref[0] = _needs_no_stabiliser(kb_ref[0], k_norm2)
        m_ref[...] = jnp.full(m_ref.shape, -jnp.inf, F32)
        acc_ref[...] = jnp.zeros(acc_ref.shape, F32)

    @pl.when(plain_ref[0] == 1)
    def _():
        _plain_tiles(k_ref, vt_ref, qm_ref, acc_ref, heads_cols)

    @pl.when(plain_ref[0] != 1)
    def _():
        _online_tiles(k_ref, vt_ref, qm_ref, m_ref, acc_ref, heads_cols)

    @pl.when(kt == pl.num_programs(3) - 1)
    def _():
        for part in range(per_head):
            heads = []
            for g in range(GQA_GROUP):
                acc = acc_ref[g * per_head + part]
                heads.append(acc[0:HEAD_DIM, :] / acc[HEAD_DIM:HEAD_DIM + 1, :])
            o_ref[0, part * col:(part + 1) * col, :] = jnp.concatenate(heads, axis=0).T.astype(BF16)


def _kv_specs(n_heads_blk, vt_rows, tk, kd_c):
    specs = [pl.BlockSpec((1, n_heads_blk, tk, LANES), lambda b, h, qi, kt: (b, h, kt, 0)),
             pl.BlockSpec((1, 1, vt_rows, tk), lambda b, h, qi, kt: (b, h, 0, kt))]
    if kd_c is not None:
        specs.append(pl.BlockSpec((1, n_heads_blk, kd_c.shape[2], LANES), lambda b, h, qi, kt: (b, h, 0, 0)))
    return specs


def _gqa_attention(q, k_bound, kd, vt, kd_c, n_batch, n_tok, tq, tk):
    has_cache = kd_c is not None
    nk = kd.shape[2] // tk
    rows = GQA_GROUP * tq
    col = min(MAX_COL, tq)
    in_specs = [pl.BlockSpec(memory_space=pltpu.SMEM),
                pl.BlockSpec((1, tq, 2 * LANES), lambda b, h, qi, kt: (b, qi, h))]
    in_specs += _kv_specs(1, GQA_VT_ROWS, tk, kd_c)
    args = [k_bound, q.reshape(n_batch, n_tok, BRANCH_W), kd, vt]
    if has_cache:
        args.append(kd_c)
    out = pl.pallas_call(
        functools.partial(_gqa_kernel, has_cache=has_cache, tq=tq),
        grid=(n_batch, GQA_KV_HEADS, n_tok // tq, nk),
        in_specs=in_specs,
        out_specs=pl.BlockSpec((1, tq, 2 * LANES), lambda b, h, qi, kt: (b, qi, h)),
        out_shape=jax.ShapeDtypeStruct((n_batch, n_tok, BRANCH_W), BF16),
        scratch_shapes=[
            pltpu.VMEM((rows, LANES), BF16),
            pltpu.VMEM((rows // col, 1, col), F32),
            pltpu.VMEM((rows // col, GQA_VT_ROWS, col), F32),
            pltpu.SMEM((1,), jnp.int32),
        ],
        compiler_params=pltpu.CompilerParams(
            dimension_semantics=("parallel", "parallel", "parallel", "arbitrary"),
            vmem_limit_bytes=VMEM_LIMIT),
        name="gqa_attention",
    )(*args)
    return out.reshape(n_batch * n_tok, BRANCH_W)


def _diff_kernel(*refs, has_cache, lam_init):
    if has_cache:
        (kb_ref, q_ref, lam_ref, gsub_ref, k_ref, vt_ref, kc_ref, o_ref,
         qm_ref, m_ref, acc_ref, plain_ref) = refs
    else:
        kb_ref, q_ref, lam_ref, gsub_ref, k_ref, vt_ref, o_ref, qm_ref, m_ref, acc_ref, plain_ref = refs
    kt = pl.program_id(3)
    tq = q_ref.shape[1]
    col = acc_ref.shape[-1]
    per_map = tq // col
    heads_cols = [(mm, range(mm * per_map, (mm + 1) * per_map)) for mm in range(2)]

    @pl.when(kt == 0)
    def _():
        blk = q_ref[0]
        for mm in range(2):
            qm_ref[mm * tq:(mm + 1) * tq, :] = _masked_queries(blk, mm == 0)
        k_norm2 = kb_ref[1]
        if has_cache:
            for mm in range(2):
                k_norm2 = jnp.maximum(k_norm2, _dup_key_norm2(kc_ref[0, mm]))
        plain_ref[0] = _needs_no_stabiliser(kb_ref[0], k_norm2)
        m_ref[...] = jnp.full(m_ref.shape, -jnp.inf, F32)
        acc_ref[...] = jnp.zeros(acc_ref.shape, F32)

    @pl.when(plain_ref[0] == 1)
    def _():
        _plain_tiles(k_ref, vt_ref, qm_ref, acc_ref, heads_cols)

    @pl.when(plain_ref[0] != 1)
    def _():
        _online_tiles(k_ref, vt_ref, qm_ref, m_ref, acc_ref, heads_cols)

    @pl.when(kt == pl.num_programs(3) - 1)
    def _():
        lp = lam_ref[...]
        lam = (jnp.exp(jnp.sum(lp[0:1] * lp[1:2], axis=-1, keepdims=True))
               - jnp.exp(jnp.sum(lp[2:3] * lp[3:4], axis=-1, keepdims=True)) + lam_init)
        for part in range(per_map):
            a0, a1 = acc_ref[part], acc_ref[per_map + part]
            o0 = a0[0:DIFF_DV, :] / a0[DIFF_DV:DIFF_DV + 1, :]
            o1 = a1[0:DIFF_DV, :] / a1[DIFF_DV:DIFF_DV + 1, :]
            d = (o0 - lam * o1).T
            ms = jnp.mean(d * d, axis=-1, keepdims=True)
            o_ref[0, part * col:(part + 1) * col, :] = (
                d * lax.rsqrt(ms + EPS) * gsub_ref[...] * (1.0 - lam_init)).astype(BF16)


def _diff_attention(q, k_bound, lam_p, gsub, kd, vt, kd_c, n_batch, n_tok, tq, tk, lam_init):
    has_cache = kd_c is not None
    nk = kd.shape[2] // tk
    col = min(MAX_COL, tq)
    in_specs = [
        pl.BlockSpec(memory_space=pltpu.SMEM),
        pl.BlockSpec((1, tq, LANES), lambda b, h, qi, kt: (b, qi, h)),
        pl.BlockSpec((4, HEAD_DIM), lambda b, h, qi, kt: (0, 0)),
        pl.BlockSpec((1, DIFF_DV), lambda b, h, qi, kt: (0, 0)),
    ]
    in_specs += _kv_specs(2, DIFF_VT_ROWS, tk, kd_c)
    args = [k_bound, q.reshape(n_batch, n_tok, BRANCH_W), lam_p, gsub, kd, vt]
    if has_cache:
        args.append(kd_c)
    out = pl.pallas_call(
        functools.partial(_diff_kernel, has_cache=has_cache, lam_init=lam_init),
        grid=(n_batch, DIFF_HEADS, n_tok // tq, nk),
        in_specs=in_specs,
        out_specs=pl.BlockSpec((1, tq, LANES), lambda b, h, qi, kt: (b, qi, h)),
        out_shape=jax.ShapeDtypeStruct((n_batch, n_tok, BRANCH_W), BF16),
        scratch_shapes=[
            pltpu.VMEM((2 * tq, LANES), BF16),
            pltpu.VMEM((2 * tq // col, 1, col), F32),
            pltpu.VMEM((2 * tq // col, DIFF_VT_ROWS, col), F32),
            pltpu.SMEM((1,), jnp.int32),
        ],
        compiler_params=pltpu.CompilerParams(
            dimension_semantics=("parallel", "parallel", "parallel", "arbitrary"),
            vmem_limit_bytes=VMEM_LIMIT),
        name="diff_attention",
    )(*args)
    return out.reshape(n_batch * n_tok, BRANCH_W)


def _gla_kernel(*refs, has_s0, n_chunk):
    if has_s0:
        qf, kf, vf, laf, qb, kb, vb, lab, s0_ref, of_ref, ob_ref, sfin_ref, st_ref = refs
    else:
        qf, kf, vf, laf, qb, kb, vb, lab, of_ref, ob_ref, sfin_ref, st_ref = refs
    i = pl.program_id(1)
    ck = GLA_CHUNK

    @pl.when(i == 0)
    def _():
        for d in range(2):
            for hd in range(GLA_HEADS):
                if has_s0:
                    s = s0_ref[0, d, hd]
                    z = jnp.zeros_like(s)
                    padded = jnp.concatenate([s, z] if hd % 2 == 0 else [z, s], axis=0)
                    st_ref[d, hd] = padded.T
                else:
                    st_ref[d, hd] = jnp.zeros((GLA_DV, LANES), F32)

    tb = qf.shape[0]
    r = lax.broadcasted_iota(jnp.int32, (tb, tb), 0)
    c = lax.broadcasted_iota(jnp.int32, (tb, tb), 1)
    lane = lax.broadcasted_iota(jnp.int32, (tb, LANES), 1)
    chunk_id = lax.broadcasted_iota(jnp.int32, (tb, GLA_W), 0) // ck
    zero_row = jnp.zeros((1, GLA_W), F32)
    streams = ((qf, kf, vf, laf, of_ref, c <= r), (qb, kb, vb, lab, ob_ref, c >= r))
    for d, (q_r, k_r, v_r, la_r, o_r, tri) in enumerate(streams):
        tri_b = tri.astype(F32).astype(BF16)
        g = sum(jnp.dot(tri_b, part, preferred_element_type=F32) for part in _bf16_terms(la_r[...]))
        if d == 0:
            bounds = [zero_row] + [g[ck * j - 1:ck * j, :] for j in range(1, n_chunk)]
            g_end = g[tb - 1:tb, :]
        else:
            bounds = [g[ck * (j + 1):ck * (j + 1) + 1, :] for j in range(n_chunk - 1)] + [zero_row]
            g_end = g[0:1, :]
        b_rows = jnp.concatenate([jnp.broadcast_to(b, (ck, GLA_W)) for b in bounds], axis=0)
        q = q_r[...].astype(F32)
        k = k_r[...].astype(F32)
        v = v_r[...]
        q_dec = q * jnp.exp(g - b_rows)
        q_glob = q * jnp.exp(g)
        k_end = k * jnp.exp(g_end - g)
        decay = jnp.exp(g_end)
        k_rel = []
        for j in range(n_chunk):
            reach = (chunk_id <= j) if d == 0 else (chunk_id >= j)
            k_rel.append(k * jnp.exp(jnp.where(reach, bounds[j] - g, 0.0)))
        for hd in range(GLA_HEADS):
            pair = slice((hd // 2) * LANES, (hd // 2 + 1) * LANES)
            keep = (lane < GLA_DK) if hd % 2 == 0 else (lane >= GLA_DK)
            qd_m = jnp.where(keep, q_dec[:, pair], 0.0).astype(BF16)
            qg_m = jnp.where(keep, q_glob[:, pair], 0.0).astype(BF16)
            ke_m = jnp.where(keep, k_end[:, pair], 0.0).astype(BF16)
            a_rows = [
                lax.dot_general(qd_m[ck * j:ck * (j + 1), :], k_rel[j][:, pair].astype(BF16),
                                (((1,), (1,)), ((), ())), preferred_element_type=F32)
                for j in range(n_chunk)
            ]
            a = jnp.where(tri, jnp.concatenate(a_rows, axis=0), 0.0).astype(BF16)
            hv = slice(hd * GLA_DV, (hd + 1) * GLA_DV)
            s_t = st_ref[d, hd]
            o = jnp.dot(a, v[:, hv], preferred_element_type=F32)
            o = o + lax.dot_general(qg_m, s_t.astype(BF16), (((1,), (1,)), ((), ())),
                                    preferred_element_type=F32)
            o_r[:, hv] = o.astype(BF16)
            ds_t = lax.dot_general(v[:, hv], ke_m, (((0,), (0,)), ((), ())), preferred_element_type=F32)
            st_ref[d, hd] = s_t * decay[:, pair] + ds_t

    @pl.when(i == pl.num_programs(1) - 1)
    def _():
        for d in range(2):
            for hd in range(GLA_HEADS):
                t = st_ref[d, hd].T
                sfin_ref[0, d, hd] = t[(hd % 2) * GLA_DK:(hd % 2 + 1) * GLA_DK, :]


def _gla(lq, lk, lv, la, s0, n_batch, n_tok, tb):
    nb = n_tok // tb
    t_total = n_batch * n_tok
    has_s0 = s0 is not None
    fwd = lambda b, i: (b * nb + i, 0)
    bwd = lambda b, i: (b * nb + nb - 1 - i, 0)
    bwd_la = lambda b, i: (b * nb + nb - 1 - i, 1)
    in_specs = [
        pl.BlockSpec((tb, GLA_W), fwd), pl.BlockSpec((tb, GLA_W), fwd),
        pl.BlockSpec((tb, BRANCH_W), fwd), pl.BlockSpec((tb, GLA_W), fwd),
        pl.BlockSpec((tb, GLA_W), bwd), pl.BlockSpec((tb, GLA_W), bwd),
        pl.BlockSpec((tb, BRANCH_W), bwd), pl.BlockSpec((tb, GLA_W), bwd_la),
    ]
    args = [lq, lk, lv, la, lq, lk, lv, la]
    state_spec = pl.BlockSpec((1, 2, GLA_HEADS, GLA_DK, GLA_DV), lambda b, i: (b, 0, 0, 0, 0))
    if has_s0:
        in_specs.append(state_spec)
        args.append(s0)
    return pl.pallas_call(
        functools.partial(_gla_kernel, has_s0=has_s0, n_chunk=tb // GLA_CHUNK),
        grid=(n_batch, nb),
        in_specs=in_specs,
        out_specs=[pl.BlockSpec((tb, BRANCH_W), fwd), pl.BlockSpec((tb, BRANCH_W), bwd), state_spec],
        out_shape=[
            jax.ShapeDtypeStruct((t_total, BRANCH_W), BF16),
            jax.ShapeDtypeStruct((t_total, BRANCH_W), BF16),
            jax.ShapeDtypeStruct((n_batch, 2, GLA_HEADS, GLA_DK, GLA_DV), F32),
        ],
        scratch_shapes=[pltpu.VMEM((2, GLA_HEADS, GLA_DV, LANES), F32)],
        compiler_params=pltpu.CompilerParams(
            dimension_semantics=("parallel", "arbitrary"), vmem_limit_bytes=VMEM_LIMIT),
        name="gla",
    )(*args)


def _merge_kernel(x_ref, mod_ref, og_ref, of_ref, ob_ref, lgo_ref, od_ref, gate_ref,
                  gout_ref, wb_ref, wo_ref, y_ref):
    o_gla = of_ref[...].astype(F32) + ob_ref[...].astype(F32)
    gla_parts = []
    for hd in range(GLA_HEADS):
        blk = o_gla[:, hd * GLA_DV:(hd + 1) * GLA_DV]
        ms = jnp.mean(blk * blk, axis=-1, keepdims=True)
        gla_parts.append(blk * lax.rsqrt(ms + EPS) * gout_ref[...])
    gla = jnp.concatenate(gla_parts, axis=1) * _silu(lgo_ref[...].astype(F32))
    branches = (og_ref[...], gla.astype(BF16), od_ref[...])
    mixed = None
    for b, ob in enumerate(branches):
        y = jnp.dot(ob, wb_ref[b], preferred_element_type=F32)
        y = y * gate_ref[:, b * D_MODEL:(b + 1) * D_MODEL].astype(F32)
        mixed = y if mixed is None else mixed + y
    out = jnp.dot(mixed.astype(BF16), wo_ref[...], preferred_element_type=F32)
    y_ref[...] = x_ref[...] + mod_ref[0, 2:3, :] * out


def _merge(x2d, mod, og, o_f, o_b, lgo, od, gates, lw, sw, l, tiles_per_mod, tm):
    t_total = x2d.shape[0]
    tile = lambda c: pl.BlockSpec((tm, c), lambda i: (i, 0))
    return pl.pallas_call(
        _merge_kernel,
        grid=(t_total // tm,),
        in_specs=[
            tile(D_MODEL),
            pl.BlockSpec((1, 6, D_MODEL), lambda i: (i // tiles_per_mod, 0, 0)),
            tile(BRANCH_W), tile(BRANCH_W), tile(BRANCH_W), tile(BRANCH_W), tile(BRANCH_W),
            tile(N_BRANCH * D_MODEL),
            _full_spec((1, GLA_DV)),
            _layer_spec((N_BRANCH, BRANCH_W, D_MODEL), l),
            _layer_spec((D_MODEL, D_MODEL), l),
        ],
        out_specs=tile(D_MODEL),
        out_shape=jax.ShapeDtypeStruct((t_total, D_MODEL), F32),
        compiler_params=pltpu.CompilerParams(
            dimension_semantics=("parallel",), vmem_limit_bytes=VMEM_LIMIT),
        name="merge",
    )(x2d, mod, og, o_f, o_b, lgo, od, gates, lw["gout"], sw["w_branch"], sw["w_out"])


FFN_CHUNK = 256


def _ffn_kernel(x_ref, mod_ref, g2_ref, wi_ref, wo_ref, y_ref):
    x = x_ref[...]
    ms = jnp.mean(x * x, axis=-1, keepdims=True)
    h = x * lax.rsqrt(ms + EPS) * g2_ref[...]
    h = h * (1.0 + mod_ref[0, 4:5, :]) + mod_ref[0, 3:4, :]
    hb = h.astype(BF16)
    def up_proj(c0):
        a = jnp.dot(hb, wi_ref[:, c0:c0 + FFN_CHUNK], preferred_element_type=F32)
        u = jnp.dot(hb, wi_ref[:, FFN_HIDDEN + c0:FFN_HIDDEN + c0 + FFN_CHUNK], preferred_element_type=F32)
        return a, u

    acc = None
    chunks = list(range(0, FFN_HIDDEN, FFN_CHUNK))
    nxt = up_proj(chunks[0])
    for i, c0 in enumerate(chunks):
        a, u = nxt
        if i + 1 < len(chunks):
            nxt = up_proj(chunks[i + 1])
        act = (_silu(a) * u).astype(BF16)
        part = jnp.dot(act, wo_ref[c0:c0 + FFN_CHUNK, :], preferred_element_type=F32)
        acc = part if acc is None else acc + part
    y_ref[...] = x + mod_ref[0, 5:6, :] * acc


def _ffn(x2d, mod, lw, sw, l, tiles_per_mod, tm):
    t_total = x2d.shape[0]
    tile = lambda c: pl.BlockSpec((tm, c), lambda i: (i, 0))
    return pl.pallas_call(
        _ffn_kernel,
        grid=(t_total // tm,),
        in_specs=[
            tile(D_MODEL),
            pl.BlockSpec((1, 6, D_MODEL), lambda i: (i // tiles_per_mod, 0, 0)),
            _full_spec((1, D_MODEL)),
            _layer_spec((D_MODEL, 2 * FFN_HIDDEN), l),
            _layer_spec((FFN_HIDDEN, D_MODEL), l),
        ],
        out_specs=tile(D_MODEL),
        out_shape=jax.ShapeDtypeStruct((t_total, D_MODEL), F32),
        compiler_params=pltpu.CompilerParams(
            dimension_semantics=("parallel",), vmem_limit_bytes=VMEM_LIMIT),
        name="ffn",
    )(x2d, mod, lw["norm2"], sw["w_ffn_in"], sw["w_ffn_out"])


def _shared_weights(p):
    w = p["w_in"]
    pad = jnp.zeros(w.shape[:2] + (N_A - SPLIT_AB,), BF16)
    return {
        "w_a": jnp.concatenate([w[:, :, :SPLIT_AB].astype(BF16), pad], axis=2),
        "w_b": w[:, :, SPLIT_AB:].astype(BF16),
        "w_branch": p["w_branch"].astype(BF16), "w_out": p["w_out"].astype(BF16),
        "w_ffn_in": p["w_ffn_in"].astype(BF16), "w_ffn_out": p["w_ffn_out"].astype(BF16),
    }


def _layer_weights(l, p):
    scale = HEAD_DIM ** -0.5 * math.log2(math.e)
    head_bound = lambda g, s: HEAD_DIM * (NORM_MARGIN * s) ** 2 * jnp.max(g * g)
    norm_bounds = lambda gq_, gk_: jnp.stack([head_bound(gq_, scale), head_bound(gk_, 1.0)])
    gq_row = jnp.concatenate([jnp.tile(p["gqa_q_norm"][l], GQA_HEADS),
                              jnp.tile(p["diff_q_norm"][l], 2 * DIFF_HEADS)]) * scale
    gk_row = jnp.concatenate([jnp.tile(p["gqa_k_norm"][l], GQA_KV_HEADS),
                              jnp.tile(p["diff_k_norm"][l], 2 * DIFF_HEADS)])
    idx = jnp.arange(2 * LANES) // HEAD_DIM
    gmat = jnp.where(idx[:, None] == idx[None, :], 1.0 / HEAD_DIM, 0.0).astype(BF16)
    up = jnp.zeros((LANES, 2 * GLA_W), F32)
    up = up.at[0:GLA_RANK, 0:GLA_W].set(p["gla_alpha_up"][l, 0])
    up = up.at[GLA_RANK:2 * GLA_RANK, GLA_W:].set(p["gla_alpha_up"][l, 1])
    ub = p["gla_alpha_bias"][l].reshape(1, 2 * GLA_W)
    up_hi = up.astype(BF16)
    up = jnp.stack([up_hi, (up - up_hi.astype(F32)).astype(BF16)])
    return {
        "gq": gq_row.reshape(1, 2 * N_QG), "gk": gk_row.reshape(1, LANES + BRANCH_W), "gmat": gmat,
        "up": up, "ub": ub,
        "kb_g": norm_bounds(p["gqa_q_norm"][l], p["gqa_k_norm"][l]),
        "kb_d": norm_bounds(p["diff_q_norm"][l], p["diff_k_norm"][l]),
        "norm1": p["norm1"][l].reshape(1, D_MODEL), "norm2": p["norm2"][l].reshape(1, D_MODEL),
        "gout": p["gla_out_norm"][l].reshape(1, GLA_DV),
        "gsub": p["diff_sub_norm"][l].reshape(1, DIFF_DV),
        "lam": p["diff_lambda"][l],
    }


def _rope_tables(n_tokens):
    n_rows = n_tokens // GRID_W
    row = jnp.repeat(jnp.arange(n_rows, dtype=F32), GRID_W)
    col = jnp.tile(jnp.arange(GRID_W, dtype=F32), n_rows)
    n_freq = HEAD_DIM // 4
    freqs = ROPE_THETA ** (-jnp.arange(n_freq, dtype=F32) / n_freq)
    ar, ac = row[:, None] * freqs, col[:, None] * freqs
    cos = jnp.concatenate([jnp.cos(ar), jnp.cos(ar), jnp.cos(ac), jnp.cos(ac)], axis=1)
    sin = jnp.concatenate([-jnp.sin(ar), jnp.sin(ar), -jnp.sin(ac), jnp.sin(ac)], axis=1)
    return jnp.tile(cos, (1, LANES // HEAD_DIM)), jnp.tile(sin, (1, LANES // HEAD_DIM))


def _with_ones_rows(v_t):
    lead = v_t.shape[:-2]
    s = v_t.shape[-1]
    ones = jnp.ones(lead + (1, s), v_t.dtype)
    zeros = jnp.zeros(lead + (BF16_ROWS - 1, s), v_t.dtype)
    return jnp.concatenate([v_t, ones, zeros], axis=-2).astype(BF16)


def _cache_layouts(l, cache_gqa_k, cache_gqa_v, cache_diff_k, cache_diff_v):
    b = cache_gqa_k.shape[0]
    gk = jnp.transpose(cache_gqa_k[:, l], (0, 2, 1, 3))
    kd_g = jnp.concatenate([gk, gk], axis=-1).astype(BF16)
    vt_g = _with_ones_rows(jnp.transpose(cache_gqa_v[:, l], (0, 2, 3, 1)))
    dk = jnp.transpose(cache_diff_k[:, l], (0, 2, 3, 1, 4)).reshape(b, 2 * DIFF_HEADS, PAST_LEN, HEAD_DIM)
    kd_d = jnp.concatenate([dk, dk], axis=-1).astype(BF16)
    vt_d = _with_ones_rows(jnp.transpose(cache_diff_v[:, l], (0, 2, 3, 1)))
    return kd_g, vt_g, kd_d, vt_d


def _run_layer(x2d, mod, lw, sw, l, n_batch, n_tok, rope_tabs, ctx, emit_cache, cfg):
    tiles_per_mod = lambda tm: (n_tok // tm) if mod.shape[0] > 1 else (n_batch * n_tok // tm)
    if ctx is None:
        cache_kv = kd_gc = kd_dc = s0 = None
    else:
        kd_gc, vt_gc, kd_dc, vt_dc, s0 = ctx
        cache_kv = (kd_gc, kd_dc, vt_gc, vt_dc)
    outs = _in_projection(x2d, mod, lw, sw, l, n_batch, n_tok, cache_kv, rope_tabs, emit_cache, cfg["tm_in"])
    qg, qd, kdg, kdd, vtg, vtd, lq, lk, lv, lgo, la, gates = outs[:12]
    og = _gqa_attention(qg, lw["kb_g"], kdg, vtg, kd_gc, n_batch, n_tok, cfg["tq_gqa"], cfg["tk"])
    lam_init = 0.8 - 0.6 * math.exp(-0.3 * l)
    od = _diff_attention(qd, lw["kb_d"], lw["lam"], lw["gsub"], kdd, vtd, kd_dc, n_batch, n_tok,
                         cfg["tq_diff"], cfg["tk"], lam_init)
    o_f, o_b, s_fin = _gla(lq, lk, lv, la, s0, n_batch, n_tok, cfg["tb_gla"])
    x2d = _merge(x2d, mod, og, o_f, o_b, lgo, od, gates, lw, sw, l, tiles_per_mod(cfg["tm"]), cfg["tm"])
    x2d = _ffn(x2d, mod, lw, sw, l, tiles_per_mod(cfg["tm"]), cfg["tm"])
    cache = tuple(outs[12:]) + (s_fin,) if emit_cache else None
    return x2d, cache


PROMPT_CFG = dict(tm_in=256, tq_gqa=256, tq_diff=256, tk=256, tb_gla=256, tm=512)
SAMPLE_CFG = dict(tm_in=512, tq_gqa=1024, tq_diff=2048, tk=1536, tb_gla=256, tm=512)


def kernel(x_prompt, x_sample, c, cache_gqa_k, cache_gqa_v, state_gla, cache_diff_k, cache_diff_v, c_ctx, w_mod, b_mod, norm1, norm2, w_in, gqa_q_norm, gqa_k_norm, gla_alpha_up, gla_alpha_bias, gla_out_norm, diff_q_norm, diff_k_norm, diff_lambda, diff_sub_norm, w_branch, w_out, w_ffn_in, w_ffn_out):
    p = {
        "norm1": norm1, "norm2": norm2, "w_in": w_in, "gqa_q_norm": gqa_q_norm, "gqa_k_norm": gqa_k_norm,
        "gla_alpha_up": gla_alpha_up, "gla_alpha_bias": gla_alpha_bias, "gla_out_norm": gla_out_norm,
        "diff_q_norm": diff_q_norm, "diff_k_norm": diff_k_norm, "diff_lambda": diff_lambda,
        "diff_sub_norm": diff_sub_norm, "w_branch": w_branch, "w_out": w_out,
        "w_ffn_in": w_ffn_in, "w_ffn_out": w_ffn_out,
    }
    n_ctx_b, n_ctx = x_prompt.shape[:2]
    n_lat_b, n_lat = x_sample.shape[:2]
    cond_rows = jnp.concatenate(
        [c_ctx[None, :], c, jnp.zeros((8 - 1 - n_lat_b, D_MODEL), F32)], axis=0)
    mod_all = _modulation(cond_rows, w_mod, b_mod)
    weights = [_layer_weights(l, p) for l in range(DEPTH)]
    shared = _shared_weights(p)

    y = x_prompt.reshape(n_ctx_b * n_ctx, D_MODEL)
    caches = []
    for l in range(DEPTH):
        mod = mod_all[l, 0:1].reshape(1, 6, D_MODEL)
        y, cache = _run_layer(y, mod, weights[l], shared, l, n_ctx_b, n_ctx, None, None, True, PROMPT_CFG)
        caches.append(cache)
    y_prompt = y.reshape(n_ctx_b, n_ctx, D_MODEL)
    stack = lambda j, shape: jnp.stack([cc[j].reshape(shape) for cc in caches], axis=1)
    new_gqa_k = stack(0, (n_ctx_b, n_ctx, GQA_KV_HEADS, HEAD_DIM))
    new_diff_k = stack(1, (n_ctx_b, n_ctx, DIFF_HEADS, 2, HEAD_DIM))
    new_gqa_v = stack(2, (n_ctx_b, n_ctx, GQA_KV_HEADS, HEAD_DIM))
    new_diff_v = stack(3, (n_ctx_b, n_ctx, DIFF_HEADS, DIFF_DV))
    new_state_gla = jnp.stack([cc[4] for cc in caches], axis=1)

    rope_tabs = _rope_tables(n_lat)
    y = x_sample.reshape(n_lat_b * n_lat, D_MODEL)
    for l in range(DEPTH):
        mod = mod_all[l, 1:1 + n_lat_b].reshape(n_lat_b, 6, D_MODEL)
        ctx = _cache_layouts(l, cache_gqa_k, cache_gqa_v, cache_diff_k, cache_diff_v) + (state_gla[:, l],)
        y, _ = _run_layer(y, mod, weights[l], shared, l, n_lat_b, n_lat, rope_tabs, ctx, False, SAMPLE_CFG)
    y_sample = y.reshape(n_lat_b, n_lat, D_MODEL)
    return (y_prompt, y_sample, new_gqa_k, new_gqa_v, new_state_gla, new_diff_k, new_diff_v)
```

```python
import functools
import math

import jax
import jax.numpy as jnp
from jax import lax
from jax.experimental import pallas as pl
from jax.experimental.pallas import tpu as pltpu

D_MODEL = 1024
DEPTH = 2
PAST_LEN = 512
GRID_W = 64
HEAD_DIM = 64
GQA_HEADS = 8
GQA_KV_HEADS = 2
GQA_GROUP = GQA_HEADS // GQA_KV_HEADS
GLA_HEADS = 4
GLA_DK = 64
GLA_DV = 128
GLA_RANK = 16
GLA_TAU = 16.0
GLA_CHUNK = 64
DIFF_HEADS = 4
DIFF_DV = 2 * HEAD_DIM
N_BRANCH = 3
BRANCH_W = 512
FFN_HIDDEN = ((8 * D_MODEL + 3 * 256 - 1) // (3 * 256)) * 256
ROPE_THETA = 10000.0
EPS = 1e-6

LANES = 128
BF16_ROWS = 16
VMEM_LIMIT = 56 * 1024 * 1024

F32 = jnp.float32
BF16 = jnp.bfloat16
HIGHEST = lax.Precision.HIGHEST

A_GQ = 0
A_KV = A_GQ + GQA_HEADS * HEAD_DIM
A_GLA = A_KV + 2 * GQA_KV_HEADS * HEAD_DIM
SPLIT_AB = A_GLA + 2 * GLA_HEADS * GLA_DK + GLA_HEADS * GLA_DV + 2 * GLA_RANK
N_A = -(-SPLIT_AB // LANES) * LANES
B_LGO = 0
B_DQ = B_LGO + GLA_HEADS * GLA_DV
B_DK = B_DQ + DIFF_HEADS * 2 * HEAD_DIM
B_DV = B_DK + DIFF_HEADS * 2 * HEAD_DIM
B_MG = B_DV + DIFF_HEADS * DIFF_DV
N_B = B_MG + N_BRANCH * D_MODEL
N_QG = GQA_HEADS * HEAD_DIM
GLA_W = GLA_HEADS * GLA_DK
GQA_VT_ROWS = HEAD_DIM + BF16_ROWS
DIFF_VT_ROWS = DIFF_DV + BF16_ROWS


def _sigmoid(x):
    return 1.0 / (1.0 + jnp.exp(-x))


def _silu(x):
    return x * _sigmoid(x)


def _bf16_terms(x):
    hi = x.astype(BF16)
    r1 = x - hi.astype(F32)
    mid = r1.astype(BF16)
    lo = (r1 - mid.astype(F32)).astype(BF16)
    return hi, mid, lo


def _full_spec(shape):
    n = len(shape)
    return pl.BlockSpec(shape, lambda *_: (0,) * n, pipeline_mode=pl.Buffered(1))


def _layer_spec(shape, l):
    n = len(shape)
    return pl.BlockSpec((None,) + tuple(shape), lambda *_: (l,) + (0,) * n, pipeline_mode=pl.Buffered(1))


def _mod_kernel(cond_ref, w_ref, b_ref, o_ref):
    s = _silu(cond_ref[...])
    o_ref[0] = jnp.dot(s, w_ref[0], precision=HIGHEST, preferred_element_type=F32) + b_ref[0]


def _modulation(cond_rows, w_mod, b_mod):
    tn = 1536
    n_out = 6 * D_MODEL
    return pl.pallas_call(
        _mod_kernel,
        grid=(DEPTH, n_out // tn),
        in_specs=[
            pl.BlockSpec((8, D_MODEL), lambda l, j: (0, 0)),
            pl.BlockSpec((1, D_MODEL, tn), lambda l, j: (l, 0, j)),
            pl.BlockSpec((1, 1, tn), lambda l, j: (l, 0, j)),
        ],
        out_specs=pl.BlockSpec((1, 8, tn), lambda l, j: (l, 0, j)),
        out_shape=jax.ShapeDtypeStruct((DEPTH, 8, n_out), F32),
        compiler_params=pltpu.CompilerParams(
            dimension_semantics=("parallel", "parallel"), vmem_limit_bytes=VMEM_LIMIT),
        name="modulation",
    )(cond_rows, w_mod, b_mod.reshape(DEPTH, 1, n_out))


def _head_norm(z, gain, gmat):
    outs = []
    n = z.shape[1]
    for c0 in range(0, n, 2 * LANES):
        w = min(2 * LANES, n - c0)
        zz = z[:, c0:c0 + w]
        ms = jnp.dot((zz * zz).astype(BF16), gmat[0:w, 0:w], preferred_element_type=F32)
        outs.append(zz * lax.rsqrt(ms + EPS) * gain[:, c0:c0 + w])
    return outs


def _rope_block(zb, cos, sin, first):
    partner = jnp.where(first, pltpu.roll(zb, LANES - 16, 1), pltpu.roll(zb, 16, 1))
    return zb * cos + partner * sin


N_IN_BASE = 10


def _in_kernel(*refs, rope, emit_cache, tiles_per_batch, cache_tiles):
    if not cache_tiles:
        _in_tile(refs, rope, emit_cache)
        return
    kcg_ref, kcd_ref, vcg_ref, vcd_ref = refs[:4]
    j = pl.program_id(1)

    @pl.when(j < tiles_per_batch)
    def _():
        _in_tile(refs[4:], rope, emit_cache)

    @pl.when(j >= tiles_per_batch)
    def _():
        n_in = 4 + N_IN_BASE + (2 if rope else 0)
        kdg_ref, kdd_ref, vtg_ref, vtd_ref = refs[n_in + 2:n_in + 6]
        kdg_ref[...] = kcg_ref[...]
        kdd_ref[...] = kcd_ref[...]
        vtg_ref[...] = vcg_ref[...]
        vtd_ref[...] = vcd_ref[...]


def _in_tile(refs, rope, emit_cache):
    it = iter(refs)
    (x_ref, mod_ref, g1_ref, wa_ref, wb_ref, gq_ref, gk_ref, gmat_ref, up_ref, ub_ref) = (
        next(it) for _ in range(N_IN_BASE))
    if rope:
        cos_ref, sin_ref = next(it), next(it)
    (qg_ref, qd_ref, kdg_ref, kdd_ref, vtg_ref, vtd_ref,
     lq_ref, lk_ref, lv_ref, lgo_ref, la_ref, gate_ref) = (next(it) for _ in range(12))
    if emit_cache:
        ck_g_ref, ck_d_ref, cv_g_ref, cv_d_ref = (next(it) for _ in range(4))

    tm = x_ref.shape[0]
    x = x_ref[...]
    ms = jnp.mean(x * x, axis=-1, keepdims=True)
    h = x * lax.rsqrt(ms + EPS) * g1_ref[...]
    h = h * (1.0 + mod_ref[0, 1:2, :]) + mod_ref[0, 0:1, :]
    hb = h.astype(BF16)

    def proj(w_ref, c0, n):
        return jnp.dot(hb, w_ref[:, c0:c0 + n], preferred_element_type=F32)

    lane = lax.broadcasted_iota(jnp.int32, (tm, LANES), 1)
    first16 = (lane % 32) < 16
    lo_half = lane < HEAD_DIM
    gmat = gmat_ref[...]
    if rope:
        cos, sin = cos_ref[...], sin_ref[...]

    def blocks128(parts):
        out = []
        for p in parts:
            for c in range(0, p.shape[1], LANES):
                out.append(p[:, c:c + LANES])
        return out

    def store_queries(z, gain, dst):
        for j, zb in enumerate(blocks128(_head_norm(z, gain, gmat))):
            if rope:
                zb = _rope_block(zb, cos, sin, first16)
            dst[:, j * LANES:(j + 1) * LANES] = zb.astype(BF16)

    def store_keys(z, gain, dst, cache_dst):
        for j, zb in enumerate(blocks128(_head_norm(z, gain, gmat))):
            if rope:
                zb = _rope_block(zb, cos, sin, first16)
            if emit_cache:
                cache_dst[:, j * LANES:(j + 1) * LANES] = zb
            sw = pltpu.roll(zb, HEAD_DIM, 1)
            dst[0, 2 * j] = jnp.where(lo_half, zb, sw).astype(BF16)
            dst[0, 2 * j + 1] = jnp.where(lo_half, sw, zb).astype(BF16)

    z_gq = proj(wa_ref, A_GQ, N_QG)
    z_dq = proj(wb_ref, B_DQ, N_QG)
    store_queries(z_gq, gq_ref[:, 0:N_QG], qg_ref)
    z_kv = proj(wa_ref, A_KV, 2 * LANES)
    store_queries(z_dq, gq_ref[:, N_QG:2 * N_QG], qd_ref)
    z_dk = proj(wb_ref, B_DK, BRANCH_W)
    store_keys(z_kv[:, 0:LANES], gk_ref[:, 0:LANES], kdg_ref, ck_g_ref if emit_cache else None)
    z_dv = proj(wb_ref, B_DV, BRANCH_W)
    store_keys(z_dk, gk_ref[:, LANES:LANES + BRANCH_W], kdd_ref, ck_d_ref if emit_cache else None)
    zl = proj(wa_ref, A_GLA, N_A - A_GLA)

    ones_rows = jnp.where(lax.broadcasted_iota(jnp.int32, (BF16_ROWS, tm), 0) == 0, 1.0, 0.0).astype(BF16)
    z_gv = z_kv[:, LANES:2 * LANES]
    if emit_cache:
        cv_g_ref[...] = z_gv
        cv_d_ref[...] = z_dv
    vt = z_gv.T
    for hh in range(GQA_KV_HEADS):
        vtg_ref[0, hh, 0:HEAD_DIM, :] = vt[hh * HEAD_DIM:(hh + 1) * HEAD_DIM, :].astype(BF16)
        vtg_ref[0, hh, HEAD_DIM:GQA_VT_ROWS, :] = ones_rows
    for hh in range(DIFF_HEADS):
        vt = z_dv[:, hh * LANES:(hh + 1) * LANES].T
        vtd_ref[0, hh, 0:DIFF_DV, :] = vt.astype(BF16)
        vtd_ref[0, hh, DIFF_DV:DIFF_VT_ROWS, :] = ones_rows

    z_lgo = proj(wb_ref, B_LGO, BRANCH_W)
    o = 0
    lq_ref[...] = (zl[:, o:o + GLA_W] * (GLA_DK ** -0.5)).astype(BF16)
    o += GLA_W
    lk_ref[...] = zl[:, o:o + GLA_W].astype(BF16)
    o += GLA_W
    lv_ref[...] = zl[:, o:o + BRANCH_W].astype(BF16)
    o += BRANCH_W
    lgo_ref[...] = z_lgo.astype(BF16)
    lr = zl[:, o:o + LANES]
    lr_hi, lr_mid, _ = _bf16_terms(lr)
    z = (jnp.dot(lr_hi, up_ref[0], preferred_element_type=F32)
         + jnp.dot(lr_mid, up_ref[0], preferred_element_type=F32)
         + jnp.dot(lr_hi, up_ref[1], preferred_element_type=F32)) + ub_ref[...]
    log_sig = jnp.minimum(z, 0.0) - jnp.log(1.0 + jnp.exp(-jnp.abs(z)))
    la_ref[...] = log_sig * (1.0 / GLA_TAU)

    for b in range(N_BRANCH):
        gate_ref[:, b * D_MODEL:(b + 1) * D_MODEL] = _sigmoid(
            proj(wb_ref, B_MG + b * D_MODEL, D_MODEL)).astype(BF16)


def _in_projection(x2d, mod, lw, sw, l, n_batch, n_tok, cache_kv, rope_tabs, emit_cache, tm):
    n_cache = 0 if cache_kv is None else cache_kv[0].shape[2]
    assert n_cache % tm == 0
    n_keys = n_tok + n_cache
    t_total = n_batch * n_tok
    tpb = n_tok // tm
    cache_tiles = n_cache // tm
    rope = rope_tabs is not None
    own = lambda j: jnp.minimum(j, tpb - 1)
    tile = lambda c: pl.BlockSpec((tm, c), lambda b, j: (b * tpb + own(j), 0))
    in_specs, args = [], []
    if cache_tiles:
        past = lambda j: jnp.maximum(j - tpb, 0)
        in_specs += [
            pl.BlockSpec((1, GQA_KV_HEADS, tm, LANES), lambda b, j: (b, 0, past(j), 0)),
            pl.BlockSpec((1, 2 * DIFF_HEADS, tm, LANES), lambda b, j: (b, 0, past(j), 0)),
            pl.BlockSpec((1, GQA_KV_HEADS, GQA_VT_ROWS, tm), lambda b, j: (b, 0, 0, past(j))),
            pl.BlockSpec((1, DIFF_HEADS, DIFF_VT_ROWS, tm), lambda b, j: (b, 0, 0, past(j))),
        ]
        args += list(cache_kv)
    in_specs += [
        tile(D_MODEL),
        pl.BlockSpec((1, 6, D_MODEL), lambda b, j: (b if mod.shape[0] > 1 else 0, 0, 0)),
        _full_spec((1, D_MODEL)),
        _layer_spec((D_MODEL, N_A), l),
        _layer_spec((D_MODEL, N_B), l),
        _full_spec((1, 2 * N_QG)),
        _full_spec((1, LANES + BRANCH_W)),
        _full_spec((2 * LANES, 2 * LANES)),
        _full_spec((2, LANES, 2 * GLA_W)),
        _full_spec((1, 2 * GLA_W)),
    ]
    args += [x2d, mod, lw["norm1"], sw["w_a"], sw["w_b"], lw["gq"], lw["gk"], lw["gmat"], lw["up"], lw["ub"]]
    if rope:
        in_specs += [pl.BlockSpec((tm, LANES), lambda b, j: (own(j), 0))] * 2
        args += list(rope_tabs)
    sd = jax.ShapeDtypeStruct
    out_shape = [
        sd((t_total, BRANCH_W), BF16), sd((t_total, BRANCH_W), BF16),
        sd((n_batch, GQA_KV_HEADS, n_keys, LANES), BF16), sd((n_batch, 2 * DIFF_HEADS, n_keys, LANES), BF16),
        sd((n_batch, GQA_KV_HEADS, GQA_VT_ROWS, n_keys), BF16), sd((n_batch, DIFF_HEADS, DIFF_VT_ROWS, n_keys), BF16),
        sd((t_total, GLA_W), BF16), sd((t_total, GLA_W), BF16),
        sd((t_total, BRANCH_W), BF16), sd((t_total, BRANCH_W), BF16),
        sd((t_total, 2 * GLA_W), F32), sd((t_total, N_BRANCH * D_MODEL), BF16),
    ]
    out_specs = [
        tile(BRANCH_W), tile(BRANCH_W),
        pl.BlockSpec((1, GQA_KV_HEADS, tm, LANES), lambda b, j: (b, 0, j, 0)),
        pl.BlockSpec((1, 2 * DIFF_HEADS, tm, LANES), lambda b, j: (b, 0, j, 0)),
        pl.BlockSpec((1, GQA_KV_HEADS, GQA_VT_ROWS, tm), lambda b, j: (b, 0, 0, j)),
        pl.BlockSpec((1, DIFF_HEADS, DIFF_VT_ROWS, tm), lambda b, j: (b, 0, 0, j)),
        tile(GLA_W), tile(GLA_W), tile(BRANCH_W), tile(BRANCH_W), tile(2 * GLA_W), tile(N_BRANCH * D_MODEL),
    ]
    if emit_cache:
        out_shape += [sd((t_total, LANES), F32), sd((t_total, BRANCH_W), F32),
                      sd((t_total, LANES), F32), sd((t_total, BRANCH_W), F32)]
        out_specs += [tile(LANES), tile(BRANCH_W), tile(LANES), tile(BRANCH_W)]
    return pl.pallas_call(
        functools.partial(_in_kernel, rope=rope, emit_cache=emit_cache, tiles_per_batch=tpb,
                          cache_tiles=cache_tiles),
        grid=(n_batch, tpb + cache_tiles),
        in_specs=in_specs,
        out_specs=out_specs,
        out_shape=out_shape,
        compiler_params=pltpu.CompilerParams(
            dimension_semantics=("parallel", "arbitrary"), vmem_limit_bytes=VMEM_LIMIT),
        name="in_projection",
    )(*args)


MAX_COL = 4 * LANES
NT_DIMS = (((1,), (1,)), ((), ()))
SCORE_BOUND = 45.0
NORM_MARGIN = 1.05


def _masked_queries(q_blk, keep_low):
    lane = lax.broadcasted_iota(jnp.int32, q_blk.shape, 1)
    keep = (lane < HEAD_DIM) if keep_low else (lane >= HEAD_DIM)
    return jnp.where(keep, q_blk.astype(F32), 0.0).astype(BF16)


def _dup_key_norm2(kd):
    kf = kd.astype(F32)
    return 0.5 * jnp.max(jnp.sum(kf * kf, axis=1, keepdims=True))


def _needs_no_stabiliser(q_norm2, k_norm2):
    return (q_norm2 * k_norm2 <= SCORE_BOUND * SCORE_BOUND).astype(jnp.int32)


KEY_SUB = 512


def _score_units(k_ref, heads_cols):
    tk = k_ref.shape[2]
    sub = min(KEY_SUB, tk)
    return sub, [(kh, vh, s0, c) for s0 in range(0, tk, sub) for kh, vh, cols in heads_cols for c in cols]


def _tile_scores(k_ref, qm_ref, unit, sub, col):
    kh, _, s0, c = unit
    return lax.dot_general(k_ref[0, kh, s0:s0 + sub, :], qm_ref[c * col:(c + 1) * col, :], NT_DIMS,
                           preferred_element_type=F32)


def _plain_tiles(k_ref, vt_ref, qm_ref, acc_ref, heads_cols):
    col = acc_ref.shape[-1]
    sub, units = _score_units(k_ref, heads_cols)
    s_next = _tile_scores(k_ref, qm_ref, units[0], sub, col)
    for i, (_, vh, s0, c) in enumerate(units):
        s = s_next
        if i + 1 < len(units):
            s_next = _tile_scores(k_ref, qm_ref, units[i + 1], sub, col)
        p = jnp.exp2(s).astype(BF16)
        acc_ref[c] += jnp.dot(vt_ref[0, vh, :, s0:s0 + sub], p, preferred_element_type=F32)


def _online_tiles(k_ref, vt_ref, qm_ref, m_ref, acc_ref, heads_cols):
    col = acc_ref.shape[-1]
    sub, units = _score_units(k_ref, heads_cols)
    for unit in units:
        _, vh, s0, c = unit
        s = _tile_scores(k_ref, qm_ref, unit, sub, col)
        m_prev = m_ref[c]
        m_new = jnp.maximum(m_prev, jnp.max(s, axis=0, keepdims=True))
        alpha = jnp.exp2(m_prev - m_new)
        p = jnp.exp2(s - m_new).astype(BF16)
        acc_ref[c] = acc_ref[c] * alpha + jnp.dot(vt_ref[0, vh, :, s0:s0 + sub], p, preferred_element_type=F32)
        m_ref[c] = m_new


def _flash_init(kb_ref, kc_ref, n_key_heads, m_ref, acc_ref, plain_ref):
    k_norm2 = kb_ref[1]
    if kc_ref is not None:
        for kh in range(n_key_heads):
            k_norm2 = jnp.maximum(k_norm2, _dup_key_norm2(kc_ref[0, kh]))
    plain_ref[0] = _needs_no_stabiliser(kb_ref[0], k_norm2)
    m_ref[...] = jnp.full(m_ref.shape, -jnp.inf, F32)
    acc_ref[...] = jnp.zeros(acc_ref.shape, F32)


def _flash_step(k_ref, vt_ref, qm_ref, m_ref, acc_ref, plain_ref, heads_cols):
    @pl.when(plain_ref[0] == 1)
    def _():
        _plain_tiles(k_ref, vt_ref, qm_ref, acc_ref, heads_cols)

    @pl.when(plain_ref[0] != 1)
    def _():
        _online_tiles(k_ref, vt_ref, qm_ref, m_ref, acc_ref, heads_cols)


def _gqa_kernel(*refs, has_cache, tq, hp):
    if has_cache:
        kb_ref, q_ref, k_ref, vt_ref, kc_ref, o_ref, qm_ref, m_ref, acc_ref, plain_ref = refs
    else:
        kb_ref, q_ref, k_ref, vt_ref, o_ref, qm_ref, m_ref, acc_ref, plain_ref = refs
        kc_ref = None
    kt = pl.program_id(3)
    col = acc_ref.shape[-1]
    per_head = tq // col
    per_kv = GQA_GROUP * per_head
    heads_cols = [(hh, hh, range(hh * per_kv, (hh + 1) * per_kv)) for hh in range(hp)]

    @pl.when(kt == 0)
    def _():
        for hh in range(hp):
            for g in range(GQA_GROUP):
                c0 = hh * 2 * LANES + (g // 2) * LANES
                row0 = (hh * GQA_GROUP + g) * tq
                qm_ref[row0:row0 + tq, :] = _masked_queries(q_ref[0, :, c0:c0 + LANES], g % 2 == 0)
        _flash_init(kb_ref, kc_ref, hp, m_ref, acc_ref, plain_ref)

    _flash_step(k_ref, vt_ref, qm_ref, m_ref, acc_ref, plain_ref, heads_cols)

    @pl.when(kt == pl.num_programs(3) - 1)
    def _():
        for hh in range(hp):
            for part in range(per_head):
                heads = []
                for g in range(GQA_GROUP):
                    acc = acc_ref[hh * per_kv + g * per_head + part]
                    heads.append(acc[0:HEAD_DIM, :] / acc[HEAD_DIM:HEAD_DIM + 1, :])
                o_ref[0, part * col:(part + 1) * col, hh * 2 * LANES:(hh + 1) * 2 * LANES] = (
                    jnp.concatenate(heads, axis=0).T.astype(BF16))


def _kv_specs(key_heads_blk, val_heads_blk, vt_rows, tk, kd_c):
    specs = [pl.BlockSpec((1, key_heads_blk, tk, LANES), lambda b, h, qi, kt: (b, h, kt, 0)),
             pl.BlockSpec((1, val_heads_blk, vt_rows, tk), lambda b, h, qi, kt: (b, h, 0, kt))]
    if kd_c is not None:
        specs.append(pl.BlockSpec((1, key_heads_blk, kd_c.shape[2], LANES), lambda b, h, qi, kt: (b, h, 0, 0)))
    return specs


def _gqa_attention(q, k_bound, kd, vt, kd_c, n_batch, n_tok, tq, tk, hp):
    has_cache = kd_c is not None
    nk = kd.shape[2] // tk
    rows = hp * GQA_GROUP * tq
    col = min(MAX_COL, tq)
    q_spec = pl.BlockSpec((1, tq, hp * 2 * LANES), lambda b, h, qi, kt: (b, qi, h))
    in_specs = [pl.BlockSpec(memory_space=pltpu.SMEM), q_spec]
    in_specs += _kv_specs(hp, hp, GQA_VT_ROWS, tk, kd_c)
    args = [k_bound, q.reshape(n_batch, n_tok, BRANCH_W), kd, vt]
    if has_cache:
        args.append(kd_c)
    out = pl.pallas_call(
        functools.partial(_gqa_kernel, has_cache=has_cache, tq=tq, hp=hp),
        grid=(n_batch, GQA_KV_HEADS // hp, n_tok // tq, nk),
        in_specs=in_specs,
        out_specs=q_spec,
        out_shape=jax.ShapeDtypeStruct((n_batch, n_tok, BRANCH_W), BF16),
        scratch_shapes=[
            pltpu.VMEM((rows, LANES), BF16),
            pltpu.VMEM((rows // col, 1, col), F32),
            pltpu.VMEM((rows // col, GQA_VT_ROWS, col), F32),
            pltpu.SMEM((1,), jnp.int32),
        ],
        compiler_params=pltpu.CompilerParams(
            dimension_semantics=("parallel", "parallel", "parallel", "arbitrary"),
            vmem_limit_bytes=VMEM_LIMIT),
        name="gqa_attention",
    )(*args)
    return out.reshape(n_batch * n_tok, BRANCH_W)


def _diff_kernel(*refs, has_cache, lam_init, hp):
    if has_cache:
        (kb_ref, q_ref, lam_ref, gsub_ref, k_ref, vt_ref, kc_ref, o_ref,
         qm_ref, m_ref, acc_ref, plain_ref) = refs
    else:
        kb_ref, q_ref, lam_ref, gsub_ref, k_ref, vt_ref, o_ref, qm_ref, m_ref, acc_ref, plain_ref = refs
        kc_ref = None
    kt = pl.program_id(3)
    tq = q_ref.shape[1]
    col = acc_ref.shape[-1]
    per_map = tq // col
    heads_cols = [(2 * hh + mm, hh, range((2 * hh + mm) * per_map, (2 * hh + mm + 1) * per_map))
                  for hh in range(hp) for mm in range(2)]

    @pl.when(kt == 0)
    def _():
        for hh in range(hp):
            blk = q_ref[0, :, hh * LANES:(hh + 1) * LANES]
            for mm in range(2):
                row0 = (2 * hh + mm) * tq
                qm_ref[row0:row0 + tq, :] = _masked_queries(blk, mm == 0)
        _flash_init(kb_ref, kc_ref, 2 * hp, m_ref, acc_ref, plain_ref)

    _flash_step(k_ref, vt_ref, qm_ref, m_ref, acc_ref, plain_ref, heads_cols)

    @pl.when(kt == pl.num_programs(3) - 1)
    def _():
        lp = lam_ref[...]
        lam = (jnp.exp(jnp.sum(lp[0:1] * lp[1:2], axis=-1, keepdims=True))
               - jnp.exp(jnp.sum(lp[2:3] * lp[3:4], axis=-1, keepdims=True)) + lam_init)
        for hh in range(hp):
            for part in range(per_map):
                a0, a1 = acc_ref[2 * hh * per_map + part], acc_ref[(2 * hh + 1) * per_map + part]
                o0 = a0[0:DIFF_DV, :] / a0[DIFF_DV:DIFF_DV + 1, :]
                o1 = a1[0:DIFF_DV, :] / a1[DIFF_DV:DIFF_DV + 1, :]
                d = (o0 - lam * o1).T
                ms = jnp.mean(d * d, axis=-1, keepdims=True)
                o_ref[0, part * col:(part + 1) * col, hh * LANES:(hh + 1) * LANES] = (
                    d * lax.rsqrt(ms + EPS) * gsub_ref[...] * (1.0 - lam_init)).astype(BF16)


def _diff_attention(q, k_bound, lam_p, gsub, kd, vt, kd_c, n_batch, n_tok, tq, tk, hp, lam_init):
    has_cache = kd_c is not None
    nk = kd.shape[2] // tk
    col = min(MAX_COL, tq)
    rows = hp * 2 * tq
    q_spec = pl.BlockSpec((1, tq, hp * LANES), lambda b, h, qi, kt: (b, qi, h))
    in_specs = [
        pl.BlockSpec(memory_space=pltpu.SMEM),
        q_spec,
        pl.BlockSpec((4, HEAD_DIM), lambda b, h, qi, kt: (0, 0)),
        pl.BlockSpec((1, DIFF_DV), lambda b, h, qi, kt: (0, 0)),
    ]
    in_specs += _kv_specs(2 * hp, hp, DIFF_VT_ROWS, tk, kd_c)
    args = [k_bound, q.reshape(n_batch, n_tok, BRANCH_W), lam_p, gsub, kd, vt]
    if has_cache:
        args.append(kd_c)
    out = pl.pallas_call(
        functools.partial(_diff_kernel, has_cache=has_cache, lam_init=lam_init, hp=hp),
        grid=(n_batch, DIFF_HEADS // hp, n_tok // tq, nk),
        in_specs=in_specs,
        out_specs=q_spec,
        out_shape=jax.ShapeDtypeStruct((n_batch, n_tok, BRANCH_W), BF16),
        scratch_shapes=[
            pltpu.VMEM((rows, LANES), BF16),
            pltpu.VMEM((rows // col, 1, col), F32),
            pltpu.VMEM((rows // col, DIFF_VT_ROWS, col), F32),
            pltpu.SMEM((1,), jnp.int32),
        ],
        compiler_params=pltpu.CompilerParams(
            dimension_semantics=("parallel", "parallel", "parallel", "arbitrary"),
            vmem_limit_bytes=VMEM_LIMIT),
        name="diff_attention",
    )(*args)
    return out.reshape(n_batch * n_tok, BRANCH_W)


def _gla_kernel(*refs, has_s0, n_chunk):
    if has_s0:
        qf, kf, vf, laf, qb, kb, vb, lab, s0_ref, of_ref, ob_ref, sfin_ref, st_ref = refs
    else:
        qf, kf, vf, laf, qb, kb, vb, lab, of_ref, ob_ref, sfin_ref, st_ref = refs
    i = pl.program_id(1)
    ck = GLA_CHUNK

    @pl.when(i == 0)
    def _():
        for d in range(2):
            for hd in range(GLA_HEADS):
                if has_s0:
                    s = s0_ref[0, d, hd]
                    z = jnp.zeros_like(s)
                    padded = jnp.concatenate([s, z] if hd % 2 == 0 else [z, s], axis=0)
                    st_ref[d, hd] = padded.T
                else:
                    st_ref[d, hd] = jnp.zeros((GLA_DV, LANES), F32)

    tb = qf.shape[0]
    r = lax.broadcasted_iota(jnp.int32, (tb, tb), 0)
    c = lax.broadcasted_iota(jnp.int32, (tb, tb), 1)
    lane = lax.broadcasted_iota(jnp.int32, (tb, LANES), 1)
    chunk_id = lax.broadcasted_iota(jnp.int32, (tb, GLA_W), 0) // ck
    zero_row = jnp.zeros((1, GLA_W), F32)
    streams = ((qf, kf, vf, laf, of_ref, c <= r), (qb, kb, vb, lab, ob_ref, c >= r))
    for d, (q_r, k_r, v_r, la_r, o_r, tri) in enumerate(streams):
        tri_b = tri.astype(F32).astype(BF16)
        g = sum(jnp.dot(tri_b, part, preferred_element_type=F32) for part in _bf16_terms(la_r[...]))
        if d == 0:
            bounds = [zero_row] + [g[ck * j - 1:ck * j, :] for j in range(1, n_chunk)]
            g_end = g[tb - 1:tb, :]
        else:
            bounds = [g[ck * (j + 1):ck * (j + 1) + 1, :] for j in range(n_chunk - 1)] + [zero_row]
            g_end = g[0:1, :]
        b_rows = jnp.concatenate([jnp.broadcast_to(b, (ck, GLA_W)) for b in bounds], axis=0)
        q = q_r[...].astype(F32)
        k = k_r[...].astype(F32)
        v = v_r[...]
        q_dec = q * jnp.exp(g - b_rows)
        q_glob = q * jnp.exp(g)
        k_end = k * jnp.exp(g_end - g)
        decay = jnp.exp(g_end)
        k_rel = []
        for j in range(n_chunk):
            reach = (chunk_id <= j) if d == 0 else (chunk_id >= j)
            k_rel.append(k * jnp.exp(jnp.where(reach, bounds[j] - g, 0.0)))
        for hd in range(GLA_HEADS):
            pair = slice((hd // 2) * LANES, (hd // 2 + 1) * LANES)
            keep = (lane < GLA_DK) if hd % 2 == 0 else (lane >= GLA_DK)
            qd_m = jnp.where(keep, q_dec[:, pair], 0.0).astype(BF16)
            qg_m = jnp.where(keep, q_glob[:, pair], 0.0).astype(BF16)
            ke_m = jnp.where(keep, k_end[:, pair], 0.0).astype(BF16)
            a_rows = [
                lax.dot_general(qd_m[ck * j:ck * (j + 1), :], k_rel[j][:, pair].astype(BF16),
                                (((1,), (1,)), ((), ())), preferred_element_type=F32)
                for j in range(n_chunk)
            ]
            a = jnp.where(tri, jnp.concatenate(a_rows, axis=0), 0.0).astype(BF16)
            hv = slice(hd * GLA_DV, (hd + 1) * GLA_DV)
            s_t = st_ref[d, hd]
            o = jnp.dot(a, v[:, hv], preferred_element_type=F32)
            o = o + lax.dot_general(qg_m, s_t.astype(BF16), (((1,), (1,)), ((), ())),
                                    preferred_element_type=F32)
            o_r[:, hv] = o.astype(BF16)
            ds_t = lax.dot_general(v[:, hv], ke_m, (((0,), (0,)), ((), ())), preferred_element_type=F32)
            st_ref[d, hd] = s_t * decay[:, pair] + ds_t

    @pl.when(i == pl.num_programs(1) - 1)
    def _():
        for d in range(2):
            for hd in range(GLA_HEADS):
                t = st_ref[d, hd].T
                sfin_ref[0, d, hd] = t[(hd % 2) * GLA_DK:(hd % 2 + 1) * GLA_DK, :]


def _gla(lq, lk, lv, la, s0, n_batch, n_tok, tb):
    nb = n_tok // tb
    t_total = n_batch * n_tok
    has_s0 = s0 is not None
    fwd = lambda b, i: (b * nb + i, 0)
    bwd = lambda b, i: (b * nb + nb - 1 - i, 0)
    bwd_la = lambda b, i: (b * nb + nb - 1 - i, 1)
    in_specs = [
        pl.BlockSpec((tb, GLA_W), fwd), pl.BlockSpec((tb, GLA_W), fwd),
        pl.BlockSpec((tb, BRANCH_W), fwd), pl.BlockSpec((tb, GLA_W), fwd),
        pl.BlockSpec((tb, GLA_W), bwd), pl.BlockSpec((tb, GLA_W), bwd),
        pl.BlockSpec((tb, BRANCH_W), bwd), pl.BlockSpec((tb, GLA_W), bwd_la),
    ]
    args = [lq, lk, lv, la, lq, lk, lv, la]
    state_spec = pl.BlockSpec((1, 2, GLA_HEADS, GLA_DK, GLA_DV), lambda b, i: (b, 0, 0, 0, 0))
    if has_s0:
        in_specs.append(state_spec)
        args.append(s0)
    return pl.pallas_call(
        functools.partial(_gla_kernel, has_s0=has_s0, n_chunk=tb // GLA_CHUNK),
        grid=(n_batch, nb),
        in_specs=in_specs,
        out_specs=[pl.BlockSpec((tb, BRANCH_W), fwd), pl.BlockSpec((tb, BRANCH_W), bwd), state_spec],
        out_shape=[
            jax.ShapeDtypeStruct((t_total, BRANCH_W), BF16),
            jax.ShapeDtypeStruct((t_total, BRANCH_W), BF16),
            jax.ShapeDtypeStruct((n_batch, 2, GLA_HEADS, GLA_DK, GLA_DV), F32),
        ],
        scratch_shapes=[pltpu.VMEM((2, GLA_HEADS, GLA_DV, LANES), F32)],
        compiler_params=pltpu.CompilerParams(
            dimension_semantics=("parallel", "arbitrary"), vmem_limit_bytes=VMEM_LIMIT),
        name="gla",
    )(*args)


def _merge_kernel(x_ref, mod_ref, og_ref, of_ref, ob_ref, lgo_ref, od_ref, gate_ref,
                  gout_ref, wb_ref, wo_ref, y_ref):
    o_gla = of_ref[...].astype(F32) + ob_ref[...].astype(F32)
    gla_parts = []
    for hd in range(GLA_HEADS):
        blk = o_gla[:, hd * GLA_DV:(hd + 1) * GLA_DV]
        ms = jnp.mean(blk * blk, axis=-1, keepdims=True)
        gla_parts.append(blk * lax.rsqrt(ms + EPS) * gout_ref[...])
    gla = jnp.concatenate(gla_parts, axis=1) * _silu(lgo_ref[...].astype(F32))
    branches = (og_ref[...], gla.astype(BF16), od_ref[...])
    mixed = None
    for b, ob in enumerate(branches):
        y = jnp.dot(ob, wb_ref[b], preferred_element_type=F32)
        y = y * gate_ref[:, b * D_MODEL:(b + 1) * D_MODEL].astype(F32)
        mixed = y if mixed is None else mixed + y
    out = jnp.dot(mixed.astype(BF16), wo_ref[...], preferred_element_type=F32)
    y_ref[...] = x_ref[...] + mod_ref[0, 2:3, :] * out


def _merge(x2d, mod, og, o_f, o_b, lgo, od, gates, lw, sw, l, tiles_per_mod, tm):
    t_total = x2d.shape[0]
    tile = lambda c: pl.BlockSpec((tm, c), lambda i: (i, 0))
    return pl.pallas_call(
        _merge_kernel,
        grid=(t_total // tm,),
        in_specs=[
            tile(D_MODEL),
            pl.BlockSpec((1, 6, D_MODEL), lambda i: (i // tiles_per_mod, 0, 0)),
            tile(BRANCH_W), tile(BRANCH_W), tile(BRANCH_W), tile(BRANCH_W), tile(BRANCH_W),
            tile(N_BRANCH * D_MODEL),
            _full_spec((1, GLA_DV)),
            _layer_spec((N_BRANCH, BRANCH_W, D_MODEL), l),
            _layer_spec((D_MODEL, D_MODEL), l),
        ],
        out_specs=tile(D_MODEL),
        out_shape=jax.ShapeDtypeStruct((t_total, D_MODEL), F32),
        compiler_params=pltpu.CompilerParams(
            dimension_semantics=("parallel",), vmem_limit_bytes=VMEM_LIMIT),
        name="merge",
    )(x2d, mod, og, o_f, o_b, lgo, od, gates, lw["gout"], sw["w_branch"], sw["w_out"])


FFN_CHUNK = 256


def _ffn_kernel(x_ref, mod_ref, g2_ref, wi_ref, wo_ref, y_ref):
    x = x_ref[...]
    ms = jnp.mean(x * x, axis=-1, keepdims=True)
    h = x * lax.rsqrt(ms + EPS) * g2_ref[...]
    h = h * (1.0 + mod_ref[0, 4:5, :]) + mod_ref[0, 3:4, :]
    hb = h.astype(BF16)
    def up_proj(c0):
        a = jnp.dot(hb, wi_ref[:, c0:c0 + FFN_CHUNK], preferred_element_type=F32)
        u = jnp.dot(hb, wi_ref[:, FFN_HIDDEN + c0:FFN_HIDDEN + c0 + FFN_CHUNK], preferred_element_type=F32)
        return a, u

    acc = None
    chunks = list(range(0, FFN_HIDDEN, FFN_CHUNK))
    nxt = up_proj(chunks[0])
    for i, c0 in enumerate(chunks):
        a, u = nxt
        if i + 1 < len(chunks):
            nxt = up_proj(chunks[i + 1])
        act = (_silu(a) * u).astype(BF16)
        part = jnp.dot(act, wo_ref[c0:c0 + FFN_CHUNK, :], preferred_element_type=F32)
        acc = part if acc is None else acc + part
    y_ref[...] = x + mod_ref[0, 5:6, :] * acc


def _ffn(x2d, mod, lw, sw, l, tiles_per_mod, tm):
    t_total = x2d.shape[0]
    tile = lambda c: pl.BlockSpec((tm, c), lambda i: (i, 0))
    return pl.pallas_call(
        _ffn_kernel,
        grid=(t_total // tm,),
        in_specs=[
            tile(D_MODEL),
            pl.BlockSpec((1, 6, D_MODEL), lambda i: (i // tiles_per_mod, 0, 0)),
            _full_spec((1, D_MODEL)),
            _layer_spec((D_MODEL, 2 * FFN_HIDDEN), l),
            _layer_spec((FFN_HIDDEN, D_MODEL), l),
        ],
        out_specs=tile(D_MODEL),
        out_shape=jax.ShapeDtypeStruct((t_total, D_MODEL), F32),
        compiler_params=pltpu.CompilerParams(
            dimension_semantics=("parallel",), vmem_limit_bytes=VMEM_LIMIT),
        name="ffn",
    )(x2d, mod, lw["norm2"], sw["w_ffn_in"], sw["w_ffn_out"])


def _shared_weights(p):
    w = p["w_in"]
    pad = jnp.zeros(w.shape[:2] + (N_A - SPLIT_AB,), BF16)
    return {
        "w_a": jnp.concatenate([w[:, :, :SPLIT_AB].astype(BF16), pad], axis=2),
        "w_b": w[:, :, SPLIT_AB:].astype(BF16),
        "w_branch": p["w_branch"].astype(BF16), "w_out": p["w_out"].astype(BF16),
        "w_ffn_in": p["w_ffn_in"].astype(BF16), "w_ffn_out": p["w_ffn_out"].astype(BF16),
    }


def _layer_weights(l, p):
    scale = HEAD_DIM ** -0.5 * math.log2(math.e)
    head_bound = lambda g, s: HEAD_DIM * (NORM_MARGIN * s) ** 2 * jnp.max(g * g)
    norm_bounds = lambda gq_, gk_: jnp.stack([head_bound(gq_, scale), head_bound(gk_, 1.0)])
    gq_row = jnp.concatenate([jnp.tile(p["gqa_q_norm"][l], GQA_HEADS),
                              jnp.tile(p["diff_q_norm"][l], 2 * DIFF_HEADS)]) * scale
    gk_row = jnp.concatenate([jnp.tile(p["gqa_k_norm"][l], GQA_KV_HEADS),
                              jnp.tile(p["diff_k_norm"][l], 2 * DIFF_HEADS)])
    idx = jnp.arange(2 * LANES) // HEAD_DIM
    gmat = jnp.where(idx[:, None] == idx[None, :], 1.0 / HEAD_DIM, 0.0).astype(BF16)
    up = jnp.zeros((LANES, 2 * GLA_W), F32)
    up = up.at[0:GLA_RANK, 0:GLA_W].set(p["gla_alpha_up"][l, 0])
    up = up.at[GLA_RANK:2 * GLA_RANK, GLA_W:].set(p["gla_alpha_up"][l, 1])
    ub = p["gla_alpha_bias"][l].reshape(1, 2 * GLA_W)
    up_hi = up.astype(BF16)
    up = jnp.stack([up_hi, (up - up_hi.astype(F32)).astype(BF16)])
    return {
        "gq": gq_row.reshape(1, 2 * N_QG), "gk": gk_row.reshape(1, LANES + BRANCH_W), "gmat": gmat,
        "up": up, "ub": ub,
        "kb_g": norm_bounds(p["gqa_q_norm"][l], p["gqa_k_norm"][l]),
        "kb_d": norm_bounds(p["diff_q_norm"][l], p["diff_k_norm"][l]),
        "norm1": p["norm1"][l].reshape(1, D_MODEL), "norm2": p["norm2"][l].reshape(1, D_MODEL),
        "gout": p["gla_out_norm"][l].reshape(1, GLA_DV),
        "gsub": p["diff_sub_norm"][l].reshape(1, DIFF_DV),
        "lam": p["diff_lambda"][l],
    }


def _rope_tables(n_tokens):
    n_rows = n_tokens // GRID_W
    row = jnp.repeat(jnp.arange(n_rows, dtype=F32), GRID_W)
    col = jnp.tile(jnp.arange(GRID_W, dtype=F32), n_rows)
    n_freq = HEAD_DIM // 4
    freqs = ROPE_THETA ** (-jnp.arange(n_freq, dtype=F32) / n_freq)
    ar, ac = row[:, None] * freqs, col[:, None] * freqs
    cos = jnp.concatenate([jnp.cos(ar), jnp.cos(ar), jnp.cos(ac), jnp.cos(ac)], axis=1)
    sin = jnp.concatenate([-jnp.sin(ar), jnp.sin(ar), -jnp.sin(ac), jnp.sin(ac)], axis=1)
    return jnp.tile(cos, (1, LANES // HEAD_DIM)), jnp.tile(sin, (1, LANES // HEAD_DIM))


def _with_ones_rows(v_t):
    lead = v_t.shape[:-2]
    s = v_t.shape[-1]
    ones = jnp.ones(lead + (1, s), v_t.dtype)
    zeros = jnp.zeros(lead + (BF16_ROWS - 1, s), v_t.dtype)
    return jnp.concatenate([v_t, ones, zeros], axis=-2).astype(BF16)


def _cache_layouts(l, cache_gqa_k, cache_gqa_v, cache_diff_k, cache_diff_v):
    b = cache_gqa_k.shape[0]
    gk = jnp.transpose(cache_gqa_k[:, l], (0, 2, 1, 3))
    kd_g = jnp.concatenate([gk, gk], axis=-1).astype(BF16)
    vt_g = _with_ones_rows(jnp.transpose(cache_gqa_v[:, l], (0, 2, 3, 1)))
    dk = jnp.transpose(cache_diff_k[:, l], (0, 2, 3, 1, 4)).reshape(b, 2 * DIFF_HEADS, PAST_LEN, HEAD_DIM)
    kd_d = jnp.concatenate([dk, dk], axis=-1).astype(BF16)
    vt_d = _with_ones_rows(jnp.transpose(cache_diff_v[:, l], (0, 2, 3, 1)))
    return kd_g, vt_g, kd_d, vt_d


def _run_layer(x2d, mod, lw, sw, l, n_batch, n_tok, rope_tabs, ctx, emit_cache, cfg):
    tiles_per_mod = lambda tm: (n_tok // tm) if mod.shape[0] > 1 else (n_batch * n_tok // tm)
    if ctx is None:
        cache_kv = kd_gc = kd_dc = s0 = None
    else:
        kd_gc, vt_gc, kd_dc, vt_dc, s0 = ctx
        cache_kv = (kd_gc, kd_dc, vt_gc, vt_dc)
    outs = _in_projection(x2d, mod, lw, sw, l, n_batch, n_tok, cache_kv, rope_tabs, emit_cache, cfg["tm_in"])
    qg, qd, kdg, kdd, vtg, vtd, lq, lk, lv, lgo, la, gates = outs[:12]
    og = _gqa_attention(qg, lw["kb_g"], kdg, vtg, kd_gc, n_batch, n_tok, cfg["tq_gqa"], cfg["tk"],
                        cfg["hp_gqa"])
    lam_init = 0.8 - 0.6 * math.exp(-0.3 * l)
    od = _diff_attention(qd, lw["kb_d"], lw["lam"], lw["gsub"], kdd, vtd, kd_dc, n_batch, n_tok,
                         cfg["tq_diff"], cfg["tk"], cfg["hp_diff"], lam_init)
    o_f, o_b, s_fin = _gla(lq, lk, lv, la, s0, n_batch, n_tok, cfg["tb_gla"])
    x2d = _merge(x2d, mod, og, o_f, o_b, lgo, od, gates, lw, sw, l, tiles_per_mod(cfg["tm"]), cfg["tm"])
    x2d = _ffn(x2d, mod, lw, sw, l, tiles_per_mod(cfg["tm"]), cfg["tm"])
    cache = tuple(outs[12:]) + (s_fin,) if emit_cache else None
    return x2d, cache


PROMPT_CFG = dict(tm_in=256, tq_gqa=256, tq_diff=256, tk=256, hp_gqa=2, hp_diff=4, tb_gla=256, tm=512)
SAMPLE_CFG = dict(tm_in=512, tq_gqa=1024, tq_diff=2048, tk=1536, hp_gqa=1, hp_diff=1, tb_gla=256, tm=512)


def kernel(x_prompt, x_sample, c, cache_gqa_k, cache_gqa_v, state_gla, cache_diff_k, cache_diff_v, c_ctx, w_mod, b_mod, norm1, norm2, w_in, gqa_q_norm, gqa_k_norm, gla_alpha_up, gla_alpha_bias, gla_out_norm, diff_q_norm, diff_k_norm, diff_lambda, diff_sub_norm, w_branch, w_out, w_ffn_in, w_ffn_out):
    p = {
        "norm1": norm1, "norm2": norm2, "w_in": w_in, "gqa_q_norm": gqa_q_norm, "gqa_k_norm": gqa_k_norm,
        "gla_alpha_up": gla_alpha_up, "gla_alpha_bias": gla_alpha_bias, "gla_out_norm": gla_out_norm,
        "diff_q_norm": diff_q_norm, "diff_k_norm": diff_k_norm, "diff_lambda": diff_lambda,
        "diff_sub_norm": diff_sub_norm, "w_branch": w_branch, "w_out": w_out,
        "w_ffn_in": w_ffn_in, "w_ffn_out": w_ffn_out,
    }
    n_ctx_b, n_ctx = x_prompt.shape[:2]
    n_lat_b, n_lat = x_sample.shape[:2]
    cond_rows = jnp.concatenate(
        [c_ctx[None, :], c, jnp.zeros((8 - 1 - n_lat_b, D_MODEL), F32)], axis=0)
    mod_all = _modulation(cond_rows, w_mod, b_mod)
    weights = [_layer_weights(l, p) for l in range(DEPTH)]
    shared = _shared_weights(p)

    y = x_prompt.reshape(n_ctx_b * n_ctx, D_MODEL)
    caches = []
    for l in range(DEPTH):
        mod = mod_all[l, 0:1].reshape(1, 6, D_MODEL)
        y, cache = _run_layer(y, mod, weights[l], shared, l, n_ctx_b, n_ctx, None, None, True, PROMPT_CFG)
        caches.append(cache)
    y_prompt = y.reshape(n_ctx_b, n_ctx, D_MODEL)
    stack = lambda j, shape: jnp.stack([cc[j].reshape(shape) for cc in caches], axis=1)
    new_gqa_k = stack(0, (n_ctx_b, n_ctx, GQA_KV_HEADS, HEAD_DIM))
    new_diff_k = stack(1, (n_ctx_b, n_ctx, DIFF_HEADS, 2, HEAD_DIM))
    new_gqa_v = stack(2, (n_ctx_b, n_ctx, GQA_KV_HEADS, HEAD_DIM))
    new_diff_v = stack(3, (n_ctx_b, n_ctx, DIFF_HEADS, DIFF_DV))
    new_state_gla = jnp.stack([cc[4] for cc in caches], axis=1)

    rope_tabs = _rope_tables(n_lat)
    y = x_sample.reshape(n_lat_b * n_lat, D_MODEL)
    for l in range(DEPTH):
        mod = mod_all[l, 1:1 + n_lat_b].reshape(n_lat_b, 6, D_MODEL)
        ctx = _cache_layouts(l, cache_gqa_k, cache_gqa_v, cache_diff_k, cache_diff_v) + (state_gla[:, l],)
        y, _ = _run_layer(y, mod, weights[l], shared, l, n_lat_b, n_lat, rope_tabs, ctx, False, SAMPLE_CFG)
    y_sample = y.reshape(n_lat_b, n_lat, D_MODEL)
    return (y_prompt, y_sample, new_gqa_k, new_gqa_v, new_state_gla, new_diff_k, new_diff_v)
```

```python
import functools
import math

import jax
import jax.numpy as jnp
from jax import lax
from jax.experimental import pallas as pl
from jax.experimental.pallas import tpu as pltpu

D_MODEL = 1024
DEPTH = 2
PAST_LEN = 512
GRID_W = 64
HEAD_DIM = 64
GQA_HEADS = 8
GQA_KV_HEADS = 2
GQA_GROUP = GQA_HEADS // GQA_KV_HEADS
GLA_HEADS = 4
GLA_DK = 64
GLA_DV = 128
GLA_RANK = 16
GLA_TAU = 16.0
GLA_CHUNK = 64
DIFF_HEADS = 4
DIFF_DV = 2 * HEAD_DIM
N_BRANCH = 3
BRANCH_W = 512
FFN_HIDDEN = ((8 * D_MODEL + 3 * 256 - 1) // (3 * 256)) * 256
ROPE_THETA = 10000.0
EPS = 1e-6

LANES = 128
BF16_ROWS = 16
VMEM_LIMIT = 56 * 1024 * 1024

F32 = jnp.float32
BF16 = jnp.bfloat16
HIGHEST = lax.Precision.HIGHEST

A_GQ = 0
A_KV = A_GQ + GQA_HEADS * HEAD_DIM
A_GLA = A_KV + 2 * GQA_KV_HEADS * HEAD_DIM
SPLIT_AB = A_GLA + 2 * GLA_HEADS * GLA_DK + GLA_HEADS * GLA_DV + 2 * GLA_RANK
N_A = -(-SPLIT_AB // LANES) * LANES
B_LGO = 0
B_DQ = B_LGO + GLA_HEADS * GLA_DV
B_DK = B_DQ + DIFF_HEADS * 2 * HEAD_DIM
B_DV = B_DK + DIFF_HEADS * 2 * HEAD_DIM
B_MG = B_DV + DIFF_HEADS * DIFF_DV
N_B = B_MG + N_BRANCH * D_MODEL
N_QG = GQA_HEADS * HEAD_DIM
GLA_W = GLA_HEADS * GLA_DK
GQA_VT_ROWS = HEAD_DIM + BF16_ROWS
DIFF_VT_ROWS = DIFF_DV + BF16_ROWS


def _sigmoid(x):
    return 1.0 / (1.0 + jnp.exp(-x))


def _silu(x):
    return x * _sigmoid(x)


def _bf16_terms(x):
    hi = x.astype(BF16)
    r1 = x - hi.astype(F32)
    mid = r1.astype(BF16)
    lo = (r1 - mid.astype(F32)).astype(BF16)
    return hi, mid, lo


def _full_spec(shape):
    n = len(shape)
    return pl.BlockSpec(shape, lambda *_: (0,) * n, pipeline_mode=pl.Buffered(1))


def _layer_spec(shape, l):
    n = len(shape)
    return pl.BlockSpec((None,) + tuple(shape), lambda *_: (l,) + (0,) * n, pipeline_mode=pl.Buffered(1))


def _mod_kernel(cond_ref, w_ref, b_ref, o_ref):
    s = _silu(cond_ref[...])
    o_ref[0] = jnp.dot(s, w_ref[0], precision=HIGHEST, preferred_element_type=F32) + b_ref[0]


def _modulation(cond_rows, w_mod, b_mod):
    tn = 1536
    n_out = 6 * D_MODEL
    return pl.pallas_call(
        _mod_kernel,
        grid=(DEPTH, n_out // tn),
        in_specs=[
            pl.BlockSpec((8, D_MODEL), lambda l, j: (0, 0)),
            pl.BlockSpec((1, D_MODEL, tn), lambda l, j: (l, 0, j)),
            pl.BlockSpec((1, 1, tn), lambda l, j: (l, 0, j)),
        ],
        out_specs=pl.BlockSpec((1, 8, tn), lambda l, j: (l, 0, j)),
        out_shape=jax.ShapeDtypeStruct((DEPTH, 8, n_out), F32),
        compiler_params=pltpu.CompilerParams(
            dimension_semantics=("parallel", "parallel"), vmem_limit_bytes=VMEM_LIMIT),
        name="modulation",
    )(cond_rows, w_mod, b_mod.reshape(DEPTH, 1, n_out))


def _head_norm(z, gain, gmat):
    outs = []
    n = z.shape[1]
    for c0 in range(0, n, 2 * LANES):
        w = min(2 * LANES, n - c0)
        zz = z[:, c0:c0 + w]
        ms = jnp.dot((zz * zz).astype(BF16), gmat[0:w, 0:w], preferred_element_type=F32)
        outs.append(zz * lax.rsqrt(ms + EPS) * gain[:, c0:c0 + w])
    return outs


def _rope_block(zb, cos, sin, first):
    partner = jnp.where(first, pltpu.roll(zb, LANES - 16, 1), pltpu.roll(zb, 16, 1))
    return zb * cos + partner * sin


N_IN_BASE = 10


def _in_kernel(*refs, rope, emit_cache, tiles_per_batch, cache_tiles):
    if not cache_tiles:
        _in_tile(refs, rope, emit_cache)
        return
    kcg_ref, kcd_ref, vcg_ref, vcd_ref = refs[:4]
    j = pl.program_id(1)

    @pl.when(j < tiles_per_batch)
    def _():
        _in_tile(refs[4:], rope, emit_cache)

    @pl.when(j >= tiles_per_batch)
    def _():
        n_in = 4 + N_IN_BASE + (2 if rope else 0)
        kdg_ref, kdd_ref, vtg_ref, vtd_ref = refs[n_in + 2:n_in + 6]
        kdg_ref[...] = kcg_ref[...]
        kdd_ref[...] = kcd_ref[...]
        vtg_ref[...] = vcg_ref[...]
        vtd_ref[...] = vcd_ref[...]


def _in_tile(refs, rope, emit_cache):
    it = iter(refs)
    (x_ref, mod_ref, g1_ref, wa_ref, wb_ref, gq_ref, gk_ref, gmat_ref, up_ref, ub_ref) = (
        next(it) for _ in range(N_IN_BASE))
    if rope:
        cos_ref, sin_ref = next(it), next(it)
    (qg_ref, qd_ref, kdg_ref, kdd_ref, vtg_ref, vtd_ref,
     lq_ref, lk_ref, lv_ref, lgo_ref, la_ref, gate_ref) = (next(it) for _ in range(12))
    if emit_cache:
        ck_g_ref, ck_d_ref, cv_g_ref, cv_d_ref = (next(it) for _ in range(4))

    tm = x_ref.shape[0]
    x = x_ref[...]
    ms = jnp.mean(x * x, axis=-1, keepdims=True)
    h = x * lax.rsqrt(ms + EPS) * g1_ref[...]
    h = h * (1.0 + mod_ref[0, 1:2, :]) + mod_ref[0, 0:1, :]
    hb = h.astype(BF16)

    def proj(w_ref, c0, n):
        return jnp.dot(hb, w_ref[:, c0:c0 + n], preferred_element_type=F32)

    lane = lax.broadcasted_iota(jnp.int32, (tm, LANES), 1)
    first16 = (lane % 32) < 16
    lo_half = lane < HEAD_DIM
    gmat = gmat_ref[...]
    if rope:
        cos, sin = cos_ref[...], sin_ref[...]

    def blocks128(parts):
        out = []
        for p in parts:
            for c in range(0, p.shape[1], LANES):
                out.append(p[:, c:c + LANES])
        return out

    def store_queries(z, gain, dst):
        for j, zb in enumerate(blocks128(_head_norm(z, gain, gmat))):
            if rope:
                zb = _rope_block(zb, cos, sin, first16)
            dst[:, j * LANES:(j + 1) * LANES] = zb.astype(BF16)

    def store_keys(z, gain, dst, cache_dst):
        for j, zb in enumerate(blocks128(_head_norm(z, gain, gmat))):
            if rope:
                zb = _rope_block(zb, cos, sin, first16)
            if emit_cache:
                cache_dst[:, j * LANES:(j + 1) * LANES] = zb
            sw = pltpu.roll(zb, HEAD_DIM, 1)
            dst[0, 2 * j] = jnp.where(lo_half, zb, sw).astype(BF16)
            dst[0, 2 * j + 1] = jnp.where(lo_half, sw, zb).astype(BF16)

    z_gq = proj(wa_ref, A_GQ, N_QG)
    z_dq = proj(wb_ref, B_DQ, N_QG)
    store_queries(z_gq, gq_ref[:, 0:N_QG], qg_ref)
    z_kv = proj(wa_ref, A_KV, 2 * LANES)
    store_queries(z_dq, gq_ref[:, N_QG:2 * N_QG], qd_ref)
    z_dk = proj(wb_ref, B_DK, BRANCH_W)
    store_keys(z_kv[:, 0:LANES], gk_ref[:, 0:LANES], kdg_ref, ck_g_ref if emit_cache else None)
    z_dv = proj(wb_ref, B_DV, BRANCH_W)
    store_keys(z_dk, gk_ref[:, LANES:LANES + BRANCH_W], kdd_ref, ck_d_ref if emit_cache else None)
    zl = proj(wa_ref, A_GLA, N_A - A_GLA)

    ones_rows = jnp.where(lax.broadcasted_iota(jnp.int32, (BF16_ROWS, tm), 0) == 0, 1.0, 0.0).astype(BF16)
    z_gv = z_kv[:, LANES:2 * LANES]
    if emit_cache:
        cv_g_ref[...] = z_gv
        cv_d_ref[...] = z_dv
    vt = z_gv.T
    for hh in range(GQA_KV_HEADS):
        vtg_ref[0, hh, 0:HEAD_DIM, :] = vt[hh * HEAD_DIM:(hh + 1) * HEAD_DIM, :].astype(BF16)
        vtg_ref[0, hh, HEAD_DIM:GQA_VT_ROWS, :] = ones_rows
    for hh in range(DIFF_HEADS):
        vt = z_dv[:, hh * LANES:(hh + 1) * LANES].T
        vtd_ref[0, hh, 0:DIFF_DV, :] = vt.astype(BF16)
        vtd_ref[0, hh, DIFF_DV:DIFF_VT_ROWS, :] = ones_rows

    z_lgo = proj(wb_ref, B_LGO, BRANCH_W)
    o = 0
    lq_ref[...] = (zl[:, o:o + GLA_W] * (GLA_DK ** -0.5)).astype(BF16)
    o += GLA_W
    lk_ref[...] = zl[:, o:o + GLA_W].astype(BF16)
    o += GLA_W
    lv_ref[...] = zl[:, o:o + BRANCH_W].astype(BF16)
    o += BRANCH_W
    lgo_ref[...] = z_lgo.astype(BF16)
    lr = zl[:, o:o + LANES]
    lr_hi, lr_mid, _ = _bf16_terms(lr)
    z = (jnp.dot(lr_hi, up_ref[0], preferred_element_type=F32)
         + jnp.dot(lr_mid, up_ref[0], preferred_element_type=F32)
         + jnp.dot(lr_hi, up_ref[1], preferred_element_type=F32)) + ub_ref[...]
    log_sig = jnp.minimum(z, 0.0) - jnp.log(1.0 + jnp.exp(-jnp.abs(z)))
    la_ref[...] = log_sig * (1.0 / GLA_TAU)

    for b in range(N_BRANCH):
        gate_ref[:, b * D_MODEL:(b + 1) * D_MODEL] = _sigmoid(
            proj(wb_ref, B_MG + b * D_MODEL, D_MODEL)).astype(BF16)


def _in_projection(x2d, mod, lw, sw, l, n_batch, n_tok, cache_kv, rope_tabs, emit_cache, tm):
    n_cache = 0 if cache_kv is None else cache_kv[0].shape[2]
    assert n_cache % tm == 0
    n_keys = n_tok + n_cache
    t_total = n_batch * n_tok
    tpb = n_tok // tm
    cache_tiles = n_cache // tm
    rope = rope_tabs is not None
    own = lambda j: jnp.minimum(j, tpb - 1)
    tile = lambda c: pl.BlockSpec((tm, c), lambda b, j: (b * tpb + own(j), 0))
    in_specs, args = [], []
    if cache_tiles:
        past = lambda j: jnp.maximum(j - tpb, 0)
        in_specs += [
            pl.BlockSpec((1, GQA_KV_HEADS, tm, LANES), lambda b, j: (b, 0, past(j), 0)),
            pl.BlockSpec((1, 2 * DIFF_HEADS, tm, LANES), lambda b, j: (b, 0, past(j), 0)),
            pl.BlockSpec((1, GQA_KV_HEADS, GQA_VT_ROWS, tm), lambda b, j: (b, 0, 0, past(j))),
            pl.BlockSpec((1, DIFF_HEADS, DIFF_VT_ROWS, tm), lambda b, j: (b, 0, 0, past(j))),
        ]
        args += list(cache_kv)
    in_specs += [
        tile(D_MODEL),
        pl.BlockSpec((1, 6, D_MODEL), lambda b, j: (b if mod.shape[0] > 1 else 0, 0, 0)),
        _full_spec((1, D_MODEL)),
        _layer_spec((D_MODEL, N_A), l),
        _layer_spec((D_MODEL, N_B), l),
        _full_spec((1, 2 * N_QG)),
        _full_spec((1, LANES + BRANCH_W)),
        _full_spec((2 * LANES, 2 * LANES)),
        _full_spec((2, LANES, 2 * GLA_W)),
        _full_spec((1, 2 * GLA_W)),
    ]
    args += [x2d, mod, lw["norm1"], sw["w_a"], sw["w_b"], lw["gq"], lw["gk"], lw["gmat"], lw["up"], lw["ub"]]
    if rope:
        in_specs += [pl.BlockSpec((tm, LANES), lambda b, j: (own(j), 0))] * 2
        args += list(rope_tabs)
    sd = jax.ShapeDtypeStruct
    out_shape = [
        sd((t_total, BRANCH_W), BF16), sd((t_total, BRANCH_W), BF16),
        sd((n_batch, GQA_KV_HEADS, n_keys, LANES), BF16), sd((n_batch, 2 * DIFF_HEADS, n_keys, LANES), BF16),
        sd((n_batch, GQA_KV_HEADS, GQA_VT_ROWS, n_keys), BF16), sd((n_batch, DIFF_HEADS, DIFF_VT_ROWS, n_keys), BF16),
        sd((t_total, GLA_W), BF16), sd((t_total, GLA_W), BF16),
        sd((t_total, BRANCH_W), BF16), sd((t_total, BRANCH_W), BF16),
        sd((t_total, 2 * GLA_W), F32), sd((t_total, N_BRANCH * D_MODEL), BF16),
    ]
    out_specs = [
        tile(BRANCH_W), tile(BRANCH_W),
        pl.BlockSpec((1, GQA_KV_HEADS, tm, LANES), lambda b, j: (b, 0, j, 0)),
        pl.BlockSpec((1, 2 * DIFF_HEADS, tm, LANES), lambda b, j: (b, 0, j, 0)),
        pl.BlockSpec((1, GQA_KV_HEADS, GQA_VT_ROWS, tm), lambda b, j: (b, 0, 0, j)),
        pl.BlockSpec((1, DIFF_HEADS, DIFF_VT_ROWS, tm), lambda b, j: (b, 0, 0, j)),
        tile(GLA_W), tile(GLA_W), tile(BRANCH_W), tile(BRANCH_W), tile(2 * GLA_W), tile(N_BRANCH * D_MODEL),
    ]
    if emit_cache:
        out_shape += [sd((t_total, LANES), F32), sd((t_total, BRANCH_W), F32),
                      sd((t_total, LANES), F32), sd((t_total, BRANCH_W), F32)]
        out_specs += [tile(LANES), tile(BRANCH_W), tile(LANES), tile(BRANCH_W)]
    return pl.pallas_call(
        functools.partial(_in_kernel, rope=rope, emit_cache=emit_cache, tiles_per_batch=tpb,
                          cache_tiles=cache_tiles),
        grid=(n_batch, tpb + cache_tiles),
        in_specs=in_specs,
        out_specs=out_specs,
        out_shape=out_shape,
        compiler_params=pltpu.CompilerParams(
            dimension_semantics=("parallel", "arbitrary"), vmem_limit_bytes=VMEM_LIMIT),
        name="in_projection",
    )(*args)


MAX_COL = 4 * LANES
NT_DIMS = (((1,), (1,)), ((), ()))
SCORE_BOUND = 45.0
NORM_MARGIN = 1.05


def _masked_queries(q_blk, keep_low):
    lane = lax.broadcasted_iota(jnp.int32, q_blk.shape, 1)
    keep = (lane < HEAD_DIM) if keep_low else (lane >= HEAD_DIM)
    return jnp.where(keep, q_blk.astype(F32), 0.0).astype(BF16)


def _dup_key_norm2(kd):
    kf = kd.astype(F32)
    return 0.5 * jnp.max(jnp.sum(kf * kf, axis=1, keepdims=True))


def _needs_no_stabiliser(q_norm2, k_norm2):
    return (q_norm2 * k_norm2 <= SCORE_BOUND * SCORE_BOUND).astype(jnp.int32)


KEY_SUB = 512


def _score_units(k_ref, heads_cols):
    tk = k_ref.shape[2]
    sub = min(KEY_SUB, tk)
    return sub, [(kh, vh, s0, c) for s0 in range(0, tk, sub) for kh, vh, cols in heads_cols for c in cols]


def _tile_scores(k_ref, qm_ref, unit, sub, col):
    kh, _, s0, c = unit
    return lax.dot_general(k_ref[0, kh, s0:s0 + sub, :], qm_ref[c * col:(c + 1) * col, :], NT_DIMS,
                           preferred_element_type=F32)


def _plain_tiles(k_ref, vt_ref, qm_ref, acc_ref, heads_cols):
    col = acc_ref.shape[-1]
    sub, units = _score_units(k_ref, heads_cols)
    s_next = _tile_scores(k_ref, qm_ref, units[0], sub, col)
    for i, (_, vh, s0, c) in enumerate(units):
        s = s_next
        if i + 1 < len(units):
            s_next = _tile_scores(k_ref, qm_ref, units[i + 1], sub, col)
        p = jnp.exp2(s).astype(BF16)
        acc_ref[c] += jnp.dot(vt_ref[0, vh, :, s0:s0 + sub], p, preferred_element_type=F32)


def _online_tiles(k_ref, vt_ref, qm_ref, m_ref, acc_ref, heads_cols):
    col = acc_ref.shape[-1]
    sub, units = _score_units(k_ref, heads_cols)
    for unit in units:
        _, vh, s0, c = unit
        s = _tile_scores(k_ref, qm_ref, unit, sub, col)
        m_prev = m_ref[c]
        m_new = jnp.maximum(m_prev, jnp.max(s, axis=0, keepdims=True))
        alpha = jnp.exp2(m_prev - m_new)
        p = jnp.exp2(s - m_new).astype(BF16)
        acc_ref[c] = acc_ref[c] * alpha + jnp.dot(vt_ref[0, vh, :, s0:s0 + sub], p, preferred_element_type=F32)
        m_ref[c] = m_new


def _flash_init(kb_ref, kc_ref, n_key_heads, m_ref, acc_ref, plain_ref):
    k_norm2 = kb_ref[1]
    if kc_ref is not None:
        for kh in range(n_key_heads):
            k_norm2 = jnp.maximum(k_norm2, _dup_key_norm2(kc_ref[0, kh]))
    plain_ref[0] = _needs_no_stabiliser(kb_ref[0], k_norm2)
    m_ref[...] = jnp.full(m_ref.shape, -jnp.inf, F32)
    acc_ref[...] = jnp.zeros(acc_ref.shape, F32)


def _flash_step(k_ref, vt_ref, qm_ref, m_ref, acc_ref, plain_ref, heads_cols):
    @pl.when(plain_ref[0] == 1)
    def _():
        _plain_tiles(k_ref, vt_ref, qm_ref, acc_ref, heads_cols)

    @pl.when(plain_ref[0] != 1)
    def _():
        _online_tiles(k_ref, vt_ref, qm_ref, m_ref, acc_ref, heads_cols)


def _gqa_kernel(*refs, has_cache, tq, hp):
    if has_cache:
        kb_ref, q_ref, k_ref, vt_ref, kc_ref, o_ref, qm_ref, m_ref, acc_ref, plain_ref = refs
    else:
        kb_ref, q_ref, k_ref, vt_ref, o_ref, qm_ref, m_ref, acc_ref, plain_ref = refs
        kc_ref = None
    kt = pl.program_id(3)
    col = acc_ref.shape[-1]
    per_head = tq // col
    per_kv = GQA_GROUP * per_head
    heads_cols = [(hh, hh, range(hh * per_kv, (hh + 1) * per_kv)) for hh in range(hp)]

    @pl.when(kt == 0)
    def _():
        for hh in range(hp):
            for g in range(GQA_GROUP):
                c0 = hh * 2 * LANES + (g // 2) * LANES
                row0 = (hh * GQA_GROUP + g) * tq
                qm_ref[row0:row0 + tq, :] = _masked_queries(q_ref[0, :, c0:c0 + LANES], g % 2 == 0)
        _flash_init(kb_ref, kc_ref, hp, m_ref, acc_ref, plain_ref)

    _flash_step(k_ref, vt_ref, qm_ref, m_ref, acc_ref, plain_ref, heads_cols)

    @pl.when(kt == pl.num_programs(3) - 1)
    def _():
        for hh in range(hp):
            for part in range(per_head):
                heads = []
                for g in range(GQA_GROUP):
                    acc = acc_ref[hh * per_kv + g * per_head + part]
                    heads.append(acc[0:HEAD_DIM, :] / acc[HEAD_DIM:HEAD_DIM + 1, :])
                o_ref[0, part * col:(part + 1) * col, hh * 2 * LANES:(hh + 1) * 2 * LANES] = (
                    jnp.concatenate(heads, axis=0).T.astype(BF16))


def _kv_specs(key_heads_blk, val_heads_blk, vt_rows, tk, kd_c):
    specs = [pl.BlockSpec((1, key_heads_blk, tk, LANES), lambda b, h, qi, kt: (b, h, kt, 0)),
             pl.BlockSpec((1, val_heads_blk, vt_rows, tk), lambda b, h, qi, kt: (b, h, 0, kt))]
    if kd_c is not None:
        specs.append(pl.BlockSpec((1, key_heads_blk, kd_c.shape[2], LANES), lambda b, h, qi, kt: (b, h, 0, 0)))
    return specs


def _gqa_attention(q, k_bound, kd, vt, kd_c, n_batch, n_tok, tq, tk, hp):
    has_cache = kd_c is not None
    nk = kd.shape[2] // tk
    rows = hp * GQA_GROUP * tq
    col = min(MAX_COL, tq)
    q_spec = pl.BlockSpec((1, tq, hp * 2 * LANES), lambda b, h, qi, kt: (b, qi, h))
    in_specs = [pl.BlockSpec(memory_space=pltpu.SMEM), q_spec]
    in_specs += _kv_specs(hp, hp, GQA_VT_ROWS, tk, kd_c)
    args = [k_bound, q.reshape(n_batch, n_tok, BRANCH_W), kd, vt]
    if has_cache:
        args.append(kd_c)
    out = pl.pallas_call(
        functools.partial(_gqa_kernel, has_cache=has_cache, tq=tq, hp=hp),
        grid=(n_batch, GQA_KV_HEADS // hp, n_tok // tq, nk),
        in_specs=in_specs,
        out_specs=q_spec,
        out_shape=jax.ShapeDtypeStruct((n_batch, n_tok, BRANCH_W), BF16),
        scratch_shapes=[
            pltpu.VMEM((rows, LANES), BF16),
            pltpu.VMEM((rows // col, 1, col), F32),
            pltpu.VMEM((rows // col, GQA_VT_ROWS, col), F32),
            pltpu.SMEM((1,), jnp.int32),
        ],
        compiler_params=pltpu.CompilerParams(
            dimension_semantics=("parallel", "parallel", "parallel", "arbitrary"),
            vmem_limit_bytes=VMEM_LIMIT),
        name="gqa_attention",
    )(*args)
    return out.reshape(n_batch * n_tok, BRANCH_W)


def _diff_kernel(*refs, has_cache, lam_init, hp):
    if has_cache:
        (kb_ref, q_ref, lam_ref, gsub_ref, k_ref, vt_ref, kc_ref, o_ref,
         qm_ref, m_ref, acc_ref, plain_ref) = refs
    else:
        kb_ref, q_ref, lam_ref, gsub_ref, k_ref, vt_ref, o_ref, qm_ref, m_ref, acc_ref, plain_ref = refs
        kc_ref = None
    kt = pl.program_id(3)
    tq = q_ref.shape[1]
    col = acc_ref.shape[-1]
    per_map = tq // col
    heads_cols = [(2 * hh + mm, hh, range((2 * hh + mm) * per_map, (2 * hh + mm + 1) * per_map))
                  for hh in range(hp) for mm in range(2)]

    @pl.when(kt == 0)
    def _():
        for hh in range(hp):
            blk = q_ref[0, :, hh * LANES:(hh + 1) * LANES]
            for mm in range(2):
                row0 = (2 * hh + mm) * tq
                qm_ref[row0:row0 + tq, :] = _masked_queries(blk, mm == 0)
        _flash_init(kb_ref, kc_ref, 2 * hp, m_ref, acc_ref, plain_ref)

    _flash_step(k_ref, vt_ref, qm_ref, m_ref, acc_ref, plain_ref, heads_cols)

    @pl.when(kt == pl.num_programs(3) - 1)
    def _():
        lp = lam_ref[...]
        lam = (jnp.exp(jnp.sum(lp[0:1] * lp[1:2], axis=-1, keepdims=True))
               - jnp.exp(jnp.sum(lp[2:3] * lp[3:4], axis=-1, keepdims=True)) + lam_init)
        for hh in range(hp):
            for part in range(per_map):
                a0, a1 = acc_ref[2 * hh * per_map + part], acc_ref[(2 * hh + 1) * per_map + part]
                o0 = a0[0:DIFF_DV, :] / a0[DIFF_DV:DIFF_DV + 1, :]
                o1 = a1[0:DIFF_DV, :] / a1[DIFF_DV:DIFF_DV + 1, :]
                d = (o0 - lam * o1).T
                ms = jnp.mean(d * d, axis=-1, keepdims=True)
                o_ref[0, part * col:(part + 1) * col, hh * LANES:(hh + 1) * LANES] = (
                    d * lax.rsqrt(ms + EPS) * gsub_ref[...] * (1.0 - lam_init)).astype(BF16)


def _diff_attention(q, k_bound, lam_p, gsub, kd, vt, kd_c, n_batch, n_tok, tq, tk, hp, lam_init):
    has_cache = kd_c is not None
    nk = kd.shape[2] // tk
    col = min(MAX_COL, tq)
    rows = hp * 2 * tq
    q_spec = pl.BlockSpec((1, tq, hp * LANES), lambda b, h, qi, kt: (b, qi, h))
    in_specs = [
        pl.BlockSpec(memory_space=pltpu.SMEM),
        q_spec,
        pl.BlockSpec((4, HEAD_DIM), lambda b, h, qi, kt: (0, 0)),
        pl.BlockSpec((1, DIFF_DV), lambda b, h, qi, kt: (0, 0)),
    ]
    in_specs += _kv_specs(2 * hp, hp, DIFF_VT_ROWS, tk, kd_c)
    args = [k_bound, q.reshape(n_batch, n_tok, BRANCH_W), lam_p, gsub, kd, vt]
    if has_cache:
        args.append(kd_c)
    out = pl.pallas_call(
        functools.partial(_diff_kernel, has_cache=has_cache, lam_init=lam_init, hp=hp),
        grid=(n_batch, DIFF_HEADS // hp, n_tok // tq, nk),
        in_specs=in_specs,
        out_specs=q_spec,
        out_shape=jax.ShapeDtypeStruct((n_batch, n_tok, BRANCH_W), BF16),
        scratch_shapes=[
            pltpu.VMEM((rows, LANES), BF16),
            pltpu.VMEM((rows // col, 1, col), F32),
            pltpu.VMEM((rows // col, DIFF_VT_ROWS, col), F32),
            pltpu.SMEM((1,), jnp.int32),
        ],
        compiler_params=pltpu.CompilerParams(
            dimension_semantics=("parallel", "parallel", "parallel", "arbitrary"),
            vmem_limit_bytes=VMEM_LIMIT),
        name="diff_attention",
    )(*args)
    return out.reshape(n_batch * n_tok, BRANCH_W)


def _gla_kernel(*refs, has_s0, n_chunk):
    if has_s0:
        qf, kf, vf, laf, qb, kb, vb, lab, s0_ref, of_ref, ob_ref, sfin_ref, st_ref = refs
    else:
        qf, kf, vf, laf, qb, kb, vb, lab, of_ref, ob_ref, sfin_ref, st_ref = refs
    i = pl.program_id(1)
    ck = GLA_CHUNK

    @pl.when(i == 0)
    def _():
        for d in range(2):
            for hd in range(GLA_HEADS):
                if has_s0:
                    s = s0_ref[0, d, hd]
                    z = jnp.zeros_like(s)
                    padded = jnp.concatenate([s, z] if hd % 2 == 0 else [z, s], axis=0)
                    st_ref[d, hd] = padded.T
                else:
                    st_ref[d, hd] = jnp.zeros((GLA_DV, LANES), F32)

    tb = qf.shape[0]
    r = lax.broadcasted_iota(jnp.int32, (tb, tb), 0)
    c = lax.broadcasted_iota(jnp.int32, (tb, tb), 1)
    lane = lax.broadcasted_iota(jnp.int32, (tb, LANES), 1)
    chunk_id = lax.broadcasted_iota(jnp.int32, (tb, GLA_W), 0) // ck
    zero_row = jnp.zeros((1, GLA_W), F32)
    streams = ((qf, kf, vf, laf, of_ref, c <= r), (qb, kb, vb, lab, ob_ref, c >= r))
    nt = (((1,), (1,)), ((), ()))
    g_all = [sum(jnp.dot(tri.astype(F32).astype(BF16), part, preferred_element_type=F32)
                 for part in _bf16_terms(la_r[...]))
             for (_, _, _, la_r, _, tri) in streams]
    prepared = []
    for d, (q_r, k_r, v_r, la_r, o_r, tri) in enumerate(streams):
        g = g_all[d]
        if d == 0:
            bounds = [zero_row] + [g[ck * j - 1:ck * j, :] for j in range(1, n_chunk)]
            g_end = g[tb - 1:tb, :]
        else:
            bounds = [g[ck * (j + 1):ck * (j + 1) + 1, :] for j in range(n_chunk - 1)] + [zero_row]
            g_end = g[0:1, :]
        b_rows = jnp.concatenate([jnp.broadcast_to(b, (ck, GLA_W)) for b in bounds], axis=0)
        q = q_r[...].astype(F32)
        k = k_r[...].astype(F32)
        q_dec = q * jnp.exp(g - b_rows)
        q_glob = q * jnp.exp(g)
        k_end = k * jnp.exp(g_end - g)
        k_rel = []
        for j in range(n_chunk):
            reach = (chunk_id <= j) if d == 0 else (chunk_id >= j)
            k_rel.append((k * jnp.exp(jnp.where(reach, bounds[j] - g, 0.0))).astype(BF16))
        prepared.append((q_dec, q_glob, k_end, k_rel, jnp.exp(g_end), v_r[...]))

    partial = {}
    for d, (q_dec, q_glob, k_end, k_rel, decay, v) in enumerate(prepared):
        for hd in range(GLA_HEADS):
            pair = slice((hd // 2) * LANES, (hd // 2 + 1) * LANES)
            hv = slice(hd * GLA_DV, (hd + 1) * GLA_DV)
            keep = (lane < GLA_DK) if hd % 2 == 0 else (lane >= GLA_DK)
            qd_m = jnp.where(keep, q_dec[:, pair], 0.0).astype(BF16)
            qg_m = jnp.where(keep, q_glob[:, pair], 0.0).astype(BF16)
            ke_m = jnp.where(keep, k_end[:, pair], 0.0).astype(BF16)
            a_rows = [lax.dot_general(qd_m[ck * j:ck * (j + 1), :], k_rel[j][:, pair], nt,
                                      preferred_element_type=F32) for j in range(n_chunk)]
            s_t = st_ref[d, hd]
            o_state = lax.dot_general(qg_m, s_t.astype(BF16), nt, preferred_element_type=F32)
            ds_t = lax.dot_general(v[:, hv], ke_m, (((0,), (0,)), ((), ())), preferred_element_type=F32)
            st_ref[d, hd] = s_t * decay[:, pair] + ds_t
            partial[d, hd] = (a_rows, o_state)

    for d, (_, _, _, _, o_r, tri) in enumerate(streams):
        v = prepared[d][5]
        for hd in range(GLA_HEADS):
            hv = slice(hd * GLA_DV, (hd + 1) * GLA_DV)
            a_rows, o_state = partial[d, hd]
            a = jnp.where(tri, jnp.concatenate(a_rows, axis=0), 0.0).astype(BF16)
            o_r[:, hv] = (jnp.dot(a, v[:, hv], preferred_element_type=F32) + o_state).astype(BF16)

    @pl.when(i == pl.num_programs(1) - 1)
    def _():
        for d in range(2):
            for hd in range(GLA_HEADS):
                t = st_ref[d, hd].T
                sfin_ref[0, d, hd] = t[(hd % 2) * GLA_DK:(hd % 2 + 1) * GLA_DK, :]


def _gla(lq, lk, lv, la, s0, n_batch, n_tok, tb):
    nb = n_tok // tb
    t_total = n_batch * n_tok
    has_s0 = s0 is not None
    fwd = lambda b, i: (b * nb + i, 0)
    bwd = lambda b, i: (b * nb + nb - 1 - i, 0)
    bwd_la = lambda b, i: (b * nb + nb - 1 - i, 1)
    in_specs = [
        pl.BlockSpec((tb, GLA_W), fwd), pl.BlockSpec((tb, GLA_W), fwd),
        pl.BlockSpec((tb, BRANCH_W), fwd), pl.BlockSpec((tb, GLA_W), fwd),
        pl.BlockSpec((tb, GLA_W), bwd), pl.BlockSpec((tb, GLA_W), bwd),
        pl.BlockSpec((tb, BRANCH_W), bwd), pl.BlockSpec((tb, GLA_W), bwd_la),
    ]
    args = [lq, lk, lv, la, lq, lk, lv, la]
    state_spec = pl.BlockSpec((1, 2, GLA_HEADS, GLA_DK, GLA_DV), lambda b, i: (b, 0, 0, 0, 0))
    if has_s0:
        in_specs.append(state_spec)
        args.append(s0)
    return pl.pallas_call(
        functools.partial(_gla_kernel, has_s0=has_s0, n_chunk=tb // GLA_CHUNK),
        grid=(n_batch, nb),
        in_specs=in_specs,
        out_specs=[pl.BlockSpec((tb, BRANCH_W), fwd), pl.BlockSpec((tb, BRANCH_W), bwd), state_spec],
        out_shape=[
            jax.ShapeDtypeStruct((t_total, BRANCH_W), BF16),
            jax.ShapeDtypeStruct((t_total, BRANCH_W), BF16),
            jax.ShapeDtypeStruct((n_batch, 2, GLA_HEADS, GLA_DK, GLA_DV), F32),
        ],
        scratch_shapes=[pltpu.VMEM((2, GLA_HEADS, GLA_DV, LANES), F32)],
        compiler_params=pltpu.CompilerParams(
            dimension_semantics=("parallel", "arbitrary"), vmem_limit_bytes=VMEM_LIMIT),
        name="gla",
    )(*args)


FFN_CHUNK = 256


def _merge_ffn_kernel(x_ref, mod_ref, og_ref, of_ref, ob_ref, lgo_ref, od_ref, gate_ref,
                      gout_ref, wb_ref, wo_ref, g2_ref, wi_ref, wd_ref, y_ref):
    o_gla = of_ref[...].astype(F32) + ob_ref[...].astype(F32)
    gla_parts = []
    for hd in range(GLA_HEADS):
        blk = o_gla[:, hd * GLA_DV:(hd + 1) * GLA_DV]
        ms = jnp.mean(blk * blk, axis=-1, keepdims=True)
        gla_parts.append(blk * lax.rsqrt(ms + EPS) * gout_ref[...])
    gla = jnp.concatenate(gla_parts, axis=1) * _silu(lgo_ref[...].astype(F32))
    branches = (og_ref[...], gla.astype(BF16), od_ref[...])
    mixed = None
    for b, ob in enumerate(branches):
        y = jnp.dot(ob, wb_ref[b], preferred_element_type=F32)
        y = y * gate_ref[:, b * D_MODEL:(b + 1) * D_MODEL].astype(F32)
        mixed = y if mixed is None else mixed + y
    out = jnp.dot(mixed.astype(BF16), wo_ref[...], preferred_element_type=F32)
    x = x_ref[...] + mod_ref[0, 2:3, :] * out

    ms = jnp.mean(x * x, axis=-1, keepdims=True)
    h = x * lax.rsqrt(ms + EPS) * g2_ref[...]
    h = h * (1.0 + mod_ref[0, 4:5, :]) + mod_ref[0, 3:4, :]
    hb = h.astype(BF16)

    def up_proj(c0):
        a = jnp.dot(hb, wi_ref[:, c0:c0 + FFN_CHUNK], preferred_element_type=F32)
        u = jnp.dot(hb, wi_ref[:, FFN_HIDDEN + c0:FFN_HIDDEN + c0 + FFN_CHUNK], preferred_element_type=F32)
        return a, u

    acc = None
    chunks = list(range(0, FFN_HIDDEN, FFN_CHUNK))
    nxt = up_proj(chunks[0])
    for i, c0 in enumerate(chunks):
        a, u = nxt
        if i + 1 < len(chunks):
            nxt = up_proj(chunks[i + 1])
        act = (_silu(a) * u).astype(BF16)
        part = jnp.dot(act, wd_ref[c0:c0 + FFN_CHUNK, :], preferred_element_type=F32)
        acc = part if acc is None else acc + part
    y_ref[...] = x + mod_ref[0, 5:6, :] * acc


def _merge_ffn(x2d, mod, og, o_f, o_b, lgo, od, gates, lw, sw, l, tiles_per_mod, tm):
    t_total = x2d.shape[0]
    tile = lambda c: pl.BlockSpec((tm, c), lambda i: (i, 0))
    return pl.pallas_call(
        _merge_ffn_kernel,
        grid=(t_total // tm,),
        in_specs=[
            tile(D_MODEL),
            pl.BlockSpec((1, 6, D_MODEL), lambda i: (i // tiles_per_mod, 0, 0)),
            tile(BRANCH_W), tile(BRANCH_W), tile(BRANCH_W), tile(BRANCH_W), tile(BRANCH_W),
            tile(N_BRANCH * D_MODEL),
            _full_spec((1, GLA_DV)),
            _layer_spec((N_BRANCH, BRANCH_W, D_MODEL), l),
            _layer_spec((D_MODEL, D_MODEL), l),
            _full_spec((1, D_MODEL)),
            _layer_spec((D_MODEL, 2 * FFN_HIDDEN), l),
            _layer_spec((FFN_HIDDEN, D_MODEL), l),
        ],
        out_specs=tile(D_MODEL),
        out_shape=jax.ShapeDtypeStruct((t_total, D_MODEL), F32),
        compiler_params=pltpu.CompilerParams(
            dimension_semantics=("parallel",), vmem_limit_bytes=VMEM_LIMIT),
        name="merge_ffn",
    )(x2d, mod, og, o_f, o_b, lgo, od, gates, lw["gout"], sw["w_branch"], sw["w_out"],
      lw["norm2"], sw["w_ffn_in"], sw["w_ffn_out"])


def _shared_weights(p):
    w = p["w_in"]
    pad = jnp.zeros(w.shape[:2] + (N_A - SPLIT_AB,), BF16)
    return {
        "w_a": jnp.concatenate([w[:, :, :SPLIT_AB].astype(BF16), pad], axis=2),
        "w_b": w[:, :, SPLIT_AB:].astype(BF16),
        "w_branch": p["w_branch"].astype(BF16), "w_out": p["w_out"].astype(BF16),
        "w_ffn_in": p["w_ffn_in"].astype(BF16), "w_ffn_out": p["w_ffn_out"].astype(BF16),
    }


def _layer_weights(l, p):
    scale = HEAD_DIM ** -0.5 * math.log2(math.e)
    head_bound = lambda g, s: HEAD_DIM * (NORM_MARGIN * s) ** 2 * jnp.max(g * g)
    norm_bounds = lambda gq_, gk_: jnp.stack([head_bound(gq_, scale), head_bound(gk_, 1.0)])
    gq_row = jnp.concatenate([jnp.tile(p["gqa_q_norm"][l], GQA_HEADS),
                              jnp.tile(p["diff_q_norm"][l], 2 * DIFF_HEADS)]) * scale
    gk_row = jnp.concatenate([jnp.tile(p["gqa_k_norm"][l], GQA_KV_HEADS),
                              jnp.tile(p["diff_k_norm"][l], 2 * DIFF_HEADS)])
    idx = jnp.arange(2 * LANES) // HEAD_DIM
    gmat = jnp.where(idx[:, None] == idx[None, :], 1.0 / HEAD_DIM, 0.0).astype(BF16)
    up = jnp.zeros((LANES, 2 * GLA_W), F32)
    up = up.at[0:GLA_RANK, 0:GLA_W].set(p["gla_alpha_up"][l, 0])
    up = up.at[GLA_RANK:2 * GLA_RANK, GLA_W:].set(p["gla_alpha_up"][l, 1])
    ub = p["gla_alpha_bias"][l].reshape(1, 2 * GLA_W)
    up_hi = up.astype(BF16)
    up = jnp.stack([up_hi, (up - up_hi.astype(F32)).astype(BF16)])
    return {
        "gq": gq_row.reshape(1, 2 * N_QG), "gk": gk_row.reshape(1, LANES + BRANCH_W), "gmat": gmat,
        "up": up, "ub": ub,
        "kb_g": norm_bounds(p["gqa_q_norm"][l], p["gqa_k_norm"][l]),
        "kb_d": norm_bounds(p["diff_q_norm"][l], p["diff_k_norm"][l]),
        "norm1": p["norm1"][l].reshape(1, D_MODEL), "norm2": p["norm2"][l].reshape(1, D_MODEL),
        "gout": p["gla_out_norm"][l].reshape(1, GLA_DV),
        "gsub": p["diff_sub_norm"][l].reshape(1, DIFF_DV),
        "lam": p["diff_lambda"][l],
    }


def _rope_tables(n_tokens):
    n_rows = n_tokens // GRID_W
    row = jnp.repeat(jnp.arange(n_rows, dtype=F32), GRID_W)
    col = jnp.tile(jnp.arange(GRID_W, dtype=F32), n_rows)
    n_freq = HEAD_DIM // 4
    freqs = ROPE_THETA ** (-jnp.arange(n_freq, dtype=F32) / n_freq)
    ar, ac = row[:, None] * freqs, col[:, None] * freqs
    cos = jnp.concatenate([jnp.cos(ar), jnp.cos(ar), jnp.cos(ac), jnp.cos(ac)], axis=1)
    sin = jnp.concatenate([-jnp.sin(ar), jnp.sin(ar), -jnp.sin(ac), jnp.sin(ac)], axis=1)
    return jnp.tile(cos, (1, LANES // HEAD_DIM)), jnp.tile(sin, (1, LANES // HEAD_DIM))


def _with_ones_rows(v_t):
    lead = v_t.shape[:-2]
    s = v_t.shape[-1]
    ones = jnp.ones(lead + (1, s), v_t.dtype)
    zeros = jnp.zeros(lead + (BF16_ROWS - 1, s), v_t.dtype)
    return jnp.concatenate([v_t, ones, zeros], axis=-2).astype(BF16)


def _cache_layouts(l, cache_gqa_k, cache_gqa_v, cache_diff_k, cache_diff_v):
    b = cache_gqa_k.shape[0]
    gk = jnp.transpose(cache_gqa_k[:, l], (0, 2, 1, 3))
    kd_g = jnp.concatenate([gk, gk], axis=-1).astype(BF16)
    vt_g = _with_ones_rows(jnp.transpose(cache_gqa_v[:, l], (0, 2, 3, 1)))
    dk = jnp.transpose(cache_diff_k[:, l], (0, 2, 3, 1, 4)).reshape(b, 2 * DIFF_HEADS, PAST_LEN, HEAD_DIM)
    kd_d = jnp.concatenate([dk, dk], axis=-1).astype(BF16)
    vt_d = _with_ones_rows(jnp.transpose(cache_diff_v[:, l], (0, 2, 3, 1)))
    return kd_g, vt_g, kd_d, vt_d


def _run_layer(x2d, mod, lw, sw, l, n_batch, n_tok, rope_tabs, ctx, emit_cache, cfg):
    tiles_per_mod = lambda tm: (n_tok // tm) if mod.shape[0] > 1 else (n_batch * n_tok // tm)
    if ctx is None:
        cache_kv = kd_gc = kd_dc = s0 = None
    else:
        kd_gc, vt_gc, kd_dc, vt_dc, s0 = ctx
        cache_kv = (kd_gc, kd_dc, vt_gc, vt_dc)
    outs = _in_projection(x2d, mod, lw, sw, l, n_batch, n_tok, cache_kv, rope_tabs, emit_cache, cfg["tm_in"])
    qg, qd, kdg, kdd, vtg, vtd, lq, lk, lv, lgo, la, gates = outs[:12]
    og = _gqa_attention(qg, lw["kb_g"], kdg, vtg, kd_gc, n_batch, n_tok, cfg["tq_gqa"], cfg["tk"],
                        cfg["hp_gqa"])
    lam_init = 0.8 - 0.6 * math.exp(-0.3 * l)
    od = _diff_attention(qd, lw["kb_d"], lw["lam"], lw["gsub"], kdd, vtd, kd_dc, n_batch, n_tok,
                         cfg["tq_diff"], cfg["tk"], cfg["hp_diff"], lam_init)
    o_f, o_b, s_fin = _gla(lq, lk, lv, la, s0, n_batch, n_tok, cfg["tb_gla"])
    x2d = _merge_ffn(x2d, mod, og, o_f, o_b, lgo, od, gates, lw, sw, l, tiles_per_mod(cfg["tm"]), cfg["tm"])
    cache = tuple(outs[12:]) + (s_fin,) if emit_cache else None
    return x2d, cache


PROMPT_CFG = dict(tm_in=256, tq_gqa=256, tq_diff=256, tk=256, hp_gqa=2, hp_diff=4, tb_gla=256, tm=512)
SAMPLE_CFG = dict(tm_in=512, tq_gqa=1024, tq_diff=2048, tk=1536, hp_gqa=1, hp_diff=1, tb_gla=256, tm=512)


def kernel(x_prompt, x_sample, c, cache_gqa_k, cache_gqa_v, state_gla, cache_diff_k, cache_diff_v, c_ctx, w_mod, b_mod, norm1, norm2, w_in, gqa_q_norm, gqa_k_norm, gla_alpha_up, gla_alpha_bias, gla_out_norm, diff_q_norm, diff_k_norm, diff_lambda, diff_sub_norm, w_branch, w_out, w_ffn_in, w_ffn_out):
    p = {
        "norm1": norm1, "norm2": norm2, "w_in": w_in, "gqa_q_norm": gqa_q_norm, "gqa_k_norm": gqa_k_norm,
        "gla_alpha_up": gla_alpha_up, "gla_alpha_bias": gla_alpha_bias, "gla_out_norm": gla_out_norm,
        "diff_q_norm": diff_q_norm, "diff_k_norm": diff_k_norm, "diff_lambda": diff_lambda,
        "diff_sub_norm": diff_sub_norm, "w_branch": w_branch, "w_out": w_out,
        "w_ffn_in": w_ffn_in, "w_ffn_out": w_ffn_out,
    }
    n_ctx_b, n_ctx = x_prompt.shape[:2]
    n_lat_b, n_lat = x_sample.shape[:2]
    cond_rows = jnp.concatenate(
        [c_ctx[None, :], c, jnp.zeros((8 - 1 - n_lat_b, D_MODEL), F32)], axis=0)
    mod_all = _modulation(cond_rows, w_mod, b_mod)
    weights = [_layer_weights(l, p) for l in range(DEPTH)]
    shared = _shared_weights(p)

    y = x_prompt.reshape(n_ctx_b * n_ctx, D_MODEL)
    caches = []
    for l in range(DEPTH):
        mod = mod_all[l, 0:1].reshape(1, 6, D_MODEL)
        y, cache = _run_layer(y, mod, weights[l], shared, l, n_ctx_b, n_ctx, None, None, True, PROMPT_CFG)
        caches.append(cache)
    y_prompt = y.reshape(n_ctx_b, n_ctx, D_MODEL)
    stack = lambda j, shape: jnp.stack([cc[j].reshape(shape) for cc in caches], axis=1)
    new_gqa_k = stack(0, (n_ctx_b, n_ctx, GQA_KV_HEADS, HEAD_DIM))
    new_diff_k = stack(1, (n_ctx_b, n_ctx, DIFF_HEADS, 2, HEAD_DIM))
    new_gqa_v = stack(2, (n_ctx_b, n_ctx, GQA_KV_HEADS, HEAD_DIM))
    new_diff_v = stack(3, (n_ctx_b, n_ctx, DIFF_HEADS, DIFF_DV))
    new_state_gla = jnp.stack([cc[4] for cc in caches], axis=1)

    rope_tabs = _rope_tables(n_lat)
    y = x_sample.reshape(n_lat_b * n_lat, D_MODEL)
    for l in range(DEPTH):
        mod = mod_all[l, 1:1 + n_lat_b].reshape(n_lat_b, 6, D_MODEL)
        ctx = _cache_layouts(l, cache_gqa_k, cache_gqa_v, cache_diff_k, cache_diff_v) + (state_gla[:, l],)
        y, _ = _run_layer(y, mod, weights[l], shared, l, n_lat_b, n_lat, rope_tabs, ctx, False, SAMPLE_CFG)
    y_sample = y.reshape(n_lat_b, n_lat, D_MODEL)
    return (y_prompt, y_sample, new_gqa_k, new_gqa_v, new_state_gla, new_diff_k, new_diff_v)
```

```python
import functools
import math

import jax
import jax.numpy as jnp
from jax import lax
from jax.experimental import pallas as pl
from jax.experimental.pallas import tpu as pltpu

D_MODEL = 1024
DEPTH = 2
PAST_LEN = 512
GRID_W = 64
HEAD_DIM = 64
GQA_HEADS = 8
GQA_KV_HEADS = 2
GQA_GROUP = GQA_HEADS // GQA_KV_HEADS
GLA_HEADS = 4
GLA_DK = 64
GLA_DV = 128
GLA_RANK = 16
GLA_TAU = 16.0
GLA_CHUNK = 64
DIFF_HEADS = 4
DIFF_DV = 2 * HEAD_DIM
N_BRANCH = 3
BRANCH_W = 512
FFN_HIDDEN = ((8 * D_MODEL + 3 * 256 - 1) // (3 * 256)) * 256
ROPE_THETA = 10000.0
EPS = 1e-6

LANES = 128
BF16_ROWS = 16
VMEM_LIMIT = 56 * 1024 * 1024

F32 = jnp.float32
BF16 = jnp.bfloat16
HIGHEST = lax.Precision.HIGHEST

R_GQ = 0
R_KV = R_GQ + GQA_HEADS * HEAD_DIM
R_GLA = R_KV + 2 * GQA_KV_HEADS * HEAD_DIM
R_LR = R_GLA + 2 * GLA_HEADS * GLA_DK + GLA_HEADS * GLA_DV
R_LGO = R_LR + 2 * GLA_RANK
R_DQ = R_LGO + GLA_HEADS * GLA_DV
R_DK = R_DQ + DIFF_HEADS * 2 * HEAD_DIM
R_DV = R_DK + DIFF_HEADS * 2 * HEAD_DIM
R_MG = R_DV + DIFF_HEADS * DIFF_DV
N_IN = R_MG + N_BRANCH * D_MODEL
NT_DIMS = (((1,), (1,)), ((), ()))
N_QG = GQA_HEADS * HEAD_DIM
GLA_W = GLA_HEADS * GLA_DK
GQA_VT_ROWS = HEAD_DIM + BF16_ROWS
DIFF_VT_ROWS = DIFF_DV + BF16_ROWS


def _sigmoid(x):
    return 1.0 / (1.0 + jnp.exp(-x))


def _silu(x):
    return x * _sigmoid(x)


def _bf16_terms(x):
    hi = x.astype(BF16)
    r1 = x - hi.astype(F32)
    mid = r1.astype(BF16)
    lo = (r1 - mid.astype(F32)).astype(BF16)
    return hi, mid, lo


def _full_spec(shape):
    n = len(shape)
    return pl.BlockSpec(shape, lambda *_: (0,) * n, pipeline_mode=pl.Buffered(1))


def _layer_spec(shape, l):
    n = len(shape)
    return pl.BlockSpec((None,) + tuple(shape), lambda *_: (l,) + (0,) * n, pipeline_mode=pl.Buffered(1))


def _mod_kernel(cond_ref, w_ref, b_ref, o_ref):
    s = _silu(cond_ref[...])
    o_ref[0] = jnp.dot(s, w_ref[0], precision=HIGHEST, preferred_element_type=F32) + b_ref[0]


def _modulation(cond_rows, w_mod, b_mod):
    tn = 1536
    n_out = 6 * D_MODEL
    return pl.pallas_call(
        _mod_kernel,
        grid=(DEPTH, n_out // tn),
        in_specs=[
            pl.BlockSpec((8, D_MODEL), lambda l, j: (0, 0)),
            pl.BlockSpec((1, D_MODEL, tn), lambda l, j: (l, 0, j)),
            pl.BlockSpec((1, 1, tn), lambda l, j: (l, 0, j)),
        ],
        out_specs=pl.BlockSpec((1, 8, tn), lambda l, j: (l, 0, j)),
        out_shape=jax.ShapeDtypeStruct((DEPTH, 8, n_out), F32),
        compiler_params=pltpu.CompilerParams(
            dimension_semantics=("parallel", "parallel"), vmem_limit_bytes=VMEM_LIMIT),
        name="modulation",
    )(cond_rows, w_mod, b_mod.reshape(DEPTH, 1, n_out))


def _head_norm(z, gain, gmat):
    outs = []
    n = z.shape[1]
    for c0 in range(0, n, 2 * LANES):
        w = min(2 * LANES, n - c0)
        zz = z[:, c0:c0 + w]
        ms = jnp.dot((zz * zz).astype(BF16), gmat[0:w, 0:w], preferred_element_type=F32)
        outs.append(zz * lax.rsqrt(ms + EPS) * gain[:, c0:c0 + w])
    return outs


def _rope_block(zb, cos, sin, first):
    partner = jnp.where(first, pltpu.roll(zb, LANES - 16, 1), pltpu.roll(zb, 16, 1))
    return zb * cos + partner * sin


N_IN_BASE = 9


def _in_kernel(*refs, rope, emit_cache, tiles_per_batch, cache_tiles):
    if not cache_tiles:
        _in_tile(refs, rope, emit_cache)
        return
    kcg_ref, kcd_ref, vcg_ref, vcd_ref = refs[:4]
    j = pl.program_id(1)

    @pl.when(j < tiles_per_batch)
    def _():
        _in_tile(refs[4:], rope, emit_cache)

    @pl.when(j >= tiles_per_batch)
    def _():
        n_in = 4 + N_IN_BASE + (2 if rope else 0)
        kdg_ref, kdd_ref, vtg_ref, vtd_ref = refs[n_in + 2:n_in + 6]
        kdg_ref[...] = kcg_ref[...]
        kdd_ref[...] = kcd_ref[...]
        vtg_ref[...] = vcg_ref[...]
        vtd_ref[...] = vcd_ref[...]


def _in_tile(refs, rope, emit_cache):
    it = iter(refs)
    (x_ref, mod_ref, g1_ref, wt_ref, gq_ref, gk_ref, gmat_ref, up_ref, ub_ref) = (
        next(it) for _ in range(N_IN_BASE))
    if rope:
        cos_ref, sin_ref = next(it), next(it)
    (qg_ref, qd_ref, kdg_ref, kdd_ref, vtg_ref, vtd_ref,
     lq_ref, lk_ref, lv_ref, lgo_ref, la_ref, gate_ref) = (next(it) for _ in range(12))
    if emit_cache:
        ck_g_ref, ck_d_ref, cv_g_ref, cv_d_ref = (next(it) for _ in range(4))

    tm = x_ref.shape[0]
    x = x_ref[...]
    ms = jnp.mean(x * x, axis=-1, keepdims=True)
    h = x * lax.rsqrt(ms + EPS) * g1_ref[...]
    h = h * (1.0 + mod_ref[0, 1:2, :]) + mod_ref[0, 0:1, :]
    hb = h.astype(BF16)

    def proj(r0, n):
        return lax.dot_general(hb, wt_ref[r0:r0 + n, :], NT_DIMS, preferred_element_type=F32)

    lane = lax.broadcasted_iota(jnp.int32, (tm, LANES), 1)
    first16 = (lane % 32) < 16
    lo_half = lane < HEAD_DIM
    gmat = gmat_ref[...]
    if rope:
        cos, sin = cos_ref[...], sin_ref[...]

    def blocks128(parts):
        out = []
        for p in parts:
            for c in range(0, p.shape[1], LANES):
                out.append(p[:, c:c + LANES])
        return out

    def store_queries(z, gain, dst):
        for j, zb in enumerate(blocks128(_head_norm(z, gain, gmat))):
            if rope:
                zb = _rope_block(zb, cos, sin, first16)
            dst[:, j * LANES:(j + 1) * LANES] = zb.astype(BF16)

    def store_keys(z, gain, dst, cache_dst):
        for j, zb in enumerate(blocks128(_head_norm(z, gain, gmat))):
            if rope:
                zb = _rope_block(zb, cos, sin, first16)
            if emit_cache:
                cache_dst[:, j * LANES:(j + 1) * LANES] = zb
            sw = pltpu.roll(zb, HEAD_DIM, 1)
            dst[0, 2 * j] = jnp.where(lo_half, zb, sw).astype(BF16)
            dst[0, 2 * j + 1] = jnp.where(lo_half, sw, zb).astype(BF16)

    z_gq = proj(R_GQ, N_QG)
    z_dq = proj(R_DQ, N_QG)
    store_queries(z_gq, gq_ref[:, 0:N_QG], qg_ref)
    z_kv = proj(R_KV, 2 * LANES)
    store_queries(z_dq, gq_ref[:, N_QG:2 * N_QG], qd_ref)
    z_dk = proj(R_DK, BRANCH_W)
    store_keys(z_kv[:, 0:LANES], gk_ref[:, 0:LANES], kdg_ref, ck_g_ref if emit_cache else None)
    z_dv = proj(R_DV, BRANCH_W)
    store_keys(z_dk, gk_ref[:, LANES:LANES + BRANCH_W], kdd_ref, ck_d_ref if emit_cache else None)
    zl = proj(R_GLA, R_LR - R_GLA)

    ones_rows = jnp.where(lax.broadcasted_iota(jnp.int32, (BF16_ROWS, tm), 0) == 0, 1.0, 0.0).astype(BF16)
    z_gv = z_kv[:, LANES:2 * LANES]
    if emit_cache:
        cv_g_ref[...] = z_gv
        cv_d_ref[...] = z_dv
    vt = z_gv.T
    for hh in range(GQA_KV_HEADS):
        vtg_ref[0, hh, 0:HEAD_DIM, :] = vt[hh * HEAD_DIM:(hh + 1) * HEAD_DIM, :].astype(BF16)
        vtg_ref[0, hh, HEAD_DIM:GQA_VT_ROWS, :] = ones_rows
    for hh in range(DIFF_HEADS):
        vt = z_dv[:, hh * LANES:(hh + 1) * LANES].T
        vtd_ref[0, hh, 0:DIFF_DV, :] = vt.astype(BF16)
        vtd_ref[0, hh, DIFF_DV:DIFF_VT_ROWS, :] = ones_rows

    z_lgo = proj(R_LGO, BRANCH_W)
    lr = proj(R_LR, LANES)
    o = 0
    lq_ref[...] = (zl[:, o:o + GLA_W] * (GLA_DK ** -0.5)).astype(BF16)
    o += GLA_W
    lk_ref[...] = zl[:, o:o + GLA_W].astype(BF16)
    o += GLA_W
    lv_ref[...] = zl[:, o:o + BRANCH_W].astype(BF16)
    o += BRANCH_W
    lgo_ref[...] = z_lgo.astype(BF16)
    lr_hi, lr_mid, _ = _bf16_terms(lr)
    z = (jnp.dot(lr_hi, up_ref[0], preferred_element_type=F32)
         + jnp.dot(lr_mid, up_ref[0], preferred_element_type=F32)
         + jnp.dot(lr_hi, up_ref[1], preferred_element_type=F32)) + ub_ref[...]
    log_sig = jnp.minimum(z, 0.0) - jnp.log(1.0 + jnp.exp(-jnp.abs(z)))
    la_ref[...] = log_sig * (1.0 / GLA_TAU)

    for b in range(N_BRANCH):
        gate_ref[:, b * D_MODEL:(b + 1) * D_MODEL] = _sigmoid(
            proj(R_MG + b * D_MODEL, D_MODEL)).astype(BF16)


def _in_projection(x2d, mod, lw, sw, l, n_batch, n_tok, cache_kv, rope_tabs, emit_cache, tm):
    n_cache = 0 if cache_kv is None else cache_kv[0].shape[2]
    assert n_cache % tm == 0
    n_keys = n_tok + n_cache
    t_total = n_batch * n_tok
    tpb = n_tok // tm
    cache_tiles = n_cache // tm
    rope = rope_tabs is not None
    own = lambda j: jnp.minimum(j, tpb - 1)
    tile = lambda c: pl.BlockSpec((tm, c), lambda b, j: (b * tpb + own(j), 0))
    in_specs, args = [], []
    if cache_tiles:
        past = lambda j: jnp.maximum(j - tpb, 0)
        in_specs += [
            pl.BlockSpec((1, GQA_KV_HEADS, tm, LANES), lambda b, j: (b, 0, past(j), 0)),
            pl.BlockSpec((1, 2 * DIFF_HEADS, tm, LANES), lambda b, j: (b, 0, past(j), 0)),
            pl.BlockSpec((1, GQA_KV_HEADS, GQA_VT_ROWS, tm), lambda b, j: (b, 0, 0, past(j))),
            pl.BlockSpec((1, DIFF_HEADS, DIFF_VT_ROWS, tm), lambda b, j: (b, 0, 0, past(j))),
        ]
        args += list(cache_kv)
    in_specs += [
        tile(D_MODEL),
        pl.BlockSpec((1, 6, D_MODEL), lambda b, j: (b if mod.shape[0] > 1 else 0, 0, 0)),
        _full_spec((1, D_MODEL)),
        _layer_spec((N_IN, D_MODEL), l),
        _full_spec((1, 2 * N_QG)),
        _full_spec((1, LANES + BRANCH_W)),
        _full_spec((2 * LANES, 2 * LANES)),
        _full_spec((2, LANES, 2 * GLA_W)),
        _full_spec((1, 2 * GLA_W)),
    ]
    args += [x2d, mod, lw["norm1"], sw["w_in_t"], lw["gq"], lw["gk"], lw["gmat"], lw["up"], lw["ub"]]
    if rope:
        in_specs += [pl.BlockSpec((tm, LANES), lambda b, j: (own(j), 0))] * 2
        args += list(rope_tabs)
    sd = jax.ShapeDtypeStruct
    out_shape = [
        sd((t_total, BRANCH_W), BF16), sd((t_total, BRANCH_W), BF16),
        sd((n_batch, GQA_KV_HEADS, n_keys, LANES), BF16), sd((n_batch, 2 * DIFF_HEADS, n_keys, LANES), BF16),
        sd((n_batch, GQA_KV_HEADS, GQA_VT_ROWS, n_keys), BF16), sd((n_batch, DIFF_HEADS, DIFF_VT_ROWS, n_keys), BF16),
        sd((t_total, GLA_W), BF16), sd((t_total, GLA_W), BF16),
        sd((t_total, BRANCH_W), BF16), sd((t_total, BRANCH_W), BF16),
        sd((t_total, 2 * GLA_W), F32), sd((t_total, N_BRANCH * D_MODEL), BF16),
    ]
    out_specs = [
        tile(BRANCH_W), tile(BRANCH_W),
        pl.BlockSpec((1, GQA_KV_HEADS, tm, LANES), lambda b, j: (b, 0, j, 0)),
        pl.BlockSpec((1, 2 * DIFF_HEADS, tm, LANES), lambda b, j: (b, 0, j, 0)),
        pl.BlockSpec((1, GQA_KV_HEADS, GQA_VT_ROWS, tm), lambda b, j: (b, 0, 0, j)),
        pl.BlockSpec((1, DIFF_HEADS, DIFF_VT_ROWS, tm), lambda b, j: (b, 0, 0, j)),
        tile(GLA_W), tile(GLA_W), tile(BRANCH_W), tile(BRANCH_W), tile(2 * GLA_W), tile(N_BRANCH * D_MODEL),
    ]
    if emit_cache:
        out_shape += [sd((t_total, LANES), F32), sd((t_total, BRANCH_W), F32),
                      sd((t_total, LANES), F32), sd((t_total, BRANCH_W), F32)]
        out_specs += [tile(LANES), tile(BRANCH_W), tile(LANES), tile(BRANCH_W)]
    return pl.pallas_call(
        functools.partial(_in_kernel, rope=rope, emit_cache=emit_cache, tiles_per_batch=tpb,
                          cache_tiles=cache_tiles),
        grid=(n_batch, tpb + cache_tiles),
        in_specs=in_specs,
        out_specs=out_specs,
        out_shape=out_shape,
        compiler_params=pltpu.CompilerParams(
            dimension_semantics=("parallel", "arbitrary"), vmem_limit_bytes=VMEM_LIMIT),
        name="in_projection",
    )(*args)


MAX_COL = 4 * LANES
SCORE_BOUND = 45.0
NORM_MARGIN = 1.05


def _masked_queries(q_blk, keep_low):
    lane = lax.broadcasted_iota(jnp.int32, q_blk.shape, 1)
    keep = (lane < HEAD_DIM) if keep_low else (lane >= HEAD_DIM)
    return jnp.where(keep, q_blk.astype(F32), 0.0).astype(BF16)


def _dup_key_norm2(kd):
    kf = kd.astype(F32)
    return 0.5 * jnp.max(jnp.sum(kf * kf, axis=1, keepdims=True))


def _needs_no_stabiliser(q_norm2, k_norm2):
    return (q_norm2 * k_norm2 <= SCORE_BOUND * SCORE_BOUND).astype(jnp.int32)


KEY_SUB = 512


def _score_units(k_ref, heads_cols):
    tk = k_ref.shape[2]
    sub = min(KEY_SUB, tk)
    return sub, [(kh, vh, s0, c) for s0 in range(0, tk, sub) for kh, vh, cols in heads_cols for c in cols]


def _tile_scores(k_ref, qm_ref, unit, sub, col):
    kh, _, s0, c = unit
    return lax.dot_general(k_ref[0, kh, s0:s0 + sub, :], qm_ref[c * col:(c + 1) * col, :], NT_DIMS,
                           preferred_element_type=F32)


def _plain_tiles(k_ref, vt_ref, qm_ref, acc_ref, heads_cols):
    col = acc_ref.shape[-1]
    sub, units = _score_units(k_ref, heads_cols)
    s_next = _tile_scores(k_ref, qm_ref, units[0], sub, col)
    for i, (_, vh, s0, c) in enumerate(units):
        s = s_next
        if i + 1 < len(units):
            s_next = _tile_scores(k_ref, qm_ref, units[i + 1], sub, col)
        p = jnp.exp2(s).astype(BF16)
        acc_ref[c] += jnp.dot(vt_ref[0, vh, :, s0:s0 + sub], p, preferred_element_type=F32)


def _online_tiles(k_ref, vt_ref, qm_ref, m_ref, acc_ref, heads_cols):
    col = acc_ref.shape[-1]
    sub, units = _score_units(k_ref, heads_cols)
    for unit in units:
        _, vh, s0, c = unit
        s = _tile_scores(k_ref, qm_ref, unit, sub, col)
        m_prev = m_ref[c]
        m_new = jnp.maximum(m_prev, jnp.max(s, axis=0, keepdims=True))
        alpha = jnp.exp2(m_prev - m_new)
        p = jnp.exp2(s - m_new).astype(BF16)
        acc_ref[c] = acc_ref[c] * alpha + jnp.dot(vt_ref[0, vh, :, s0:s0 + sub], p, preferred_element_type=F32)
        m_ref[c] = m_new


def _flash_init(kb_ref, kc_ref, n_key_heads, m_ref, acc_ref, plain_ref):
    k_norm2 = kb_ref[1]
    if kc_ref is not None:
        for kh in range(n_key_heads):
            k_norm2 = jnp.maximum(k_norm2, _dup_key_norm2(kc_ref[0, kh]))
    plain_ref[0] = _needs_no_stabiliser(kb_ref[0], k_norm2)
    m_ref[...] = jnp.full(m_ref.shape, -jnp.inf, F32)
    acc_ref[...] = jnp.zeros(acc_ref.shape, F32)


def _flash_step(k_ref, vt_ref, qm_ref, m_ref, acc_ref, plain_ref, heads_cols):
    @pl.when(plain_ref[0] == 1)
    def _():
        _plain_tiles(k_ref, vt_ref, qm_ref, acc_ref, heads_cols)

    @pl.when(plain_ref[0] != 1)
    def _():
        _online_tiles(k_ref, vt_ref, qm_ref, m_ref, acc_ref, heads_cols)


def _gqa_kernel(*refs, has_cache, tq, hp):
    if has_cache:
        kb_ref, q_ref, k_ref, vt_ref, kc_ref, o_ref, qm_ref, m_ref, acc_ref, plain_ref = refs
    else:
        kb_ref, q_ref, k_ref, vt_ref, o_ref, qm_ref, m_ref, acc_ref, plain_ref = refs
        kc_ref = None
    kt = pl.program_id(3)
    col = acc_ref.shape[-1]
    per_head = tq // col
    per_kv = GQA_GROUP * per_head
    heads_cols = [(hh, hh, range(hh * per_kv, (hh + 1) * per_kv)) for hh in range(hp)]

    @pl.when(kt == 0)
    def _():
        for hh in range(hp):
            for g in range(GQA_GROUP):
                c0 = hh * 2 * LANES + (g // 2) * LANES
                row0 = (hh * GQA_GROUP + g) * tq
                qm_ref[row0:row0 + tq, :] = _masked_queries(q_ref[0, :, c0:c0 + LANES], g % 2 == 0)
        _flash_init(kb_ref, kc_ref, hp, m_ref, acc_ref, plain_ref)

    _flash_step(k_ref, vt_ref, qm_ref, m_ref, acc_ref, plain_ref, heads_cols)

    @pl.when(kt == pl.num_programs(3) - 1)
    def _():
        for hh in range(hp):
            for part in range(per_head):
                heads = []
                for g in range(GQA_GROUP):
                    acc = acc_ref[hh * per_kv + g * per_head + part]
                    heads.append(acc[0:HEAD_DIM, :] / acc[HEAD_DIM:HEAD_DIM + 1, :])
                o_ref[0, part * col:(part + 1) * col, hh * 2 * LANES:(hh + 1) * 2 * LANES] = (
                    jnp.concatenate(heads, axis=0).T.astype(BF16))


def _kv_specs(key_heads_blk, val_heads_blk, vt_rows, tk, kd_c):
    specs = [pl.BlockSpec((1, key_heads_blk, tk, LANES), lambda b, h, qi, kt: (b, h, kt, 0)),
             pl.BlockSpec((1, val_heads_blk, vt_rows, tk), lambda b, h, qi, kt: (b, h, 0, kt))]
    if kd_c is not None:
        specs.append(pl.BlockSpec((1, key_heads_blk, kd_c.shape[2], LANES), lambda b, h, qi, kt: (b, h, 0, 0)))
    return specs


def _gqa_attention(q, k_bound, kd, vt, kd_c, n_batch, n_tok, tq, tk, hp):
    has_cache = kd_c is not None
    nk = kd.shape[2] // tk
    rows = hp * GQA_GROUP * tq
    col = min(MAX_COL, tq)
    q_spec = pl.BlockSpec((1, tq, hp * 2 * LANES), lambda b, h, qi, kt: (b, qi, h))
    in_specs = [pl.BlockSpec(memory_space=pltpu.SMEM), q_spec]
    in_specs += _kv_specs(hp, hp, GQA_VT_ROWS, tk, kd_c)
    args = [k_bound, q.reshape(n_batch, n_tok, BRANCH_W), kd, vt]
    if has_cache:
        args.append(kd_c)
    out = pl.pallas_call(
        functools.partial(_gqa_kernel, has_cache=has_cache, tq=tq, hp=hp),
        grid=(n_batch, GQA_KV_HEADS // hp, n_tok // tq, nk),
        in_specs=in_specs,
        out_specs=q_spec,
        out_shape=jax.ShapeDtypeStruct((n_batch, n_tok, BRANCH_W), BF16),
        scratch_shapes=[
            pltpu.VMEM((rows, LANES), BF16),
            pltpu.VMEM((rows // col, 1, col), F32),
            pltpu.VMEM((rows // col, GQA_VT_ROWS, col), F32),
            pltpu.SMEM((1,), jnp.int32),
        ],
        compiler_params=pltpu.CompilerParams(
            dimension_semantics=("parallel", "parallel", "parallel", "arbitrary"),
            vmem_limit_bytes=VMEM_LIMIT),
        name="gqa_attention",
    )(*args)
    return out.reshape(n_batch * n_tok, BRANCH_W)


def _diff_kernel(*refs, has_cache, lam_init, hp):
    if has_cache:
        (kb_ref, q_ref, lam_ref, gsub_ref, k_ref, vt_ref, kc_ref, o_ref,
         qm_ref, m_ref, acc_ref, plain_ref) = refs
    else:
        kb_ref, q_ref, lam_ref, gsub_ref, k_ref, vt_ref, o_ref, qm_ref, m_ref, acc_ref, plain_ref = refs
        kc_ref = None
    kt = pl.program_id(3)
    tq = q_ref.shape[1]
    col = acc_ref.shape[-1]
    per_map = tq // col
    heads_cols = [(2 * hh + mm, hh, range((2 * hh + mm) * per_map, (2 * hh + mm + 1) * per_map))
                  for hh in range(hp) for mm in range(2)]

    @pl.when(kt == 0)
    def _():
        for hh in range(hp):
            blk = q_ref[0, :, hh * LANES:(hh + 1) * LANES]
            for mm in range(2):
                row0 = (2 * hh + mm) * tq
                qm_ref[row0:row0 + tq, :] = _masked_queries(blk, mm == 0)
        _flash_init(kb_ref, kc_ref, 2 * hp, m_ref, acc_ref, plain_ref)

    _flash_step(k_ref, vt_ref, qm_ref, m_ref, acc_ref, plain_ref, heads_cols)

    @pl.when(kt == pl.num_programs(3) - 1)
    def _():
        lp = lam_ref[...]
        lam = (jnp.exp(jnp.sum(lp[0:1] * lp[1:2], axis=-1, keepdims=True))
               - jnp.exp(jnp.sum(lp[2:3] * lp[3:4], axis=-1, keepdims=True)) + lam_init)
        for hh in range(hp):
            for part in range(per_map):
                a0, a1 = acc_ref[2 * hh * per_map + part], acc_ref[(2 * hh + 1) * per_map + part]
                o0 = a0[0:DIFF_DV, :] / a0[DIFF_DV:DIFF_DV + 1, :]
                o1 = a1[0:DIFF_DV, :] / a1[DIFF_DV:DIFF_DV + 1, :]
                d = (o0 - lam * o1).T
                ms = jnp.mean(d * d, axis=-1, keepdims=True)
                o_ref[0, part * col:(part + 1) * col, hh * LANES:(hh + 1) * LANES] = (
                    d * lax.rsqrt(ms + EPS) * gsub_ref[...] * (1.0 - lam_init)).astype(BF16)


def _diff_attention(q, k_bound, lam_p, gsub, kd, vt, kd_c, n_batch, n_tok, tq, tk, hp, lam_init):
    has_cache = kd_c is not None
    nk = kd.shape[2] // tk
    col = min(MAX_COL, tq)
    rows = hp * 2 * tq
    q_spec = pl.BlockSpec((1, tq, hp * LANES), lambda b, h, qi, kt: (b, qi, h))
    in_specs = [
        pl.BlockSpec(memory_space=pltpu.SMEM),
        q_spec,
        pl.BlockSpec((4, HEAD_DIM), lambda b, h, qi, kt: (0, 0)),
        pl.BlockSpec((1, DIFF_DV), lambda b, h, qi, kt: (0, 0)),
    ]
    in_specs += _kv_specs(2 * hp, hp, DIFF_VT_ROWS, tk, kd_c)
    args = [k_bound, q.reshape(n_batch, n_tok, BRANCH_W), lam_p, gsub, kd, vt]
    if has_cache:
        args.append(kd_c)
    out = pl.pallas_call(
        functools.partial(_diff_kernel, has_cache=has_cache, lam_init=lam_init, hp=hp),
        grid=(n_batch, DIFF_HEADS // hp, n_tok // tq, nk),
        in_specs=in_specs,
        out_specs=q_spec,
        out_shape=jax.ShapeDtypeStruct((n_batch, n_tok, BRANCH_W), BF16),
        scratch_shapes=[
            pltpu.VMEM((rows, LANES), BF16),
            pltpu.VMEM((rows // col, 1, col), F32),
            pltpu.VMEM((rows // col, DIFF_VT_ROWS, col), F32),
            pltpu.SMEM((1,), jnp.int32),
        ],
        compiler_params=pltpu.CompilerParams(
            dimension_semantics=("parallel", "parallel", "parallel", "arbitrary"),
            vmem_limit_bytes=VMEM_LIMIT),
        name="diff_attention",
    )(*args)
    return out.reshape(n_batch * n_tok, BRANCH_W)


def _gla_kernel(*refs, has_s0, n_chunk):
    if has_s0:
        qf, kf, vf, laf, qb, kb, vb, lab, s0_ref, of_ref, ob_ref, sfin_ref, st_ref = refs
    else:
        qf, kf, vf, laf, qb, kb, vb, lab, of_ref, ob_ref, sfin_ref, st_ref = refs
    i = pl.program_id(1)
    ck = GLA_CHUNK

    @pl.when(i == 0)
    def _():
        for d in range(2):
            for hd in range(GLA_HEADS):
                if has_s0:
                    s = s0_ref[0, d, hd]
                    z = jnp.zeros_like(s)
                    padded = jnp.concatenate([s, z] if hd % 2 == 0 else [z, s], axis=0)
                    st_ref[d, hd] = padded.T
                else:
                    st_ref[d, hd] = jnp.zeros((GLA_DV, LANES), F32)

    tb = qf.shape[0]
    r = lax.broadcasted_iota(jnp.int32, (tb, tb), 0)
    c = lax.broadcasted_iota(jnp.int32, (tb, tb), 1)
    lane = lax.broadcasted_iota(jnp.int32, (tb, LANES), 1)
    chunk_id = lax.broadcasted_iota(jnp.int32, (tb, GLA_W), 0) // ck
    zero_row = jnp.zeros((1, GLA_W), F32)
    streams = ((qf, kf, vf, laf, of_ref, c <= r), (qb, kb, vb, lab, ob_ref, c >= r))
    nt = (((1,), (1,)), ((), ()))
    g_all = [sum(jnp.dot(tri.astype(F32).astype(BF16), part, preferred_element_type=F32)
                 for part in _bf16_terms(la_r[...]))
             for (_, _, _, la_r, _, tri) in streams]
    prepared = []
    for d, (q_r, k_r, v_r, la_r, o_r, tri) in enumerate(streams):
        g = g_all[d]
        if d == 0:
            bounds = [zero_row] + [g[ck * j - 1:ck * j, :] for j in range(1, n_chunk)]
            g_end = g[tb - 1:tb, :]
        else:
            bounds = [g[ck * (j + 1):ck * (j + 1) + 1, :] for j in range(n_chunk - 1)] + [zero_row]
            g_end = g[0:1, :]
        b_rows = jnp.concatenate([jnp.broadcast_to(b, (ck, GLA_W)) for b in bounds], axis=0)
        q = q_r[...].astype(F32)
        k = k_r[...].astype(F32)
        q_dec = q * jnp.exp(g - b_rows)
        q_glob = q * jnp.exp(g)
        k_end = k * jnp.exp(g_end - g)
        k_rel = []
        for j in range(n_chunk):
            reach = (chunk_id <= j) if d == 0 else (chunk_id >= j)
            k_rel.append((k * jnp.exp(jnp.where(reach, bounds[j] - g, 0.0))).astype(BF16))
        prepared.append((q_dec, q_glob, k_end, k_rel, jnp.exp(g_end), v_r[...]))

    partial = {}
    for d, (q_dec, q_glob, k_end, k_rel, decay, v) in enumerate(prepared):
        for hd in range(GLA_HEADS):
            pair = slice((hd // 2) * LANES, (hd // 2 + 1) * LANES)
            hv = slice(hd * GLA_DV, (hd + 1) * GLA_DV)
            keep = (lane < GLA_DK) if hd % 2 == 0 else (lane >= GLA_DK)
            qd_m = jnp.where(keep, q_dec[:, pair], 0.0).astype(BF16)
            qg_m = jnp.where(keep, q_glob[:, pair], 0.0).astype(BF16)
            ke_m = jnp.where(keep, k_end[:, pair], 0.0).astype(BF16)
            a_rows = [lax.dot_general(qd_m[ck * j:ck * (j + 1), :], k_rel[j][:, pair], nt,
                                      preferred_element_type=F32) for j in range(n_chunk)]
            s_t = st_ref[d, hd]
            o_state = lax.dot_general(qg_m, s_t.astype(BF16), nt, preferred_element_type=F32)
            ds_t = lax.dot_general(v[:, hv], ke_m, (((0,), (0,)), ((), ())), preferred_element_type=F32)
            st_ref[d, hd] = s_t * decay[:, pair] + ds_t
            partial[d, hd] = (a_rows, o_state)

    for d, (_, _, _, _, o_r, tri) in enumerate(streams):
        v = prepared[d][5]
        for hd in range(GLA_HEADS):
            hv = slice(hd * GLA_DV, (hd + 1) * GLA_DV)
            a_rows, o_state = partial[d, hd]
            a = jnp.where(tri, jnp.concatenate(a_rows, axis=0), 0.0).astype(BF16)
            o_r[:, hv] = (jnp.dot(a, v[:, hv], preferred_element_type=F32) + o_state).astype(BF16)

    @pl.when(i == pl.num_programs(1) - 1)
    def _():
        for d in range(2):
            for hd in range(GLA_HEADS):
                t = st_ref[d, hd].T
                sfin_ref[0, d, hd] = t[(hd % 2) * GLA_DK:(hd % 2 + 1) * GLA_DK, :]


def _gla(lq, lk, lv, la, s0, n_batch, n_tok, tb):
    nb = n_tok // tb
    t_total = n_batch * n_tok
    has_s0 = s0 is not None
    fwd = lambda b, i: (b * nb + i, 0)
    bwd = lambda b, i: (b * nb + nb - 1 - i, 0)
    bwd_la = lambda b, i: (b * nb + nb - 1 - i, 1)
    in_specs = [
        pl.BlockSpec((tb, GLA_W), fwd), pl.BlockSpec((tb, GLA_W), fwd),
        pl.BlockSpec((tb, BRANCH_W), fwd), pl.BlockSpec((tb, GLA_W), fwd),
        pl.BlockSpec((tb, GLA_W), bwd), pl.BlockSpec((tb, GLA_W), bwd),
        pl.BlockSpec((tb, BRANCH_W), bwd), pl.BlockSpec((tb, GLA_W), bwd_la),
    ]
    args = [lq, lk, lv, la, lq, lk, lv, la]
    state_spec = pl.BlockSpec((1, 2, GLA_HEADS, GLA_DK, GLA_DV), lambda b, i: (b, 0, 0, 0, 0))
    if has_s0:
        in_specs.append(state_spec)
        args.append(s0)
    return pl.pallas_call(
        functools.partial(_gla_kernel, has_s0=has_s0, n_chunk=tb // GLA_CHUNK),
        grid=(n_batch, nb),
        in_specs=in_specs,
        out_specs=[pl.BlockSpec((tb, BRANCH_W), fwd), pl.BlockSpec((tb, BRANCH_W), bwd), state_spec],
        out_shape=[
            jax.ShapeDtypeStruct((t_total, BRANCH_W), BF16),
            jax.ShapeDtypeStruct((t_total, BRANCH_W), BF16),
            jax.ShapeDtypeStruct((n_batch, 2, GLA_HEADS, GLA_DK, GLA_DV), F32),
        ],
        scratch_shapes=[pltpu.VMEM((2, GLA_HEADS, GLA_DV, LANES), F32)],
        compiler_params=pltpu.CompilerParams(
            dimension_semantics=("parallel", "arbitrary"), vmem_limit_bytes=VMEM_LIMIT),
        name="gla",
    )(*args)


FFN_CHUNK = 256


def _merge_ffn_kernel(x_ref, mod_ref, og_ref, of_ref, ob_ref, lgo_ref, od_ref, gate_ref,
                      gout_ref, wb_ref, wo_ref, g2_ref, wi_ref, wd_ref, y_ref):
    o_gla = of_ref[...].astype(F32) + ob_ref[...].astype(F32)
    gla_parts = []
    for hd in range(GLA_HEADS):
        blk = o_gla[:, hd * GLA_DV:(hd + 1) * GLA_DV]
        ms = jnp.mean(blk * blk, axis=-1, keepdims=True)
        gla_parts.append(blk * lax.rsqrt(ms + EPS) * gout_ref[...])
    gla = jnp.concatenate(gla_parts, axis=1) * _silu(lgo_ref[...].astype(F32))
    branches = (og_ref[...], gla.astype(BF16), od_ref[...])
    mixed = None
    for b, ob in enumerate(branches):
        y = jnp.dot(ob, wb_ref[b], preferred_element_type=F32)
        y = y * gate_ref[:, b * D_MODEL:(b + 1) * D_MODEL].astype(F32)
        mixed = y if mixed is None else mixed + y
    out = jnp.dot(mixed.astype(BF16), wo_ref[...], preferred_element_type=F32)
    x = x_ref[...] + mod_ref[0, 2:3, :] * out

    ms = jnp.mean(x * x, axis=-1, keepdims=True)
    h = x * lax.rsqrt(ms + EPS) * g2_ref[...]
    h = h * (1.0 + mod_ref[0, 4:5, :]) + mod_ref[0, 3:4, :]
    hb = h.astype(BF16)

    def up_proj(c0):
        a = jnp.dot(hb, wi_ref[:, c0:c0 + FFN_CHUNK], preferred_element_type=F32)
        u = jnp.dot(hb, wi_ref[:, FFN_HIDDEN + c0:FFN_HIDDEN + c0 + FFN_CHUNK], preferred_element_type=F32)
        return a, u

    acc = None
    chunks = list(range(0, FFN_HIDDEN, FFN_CHUNK))
    nxt = up_proj(chunks[0])
    for i, c0 in enumerate(chunks):
        a, u = nxt
        if i + 1 < len(chunks):
            nxt = up_proj(chunks[i + 1])
        act = (_silu(a) * u).astype(BF16)
        part = jnp.dot(act, wd_ref[c0:c0 + FFN_CHUNK, :], preferred_element_type=F32)
        acc = part if acc is None else acc + part
    y_ref[...] = x + mod_ref[0, 5:6, :] * acc


def _merge_ffn(x2d, mod, og, o_f, o_b, lgo, od, gates, lw, sw, l, tiles_per_mod, tm):
    t_total = x2d.shape[0]
    tile = lambda c: pl.BlockSpec((tm, c), lambda i: (i, 0))
    return pl.pallas_call(
        _merge_ffn_kernel,
        grid=(t_total // tm,),
        in_specs=[
            tile(D_MODEL),
            pl.BlockSpec((1, 6, D_MODEL), lambda i: (i // tiles_per_mod, 0, 0)),
            tile(BRANCH_W), tile(BRANCH_W), tile(BRANCH_W), tile(BRANCH_W), tile(BRANCH_W),
            tile(N_BRANCH * D_MODEL),
            _full_spec((1, GLA_DV)),
            _layer_spec((N_BRANCH, BRANCH_W, D_MODEL), l),
            _layer_spec((D_MODEL, D_MODEL), l),
            _full_spec((1, D_MODEL)),
            _layer_spec((D_MODEL, 2 * FFN_HIDDEN), l),
            _layer_spec((FFN_HIDDEN, D_MODEL), l),
        ],
        out_specs=tile(D_MODEL),
        out_shape=jax.ShapeDtypeStruct((t_total, D_MODEL), F32),
        compiler_params=pltpu.CompilerParams(
            dimension_semantics=("parallel",), vmem_limit_bytes=VMEM_LIMIT),
        name="merge_ffn",
    )(x2d, mod, og, o_f, o_b, lgo, od, gates, lw["gout"], sw["w_branch"], sw["w_out"],
      lw["norm2"], sw["w_ffn_in"], sw["w_ffn_out"])


def _shared_weights(p):
    return {
        "w_in_t": jnp.swapaxes(p["w_in"], 1, 2).astype(BF16),
        "w_branch": p["w_branch"].astype(BF16), "w_out": p["w_out"].astype(BF16),
        "w_ffn_in": p["w_ffn_in"].astype(BF16), "w_ffn_out": p["w_ffn_out"].astype(BF16),
    }


def _layer_weights(l, p):
    scale = HEAD_DIM ** -0.5 * math.log2(math.e)
    head_bound = lambda g, s: HEAD_DIM * (NORM_MARGIN * s) ** 2 * jnp.max(g * g)
    norm_bounds = lambda gq_, gk_: jnp.stack([head_bound(gq_, scale), head_bound(gk_, 1.0)])
    gq_row = jnp.concatenate([jnp.tile(p["gqa_q_norm"][l], GQA_HEADS),
                              jnp.tile(p["diff_q_norm"][l], 2 * DIFF_HEADS)]) * scale
    gk_row = jnp.concatenate([jnp.tile(p["gqa_k_norm"][l], GQA_KV_HEADS),
                              jnp.tile(p["diff_k_norm"][l], 2 * DIFF_HEADS)])
    idx = jnp.arange(2 * LANES) // HEAD_DIM
    gmat = jnp.where(idx[:, None] == idx[None, :], 1.0 / HEAD_DIM, 0.0).astype(BF16)
    up = jnp.zeros((LANES, 2 * GLA_W), F32)
    up = up.at[0:GLA_RANK, 0:GLA_W].set(p["gla_alpha_up"][l, 0])
    up = up.at[GLA_RANK:2 * GLA_RANK, GLA_W:].set(p["gla_alpha_up"][l, 1])
    ub = p["gla_alpha_bias"][l].reshape(1, 2 * GLA_W)
    up_hi = up.astype(BF16)
    up = jnp.stack([up_hi, (up - up_hi.astype(F32)).astype(BF16)])
    return {
        "gq": gq_row.reshape(1, 2 * N_QG), "gk": gk_row.reshape(1, LANES + BRANCH_W), "gmat": gmat,
        "up": up, "ub": ub,
        "kb_g": norm_bounds(p["gqa_q_norm"][l], p["gqa_k_norm"][l]),
        "kb_d": norm_bounds(p["diff_q_norm"][l], p["diff_k_norm"][l]),
        "norm1": p["norm1"][l].reshape(1, D_MODEL), "norm2": p["norm2"][l].reshape(1, D_MODEL),
        "gout": p["gla_out_norm"][l].reshape(1, GLA_DV),
        "gsub": p["diff_sub_norm"][l].reshape(1, DIFF_DV),
        "lam": p["diff_lambda"][l],
    }


def _rope_tables(n_tokens):
    n_rows = n_tokens // GRID_W
    row = jnp.repeat(jnp.arange(n_rows, dtype=F32), GRID_W)
    col = jnp.tile(jnp.arange(GRID_W, dtype=F32), n_rows)
    n_freq = HEAD_DIM // 4
    freqs = ROPE_THETA ** (-jnp.arange(n_freq, dtype=F32) / n_freq)
    ar, ac = row[:, None] * freqs, col[:, None] * freqs
    cos = jnp.concatenate([jnp.cos(ar), jnp.cos(ar), jnp.cos(ac), jnp.cos(ac)], axis=1)
    sin = jnp.concatenate([-jnp.sin(ar), jnp.sin(ar), -jnp.sin(ac), jnp.sin(ac)], axis=1)
    return jnp.tile(cos, (1, LANES // HEAD_DIM)), jnp.tile(sin, (1, LANES // HEAD_DIM))


def _with_ones_rows(v_t):
    lead = v_t.shape[:-2]
    s = v_t.shape[-1]
    ones = jnp.ones(lead + (1, s), v_t.dtype)
    zeros = jnp.zeros(lead + (BF16_ROWS - 1, s), v_t.dtype)
    return jnp.concatenate([v_t, ones, zeros], axis=-2).astype(BF16)


def _cache_layouts(l, cache_gqa_k, cache_gqa_v, cache_diff_k, cache_diff_v):
    b = cache_gqa_k.shape[0]
    gk = jnp.transpose(cache_gqa_k[:, l], (0, 2, 1, 3))
    kd_g = jnp.concatenate([gk, gk], axis=-1).astype(BF16)
    vt_g = _with_ones_rows(jnp.transpose(cache_gqa_v[:, l], (0, 2, 3, 1)))
    dk = jnp.transpose(cache_diff_k[:, l], (0, 2, 3, 1, 4)).reshape(b, 2 * DIFF_HEADS, PAST_LEN, HEAD_DIM)
    kd_d = jnp.concatenate([dk, dk], axis=-1).astype(BF16)
    vt_d = _with_ones_rows(jnp.transpose(cache_diff_v[:, l], (0, 2, 3, 1)))
    return kd_g, vt_g, kd_d, vt_d


def _run_layer(x2d, mod, lw, sw, l, n_batch, n_tok, rope_tabs, ctx, emit_cache, cfg):
    tiles_per_mod = lambda tm: (n_tok // tm) if mod.shape[0] > 1 else (n_batch * n_tok // tm)
    if ctx is None:
        cache_kv = kd_gc = kd_dc = s0 = None
    else:
        kd_gc, vt_gc, kd_dc, vt_dc, s0 = ctx
        cache_kv = (kd_gc, kd_dc, vt_gc, vt_dc)
    outs = _in_projection(x2d, mod, lw, sw, l, n_batch, n_tok, cache_kv, rope_tabs, emit_cache, cfg["tm_in"])
    qg, qd, kdg, kdd, vtg, vtd, lq, lk, lv, lgo, la, gates = outs[:12]
    og = _gqa_attention(qg, lw["kb_g"], kdg, vtg, kd_gc, n_batch, n_tok, cfg["tq_gqa"], cfg["tk"],
                        cfg["hp_gqa"])
    lam_init = 0.8 - 0.6 * math.exp(-0.3 * l)
    od = _diff_attention(qd, lw["kb_d"], lw["lam"], lw["gsub"], kdd, vtd, kd_dc, n_batch, n_tok,
                         cfg["tq_diff"], cfg["tk"], cfg["hp_diff"], lam_init)
    o_f, o_b, s_fin = _gla(lq, lk, lv, la, s0, n_batch, n_tok, cfg["tb_gla"])
    x2d = _merge_ffn(x2d, mod, og, o_f, o_b, lgo, od, gates, lw, sw, l, tiles_per_mod(cfg["tm"]), cfg["tm"])
    cache = tuple(outs[12:]) + (s_fin,) if emit_cache else None
    return x2d, cache


PROMPT_CFG = dict(tm_in=256, tq_gqa=256, tq_diff=256, tk=256, hp_gqa=2, hp_diff=4, tb_gla=256, tm=512)
SAMPLE_CFG = dict(tm_in=512, tq_gqa=1024, tq_diff=2048, tk=1536, hp_gqa=1, hp_diff=1, tb_gla=256, tm=512)


def kernel(x_prompt, x_sample, c, cache_gqa_k, cache_gqa_v, state_gla, cache_diff_k, cache_diff_v, c_ctx, w_mod, b_mod, norm1, norm2, w_in, gqa_q_norm, gqa_k_norm, gla_alpha_up, gla_alpha_bias, gla_out_norm, diff_q_norm, diff_k_norm, diff_lambda, diff_sub_norm, w_branch, w_out, w_ffn_in, w_ffn_out):
    p = {
        "norm1": norm1, "norm2": norm2, "w_in": w_in, "gqa_q_norm": gqa_q_norm, "gqa_k_norm": gqa_k_norm,
        "gla_alpha_up": gla_alpha_up, "gla_alpha_bias": gla_alpha_bias, "gla_out_norm": gla_out_norm,
        "diff_q_norm": diff_q_norm, "diff_k_norm": diff_k_norm, "diff_lambda": diff_lambda,
        "diff_sub_norm": diff_sub_norm, "w_branch": w_branch, "w_out": w_out,
        "w_ffn_in": w_ffn_in, "w_ffn_out": w_ffn_out,
    }
    n_ctx_b, n_ctx = x_prompt.shape[:2]
    n_lat_b, n_lat = x_sample.shape[:2]
    cond_rows = jnp.concatenate(
        [c_ctx[None, :], c, jnp.zeros((8 - 1 - n_lat_b, D_MODEL), F32)], axis=0)
    mod_all = _modulation(cond_rows, w_mod, b_mod)
    weights = [_layer_weights(l, p) for l in range(DEPTH)]
    shared = _shared_weights(p)

    y = x_prompt.reshape(n_ctx_b * n_ctx, D_MODEL)
    caches = []
    for l in range(DEPTH):
        mod = mod_all[l, 0:1].reshape(1, 6, D_MODEL)
        y, cache = _run_layer(y, mod, weights[l], shared, l, n_ctx_b, n_ctx, None, None, True, PROMPT_CFG)
        caches.append(cache)
    y_prompt = y.reshape(n_ctx_b, n_ctx, D_MODEL)
    stack = lambda j, shape: jnp.stack([cc[j].reshape(shape) for cc in caches], axis=1)
    new_gqa_k = stack(0, (n_ctx_b, n_ctx, GQA_KV_HEADS, HEAD_DIM))
    new_diff_k = stack(1, (n_ctx_b, n_ctx, DIFF_HEADS, 2, HEAD_DIM))
    new_gqa_v = stack(2, (n_ctx_b, n_ctx, GQA_KV_HEADS, HEAD_DIM))
    new_diff_v = stack(3, (n_ctx_b, n_ctx, DIFF_HEADS, DIFF_DV))
    new_state_gla = jnp.stack([cc[4] for cc in caches], axis=1)

    rope_tabs = _rope_tables(n_lat)
    y = x_sample.reshape(n_lat_b * n_lat, D_MODEL)
    for l in range(DEPTH):
        mod = mod_all[l, 1:1 + n_lat_b].reshape(n_lat_b, 6, D_MODEL)
        ctx = _cache_layouts(l, cache_gqa_k, cache_gqa_v, cache_diff_k, cache_diff_v) + (state_gla[:, l],)
        y, _ = _run_layer(y, mod, weights[l], shared, l, n_lat_b, n_lat, rope_tabs, ctx, False, SAMPLE_CFG)
    y_sample = y.reshape(n_lat_b, n_lat, D_MODEL)
    return (y_prompt, y_sample, new_gqa_k, new_gqa_v, new_state_gla, new_diff_k, new_diff_v)
```

```python
import functools
import math

import jax
import jax.numpy as jnp
from jax import lax
from jax.experimental import pallas as pl
from jax.experimental.pallas import tpu as pltpu

D_MODEL = 1024
DEPTH = 2
PAST_LEN = 512
GRID_W = 64
HEAD_DIM = 64
GQA_HEADS = 8
GQA_KV_HEADS = 2
GQA_GROUP = GQA_HEADS // GQA_KV_HEADS
GLA_HEADS = 4
GLA_DK = 64
GLA_DV = 128
GLA_RANK = 16
GLA_TAU = 16.0
GLA_CHUNK = 64
DIFF_HEADS = 4
DIFF_DV = 2 * HEAD_DIM
N_BRANCH = 3
BRANCH_W = 512
FFN_HIDDEN = ((8 * D_MODEL + 3 * 256 - 1) // (3 * 256)) * 256
ROPE_THETA = 10000.0
EPS = 1e-6

LANES = 128
BF16_ROWS = 16
VMEM_LIMIT = 56 * 1024 * 1024

F32 = jnp.float32
BF16 = jnp.bfloat16
HIGHEST = lax.Precision.HIGHEST

R_GQ = 0
R_KV = R_GQ + GQA_HEADS * HEAD_DIM
R_GLA = R_KV + 2 * GQA_KV_HEADS * HEAD_DIM
R_LR = R_GLA + 2 * GLA_HEADS * GLA_DK + GLA_HEADS * GLA_DV
R_LGO = R_LR + 2 * GLA_RANK
R_DQ = R_LGO + GLA_HEADS * GLA_DV
R_DK = R_DQ + DIFF_HEADS * 2 * HEAD_DIM
R_DV = R_DK + DIFF_HEADS * 2 * HEAD_DIM
R_MG = R_DV + DIFF_HEADS * DIFF_DV
N_IN = R_MG + N_BRANCH * D_MODEL
NT_DIMS = (((1,), (1,)), ((), ()))
N_QG = GQA_HEADS * HEAD_DIM
GLA_W = GLA_HEADS * GLA_DK
GQA_VT_ROWS = HEAD_DIM + BF16_ROWS
DIFF_VT_ROWS = DIFF_DV + BF16_ROWS


def _sigmoid(x):
    return 1.0 / (1.0 + jnp.exp(-x))


def _silu(x):
    return x * _sigmoid(x)


def _bf16_terms(x):
    hi = x.astype(BF16)
    r1 = x - hi.astype(F32)
    mid = r1.astype(BF16)
    lo = (r1 - mid.astype(F32)).astype(BF16)
    return hi, mid, lo


def _full_spec(shape):
    n = len(shape)
    return pl.BlockSpec(shape, lambda *_: (0,) * n, pipeline_mode=pl.Buffered(1))


def _layer_spec(shape, l):
    n = len(shape)
    return pl.BlockSpec((None,) + tuple(shape), lambda *_: (l,) + (0,) * n, pipeline_mode=pl.Buffered(1))


def _mod_kernel(cond_ref, w_ref, b_ref, o_ref):
    s = _silu(cond_ref[...])
    o_ref[0] = jnp.dot(s, w_ref[0], precision=HIGHEST, preferred_element_type=F32) + b_ref[0]


def _modulation(cond_rows, w_mod, b_mod):
    tn = 1536
    n_out = 6 * D_MODEL
    return pl.pallas_call(
        _mod_kernel,
        grid=(DEPTH, n_out // tn),
        in_specs=[
            pl.BlockSpec((8, D_MODEL), lambda l, j: (0, 0)),
            pl.BlockSpec((1, D_MODEL, tn), lambda l, j: (l, 0, j)),
            pl.BlockSpec((1, 1, tn), lambda l, j: (l, 0, j)),
        ],
        out_specs=pl.BlockSpec((1, 8, tn), lambda l, j: (l, 0, j)),
        out_shape=jax.ShapeDtypeStruct((DEPTH, 8, n_out), F32),
        compiler_params=pltpu.CompilerParams(
            dimension_semantics=("parallel", "parallel"), vmem_limit_bytes=VMEM_LIMIT),
        name="modulation",
    )(cond_rows, w_mod, b_mod.reshape(DEPTH, 1, n_out))


def _head_norm(z, gain, gmat):
    outs = []
    n = z.shape[1]
    for c0 in range(0, n, 2 * LANES):
        w = min(2 * LANES, n - c0)
        zz = z[:, c0:c0 + w]
        ms = jnp.dot((zz * zz).astype(BF16), gmat[0:w, 0:w], preferred_element_type=F32)
        outs.append(zz * lax.rsqrt(ms + EPS) * gain[:, c0:c0 + w])
    return outs


def _rope_block(zb, cos, sin, first):
    partner = jnp.where(first, pltpu.roll(zb, LANES - 16, 1), pltpu.roll(zb, 16, 1))
    return zb * cos + partner * sin


N_IN_BASE = 9


def _in_kernel(*refs, rope, emit_cache, tiles_per_batch, cache_tiles):
    if not cache_tiles:
        _in_tile(refs, rope, emit_cache)
        return
    kcg_ref, kcd_ref, vcg_ref, vcd_ref = refs[:4]
    j = pl.program_id(1)

    @pl.when(j < tiles_per_batch)
    def _():
        _in_tile(refs[4:], rope, emit_cache)

    @pl.when(j >= tiles_per_batch)
    def _():
        n_in = 4 + N_IN_BASE + (2 if rope else 0)
        kdg_ref, kdd_ref, vtg_ref, vtd_ref = refs[n_in + 2:n_in + 6]
        kdg_ref[...] = kcg_ref[...]
        kdd_ref[...] = kcd_ref[...]
        vtg_ref[...] = vcg_ref[...]
        vtd_ref[...] = vcd_ref[...]


def _in_tile(refs, rope, emit_cache):
    it = iter(refs)
    (x_ref, mod_ref, g1_ref, wt_ref, gq_ref, gk_ref, gmat_ref, up_ref, ub_ref) = (
        next(it) for _ in range(N_IN_BASE))
    if rope:
        cos_ref, sin_ref = next(it), next(it)
    (qg_ref, qd_ref, kdg_ref, kdd_ref, vtg_ref, vtd_ref,
     lq_ref, lk_ref, lv_ref, lgo_ref, la_ref, gate_ref) = (next(it) for _ in range(12))
    if emit_cache:
        ck_g_ref, ck_d_ref, cv_g_ref, cv_d_ref = (next(it) for _ in range(4))

    tm = x_ref.shape[0]
    x = x_ref[...]
    ms = jnp.mean(x * x, axis=-1, keepdims=True)
    h = x * lax.rsqrt(ms + EPS) * g1_ref[...]
    h = h * (1.0 + mod_ref[0, 1:2, :]) + mod_ref[0, 0:1, :]
    hb = h.astype(BF16)

    def proj(r0, n):
        return lax.dot_general(hb, wt_ref[r0:r0 + n, :], NT_DIMS, preferred_element_type=F32)

    lane = lax.broadcasted_iota(jnp.int32, (tm, LANES), 1)
    first16 = (lane % 32) < 16
    lo_half = lane < HEAD_DIM
    gmat = gmat_ref[...]
    if rope:
        cos, sin = cos_ref[...], sin_ref[...]

    def blocks128(parts):
        out = []
        for p in parts:
            for c in range(0, p.shape[1], LANES):
                out.append(p[:, c:c + LANES])
        return out

    def store_queries(z, gain, dst):
        for j, zb in enumerate(blocks128(_head_norm(z, gain, gmat))):
            if rope:
                zb = _rope_block(zb, cos, sin, first16)
            dst[:, j * LANES:(j + 1) * LANES] = zb.astype(BF16)

    def store_keys(z, gain, dst, cache_dst):
        for j, zb in enumerate(blocks128(_head_norm(z, gain, gmat))):
            if rope:
                zb = _rope_block(zb, cos, sin, first16)
            if emit_cache:
                cache_dst[:, j * LANES:(j + 1) * LANES] = zb
            sw = pltpu.roll(zb, HEAD_DIM, 1)
            dst[0, 2 * j] = jnp.where(lo_half, zb, sw).astype(BF16)
            dst[0, 2 * j + 1] = jnp.where(lo_half, sw, zb).astype(BF16)

    z_gq = proj(R_GQ, N_QG)
    z_dq = proj(R_DQ, N_QG)
    store_queries(z_gq, gq_ref[:, 0:N_QG], qg_ref)
    z_kv = proj(R_KV, 2 * LANES)
    store_queries(z_dq, gq_ref[:, N_QG:2 * N_QG], qd_ref)
    z_dk = proj(R_DK, BRANCH_W)
    store_keys(z_kv[:, 0:LANES], gk_ref[:, 0:LANES], kdg_ref, ck_g_ref if emit_cache else None)
    z_dv = proj(R_DV, BRANCH_W)
    store_keys(z_dk, gk_ref[:, LANES:LANES + BRANCH_W], kdd_ref, ck_d_ref if emit_cache else None)
    zl = proj(R_GLA, R_LR - R_GLA)

    ones_rows = jnp.where(lax.broadcasted_iota(jnp.int32, (BF16_ROWS, tm), 0) == 0, 1.0, 0.0).astype(BF16)
    z_gv = z_kv[:, LANES:2 * LANES]
    if emit_cache:
        cv_g_ref[...] = z_gv
        cv_d_ref[...] = z_dv
    vt = z_gv.T
    for hh in range(GQA_KV_HEADS):
        vtg_ref[0, hh, 0:HEAD_DIM, :] = vt[hh * HEAD_DIM:(hh + 1) * HEAD_DIM, :].astype(BF16)
        vtg_ref[0, hh, HEAD_DIM:GQA_VT_ROWS, :] = ones_rows
    for hh in range(DIFF_HEADS):
        vt = z_dv[:, hh * LANES:(hh + 1) * LANES].T
        vtd_ref[0, hh, 0:DIFF_DV, :] = vt.astype(BF16)
        vtd_ref[0, hh, DIFF_DV:DIFF_VT_ROWS, :] = ones_rows

    z_lgo = proj(R_LGO, BRANCH_W)
    lr = proj(R_LR, LANES)
    o = 0
    lq_ref[...] = (zl[:, o:o + GLA_W] * (GLA_DK ** -0.5)).astype(BF16)
    o += GLA_W
    lk_ref[...] = zl[:, o:o + GLA_W].astype(BF16)
    o += GLA_W
    lv_ref[...] = zl[:, o:o + BRANCH_W].astype(BF16)
    o += BRANCH_W
    lgo_ref[...] = z_lgo.astype(BF16)
    lr_hi, lr_mid, _ = _bf16_terms(lr)
    z = (jnp.dot(lr_hi, up_ref[0], preferred_element_type=F32)
         + jnp.dot(lr_mid, up_ref[0], preferred_element_type=F32)
         + jnp.dot(lr_hi, up_ref[1], preferred_element_type=F32)) + ub_ref[...]
    log_sig = jnp.minimum(z, 0.0) - jnp.log(1.0 + jnp.exp(-jnp.abs(z)))
    la_ref[...] = log_sig * (1.0 / GLA_TAU)

    for b in range(N_BRANCH):
        gate_ref[:, b * D_MODEL:(b + 1) * D_MODEL] = _sigmoid(
            proj(R_MG + b * D_MODEL, D_MODEL)).astype(BF16)


def _in_projection(x2d, mod, lw, sw, l, n_batch, n_tok, cache_kv, rope_tabs, emit_cache, tm):
    n_cache = 0 if cache_kv is None else cache_kv[0].shape[2]
    assert n_cache % tm == 0
    n_keys = n_tok + n_cache
    t_total = n_batch * n_tok
    tpb = n_tok // tm
    cache_tiles = n_cache // tm
    rope = rope_tabs is not None
    own = lambda j: jnp.minimum(j, tpb - 1)
    tile = lambda c: pl.BlockSpec((tm, c), lambda b, j: (b * tpb + own(j), 0))
    in_specs, args = [], []
    if cache_tiles:
        past = lambda j: jnp.maximum(j - tpb, 0)
        in_specs += [
            pl.BlockSpec((1, GQA_KV_HEADS, tm, LANES), lambda b, j: (b, 0, past(j), 0)),
            pl.BlockSpec((1, 2 * DIFF_HEADS, tm, LANES), lambda b, j: (b, 0, past(j), 0)),
            pl.BlockSpec((1, GQA_KV_HEADS, GQA_VT_ROWS, tm), lambda b, j: (b, 0, 0, past(j))),
            pl.BlockSpec((1, DIFF_HEADS, DIFF_VT_ROWS, tm), lambda b, j: (b, 0, 0, past(j))),
        ]
        args += list(cache_kv)
    in_specs += [
        tile(D_MODEL),
        pl.BlockSpec((1, 6, D_MODEL), lambda b, j: (b if mod.shape[0] > 1 else 0, 0, 0)),
        _full_spec((1, D_MODEL)),
        _layer_spec((N_IN, D_MODEL), l),
        _full_spec((1, 2 * N_QG)),
        _full_spec((1, LANES + BRANCH_W)),
        _full_spec((2 * LANES, 2 * LANES)),
        _full_spec((2, LANES, 2 * GLA_W)),
        _full_spec((1, 2 * GLA_W)),
    ]
    args += [x2d, mod, lw["norm1"], sw["w_in_t"], lw["gq"], lw["gk"], lw["gmat"], lw["up"], lw["ub"]]
    if rope:
        in_specs += [pl.BlockSpec((tm, LANES), lambda b, j: (own(j), 0))] * 2
        args += list(rope_tabs)
    sd = jax.ShapeDtypeStruct
    out_shape = [
        sd((t_total, BRANCH_W), BF16), sd((t_total, BRANCH_W), BF16),
        sd((n_batch, GQA_KV_HEADS, n_keys, LANES), BF16), sd((n_batch, 2 * DIFF_HEADS, n_keys, LANES), BF16),
        sd((n_batch, GQA_KV_HEADS, GQA_VT_ROWS, n_keys), BF16), sd((n_batch, DIFF_HEADS, DIFF_VT_ROWS, n_keys), BF16),
        sd((t_total, GLA_W), BF16), sd((t_total, GLA_W), BF16),
        sd((t_total, BRANCH_W), BF16), sd((t_total, BRANCH_W), BF16),
        sd((t_total, 2 * GLA_W), F32), sd((t_total, N_BRANCH * D_MODEL), BF16),
    ]
    out_specs = [
        tile(BRANCH_W), tile(BRANCH_W),
        pl.BlockSpec((1, GQA_KV_HEADS, tm, LANES), lambda b, j: (b, 0, j, 0)),
        pl.BlockSpec((1, 2 * DIFF_HEADS, tm, LANES), lambda b, j: (b, 0, j, 0)),
        pl.BlockSpec((1, GQA_KV_HEADS, GQA_VT_ROWS, tm), lambda b, j: (b, 0, 0, j)),
        pl.BlockSpec((1, DIFF_HEADS, DIFF_VT_ROWS, tm), lambda b, j: (b, 0, 0, j)),
        tile(GLA_W), tile(GLA_W), tile(BRANCH_W), tile(BRANCH_W), tile(2 * GLA_W), tile(N_BRANCH * D_MODEL),
    ]
    if emit_cache:
        out_shape += [sd((t_total, LANES), F32), sd((t_total, BRANCH_W), F32),
                      sd((t_total, LANES), F32), sd((t_total, BRANCH_W), F32)]
        out_specs += [tile(LANES), tile(BRANCH_W), tile(LANES), tile(BRANCH_W)]
    return pl.pallas_call(
        functools.partial(_in_kernel, rope=rope, emit_cache=emit_cache, tiles_per_batch=tpb,
                          cache_tiles=cache_tiles),
        grid=(n_batch, tpb + cache_tiles),
        in_specs=in_specs,
        out_specs=out_specs,
        out_shape=out_shape,
        compiler_params=pltpu.CompilerParams(
            dimension_semantics=("parallel", "arbitrary"), vmem_limit_bytes=VMEM_LIMIT),
        name="in_projection",
    )(*args)


MAX_COL = 4 * LANES
SCORE_BOUND = 45.0
NORM_MARGIN = 1.05


def _masked_queries_t(q_blk, keep_low):
    lane = lax.broadcasted_iota(jnp.int32, q_blk.shape, 1)
    keep = (lane < HEAD_DIM) if keep_low else (lane >= HEAD_DIM)
    return jnp.where(keep, q_blk.astype(F32), 0.0).T.astype(BF16)


def _dup_key_norm2(kd):
    kf = kd.astype(F32)
    return 0.5 * jnp.max(jnp.sum(kf * kf, axis=1, keepdims=True))


def _needs_no_stabiliser(q_norm2, k_norm2):
    return (q_norm2 * k_norm2 <= SCORE_BOUND * SCORE_BOUND).astype(jnp.int32)


KEY_SUB = 512


def _score_units(k_ref, heads_cols):
    tk = k_ref.shape[2]
    sub = min(KEY_SUB, tk)
    return sub, [(kh, vh, s0, c) for s0 in range(0, tk, sub) for kh, vh, cols in heads_cols for c in cols]


def _tile_scores(k_ref, qm_ref, unit, sub, col):
    kh, _, s0, c = unit
    return jnp.dot(k_ref[0, kh, s0:s0 + sub, :], qm_ref[:, c * col:(c + 1) * col], preferred_element_type=F32)


def _plain_tiles(k_ref, vt_ref, qm_ref, acc_ref, heads_cols):
    col = acc_ref.shape[-1]
    sub, units = _score_units(k_ref, heads_cols)
    s_next = _tile_scores(k_ref, qm_ref, units[0], sub, col)
    for i, (_, vh, s0, c) in enumerate(units):
        s = s_next
        if i + 1 < len(units):
            s_next = _tile_scores(k_ref, qm_ref, units[i + 1], sub, col)
        p = jnp.exp2(s).astype(BF16)
        acc_ref[c] += jnp.dot(vt_ref[0, vh, :, s0:s0 + sub], p, preferred_element_type=F32)


def _online_tiles(k_ref, vt_ref, qm_ref, m_ref, acc_ref, heads_cols):
    col = acc_ref.shape[-1]
    sub, units = _score_units(k_ref, heads_cols)
    for unit in units:
        _, vh, s0, c = unit
        s = _tile_scores(k_ref, qm_ref, unit, sub, col)
        m_prev = m_ref[c]
        m_new = jnp.maximum(m_prev, jnp.max(s, axis=0, keepdims=True))
        alpha = jnp.exp2(m_prev - m_new)
        p = jnp.exp2(s - m_new).astype(BF16)
        acc_ref[c] = acc_ref[c] * alpha + jnp.dot(vt_ref[0, vh, :, s0:s0 + sub], p, preferred_element_type=F32)
        m_ref[c] = m_new


def _flash_init(kb_ref, kc_ref, n_key_heads, m_ref, acc_ref, plain_ref):
    k_norm2 = kb_ref[1]
    if kc_ref is not None:
        for kh in range(n_key_heads):
            k_norm2 = jnp.maximum(k_norm2, _dup_key_norm2(kc_ref[0, kh]))
    plain_ref[0] = _needs_no_stabiliser(kb_ref[0], k_norm2)
    m_ref[...] = jnp.full(m_ref.shape, -jnp.inf, F32)
    acc_ref[...] = jnp.zeros(acc_ref.shape, F32)


def _flash_step(k_ref, vt_ref, qm_ref, m_ref, acc_ref, plain_ref, heads_cols):
    @pl.when(plain_ref[0] == 1)
    def _():
        _plain_tiles(k_ref, vt_ref, qm_ref, acc_ref, heads_cols)

    @pl.when(plain_ref[0] != 1)
    def _():
        _online_tiles(k_ref, vt_ref, qm_ref, m_ref, acc_ref, heads_cols)


def _gqa_kernel(*refs, has_cache, tq, hp):
    if has_cache:
        kb_ref, q_ref, k_ref, vt_ref, kc_ref, o_ref, qm_ref, m_ref, acc_ref, plain_ref = refs
    else:
        kb_ref, q_ref, k_ref, vt_ref, o_ref, qm_ref, m_ref, acc_ref, plain_ref = refs
        kc_ref = None
    kt = pl.program_id(3)
    col = acc_ref.shape[-1]
    per_head = tq // col
    per_kv = GQA_GROUP * per_head
    heads_cols = [(hh, hh, range(hh * per_kv, (hh + 1) * per_kv)) for hh in range(hp)]

    @pl.when(kt == 0)
    def _():
        for hh in range(hp):
            for g in range(GQA_GROUP):
                c0 = hh * 2 * LANES + (g // 2) * LANES
                row0 = (hh * GQA_GROUP + g) * tq
                qm_ref[:, row0:row0 + tq] = _masked_queries_t(q_ref[0, :, c0:c0 + LANES], g % 2 == 0)
        _flash_init(kb_ref, kc_ref, hp, m_ref, acc_ref, plain_ref)

    _flash_step(k_ref, vt_ref, qm_ref, m_ref, acc_ref, plain_ref, heads_cols)

    @pl.when(kt == pl.num_programs(3) - 1)
    def _():
        for hh in range(hp):
            for part in range(per_head):
                heads = []
                for g in range(GQA_GROUP):
                    acc = acc_ref[hh * per_kv + g * per_head + part]
                    heads.append(acc[0:HEAD_DIM, :] / acc[HEAD_DIM:HEAD_DIM + 1, :])
                o_ref[0, part * col:(part + 1) * col, hh * 2 * LANES:(hh + 1) * 2 * LANES] = (
                    jnp.concatenate(heads, axis=0).T.astype(BF16))


def _kv_specs(key_heads_blk, val_heads_blk, vt_rows, tk, kd_c):
    specs = [pl.BlockSpec((1, key_heads_blk, tk, LANES), lambda b, h, qi, kt: (b, h, kt, 0)),
             pl.BlockSpec((1, val_heads_blk, vt_rows, tk), lambda b, h, qi, kt: (b, h, 0, kt))]
    if kd_c is not None:
        specs.append(pl.BlockSpec((1, key_heads_blk, kd_c.shape[2], LANES), lambda b, h, qi, kt: (b, h, 0, 0)))
    return specs


def _gqa_attention(q, k_bound, kd, vt, kd_c, n_batch, n_tok, tq, tk, hp):
    has_cache = kd_c is not None
    nk = kd.shape[2] // tk
    rows = hp * GQA_GROUP * tq
    col = min(MAX_COL, tq)
    q_spec = pl.BlockSpec((1, tq, hp * 2 * LANES), lambda b, h, qi, kt: (b, qi, h))
    in_specs = [pl.BlockSpec(memory_space=pltpu.SMEM), q_spec]
    in_specs += _kv_specs(hp, hp, GQA_VT_ROWS, tk, kd_c)
    args = [k_bound, q.reshape(n_batch, n_tok, BRANCH_W), kd, vt]
    if has_cache:
        args.append(kd_c)
    out = pl.pallas_call(
        functools.partial(_gqa_kernel, has_cache=has_cache, tq=tq, hp=hp),
        grid=(n_batch, GQA_KV_HEADS // hp, n_tok // tq, nk),
        in_specs=in_specs,
        out_specs=q_spec,
        out_shape=jax.ShapeDtypeStruct((n_batch, n_tok, BRANCH_W), BF16),
        scratch_shapes=[
            pltpu.VMEM((LANES, rows), BF16),
            pltpu.VMEM((rows // col, 1, col), F32),
            pltpu.VMEM((rows // col, GQA_VT_ROWS, col), F32),
            pltpu.SMEM((1,), jnp.int32),
        ],
        compiler_params=pltpu.CompilerParams(
            dimension_semantics=("parallel", "parallel", "parallel", "arbitrary"),
            vmem_limit_bytes=VMEM_LIMIT),
        name="gqa_attention",
    )(*args)
    return out.reshape(n_batch * n_tok, BRANCH_W)


def _diff_kernel(*refs, has_cache, lam_init, hp):
    if has_cache:
        (kb_ref, q_ref, lam_ref, gsub_ref, k_ref, vt_ref, kc_ref, o_ref,
         qm_ref, m_ref, acc_ref, plain_ref) = refs
    else:
        kb_ref, q_ref, lam_ref, gsub_ref, k_ref, vt_ref, o_ref, qm_ref, m_ref, acc_ref, plain_ref = refs
        kc_ref = None
    kt = pl.program_id(3)
    tq = q_ref.shape[1]
    col = acc_ref.shape[-1]
    per_map = tq // col
    heads_cols = [(2 * hh + mm, hh, range((2 * hh + mm) * per_map, (2 * hh + mm + 1) * per_map))
                  for hh in range(hp) for mm in range(2)]

    @pl.when(kt == 0)
    def _():
        for hh in range(hp):
            blk = q_ref[0, :, hh * LANES:(hh + 1) * LANES]
            for mm in range(2):
                row0 = (2 * hh + mm) * tq
                qm_ref[:, row0:row0 + tq] = _masked_queries_t(blk, mm == 0)
        _flash_init(kb_ref, kc_ref, 2 * hp, m_ref, acc_ref, plain_ref)

    _flash_step(k_ref, vt_ref, qm_ref, m_ref, acc_ref, plain_ref, heads_cols)

    @pl.when(kt == pl.num_programs(3) - 1)
    def _():
        lp = lam_ref[...]
        lam = (jnp.exp(jnp.sum(lp[0:1] * lp[1:2], axis=-1, keepdims=True))
               - jnp.exp(jnp.sum(lp[2:3] * lp[3:4], axis=-1, keepdims=True)) + lam_init)
        for hh in range(hp):
            for part in range(per_map):
                a0, a1 = acc_ref[2 * hh * per_map + part], acc_ref[(2 * hh + 1) * per_map + part]
                o0 = a0[0:DIFF_DV, :] / a0[DIFF_DV:DIFF_DV + 1, :]
                o1 = a1[0:DIFF_DV, :] / a1[DIFF_DV:DIFF_DV + 1, :]
                d = (o0 - lam * o1).T
                ms = jnp.mean(d * d, axis=-1, keepdims=True)
                o_ref[0, part * col:(part + 1) * col, hh * LANES:(hh + 1) * LANES] = (
                    d * lax.rsqrt(ms + EPS) * gsub_ref[...] * (1.0 - lam_init)).astype(BF16)


def _diff_attention(q, k_bound, lam_p, gsub, kd, vt, kd_c, n_batch, n_tok, tq, tk, hp, lam_init):
    has_cache = kd_c is not None
    nk = kd.shape[2] // tk
    col = min(MAX_COL, tq)
    rows = hp * 2 * tq
    q_spec = pl.BlockSpec((1, tq, hp * LANES), lambda b, h, qi, kt: (b, qi, h))
    in_specs = [
        pl.BlockSpec(memory_space=pltpu.SMEM),
        q_spec,
        pl.BlockSpec((4, HEAD_DIM), lambda b, h, qi, kt: (0, 0)),
        pl.BlockSpec((1, DIFF_DV), lambda b, h, qi, kt: (0, 0)),
    ]
    in_specs += _kv_specs(2 * hp, hp, DIFF_VT_ROWS, tk, kd_c)
    args = [k_bound, q.reshape(n_batch, n_tok, BRANCH_W), lam_p, gsub, kd, vt]
    if has_cache:
        args.append(kd_c)
    out = pl.pallas_call(
        functools.partial(_diff_kernel, has_cache=has_cache, lam_init=lam_init, hp=hp),
        grid=(n_batch, DIFF_HEADS // hp, n_tok // tq, nk),
        in_specs=in_specs,
        out_specs=q_spec,
        out_shape=jax.ShapeDtypeStruct((n_batch, n_tok, BRANCH_W), BF16),
        scratch_shapes=[
            pltpu.VMEM((LANES, rows), BF16),
            pltpu.VMEM((rows // col, 1, col), F32),
            pltpu.VMEM((rows // col, DIFF_VT_ROWS, col), F32),
            pltpu.SMEM((1,), jnp.int32),
        ],
        compiler_params=pltpu.CompilerParams(
            dimension_semantics=("parallel", "parallel", "parallel", "arbitrary"),
            vmem_limit_bytes=VMEM_LIMIT),
        name="diff_attention",
    )(*args)
    return out.reshape(n_batch * n_tok, BRANCH_W)


def _gla_kernel(*refs, has_s0, n_chunk, bg):
    if has_s0:
        qf, kf, vf, laf, qb, kb, vb, lab, s0_ref, of_ref, ob_ref, sfin_ref, st_ref = refs
    else:
        qf, kf, vf, laf, qb, kb, vb, lab, of_ref, ob_ref, sfin_ref, st_ref = refs
    i = pl.program_id(1)
    ck = GLA_CHUNK

    @pl.when(i == 0)
    def _():
        for bb in range(bg):
            for d in range(2):
                for hd in range(GLA_HEADS):
                    if has_s0:
                        s = s0_ref[bb, d, hd]
                        z = jnp.zeros_like(s)
                        padded = jnp.concatenate([s, z] if hd % 2 == 0 else [z, s], axis=0)
                        st_ref[bb, d, hd] = padded.T
                    else:
                        st_ref[bb, d, hd] = jnp.zeros((GLA_DV, LANES), F32)

    tb = qf.shape[1]
    r = lax.broadcasted_iota(jnp.int32, (tb, tb), 0)
    c = lax.broadcasted_iota(jnp.int32, (tb, tb), 1)
    lane = lax.broadcasted_iota(jnp.int32, (tb, LANES), 1)
    chunk_id = lax.broadcasted_iota(jnp.int32, (tb, GLA_W), 0) // ck
    zero_row = jnp.zeros((1, GLA_W), F32)
    streams = []
    for bb in range(bg):
        streams.append((bb, 0, qf.at[bb], kf.at[bb], vf.at[bb], laf.at[bb], of_ref.at[bb], c <= r))
        streams.append((bb, 1, qb.at[bb], kb.at[bb], vb.at[bb], lab.at[bb], ob_ref.at[bb], c >= r))
    nt = (((1,), (1,)), ((), ()))
    g_all = [sum(jnp.dot(tri.astype(F32).astype(BF16), part, preferred_element_type=F32)
                 for part in _bf16_terms(la_r[...]))
             for (_, _, _, _, _, la_r, _, tri) in streams]
    prepared = []
    for g, (bb, d, q_r, k_r, v_r, la_r, o_r, tri) in zip(g_all, streams):
        if d == 0:
            bounds = [zero_row] + [g[ck * j - 1:ck * j, :] for j in range(1, n_chunk)]
            g_end = g[tb - 1:tb, :]
        else:
            bounds = [g[ck * (j + 1):ck * (j + 1) + 1, :] for j in range(n_chunk - 1)] + [zero_row]
            g_end = g[0:1, :]
        b_rows = jnp.concatenate([jnp.broadcast_to(b, (ck, GLA_W)) for b in bounds], axis=0)
        q = q_r[...].astype(F32)
        k = k_r[...].astype(F32)
        q_dec = q * jnp.exp(g - b_rows)
        q_glob = q * jnp.exp(g)
        k_end = k * jnp.exp(g_end - g)
        k_rel = []
        for j in range(n_chunk):
            reach = (chunk_id <= j) if d == 0 else (chunk_id >= j)
            k_rel.append((k * jnp.exp(jnp.where(reach, bounds[j] - g, 0.0))).astype(BF16))
        prepared.append((q_dec, q_glob, k_end, k_rel, jnp.exp(g_end), v_r[...]))

    partial = []
    for (bb, d, *_), (q_dec, q_glob, k_end, k_rel, decay, v) in zip(streams, prepared):
        per_head = []
        for hd in range(GLA_HEADS):
            pair = slice((hd // 2) * LANES, (hd // 2 + 1) * LANES)
            hv = slice(hd * GLA_DV, (hd + 1) * GLA_DV)
            keep = (lane < GLA_DK) if hd % 2 == 0 else (lane >= GLA_DK)
            qd_m = jnp.where(keep, q_dec[:, pair], 0.0).astype(BF16)
            qg_m = jnp.where(keep, q_glob[:, pair], 0.0).astype(BF16)
            ke_m = jnp.where(keep, k_end[:, pair], 0.0).astype(BF16)
            a_rows = [lax.dot_general(qd_m[ck * j:ck * (j + 1), :], k_rel[j][:, pair], nt,
                                      preferred_element_type=F32) for j in range(n_chunk)]
            s_t = st_ref[bb, d, hd]
            o_state = lax.dot_general(qg_m, s_t.astype(BF16), nt, preferred_element_type=F32)
            ds_t = lax.dot_general(v[:, hv], ke_m, (((0,), (0,)), ((), ())), preferred_element_type=F32)
            st_ref[bb, d, hd] = s_t * decay[:, pair] + ds_t
            per_head.append((a_rows, o_state))
        partial.append(per_head)

    for (bb, d, _, _, _, _, o_r, tri), prep, per_head in zip(streams, prepared, partial):
        v = prep[5]
        for hd in range(GLA_HEADS):
            hv = slice(hd * GLA_DV, (hd + 1) * GLA_DV)
            a_rows, o_state = per_head[hd]
            a = jnp.where(tri, jnp.concatenate(a_rows, axis=0), 0.0).astype(BF16)
            o_r[:, hv] = (jnp.dot(a, v[:, hv], preferred_element_type=F32) + o_state).astype(BF16)

    @pl.when(i == pl.num_programs(1) - 1)
    def _():
        for bb in range(bg):
            for d in range(2):
                for hd in range(GLA_HEADS):
                    t = st_ref[bb, d, hd].T
                    sfin_ref[bb, d, hd] = t[(hd % 2) * GLA_DK:(hd % 2 + 1) * GLA_DK, :]


def _gla(lq, lk, lv, la, s0, n_batch, n_tok, tb, bg):
    nb = n_tok // tb
    has_s0 = s0 is not None
    as3d = lambda x: x.reshape(n_batch, n_tok, x.shape[-1])
    fwd = lambda b, i: (b, i, 0)
    bwd = lambda b, i: (b, nb - 1 - i, 0)
    bwd_la = lambda b, i: (b, nb - 1 - i, 1)
    in_specs = [
        pl.BlockSpec((bg, tb, GLA_W), fwd), pl.BlockSpec((bg, tb, GLA_W), fwd),
        pl.BlockSpec((bg, tb, BRANCH_W), fwd), pl.BlockSpec((bg, tb, GLA_W), fwd),
        pl.BlockSpec((bg, tb, GLA_W), bwd), pl.BlockSpec((bg, tb, GLA_W), bwd),
        pl.BlockSpec((bg, tb, BRANCH_W), bwd), pl.BlockSpec((bg, tb, GLA_W), bwd_la),
    ]
    args = [as3d(lq), as3d(lk), as3d(lv), as3d(la)] * 2
    state_spec = pl.BlockSpec((bg, 2, GLA_HEADS, GLA_DK, GLA_DV), lambda b, i: (b, 0, 0, 0, 0))
    if has_s0:
        in_specs.append(state_spec)
        args.append(s0)
    o_f, o_b, s_fin = pl.pallas_call(
        functools.partial(_gla_kernel, has_s0=has_s0, n_chunk=tb // GLA_CHUNK, bg=bg),
        grid=(n_batch // bg, nb),
        in_specs=in_specs,
        out_specs=[pl.BlockSpec((bg, tb, BRANCH_W), fwd), pl.BlockSpec((bg, tb, BRANCH_W), bwd), state_spec],
        out_shape=[
            jax.ShapeDtypeStruct((n_batch, n_tok, BRANCH_W), BF16),
            jax.ShapeDtypeStruct((n_batch, n_tok, BRANCH_W), BF16),
            jax.ShapeDtypeStruct((n_batch, 2, GLA_HEADS, GLA_DK, GLA_DV), F32),
        ],
        scratch_shapes=[pltpu.VMEM((bg, 2, GLA_HEADS, GLA_DV, LANES), F32)],
        compiler_params=pltpu.CompilerParams(
            dimension_semantics=("parallel", "arbitrary"), vmem_limit_bytes=VMEM_LIMIT),
        name="gla",
    )(*args)
    flat = lambda x: x.reshape(n_batch * n_tok, BRANCH_W)
    return flat(o_f), flat(o_b), s_fin


FFN_CHUNK = 256


def _merge_ffn_kernel(x_ref, mod_ref, og_ref, of_ref, ob_ref, lgo_ref, od_ref, gate_ref,
                      gout_ref, wb_ref, wo_ref, g2_ref, wi_ref, wd_ref, y_ref):
    o_gla = of_ref[...].astype(F32) + ob_ref[...].astype(F32)
    gla_parts = []
    for hd in range(GLA_HEADS):
        blk = o_gla[:, hd * GLA_DV:(hd + 1) * GLA_DV]
        ms = jnp.mean(blk * blk, axis=-1, keepdims=True)
        gla_parts.append(blk * lax.rsqrt(ms + EPS) * gout_ref[...])
    gla = jnp.concatenate(gla_parts, axis=1) * _silu(lgo_ref[...].astype(F32))
    branches = (og_ref[...], gla.astype(BF16), od_ref[...])
    mixed = None
    for b, ob in enumerate(branches):
        y = jnp.dot(ob, wb_ref[b], preferred_element_type=F32)
        y = y * gate_ref[:, b * D_MODEL:(b + 1) * D_MODEL].astype(F32)
        mixed = y if mixed is None else mixed + y
    out = jnp.dot(mixed.astype(BF16), wo_ref[...], preferred_element_type=F32)
    x = x_ref[...] + mod_ref[0, 2:3, :] * out

    ms = jnp.mean(x * x, axis=-1, keepdims=True)
    h = x * lax.rsqrt(ms + EPS) * g2_ref[...]
    h = h * (1.0 + mod_ref[0, 4:5, :]) + mod_ref[0, 3:4, :]
    hb = h.astype(BF16)

    def up_proj(c0):
        a = jnp.dot(hb, wi_ref[:, c0:c0 + FFN_CHUNK], preferred_element_type=F32)
        u = jnp.dot(hb, wi_ref[:, FFN_HIDDEN + c0:FFN_HIDDEN + c0 + FFN_CHUNK], preferred_element_type=F32)
        return a, u

    acc = None
    chunks = list(range(0, FFN_HIDDEN, FFN_CHUNK))
    nxt = up_proj(chunks[0])
    for i, c0 in enumerate(chunks):
        a, u = nxt
        if i + 1 < len(chunks):
            nxt = up_proj(chunks[i + 1])
        act = (_silu(a) * u).astype(BF16)
        part = jnp.dot(act, wd_ref[c0:c0 + FFN_CHUNK, :], preferred_element_type=F32)
        acc = part if acc is None else acc + part
    y_ref[...] = x + mod_ref[0, 5:6, :] * acc


def _merge_ffn(x2d, mod, og, o_f, o_b, lgo, od, gates, lw, sw, l, tiles_per_mod, tm):
    t_total = x2d.shape[0]
    tile = lambda c: pl.BlockSpec((tm, c), lambda i: (i, 0))
    return pl.pallas_call(
        _merge_ffn_kernel,
        grid=(t_total // tm,),
        in_specs=[
            tile(D_MODEL),
            pl.BlockSpec((1, 6, D_MODEL), lambda i: (i // tiles_per_mod, 0, 0)),
            tile(BRANCH_W), tile(BRANCH_W), tile(BRANCH_W), tile(BRANCH_W), tile(BRANCH_W),
            tile(N_BRANCH * D_MODEL),
            _full_spec((1, GLA_DV)),
            _layer_spec((N_BRANCH, BRANCH_W, D_MODEL), l),
            _layer_spec((D_MODEL, D_MODEL), l),
            _full_spec((1, D_MODEL)),
            _layer_spec((D_MODEL, 2 * FFN_HIDDEN), l),
            _layer_spec((FFN_HIDDEN, D_MODEL), l),
        ],
        out_specs=tile(D_MODEL),
        out_shape=jax.ShapeDtypeStruct((t_total, D_MODEL), F32),
        compiler_params=pltpu.CompilerParams(
            dimension_semantics=("parallel",), vmem_limit_bytes=VMEM_LIMIT),
        name="merge_ffn",
    )(x2d, mod, og, o_f, o_b, lgo, od, gates, lw["gout"], sw["w_branch"], sw["w_out"],
      lw["norm2"], sw["w_ffn_in"], sw["w_ffn_out"])


def _shared_weights(p):
    return {
        "w_in_t": jnp.swapaxes(p["w_in"], 1, 2).astype(BF16),
        "w_branch": p["w_branch"].astype(BF16), "w_out": p["w_out"].astype(BF16),
        "w_ffn_in": p["w_ffn_in"].astype(BF16), "w_ffn_out": p["w_ffn_out"].astype(BF16),
    }


def _layer_weights(l, p):
    scale = HEAD_DIM ** -0.5 * math.log2(math.e)
    head_bound = lambda g, s: HEAD_DIM * (NORM_MARGIN * s) ** 2 * jnp.max(g * g)
    norm_bounds = lambda gq_, gk_: jnp.stack([head_bound(gq_, scale), head_bound(gk_, 1.0)])
    gq_row = jnp.concatenate([jnp.tile(p["gqa_q_norm"][l], GQA_HEADS),
                              jnp.tile(p["diff_q_norm"][l], 2 * DIFF_HEADS)]) * scale
    gk_row = jnp.concatenate([jnp.tile(p["gqa_k_norm"][l], GQA_KV_HEADS),
                              jnp.tile(p["diff_k_norm"][l], 2 * DIFF_HEADS)])
    idx = jnp.arange(2 * LANES) // HEAD_DIM
    gmat = jnp.where(idx[:, None] == idx[None, :], 1.0 / HEAD_DIM, 0.0).astype(BF16)
    up = jnp.zeros((LANES, 2 * GLA_W), F32)
    up = up.at[0:GLA_RANK, 0:GLA_W].set(p["gla_alpha_up"][l, 0])
    up = up.at[GLA_RANK:2 * GLA_RANK, GLA_W:].set(p["gla_alpha_up"][l, 1])
    ub = p["gla_alpha_bias"][l].reshape(1, 2 * GLA_W)
    up_hi = up.astype(BF16)
    up = jnp.stack([up_hi, (up - up_hi.astype(F32)).astype(BF16)])
    return {
        "gq": gq_row.reshape(1, 2 * N_QG), "gk": gk_row.reshape(1, LANES + BRANCH_W), "gmat": gmat,
        "up": up, "ub": ub,
        "kb_g": norm_bounds(p["gqa_q_norm"][l], p["gqa_k_norm"][l]),
        "kb_d": norm_bounds(p["diff_q_norm"][l], p["diff_k_norm"][l]),
        "norm1": p["norm1"][l].reshape(1, D_MODEL), "norm2": p["norm2"][l].reshape(1, D_MODEL),
        "gout": p["gla_out_norm"][l].reshape(1, GLA_DV),
        "gsub": p["diff_sub_norm"][l].reshape(1, DIFF_DV),
        "lam": p["diff_lambda"][l],
    }


def _rope_tables(n_tokens):
    n_rows = n_tokens // GRID_W
    row = jnp.repeat(jnp.arange(n_rows, dtype=F32), GRID_W)
    col = jnp.tile(jnp.arange(GRID_W, dtype=F32), n_rows)
    n_freq = HEAD_DIM // 4
    freqs = ROPE_THETA ** (-jnp.arange(n_freq, dtype=F32) / n_freq)
    ar, ac = row[:, None] * freqs, col[:, None] * freqs
    cos = jnp.concatenate([jnp.cos(ar), jnp.cos(ar), jnp.cos(ac), jnp.cos(ac)], axis=1)
    sin = jnp.concatenate([-jnp.sin(ar), jnp.sin(ar), -jnp.sin(ac), jnp.sin(ac)], axis=1)
    return jnp.tile(cos, (1, LANES // HEAD_DIM)), jnp.tile(sin, (1, LANES // HEAD_DIM))


def _with_ones_rows(v_t):
    lead = v_t.shape[:-2]
    s = v_t.shape[-1]
    ones = jnp.ones(lead + (1, s), v_t.dtype)
    zeros = jnp.zeros(lead + (BF16_ROWS - 1, s), v_t.dtype)
    return jnp.concatenate([v_t, ones, zeros], axis=-2).astype(BF16)


def _cache_layouts(l, cache_gqa_k, cache_gqa_v, cache_diff_k, cache_diff_v):
    b = cache_gqa_k.shape[0]
    gk = jnp.transpose(cache_gqa_k[:, l], (0, 2, 1, 3))
    kd_g = jnp.concatenate([gk, gk], axis=-1).astype(BF16)
    vt_g = _with_ones_rows(jnp.transpose(cache_gqa_v[:, l], (0, 2, 3, 1)))
    dk = jnp.transpose(cache_diff_k[:, l], (0, 2, 3, 1, 4)).reshape(b, 2 * DIFF_HEADS, PAST_LEN, HEAD_DIM)
    kd_d = jnp.concatenate([dk, dk], axis=-1).astype(BF16)
    vt_d = _with_ones_rows(jnp.transpose(cache_diff_v[:, l], (0, 2, 3, 1)))
    return kd_g, vt_g, kd_d, vt_d


def _run_layer(x2d, mod, lw, sw, l, n_batch, n_tok, rope_tabs, ctx, emit_cache, cfg):
    tiles_per_mod = lambda tm: (n_tok // tm) if mod.shape[0] > 1 else (n_batch * n_tok // tm)
    if ctx is None:
        cache_kv = kd_gc = kd_dc = s0 = None
    else:
        kd_gc, vt_gc, kd_dc, vt_dc, s0 = ctx
        cache_kv = (kd_gc, kd_dc, vt_gc, vt_dc)
    outs = _in_projection(x2d, mod, lw, sw, l, n_batch, n_tok, cache_kv, rope_tabs, emit_cache, cfg["tm_in"])
    qg, qd, kdg, kdd, vtg, vtd, lq, lk, lv, lgo, la, gates = outs[:12]
    og = _gqa_attention(qg, lw["kb_g"], kdg, vtg, kd_gc, n_batch, n_tok, cfg["tq_gqa"], cfg["tk"],
                        cfg["hp_gqa"])
    lam_init = 0.8 - 0.6 * math.exp(-0.3 * l)
    od = _diff_attention(qd, lw["kb_d"], lw["lam"], lw["gsub"], kdd, vtd, kd_dc, n_batch, n_tok,
                         cfg["tq_diff"], cfg["tk"], cfg["hp_diff"], lam_init)
    o_f, o_b, s_fin = _gla(lq, lk, lv, la, s0, n_batch, n_tok, cfg["tb_gla"], cfg["bg_gla"])
    x2d = _merge_ffn(x2d, mod, og, o_f, o_b, lgo, od, gates, lw, sw, l, tiles_per_mod(cfg["tm"]), cfg["tm"])
    cache = tuple(outs[12:]) + (s_fin,) if emit_cache else None
    return x2d, cache


PROMPT_CFG = dict(tm_in=256, tq_gqa=256, tq_diff=256, tk=256, hp_gqa=2, hp_diff=4, tb_gla=256, bg_gla=2, tm=512)
SAMPLE_CFG = dict(tm_in=512, tq_gqa=1024, tq_diff=2048, tk=1536, hp_gqa=1, hp_diff=1, tb_gla=256, bg_gla=2, tm=512)


def kernel(x_prompt, x_sample, c, cache_gqa_k, cache_gqa_v, state_gla, cache_diff_k, cache_diff_v, c_ctx, w_mod, b_mod, norm1, norm2, w_in, gqa_q_norm, gqa_k_norm, gla_alpha_up, gla_alpha_bias, gla_out_norm, diff_q_norm, diff_k_norm, diff_lambda, diff_sub_norm, w_branch, w_out, w_ffn_in, w_ffn_out):
    p = {
        "norm1": norm1, "norm2": norm2, "w_in": w_in, "gqa_q_norm": gqa_q_norm, "gqa_k_norm": gqa_k_norm,
        "gla_alpha_up": gla_alpha_up, "gla_alpha_bias": gla_alpha_bias, "gla_out_norm": gla_out_norm,
        "diff_q_norm": diff_q_norm, "diff_k_norm": diff_k_norm, "diff_lambda": diff_lambda,
        "diff_sub_norm": diff_sub_norm, "w_branch": w_branch, "w_out": w_out,
        "w_ffn_in": w_ffn_in, "w_ffn_out": w_ffn_out,
    }
    n_ctx_b, n_ctx = x_prompt.shape[:2]
    n_lat_b, n_lat = x_sample.shape[:2]
    cond_rows = jnp.concatenate(
        [c_ctx[None, :], c, jnp.zeros((8 - 1 - n_lat_b, D_MODEL), F32)], axis=0)
    mod_all = _modulation(cond_rows, w_mod, b_mod)
    weights = [_layer_weights(l, p) for l in range(DEPTH)]
    shared = _shared_weights(p)

    y = x_prompt.reshape(n_ctx_b * n_ctx, D_MODEL)
    caches = []
    for l in range(DEPTH):
        mod = mod_all[l, 0:1].reshape(1, 6, D_MODEL)
        y, cache = _run_layer(y, mod, weights[l], shared, l, n_ctx_b, n_ctx, None, None, True, PROMPT_CFG)
        caches.append(cache)
    y_prompt = y.reshape(n_ctx_b, n_ctx, D_MODEL)
    stack = lambda j, shape: jnp.stack([cc[j].reshape(shape) for cc in caches], axis=1)
    new_gqa_k = stack(0, (n_ctx_b, n_ctx, GQA_KV_HEADS, HEAD_DIM))
    new_diff_k = stack(1, (n_ctx_b, n_ctx, DIFF_HEADS, 2, HEAD_DIM))
    new_gqa_v = stack(2, (n_ctx_b, n_ctx, GQA_KV_HEADS, HEAD_DIM))
    new_diff_v = stack(3, (n_ctx_b, n_ctx, DIFF_HEADS, DIFF_DV))
    new_state_gla = jnp.stack([cc[4] for cc in caches], axis=1)

    rope_tabs = _rope_tables(n_lat)
    y = x_sample.reshape(n_lat_b * n_lat, D_MODEL)
    for l in range(DEPTH):
        mod = mod_all[l, 1:1 + n_lat_b].reshape(n_lat_b, 6, D_MODEL)
        ctx = _cache_layouts(l, cache_gqa_k, cache_gqa_v, cache_diff_k, cache_diff_v) + (state_gla[:, l],)
        y, _ = _run_layer(y, mod, weights[l], shared, l, n_lat_b, n_lat, rope_tabs, ctx, False, SAMPLE_CFG)
    y_sample = y.reshape(n_lat_b, n_lat, D_MODEL)
    return (y_prompt, y_sample, new_gqa_k, new_gqa_v, new_state_gla, new_diff_k, new_diff_v)
```

```python
import functools
import math

import jax
import jax.numpy as jnp
from jax import lax
from jax.experimental import pallas as pl
from jax.experimental.pallas import tpu as pltpu

D_MODEL = 1024
DEPTH = 2
PAST_LEN = 512
GRID_W = 64
HEAD_DIM = 64
GQA_HEADS = 8
GQA_KV_HEADS = 2
GQA_GROUP = GQA_HEADS // GQA_KV_HEADS
GLA_HEADS = 4
GLA_DK = 64
GLA_DV = 128
GLA_RANK = 16
GLA_TAU = 16.0
GLA_CHUNK = 64
DIFF_HEADS = 4
DIFF_DV = 2 * HEAD_DIM
N_BRANCH = 3
BRANCH_W = 512
FFN_HIDDEN = ((8 * D_MODEL + 3 * 256 - 1) // (3 * 256)) * 256
ROPE_THETA = 10000.0
EPS = 1e-6

LANES = 128
BF16_ROWS = 16
VMEM_LIMIT = 56 * 1024 * 1024

F32 = jnp.float32
BF16 = jnp.bfloat16

R_GQ = 0
R_KV = R_GQ + GQA_HEADS * HEAD_DIM
R_GLA = R_KV + 2 * GQA_KV_HEADS * HEAD_DIM
R_LR = R_GLA + 2 * GLA_HEADS * GLA_DK + GLA_HEADS * GLA_DV
R_LGO = R_LR + 2 * GLA_RANK
R_DQ = R_LGO + GLA_HEADS * GLA_DV
R_DK = R_DQ + DIFF_HEADS * 2 * HEAD_DIM
R_DV = R_DK + DIFF_HEADS * 2 * HEAD_DIM
R_MG = R_DV + DIFF_HEADS * DIFF_DV
N_IN = R_MG + N_BRANCH * D_MODEL
NT_DIMS = (((1,), (1,)), ((), ()))
N_QG = GQA_HEADS * HEAD_DIM
GLA_W = GLA_HEADS * GLA_DK
GQA_VT_ROWS = HEAD_DIM + BF16_ROWS
DIFF_VT_ROWS = DIFF_DV + BF16_ROWS


def _sigmoid(x):
    return 1.0 / (1.0 + jnp.exp(-x))


def _silu(x):
    return x * _sigmoid(x)


def _bf16_terms(x):
    hi = x.astype(BF16)
    r1 = x - hi.astype(F32)
    mid = r1.astype(BF16)
    lo = (r1 - mid.astype(F32)).astype(BF16)
    return hi, mid, lo


def _full_spec(shape):
    n = len(shape)
    return pl.BlockSpec(shape, lambda *_: (0,) * n, pipeline_mode=pl.Buffered(1))


def _layer_spec(shape, l):
    n = len(shape)
    return pl.BlockSpec((None,) + tuple(shape), lambda *_: (l,) + (0,) * n, pipeline_mode=pl.Buffered(1))


def _mod_kernel(cond_ref, w_ref, b_ref, o_ref):
    s_hi, s_mid, _ = _bf16_terms(_silu(cond_ref[...]))
    lhs = jnp.concatenate([s_hi, s_mid], axis=0)
    w = w_ref[0]
    w_hi = w.astype(BF16)
    w_mid = (w - w_hi.astype(F32)).astype(BF16)
    r = jnp.dot(lhs, w_hi, preferred_element_type=F32) + jnp.dot(lhs, w_mid, preferred_element_type=F32)
    n = cond_ref.shape[0]
    o_ref[0] = r[0:n] + r[n:2 * n] + b_ref[0]


def _modulation(cond_rows, w_mod, b_mod):
    tn = 1536
    n_out = 6 * D_MODEL
    return pl.pallas_call(
        _mod_kernel,
        grid=(DEPTH, n_out // tn),
        in_specs=[
            pl.BlockSpec((8, D_MODEL), lambda l, j: (0, 0)),
            pl.BlockSpec((1, D_MODEL, tn), lambda l, j: (l, 0, j)),
            pl.BlockSpec((1, 1, tn), lambda l, j: (l, 0, j)),
        ],
        out_specs=pl.BlockSpec((1, 8, tn), lambda l, j: (l, 0, j)),
        out_shape=jax.ShapeDtypeStruct((DEPTH, 8, n_out), F32),
        compiler_params=pltpu.CompilerParams(
            dimension_semantics=("parallel", "parallel"), vmem_limit_bytes=VMEM_LIMIT),
        name="modulation",
    )(cond_rows, w_mod, b_mod.reshape(DEPTH, 1, n_out))


def _head_norm(z, gain, gmat):
    outs = []
    n = z.shape[1]
    for c0 in range(0, n, 2 * LANES):
        w = min(2 * LANES, n - c0)
        zz = z[:, c0:c0 + w]
        ms = jnp.dot((zz * zz).astype(BF16), gmat[0:w, 0:w], preferred_element_type=F32)
        outs.append(zz * lax.rsqrt(ms + EPS) * gain[:, c0:c0 + w])
    return outs


def _rope_block(zb, cos, sin, first):
    partner = jnp.where(first, pltpu.roll(zb, LANES - 16, 1), pltpu.roll(zb, 16, 1))
    return zb * cos + partner * sin


N_IN_BASE = 9


def _in_kernel(*refs, rope, emit_cache, tiles_per_batch, cache_tiles):
    if not cache_tiles:
        _in_tile(refs, rope, emit_cache)
        return
    kcg_ref, kcd_ref, vcg_ref, vcd_ref = refs[:4]
    j = pl.program_id(1)

    @pl.when(j < tiles_per_batch)
    def _():
        _in_tile(refs[4:], rope, emit_cache)

    @pl.when(j >= tiles_per_batch)
    def _():
        n_in = 4 + N_IN_BASE + (2 if rope else 0)
        kdg_ref, kdd_ref, vtg_ref, vtd_ref = refs[n_in + 2:n_in + 6]
        kdg_ref[...] = kcg_ref[...]
        kdd_ref[...] = kcd_ref[...]
        vtg_ref[...] = vcg_ref[...]
        vtd_ref[...] = vcd_ref[...]


def _in_tile(refs, rope, emit_cache):
    it = iter(refs)
    (x_ref, mod_ref, g1_ref, wt_ref, gq_ref, gk_ref, gmat_ref, up_ref, ub_ref) = (
        next(it) for _ in range(N_IN_BASE))
    if rope:
        cos_ref, sin_ref = next(it), next(it)
    (qg_ref, qd_ref, kdg_ref, kdd_ref, vtg_ref, vtd_ref,
     lq_ref, lk_ref, lv_ref, lgo_ref, la_ref, gate_ref) = (next(it) for _ in range(12))
    if emit_cache:
        ck_g_ref, ck_d_ref, cv_g_ref, cv_d_ref = (next(it) for _ in range(4))

    tm = x_ref.shape[0]
    bpt = kdg_ref.shape[0]
    nt = tm // bpt
    x = x_ref[...]
    ms = jnp.mean(x * x, axis=-1, keepdims=True)
    h = x * lax.rsqrt(ms + EPS) * g1_ref[...]
    h = h * (1.0 + mod_ref[0, 1:2, :]) + mod_ref[0, 0:1, :]
    hb = h.astype(BF16)

    def proj(r0, n):
        return lax.dot_general(hb, wt_ref[r0:r0 + n, :], NT_DIMS, preferred_element_type=F32)

    lane = lax.broadcasted_iota(jnp.int32, (tm, LANES), 1)
    first16 = (lane % 32) < 16
    lo_half = lane < HEAD_DIM
    gmat = gmat_ref[...]
    if rope:
        cos, sin = cos_ref[...], sin_ref[...]

    def blocks128(parts):
        out = []
        for p in parts:
            for c in range(0, p.shape[1], LANES):
                out.append(p[:, c:c + LANES])
        return out

    def store_queries(z, gain, dst):
        for j, zb in enumerate(blocks128(_head_norm(z, gain, gmat))):
            if rope:
                zb = _rope_block(zb, cos, sin, first16)
            dst[:, j * LANES:(j + 1) * LANES] = zb.astype(BF16)

    def store_keys(z, gain, dst, cache_dst):
        for j, zb in enumerate(blocks128(_head_norm(z, gain, gmat))):
            if rope:
                zb = _rope_block(zb, cos, sin, first16)
            if emit_cache:
                cache_dst[:, j * LANES:(j + 1) * LANES] = zb
            sw = pltpu.roll(zb, HEAD_DIM, 1)
            dup_lo = jnp.where(lo_half, zb, sw).astype(BF16)
            dup_hi = jnp.where(lo_half, sw, zb).astype(BF16)
            for bb in range(bpt):
                dst[bb, 2 * j] = dup_lo[bb * nt:(bb + 1) * nt, :]
                dst[bb, 2 * j + 1] = dup_hi[bb * nt:(bb + 1) * nt, :]

    z_gq = proj(R_GQ, N_QG)
    z_dq = proj(R_DQ, N_QG)
    store_queries(z_gq, gq_ref[:, 0:N_QG], qg_ref)
    z_kv = proj(R_KV, 2 * LANES)
    store_queries(z_dq, gq_ref[:, N_QG:2 * N_QG], qd_ref)
    z_dk = proj(R_DK, BRANCH_W)
    store_keys(z_kv[:, 0:LANES], gk_ref[:, 0:LANES], kdg_ref, ck_g_ref if emit_cache else None)
    z_dv = proj(R_DV, BRANCH_W)
    store_keys(z_dk, gk_ref[:, LANES:LANES + BRANCH_W], kdd_ref, ck_d_ref if emit_cache else None)
    zl = proj(R_GLA, R_LR - R_GLA)

    ones_rows = jnp.where(lax.broadcasted_iota(jnp.int32, (BF16_ROWS, nt), 0) == 0, 1.0, 0.0).astype(BF16)
    z_gv = z_kv[:, LANES:2 * LANES]
    if emit_cache:
        cv_g_ref[...] = z_gv
        cv_d_ref[...] = z_dv
    for bb in range(bpt):
        rows = slice(bb * nt, (bb + 1) * nt)
        vt = z_gv[rows, :].T
        for hh in range(GQA_KV_HEADS):
            vtg_ref[bb, hh, 0:HEAD_DIM, :] = vt[hh * HEAD_DIM:(hh + 1) * HEAD_DIM, :].astype(BF16)
            vtg_ref[bb, hh, HEAD_DIM:GQA_VT_ROWS, :] = ones_rows
        for hh in range(DIFF_HEADS):
            vt = z_dv[rows, hh * LANES:(hh + 1) * LANES].T
            vtd_ref[bb, hh, 0:DIFF_DV, :] = vt.astype(BF16)
            vtd_ref[bb, hh, DIFF_DV:DIFF_VT_ROWS, :] = ones_rows

    z_lgo = proj(R_LGO, BRANCH_W)
    lr = proj(R_LR, LANES)
    o = 0
    lq_ref[...] = (zl[:, o:o + GLA_W] * (GLA_DK ** -0.5)).astype(BF16)
    o += GLA_W
    lk_ref[...] = zl[:, o:o + GLA_W].astype(BF16)
    o += GLA_W
    lv_ref[...] = zl[:, o:o + BRANCH_W].astype(BF16)
    o += BRANCH_W
    lgo_ref[...] = z_lgo.astype(BF16)
    lr_hi, lr_mid, _ = _bf16_terms(lr)
    z = (jnp.dot(lr_hi, up_ref[0], preferred_element_type=F32)
         + jnp.dot(lr_mid, up_ref[0], preferred_element_type=F32)
         + jnp.dot(lr_hi, up_ref[1], preferred_element_type=F32)) + ub_ref[...]
    log_sig = jnp.minimum(z, 0.0) - jnp.log(1.0 + jnp.exp(-jnp.abs(z)))
    la_ref[...] = log_sig * (1.0 / GLA_TAU)

    for b in range(N_BRANCH):
        gate_ref[:, b * D_MODEL:(b + 1) * D_MODEL] = _sigmoid(
            proj(R_MG + b * D_MODEL, D_MODEL)).astype(BF16)


def _in_projection(x2d, mod, lw, sw, l, n_batch, n_tok, cache_kv, rope_tabs, emit_cache, tm):
    n_cache = 0 if cache_kv is None else cache_kv[0].shape[2]
    assert n_cache % tm == 0
    n_keys = n_tok + n_cache
    t_total = n_batch * n_tok
    bpt = max(1, tm // n_tok)
    assert bpt == 1 or (mod.shape[0] == 1 and n_cache == 0 and tm == bpt * n_tok)
    nt = tm // bpt
    tpb = n_tok // nt
    cache_tiles = n_cache // tm
    rope = rope_tabs is not None
    own = lambda j: jnp.minimum(j, tpb - 1)
    tile = lambda c: pl.BlockSpec((tm, c), lambda b, j: (b * tpb + own(j), 0))
    in_specs, args = [], []
    if cache_tiles:
        past = lambda j: jnp.maximum(j - tpb, 0)
        in_specs += [
            pl.BlockSpec((1, GQA_KV_HEADS, tm, LANES), lambda b, j: (b, 0, past(j), 0)),
            pl.BlockSpec((1, 2 * DIFF_HEADS, tm, LANES), lambda b, j: (b, 0, past(j), 0)),
            pl.BlockSpec((1, GQA_KV_HEADS, GQA_VT_ROWS, tm), lambda b, j: (b, 0, 0, past(j))),
            pl.BlockSpec((1, DIFF_HEADS, DIFF_VT_ROWS, tm), lambda b, j: (b, 0, 0, past(j))),
        ]
        args += list(cache_kv)
    in_specs += [
        tile(D_MODEL),
        pl.BlockSpec((1, 6, D_MODEL), lambda b, j: (b if mod.shape[0] > 1 else 0, 0, 0)),
        _full_spec((1, D_MODEL)),
        _layer_spec((N_IN, D_MODEL), l),
        _full_spec((1, 2 * N_QG)),
        _full_spec((1, LANES + BRANCH_W)),
        _full_spec((2 * LANES, 2 * LANES)),
        _full_spec((2, LANES, 2 * GLA_W)),
        _full_spec((1, 2 * GLA_W)),
    ]
    args += [x2d, mod, lw["norm1"], sw["w_in_t"], lw["gq"], lw["gk"], lw["gmat"], lw["up"], lw["ub"]]
    if rope:
        in_specs += [pl.BlockSpec((tm, LANES), lambda b, j: (own(j), 0))] * 2
        args += list(rope_tabs)
    sd = jax.ShapeDtypeStruct
    out_shape = [
        sd((t_total, BRANCH_W), BF16), sd((t_total, BRANCH_W), BF16),
        sd((n_batch, GQA_KV_HEADS, n_keys, LANES), BF16), sd((n_batch, 2 * DIFF_HEADS, n_keys, LANES), BF16),
        sd((n_batch, GQA_KV_HEADS, GQA_VT_ROWS, n_keys), BF16), sd((n_batch, DIFF_HEADS, DIFF_VT_ROWS, n_keys), BF16),
        sd((t_total, GLA_W), BF16), sd((t_total, GLA_W), BF16),
        sd((t_total, BRANCH_W), BF16), sd((t_total, BRANCH_W), BF16),
        sd((t_total, 2 * GLA_W), F32), sd((t_total, N_BRANCH * D_MODEL), BF16),
    ]
    out_specs = [
        tile(BRANCH_W), tile(BRANCH_W),
        pl.BlockSpec((bpt, GQA_KV_HEADS, nt, LANES), lambda b, j: (b, 0, j, 0)),
        pl.BlockSpec((bpt, 2 * DIFF_HEADS, nt, LANES), lambda b, j: (b, 0, j, 0)),
        pl.BlockSpec((bpt, GQA_KV_HEADS, GQA_VT_ROWS, nt), lambda b, j: (b, 0, 0, j)),
        pl.BlockSpec((bpt, DIFF_HEADS, DIFF_VT_ROWS, nt), lambda b, j: (b, 0, 0, j)),
        tile(GLA_W), tile(GLA_W), tile(BRANCH_W), tile(BRANCH_W), tile(2 * GLA_W), tile(N_BRANCH * D_MODEL),
    ]
    if emit_cache:
        out_shape += [sd((t_total, LANES), F32), sd((t_total, BRANCH_W), F32),
                      sd((t_total, LANES), F32), sd((t_total, BRANCH_W), F32)]
        out_specs += [tile(LANES), tile(BRANCH_W), tile(LANES), tile(BRANCH_W)]
    return pl.pallas_call(
        functools.partial(_in_kernel, rope=rope, emit_cache=emit_cache, tiles_per_batch=tpb,
                          cache_tiles=cache_tiles),
        grid=(n_batch // bpt, tpb + cache_tiles),
        in_specs=in_specs,
        out_specs=out_specs,
        out_shape=out_shape,
        compiler_params=pltpu.CompilerParams(
            dimension_semantics=("parallel", "arbitrary"), vmem_limit_bytes=VMEM_LIMIT),
        name="in_projection",
    )(*args)


MAX_COL = 4 * LANES
SCORE_BOUND = 45.0
NORM_MARGIN = 1.05


def _masked_queries_t(q_blk, keep_low):
    lane = lax.broadcasted_iota(jnp.int32, q_blk.shape, 1)
    keep = (lane < HEAD_DIM) if keep_low else (lane >= HEAD_DIM)
    return jnp.where(keep, q_blk.astype(F32), 0.0).T.astype(BF16)


def _dup_key_norm2(kd):
    kf = kd.astype(F32)
    return 0.5 * jnp.max(jnp.sum(kf * kf, axis=1, keepdims=True))


def _needs_no_stabiliser(q_norm2, k_norm2):
    return (q_norm2 * k_norm2 <= SCORE_BOUND * SCORE_BOUND).astype(jnp.int32)


KEY_SUB = 512


def _score_units(k_ref, heads_cols):
    tk = k_ref.shape[2]
    sub = min(KEY_SUB, tk)
    return sub, [(kh, vh, s0, c) for s0 in range(0, tk, sub) for kh, vh, cols in heads_cols for c in cols]


def _tile_scores(k_ref, qm_ref, unit, sub, col):
    kh, _, s0, c = unit
    return jnp.dot(k_ref[0, kh, s0:s0 + sub, :], qm_ref[:, c * col:(c + 1) * col], preferred_element_type=F32)


def _plain_tiles(k_ref, vt_ref, qm_ref, acc_ref, heads_cols):
    col = acc_ref.shape[-1]
    sub, units = _score_units(k_ref, heads_cols)
    s_next = _tile_scores(k_ref, qm_ref, units[0], sub, col)
    for i, (_, vh, s0, c) in enumerate(units):
        s = s_next
        if i + 1 < len(units):
            s_next = _tile_scores(k_ref, qm_ref, units[i + 1], sub, col)
        p = jnp.exp2(s).astype(BF16)
        acc_ref[c] += jnp.dot(vt_ref[0, vh, :, s0:s0 + sub], p, preferred_element_type=F32)


def _online_tiles(k_ref, vt_ref, qm_ref, m_ref, acc_ref, heads_cols):
    col = acc_ref.shape[-1]
    sub, units = _score_units(k_ref, heads_cols)
    for unit in units:
        _, vh, s0, c = unit
        s = _tile_scores(k_ref, qm_ref, unit, sub, col)
        m_prev = m_ref[c]
        m_new = jnp.maximum(m_prev, jnp.max(s, axis=0, keepdims=True))
        alpha = jnp.exp2(m_prev - m_new)
        p = jnp.exp2(s - m_new).astype(BF16)
        acc_ref[c] = acc_ref[c] * alpha + jnp.dot(vt_ref[0, vh, :, s0:s0 + sub], p, preferred_element_type=F32)
        m_ref[c] = m_new


def _flash_init(kb_ref, kc_ref, n_key_heads, m_ref, acc_ref, plain_ref):
    k_norm2 = kb_ref[1]
    if kc_ref is not None:
        for kh in range(n_key_heads):
            k_norm2 = jnp.maximum(k_norm2, _dup_key_norm2(kc_ref[0, kh]))
    plain_ref[0] = _needs_no_stabiliser(kb_ref[0], k_norm2)
    m_ref[...] = jnp.full(m_ref.shape, -jnp.inf, F32)
    acc_ref[...] = jnp.zeros(acc_ref.shape, F32)


def _flash_step(k_ref, vt_ref, qm_ref, m_ref, acc_ref, plain_ref, heads_cols):
    @pl.when(plain_ref[0] == 1)
    def _():
        _plain_tiles(k_ref, vt_ref, qm_ref, acc_ref, heads_cols)

    @pl.when(plain_ref[0] != 1)
    def _():
        _online_tiles(k_ref, vt_ref, qm_ref, m_ref, acc_ref, heads_cols)


def _gqa_kernel(*refs, has_cache, tq, hp):
    if has_cache:
        kb_ref, q_ref, k_ref, vt_ref, kc_ref, o_ref, qm_ref, m_ref, acc_ref, plain_ref = refs
    else:
        kb_ref, q_ref, k_ref, vt_ref, o_ref, qm_ref, m_ref, acc_ref, plain_ref = refs
        kc_ref = None
    kt = pl.program_id(3)
    col = acc_ref.shape[-1]
    per_head = tq // col
    per_kv = GQA_GROUP * per_head
    heads_cols = [(hh, hh, range(hh * per_kv, (hh + 1) * per_kv)) for hh in range(hp)]

    @pl.when(kt == 0)
    def _():
        for hh in range(hp):
            for g in range(GQA_GROUP):
                c0 = hh * 2 * LANES + (g // 2) * LANES
                row0 = (hh * GQA_GROUP + g) * tq
                qm_ref[:, row0:row0 + tq] = _masked_queries_t(q_ref[0, :, c0:c0 + LANES], g % 2 == 0)
        _flash_init(kb_ref, kc_ref, hp, m_ref, acc_ref, plain_ref)

    _flash_step(k_ref, vt_ref, qm_ref, m_ref, acc_ref, plain_ref, heads_cols)

    @pl.when(kt == pl.num_programs(3) - 1)
    def _():
        for hh in range(hp):
            for part in range(per_head):
                heads = []
                for g in range(GQA_GROUP):
                    acc = acc_ref[hh * per_kv + g * per_head + part]
                    heads.append(acc[0:HEAD_DIM, :] / acc[HEAD_DIM:HEAD_DIM + 1, :])
                o_ref[0, part * col:(part + 1) * col, hh * 2 * LANES:(hh + 1) * 2 * LANES] = (
                    jnp.concatenate(heads, axis=0).T.astype(BF16))


def _kv_specs(key_heads_blk, val_heads_blk, vt_rows, tk, kd_c):
    specs = [pl.BlockSpec((1, key_heads_blk, tk, LANES), lambda b, h, qi, kt: (b, h, kt, 0)),
             pl.BlockSpec((1, val_heads_blk, vt_rows, tk), lambda b, h, qi, kt: (b, h, 0, kt))]
    if kd_c is not None:
        specs.append(pl.BlockSpec((1, key_heads_blk, kd_c.shape[2], LANES), lambda b, h, qi, kt: (b, h, 0, 0)))
    return specs


def _gqa_attention(q, k_bound, kd, vt, kd_c, n_batch, n_tok, tq, tk, hp):
    has_cache = kd_c is not None
    nk = kd.shape[2] // tk
    rows = hp * GQA_GROUP * tq
    col = min(MAX_COL, tq)
    q_spec = pl.BlockSpec((1, tq, hp * 2 * LANES), lambda b, h, qi, kt: (b, qi, h))
    in_specs = [pl.BlockSpec(memory_space=pltpu.SMEM), q_spec]
    in_specs += _kv_specs(hp, hp, GQA_VT_ROWS, tk, kd_c)
    args = [k_bound, q.reshape(n_batch, n_tok, BRANCH_W), kd, vt]
    if has_cache:
        args.append(kd_c)
    out = pl.pallas_call(
        functools.partial(_gqa_kernel, has_cache=has_cache, tq=tq, hp=hp),
        grid=(n_batch, GQA_KV_HEADS // hp, n_tok // tq, nk),
        in_specs=in_specs,
        out_specs=q_spec,
        out_shape=jax.ShapeDtypeStruct((n_batch, n_tok, BRANCH_W), BF16),
        scratch_shapes=[
            pltpu.VMEM((LANES, rows), BF16),
            pltpu.VMEM((rows // col, 1, col), F32),
            pltpu.VMEM((rows // col, GQA_VT_ROWS, col), F32),
            pltpu.SMEM((1,), jnp.int32),
        ],
        compiler_params=pltpu.CompilerParams(
            dimension_semantics=("parallel", "parallel", "parallel", "arbitrary"),
            vmem_limit_bytes=VMEM_LIMIT),
        name="gqa_attention",
    )(*args)
    return out.reshape(n_batch * n_tok, BRANCH_W)


def _diff_kernel(*refs, has_cache, lam_init, hp):
    if has_cache:
        (kb_ref, q_ref, lam_ref, gsub_ref, k_ref, vt_ref, kc_ref, o_ref,
         qm_ref, m_ref, acc_ref, plain_ref) = refs
    else:
        kb_ref, q_ref, lam_ref, gsub_ref, k_ref, vt_ref, o_ref, qm_ref, m_ref, acc_ref, plain_ref = refs
        kc_ref = None
    kt = pl.program_id(3)
    tq = q_ref.shape[1]
    col = acc_ref.shape[-1]
    per_map = tq // col
    heads_cols = [(2 * hh + mm, hh, range((2 * hh + mm) * per_map, (2 * hh + mm + 1) * per_map))
                  for hh in range(hp) for mm in range(2)]

    @pl.when(kt == 0)
    def _():
        for hh in range(hp):
            blk = q_ref[0, :, hh * LANES:(hh + 1) * LANES]
            for mm in range(2):
                row0 = (2 * hh + mm) * tq
                qm_ref[:, row0:row0 + tq] = _masked_queries_t(blk, mm == 0)
        _flash_init(kb_ref, kc_ref, 2 * hp, m_ref, acc_ref, plain_ref)

    _flash_step(k_ref, vt_ref, qm_ref, m_ref, acc_ref, plain_ref, heads_cols)

    @pl.when(kt == pl.num_programs(3) - 1)
    def _():
        lp = lam_ref[...]
        lam = (jnp.exp(jnp.sum(lp[0:1] * lp[1:2], axis=-1, keepdims=True))
               - jnp.exp(jnp.sum(lp[2:3] * lp[3:4], axis=-1, keepdims=True)) + lam_init)
        for hh in range(hp):
            for part in range(per_map):
                a0, a1 = acc_ref[2 * hh * per_map + part], acc_ref[(2 * hh + 1) * per_map + part]
                o0 = a0[0:DIFF_DV, :] / a0[DIFF_DV:DIFF_DV + 1, :]
                o1 = a1[0:DIFF_DV, :] / a1[DIFF_DV:DIFF_DV + 1, :]
                d = (o0 - lam * o1).T
                ms = jnp.mean(d * d, axis=-1, keepdims=True)
                o_ref[0, part * col:(part + 1) * col, hh * LANES:(hh + 1) * LANES] = (
                    d * lax.rsqrt(ms + EPS) * gsub_ref[...] * (1.0 - lam_init)).astype(BF16)


def _diff_attention(q, k_bound, lam_p, gsub, kd, vt, kd_c, n_batch, n_tok, tq, tk, hp, lam_init):
    has_cache = kd_c is not None
    nk = kd.shape[2] // tk
    col = min(MAX_COL, tq)
    rows = hp * 2 * tq
    q_spec = pl.BlockSpec((1, tq, hp * LANES), lambda b, h, qi, kt: (b, qi, h))
    in_specs = [
        pl.BlockSpec(memory_space=pltpu.SMEM),
        q_spec,
        pl.BlockSpec((4, HEAD_DIM), lambda b, h, qi, kt: (0, 0)),
        pl.BlockSpec((1, DIFF_DV), lambda b, h, qi, kt: (0, 0)),
    ]
    in_specs += _kv_specs(2 * hp, hp, DIFF_VT_ROWS, tk, kd_c)
    args = [k_bound, q.reshape(n_batch, n_tok, BRANCH_W), lam_p, gsub, kd, vt]
    if has_cache:
        args.append(kd_c)
    out = pl.pallas_call(
        functools.partial(_diff_kernel, has_cache=has_cache, lam_init=lam_init, hp=hp),
        grid=(n_batch, DIFF_HEADS // hp, n_tok // tq, nk),
        in_specs=in_specs,
        out_specs=q_spec,
        out_shape=jax.ShapeDtypeStruct((n_batch, n_tok, BRANCH_W), BF16),
        scratch_shapes=[
            pltpu.VMEM((LANES, rows), BF16),
            pltpu.VMEM((rows // col, 1, col), F32),
            pltpu.VMEM((rows // col, DIFF_VT_ROWS, col), F32),
            pltpu.SMEM((1,), jnp.int32),
        ],
        compiler_params=pltpu.CompilerParams(
            dimension_semantics=("parallel", "parallel", "parallel", "arbitrary"),
            vmem_limit_bytes=VMEM_LIMIT),
        name="diff_attention",
    )(*args)
    return out.reshape(n_batch * n_tok, BRANCH_W)


def _gla_kernel(*refs, has_s0, n_chunk, bg):
    if has_s0:
        qf, kf, vf, laf, qb, kb, vb, lab, s0_ref, of_ref, ob_ref, sfin_ref, st_ref = refs
    else:
        qf, kf, vf, laf, qb, kb, vb, lab, of_ref, ob_ref, sfin_ref, st_ref = refs
    i = pl.program_id(1)
    ck = GLA_CHUNK

    @pl.when(i == 0)
    def _():
        for bb in range(bg):
            for d in range(2):
                for hd in range(GLA_HEADS):
                    if has_s0:
                        s = s0_ref[bb, d, hd]
                        z = jnp.zeros_like(s)
                        padded = jnp.concatenate([s, z] if hd % 2 == 0 else [z, s], axis=0)
                        st_ref[bb, d, hd] = padded.T
                    else:
                        st_ref[bb, d, hd] = jnp.zeros((GLA_DV, LANES), F32)

    tb = qf.shape[1]
    r = lax.broadcasted_iota(jnp.int32, (tb, tb), 0)
    c = lax.broadcasted_iota(jnp.int32, (tb, tb), 1)
    lane = lax.broadcasted_iota(jnp.int32, (tb, LANES), 1)
    chunk_id = lax.broadcasted_iota(jnp.int32, (tb, GLA_W), 0) // ck
    zero_row = jnp.zeros((1, GLA_W), F32)
    streams = []
    for bb in range(bg):
        streams.append((bb, 0, qf.at[bb], kf.at[bb], vf.at[bb], laf.at[bb], of_ref.at[bb], c <= r))
        streams.append((bb, 1, qb.at[bb], kb.at[bb], vb.at[bb], lab.at[bb], ob_ref.at[bb], c >= r))
    nt = (((1,), (1,)), ((), ()))
    g_all = [sum(jnp.dot(tri.astype(F32).astype(BF16), part, preferred_element_type=F32)
                 for part in _bf16_terms(la_r[...]))
             for (_, _, _, _, _, la_r, _, tri) in streams]
    prepared = []
    for g, (bb, d, q_r, k_r, v_r, la_r, o_r, tri) in zip(g_all, streams):
        if d == 0:
            bounds = [zero_row] + [g[ck * j - 1:ck * j, :] for j in range(1, n_chunk)]
            g_end = g[tb - 1:tb, :]
        else:
            bounds = [g[ck * (j + 1):ck * (j + 1) + 1, :] for j in range(n_chunk - 1)] + [zero_row]
            g_end = g[0:1, :]
        b_rows = jnp.concatenate([jnp.broadcast_to(b, (ck, GLA_W)) for b in bounds], axis=0)
        q = q_r[...].astype(F32)
        k = k_r[...].astype(F32)
        q_dec = q * jnp.exp(g - b_rows)
        q_glob = q * jnp.exp(g)
        k_end = k * jnp.exp(g_end - g)
        k_rel = []
        for j in range(n_chunk):
            reach = (chunk_id <= j) if d == 0 else (chunk_id >= j)
            k_rel.append((k * jnp.exp(jnp.where(reach, bounds[j] - g, 0.0))).astype(BF16))
        prepared.append((q_dec, q_glob, k_end, k_rel, jnp.exp(g_end), v_r[...]))

    partial = []
    for (bb, d, *_), (q_dec, q_glob, k_end, k_rel, decay, v) in zip(streams, prepared):
        per_head = []
        for hd in range(GLA_HEADS):
            pair = slice((hd // 2) * LANES, (hd // 2 + 1) * LANES)
            hv = slice(hd * GLA_DV, (hd + 1) * GLA_DV)
            keep = (lane < GLA_DK) if hd % 2 == 0 else (lane >= GLA_DK)
            qd_m = jnp.where(keep, q_dec[:, pair], 0.0).astype(BF16)
            qg_m = jnp.where(keep, q_glob[:, pair], 0.0).astype(BF16)
            ke_m = jnp.where(keep, k_end[:, pair], 0.0).astype(BF16)
            a_rows = [lax.dot_general(qd_m[ck * j:ck * (j + 1), :], k_rel[j][:, pair], nt,
                                      preferred_element_type=F32) for j in range(n_chunk)]
            s_t = st_ref[bb, d, hd]
            o_state = lax.dot_general(qg_m, s_t.astype(BF16), nt, preferred_element_type=F32)
            ds_t = lax.dot_general(v[:, hv], ke_m, (((0,), (0,)), ((), ())), preferred_element_type=F32)
            st_ref[bb, d, hd] = s_t * decay[:, pair] + ds_t
            per_head.append((a_rows, o_state))
        partial.append(per_head)

    for (bb, d, _, _, _, _, o_r, tri), prep, per_head in zip(streams, prepared, partial):
        v = prep[5]
        for hd in range(GLA_HEADS):
            hv = slice(hd * GLA_DV, (hd + 1) * GLA_DV)
            a_rows, o_state = per_head[hd]
            a = jnp.where(tri, jnp.concatenate(a_rows, axis=0), 0.0).astype(BF16)
            o_r[:, hv] = (jnp.dot(a, v[:, hv], preferred_element_type=F32) + o_state).astype(BF16)

    @pl.when(i == pl.num_programs(1) - 1)
    def _():
        for bb in range(bg):
            for d in range(2):
                for hd in range(GLA_HEADS):
                    t = st_ref[bb, d, hd].T
                    sfin_ref[bb, d, hd] = t[(hd % 2) * GLA_DK:(hd % 2 + 1) * GLA_DK, :]


def _gla(lq, lk, lv, la, s0, n_batch, n_tok, tb, bg):
    nb = n_tok // tb
    has_s0 = s0 is not None
    as3d = lambda x: x.reshape(n_batch, n_tok, x.shape[-1])
    fwd = lambda b, i: (b, i, 0)
    bwd = lambda b, i: (b, nb - 1 - i, 0)
    bwd_la = lambda b, i: (b, nb - 1 - i, 1)
    in_specs = [
        pl.BlockSpec((bg, tb, GLA_W), fwd), pl.BlockSpec((bg, tb, GLA_W), fwd),
        pl.BlockSpec((bg, tb, BRANCH_W), fwd), pl.BlockSpec((bg, tb, GLA_W), fwd),
        pl.BlockSpec((bg, tb, GLA_W), bwd), pl.BlockSpec((bg, tb, GLA_W), bwd),
        pl.BlockSpec((bg, tb, BRANCH_W), bwd), pl.BlockSpec((bg, tb, GLA_W), bwd_la),
    ]
    args = [as3d(lq), as3d(lk), as3d(lv), as3d(la)] * 2
    state_spec = pl.BlockSpec((bg, 2, GLA_HEADS, GLA_DK, GLA_DV), lambda b, i: (b, 0, 0, 0, 0))
    if has_s0:
        in_specs.append(state_spec)
        args.append(s0)
    o_f, o_b, s_fin = pl.pallas_call(
        functools.partial(_gla_kernel, has_s0=has_s0, n_chunk=tb // GLA_CHUNK, bg=bg),
        grid=(n_batch // bg, nb),
        in_specs=in_specs,
        out_specs=[pl.BlockSpec((bg, tb, BRANCH_W), fwd), pl.BlockSpec((bg, tb, BRANCH_W), bwd), state_spec],
        out_shape=[
            jax.ShapeDtypeStruct((n_batch, n_tok, BRANCH_W), BF16),
            jax.ShapeDtypeStruct((n_batch, n_tok, BRANCH_W), BF16),
            jax.ShapeDtypeStruct((n_batch, 2, GLA_HEADS, GLA_DK, GLA_DV), F32),
        ],
        scratch_shapes=[pltpu.VMEM((bg, 2, GLA_HEADS, GLA_DV, LANES), F32)],
        compiler_params=pltpu.CompilerParams(
            dimension_semantics=("parallel", "arbitrary"), vmem_limit_bytes=VMEM_LIMIT),
        name="gla",
    )(*args)
    flat = lambda x: x.reshape(n_batch * n_tok, BRANCH_W)
    return flat(o_f), flat(o_b), s_fin


FFN_CHUNK = 256


def _merge_ffn_kernel(x_ref, mod_ref, og_ref, of_ref, ob_ref, lgo_ref, od_ref, gate_ref,
                      gout_ref, wb_ref, wo_ref, g2_ref, wi_ref, wd_ref, y_ref):
    o_gla = of_ref[...].astype(F32) + ob_ref[...].astype(F32)
    gla_parts = []
    for hd in range(GLA_HEADS):
        blk = o_gla[:, hd * GLA_DV:(hd + 1) * GLA_DV]
        ms = jnp.mean(blk * blk, axis=-1, keepdims=True)
        gla_parts.append(blk * lax.rsqrt(ms + EPS) * gout_ref[...])
    gla = jnp.concatenate(gla_parts, axis=1) * _silu(lgo_ref[...].astype(F32))
    branches = (og_ref[...], gla.astype(BF16), od_ref[...])
    mixed = None
    for b, ob in enumerate(branches):
        y = jnp.dot(ob, wb_ref[b], preferred_element_type=F32)
        y = y * gate_ref[:, b * D_MODEL:(b + 1) * D_MODEL].astype(F32)
        mixed = y if mixed is None else mixed + y
    out = jnp.dot(mixed.astype(BF16), wo_ref[...], preferred_element_type=F32)
    x = x_ref[...] + mod_ref[0, 2:3, :] * out

    ms = jnp.mean(x * x, axis=-1, keepdims=True)
    h = x * lax.rsqrt(ms + EPS) * g2_ref[...]
    h = h * (1.0 + mod_ref[0, 4:5, :]) + mod_ref[0, 3:4, :]
    hb = h.astype(BF16)

    def up_proj(c0):
        a = jnp.dot(hb, wi_ref[:, c0:c0 + FFN_CHUNK], preferred_element_type=F32)
        u = jnp.dot(hb, wi_ref[:, FFN_HIDDEN + c0:FFN_HIDDEN + c0 + FFN_CHUNK], preferred_element_type=F32)
        return a, u

    acc = None
    chunks = list(range(0, FFN_HIDDEN, FFN_CHUNK))
    nxt = up_proj(chunks[0])
    for i, c0 in enumerate(chunks):
        a, u = nxt
        if i + 1 < len(chunks):
            nxt = up_proj(chunks[i + 1])
        act = (_silu(a) * u).astype(BF16)
        part = jnp.dot(act, wd_ref[c0:c0 + FFN_CHUNK, :], preferred_element_type=F32)
        acc = part if acc is None else acc + part
    y_ref[...] = x + mod_ref[0, 5:6, :] * acc


def _merge_ffn(x2d, mod, og, o_f, o_b, lgo, od, gates, lw, sw, l, tiles_per_mod, tm):
    t_total = x2d.shape[0]
    tile = lambda c: pl.BlockSpec((tm, c), lambda i: (i, 0))
    return pl.pallas_call(
        _merge_ffn_kernel,
        grid=(t_total // tm,),
        in_specs=[
            tile(D_MODEL),
            pl.BlockSpec((1, 6, D_MODEL), lambda i: (i // tiles_per_mod, 0, 0)),
            tile(BRANCH_W), tile(BRANCH_W), tile(BRANCH_W), tile(BRANCH_W), tile(BRANCH_W),
            tile(N_BRANCH * D_MODEL),
            _full_spec((1, GLA_DV)),
            _layer_spec((N_BRANCH, BRANCH_W, D_MODEL), l),
            _layer_spec((D_MODEL, D_MODEL), l),
            _full_spec((1, D_MODEL)),
            _layer_spec((D_MODEL, 2 * FFN_HIDDEN), l),
            _layer_spec((FFN_HIDDEN, D_MODEL), l),
        ],
        out_specs=tile(D_MODEL),
        out_shape=jax.ShapeDtypeStruct((t_total, D_MODEL), F32),
        compiler_params=pltpu.CompilerParams(
            dimension_semantics=("parallel",), vmem_limit_bytes=VMEM_LIMIT),
        name="merge_ffn",
    )(x2d, mod, og, o_f, o_b, lgo, od, gates, lw["gout"], sw["w_branch"], sw["w_out"],
      lw["norm2"], sw["w_ffn_in"], sw["w_ffn_out"])


def _shared_weights(p):
    return {
        "w_in_t": jnp.swapaxes(p["w_in"], 1, 2).astype(BF16),
        "w_branch": p["w_branch"].astype(BF16), "w_out": p["w_out"].astype(BF16),
        "w_ffn_in": p["w_ffn_in"].astype(BF16), "w_ffn_out": p["w_ffn_out"].astype(BF16),
    }


def _layer_weights(l, p):
    scale = HEAD_DIM ** -0.5 * math.log2(math.e)
    head_bound = lambda g, s: HEAD_DIM * (NORM_MARGIN * s) ** 2 * jnp.max(g * g)
    norm_bounds = lambda gq_, gk_: jnp.stack([head_bound(gq_, scale), head_bound(gk_, 1.0)])
    gq_row = jnp.concatenate([jnp.tile(p["gqa_q_norm"][l], GQA_HEADS),
                              jnp.tile(p["diff_q_norm"][l], 2 * DIFF_HEADS)]) * scale
    gk_row = jnp.concatenate([jnp.tile(p["gqa_k_norm"][l], GQA_KV_HEADS),
                              jnp.tile(p["diff_k_norm"][l], 2 * DIFF_HEADS)])
    idx = jnp.arange(2 * LANES) // HEAD_DIM
    gmat = jnp.where(idx[:, None] == idx[None, :], 1.0 / HEAD_DIM, 0.0).astype(BF16)
    up = jnp.zeros((LANES, 2 * GLA_W), F32)
    up = up.at[0:GLA_RANK, 0:GLA_W].set(p["gla_alpha_up"][l, 0])
    up = up.at[GLA_RANK:2 * GLA_RANK, GLA_W:].set(p["gla_alpha_up"][l, 1])
    ub = p["gla_alpha_bias"][l].reshape(1, 2 * GLA_W)
    up_hi = up.astype(BF16)
    up = jnp.stack([up_hi, (up - up_hi.astype(F32)).astype(BF16)])
    return {
        "gq": gq_row.reshape(1, 2 * N_QG), "gk": gk_row.reshape(1, LANES + BRANCH_W), "gmat": gmat,
        "up": up, "ub": ub,
        "kb_g": norm_bounds(p["gqa_q_norm"][l], p["gqa_k_norm"][l]),
        "kb_d": norm_bounds(p["diff_q_norm"][l], p["diff_k_norm"][l]),
        "norm1": p["norm1"][l].reshape(1, D_MODEL), "norm2": p["norm2"][l].reshape(1, D_MODEL),
        "gout": p["gla_out_norm"][l].reshape(1, GLA_DV),
        "gsub": p["diff_sub_norm"][l].reshape(1, DIFF_DV),
        "lam": p["diff_lambda"][l],
    }


def _rope_tables(n_tokens):
    n_rows = n_tokens // GRID_W
    row = jnp.repeat(jnp.arange(n_rows, dtype=F32), GRID_W)
    col = jnp.tile(jnp.arange(GRID_W, dtype=F32), n_rows)
    n_freq = HEAD_DIM // 4
    freqs = ROPE_THETA ** (-jnp.arange(n_freq, dtype=F32) / n_freq)
    ar, ac = row[:, None] * freqs, col[:, None] * freqs
    cos = jnp.concatenate([jnp.cos(ar), jnp.cos(ar), jnp.cos(ac), jnp.cos(ac)], axis=1)
    sin = jnp.concatenate([-jnp.sin(ar), jnp.sin(ar), -jnp.sin(ac), jnp.sin(ac)], axis=1)
    return jnp.tile(cos, (1, LANES // HEAD_DIM)), jnp.tile(sin, (1, LANES // HEAD_DIM))


def _with_ones_rows(v_t):
    lead = v_t.shape[:-2]
    s = v_t.shape[-1]
    ones = jnp.ones(lead + (1, s), v_t.dtype)
    zeros = jnp.zeros(lead + (BF16_ROWS - 1, s), v_t.dtype)
    return jnp.concatenate([v_t, ones, zeros], axis=-2).astype(BF16)


def _cache_layouts(l, cache_gqa_k, cache_gqa_v, cache_diff_k, cache_diff_v):
    b = cache_gqa_k.shape[0]
    gk = jnp.transpose(cache_gqa_k[:, l], (0, 2, 1, 3))
    kd_g = jnp.concatenate([gk, gk], axis=-1).astype(BF16)
    vt_g = _with_ones_rows(jnp.transpose(cache_gqa_v[:, l], (0, 2, 3, 1)))
    dk = jnp.transpose(cache_diff_k[:, l], (0, 2, 3, 1, 4)).reshape(b, 2 * DIFF_HEADS, PAST_LEN, HEAD_DIM)
    kd_d = jnp.concatenate([dk, dk], axis=-1).astype(BF16)
    vt_d = _with_ones_rows(jnp.transpose(cache_diff_v[:, l], (0, 2, 3, 1)))
    return kd_g, vt_g, kd_d, vt_d


def _run_layer(x2d, mod, lw, sw, l, n_batch, n_tok, rope_tabs, ctx, emit_cache, cfg):
    tiles_per_mod = lambda tm: (n_tok // tm) if mod.shape[0] > 1 else (n_batch * n_tok // tm)
    if ctx is None:
        cache_kv = kd_gc = kd_dc = s0 = None
    else:
        kd_gc, vt_gc, kd_dc, vt_dc, s0 = ctx
        cache_kv = (kd_gc, kd_dc, vt_gc, vt_dc)
    outs = _in_projection(x2d, mod, lw, sw, l, n_batch, n_tok, cache_kv, rope_tabs, emit_cache, cfg["tm_in"])
    qg, qd, kdg, kdd, vtg, vtd, lq, lk, lv, lgo, la, gates = outs[:12]
    og = _gqa_attention(qg, lw["kb_g"], kdg, vtg, kd_gc, n_batch, n_tok, cfg["tq_gqa"], cfg["tk"],
                        cfg["hp_gqa"])
    lam_init = 0.8 - 0.6 * math.exp(-0.3 * l)
    od = _diff_attention(qd, lw["kb_d"], lw["lam"], lw["gsub"], kdd, vtd, kd_dc, n_batch, n_tok,
                         cfg["tq_diff"], cfg["tk"], cfg["hp_diff"], lam_init)
    o_f, o_b, s_fin = _gla(lq, lk, lv, la, s0, n_batch, n_tok, cfg["tb_gla"], cfg["bg_gla"])
    x2d = _merge_ffn(x2d, mod, og, o_f, o_b, lgo, od, gates, lw, sw, l, tiles_per_mod(cfg["tm"]), cfg["tm"])
    cache = tuple(outs[12:]) + (s_fin,) if emit_cache else None
    return x2d, cache


PROMPT_CFG = dict(tm_in=512, tq_gqa=256, tq_diff=256, tk=256, hp_gqa=2, hp_diff=4, tb_gla=256, bg_gla=2, tm=512)
SAMPLE_CFG = dict(tm_in=512, tq_gqa=1024, tq_diff=2048, tk=1536, hp_gqa=1, hp_diff=1, tb_gla=256, bg_gla=2, tm=512)


def kernel(x_prompt, x_sample, c, cache_gqa_k, cache_gqa_v, state_gla, cache_diff_k, cache_diff_v, c_ctx, w_mod, b_mod, norm1, norm2, w_in, gqa_q_norm, gqa_k_norm, gla_alpha_up, gla_alpha_bias, gla_out_norm, diff_q_norm, diff_k_norm, diff_lambda, diff_sub_norm, w_branch, w_out, w_ffn_in, w_ffn_out):
    p = {
        "norm1": norm1, "norm2": norm2, "w_in": w_in, "gqa_q_norm": gqa_q_norm, "gqa_k_norm": gqa_k_norm,
        "gla_alpha_up": gla_alpha_up, "gla_alpha_bias": gla_alpha_bias, "gla_out_norm": gla_out_norm,
        "diff_q_norm": diff_q_norm, "diff_k_norm": diff_k_norm, "diff_lambda": diff_lambda,
        "diff_sub_norm": diff_sub_norm, "w_branch": w_branch, "w_out": w_out,
        "w_ffn_in": w_ffn_in, "w_ffn_out": w_ffn_out,
    }
    n_ctx_b, n_ctx = x_prompt.shape[:2]
    n_lat_b, n_lat = x_sample.shape[:2]
    cond_rows = jnp.concatenate(
        [c_ctx[None, :], c, jnp.zeros((8 - 1 - n_lat_b, D_MODEL), F32)], axis=0)
    mod_all = _modulation(cond_rows, w_mod, b_mod)
    weights = [_layer_weights(l, p) for l in range(DEPTH)]
    shared = _shared_weights(p)

    y = x_prompt.reshape(n_ctx_b * n_ctx, D_MODEL)
    caches = []
    for l in range(DEPTH):
        mod = mod_all[l, 0:1].reshape(1, 6, D_MODEL)
        y, cache = _run_layer(y, mod, weights[l], shared, l, n_ctx_b, n_ctx, None, None, True, PROMPT_CFG)
        caches.append(cache)
    y_prompt = y.reshape(n_ctx_b, n_ctx, D_MODEL)
    stack = lambda j, shape: jnp.stack([cc[j].reshape(shape) for cc in caches], axis=1)
    new_gqa_k = stack(0, (n_ctx_b, n_ctx, GQA_KV_HEADS, HEAD_DIM))
    new_diff_k = stack(1, (n_ctx_b, n_ctx, DIFF_HEADS, 2, HEAD_DIM))
    new_gqa_v = stack(2, (n_ctx_b, n_ctx, GQA_KV_HEADS, HEAD_DIM))
    new_diff_v = stack(3, (n_ctx_b, n_ctx, DIFF_HEADS, DIFF_DV))
    new_state_gla = jnp.stack([cc[4] for cc in caches], axis=1)

    rope_tabs = _rope_tables(n_lat)
    y = x_sample.reshape(n_lat_b * n_lat, D_MODEL)
    for l in range(DEPTH):
        mod = mod_all[l, 1:1 + n_lat_b].reshape(n_lat_b, 6, D_MODEL)
        ctx = _cache_layouts(l, cache_gqa_k, cache_gqa_v, cache_diff_k, cache_diff_v) + (state_gla[:, l],)
        y, _ = _run_layer(y, mod, weights[l], shared, l, n_lat_b, n_lat, rope_tabs, ctx, False, SAMPLE_CFG)
    y_sample = y.reshape(n_lat_b, n_lat, D_MODEL)
    return (y_prompt, y_sample, new_gqa_k, new_gqa_v, new_state_gla, new_diff_k, new_diff_v)
```

```python
import functools
import math

import jax
import jax.numpy as jnp
from jax import lax
from jax.experimental import pallas as pl
from jax.experimental.pallas import tpu as pltpu

D_MODEL = 1024
DEPTH = 2
PAST_LEN = 512
GRID_W = 64
HEAD_DIM = 64
GQA_HEADS = 8
GQA_KV_HEADS = 2
GQA_GROUP = GQA_HEADS // GQA_KV_HEADS
GLA_HEADS = 4
GLA_DK = 64
GLA_DV = 128
GLA_RANK = 16
GLA_TAU = 16.0
GLA_CHUNK = 64
DIFF_HEADS = 4
DIFF_DV = 2 * HEAD_DIM
N_BRANCH = 3
BRANCH_W = 512
FFN_HIDDEN = ((8 * D_MODEL + 3 * 256 - 1) // (3 * 256)) * 256
ROPE_THETA = 10000.0
EPS = 1e-6

LANES = 128
BF16_ROWS = 16
VMEM_LIMIT = 56 * 1024 * 1024

F32 = jnp.float32
BF16 = jnp.bfloat16

R_GQ = 0
R_KV = R_GQ + GQA_HEADS * HEAD_DIM
R_GLA = R_KV + 2 * GQA_KV_HEADS * HEAD_DIM
R_LR = R_GLA + 2 * GLA_HEADS * GLA_DK + GLA_HEADS * GLA_DV
R_LGO = R_LR + 2 * GLA_RANK
R_DQ = R_LGO + GLA_HEADS * GLA_DV
R_DK = R_DQ + DIFF_HEADS * 2 * HEAD_DIM
R_DV = R_DK + DIFF_HEADS * 2 * HEAD_DIM
R_MG = R_DV + DIFF_HEADS * DIFF_DV
N_IN = R_MG + N_BRANCH * D_MODEL
NT_DIMS = (((1,), (1,)), ((), ()))
N_QG = GQA_HEADS * HEAD_DIM
GLA_W = GLA_HEADS * GLA_DK
GQA_VT_ROWS = HEAD_DIM + BF16_ROWS
DIFF_VT_ROWS = DIFF_DV + BF16_ROWS


def _sigmoid(x):
    return 1.0 / (1.0 + jnp.exp(-x))


def _silu(x):
    return x * _sigmoid(x)


def _bf16_terms(x):
    hi = x.astype(BF16)
    r1 = x - hi.astype(F32)
    mid = r1.astype(BF16)
    lo = (r1 - mid.astype(F32)).astype(BF16)
    return hi, mid, lo


def _full_spec(shape):
    n = len(shape)
    return pl.BlockSpec(shape, lambda *_: (0,) * n, pipeline_mode=pl.Buffered(1))


def _layer_spec(shape, l):
    n = len(shape)
    return pl.BlockSpec((None,) + tuple(shape), lambda *_: (l,) + (0,) * n, pipeline_mode=pl.Buffered(1))


def _mod_kernel(cond_ref, w_ref, b_ref, o_ref):
    s_hi, s_mid, _ = _bf16_terms(_silu(cond_ref[...]))
    lhs = jnp.concatenate([s_hi, s_mid], axis=0)
    w = w_ref[0]
    w_hi = w.astype(BF16)
    w_mid = (w - w_hi.astype(F32)).astype(BF16)
    r = jnp.dot(lhs, w_hi, preferred_element_type=F32) + jnp.dot(lhs, w_mid, preferred_element_type=F32)
    n = cond_ref.shape[0]
    o_ref[0] = r[0:n] + r[n:2 * n] + b_ref[0]


def _modulation(cond_rows, w_mod, b_mod):
    tn = 1536
    n_out = 6 * D_MODEL
    return pl.pallas_call(
        _mod_kernel,
        grid=(DEPTH, n_out // tn),
        in_specs=[
            pl.BlockSpec((8, D_MODEL), lambda l, j: (0, 0)),
            pl.BlockSpec((1, D_MODEL, tn), lambda l, j: (l, 0, j)),
            pl.BlockSpec((1, 1, tn), lambda l, j: (l, 0, j)),
        ],
        out_specs=pl.BlockSpec((1, 8, tn), lambda l, j: (l, 0, j)),
        out_shape=jax.ShapeDtypeStruct((DEPTH, 8, n_out), F32),
        compiler_params=pltpu.CompilerParams(
            dimension_semantics=("parallel", "parallel"), vmem_limit_bytes=VMEM_LIMIT),
        name="modulation",
    )(cond_rows, w_mod, b_mod.reshape(DEPTH, 1, n_out))


def _head_norm(z, gain, gmat):
    outs = []
    n = z.shape[1]
    for c0 in range(0, n, 2 * LANES):
        w = min(2 * LANES, n - c0)
        zz = z[:, c0:c0 + w]
        ms = jnp.dot((zz * zz).astype(BF16), gmat[0:w, 0:w], preferred_element_type=F32)
        outs.append(zz * lax.rsqrt(ms + EPS) * gain[:, c0:c0 + w])
    return outs


def _rope_block(zb, cos, sin, first):
    partner = jnp.where(first, pltpu.roll(zb, LANES - 16, 1), pltpu.roll(zb, 16, 1))
    return zb * cos + partner * sin


N_IN_BASE = 9


def _in_kernel(*refs, rope, emit_cache, tiles_per_batch, cache_tiles):
    if not cache_tiles:
        _in_tile(refs, rope, emit_cache)
        return
    kcg_ref, kcd_ref, vcg_ref, vcd_ref = refs[:4]
    j = pl.program_id(1)

    @pl.when(j < tiles_per_batch)
    def _():
        _in_tile(refs[4:], rope, emit_cache)

    @pl.when(j >= tiles_per_batch)
    def _():
        n_in = 4 + N_IN_BASE + (2 if rope else 0)
        kdg_ref, kdd_ref, vtg_ref, vtd_ref = refs[n_in + 2:n_in + 6]
        kdg_ref[...] = kcg_ref[...]
        kdd_ref[...] = kcd_ref[...]
        vtg_ref[...] = vcg_ref[...]
        vtd_ref[...] = vcd_ref[...]


def _in_tile(refs, rope, emit_cache):
    it = iter(refs)
    (x_ref, mod_ref, g1_ref, wt_ref, gq_ref, gk_ref, gmat_ref, up_ref, ub_ref) = (
        next(it) for _ in range(N_IN_BASE))
    if rope:
        cos_ref, sin_ref = next(it), next(it)
    (qg_ref, qd_ref, kdg_ref, kdd_ref, vtg_ref, vtd_ref,
     lq_ref, lk_ref, lv_ref, lgo_ref, la_ref, gate_ref) = (next(it) for _ in range(12))
    if emit_cache:
        ck_g_ref, ck_d_ref, cv_g_ref, cv_d_ref = (next(it) for _ in range(4))

    tm = x_ref.shape[0]
    bpt = kdg_ref.shape[0]
    nt = tm // bpt
    x = x_ref[...]
    ms = jnp.mean(x * x, axis=-1, keepdims=True)
    h = x * lax.rsqrt(ms + EPS) * g1_ref[...]
    h = h * (1.0 + mod_ref[0, 1:2, :]) + mod_ref[0, 0:1, :]
    hb = h.astype(BF16)

    def proj(r0, n):
        return lax.dot_general(hb, wt_ref[r0:r0 + n, :], NT_DIMS, preferred_element_type=F32)

    lane = lax.broadcasted_iota(jnp.int32, (tm, LANES), 1)
    first16 = (lane % 32) < 16
    gmat = gmat_ref[...]
    if rope:
        cos, sin = cos_ref[...], sin_ref[...]

    def blocks128(parts):
        out = []
        for p in parts:
            for c in range(0, p.shape[1], LANES):
                out.append(p[:, c:c + LANES])
        return out

    def store_queries(z, gain, dst):
        for j, zb in enumerate(blocks128(_head_norm(z, gain, gmat))):
            if rope:
                zb = _rope_block(zb, cos, sin, first16)
            dst[:, j * LANES:(j + 1) * LANES] = zb.astype(BF16)

    def store_keys(z, gain, dst, cache_dst):
        for j, zb in enumerate(blocks128(_head_norm(z, gain, gmat))):
            if rope:
                zb = _rope_block(zb, cos, sin, first16)
            if emit_cache:
                cache_dst[:, j * LANES:(j + 1) * LANES] = zb
            sw = pltpu.roll(zb, HEAD_DIM, 1)
            for bb in range(bpt):
                dst[bb, 2 * j] = zb[bb * nt:(bb + 1) * nt, 0:HEAD_DIM].astype(BF16)
                dst[bb, 2 * j + 1] = sw[bb * nt:(bb + 1) * nt, 0:HEAD_DIM].astype(BF16)

    z_gq = proj(R_GQ, N_QG)
    z_dq = proj(R_DQ, N_QG)
    store_queries(z_gq, gq_ref[:, 0:N_QG], qg_ref)
    z_kv = proj(R_KV, 2 * LANES)
    store_queries(z_dq, gq_ref[:, N_QG:2 * N_QG], qd_ref)
    z_dk = proj(R_DK, BRANCH_W)
    store_keys(z_kv[:, 0:LANES], gk_ref[:, 0:LANES], kdg_ref, ck_g_ref if emit_cache else None)
    z_dv = proj(R_DV, BRANCH_W)
    store_keys(z_dk, gk_ref[:, LANES:LANES + BRANCH_W], kdd_ref, ck_d_ref if emit_cache else None)
    zl = proj(R_GLA, R_LR - R_GLA)

    ones_rows = jnp.where(lax.broadcasted_iota(jnp.int32, (BF16_ROWS, nt), 0) == 0, 1.0, 0.0).astype(BF16)
    z_gv = z_kv[:, LANES:2 * LANES]
    if emit_cache:
        cv_g_ref[...] = z_gv
        cv_d_ref[...] = z_dv
    for bb in range(bpt):
        rows = slice(bb * nt, (bb + 1) * nt)
        vt = z_gv[rows, :].T
        for hh in range(GQA_KV_HEADS):
            vtg_ref[bb, hh, 0:HEAD_DIM, :] = vt[hh * HEAD_DIM:(hh + 1) * HEAD_DIM, :].astype(BF16)
            vtg_ref[bb, hh, HEAD_DIM:GQA_VT_ROWS, :] = ones_rows
        for hh in range(DIFF_HEADS):
            vt = z_dv[rows, hh * LANES:(hh + 1) * LANES].T
            vtd_ref[bb, hh, 0:DIFF_DV, :] = vt.astype(BF16)
            vtd_ref[bb, hh, DIFF_DV:DIFF_VT_ROWS, :] = ones_rows

    z_lgo = proj(R_LGO, BRANCH_W)
    lr = proj(R_LR, LANES)
    o = 0
    lq_ref[...] = (zl[:, o:o + GLA_W] * (GLA_DK ** -0.5)).astype(BF16)
    o += GLA_W
    lk_ref[...] = zl[:, o:o + GLA_W].astype(BF16)
    o += GLA_W
    lv_ref[...] = zl[:, o:o + BRANCH_W].astype(BF16)
    o += BRANCH_W
    lgo_ref[...] = z_lgo.astype(BF16)
    lr_hi, lr_mid, _ = _bf16_terms(lr)
    z = (jnp.dot(lr_hi, up_ref[0], preferred_element_type=F32)
         + jnp.dot(lr_mid, up_ref[0], preferred_element_type=F32)
         + jnp.dot(lr_hi, up_ref[1], preferred_element_type=F32)) + ub_ref[...]
    log_sig = jnp.minimum(z, 0.0) - jnp.log(1.0 + jnp.exp(-jnp.abs(z)))
    la_ref[...] = log_sig * (1.0 / GLA_TAU)

    for b in range(N_BRANCH):
        gate_ref[:, b * D_MODEL:(b + 1) * D_MODEL] = _sigmoid(
            proj(R_MG + b * D_MODEL, D_MODEL)).astype(BF16)


def _in_projection(x2d, mod, lw, sw, l, n_batch, n_tok, cache_kv, rope_tabs, emit_cache, tm):
    n_cache = 0 if cache_kv is None else cache_kv[0].shape[2]
    assert n_cache % tm == 0
    n_keys = n_tok + n_cache
    t_total = n_batch * n_tok
    bpt = max(1, tm // n_tok)
    assert bpt == 1 or (mod.shape[0] == 1 and n_cache == 0 and tm == bpt * n_tok)
    nt = tm // bpt
    tpb = n_tok // nt
    cache_tiles = n_cache // tm
    rope = rope_tabs is not None
    own = lambda j: jnp.minimum(j, tpb - 1)
    tile = lambda c: pl.BlockSpec((tm, c), lambda b, j: (b * tpb + own(j), 0))
    in_specs, args = [], []
    if cache_tiles:
        past = lambda j: jnp.maximum(j - tpb, 0)
        in_specs += [
            pl.BlockSpec((1, GQA_KV_HEADS, tm, HEAD_DIM), lambda b, j: (b, 0, past(j), 0)),
            pl.BlockSpec((1, 2 * DIFF_HEADS, tm, HEAD_DIM), lambda b, j: (b, 0, past(j), 0)),
            pl.BlockSpec((1, GQA_KV_HEADS, GQA_VT_ROWS, tm), lambda b, j: (b, 0, 0, past(j))),
            pl.BlockSpec((1, DIFF_HEADS, DIFF_VT_ROWS, tm), lambda b, j: (b, 0, 0, past(j))),
        ]
        args += list(cache_kv)
    in_specs += [
        tile(D_MODEL),
        pl.BlockSpec((1, 6, D_MODEL), lambda b, j: (b if mod.shape[0] > 1 else 0, 0, 0)),
        _full_spec((1, D_MODEL)),
        _layer_spec((N_IN, D_MODEL), l),
        _full_spec((1, 2 * N_QG)),
        _full_spec((1, LANES + BRANCH_W)),
        _full_spec((2 * LANES, 2 * LANES)),
        _full_spec((2, LANES, 2 * GLA_W)),
        _full_spec((1, 2 * GLA_W)),
    ]
    args += [x2d, mod, lw["norm1"], sw["w_in_t"], lw["gq"], lw["gk"], lw["gmat"], lw["up"], lw["ub"]]
    if rope:
        in_specs += [pl.BlockSpec((tm, LANES), lambda b, j: (own(j), 0))] * 2
        args += list(rope_tabs)
    sd = jax.ShapeDtypeStruct
    out_shape = [
        sd((t_total, BRANCH_W), BF16), sd((t_total, BRANCH_W), BF16),
        sd((n_batch, GQA_KV_HEADS, n_keys, HEAD_DIM), BF16), sd((n_batch, 2 * DIFF_HEADS, n_keys, HEAD_DIM), BF16),
        sd((n_batch, GQA_KV_HEADS, GQA_VT_ROWS, n_keys), BF16), sd((n_batch, DIFF_HEADS, DIFF_VT_ROWS, n_keys), BF16),
        sd((t_total, GLA_W), BF16), sd((t_total, GLA_W), BF16),
        sd((t_total, BRANCH_W), BF16), sd((t_total, BRANCH_W), BF16),
        sd((t_total, 2 * GLA_W), F32), sd((t_total, N_BRANCH * D_MODEL), BF16),
    ]
    out_specs = [
        tile(BRANCH_W), tile(BRANCH_W),
        pl.BlockSpec((bpt, GQA_KV_HEADS, nt, HEAD_DIM), lambda b, j: (b, 0, j, 0)),
        pl.BlockSpec((bpt, 2 * DIFF_HEADS, nt, HEAD_DIM), lambda b, j: (b, 0, j, 0)),
        pl.BlockSpec((bpt, GQA_KV_HEADS, GQA_VT_ROWS, nt), lambda b, j: (b, 0, 0, j)),
        pl.BlockSpec((bpt, DIFF_HEADS, DIFF_VT_ROWS, nt), lambda b, j: (b, 0, 0, j)),
        tile(GLA_W), tile(GLA_W), tile(BRANCH_W), tile(BRANCH_W), tile(2 * GLA_W), tile(N_BRANCH * D_MODEL),
    ]
    if emit_cache:
        out_shape += [sd((t_total, LANES), F32), sd((t_total, BRANCH_W), F32),
                      sd((t_total, LANES), F32), sd((t_total, BRANCH_W), F32)]
        out_specs += [tile(LANES), tile(BRANCH_W), tile(LANES), tile(BRANCH_W)]
    return pl.pallas_call(
        functools.partial(_in_kernel, rope=rope, emit_cache=emit_cache, tiles_per_batch=tpb,
                          cache_tiles=cache_tiles),
        grid=(n_batch // bpt, tpb + cache_tiles),
        in_specs=in_specs,
        out_specs=out_specs,
        out_shape=out_shape,
        compiler_params=pltpu.CompilerParams(
            dimension_semantics=("parallel", "arbitrary"), vmem_limit_bytes=VMEM_LIMIT),
        name="in_projection",
    )(*args)


MAX_COL = 4 * LANES
SCORE_BOUND = 45.0
NORM_MARGIN = 1.05


def _head_queries_t(q_blk, low):
    t = q_blk.astype(F32).T
    return (t[0:HEAD_DIM, :] if low else t[HEAD_DIM:2 * HEAD_DIM, :]).astype(BF16)


def _key_norm2(kd):
    kf = kd.astype(F32)
    return jnp.max(jnp.sum(kf * kf, axis=1, keepdims=True))


def _needs_no_stabiliser(q_norm2, k_norm2):
    return (q_norm2 * k_norm2 <= SCORE_BOUND * SCORE_BOUND).astype(jnp.int32)


KEY_SUB = 512


def _score_units(k_ref, heads_cols):
    tk = k_ref.shape[2]
    sub = min(KEY_SUB, tk)
    return sub, [(kh, vh, s0, c) for s0 in range(0, tk, sub) for kh, vh, cols in heads_cols for c in cols]


def _tile_scores(k_ref, qm_ref, unit, sub, col):
    kh, _, s0, c = unit
    return jnp.dot(k_ref[0, kh, s0:s0 + sub, :], qm_ref[:, c * col:(c + 1) * col], preferred_element_type=F32)


def _plain_tiles(k_ref, vt_ref, qm_ref, acc_ref, heads_cols):
    col = acc_ref.shape[-1]
    sub, units = _score_units(k_ref, heads_cols)
    s_next = _tile_scores(k_ref, qm_ref, units[0], sub, col)
    for i, (_, vh, s0, c) in enumerate(units):
        s = s_next
        if i + 1 < len(units):
            s_next = _tile_scores(k_ref, qm_ref, units[i + 1], sub, col)
        p = jnp.exp2(s).astype(BF16)
        acc_ref[c] += jnp.dot(vt_ref[0, vh, :, s0:s0 + sub], p, preferred_element_type=F32)


def _online_tiles(k_ref, vt_ref, qm_ref, m_ref, acc_ref, heads_cols):
    col = acc_ref.shape[-1]
    sub, units = _score_units(k_ref, heads_cols)
    for unit in units:
        _, vh, s0, c = unit
        s = _tile_scores(k_ref, qm_ref, unit, sub, col)
        m_prev = m_ref[c]
        m_new = jnp.maximum(m_prev, jnp.max(s, axis=0, keepdims=True))
        alpha = jnp.exp2(m_prev - m_new)
        p = jnp.exp2(s - m_new).astype(BF16)
        acc_ref[c] = acc_ref[c] * alpha + jnp.dot(vt_ref[0, vh, :, s0:s0 + sub], p, preferred_element_type=F32)
        m_ref[c] = m_new


def _flash_init(kb_ref, kc_ref, n_key_heads, m_ref, acc_ref, plain_ref):
    k_norm2 = kb_ref[1]
    if kc_ref is not None:
        for kh in range(n_key_heads):
            k_norm2 = jnp.maximum(k_norm2, _key_norm2(kc_ref[0, kh]))
    plain_ref[0] = _needs_no_stabiliser(kb_ref[0], k_norm2)
    m_ref[...] = jnp.full(m_ref.shape, -jnp.inf, F32)
    acc_ref[...] = jnp.zeros(acc_ref.shape, F32)


def _flash_step(k_ref, vt_ref, qm_ref, m_ref, acc_ref, plain_ref, heads_cols):
    @pl.when(plain_ref[0] == 1)
    def _():
        _plain_tiles(k_ref, vt_ref, qm_ref, acc_ref, heads_cols)

    @pl.when(plain_ref[0] != 1)
    def _():
        _online_tiles(k_ref, vt_ref, qm_ref, m_ref, acc_ref, heads_cols)


def _gqa_kernel(*refs, has_cache, tq, hp):
    if has_cache:
        kb_ref, q_ref, k_ref, vt_ref, kc_ref, o_ref, qm_ref, m_ref, acc_ref, plain_ref = refs
    else:
        kb_ref, q_ref, k_ref, vt_ref, o_ref, qm_ref, m_ref, acc_ref, plain_ref = refs
        kc_ref = None
    kt = pl.program_id(3)
    col = acc_ref.shape[-1]
    per_head = tq // col
    per_kv = GQA_GROUP * per_head
    heads_cols = [(hh, hh, range(hh * per_kv, (hh + 1) * per_kv)) for hh in range(hp)]

    @pl.when(kt == 0)
    def _():
        for hh in range(hp):
            for g in range(GQA_GROUP):
                c0 = hh * 2 * LANES + (g // 2) * LANES
                row0 = (hh * GQA_GROUP + g) * tq
                qm_ref[:, row0:row0 + tq] = _head_queries_t(q_ref[0, :, c0:c0 + LANES], g % 2 == 0)
        _flash_init(kb_ref, kc_ref, hp, m_ref, acc_ref, plain_ref)

    _flash_step(k_ref, vt_ref, qm_ref, m_ref, acc_ref, plain_ref, heads_cols)

    @pl.when(kt == pl.num_programs(3) - 1)
    def _():
        for hh in range(hp):
            for part in range(per_head):
                heads = []
                for g in range(GQA_GROUP):
                    acc = acc_ref[hh * per_kv + g * per_head + part]
                    heads.append(acc[0:HEAD_DIM, :] / acc[HEAD_DIM:HEAD_DIM + 1, :])
                o_ref[0, part * col:(part + 1) * col, hh * 2 * LANES:(hh + 1) * 2 * LANES] = (
                    jnp.concatenate(heads, axis=0).T.astype(BF16))


def _kv_specs(key_heads_blk, val_heads_blk, vt_rows, tk, kd_c):
    specs = [pl.BlockSpec((1, key_heads_blk, tk, HEAD_DIM), lambda b, h, qi, kt: (b, h, kt, 0)),
             pl.BlockSpec((1, val_heads_blk, vt_rows, tk), lambda b, h, qi, kt: (b, h, 0, kt))]
    if kd_c is not None:
        specs.append(pl.BlockSpec((1, key_heads_blk, kd_c.shape[2], HEAD_DIM), lambda b, h, qi, kt: (b, h, 0, 0)))
    return specs


def _gqa_attention(q, k_bound, kd, vt, kd_c, n_batch, n_tok, tq, tk, hp):
    has_cache = kd_c is not None
    nk = kd.shape[2] // tk
    rows = hp * GQA_GROUP * tq
    col = min(MAX_COL, tq)
    q_spec = pl.BlockSpec((1, tq, hp * 2 * LANES), lambda b, h, qi, kt: (b, qi, h))
    in_specs = [pl.BlockSpec(memory_space=pltpu.SMEM), q_spec]
    in_specs += _kv_specs(hp, hp, GQA_VT_ROWS, tk, kd_c)
    args = [k_bound, q.reshape(n_batch, n_tok, BRANCH_W), kd, vt]
    if has_cache:
        args.append(kd_c)
    out = pl.pallas_call(
        functools.partial(_gqa_kernel, has_cache=has_cache, tq=tq, hp=hp),
        grid=(n_batch, GQA_KV_HEADS // hp, n_tok // tq, nk),
        in_specs=in_specs,
        out_specs=q_spec,
        out_shape=jax.ShapeDtypeStruct((n_batch, n_tok, BRANCH_W), BF16),
        scratch_shapes=[
            pltpu.VMEM((HEAD_DIM, rows), BF16),
            pltpu.VMEM((rows // col, 1, col), F32),
            pltpu.VMEM((rows // col, GQA_VT_ROWS, col), F32),
            pltpu.SMEM((1,), jnp.int32),
        ],
        compiler_params=pltpu.CompilerParams(
            dimension_semantics=("parallel", "parallel", "parallel", "arbitrary"),
            vmem_limit_bytes=VMEM_LIMIT),
        name="gqa_attention",
    )(*args)
    return out.reshape(n_batch * n_tok, BRANCH_W)


def _diff_kernel(*refs, has_cache, lam_init, hp):
    if has_cache:
        (kb_ref, q_ref, lam_ref, gsub_ref, k_ref, vt_ref, kc_ref, o_ref,
         qm_ref, m_ref, acc_ref, plain_ref) = refs
    else:
        kb_ref, q_ref, lam_ref, gsub_ref, k_ref, vt_ref, o_ref, qm_ref, m_ref, acc_ref, plain_ref = refs
        kc_ref = None
    kt = pl.program_id(3)
    tq = q_ref.shape[1]
    col = acc_ref.shape[-1]
    per_map = tq // col
    heads_cols = [(2 * hh + mm, hh, range((2 * hh + mm) * per_map, (2 * hh + mm + 1) * per_map))
                  for hh in range(hp) for mm in range(2)]

    @pl.when(kt == 0)
    def _():
        for hh in range(hp):
            blk = q_ref[0, :, hh * LANES:(hh + 1) * LANES]
            for mm in range(2):
                row0 = (2 * hh + mm) * tq
                qm_ref[:, row0:row0 + tq] = _head_queries_t(blk, mm == 0)
        _flash_init(kb_ref, kc_ref, 2 * hp, m_ref, acc_ref, plain_ref)

    _flash_step(k_ref, vt_ref, qm_ref, m_ref, acc_ref, plain_ref, heads_cols)

    @pl.when(kt == pl.num_programs(3) - 1)
    def _():
        lp = lam_ref[...]
        lam = (jnp.exp(jnp.sum(lp[0:1] * lp[1:2], axis=-1, keepdims=True))
               - jnp.exp(jnp.sum(lp[2:3] * lp[3:4], axis=-1, keepdims=True)) + lam_init)
        for hh in range(hp):
            for part in range(per_map):
                a0, a1 = acc_ref[2 * hh * per_map + part], acc_ref[(2 * hh + 1) * per_map + part]
                o0 = a0[0:DIFF_DV, :] / a0[DIFF_DV:DIFF_DV + 1, :]
                o1 = a1[0:DIFF_DV, :] / a1[DIFF_DV:DIFF_DV + 1, :]
                d = (o0 - lam * o1).T
                ms = jnp.mean(d * d, axis=-1, keepdims=True)
                o_ref[0, part * col:(part + 1) * col, hh * LANES:(hh + 1) * LANES] = (
                    d * lax.rsqrt(ms + EPS) * gsub_ref[...] * (1.0 - lam_init)).astype(BF16)


def _diff_attention(q, k_bound, lam_p, gsub, kd, vt, kd_c, n_batch, n_tok, tq, tk, hp, lam_init):
    has_cache = kd_c is not None
    nk = kd.shape[2] // tk
    col = min(MAX_COL, tq)
    rows = hp * 2 * tq
    q_spec = pl.BlockSpec((1, tq, hp * LANES), lambda b, h, qi, kt: (b, qi, h))
    in_specs = [
        pl.BlockSpec(memory_space=pltpu.SMEM),
        q_spec,
        pl.BlockSpec((4, HEAD_DIM), lambda b, h, qi, kt: (0, 0)),
        pl.BlockSpec((1, DIFF_DV), lambda b, h, qi, kt: (0, 0)),
    ]
    in_specs += _kv_specs(2 * hp, hp, DIFF_VT_ROWS, tk, kd_c)
    args = [k_bound, q.reshape(n_batch, n_tok, BRANCH_W), lam_p, gsub, kd, vt]
    if has_cache:
        args.append(kd_c)
    out = pl.pallas_call(
        functools.partial(_diff_kernel, has_cache=has_cache, lam_init=lam_init, hp=hp),
        grid=(n_batch, DIFF_HEADS // hp, n_tok // tq, nk),
        in_specs=in_specs,
        out_specs=q_spec,
        out_shape=jax.ShapeDtypeStruct((n_batch, n_tok, BRANCH_W), BF16),
        scratch_shapes=[
            pltpu.VMEM((HEAD_DIM, rows), BF16),
            pltpu.VMEM((rows // col, 1, col), F32),
            pltpu.VMEM((rows // col, DIFF_VT_ROWS, col), F32),
            pltpu.SMEM((1,), jnp.int32),
        ],
        compiler_params=pltpu.CompilerParams(
            dimension_semantics=("parallel", "parallel", "parallel", "arbitrary"),
            vmem_limit_bytes=VMEM_LIMIT),
        name="diff_attention",
    )(*args)
    return out.reshape(n_batch * n_tok, BRANCH_W)


def _gla_kernel(*refs, has_s0, n_chunk, bg):
    if has_s0:
        qf, kf, vf, laf, qb, kb, vb, lab, s0_ref, of_ref, ob_ref, sfin_ref, st_ref = refs
    else:
        qf, kf, vf, laf, qb, kb, vb, lab, of_ref, ob_ref, sfin_ref, st_ref = refs
    i = pl.program_id(1)
    ck = GLA_CHUNK

    @pl.when(i == 0)
    def _():
        for bb in range(bg):
            for d in range(2):
                for hd in range(GLA_HEADS):
                    if has_s0:
                        s = s0_ref[bb, d, hd]
                        z = jnp.zeros_like(s)
                        padded = jnp.concatenate([s, z] if hd % 2 == 0 else [z, s], axis=0)
                        st_ref[bb, d, hd] = padded.T
                    else:
                        st_ref[bb, d, hd] = jnp.zeros((GLA_DV, LANES), F32)

    tb = qf.shape[1]
    r = lax.broadcasted_iota(jnp.int32, (tb, tb), 0)
    c = lax.broadcasted_iota(jnp.int32, (tb, tb), 1)
    lane = lax.broadcasted_iota(jnp.int32, (tb, LANES), 1)
    chunk_id = lax.broadcasted_iota(jnp.int32, (tb, GLA_W), 0) // ck
    zero_row = jnp.zeros((1, GLA_W), F32)
    streams = []
    for bb in range(bg):
        streams.append((bb, 0, qf.at[bb], kf.at[bb], vf.at[bb], laf.at[bb], of_ref.at[bb], c <= r))
        streams.append((bb, 1, qb.at[bb], kb.at[bb], vb.at[bb], lab.at[bb], ob_ref.at[bb], c >= r))
    nt = (((1,), (1,)), ((), ()))
    g_all = [sum(jnp.dot(tri.astype(F32).astype(BF16), part, preferred_element_type=F32)
                 for part in _bf16_terms(la_r[...]))
             for (_, _, _, _, _, la_r, _, tri) in streams]
    prepared = []
    for g, (bb, d, q_r, k_r, v_r, la_r, o_r, tri) in zip(g_all, streams):
        if d == 0:
            bounds = [zero_row] + [g[ck * j - 1:ck * j, :] for j in range(1, n_chunk)]
            g_end = g[tb - 1:tb, :]
        else:
            bounds = [g[ck * (j + 1):ck * (j + 1) + 1, :] for j in range(n_chunk - 1)] + [zero_row]
            g_end = g[0:1, :]
        b_rows = jnp.concatenate([jnp.broadcast_to(b, (ck, GLA_W)) for b in bounds], axis=0)
        q = q_r[...].astype(F32)
        k = k_r[...].astype(F32)
        q_dec = q * jnp.exp(g - b_rows)
        q_glob = q * jnp.exp(g)
        k_end = k * jnp.exp(g_end - g)
        k_rel = []
        for j in range(n_chunk):
            reach = (chunk_id <= j) if d == 0 else (chunk_id >= j)
            k_rel.append((k * jnp.exp(jnp.where(reach, bounds[j] - g, 0.0))).astype(BF16))
        prepared.append((q_dec, q_glob, k_end, k_rel, jnp.exp(g_end), v_r[...]))

    partial = []
    for (bb, d, *_), (q_dec, q_glob, k_end, k_rel, decay, v) in zip(streams, prepared):
        per_head = []
        for hd in range(GLA_HEADS):
            pair = slice((hd // 2) * LANES, (hd // 2 + 1) * LANES)
            hv = slice(hd * GLA_DV, (hd + 1) * GLA_DV)
            keep = (lane < GLA_DK) if hd % 2 == 0 else (lane >= GLA_DK)
            qd_m = jnp.where(keep, q_dec[:, pair], 0.0).astype(BF16)
            qg_m = jnp.where(keep, q_glob[:, pair], 0.0).astype(BF16)
            ke_m = jnp.where(keep, k_end[:, pair], 0.0).astype(BF16)
            a_rows = [lax.dot_general(qd_m[ck * j:ck * (j + 1), :], k_rel[j][:, pair], nt,
                                      preferred_element_type=F32) for j in range(n_chunk)]
            s_t = st_ref[bb, d, hd]
            o_state = lax.dot_general(qg_m, s_t.astype(BF16), nt, preferred_element_type=F32)
            ds_t = lax.dot_general(v[:, hv], ke_m, (((0,), (0,)), ((), ())), preferred_element_type=F32)
            st_ref[bb, d, hd] = s_t * decay[:, pair] + ds_t
            per_head.append((a_rows, o_state))
        partial.append(per_head)

    for (bb, d, _, _, _, _, o_r, tri), prep, per_head in zip(streams, prepared, partial):
        v = prep[5]
        for hd in range(GLA_HEADS):
            hv = slice(hd * GLA_DV, (hd + 1) * GLA_DV)
            a_rows, o_state = per_head[hd]
            a = jnp.where(tri, jnp.concatenate(a_rows, axis=0), 0.0).astype(BF16)
            o_r[:, hv] = (jnp.dot(a, v[:, hv], preferred_element_type=F32) + o_state).astype(BF16)

    @pl.when(i == pl.num_programs(1) - 1)
    def _():
        for bb in range(bg):
            for d in range(2):
                for hd in range(GLA_HEADS):
                    t = st_ref[bb, d, hd].T
                    sfin_ref[bb, d, hd] = t[(hd % 2) * GLA_DK:(hd % 2 + 1) * GLA_DK, :]


def _gla(lq, lk, lv, la, s0, n_batch, n_tok, tb, bg):
    nb = n_tok // tb
    has_s0 = s0 is not None
    as3d = lambda x: x.reshape(n_batch, n_tok, x.shape[-1])
    fwd = lambda b, i: (b, i, 0)
    bwd = lambda b, i: (b, nb - 1 - i, 0)
    bwd_la = lambda b, i: (b, nb - 1 - i, 1)
    in_specs = [
        pl.BlockSpec((bg, tb, GLA_W), fwd), pl.BlockSpec((bg, tb, GLA_W), fwd),
        pl.BlockSpec((bg, tb, BRANCH_W), fwd), pl.BlockSpec((bg, tb, GLA_W), fwd),
        pl.BlockSpec((bg, tb, GLA_W), bwd), pl.BlockSpec((bg, tb, GLA_W), bwd),
        pl.BlockSpec((bg, tb, BRANCH_W), bwd), pl.BlockSpec((bg, tb, GLA_W), bwd_la),
    ]
    args = [as3d(lq), as3d(lk), as3d(lv), as3d(la)] * 2
    state_spec = pl.BlockSpec((bg, 2, GLA_HEADS, GLA_DK, GLA_DV), lambda b, i: (b, 0, 0, 0, 0))
    if has_s0:
        in_specs.append(state_spec)
        args.append(s0)
    o_f, o_b, s_fin = pl.pallas_call(
        functools.partial(_gla_kernel, has_s0=has_s0, n_chunk=tb // GLA_CHUNK, bg=bg),
        grid=(n_batch // bg, nb),
        in_specs=in_specs,
        out_specs=[pl.BlockSpec((bg, tb, BRANCH_W), fwd), pl.BlockSpec((bg, tb, BRANCH_W), bwd), state_spec],
        out_shape=[
            jax.ShapeDtypeStruct((n_batch, n_tok, BRANCH_W), BF16),
            jax.ShapeDtypeStruct((n_batch, n_tok, BRANCH_W), BF16),
            jax.ShapeDtypeStruct((n_batch, 2, GLA_HEADS, GLA_DK, GLA_DV), F32),
        ],
        scratch_shapes=[pltpu.VMEM((bg, 2, GLA_HEADS, GLA_DV, LANES), F32)],
        compiler_params=pltpu.CompilerParams(
            dimension_semantics=("parallel", "arbitrary"), vmem_limit_bytes=VMEM_LIMIT),
        name="gla",
    )(*args)
    flat = lambda x: x.reshape(n_batch * n_tok, BRANCH_W)
    return flat(o_f), flat(o_b), s_fin


FFN_CHUNK = 256


def _merge_ffn_kernel(x_ref, mod_ref, og_ref, of_ref, ob_ref, lgo_ref, od_ref, gate_ref,
                      gout_ref, wb_ref, wo_ref, g2_ref, wi_ref, wd_ref, y_ref):
    o_gla = of_ref[...].astype(F32) + ob_ref[...].astype(F32)
    gla_parts = []
    for hd in range(GLA_HEADS):
        blk = o_gla[:, hd * GLA_DV:(hd + 1) * GLA_DV]
        ms = jnp.mean(blk * blk, axis=-1, keepdims=True)
        gla_parts.append(blk * lax.rsqrt(ms + EPS) * gout_ref[...])
    gla = jnp.concatenate(gla_parts, axis=1) * _silu(lgo_ref[...].astype(F32))
    branches = (og_ref[...], gla.astype(BF16), od_ref[...])
    mixed = None
    for b, ob in enumerate(branches):
        y = jnp.dot(ob, wb_ref[b], preferred_element_type=F32)
        y = y * gate_ref[:, b * D_MODEL:(b + 1) * D_MODEL].astype(F32)
        mixed = y if mixed is None else mixed + y
    out = jnp.dot(mixed.astype(BF16), wo_ref[...], preferred_element_type=F32)
    x = x_ref[...] + mod_ref[0, 2:3, :] * out

    ms = jnp.mean(x * x, axis=-1, keepdims=True)
    h = x * lax.rsqrt(ms + EPS) * g2_ref[...]
    h = h * (1.0 + mod_ref[0, 4:5, :]) + mod_ref[0, 3:4, :]
    hb = h.astype(BF16)

    def up_proj(c0):
        a = jnp.dot(hb, wi_ref[:, c0:c0 + FFN_CHUNK], preferred_element_type=F32)
        u = jnp.dot(hb, wi_ref[:, FFN_HIDDEN + c0:FFN_HIDDEN + c0 + FFN_CHUNK], preferred_element_type=F32)
        return a, u

    acc = None
    chunks = list(range(0, FFN_HIDDEN, FFN_CHUNK))
    nxt = up_proj(chunks[0])
    for i, c0 in enumerate(chunks):
        a, u = nxt
        if i + 1 < len(chunks):
            nxt = up_proj(chunks[i + 1])
        act = (_silu(a) * u).astype(BF16)
        part = jnp.dot(act, wd_ref[c0:c0 + FFN_CHUNK, :], preferred_element_type=F32)
        acc = part if acc is None else acc + part
    y_ref[...] = x + mod_ref[0, 5:6, :] * acc


def _merge_ffn(x2d, mod, og, o_f, o_b, lgo, od, gates, lw, sw, l, tiles_per_mod, tm):
    t_total = x2d.shape[0]
    tile = lambda c: pl.BlockSpec((tm, c), lambda i: (i, 0))
    return pl.pallas_call(
        _merge_ffn_kernel,
        grid=(t_total // tm,),
        in_specs=[
            tile(D_MODEL),
            pl.BlockSpec((1, 6, D_MODEL), lambda i: (i // tiles_per_mod, 0, 0)),
            tile(BRANCH_W), tile(BRANCH_W), tile(BRANCH_W), tile(BRANCH_W), tile(BRANCH_W),
            tile(N_BRANCH * D_MODEL),
            _full_spec((1, GLA_DV)),
            _layer_spec((N_BRANCH, BRANCH_W, D_MODEL), l),
            _layer_spec((D_MODEL, D_MODEL), l),
            _full_spec((1, D_MODEL)),
            _layer_spec((D_MODEL, 2 * FFN_HIDDEN), l),
            _layer_spec((FFN_HIDDEN, D_MODEL), l),
        ],
        out_specs=tile(D_MODEL),
        out_shape=jax.ShapeDtypeStruct((t_total, D_MODEL), F32),
        compiler_params=pltpu.CompilerParams(
            dimension_semantics=("parallel",), vmem_limit_bytes=VMEM_LIMIT),
        name="merge_ffn",
    )(x2d, mod, og, o_f, o_b, lgo, od, gates, lw["gout"], sw["w_branch"], sw["w_out"],
      lw["norm2"], sw["w_ffn_in"], sw["w_ffn_out"])


def _shared_weights(p):
    return {
        "w_in_t": jnp.swapaxes(p["w_in"], 1, 2).astype(BF16),
        "w_branch": p["w_branch"].astype(BF16), "w_out": p["w_out"].astype(BF16),
        "w_ffn_in": p["w_ffn_in"].astype(BF16), "w_ffn_out": p["w_ffn_out"].astype(BF16),
    }


def _layer_weights(l, p):
    scale = HEAD_DIM ** -0.5 * math.log2(math.e)
    head_bound = lambda g, s: HEAD_DIM * (NORM_MARGIN * s) ** 2 * jnp.max(g * g)
    norm_bounds = lambda gq_, gk_: jnp.stack([head_bound(gq_, scale), head_bound(gk_, 1.0)])
    gq_row = jnp.concatenate([jnp.tile(p["gqa_q_norm"][l], GQA_HEADS),
                              jnp.tile(p["diff_q_norm"][l], 2 * DIFF_HEADS)]) * scale
    gk_row = jnp.concatenate([jnp.tile(p["gqa_k_norm"][l], GQA_KV_HEADS),
                              jnp.tile(p["diff_k_norm"][l], 2 * DIFF_HEADS)])
    idx = jnp.arange(2 * LANES) // HEAD_DIM
    gmat = jnp.where(idx[:, None] == idx[None, :], 1.0 / HEAD_DIM, 0.0).astype(BF16)
    up = jnp.zeros((LANES, 2 * GLA_W), F32)
    up = up.at[0:GLA_RANK, 0:GLA_W].set(p["gla_alpha_up"][l, 0])
    up = up.at[GLA_RANK:2 * GLA_RANK, GLA_W:].set(p["gla_alpha_up"][l, 1])
    ub = p["gla_alpha_bias"][l].reshape(1, 2 * GLA_W)
    up_hi = up.astype(BF16)
    up = jnp.stack([up_hi, (up - up_hi.astype(F32)).astype(BF16)])
    return {
        "gq": gq_row.reshape(1, 2 * N_QG), "gk": gk_row.reshape(1, LANES + BRANCH_W), "gmat": gmat,
        "up": up, "ub": ub,
        "kb_g": norm_bounds(p["gqa_q_norm"][l], p["gqa_k_norm"][l]),
        "kb_d": norm_bounds(p["diff_q_norm"][l], p["diff_k_norm"][l]),
        "norm1": p["norm1"][l].reshape(1, D_MODEL), "norm2": p["norm2"][l].reshape(1, D_MODEL),
        "gout": p["gla_out_norm"][l].reshape(1, GLA_DV),
        "gsub": p["diff_sub_norm"][l].reshape(1, DIFF_DV),
        "lam": p["diff_lambda"][l],
    }


def _rope_tables(n_tokens):
    n_rows = n_tokens // GRID_W
    row = jnp.repeat(jnp.arange(n_rows, dtype=F32), GRID_W)
    col = jnp.tile(jnp.arange(GRID_W, dtype=F32), n_rows)
    n_freq = HEAD_DIM // 4
    freqs = ROPE_THETA ** (-jnp.arange(n_freq, dtype=F32) / n_freq)
    ar, ac = row[:, None] * freqs, col[:, None] * freqs
    cos = jnp.concatenate([jnp.cos(ar), jnp.cos(ar), jnp.cos(ac), jnp.cos(ac)], axis=1)
    sin = jnp.concatenate([-jnp.sin(ar), jnp.sin(ar), -jnp.sin(ac), jnp.sin(ac)], axis=1)
    return jnp.tile(cos, (1, LANES // HEAD_DIM)), jnp.tile(sin, (1, LANES // HEAD_DIM))


def _with_ones_rows(v_t):
    lead = v_t.shape[:-2]
    s = v_t.shape[-1]
    ones = jnp.ones(lead + (1, s), v_t.dtype)
    zeros = jnp.zeros(lead + (BF16_ROWS - 1, s), v_t.dtype)
    return jnp.concatenate([v_t, ones, zeros], axis=-2).astype(BF16)


def _cache_layouts(l, cache_gqa_k, cache_gqa_v, cache_diff_k, cache_diff_v):
    b = cache_gqa_k.shape[0]
    gk = jnp.transpose(cache_gqa_k[:, l], (0, 2, 1, 3))
    kd_g = gk.astype(BF16)
    vt_g = _with_ones_rows(jnp.transpose(cache_gqa_v[:, l], (0, 2, 3, 1)))
    dk = jnp.transpose(cache_diff_k[:, l], (0, 2, 3, 1, 4)).reshape(b, 2 * DIFF_HEADS, PAST_LEN, HEAD_DIM)
    kd_d = dk.astype(BF16)
    vt_d = _with_ones_rows(jnp.transpose(cache_diff_v[:, l], (0, 2, 3, 1)))
    return kd_g, vt_g, kd_d, vt_d


def _run_layer(x2d, mod, lw, sw, l, n_batch, n_tok, rope_tabs, ctx, emit_cache, cfg):
    tiles_per_mod = lambda tm: (n_tok // tm) if mod.shape[0] > 1 else (n_batch * n_tok // tm)
    if ctx is None:
        cache_kv = kd_gc = kd_dc = s0 = None
    else:
        kd_gc, vt_gc, kd_dc, vt_dc, s0 = ctx
        cache_kv = (kd_gc, kd_dc, vt_gc, vt_dc)
    outs = _in_projection(x2d, mod, lw, sw, l, n_batch, n_tok, cache_kv, rope_tabs, emit_cache, cfg["tm_in"])
    qg, qd, kdg, kdd, vtg, vtd, lq, lk, lv, lgo, la, gates = outs[:12]
    og = _gqa_attention(qg, lw["kb_g"], kdg, vtg, kd_gc, n_batch, n_tok, cfg["tq_gqa"], cfg["tk"],
                        cfg["hp_gqa"])
    lam_init = 0.8 - 0.6 * math.exp(-0.3 * l)
    od = _diff_attention(qd, lw["kb_d"], lw["lam"], lw["gsub"], kdd, vtd, kd_dc, n_batch, n_tok,
                         cfg["tq_diff"], cfg["tk"], cfg["hp_diff"], lam_init)
    o_f, o_b, s_fin = _gla(lq, lk, lv, la, s0, n_batch, n_tok, cfg["tb_gla"], cfg["bg_gla"])
    x2d = _merge_ffn(x2d, mod, og, o_f, o_b, lgo, od, gates, lw, sw, l, tiles_per_mod(cfg["tm"]), cfg["tm"])
    cache = tuple(outs[12:]) + (s_fin,) if emit_cache else None
    return x2d, cache


PROMPT_CFG = dict(tm_in=512, tq_gqa=256, tq_diff=256, tk=256, hp_gqa=2, hp_diff=4, tb_gla=256, bg_gla=2, tm=512)
SAMPLE_CFG = dict(tm_in=512, tq_gqa=1024, tq_diff=2048, tk=1536, hp_gqa=1, hp_diff=1, tb_gla=256, bg_gla=2, tm=512)


def kernel(x_prompt, x_sample, c, cache_gqa_k, cache_gqa_v, state_gla, cache_diff_k, cache_diff_v, c_ctx, w_mod, b_mod, norm1, norm2, w_in, gqa_q_norm, gqa_k_norm, gla_alpha_up, gla_alpha_bias, gla_out_norm, diff_q_norm, diff_k_norm, diff_lambda, diff_sub_norm, w_branch, w_out, w_ffn_in, w_ffn_out):
    p = {
        "norm1": norm1, "norm2": norm2, "w_in": w_in, "gqa_q_norm": gqa_q_norm, "gqa_k_norm": gqa_k_norm,
        "gla_alpha_up": gla_alpha_up, "gla_alpha_bias": gla_alpha_bias, "gla_out_norm": gla_out_norm,
        "diff_q_norm": diff_q_norm, "diff_k_norm": diff_k_norm, "diff_lambda": diff_lambda,
        "diff_sub_norm": diff_sub_norm, "w_branch": w_branch, "w_out": w_out,
        "w_ffn_in": w_ffn_in, "w_ffn_out": w_ffn_out,
    }
    n_ctx_b, n_ctx = x_prompt.shape[:2]
    n_lat_b, n_lat = x_sample.shape[:2]
    cond_rows = jnp.concatenate(
        [c_ctx[None, :], c, jnp.zeros((8 - 1 - n_lat_b, D_MODEL), F32)], axis=0)
    mod_all = _modulation(cond_rows, w_mod, b_mod)
    weights = [_layer_weights(l, p) for l in range(DEPTH)]
    shared = _shared_weights(p)

    y = x_prompt.reshape(n_ctx_b * n_ctx, D_MODEL)
    caches = []
    for l in range(DEPTH):
        mod = mod_all[l, 0:1].reshape(1, 6, D_MODEL)
        y, cache = _run_layer(y, mod, weights[l], shared, l, n_ctx_b, n_ctx, None, None, True, PROMPT_CFG)
        caches.append(cache)
    y_prompt = y.reshape(n_ctx_b, n_ctx, D_MODEL)
    stack = lambda j, shape: jnp.stack([cc[j].reshape(shape) for cc in caches], axis=1)
    new_gqa_k = stack(0, (n_ctx_b, n_ctx, GQA_KV_HEADS, HEAD_DIM))
    new_diff_k = stack(1, (n_ctx_b, n_ctx, DIFF_HEADS, 2, HEAD_DIM))
    new_gqa_v = stack(2, (n_ctx_b, n_ctx, GQA_KV_HEADS, HEAD_DIM))
    new_diff_v = stack(3, (n_ctx_b, n_ctx, DIFF_HEADS, DIFF_DV))
    new_state_gla = jnp.stack([cc[4] for cc in caches], axis=1)

    rope_tabs = _rope_tables(n_lat)
    y = x_sample.reshape(n_lat_b * n_lat, D_MODEL)
    for l in range(DEPTH):
        mod = mod_all[l, 1:1 + n_lat_b].reshape(n_lat_b, 6, D_MODEL)
        ctx = _cache_layouts(l, cache_gqa_k, cache_gqa_v, cache_diff_k, cache_diff_v) + (state_gla[:, l],)
        y, _ = _run_layer(y, mod, weights[l], shared, l, n_lat_b, n_lat, rope_tabs, ctx, False, SAMPLE_CFG)
    y_sample = y.reshape(n_lat_b, n_lat, D_MODEL)
    return (y_prompt, y_sample, new_gqa_k, new_gqa_v, new_state_gla, new_diff_k, new_diff_v)
```

```python
import functools
import math

import jax
import jax.numpy as jnp
from jax import lax
from jax.experimental import pallas as pl
from jax.experimental.pallas import tpu as pltpu

D_MODEL = 1024
DEPTH = 2
PAST_LEN = 512
GRID_W = 64
HEAD_DIM = 64
GQA_HEADS = 8
GQA_KV_HEADS = 2
GQA_GROUP = GQA_HEADS // GQA_KV_HEADS
GLA_HEADS = 4
GLA_DK = 64
GLA_DV = 128
GLA_RANK = 16
GLA_TAU = 16.0
GLA_CHUNK = 64
DIFF_HEADS = 4
DIFF_DV = 2 * HEAD_DIM
N_BRANCH = 3
BRANCH_W = 512
FFN_HIDDEN = ((8 * D_MODEL + 3 * 256 - 1) // (3 * 256)) * 256
ROPE_THETA = 10000.0
EPS = 1e-6

LANES = 128
BF16_ROWS = 16
VMEM_LIMIT = 56 * 1024 * 1024

F32 = jnp.float32
BF16 = jnp.bfloat16

R_GQ = 0
R_KV = R_GQ + GQA_HEADS * HEAD_DIM
R_GLA = R_KV + 2 * GQA_KV_HEADS * HEAD_DIM
R_LR = R_GLA + 2 * GLA_HEADS * GLA_DK + GLA_HEADS * GLA_DV
R_LGO = R_LR + 2 * GLA_RANK
R_DQ = R_LGO + GLA_HEADS * GLA_DV
R_DK = R_DQ + DIFF_HEADS * 2 * HEAD_DIM
R_DV = R_DK + DIFF_HEADS * 2 * HEAD_DIM
R_MG = R_DV + DIFF_HEADS * DIFF_DV
N_IN = R_MG + N_BRANCH * D_MODEL
NT_DIMS = (((1,), (1,)), ((), ()))
N_QG = GQA_HEADS * HEAD_DIM
GLA_W = GLA_HEADS * GLA_DK
GQA_VT_ROWS = HEAD_DIM + BF16_ROWS
DIFF_VT_ROWS = DIFF_DV + BF16_ROWS


def _sigmoid(x):
    return 1.0 / (1.0 + jnp.exp(-x))


def _silu(x):
    return x * _sigmoid(x)


def _bf16_terms(x):
    hi = x.astype(BF16)
    r1 = x - hi.astype(F32)
    mid = r1.astype(BF16)
    lo = (r1 - mid.astype(F32)).astype(BF16)
    return hi, mid, lo


def _full_spec(shape):
    n = len(shape)
    return pl.BlockSpec(shape, lambda *_: (0,) * n, pipeline_mode=pl.Buffered(1))


def _layer_spec(shape, l):
    n = len(shape)
    return pl.BlockSpec((None,) + tuple(shape), lambda *_: (l,) + (0,) * n, pipeline_mode=pl.Buffered(1))


def _mod_kernel(cond_ref, w_ref, b_ref, o_ref):
    s_hi, s_mid, _ = _bf16_terms(_silu(cond_ref[...]))
    lhs = jnp.concatenate([s_hi, s_mid], axis=0)
    w = w_ref[0]
    w_hi = w.astype(BF16)
    w_mid = (w - w_hi.astype(F32)).astype(BF16)
    r = jnp.dot(lhs, w_hi, preferred_element_type=F32) + jnp.dot(lhs, w_mid, preferred_element_type=F32)
    n = cond_ref.shape[0]
    o_ref[0] = r[0:n] + r[n:2 * n] + b_ref[0]


def _modulation(cond_rows, w_mod, b_mod):
    tn = 1536
    n_out = 6 * D_MODEL
    return pl.pallas_call(
        _mod_kernel,
        grid=(DEPTH, n_out // tn),
        in_specs=[
            pl.BlockSpec((8, D_MODEL), lambda l, j: (0, 0)),
            pl.BlockSpec((1, D_MODEL, tn), lambda l, j: (l, 0, j)),
            pl.BlockSpec((1, 1, tn), lambda l, j: (l, 0, j)),
        ],
        out_specs=pl.BlockSpec((1, 8, tn), lambda l, j: (l, 0, j)),
        out_shape=jax.ShapeDtypeStruct((DEPTH, 8, n_out), F32),
        compiler_params=pltpu.CompilerParams(
            dimension_semantics=("parallel", "parallel"), vmem_limit_bytes=VMEM_LIMIT),
        name="modulation",
    )(cond_rows, w_mod, b_mod.reshape(DEPTH, 1, n_out))


def _head_norm(z, gain, gmat):
    outs = []
    n = z.shape[1]
    for c0 in range(0, n, 2 * LANES):
        w = min(2 * LANES, n - c0)
        zz = z[:, c0:c0 + w]
        ms = jnp.dot((zz * zz).astype(BF16), gmat[0:w, 0:w], preferred_element_type=F32)
        outs.append(zz * lax.rsqrt(ms + EPS) * gain[:, c0:c0 + w])
    return outs


def _rope_block(zb, cos, sin, first):
    partner = jnp.where(first, pltpu.roll(zb, LANES - 16, 1), pltpu.roll(zb, 16, 1))
    return zb * cos + partner * sin


N_IN_BASE = 9


def _in_kernel(*refs, rope, emit_cache, prev_layers, tiles_per_batch, cache_tiles):
    if not cache_tiles:
        _in_tile(refs, rope, emit_cache, prev_layers)
        return
    kcg_ref, kcd_ref, vcg_ref, vcd_ref = refs[:4]
    j = pl.program_id(1)

    @pl.when(j < tiles_per_batch)
    def _():
        _in_tile(refs[4:], rope, emit_cache, prev_layers)

    @pl.when(j >= tiles_per_batch)
    def _():
        n_in = 4 + N_IN_BASE + (2 if rope else 0)
        kdg_ref, kdd_ref, vtg_ref, vtd_ref = refs[n_in + 2:n_in + 6]
        kdg_ref[...] = kcg_ref[...]
        kdd_ref[...] = kcd_ref[...]
        vtg_ref[...] = vcg_ref[...]
        vtd_ref[...] = vcd_ref[...]


def _in_tile(refs, rope, emit_cache, prev_layers):
    it = iter(refs)
    (x_ref, mod_ref, g1_ref, wt_ref, gq_ref, gk_ref, gmat_ref, up_ref, ub_ref) = (
        next(it) for _ in range(N_IN_BASE))
    if rope:
        cos_ref, sin_ref = next(it), next(it)
    prev_refs = [[next(it) for _ in range(4)] for _ in range(prev_layers or 0)]
    (qg_ref, qd_ref, kdg_ref, kdd_ref, vtg_ref, vtd_ref,
     lq_ref, lk_ref, lv_ref, lgo_ref, la_ref, gate_ref) = (next(it) for _ in range(12))
    if emit_cache:
        ck_g_ref, ck_d_ref, cv_g_ref, cv_d_ref = (next(it) for _ in range(4))

    tm = x_ref.shape[0]
    bpt = kdg_ref.shape[0]
    nt = tm // bpt
    x = x_ref[...]
    ms = jnp.mean(x * x, axis=-1, keepdims=True)
    h = x * lax.rsqrt(ms + EPS) * g1_ref[...]
    h = h * (1.0 + mod_ref[0, 1:2, :]) + mod_ref[0, 0:1, :]
    hb = h.astype(BF16)

    def proj(r0, n):
        return lax.dot_general(hb, wt_ref[r0:r0 + n, :], NT_DIMS, preferred_element_type=F32)

    lane = lax.broadcasted_iota(jnp.int32, (tm, LANES), 1)
    first16 = (lane % 32) < 16
    gmat = gmat_ref[...]
    if rope:
        cos, sin = cos_ref[...], sin_ref[...]

    def blocks128(parts):
        out = []
        for p in parts:
            for c in range(0, p.shape[1], LANES):
                out.append(p[:, c:c + LANES])
        return out

    def store_queries(z, gain, dst):
        for j, zb in enumerate(blocks128(_head_norm(z, gain, gmat))):
            if rope:
                zb = _rope_block(zb, cos, sin, first16)
            dst[:, j * LANES:(j + 1) * LANES] = zb.astype(BF16)

    def store_cache(dst, c0, value):
        w = value.shape[1]
        if prev_layers is None:
            dst[:, c0:c0 + w] = value
        else:
            for bb in range(bpt):
                dst[bb, prev_layers, :, c0:c0 + w] = value[bb * nt:(bb + 1) * nt, :]

    def store_keys(z, gain, dst, cache_dst):
        for j, zb in enumerate(blocks128(_head_norm(z, gain, gmat))):
            if rope:
                zb = _rope_block(zb, cos, sin, first16)
            if emit_cache:
                store_cache(cache_dst, j * LANES, zb)
            sw = pltpu.roll(zb, HEAD_DIM, 1)
            for bb in range(bpt):
                dst[bb, 2 * j] = zb[bb * nt:(bb + 1) * nt, 0:HEAD_DIM].astype(BF16)
                dst[bb, 2 * j + 1] = sw[bb * nt:(bb + 1) * nt, 0:HEAD_DIM].astype(BF16)

    z_gq = proj(R_GQ, N_QG)
    z_dq = proj(R_DQ, N_QG)
    store_queries(z_gq, gq_ref[:, 0:N_QG], qg_ref)
    z_kv = proj(R_KV, 2 * LANES)
    store_queries(z_dq, gq_ref[:, N_QG:2 * N_QG], qd_ref)
    z_dk = proj(R_DK, BRANCH_W)
    store_keys(z_kv[:, 0:LANES], gk_ref[:, 0:LANES], kdg_ref, ck_g_ref if emit_cache else None)
    z_dv = proj(R_DV, BRANCH_W)
    store_keys(z_dk, gk_ref[:, LANES:LANES + BRANCH_W], kdd_ref, ck_d_ref if emit_cache else None)
    zl = proj(R_GLA, R_LR - R_GLA)

    ones_rows = jnp.where(lax.broadcasted_iota(jnp.int32, (BF16_ROWS, nt), 0) == 0, 1.0, 0.0).astype(BF16)
    z_gv = z_kv[:, LANES:2 * LANES]
    if emit_cache:
        store_cache(cv_g_ref, 0, z_gv)
        store_cache(cv_d_ref, 0, z_dv)
        for layer, layer_refs in enumerate(prev_refs):
            for dst, src in zip((ck_g_ref, ck_d_ref, cv_g_ref, cv_d_ref), layer_refs):
                for bb in range(bpt):
                    dst[bb, layer] = src[bb * nt:(bb + 1) * nt, :]
    for bb in range(bpt):
        rows = slice(bb * nt, (bb + 1) * nt)
        vt = z_gv[rows, :].T
        for hh in range(GQA_KV_HEADS):
            vtg_ref[bb, hh, 0:HEAD_DIM, :] = vt[hh * HEAD_DIM:(hh + 1) * HEAD_DIM, :].astype(BF16)
            vtg_ref[bb, hh, HEAD_DIM:GQA_VT_ROWS, :] = ones_rows
        for hh in range(DIFF_HEADS):
            vt = z_dv[rows, hh * LANES:(hh + 1) * LANES].T
            vtd_ref[bb, hh, 0:DIFF_DV, :] = vt.astype(BF16)
            vtd_ref[bb, hh, DIFF_DV:DIFF_VT_ROWS, :] = ones_rows

    z_lgo = proj(R_LGO, BRANCH_W)
    lr = proj(R_LR, LANES)
    o = 0
    lq_ref[...] = (zl[:, o:o + GLA_W] * (GLA_DK ** -0.5)).astype(BF16)
    o += GLA_W
    lk_ref[...] = zl[:, o:o + GLA_W].astype(BF16)
    o += GLA_W
    lv_ref[...] = zl[:, o:o + BRANCH_W].astype(BF16)
    o += BRANCH_W
    lgo_ref[...] = z_lgo.astype(BF16)
    lr_hi, lr_mid, _ = _bf16_terms(lr)
    z = (jnp.dot(lr_hi, up_ref[0], preferred_element_type=F32)
         + jnp.dot(lr_mid, up_ref[0], preferred_element_type=F32)
         + jnp.dot(lr_hi, up_ref[1], preferred_element_type=F32)) + ub_ref[...]
    log_sig = jnp.minimum(z, 0.0) - jnp.log(1.0 + jnp.exp(-jnp.abs(z)))
    la_ref[...] = log_sig * (1.0 / GLA_TAU)

    for b in range(N_BRANCH):
        gate_ref[:, b * D_MODEL:(b + 1) * D_MODEL] = _sigmoid(
            proj(R_MG + b * D_MODEL, D_MODEL)).astype(BF16)


def _in_projection(x2d, mod, lw, sw, l, n_batch, n_tok, cache_kv, rope_tabs, emit_cache, prev_caches, tm):
    n_cache = 0 if cache_kv is None else cache_kv[0].shape[2]
    assert n_cache % tm == 0
    n_keys = n_tok + n_cache
    t_total = n_batch * n_tok
    bpt = max(1, tm // n_tok)
    assert bpt == 1 or (mod.shape[0] == 1 and n_cache == 0 and tm == bpt * n_tok)
    nt = tm // bpt
    tpb = n_tok // nt
    cache_tiles = n_cache // tm
    rope = rope_tabs is not None
    own = lambda j: jnp.minimum(j, tpb - 1)
    tile = lambda c: pl.BlockSpec((tm, c), lambda b, j: (b * tpb + own(j), 0))
    in_specs, args = [], []
    if cache_tiles:
        past = lambda j: jnp.maximum(j - tpb, 0)
        in_specs += [
            pl.BlockSpec((1, GQA_KV_HEADS, tm, HEAD_DIM), lambda b, j: (b, 0, past(j), 0)),
            pl.BlockSpec((1, 2 * DIFF_HEADS, tm, HEAD_DIM), lambda b, j: (b, 0, past(j), 0)),
            pl.BlockSpec((1, GQA_KV_HEADS, GQA_VT_ROWS, tm), lambda b, j: (b, 0, 0, past(j))),
            pl.BlockSpec((1, DIFF_HEADS, DIFF_VT_ROWS, tm), lambda b, j: (b, 0, 0, past(j))),
        ]
        args += list(cache_kv)
    in_specs += [
        tile(D_MODEL),
        pl.BlockSpec((1, 6, D_MODEL), lambda b, j: (b if mod.shape[0] > 1 else 0, 0, 0)),
        _full_spec((1, D_MODEL)),
        _layer_spec((N_IN, D_MODEL), l),
        _full_spec((1, 2 * N_QG)),
        _full_spec((1, LANES + BRANCH_W)),
        _full_spec((2 * LANES, 2 * LANES)),
        _full_spec((2, LANES, 2 * GLA_W)),
        _full_spec((1, 2 * GLA_W)),
    ]
    args += [x2d, mod, lw["norm1"], sw["w_in_t"], lw["gq"], lw["gk"], lw["gmat"], lw["up"], lw["ub"]]
    if rope:
        in_specs += [pl.BlockSpec((tm, LANES), lambda b, j: (own(j), 0))] * 2
        args += list(rope_tabs)
    sd = jax.ShapeDtypeStruct
    out_shape = [
        sd((t_total, BRANCH_W), BF16), sd((t_total, BRANCH_W), BF16),
        sd((n_batch, GQA_KV_HEADS, n_keys, HEAD_DIM), BF16), sd((n_batch, 2 * DIFF_HEADS, n_keys, HEAD_DIM), BF16),
        sd((n_batch, GQA_KV_HEADS, GQA_VT_ROWS, n_keys), BF16), sd((n_batch, DIFF_HEADS, DIFF_VT_ROWS, n_keys), BF16),
        sd((t_total, GLA_W), BF16), sd((t_total, GLA_W), BF16),
        sd((t_total, BRANCH_W), BF16), sd((t_total, BRANCH_W), BF16),
        sd((t_total, 2 * GLA_W), F32), sd((t_total, N_BRANCH * D_MODEL), BF16),
    ]
    out_specs = [
        tile(BRANCH_W), tile(BRANCH_W),
        pl.BlockSpec((bpt, GQA_KV_HEADS, nt, HEAD_DIM), lambda b, j: (b, 0, j, 0)),
        pl.BlockSpec((bpt, 2 * DIFF_HEADS, nt, HEAD_DIM), lambda b, j: (b, 0, j, 0)),
        pl.BlockSpec((bpt, GQA_KV_HEADS, GQA_VT_ROWS, nt), lambda b, j: (b, 0, 0, j)),
        pl.BlockSpec((bpt, DIFF_HEADS, DIFF_VT_ROWS, nt), lambda b, j: (b, 0, 0, j)),
        tile(GLA_W), tile(GLA_W), tile(BRANCH_W), tile(BRANCH_W), tile(2 * GLA_W), tile(N_BRANCH * D_MODEL),
    ]
    cache_widths = (LANES, BRANCH_W, LANES, BRANCH_W)
    prev_layers = None
    if emit_cache and prev_caches is None:
        out_shape += [sd((t_total, w), F32) for w in cache_widths]
        out_specs += [tile(w) for w in cache_widths]
    elif emit_cache:
        prev_layers = len(prev_caches)
        n_layers = prev_layers + 1
        for layer_arrays in prev_caches:
            in_specs += [tile(w) for w in cache_widths]
            args += list(layer_arrays)
        out_shape += [sd((n_batch, n_layers, n_tok, w), F32) for w in cache_widths]
        out_specs += [pl.BlockSpec((bpt, n_layers, nt, w), lambda b, j: (b, 0, j, 0)) for w in cache_widths]
    return pl.pallas_call(
        functools.partial(_in_kernel, rope=rope, emit_cache=emit_cache, prev_layers=prev_layers,
                          tiles_per_batch=tpb, cache_tiles=cache_tiles),
        grid=(n_batch // bpt, tpb + cache_tiles),
        in_specs=in_specs,
        out_specs=out_specs,
        out_shape=out_shape,
        compiler_params=pltpu.CompilerParams(
            dimension_semantics=("parallel", "arbitrary"), vmem_limit_bytes=VMEM_LIMIT),
        name="in_projection",
    )(*args)


MAX_COL = 4 * LANES
SCORE_BOUND = 45.0
NORM_MARGIN = 1.05


def _head_queries_t(q_blk, low):
    t = q_blk.astype(F32).T
    return (t[0:HEAD_DIM, :] if low else t[HEAD_DIM:2 * HEAD_DIM, :]).astype(BF16)


def _key_norm2(kd):
    kf = kd.astype(F32)
    return jnp.max(jnp.sum(kf * kf, axis=1, keepdims=True))


def _needs_no_stabiliser(q_norm2, k_norm2):
    return (q_norm2 * k_norm2 <= SCORE_BOUND * SCORE_BOUND).astype(jnp.int32)


KEY_SUB = 512


def _score_units(k_ref, heads_cols):
    tk = k_ref.shape[2]
    sub = min(KEY_SUB, tk)
    return sub, [(kh, vh, s0, c) for s0 in range(0, tk, sub) for kh, vh, cols in heads_cols for c in cols]


def _tile_scores(k_ref, qm_ref, unit, sub, col):
    kh, _, s0, c = unit
    return jnp.dot(k_ref[0, kh, s0:s0 + sub, :], qm_ref[:, c * col:(c + 1) * col], preferred_element_type=F32)


def _plain_tiles(k_ref, vt_ref, qm_ref, acc_ref, heads_cols):
    col = acc_ref.shape[-1]
    sub, units = _score_units(k_ref, heads_cols)
    s_next = _tile_scores(k_ref, qm_ref, units[0], sub, col)
    for i, (_, vh, s0, c) in enumerate(units):
        s = s_next
        if i + 1 < len(units):
            s_next = _tile_scores(k_ref, qm_ref, units[i + 1], sub, col)
        p = jnp.exp2(s).astype(BF16)
        acc_ref[c] += jnp.dot(vt_ref[0, vh, :, s0:s0 + sub], p, preferred_element_type=F32)


def _online_tiles(k_ref, vt_ref, qm_ref, m_ref, acc_ref, heads_cols):
    col = acc_ref.shape[-1]
    sub, units = _score_units(k_ref, heads_cols)
    for unit in units:
        _, vh, s0, c = unit
        s = _tile_scores(k_ref, qm_ref, unit, sub, col)
        m_prev = m_ref[c]
        m_new = jnp.maximum(m_prev, jnp.max(s, axis=0, keepdims=True))
        alpha = jnp.exp2(m_prev - m_new)
        p = jnp.exp2(s - m_new).astype(BF16)
        acc_ref[c] = acc_ref[c] * alpha + jnp.dot(vt_ref[0, vh, :, s0:s0 + sub], p, preferred_element_type=F32)
        m_ref[c] = m_new


def _flash_init(kb_ref, kc_ref, n_key_heads, m_ref, acc_ref, plain_ref):
    k_norm2 = kb_ref[1]
    if kc_ref is not None:
        for kh in range(n_key_heads):
            k_norm2 = jnp.maximum(k_norm2, _key_norm2(kc_ref[0, kh]))
    plain_ref[0] = _needs_no_stabiliser(kb_ref[0], k_norm2)
    m_ref[...] = jnp.full(m_ref.shape, -jnp.inf, F32)
    acc_ref[...] = jnp.zeros(acc_ref.shape, F32)


def _flash_step(k_ref, vt_ref, qm_ref, m_ref, acc_ref, plain_ref, heads_cols):
    @pl.when(plain_ref[0] == 1)
    def _():
        _plain_tiles(k_ref, vt_ref, qm_ref, acc_ref, heads_cols)

    @pl.when(plain_ref[0] != 1)
    def _():
        _online_tiles(k_ref, vt_ref, qm_ref, m_ref, acc_ref, heads_cols)


def _gqa_kernel(*refs, has_cache, tq, hp):
    if has_cache:
        kb_ref, q_ref, k_ref, vt_ref, kc_ref, o_ref, qm_ref, m_ref, acc_ref, plain_ref = refs
    else:
        kb_ref, q_ref, k_ref, vt_ref, o_ref, qm_ref, m_ref, acc_ref, plain_ref = refs
        kc_ref = None
    kt = pl.program_id(3)
    col = acc_ref.shape[-1]
    per_head = tq // col
    per_kv = GQA_GROUP * per_head
    heads_cols = [(hh, hh, range(hh * per_kv, (hh + 1) * per_kv)) for hh in range(hp)]

    @pl.when(kt == 0)
    def _():
        for hh in range(hp):
            for g in range(GQA_GROUP):
                c0 = hh * 2 * LANES + (g // 2) * LANES
                row0 = (hh * GQA_GROUP + g) * tq
                qm_ref[:, row0:row0 + tq] = _head_queries_t(q_ref[0, :, c0:c0 + LANES], g % 2 == 0)
        _flash_init(kb_ref, kc_ref, hp, m_ref, acc_ref, plain_ref)

    _flash_step(k_ref, vt_ref, qm_ref, m_ref, acc_ref, plain_ref, heads_cols)

    @pl.when(kt == pl.num_programs(3) - 1)
    def _():
        for hh in range(hp):
            for part in range(per_head):
                heads = []
                for g in range(GQA_GROUP):
                    acc = acc_ref[hh * per_kv + g * per_head + part]
                    heads.append(acc[0:HEAD_DIM, :] / acc[HEAD_DIM:HEAD_DIM + 1, :])
                o_ref[0, part * col:(part + 1) * col, hh * 2 * LANES:(hh + 1) * 2 * LANES] = (
                    jnp.concatenate(heads, axis=0).T.astype(BF16))


def _kv_specs(key_heads_blk, val_heads_blk, vt_rows, tk, kd_c):
    specs = [pl.BlockSpec((1, key_heads_blk, tk, HEAD_DIM), lambda b, h, qi, kt: (b, h, kt, 0)),
             pl.BlockSpec((1, val_heads_blk, vt_rows, tk), lambda b, h, qi, kt: (b, h, 0, kt))]
    if kd_c is not None:
        specs.append(pl.BlockSpec((1, key_heads_blk, kd_c.shape[2], HEAD_DIM), lambda b, h, qi, kt: (b, h, 0, 0)))
    return specs


def _gqa_attention(q, k_bound, kd, vt, kd_c, n_batch, n_tok, tq, tk, hp):
    has_cache = kd_c is not None
    nk = kd.shape[2] // tk
    rows = hp * GQA_GROUP * tq
    col = min(MAX_COL, tq)
    q_spec = pl.BlockSpec((1, tq, hp * 2 * LANES), lambda b, h, qi, kt: (b, qi, h))
    in_specs = [pl.BlockSpec(memory_space=pltpu.SMEM), q_spec]
    in_specs += _kv_specs(hp, hp, GQA_VT_ROWS, tk, kd_c)
    args = [k_bound, q.reshape(n_batch, n_tok, BRANCH_W), kd, vt]
    if has_cache:
        args.append(kd_c)
    out = pl.pallas_call(
        functools.partial(_gqa_kernel, has_cache=has_cache, tq=tq, hp=hp),
        grid=(n_batch, GQA_KV_HEADS // hp, n_tok // tq, nk),
        in_specs=in_specs,
        out_specs=q_spec,
        out_shape=jax.ShapeDtypeStruct((n_batch, n_tok, BRANCH_W), BF16),
        scratch_shapes=[
            pltpu.VMEM((HEAD_DIM, rows), BF16),
            pltpu.VMEM((rows // col, 1, col), F32),
            pltpu.VMEM((rows // col, GQA_VT_ROWS, col), F32),
            pltpu.SMEM((1,), jnp.int32),
        ],
        compiler_params=pltpu.CompilerParams(
            dimension_semantics=("parallel", "parallel", "parallel", "arbitrary"),
            vmem_limit_bytes=VMEM_LIMIT),
        name="gqa_attention",
    )(*args)
    return out.reshape(n_batch * n_tok, BRANCH_W)


def _diff_kernel(*refs, has_cache, lam_init, hp):
    if has_cache:
        (kb_ref, q_ref, lam_ref, gsub_ref, k_ref, vt_ref, kc_ref, o_ref,
         qm_ref, m_ref, acc_ref, plain_ref) = refs
    else:
        kb_ref, q_ref, lam_ref, gsub_ref, k_ref, vt_ref, o_ref, qm_ref, m_ref, acc_ref, plain_ref = refs
        kc_ref = None
    kt = pl.program_id(3)
    tq = q_ref.shape[1]
    col = acc_ref.shape[-1]
    per_map = tq // col
    heads_cols = [(2 * hh + mm, hh, range((2 * hh + mm) * per_map, (2 * hh + mm + 1) * per_map))
                  for hh in range(hp) for mm in range(2)]

    @pl.when(kt == 0)
    def _():
        for hh in range(hp):
            blk = q_ref[0, :, hh * LANES:(hh + 1) * LANES]
            for mm in range(2):
                row0 = (2 * hh + mm) * tq
                qm_ref[:, row0:row0 + tq] = _head_queries_t(blk, mm == 0)
        _flash_init(kb_ref, kc_ref, 2 * hp, m_ref, acc_ref, plain_ref)

    _flash_step(k_ref, vt_ref, qm_ref, m_ref, acc_ref, plain_ref, heads_cols)

    @pl.when(kt == pl.num_programs(3) - 1)
    def _():
        lp = lam_ref[...]
        lam = (jnp.exp(jnp.sum(lp[0:1] * lp[1:2], axis=-1, keepdims=True))
               - jnp.exp(jnp.sum(lp[2:3] * lp[3:4], axis=-1, keepdims=True)) + lam_init)
        for hh in range(hp):
            for part in range(per_map):
                a0, a1 = acc_ref[2 * hh * per_map + part], acc_ref[(2 * hh + 1) * per_map + part]
                o0 = a0[0:DIFF_DV, :] / a0[DIFF_DV:DIFF_DV + 1, :]
                o1 = a1[0:DIFF_DV, :] / a1[DIFF_DV:DIFF_DV + 1, :]
                d = (o0 - lam * o1).T
                ms = jnp.mean(d * d, axis=-1, keepdims=True)
                o_ref[0, part * col:(part + 1) * col, hh * LANES:(hh + 1) * LANES] = (
                    d * lax.rsqrt(ms + EPS) * gsub_ref[...] * (1.0 - lam_init)).astype(BF16)


def _diff_attention(q, k_bound, lam_p, gsub, kd, vt, kd_c, n_batch, n_tok, tq, tk, hp, lam_init):
    has_cache = kd_c is not None
    nk = kd.shape[2] // tk
    col = min(MAX_COL, tq)
    rows = hp * 2 * tq
    q_spec = pl.BlockSpec((1, tq, hp * LANES), lambda b, h, qi, kt: (b, qi, h))
    in_specs = [
        pl.BlockSpec(memory_space=pltpu.SMEM),
        q_spec,
        pl.BlockSpec((4, HEAD_DIM), lambda b, h, qi, kt: (0, 0)),
        pl.BlockSpec((1, DIFF_DV), lambda b, h, qi, kt: (0, 0)),
    ]
    in_specs += _kv_specs(2 * hp, hp, DIFF_VT_ROWS, tk, kd_c)
    args = [k_bound, q.reshape(n_batch, n_tok, BRANCH_W), lam_p, gsub, kd, vt]
    if has_cache:
        args.append(kd_c)
    out = pl.pallas_call(
        functools.partial(_diff_kernel, has_cache=has_cache, lam_init=lam_init, hp=hp),
        grid=(n_batch, DIFF_HEADS // hp, n_tok // tq, nk),
        in_specs=in_specs,
        out_specs=q_spec,
        out_shape=jax.ShapeDtypeStruct((n_batch, n_tok, BRANCH_W), BF16),
        scratch_shapes=[
            pltpu.VMEM((HEAD_DIM, rows), BF16),
            pltpu.VMEM((rows // col, 1, col), F32),
            pltpu.VMEM((rows // col, DIFF_VT_ROWS, col), F32),
            pltpu.SMEM((1,), jnp.int32),
        ],
        compiler_params=pltpu.CompilerParams(
            dimension_semantics=("parallel", "parallel", "parallel", "arbitrary"),
            vmem_limit_bytes=VMEM_LIMIT),
        name="diff_attention",
    )(*args)
    return out.reshape(n_batch * n_tok, BRANCH_W)


def _gla_kernel(*refs, has_s0, n_chunk, bg):
    if has_s0:
        qf, kf, vf, laf, qb, kb, vb, lab, s0_ref, of_ref, ob_ref, sfin_ref, st_ref = refs
    else:
        qf, kf, vf, laf, qb, kb, vb, lab, of_ref, ob_ref, sfin_ref, st_ref = refs
    i = pl.program_id(1)
    ck = GLA_CHUNK

    @pl.when(i == 0)
    def _():
        for bb in range(bg):
            for d in range(2):
                for hd in range(GLA_HEADS):
                    if has_s0:
                        s = s0_ref[bb, d, hd]
                        z = jnp.zeros_like(s)
                        padded = jnp.concatenate([s, z] if hd % 2 == 0 else [z, s], axis=0)
                        st_ref[bb, d, hd] = padded.T
                    else:
                        st_ref[bb, d, hd] = jnp.zeros((GLA_DV, LANES), F32)

    tb = qf.shape[1]
    r = lax.broadcasted_iota(jnp.int32, (tb, tb), 0)
    c = lax.broadcasted_iota(jnp.int32, (tb, tb), 1)
    lane = lax.broadcasted_iota(jnp.int32, (tb, LANES), 1)
    chunk_id = lax.broadcasted_iota(jnp.int32, (tb, GLA_W), 0) // ck
    zero_row = jnp.zeros((1, GLA_W), F32)
    streams = []
    for bb in range(bg):
        streams.append((bb, 0, qf.at[bb], kf.at[bb], vf.at[bb], laf.at[bb], of_ref.at[bb], c <= r))
        streams.append((bb, 1, qb.at[bb], kb.at[bb], vb.at[bb], lab.at[bb], ob_ref.at[bb], c >= r))
    nt = (((1,), (1,)), ((), ()))
    g_all = [sum(jnp.dot(tri.astype(F32).astype(BF16), part, preferred_element_type=F32)
                 for part in _bf16_terms(la_r[...]))
             for (_, _, _, _, _, la_r, _, tri) in streams]
    prepared = []
    for g, (bb, d, q_r, k_r, v_r, la_r, o_r, tri) in zip(g_all, streams):
        if d == 0:
            bounds = [zero_row] + [g[ck * j - 1:ck * j, :] for j in range(1, n_chunk)]
            g_end = g[tb - 1:tb, :]
        else:
            bounds = [g[ck * (j + 1):ck * (j + 1) + 1, :] for j in range(n_chunk - 1)] + [zero_row]
            g_end = g[0:1, :]
        b_rows = jnp.concatenate([jnp.broadcast_to(b, (ck, GLA_W)) for b in bounds], axis=0)
        q = q_r[...].astype(F32)
        k = k_r[...].astype(F32)
        q_dec = q * jnp.exp(g - b_rows)
        q_glob = q * jnp.exp(g)
        k_end = k * jnp.exp(g_end - g)
        k_rel = []
        for j in range(n_chunk):
            reach = (chunk_id <= j) if d == 0 else (chunk_id >= j)
            k_rel.append((k * jnp.exp(jnp.where(reach, bounds[j] - g, 0.0))).astype(BF16))
        prepared.append((q_dec, q_glob, k_end, k_rel, jnp.exp(g_end), v_r[...]))

    partial = []
    for (bb, d, *_), (q_dec, q_glob, k_end, k_rel, decay, v) in zip(streams, prepared):
        per_head = []
        for hd in range(GLA_HEADS):
            pair = slice((hd // 2) * LANES, (hd // 2 + 1) * LANES)
            hv = slice(hd * GLA_DV, (hd + 1) * GLA_DV)
            keep = (lane < GLA_DK) if hd % 2 == 0 else (lane >= GLA_DK)
            qd_m = jnp.where(keep, q_dec[:, pair], 0.0).astype(BF16)
            qg_m = jnp.where(keep, q_glob[:, pair], 0.0).astype(BF16)
            ke_m = jnp.where(keep, k_end[:, pair], 0.0).astype(BF16)
            a_rows = [lax.dot_general(qd_m[ck * j:ck * (j + 1), :], k_rel[j][:, pair], nt,
                                      preferred_element_type=F32) for j in range(n_chunk)]
            s_t = st_ref[bb, d, hd]
            o_state = lax.dot_general(qg_m, s_t.astype(BF16), nt, preferred_element_type=F32)
            ds_t = lax.dot_general(v[:, hv], ke_m, (((0,), (0,)), ((), ())), preferred_element_type=F32)
            st_ref[bb, d, hd] = s_t * decay[:, pair] + ds_t
            per_head.append((a_rows, o_state))
        partial.append(per_head)

    for (bb, d, _, _, _, _, o_r, tri), prep, per_head in zip(streams, prepared, partial):
        v = prep[5]
        for hd in range(GLA_HEADS):
            hv = slice(hd * GLA_DV, (hd + 1) * GLA_DV)
            a_rows, o_state = per_head[hd]
            a = jnp.where(tri, jnp.concatenate(a_rows, axis=0), 0.0).astype(BF16)
            o_r[:, hv] = (jnp.dot(a, v[:, hv], preferred_element_type=F32) + o_state).astype(BF16)

    @pl.when(i == pl.num_programs(1) - 1)
    def _():
        for bb in range(bg):
            for d in range(2):
                for hd in range(GLA_HEADS):
                    t = st_ref[bb, d, hd].T
                    sfin_ref[bb, d, hd] = t[(hd % 2) * GLA_DK:(hd % 2 + 1) * GLA_DK, :]


def _gla(lq, lk, lv, la, s0, n_batch, n_tok, tb, bg):
    nb = n_tok // tb
    has_s0 = s0 is not None
    as3d = lambda x: x.reshape(n_batch, n_tok, x.shape[-1])
    fwd = lambda b, i: (b, i, 0)
    bwd = lambda b, i: (b, nb - 1 - i, 0)
    bwd_la = lambda b, i: (b, nb - 1 - i, 1)
    in_specs = [
        pl.BlockSpec((bg, tb, GLA_W), fwd), pl.BlockSpec((bg, tb, GLA_W), fwd),
        pl.BlockSpec((bg, tb, BRANCH_W), fwd), pl.BlockSpec((bg, tb, GLA_W), fwd),
        pl.BlockSpec((bg, tb, GLA_W), bwd), pl.BlockSpec((bg, tb, GLA_W), bwd),
        pl.BlockSpec((bg, tb, BRANCH_W), bwd), pl.BlockSpec((bg, tb, GLA_W), bwd_la),
    ]
    args = [as3d(lq), as3d(lk), as3d(lv), as3d(la)] * 2
    state_spec = pl.BlockSpec((bg, 2, GLA_HEADS, GLA_DK, GLA_DV), lambda b, i: (b, 0, 0, 0, 0))
    if has_s0:
        in_specs.append(state_spec)
        args.append(s0)
    o_f, o_b, s_fin = pl.pallas_call(
        functools.partial(_gla_kernel, has_s0=has_s0, n_chunk=tb // GLA_CHUNK, bg=bg),
        grid=(n_batch // bg, nb),
        in_specs=in_specs,
        out_specs=[pl.BlockSpec((bg, tb, BRANCH_W), fwd), pl.BlockSpec((bg, tb, BRANCH_W), bwd), state_spec],
        out_shape=[
            jax.ShapeDtypeStruct((n_batch, n_tok, BRANCH_W), BF16),
            jax.ShapeDtypeStruct((n_batch, n_tok, BRANCH_W), BF16),
            jax.ShapeDtypeStruct((n_batch, 2, GLA_HEADS, GLA_DK, GLA_DV), F32),
        ],
        scratch_shapes=[pltpu.VMEM((bg, 2, GLA_HEADS, GLA_DV, LANES), F32)],
        compiler_params=pltpu.CompilerParams(
            dimension_semantics=("parallel", "arbitrary"), vmem_limit_bytes=VMEM_LIMIT),
        name="gla",
    )(*args)
    flat = lambda x: x.reshape(n_batch * n_tok, BRANCH_W)
    return flat(o_f), flat(o_b), s_fin


FFN_CHUNK = 256


def _merge_ffn_kernel(x_ref, mod_ref, og_ref, of_ref, ob_ref, lgo_ref, od_ref, gate_ref,
                      gout_ref, wb_ref, wo_ref, g2_ref, wi_ref, wd_ref, y_ref):
    o_gla = of_ref[...].astype(F32) + ob_ref[...].astype(F32)
    gla_parts = []
    for hd in range(GLA_HEADS):
        blk = o_gla[:, hd * GLA_DV:(hd + 1) * GLA_DV]
        ms = jnp.mean(blk * blk, axis=-1, keepdims=True)
        gla_parts.append(blk * lax.rsqrt(ms + EPS) * gout_ref[...])
    gla = jnp.concatenate(gla_parts, axis=1) * _silu(lgo_ref[...].astype(F32))
    branches = (og_ref[...], gla.astype(BF16), od_ref[...])
    mixed = None
    for b, ob in enumerate(branches):
        y = jnp.dot(ob, wb_ref[b], preferred_element_type=F32)
        y = y * gate_ref[:, b * D_MODEL:(b + 1) * D_MODEL].astype(F32)
        mixed = y if mixed is None else mixed + y
    out = jnp.dot(mixed.astype(BF16), wo_ref[...], preferred_element_type=F32)
    x = x_ref[...] + mod_ref[0, 2:3, :] * out

    ms = jnp.mean(x * x, axis=-1, keepdims=True)
    h = x * lax.rsqrt(ms + EPS) * g2_ref[...]
    h = h * (1.0 + mod_ref[0, 4:5, :]) + mod_ref[0, 3:4, :]
    hb = h.astype(BF16)

    def up_proj(c0):
        a = jnp.dot(hb, wi_ref[:, c0:c0 + FFN_CHUNK], preferred_element_type=F32)
        u = jnp.dot(hb, wi_ref[:, FFN_HIDDEN + c0:FFN_HIDDEN + c0 + FFN_CHUNK], preferred_element_type=F32)
        return a, u

    acc = None
    chunks = list(range(0, FFN_HIDDEN, FFN_CHUNK))
    nxt = up_proj(chunks[0])
    for i, c0 in enumerate(chunks):
        a, u = nxt
        if i + 1 < len(chunks):
            nxt = up_proj(chunks[i + 1])
        act = (_silu(a) * u).astype(BF16)
        part = jnp.dot(act, wd_ref[c0:c0 + FFN_CHUNK, :], preferred_element_type=F32)
        acc = part if acc is None else acc + part
    y_ref[...] = x + mod_ref[0, 5:6, :] * acc


def _merge_ffn(x2d, mod, og, o_f, o_b, lgo, od, gates, lw, sw, l, tiles_per_mod, tm):
    t_total = x2d.shape[0]
    tile = lambda c: pl.BlockSpec((tm, c), lambda i: (i, 0))
    return pl.pallas_call(
        _merge_ffn_kernel,
        grid=(t_total // tm,),
        in_specs=[
            tile(D_MODEL),
            pl.BlockSpec((1, 6, D_MODEL), lambda i: (i // tiles_per_mod, 0, 0)),
            tile(BRANCH_W), tile(BRANCH_W), tile(BRANCH_W), tile(BRANCH_W), tile(BRANCH_W),
            tile(N_BRANCH * D_MODEL),
            _full_spec((1, GLA_DV)),
            _layer_spec((N_BRANCH, BRANCH_W, D_MODEL), l),
            _layer_spec((D_MODEL, D_MODEL), l),
            _full_spec((1, D_MODEL)),
            _layer_spec((D_MODEL, 2 * FFN_HIDDEN), l),
            _layer_spec((FFN_HIDDEN, D_MODEL), l),
        ],
        out_specs=tile(D_MODEL),
        out_shape=jax.ShapeDtypeStruct((t_total, D_MODEL), F32),
        compiler_params=pltpu.CompilerParams(
            dimension_semantics=("parallel",), vmem_limit_bytes=VMEM_LIMIT),
        name="merge_ffn",
    )(x2d, mod, og, o_f, o_b, lgo, od, gates, lw["gout"], sw["w_branch"], sw["w_out"],
      lw["norm2"], sw["w_ffn_in"], sw["w_ffn_out"])


def _shared_weights(p):
    return {
        "w_in_t": jnp.swapaxes(p["w_in"], 1, 2).astype(BF16),
        "w_branch": p["w_branch"].astype(BF16), "w_out": p["w_out"].astype(BF16),
        "w_ffn_in": p["w_ffn_in"].astype(BF16), "w_ffn_out": p["w_ffn_out"].astype(BF16),
    }


def _layer_weights(l, p):
    scale = HEAD_DIM ** -0.5 * math.log2(math.e)
    head_bound = lambda g, s: HEAD_DIM * (NORM_MARGIN * s) ** 2 * jnp.max(g * g)
    norm_bounds = lambda gq_, gk_: jnp.stack([head_bound(gq_, scale), head_bound(gk_, 1.0)])
    gq_row = jnp.concatenate([jnp.tile(p["gqa_q_norm"][l], GQA_HEADS),
                              jnp.tile(p["diff_q_norm"][l], 2 * DIFF_HEADS)]) * scale
    gk_row = jnp.concatenate([jnp.tile(p["gqa_k_norm"][l], GQA_KV_HEADS),
                              jnp.tile(p["diff_k_norm"][l], 2 * DIFF_HEADS)])
    idx = jnp.arange(2 * LANES) // HEAD_DIM
    gmat = jnp.where(idx[:, None] == idx[None, :], 1.0 / HEAD_DIM, 0.0).astype(BF16)
    up = jnp.zeros((LANES, 2 * GLA_W), F32)
    up = up.at[0:GLA_RANK, 0:GLA_W].set(p["gla_alpha_up"][l, 0])
    up = up.at[GLA_RANK:2 * GLA_RANK, GLA_W:].set(p["gla_alpha_up"][l, 1])
    ub = p["gla_alpha_bias"][l].reshape(1, 2 * GLA_W)
    up_hi = up.astype(BF16)
    up = jnp.stack([up_hi, (up - up_hi.astype(F32)).astype(BF16)])
    return {
        "gq": gq_row.reshape(1, 2 * N_QG), "gk": gk_row.reshape(1, LANES + BRANCH_W), "gmat": gmat,
        "up": up, "ub": ub,
        "kb_g": norm_bounds(p["gqa_q_norm"][l], p["gqa_k_norm"][l]),
        "kb_d": norm_bounds(p["diff_q_norm"][l], p["diff_k_norm"][l]),
        "norm1": p["norm1"][l].reshape(1, D_MODEL), "norm2": p["norm2"][l].reshape(1, D_MODEL),
        "gout": p["gla_out_norm"][l].reshape(1, GLA_DV),
        "gsub": p["diff_sub_norm"][l].reshape(1, DIFF_DV),
        "lam": p["diff_lambda"][l],
    }


def _rope_tables(n_tokens):
    n_rows = n_tokens // GRID_W
    row = jnp.repeat(jnp.arange(n_rows, dtype=F32), GRID_W)
    col = jnp.tile(jnp.arange(GRID_W, dtype=F32), n_rows)
    n_freq = HEAD_DIM // 4
    freqs = ROPE_THETA ** (-jnp.arange(n_freq, dtype=F32) / n_freq)
    ar, ac = row[:, None] * freqs, col[:, None] * freqs
    cos = jnp.concatenate([jnp.cos(ar), jnp.cos(ar), jnp.cos(ac), jnp.cos(ac)], axis=1)
    sin = jnp.concatenate([-jnp.sin(ar), jnp.sin(ar), -jnp.sin(ac), jnp.sin(ac)], axis=1)
    return jnp.tile(cos, (1, LANES // HEAD_DIM)), jnp.tile(sin, (1, LANES // HEAD_DIM))


def _with_ones_rows(v_t):
    lead = v_t.shape[:-2]
    s = v_t.shape[-1]
    ones = jnp.ones(lead + (1, s), v_t.dtype)
    zeros = jnp.zeros(lead + (BF16_ROWS - 1, s), v_t.dtype)
    return jnp.concatenate([v_t, ones, zeros], axis=-2).astype(BF16)


def _cache_layouts(l, cache_gqa_k, cache_gqa_v, cache_diff_k, cache_diff_v):
    b = cache_gqa_k.shape[0]
    gk = jnp.transpose(cache_gqa_k[:, l], (0, 2, 1, 3))
    kd_g = gk.astype(BF16)
    vt_g = _with_ones_rows(jnp.transpose(cache_gqa_v[:, l], (0, 2, 3, 1)))
    dk = jnp.transpose(cache_diff_k[:, l], (0, 2, 3, 1, 4)).reshape(b, 2 * DIFF_HEADS, PAST_LEN, HEAD_DIM)
    kd_d = dk.astype(BF16)
    vt_d = _with_ones_rows(jnp.transpose(cache_diff_v[:, l], (0, 2, 3, 1)))
    return kd_g, vt_g, kd_d, vt_d


def _run_layer(x2d, mod, lw, sw, l, n_batch, n_tok, rope_tabs, ctx, emit_cache, prev_caches, cfg):
    tiles_per_mod = lambda tm: (n_tok // tm) if mod.shape[0] > 1 else (n_batch * n_tok // tm)
    if ctx is None:
        cache_kv = kd_gc = kd_dc = s0 = None
    else:
        kd_gc, vt_gc, kd_dc, vt_dc, s0 = ctx
        cache_kv = (kd_gc, kd_dc, vt_gc, vt_dc)
    outs = _in_projection(x2d, mod, lw, sw, l, n_batch, n_tok, cache_kv, rope_tabs, emit_cache, prev_caches,
                          cfg["tm_in"])
    qg, qd, kdg, kdd, vtg, vtd, lq, lk, lv, lgo, la, gates = outs[:12]
    og = _gqa_attention(qg, lw["kb_g"], kdg, vtg, kd_gc, n_batch, n_tok, cfg["tq_gqa"], cfg["tk"],
                        cfg["hp_gqa"])
    lam_init = 0.8 - 0.6 * math.exp(-0.3 * l)
    od = _diff_attention(qd, lw["kb_d"], lw["lam"], lw["gsub"], kdd, vtd, kd_dc, n_batch, n_tok,
                         cfg["tq_diff"], cfg["tk"], cfg["hp_diff"], lam_init)
    o_f, o_b, s_fin = _gla(lq, lk, lv, la, s0, n_batch, n_tok, cfg["tb_gla"], cfg["bg_gla"])
    x2d = _merge_ffn(x2d, mod, og, o_f, o_b, lgo, od, gates, lw, sw, l, tiles_per_mod(cfg["tm"]), cfg["tm"])
    cache = tuple(outs[12:]) + (s_fin,) if emit_cache else None
    return x2d, cache


PROMPT_CFG = dict(tm_in=512, tq_gqa=256, tq_diff=256, tk=256, hp_gqa=2, hp_diff=4, tb_gla=256, bg_gla=2, tm=512)
SAMPLE_CFG = dict(tm_in=512, tq_gqa=1024, tq_diff=2048, tk=1536, hp_gqa=1, hp_diff=1, tb_gla=256, bg_gla=2, tm=512)


def kernel(x_prompt, x_sample, c, cache_gqa_k, cache_gqa_v, state_gla, cache_diff_k, cache_diff_v, c_ctx, w_mod, b_mod, norm1, norm2, w_in, gqa_q_norm, gqa_k_norm, gla_alpha_up, gla_alpha_bias, gla_out_norm, diff_q_norm, diff_k_norm, diff_lambda, diff_sub_norm, w_branch, w_out, w_ffn_in, w_ffn_out):
    p = {
        "norm1": norm1, "norm2": norm2, "w_in": w_in, "gqa_q_norm": gqa_q_norm, "gqa_k_norm": gqa_k_norm,
        "gla_alpha_up": gla_alpha_up, "gla_alpha_bias": gla_alpha_bias, "gla_out_norm": gla_out_norm,
        "diff_q_norm": diff_q_norm, "diff_k_norm": diff_k_norm, "diff_lambda": diff_lambda,
        "diff_sub_norm": diff_sub_norm, "w_branch": w_branch, "w_out": w_out,
        "w_ffn_in": w_ffn_in, "w_ffn_out": w_ffn_out,
    }
    n_ctx_b, n_ctx = x_prompt.shape[:2]
    n_lat_b, n_lat = x_sample.shape[:2]
    cond_rows = jnp.concatenate(
        [c_ctx[None, :], c, jnp.zeros((8 - 1 - n_lat_b, D_MODEL), F32)], axis=0)
    mod_all = _modulation(cond_rows, w_mod, b_mod)
    weights = [_layer_weights(l, p) for l in range(DEPTH)]
    shared = _shared_weights(p)

    y = x_prompt.reshape(n_ctx_b * n_ctx, D_MODEL)
    caches = []
    for l in range(DEPTH):
        mod = mod_all[l, 0:1].reshape(1, 6, D_MODEL)
        prev = [cc[:4] for cc in caches] if l == DEPTH - 1 else None
        y, cache = _run_layer(y, mod, weights[l], shared, l, n_ctx_b, n_ctx, None, None, True, prev, PROMPT_CFG)
        caches.append(cache)
    y_prompt = y.reshape(n_ctx_b, n_ctx, D_MODEL)
    stacked = caches[-1]
    new_gqa_k = stacked[0].reshape(n_ctx_b, DEPTH, n_ctx, GQA_KV_HEADS, HEAD_DIM)
    new_diff_k = stacked[1].reshape(n_ctx_b, DEPTH, n_ctx, DIFF_HEADS, 2, HEAD_DIM)
    new_gqa_v = stacked[2].reshape(n_ctx_b, DEPTH, n_ctx, GQA_KV_HEADS, HEAD_DIM)
    new_diff_v = stacked[3].reshape(n_ctx_b, DEPTH, n_ctx, DIFF_HEADS, DIFF_DV)
    new_state_gla = jnp.stack([cc[4] for cc in caches], axis=1)

    rope_tabs = _rope_tables(n_lat)
    y = x_sample.reshape(n_lat_b * n_lat, D_MODEL)
    for l in range(DEPTH):
        mod = mod_all[l, 1:1 + n_lat_b].reshape(n_lat_b, 6, D_MODEL)
        ctx = _cache_layouts(l, cache_gqa_k, cache_gqa_v, cache_diff_k, cache_diff_v) + (state_gla[:, l],)
        y, _ = _run_layer(y, mod, weights[l], shared, l, n_lat_b, n_lat, rope_tabs, ctx, False, None, SAMPLE_CFG)
    y_sample = y.reshape(n_lat_b, n_lat, D_MODEL)
    return (y_prompt, y_sample, new_gqa_k, new_gqa_v, new_state_gla, new_diff_k, new_diff_v)
```

```python
import functools
import math

import jax
import jax.numpy as jnp
from jax import lax
from jax.experimental import pallas as pl
from jax.experimental.pallas import tpu as pltpu

D_MODEL = 1024
DEPTH = 2
PAST_LEN = 512
GRID_W = 64
HEAD_DIM = 64
GQA_HEADS = 8
GQA_KV_HEADS = 2
GQA_GROUP = GQA_HEADS // GQA_KV_HEADS
GLA_HEADS = 4
GLA_DK = 64
GLA_DV = 128
GLA_RANK = 16
GLA_TAU = 16.0
GLA_CHUNK = 64
DIFF_HEADS = 4
DIFF_DV = 2 * HEAD_DIM
N_BRANCH = 3
BRANCH_W = 512
FFN_HIDDEN = ((8 * D_MODEL + 3 * 256 - 1) // (3 * 256)) * 256
ROPE_THETA = 10000.0
EPS = 1e-6

LANES = 128
BF16_ROWS = 16
VMEM_LIMIT = 56 * 1024 * 1024

F32 = jnp.float32
BF16 = jnp.bfloat16

R_GQ = 0
R_KV = R_GQ + GQA_HEADS * HEAD_DIM
R_GLA = R_KV + 2 * GQA_KV_HEADS * HEAD_DIM
R_LR = R_GLA + 2 * GLA_HEADS * GLA_DK + GLA_HEADS * GLA_DV
R_LGO = R_LR + 2 * GLA_RANK
R_DQ = R_LGO + GLA_HEADS * GLA_DV
R_DK = R_DQ + DIFF_HEADS * 2 * HEAD_DIM
R_DV = R_DK + DIFF_HEADS * 2 * HEAD_DIM
R_MG = R_DV + DIFF_HEADS * DIFF_DV
N_IN = R_MG + N_BRANCH * D_MODEL
NT_DIMS = (((1,), (1,)), ((), ()))
N_QG = GQA_HEADS * HEAD_DIM
GLA_W = GLA_HEADS * GLA_DK
GQA_VT_ROWS = HEAD_DIM + BF16_ROWS
DIFF_VT_ROWS = DIFF_DV + BF16_ROWS


def _sigmoid(x):
    return 1.0 / (1.0 + jnp.exp(-x))


def _silu(x):
    return x * _sigmoid(x)


def _bf16_terms(x):
    hi = x.astype(BF16)
    r1 = x - hi.astype(F32)
    mid = r1.astype(BF16)
    lo = (r1 - mid.astype(F32)).astype(BF16)
    return hi, mid, lo


def _full_spec(shape):
    n = len(shape)
    return pl.BlockSpec(shape, lambda *_: (0,) * n, pipeline_mode=pl.Buffered(1))


def _layer_spec(shape, l):
    n = len(shape)
    return pl.BlockSpec((None,) + tuple(shape), lambda *_: (l,) + (0,) * n, pipeline_mode=pl.Buffered(1))


def _mod_kernel(cond_ref, w_ref, b_ref, o_ref):
    s_hi, s_mid, _ = _bf16_terms(_silu(cond_ref[...]))
    lhs = jnp.concatenate([s_hi, s_mid], axis=0)
    w = w_ref[0]
    w_hi = w.astype(BF16)
    w_mid = (w - w_hi.astype(F32)).astype(BF16)
    r = jnp.dot(lhs, w_hi, preferred_element_type=F32) + jnp.dot(lhs, w_mid, preferred_element_type=F32)
    n = cond_ref.shape[0]
    o_ref[0] = r[0:n] + r[n:2 * n] + b_ref[0]


def _modulation(cond_rows, w_mod, b_mod):
    tn = 1536
    n_out = 6 * D_MODEL
    return pl.pallas_call(
        _mod_kernel,
        grid=(DEPTH, n_out // tn),
        in_specs=[
            pl.BlockSpec((8, D_MODEL), lambda l, j: (0, 0)),
            pl.BlockSpec((1, D_MODEL, tn), lambda l, j: (l, 0, j)),
            pl.BlockSpec((1, 1, tn), lambda l, j: (l, 0, j)),
        ],
        out_specs=pl.BlockSpec((1, 8, tn), lambda l, j: (l, 0, j)),
        out_shape=jax.ShapeDtypeStruct((DEPTH, 8, n_out), F32),
        compiler_params=pltpu.CompilerParams(
            dimension_semantics=("parallel", "parallel"), vmem_limit_bytes=VMEM_LIMIT),
        name="modulation",
    )(cond_rows, w_mod, b_mod.reshape(DEPTH, 1, n_out))


def _head_norm(z, gain, gmat):
    outs = []
    n = z.shape[1]
    for c0 in range(0, n, 2 * LANES):
        w = min(2 * LANES, n - c0)
        zz = z[:, c0:c0 + w]
        ms = jnp.dot((zz * zz).astype(BF16), gmat[0:w, 0:w], preferred_element_type=F32)
        outs.append(zz * lax.rsqrt(ms + EPS) * gain[:, c0:c0 + w])
    return outs


def _rope_block(zb, cos, sin, first):
    partner = jnp.where(first, pltpu.roll(zb, LANES - 16, 1), pltpu.roll(zb, 16, 1))
    return zb * cos + partner * sin


N_IN_BASE = 9


def _in_kernel(*refs, rope, emit_cache, prev_layers, tiles_per_batch, cache_tiles):
    if not cache_tiles:
        _in_tile(refs, rope, emit_cache, prev_layers)
        return
    kcg_ref, kcd_ref, vcg_ref, vcd_ref = refs[:4]
    j = pl.program_id(1)

    @pl.when(j < tiles_per_batch)
    def _():
        _in_tile(refs[4:], rope, emit_cache, prev_layers)

    @pl.when(j >= tiles_per_batch)
    def _():
        n_in = 4 + N_IN_BASE + (2 if rope else 0)
        kdg_ref, kdd_ref, vtg_ref, vtd_ref = refs[n_in + 2:n_in + 6]
        kdg_ref[...] = kcg_ref[...]
        kdd_ref[...] = kcd_ref[...]
        vtg_ref[...] = vcg_ref[...]
        vtd_ref[...] = vcd_ref[...]


def _in_tile(refs, rope, emit_cache, prev_layers):
    it = iter(refs)
    (x_ref, mod_ref, g1_ref, wt_ref, gq_ref, gk_ref, gmat_ref, up_ref, ub_ref) = (
        next(it) for _ in range(N_IN_BASE))
    if rope:
        cos_ref, sin_ref = next(it), next(it)
    prev_refs = [[next(it) for _ in range(4)] for _ in range(prev_layers or 0)]
    (qg_ref, qd_ref, kdg_ref, kdd_ref, vtg_ref, vtd_ref,
     lq_ref, lk_ref, lv_ref, lgo_ref, la_ref, gate_ref) = (next(it) for _ in range(12))
    if emit_cache:
        ck_g_ref, ck_d_ref, cv_g_ref, cv_d_ref = (next(it) for _ in range(4))

    tm = x_ref.shape[0]
    bpt = kdg_ref.shape[0]
    nt = tm // bpt
    x = x_ref[...]
    ms = jnp.mean(x * x, axis=-1, keepdims=True)
    h = x * lax.rsqrt(ms + EPS) * g1_ref[...]
    h = h * (1.0 + mod_ref[0, 1:2, :]) + mod_ref[0, 0:1, :]
    hb = h.astype(BF16)

    def proj(r0, n):
        return lax.dot_general(hb, wt_ref[r0:r0 + n, :], NT_DIMS, preferred_element_type=F32)

    lane = lax.broadcasted_iota(jnp.int32, (tm, LANES), 1)
    first16 = (lane % 32) < 16
    gmat = gmat_ref[...]
    if rope:
        cos, sin = cos_ref[...], sin_ref[...]

    def blocks128(parts):
        out = []
        for p in parts:
            for c in range(0, p.shape[1], LANES):
                out.append(p[:, c:c + LANES])
        return out

    def store_queries(z, gain, dst):
        for j, zb in enumerate(blocks128(_head_norm(z, gain, gmat))):
            if rope:
                zb = _rope_block(zb, cos, sin, first16)
            dst[:, j * LANES:(j + 1) * LANES] = zb.astype(BF16)

    def store_cache(dst, c0, value):
        w = value.shape[1]
        if prev_layers is None:
            dst[:, c0:c0 + w] = value
        else:
            for bb in range(bpt):
                dst[bb, prev_layers, :, c0:c0 + w] = value[bb * nt:(bb + 1) * nt, :]

    def store_keys(z, gain, dst, cache_dst):
        for j, zb in enumerate(blocks128(_head_norm(z, gain, gmat))):
            if rope:
                zb = _rope_block(zb, cos, sin, first16)
            if emit_cache:
                store_cache(cache_dst, j * LANES, zb)
            sw = pltpu.roll(zb, HEAD_DIM, 1)
            for bb in range(bpt):
                dst[bb, 2 * j] = zb[bb * nt:(bb + 1) * nt, 0:HEAD_DIM].astype(BF16)
                dst[bb, 2 * j + 1] = sw[bb * nt:(bb + 1) * nt, 0:HEAD_DIM].astype(BF16)

    def store_gate(b):
        gate_ref[:, b * D_MODEL:(b + 1) * D_MODEL] = _sigmoid(proj(R_MG + b * D_MODEL, D_MODEL)).astype(BF16)

    z_gq = proj(R_GQ, N_QG)
    z_dq = proj(R_DQ, N_QG)
    store_gate(0)
    store_queries(z_gq, gq_ref[:, 0:N_QG], qg_ref)
    z_kv = proj(R_KV, 2 * LANES)
    store_queries(z_dq, gq_ref[:, N_QG:2 * N_QG], qd_ref)
    z_dk = proj(R_DK, BRANCH_W)
    store_gate(1)
    store_keys(z_kv[:, 0:LANES], gk_ref[:, 0:LANES], kdg_ref, ck_g_ref if emit_cache else None)
    z_dv = proj(R_DV, BRANCH_W)
    store_keys(z_dk, gk_ref[:, LANES:LANES + BRANCH_W], kdd_ref, ck_d_ref if emit_cache else None)
    zl = proj(R_GLA, R_LR - R_GLA)
    z_lgo = proj(R_LGO, BRANCH_W)
    lr = proj(R_LR, LANES)
    store_gate(2)

    ones_rows = jnp.where(lax.broadcasted_iota(jnp.int32, (BF16_ROWS, nt), 0) == 0, 1.0, 0.0).astype(BF16)
    z_gv = z_kv[:, LANES:2 * LANES]
    if emit_cache:
        store_cache(cv_g_ref, 0, z_gv)
        store_cache(cv_d_ref, 0, z_dv)
        for layer, layer_refs in enumerate(prev_refs):
            for dst, src in zip((ck_g_ref, ck_d_ref, cv_g_ref, cv_d_ref), layer_refs):
                for bb in range(bpt):
                    dst[bb, layer] = src[bb * nt:(bb + 1) * nt, :]
    for bb in range(bpt):
        rows = slice(bb * nt, (bb + 1) * nt)
        vt = z_gv[rows, :].T
        for hh in range(GQA_KV_HEADS):
            vtg_ref[bb, hh, 0:HEAD_DIM, :] = vt[hh * HEAD_DIM:(hh + 1) * HEAD_DIM, :].astype(BF16)
            vtg_ref[bb, hh, HEAD_DIM:GQA_VT_ROWS, :] = ones_rows
        for hh in range(DIFF_HEADS):
            vt = z_dv[rows, hh * LANES:(hh + 1) * LANES].T
            vtd_ref[bb, hh, 0:DIFF_DV, :] = vt.astype(BF16)
            vtd_ref[bb, hh, DIFF_DV:DIFF_VT_ROWS, :] = ones_rows

    o = 0
    lq_ref[...] = (zl[:, o:o + GLA_W] * (GLA_DK ** -0.5)).astype(BF16)
    o += GLA_W
    lk_ref[...] = zl[:, o:o + GLA_W].astype(BF16)
    o += GLA_W
    lv_ref[...] = zl[:, o:o + BRANCH_W].astype(BF16)
    o += BRANCH_W
    lgo_ref[...] = z_lgo.astype(BF16)
    lr_hi, lr_mid, _ = _bf16_terms(lr)
    z = (jnp.dot(lr_hi, up_ref[0], preferred_element_type=F32)
         + jnp.dot(lr_mid, up_ref[0], preferred_element_type=F32)
         + jnp.dot(lr_hi, up_ref[1], preferred_element_type=F32)) + ub_ref[...]
    log_sig = jnp.minimum(z, 0.0) - jnp.log(1.0 + jnp.exp(-jnp.abs(z)))
    la_ref[...] = log_sig * (1.0 / GLA_TAU)


def _in_projection(x2d, mod, lw, sw, l, n_batch, n_tok, cache_kv, rope_tabs, emit_cache, prev_caches, tm):
    n_cache = 0 if cache_kv is None else cache_kv[0].shape[2]
    assert n_cache % tm == 0
    n_keys = n_tok + n_cache
    t_total = n_batch * n_tok
    bpt = max(1, tm // n_tok)
    assert bpt == 1 or (mod.shape[0] == 1 and n_cache == 0 and tm == bpt * n_tok)
    nt = tm // bpt
    tpb = n_tok // nt
    cache_tiles = n_cache // tm
    rope = rope_tabs is not None
    own = lambda j: jnp.minimum(j, tpb - 1)
    tile = lambda c: pl.BlockSpec((tm, c), lambda b, j: (b * tpb + own(j), 0))
    in_specs, args = [], []
    if cache_tiles:
        past = lambda j: jnp.maximum(j - tpb, 0)
        in_specs += [
            pl.BlockSpec((1, GQA_KV_HEADS, tm, HEAD_DIM), lambda b, j: (b, 0, past(j), 0)),
            pl.BlockSpec((1, 2 * DIFF_HEADS, tm, HEAD_DIM), lambda b, j: (b, 0, past(j), 0)),
            pl.BlockSpec((1, GQA_KV_HEADS, GQA_VT_ROWS, tm), lambda b, j: (b, 0, 0, past(j))),
            pl.BlockSpec((1, DIFF_HEADS, DIFF_VT_ROWS, tm), lambda b, j: (b, 0, 0, past(j))),
        ]
        args += list(cache_kv)
    in_specs += [
        tile(D_MODEL),
        pl.BlockSpec((1, 6, D_MODEL), lambda b, j: (b if mod.shape[0] > 1 else 0, 0, 0)),
        _full_spec((1, D_MODEL)),
        _layer_spec((N_IN, D_MODEL), l),
        _full_spec((1, 2 * N_QG)),
        _full_spec((1, LANES + BRANCH_W)),
        _full_spec((2 * LANES, 2 * LANES)),
        _full_spec((2, LANES, 2 * GLA_W)),
        _full_spec((1, 2 * GLA_W)),
    ]
    args += [x2d, mod, lw["norm1"], sw["w_in_t"], lw["gq"], lw["gk"], lw["gmat"], lw["up"], lw["ub"]]
    if rope:
        in_specs += [pl.BlockSpec((tm, LANES), lambda b, j: (own(j), 0))] * 2
        args += list(rope_tabs)
    sd = jax.ShapeDtypeStruct
    out_shape = [
        sd((t_total, BRANCH_W), BF16), sd((t_total, BRANCH_W), BF16),
        sd((n_batch, GQA_KV_HEADS, n_keys, HEAD_DIM), BF16), sd((n_batch, 2 * DIFF_HEADS, n_keys, HEAD_DIM), BF16),
        sd((n_batch, GQA_KV_HEADS, GQA_VT_ROWS, n_keys), BF16), sd((n_batch, DIFF_HEADS, DIFF_VT_ROWS, n_keys), BF16),
        sd((t_total, GLA_W), BF16), sd((t_total, GLA_W), BF16),
        sd((t_total, BRANCH_W), BF16), sd((t_total, BRANCH_W), BF16),
        sd((t_total, 2 * GLA_W), F32), sd((t_total, N_BRANCH * D_MODEL), BF16),
    ]
    out_specs = [
        tile(BRANCH_W), tile(BRANCH_W),
        pl.BlockSpec((bpt, GQA_KV_HEADS, nt, HEAD_DIM), lambda b, j: (b, 0, j, 0)),
        pl.BlockSpec((bpt, 2 * DIFF_HEADS, nt, HEAD_DIM), lambda b, j: (b, 0, j, 0)),
        pl.BlockSpec((bpt, GQA_KV_HEADS, GQA_VT_ROWS, nt), lambda b, j: (b, 0, 0, j)),
        pl.BlockSpec((bpt, DIFF_HEADS, DIFF_VT_ROWS, nt), lambda b, j: (b, 0, 0, j)),
        tile(GLA_W), tile(GLA_W), tile(BRANCH_W), tile(BRANCH_W), tile(2 * GLA_W), tile(N_BRANCH * D_MODEL),
    ]
    cache_widths = (LANES, BRANCH_W, LANES, BRANCH_W)
    prev_layers = None
    if emit_cache and prev_caches is None:
        out_shape += [sd((t_total, w), F32) for w in cache_widths]
        out_specs += [tile(w) for w in cache_widths]
    elif emit_cache:
        prev_layers = len(prev_caches)
        n_layers = prev_layers + 1
        for layer_arrays in prev_caches:
            in_specs += [tile(w) for w in cache_widths]
            args += list(layer_arrays)
        out_shape += [sd((n_batch, n_layers, n_tok, w), F32) for w in cache_widths]
        out_specs += [pl.BlockSpec((bpt, n_layers, nt, w), lambda b, j: (b, 0, j, 0)) for w in cache_widths]
    return pl.pallas_call(
        functools.partial(_in_kernel, rope=rope, emit_cache=emit_cache, prev_layers=prev_layers,
                          tiles_per_batch=tpb, cache_tiles=cache_tiles),
        grid=(n_batch // bpt, tpb + cache_tiles),
        in_specs=in_specs,
        out_specs=out_specs,
        out_shape=out_shape,
        compiler_params=pltpu.CompilerParams(
            dimension_semantics=("parallel", "arbitrary"), vmem_limit_bytes=VMEM_LIMIT),
        name="in_projection",
    )(*args)


MAX_COL = 4 * LANES
SCORE_BOUND = 45.0
NORM_MARGIN = 1.05


def _head_queries_t(q_blk, low):
    t = q_blk.astype(F32).T
    return (t[0:HEAD_DIM, :] if low else t[HEAD_DIM:2 * HEAD_DIM, :]).astype(BF16)


def _key_norm2(kd):
    kf = kd.astype(F32)
    return jnp.max(jnp.sum(kf * kf, axis=1, keepdims=True))


def _needs_no_stabiliser(q_norm2, k_norm2):
    return (q_norm2 * k_norm2 <= SCORE_BOUND * SCORE_BOUND).astype(jnp.int32)


KEY_SUB = 512


def _score_units(k_ref, heads_cols):
    tk = k_ref.shape[2]
    sub = min(KEY_SUB, tk)
    return sub, [(kh, vh, s0, c) for s0 in range(0, tk, sub) for kh, vh, cols in heads_cols for c in cols]


def _tile_scores(k_ref, qm_ref, unit, sub, col):
    kh, _, s0, c = unit
    return jnp.dot(k_ref[0, kh, s0:s0 + sub, :], qm_ref[:, c * col:(c + 1) * col], preferred_element_type=F32)


def _plain_tiles(k_ref, vt_ref, qm_ref, acc_ref, heads_cols):
    col = acc_ref.shape[-1]
    sub, units = _score_units(k_ref, heads_cols)
    s_next = _tile_scores(k_ref, qm_ref, units[0], sub, col)
    for i, (_, vh, s0, c) in enumerate(units):
        s = s_next
        if i + 1 < len(units):
            s_next = _tile_scores(k_ref, qm_ref, units[i + 1], sub, col)
        p = jnp.exp2(s).astype(BF16)
        acc_ref[c] += jnp.dot(vt_ref[0, vh, :, s0:s0 + sub], p, preferred_element_type=F32)


def _online_tiles(k_ref, vt_ref, qm_ref, m_ref, acc_ref, heads_cols):
    col = acc_ref.shape[-1]
    sub, units = _score_units(k_ref, heads_cols)
    for unit in units:
        _, vh, s0, c = unit
        s = _tile_scores(k_ref, qm_ref, unit, sub, col)
        m_prev = m_ref[c]
        m_new = jnp.maximum(m_prev, jnp.max(s, axis=0, keepdims=True))
        alpha = jnp.exp2(m_prev - m_new)
        p = jnp.exp2(s - m_new).astype(BF16)
        acc_ref[c] = acc_ref[c] * alpha + jnp.dot(vt_ref[0, vh, :, s0:s0 + sub], p, preferred_element_type=F32)
        m_ref[c] = m_new


def _flash_init(kb_ref, kc_ref, n_key_heads, m_ref, acc_ref, plain_ref):
    k_norm2 = kb_ref[1]
    if kc_ref is not None:
        for kh in range(n_key_heads):
            k_norm2 = jnp.maximum(k_norm2, _key_norm2(kc_ref[0, kh]))
    plain_ref[0] = _needs_no_stabiliser(kb_ref[0], k_norm2)
    m_ref[...] = jnp.full(m_ref.shape, -jnp.inf, F32)
    acc_ref[...] = jnp.zeros(acc_ref.shape, F32)


def _flash_step(k_ref, vt_ref, qm_ref, m_ref, acc_ref, plain_ref, heads_cols):
    @pl.when(plain_ref[0] == 1)
    def _():
        _plain_tiles(k_ref, vt_ref, qm_ref, acc_ref, heads_cols)

    @pl.when(plain_ref[0] != 1)
    def _():
        _online_tiles(k_ref, vt_ref, qm_ref, m_ref, acc_ref, heads_cols)


def _gqa_kernel(*refs, has_cache, tq, hp):
    if has_cache:
        kb_ref, q_ref, k_ref, vt_ref, kc_ref, o_ref, qm_ref, m_ref, acc_ref, plain_ref = refs
    else:
        kb_ref, q_ref, k_ref, vt_ref, o_ref, qm_ref, m_ref, acc_ref, plain_ref = refs
        kc_ref = None
    kt = pl.program_id(3)
    col = acc_ref.shape[-1]
    per_head = tq // col
    per_kv = GQA_GROUP * per_head
    heads_cols = [(hh, hh, range(hh * per_kv, (hh + 1) * per_kv)) for hh in range(hp)]

    @pl.when(kt == 0)
    def _():
        for hh in range(hp):
            for g in range(GQA_GROUP):
                c0 = hh * 2 * LANES + (g // 2) * LANES
                row0 = (hh * GQA_GROUP + g) * tq
                qm_ref[:, row0:row0 + tq] = _head_queries_t(q_ref[0, :, c0:c0 + LANES], g % 2 == 0)
        _flash_init(kb_ref, kc_ref, hp, m_ref, acc_ref, plain_ref)

    _flash_step(k_ref, vt_ref, qm_ref, m_ref, acc_ref, plain_ref, heads_cols)

    @pl.when(kt == pl.num_programs(3) - 1)
    def _():
        for hh in range(hp):
            for part in range(per_head):
                heads = []
                for g in range(GQA_GROUP):
                    acc = acc_ref[hh * per_kv + g * per_head + part]
                    heads.append(acc[0:HEAD_DIM, :] / acc[HEAD_DIM:HEAD_DIM + 1, :])
                o_ref[0, part * col:(part + 1) * col, hh * 2 * LANES:(hh + 1) * 2 * LANES] = (
                    jnp.concatenate(heads, axis=0).T.astype(BF16))


def _kv_specs(key_heads_blk, val_heads_blk, vt_rows, tk, kd_c):
    specs = [pl.BlockSpec((1, key_heads_blk, tk, HEAD_DIM), lambda b, h, qi, kt: (b, h, kt, 0)),
             pl.BlockSpec((1, val_heads_blk, vt_rows, tk), lambda b, h, qi, kt: (b, h, 0, kt))]
    if kd_c is not None:
        specs.append(pl.BlockSpec((1, key_heads_blk, kd_c.shape[2], HEAD_DIM), lambda b, h, qi, kt: (b, h, 0, 0)))
    return specs


def _gqa_attention(q, k_bound, kd, vt, kd_c, n_batch, n_tok, tq, tk, hp):
    has_cache = kd_c is not None
    nk = kd.shape[2] // tk
    rows = hp * GQA_GROUP * tq
    col = min(MAX_COL, tq)
    q_spec = pl.BlockSpec((1, tq, hp * 2 * LANES), lambda b, h, qi, kt: (b, qi, h))
    in_specs = [pl.BlockSpec(memory_space=pltpu.SMEM), q_spec]
    in_specs += _kv_specs(hp, hp, GQA_VT_ROWS, tk, kd_c)
    args = [k_bound, q.reshape(n_batch, n_tok, BRANCH_W), kd, vt]
    if has_cache:
        args.append(kd_c)
    out = pl.pallas_call(
        functools.partial(_gqa_kernel, has_cache=has_cache, tq=tq, hp=hp),
        grid=(n_batch, GQA_KV_HEADS // hp, n_tok // tq, nk),
        in_specs=in_specs,
        out_specs=q_spec,
        out_shape=jax.ShapeDtypeStruct((n_batch, n_tok, BRANCH_W), BF16),
        scratch_shapes=[
            pltpu.VMEM((HEAD_DIM, rows), BF16),
            pltpu.VMEM((rows // col, 1, col), F32),
            pltpu.VMEM((rows // col, GQA_VT_ROWS, col), F32),
            pltpu.SMEM((1,), jnp.int32),
        ],
        compiler_params=pltpu.CompilerParams(
            dimension_semantics=("parallel", "parallel", "parallel", "arbitrary"),
            vmem_limit_bytes=VMEM_LIMIT),
        name="gqa_attention",
    )(*args)
    return out.reshape(n_batch * n_tok, BRANCH_W)


def _diff_kernel(*refs, has_cache, lam_init, hp):
    if has_cache:
        (kb_ref, q_ref, lam_ref, gsub_ref, k_ref, vt_ref, kc_ref, o_ref,
         qm_ref, m_ref, acc_ref, plain_ref) = refs
    else:
        kb_ref, q_ref, lam_ref, gsub_ref, k_ref, vt_ref, o_ref, qm_ref, m_ref, acc_ref, plain_ref = refs
        kc_ref = None
    kt = pl.program_id(3)
    tq = q_ref.shape[1]
    col = acc_ref.shape[-1]
    per_map = tq // col
    heads_cols = [(2 * hh + mm, hh, range((2 * hh + mm) * per_map, (2 * hh + mm + 1) * per_map))
                  for hh in range(hp) for mm in range(2)]

    @pl.when(kt == 0)
    def _():
        for hh in range(hp):
            blk = q_ref[0, :, hh * LANES:(hh + 1) * LANES]
            for mm in range(2):
                row0 = (2 * hh + mm) * tq
                qm_ref[:, row0:row0 + tq] = _head_queries_t(blk, mm == 0)
        _flash_init(kb_ref, kc_ref, 2 * hp, m_ref, acc_ref, plain_ref)

    _flash_step(k_ref, vt_ref, qm_ref, m_ref, acc_ref, plain_ref, heads_cols)

    @pl.when(kt == pl.num_programs(3) - 1)
    def _():
        lp = lam_ref[...]
        lam = (jnp.exp(jnp.sum(lp[0:1] * lp[1:2], axis=-1, keepdims=True))
               - jnp.exp(jnp.sum(lp[2:3] * lp[3:4], axis=-1, keepdims=True)) + lam_init)
        for hh in range(hp):
            for part in range(per_map):
                a0, a1 = acc_ref[2 * hh * per_map + part], acc_ref[(2 * hh + 1) * per_map + part]
                o0 = a0[0:DIFF_DV, :] / a0[DIFF_DV:DIFF_DV + 1, :]
                o1 = a1[0:DIFF_DV, :] / a1[DIFF_DV:DIFF_DV + 1, :]
                d = (o0 - lam * o1).T
                ms = jnp.mean(d * d, axis=-1, keepdims=True)
                o_ref[0, part * col:(part + 1) * col, hh * LANES:(hh + 1) * LANES] = (
                    d * lax.rsqrt(ms + EPS) * gsub_ref[...] * (1.0 - lam_init)).astype(BF16)


def _diff_attention(q, k_bound, lam_p, gsub, kd, vt, kd_c, n_batch, n_tok, tq, tk, hp, lam_init):
    has_cache = kd_c is not None
    nk = kd.shape[2] // tk
    col = min(MAX_COL, tq)
    rows = hp * 2 * tq
    q_spec = pl.BlockSpec((1, tq, hp * LANES), lambda b, h, qi, kt: (b, qi, h))
    in_specs = [
        pl.BlockSpec(memory_space=pltpu.SMEM),
        q_spec,
        pl.BlockSpec((4, HEAD_DIM), lambda b, h, qi, kt: (0, 0)),
        pl.BlockSpec((1, DIFF_DV), lambda b, h, qi, kt: (0, 0)),
    ]
    in_specs += _kv_specs(2 * hp, hp, DIFF_VT_ROWS, tk, kd_c)
    args = [k_bound, q.reshape(n_batch, n_tok, BRANCH_W), lam_p, gsub, kd, vt]
    if has_cache:
        args.append(kd_c)
    out = pl.pallas_call(
        functools.partial(_diff_kernel, has_cache=has_cache, lam_init=lam_init, hp=hp),
        grid=(n_batch, DIFF_HEADS // hp, n_tok // tq, nk),
        in_specs=in_specs,
        out_specs=q_spec,
        out_shape=jax.ShapeDtypeStruct((n_batch, n_tok, BRANCH_W), BF16),
        scratch_shapes=[
            pltpu.VMEM((HEAD_DIM, rows), BF16),
            pltpu.VMEM((rows // col, 1, col), F32),
            pltpu.VMEM((rows // col, DIFF_VT_ROWS, col), F32),
            pltpu.SMEM((1,), jnp.int32),
        ],
        compiler_params=pltpu.CompilerParams(
            dimension_semantics=("parallel", "parallel", "parallel", "arbitrary"),
            vmem_limit_bytes=VMEM_LIMIT),
        name="diff_attention",
    )(*args)
    return out.reshape(n_batch * n_tok, BRANCH_W)


def _gla_kernel(*refs, has_s0, n_chunk, bg):
    if has_s0:
        qf, kf, vf, laf, qb, kb, vb, lab, s0_ref, of_ref, ob_ref, sfin_ref, st_ref = refs
    else:
        qf, kf, vf, laf, qb, kb, vb, lab, of_ref, ob_ref, sfin_ref, st_ref = refs
    i = pl.program_id(1)
    ck = GLA_CHUNK

    @pl.when(i == 0)
    def _():
        for bb in range(bg):
            for d in range(2):
                for hd in range(GLA_HEADS):
                    if has_s0:
                        s = s0_ref[bb, d, hd]
                        z = jnp.zeros_like(s)
                        padded = jnp.concatenate([s, z] if hd % 2 == 0 else [z, s], axis=0)
                        st_ref[bb, d, hd] = padded.T
                    else:
                        st_ref[bb, d, hd] = jnp.zeros((GLA_DV, LANES), F32)

    tb = qf.shape[1]
    r = lax.broadcasted_iota(jnp.int32, (tb, tb), 0)
    c = lax.broadcasted_iota(jnp.int32, (tb, tb), 1)
    lane = lax.broadcasted_iota(jnp.int32, (tb, LANES), 1)
    chunk_id = lax.broadcasted_iota(jnp.int32, (tb, GLA_W), 0) // ck
    zero_row = jnp.zeros((1, GLA_W), F32)
    streams = []
    for bb in range(bg):
        streams.append((bb, 0, qf.at[bb], kf.at[bb], vf.at[bb], laf.at[bb], of_ref.at[bb], c <= r))
        streams.append((bb, 1, qb.at[bb], kb.at[bb], vb.at[bb], lab.at[bb], ob_ref.at[bb], c >= r))
    nt = (((1,), (1,)), ((), ()))
    g_all = [sum(jnp.dot(tri.astype(F32).astype(BF16), part, preferred_element_type=F32)
                 for part in _bf16_terms(la_r[...]))
             for (_, _, _, _, _, la_r, _, tri) in streams]
    prepared = []
    for g, (bb, d, q_r, k_r, v_r, la_r, o_r, tri) in zip(g_all, streams):
        if d == 0:
            bounds = [zero_row] + [g[ck * j - 1:ck * j, :] for j in range(1, n_chunk)]
            g_end = g[tb - 1:tb, :]
        else:
            bounds = [g[ck * (j + 1):ck * (j + 1) + 1, :] for j in range(n_chunk - 1)] + [zero_row]
            g_end = g[0:1, :]
        b_rows = jnp.concatenate([jnp.broadcast_to(b, (ck, GLA_W)) for b in bounds], axis=0)
        q = q_r[...].astype(F32)
        k = k_r[...].astype(F32)
        q_dec = q * jnp.exp(g - b_rows)
        q_glob = q * jnp.exp(g)
        k_end = k * jnp.exp(g_end - g)
        k_rel = []
        for j in range(n_chunk):
            reach = (chunk_id <= j) if d == 0 else (chunk_id >= j)
            k_rel.append((k * jnp.exp(jnp.where(reach, bounds[j] - g, 0.0))).astype(BF16))
        prepared.append((q_dec, q_glob, k_end, k_rel, jnp.exp(g_end), v_r[...]))

    partial = []
    for (bb, d, *_), (q_dec, q_glob, k_end, k_rel, decay, v) in zip(streams, prepared):
        per_head = []
        for hd in range(GLA_HEADS):
            pair = slice((hd // 2) * LANES, (hd // 2 + 1) * LANES)
            hv = slice(hd * GLA_DV, (hd + 1) * GLA_DV)
            keep = (lane < GLA_DK) if hd % 2 == 0 else (lane >= GLA_DK)
            qd_m = jnp.where(keep, q_dec[:, pair], 0.0).astype(BF16)
            qg_m = jnp.where(keep, q_glob[:, pair], 0.0).astype(BF16)
            ke_m = jnp.where(keep, k_end[:, pair], 0.0).astype(BF16)
            a_rows = [lax.dot_general(qd_m[ck * j:ck * (j + 1), :], k_rel[j][:, pair], nt,
                                      preferred_element_type=F32) for j in range(n_chunk)]
            s_t = st_ref[bb, d, hd]
            o_state = lax.dot_general(qg_m, s_t.astype(BF16), nt, preferred_element_type=F32)
            ds_t = lax.dot_general(v[:, hv], ke_m, (((0,), (0,)), ((), ())), preferred_element_type=F32)
            st_ref[bb, d, hd] = s_t * decay[:, pair] + ds_t
            per_head.append((a_rows, o_state))
        partial.append(per_head)

    for (bb, d, _, _, _, _, o_r, tri), prep, per_head in zip(streams, prepared, partial):
        v = prep[5]
        for hd in range(GLA_HEADS):
            hv = slice(hd * GLA_DV, (hd + 1) * GLA_DV)
            a_rows, o_state = per_head[hd]
            a = jnp.where(tri, jnp.concatenate(a_rows, axis=0), 0.0).astype(BF16)
            o_r[:, hv] = (jnp.dot(a, v[:, hv], preferred_element_type=F32) + o_state).astype(BF16)

    @pl.when(i == pl.num_programs(1) - 1)
    def _():
        for bb in range(bg):
            for d in range(2):
                for hd in range(GLA_HEADS):
                    t = st_ref[bb, d, hd].T
                    sfin_ref[bb, d, hd] = t[(hd % 2) * GLA_DK:(hd % 2 + 1) * GLA_DK, :]


def _gla(lq, lk, lv, la, s0, n_batch, n_tok, tb, bg):
    nb = n_tok // tb
    has_s0 = s0 is not None
    as3d = lambda x: x.reshape(n_batch, n_tok, x.shape[-1])
    fwd = lambda b, i: (b, i, 0)
    bwd = lambda b, i: (b, nb - 1 - i, 0)
    bwd_la = lambda b, i: (b, nb - 1 - i, 1)
    in_specs = [
        pl.BlockSpec((bg, tb, GLA_W), fwd), pl.BlockSpec((bg, tb, GLA_W), fwd),
        pl.BlockSpec((bg, tb, BRANCH_W), fwd), pl.BlockSpec((bg, tb, GLA_W), fwd),
        pl.BlockSpec((bg, tb, GLA_W), bwd), pl.BlockSpec((bg, tb, GLA_W), bwd),
        pl.BlockSpec((bg, tb, BRANCH_W), bwd), pl.BlockSpec((bg, tb, GLA_W), bwd_la),
    ]
    args = [as3d(lq), as3d(lk), as3d(lv), as3d(la)] * 2
    state_spec = pl.BlockSpec((bg, 2, GLA_HEADS, GLA_DK, GLA_DV), lambda b, i: (b, 0, 0, 0, 0))
    if has_s0:
        in_specs.append(state_spec)
        args.append(s0)
    o_f, o_b, s_fin = pl.pallas_call(
        functools.partial(_gla_kernel, has_s0=has_s0, n_chunk=tb // GLA_CHUNK, bg=bg),
        grid=(n_batch // bg, nb),
        in_specs=in_specs,
        out_specs=[pl.BlockSpec((bg, tb, BRANCH_W), fwd), pl.BlockSpec((bg, tb, BRANCH_W), bwd), state_spec],
        out_shape=[
            jax.ShapeDtypeStruct((n_batch, n_tok, BRANCH_W), BF16),
            jax.ShapeDtypeStruct((n_batch, n_tok, BRANCH_W), BF16),
            jax.ShapeDtypeStruct((n_batch, 2, GLA_HEADS, GLA_DK, GLA_DV), F32),
        ],
        scratch_shapes=[pltpu.VMEM((bg, 2, GLA_HEADS, GLA_DV, LANES), F32)],
        compiler_params=pltpu.CompilerParams(
            dimension_semantics=("parallel", "arbitrary"), vmem_limit_bytes=VMEM_LIMIT),
        name="gla",
    )(*args)
    flat = lambda x: x.reshape(n_batch * n_tok, BRANCH_W)
    return flat(o_f), flat(o_b), s_fin


FFN_CHUNK = 256


def _merge_ffn_kernel(x_ref, mod_ref, og_ref, of_ref, ob_ref, lgo_ref, od_ref, gate_ref,
                      gout_ref, wb_ref, wo_ref, g2_ref, wi_ref, wd_ref, y_ref):
    o_gla = of_ref[...].astype(F32) + ob_ref[...].astype(F32)
    gla_parts = []
    for hd in range(GLA_HEADS):
        blk = o_gla[:, hd * GLA_DV:(hd + 1) * GLA_DV]
        ms = jnp.mean(blk * blk, axis=-1, keepdims=True)
        gla_parts.append(blk * lax.rsqrt(ms + EPS) * gout_ref[...])
    gla = jnp.concatenate(gla_parts, axis=1) * _silu(lgo_ref[...].astype(F32))
    branches = (og_ref[...], gla.astype(BF16), od_ref[...])
    mixed = None
    for b, ob in enumerate(branches):
        y = jnp.dot(ob, wb_ref[b], preferred_element_type=F32)
        y = y * gate_ref[:, b * D_MODEL:(b + 1) * D_MODEL].astype(F32)
        mixed = y if mixed is None else mixed + y
    out = jnp.dot(mixed.astype(BF16), wo_ref[...], preferred_element_type=F32)
    x = x_ref[...] + mod_ref[0, 2:3, :] * out

    ms = jnp.mean(x * x, axis=-1, keepdims=True)
    h = x * lax.rsqrt(ms + EPS) * g2_ref[...]
    h = h * (1.0 + mod_ref[0, 4:5, :]) + mod_ref[0, 3:4, :]
    hb = h.astype(BF16)

    def up_proj(c0):
        a = jnp.dot(hb, wi_ref[:, c0:c0 + FFN_CHUNK], preferred_element_type=F32)
        u = jnp.dot(hb, wi_ref[:, FFN_HIDDEN + c0:FFN_HIDDEN + c0 + FFN_CHUNK], preferred_element_type=F32)
        return a, u

    acc = None
    chunks = list(range(0, FFN_HIDDEN, FFN_CHUNK))
    nxt = up_proj(chunks[0])
    for i, c0 in enumerate(chunks):
        a, u = nxt
        if i + 1 < len(chunks):
            nxt = up_proj(chunks[i + 1])
        act = (_silu(a) * u).astype(BF16)
        part = jnp.dot(act, wd_ref[c0:c0 + FFN_CHUNK, :], preferred_element_type=F32)
        acc = part if acc is None else acc + part
    y_ref[...] = x + mod_ref[0, 5:6, :] * acc


def _merge_ffn(x2d, mod, og, o_f, o_b, lgo, od, gates, lw, sw, l, tiles_per_mod, tm):
    t_total = x2d.shape[0]
    tile = lambda c: pl.BlockSpec((tm, c), lambda i: (i, 0))
    return pl.pallas_call(
        _merge_ffn_kernel,
        grid=(t_total // tm,),
        in_specs=[
            tile(D_MODEL),
            pl.BlockSpec((1, 6, D_MODEL), lambda i: (i // tiles_per_mod, 0, 0)),
            tile(BRANCH_W), tile(BRANCH_W), tile(BRANCH_W), tile(BRANCH_W), tile(BRANCH_W),
            tile(N_BRANCH * D_MODEL),
            _full_spec((1, GLA_DV)),
            _layer_spec((N_BRANCH, BRANCH_W, D_MODEL), l),
            _layer_spec((D_MODEL, D_MODEL), l),
            _full_spec((1, D_MODEL)),
            _layer_spec((D_MODEL, 2 * FFN_HIDDEN), l),
            _layer_spec((FFN_HIDDEN, D_MODEL), l),
        ],
        out_specs=tile(D_MODEL),
        out_shape=jax.ShapeDtypeStruct((t_total, D_MODEL), F32),
        compiler_params=pltpu.CompilerParams(
            dimension_semantics=("parallel",), vmem_limit_bytes=VMEM_LIMIT),
        name="merge_ffn",
    )(x2d, mod, og, o_f, o_b, lgo, od, gates, lw["gout"], sw["w_branch"], sw["w_out"],
      lw["norm2"], sw["w_ffn_in"], sw["w_ffn_out"])


def _shared_weights(p):
    return {
        "w_in_t": jnp.swapaxes(p["w_in"], 1, 2).astype(BF16),
        "w_branch": p["w_branch"].astype(BF16), "w_out": p["w_out"].astype(BF16),
        "w_ffn_in": p["w_ffn_in"].astype(BF16), "w_ffn_out": p["w_ffn_out"].astype(BF16),
    }


def _layer_weights(l, p):
    scale = HEAD_DIM ** -0.5 * math.log2(math.e)
    head_bound = lambda g, s: HEAD_DIM * (NORM_MARGIN * s) ** 2 * jnp.max(g * g)
    norm_bounds = lambda gq_, gk_: jnp.stack([head_bound(gq_, scale), head_bound(gk_, 1.0)])
    gq_row = jnp.concatenate([jnp.tile(p["gqa_q_norm"][l], GQA_HEADS),
                              jnp.tile(p["diff_q_norm"][l], 2 * DIFF_HEADS)]) * scale
    gk_row = jnp.concatenate([jnp.tile(p["gqa_k_norm"][l], GQA_KV_HEADS),
                              jnp.tile(p["diff_k_norm"][l], 2 * DIFF_HEADS)])
    idx = jnp.arange(2 * LANES) // HEAD_DIM
    gmat = jnp.where(idx[:, None] == idx[None, :], 1.0 / HEAD_DIM, 0.0).astype(BF16)
    up = jnp.zeros((LANES, 2 * GLA_W), F32)
    up = up.at[0:GLA_RANK, 0:GLA_W].set(p["gla_alpha_up"][l, 0])
    up = up.at[GLA_RANK:2 * GLA_RANK, GLA_W:].set(p["gla_alpha_up"][l, 1])
    ub = p["gla_alpha_bias"][l].reshape(1, 2 * GLA_W)
    up_hi = up.astype(BF16)
    up = jnp.stack([up_hi, (up - up_hi.astype(F32)).astype(BF16)])
    return {
        "gq": gq_row.reshape(1, 2 * N_QG), "gk": gk_row.reshape(1, LANES + BRANCH_W), "gmat": gmat,
        "up": up, "ub": ub,
        "kb_g": norm_bounds(p["gqa_q_norm"][l], p["gqa_k_norm"][l]),
        "kb_d": norm_bounds(p["diff_q_norm"][l], p["diff_k_norm"][l]),
        "norm1": p["norm1"][l].reshape(1, D_MODEL), "norm2": p["norm2"][l].reshape(1, D_MODEL),
        "gout": p["gla_out_norm"][l].reshape(1, GLA_DV),
        "gsub": p["diff_sub_norm"][l].reshape(1, DIFF_DV),
        "lam": p["diff_lambda"][l],
    }


def _rope_tables(n_tokens):
    n_rows = n_tokens // GRID_W
    row = jnp.repeat(jnp.arange(n_rows, dtype=F32), GRID_W)
    col = jnp.tile(jnp.arange(GRID_W, dtype=F32), n_rows)
    n_freq = HEAD_DIM // 4
    freqs = ROPE_THETA ** (-jnp.arange(n_freq, dtype=F32) / n_freq)
    ar, ac = row[:, None] * freqs, col[:, None] * freqs
    cos = jnp.concatenate([jnp.cos(ar), jnp.cos(ar), jnp.cos(ac), jnp.cos(ac)], axis=1)
    sin = jnp.concatenate([-jnp.sin(ar), jnp.sin(ar), -jnp.sin(ac), jnp.sin(ac)], axis=1)
    return jnp.tile(cos, (1, LANES // HEAD_DIM)), jnp.tile(sin, (1, LANES // HEAD_DIM))


def _with_ones_rows(v_t):
    lead = v_t.shape[:-2]
    s = v_t.shape[-1]
    ones = jnp.ones(lead + (1, s), v_t.dtype)
    zeros = jnp.zeros(lead + (BF16_ROWS - 1, s), v_t.dtype)
    return jnp.concatenate([v_t, ones, zeros], axis=-2).astype(BF16)


def _cache_layouts(l, cache_gqa_k, cache_gqa_v, cache_diff_k, cache_diff_v):
    b = cache_gqa_k.shape[0]
    gk = jnp.transpose(cache_gqa_k[:, l], (0, 2, 1, 3))
    kd_g = gk.astype(BF16)
    vt_g = _with_ones_rows(jnp.transpose(cache_gqa_v[:, l], (0, 2, 3, 1)))
    dk = jnp.transpose(cache_diff_k[:, l], (0, 2, 3, 1, 4)).reshape(b, 2 * DIFF_HEADS, PAST_LEN, HEAD_DIM)
    kd_d = dk.astype(BF16)
    vt_d = _with_ones_rows(jnp.transpose(cache_diff_v[:, l], (0, 2, 3, 1)))
    return kd_g, vt_g, kd_d, vt_d


def _run_layer(x2d, mod, lw, sw, l, n_batch, n_tok, rope_tabs, ctx, emit_cache, prev_caches, cfg):
    tiles_per_mod = lambda tm: (n_tok // tm) if mod.shape[0] > 1 else (n_batch * n_tok // tm)
    if ctx is None:
        cache_kv = kd_gc = kd_dc = s0 = None
    else:
        kd_gc, vt_gc, kd_dc, vt_dc, s0 = ctx
        cache_kv = (kd_gc, kd_dc, vt_gc, vt_dc)
    outs = _in_projection(x2d, mod, lw, sw, l, n_batch, n_tok, cache_kv, rope_tabs, emit_cache, prev_caches,
                          cfg["tm_in"])
    qg, qd, kdg, kdd, vtg, vtd, lq, lk, lv, lgo, la, gates = outs[:12]
    og = _gqa_attention(qg, lw["kb_g"], kdg, vtg, kd_gc, n_batch, n_tok, cfg["tq_gqa"], cfg["tk"],
                        cfg["hp_gqa"])
    lam_init = 0.8 - 0.6 * math.exp(-0.3 * l)
    od = _diff_attention(qd, lw["kb_d"], lw["lam"], lw["gsub"], kdd, vtd, kd_dc, n_batch, n_tok,
                         cfg["tq_diff"], cfg["tk"], cfg["hp_diff"], lam_init)
    o_f, o_b, s_fin = _gla(lq, lk, lv, la, s0, n_batch, n_tok, cfg["tb_gla"], cfg["bg_gla"])
    x2d = _merge_ffn(x2d, mod, og, o_f, o_b, lgo, od, gates, lw, sw, l, tiles_per_mod(cfg["tm"]), cfg["tm"])
    cache = tuple(outs[12:]) + (s_fin,) if emit_cache else None
    return x2d, cache


PROMPT_CFG = dict(tm_in=512, tq_gqa=256, tq_diff=256, tk=256, hp_gqa=2, hp_diff=4, tb_gla=256, bg_gla=2, tm=512)
SAMPLE_CFG = dict(tm_in=512, tq_gqa=1024, tq_diff=2048, tk=1536, hp_gqa=1, hp_diff=1, tb_gla=256, bg_gla=2, tm=512)


def kernel(x_prompt, x_sample, c, cache_gqa_k, cache_gqa_v, state_gla, cache_diff_k, cache_diff_v, c_ctx, w_mod, b_mod, norm1, norm2, w_in, gqa_q_norm, gqa_k_norm, gla_alpha_up, gla_alpha_bias, gla_out_norm, diff_q_norm, diff_k_norm, diff_lambda, diff_sub_norm, w_branch, w_out, w_ffn_in, w_ffn_out):
    p = {
        "norm1": norm1, "norm2": norm2, "w_in": w_in, "gqa_q_norm": gqa_q_norm, "gqa_k_norm": gqa_k_norm,
        "gla_alpha_up": gla_alpha_up, "gla_alpha_bias": gla_alpha_bias, "gla_out_norm": gla_out_norm,
        "diff_q_norm": diff_q_norm, "diff_k_norm": diff_k_norm, "diff_lambda": diff_lambda,
        "diff_sub_norm": diff_sub_norm, "w_branch": w_branch, "w_out": w_out,
        "w_ffn_in": w_ffn_in, "w_ffn_out": w_ffn_out,
    }
    n_ctx_b, n_ctx = x_prompt.shape[:2]
    n_lat_b, n_lat = x_sample.shape[:2]
    cond_rows = jnp.concatenate(
        [c_ctx[None, :], c, jnp.zeros((8 - 1 - n_lat_b, D_MODEL), F32)], axis=0)
    mod_all = _modulation(cond_rows, w_mod, b_mod)
    weights = [_layer_weights(l, p) for l in range(DEPTH)]
    shared = _shared_weights(p)

    y = x_prompt.reshape(n_ctx_b * n_ctx, D_MODEL)
    caches = []
    for l in range(DEPTH):
        mod = mod_all[l, 0:1].reshape(1, 6, D_MODEL)
        prev = [cc[:4] for cc in caches] if l == DEPTH - 1 else None
        y, cache = _run_layer(y, mod, weights[l], shared, l, n_ctx_b, n_ctx, None, None, True, prev, PROMPT_CFG)
        caches.append(cache)
    y_prompt = y.reshape(n_ctx_b, n_ctx, D_MODEL)
    stacked = caches[-1]
    new_gqa_k = stacked[0].reshape(n_ctx_b, DEPTH, n_ctx, GQA_KV_HEADS, HEAD_DIM)
    new_diff_k = stacked[1].reshape(n_ctx_b, DEPTH, n_ctx, DIFF_HEADS, 2, HEAD_DIM)
    new_gqa_v = stacked[2].reshape(n_ctx_b, DEPTH, n_ctx, GQA_KV_HEADS, HEAD_DIM)
    new_diff_v = stacked[3].reshape(n_ctx_b, DEPTH, n_ctx, DIFF_HEADS, DIFF_DV)
    new_state_gla = jnp.stack([cc[4] for cc in caches], axis=1)

    rope_tabs = _rope_tables(n_lat)
    y = x_sample.reshape(n_lat_b * n_lat, D_MODEL)
    for l in range(DEPTH):
        mod = mod_all[l, 1:1 + n_lat_b].reshape(n_lat_b, 6, D_MODEL)
        ctx = _cache_layouts(l, cache_gqa_k, cache_gqa_v, cache_diff_k, cache_diff_v) + (state_gla[:, l],)
        y, _ = _run_layer(y, mod, weights[l], shared, l, n_lat_b, n_lat, rope_tabs, ctx, False, None, SAMPLE_CFG)
    y_sample = y.reshape(n_lat_b, n_lat, D_MODEL)
    return (y_prompt, y_sample, new_gqa_k, new_gqa_v, new_state_gla, new_diff_k, new_diff_v)
```

```python
import functools
import math

import jax
import jax.numpy as jnp
from jax import lax
from jax.experimental import pallas as pl
from jax.experimental.pallas import tpu as pltpu

D_MODEL = 1024
DEPTH = 2
PAST_LEN = 512
GRID_W = 64
HEAD_DIM = 64
GQA_HEADS = 8
GQA_KV_HEADS = 2
GQA_GROUP = GQA_HEADS // GQA_KV_HEADS
GLA_HEADS = 4
GLA_DK = 64
GLA_DV = 128
GLA_RANK = 16
GLA_TAU = 16.0
GLA_CHUNK = 64
DIFF_HEADS = 4
DIFF_DV = 2 * HEAD_DIM
N_BRANCH = 3
BRANCH_W = 512
FFN_HIDDEN = ((8 * D_MODEL + 3 * 256 - 1) // (3 * 256)) * 256
ROPE_THETA = 10000.0
EPS = 1e-6

LANES = 128
BF16_ROWS = 16
F32_ROWS = 8
VMEM_LIMIT = 56 * 1024 * 1024

F32 = jnp.float32
BF16 = jnp.bfloat16

R_GQ = 0
R_KV = R_GQ + GQA_HEADS * HEAD_DIM
R_GLA = R_KV + 2 * GQA_KV_HEADS * HEAD_DIM
R_LR = R_GLA + 2 * GLA_HEADS * GLA_DK + GLA_HEADS * GLA_DV
R_LGO = R_LR + 2 * GLA_RANK
R_DQ = R_LGO + GLA_HEADS * GLA_DV
R_DK = R_DQ + DIFF_HEADS * 2 * HEAD_DIM
R_DV = R_DK + DIFF_HEADS * 2 * HEAD_DIM
R_MG = R_DV + DIFF_HEADS * DIFF_DV
N_IN = R_MG + N_BRANCH * D_MODEL
NT_DIMS = (((1,), (1,)), ((), ()))
N_QG = GQA_HEADS * HEAD_DIM
GLA_W = GLA_HEADS * GLA_DK
GQA_VT_ROWS = HEAD_DIM + BF16_ROWS
DIFF_VT_ROWS = DIFF_DV + BF16_ROWS


def _sigmoid(x):
    return 0.5 * jnp.tanh(0.5 * x) + 0.5


def _silu(x):
    return x * _sigmoid(x)


def _bf16_terms(x):
    hi = x.astype(BF16)
    r1 = x - hi.astype(F32)
    mid = r1.astype(BF16)
    lo = (r1 - mid.astype(F32)).astype(BF16)
    return hi, mid, lo


def _full_spec(shape):
    n = len(shape)
    return pl.BlockSpec(shape, lambda *_: (0,) * n, pipeline_mode=pl.Buffered(1))


def _layer_spec(shape, l):
    n = len(shape)
    return pl.BlockSpec((None,) + tuple(shape), lambda *_: (l,) + (0,) * n, pipeline_mode=pl.Buffered(1))


def _mod_kernel(cond_ref, w_ref, b_ref, o_ref):
    s_hi, s_mid, _ = _bf16_terms(_silu(cond_ref[...]))
    lhs = jnp.concatenate([s_hi, s_mid], axis=0)
    w = w_ref[0]
    w_hi = w.astype(BF16)
    w_mid = (w - w_hi.astype(F32)).astype(BF16)
    r = jnp.dot(lhs, w_hi, preferred_element_type=F32) + jnp.dot(lhs, w_mid, preferred_element_type=F32)
    n = cond_ref.shape[0]
    o_ref[0] = r[0:n] + r[n:2 * n] + b_ref[0]


def _modulation(cond_rows, w_mod, b_mod):
    tn = 1536
    n_out = 6 * D_MODEL
    return pl.pallas_call(
        _mod_kernel,
        grid=(DEPTH, n_out // tn),
        in_specs=[
            pl.BlockSpec((F32_ROWS, D_MODEL), lambda l, j: (0, 0)),
            pl.BlockSpec((1, D_MODEL, tn), lambda l, j: (l, 0, j)),
            pl.BlockSpec((1, 1, tn), lambda l, j: (l, 0, j)),
        ],
        out_specs=pl.BlockSpec((1, F32_ROWS, tn), lambda l, j: (l, 0, j)),
        out_shape=jax.ShapeDtypeStruct((DEPTH, F32_ROWS, n_out), F32),
        compiler_params=pltpu.CompilerParams(
            dimension_semantics=("parallel", "parallel"), vmem_limit_bytes=VMEM_LIMIT),
        name="modulation",
    )(cond_rows, w_mod, b_mod.reshape(DEPTH, 1, n_out))


def _head_norm(z, gain, gmat):
    outs = []
    n = z.shape[1]
    for c0 in range(0, n, 2 * LANES):
        w = min(2 * LANES, n - c0)
        zz = z[:, c0:c0 + w]
        ms = jnp.dot((zz * zz).astype(BF16), gmat[0:w, 0:w], preferred_element_type=F32)
        outs.append(zz * lax.rsqrt(ms + EPS) * gain[:, c0:c0 + w])
    return outs


def _rope_block(zb, cos, sin, first):
    partner = jnp.where(first, pltpu.roll(zb, LANES - 16, 1), pltpu.roll(zb, 16, 1))
    return zb * cos + partner * sin


N_IN_BASE = 9


def _in_kernel(*refs, rope, emit_cache, prev_layers, tiles_per_batch, cache_tiles):
    if not cache_tiles:
        _in_tile(refs, rope, emit_cache, prev_layers)
        return
    kcg_ref, kcd_ref, vcg_ref, vcd_ref = refs[:4]
    j = pl.program_id(1)

    @pl.when(j < tiles_per_batch)
    def _():
        _in_tile(refs[4:], rope, emit_cache, prev_layers)

    @pl.when(j >= tiles_per_batch)
    def _():
        n_in = 4 + N_IN_BASE + (2 if rope else 0)
        kdg_ref, kdd_ref, vtg_ref, vtd_ref = refs[n_in + 2:n_in + 6]
        kdg_ref[...] = kcg_ref[...]
        kdd_ref[...] = kcd_ref[...]
        vtg_ref[...] = vcg_ref[...]
        vtd_ref[...] = vcd_ref[...]


def _in_tile(refs, rope, emit_cache, prev_layers):
    it = iter(refs)
    (x_ref, mod_ref, g1_ref, wt_ref, gq_ref, gk_ref, gmat_ref, up_ref, ub_ref) = (
        next(it) for _ in range(N_IN_BASE))
    if rope:
        cos_ref, sin_ref = next(it), next(it)
    prev_refs = [[next(it) for _ in range(4)] for _ in range(prev_layers or 0)]
    (qg_ref, qd_ref, kdg_ref, kdd_ref, vtg_ref, vtd_ref,
     lq_ref, lk_ref, lv_ref, lgo_ref, la_ref, gate_ref) = (next(it) for _ in range(12))
    if emit_cache:
        ck_g_ref, ck_d_ref, cv_g_ref, cv_d_ref = (next(it) for _ in range(4))

    tm = x_ref.shape[0]
    bpt = kdg_ref.shape[0]
    nt = tm // bpt
    x = x_ref[...]
    ms = jnp.mean(x * x, axis=-1, keepdims=True)
    h = x * lax.rsqrt(ms + EPS) * g1_ref[...]
    h = h * (1.0 + mod_ref[0, 1:2, :]) + mod_ref[0, 0:1, :]
    hb = h.astype(BF16)

    def proj(r0, n):
        return lax.dot_general(hb, wt_ref[r0:r0 + n, :], NT_DIMS, preferred_element_type=F32)

    lane = lax.broadcasted_iota(jnp.int32, (tm, LANES), 1)
    first16 = (lane % 32) < 16
    gmat = gmat_ref[...]
    if rope:
        cos, sin = cos_ref[...], sin_ref[...]

    def blocks128(parts):
        out = []
        for p in parts:
            for c in range(0, p.shape[1], LANES):
                out.append(p[:, c:c + LANES])
        return out

    def store_queries(z, gain, dst):
        for j, zb in enumerate(blocks128(_head_norm(z, gain, gmat))):
            if rope:
                zb = _rope_block(zb, cos, sin, first16)
            dst[:, j * LANES:(j + 1) * LANES] = zb.astype(BF16)

    def store_cache(dst, c0, value):
        w = value.shape[1]
        if prev_layers is None:
            dst[:, c0:c0 + w] = value
        else:
            for bb in range(bpt):
                dst[bb, prev_layers, :, c0:c0 + w] = value[bb * nt:(bb + 1) * nt, :]

    def store_keys(z, gain, dst, cache_dst):
        for j, zb in enumerate(blocks128(_head_norm(z, gain, gmat))):
            if rope:
                zb = _rope_block(zb, cos, sin, first16)
            if emit_cache:
                store_cache(cache_dst, j * LANES, zb)
            sw = pltpu.roll(zb, HEAD_DIM, 1)
            for bb in range(bpt):
                dst[bb, 2 * j] = zb[bb * nt:(bb + 1) * nt, 0:HEAD_DIM].astype(BF16)
                dst[bb, 2 * j + 1] = sw[bb * nt:(bb + 1) * nt, 0:HEAD_DIM].astype(BF16)

    def store_gate(b):
        gate_ref[:, b * D_MODEL:(b + 1) * D_MODEL] = _sigmoid(proj(R_MG + b * D_MODEL, D_MODEL)).astype(BF16)

    z_gq = proj(R_GQ, N_QG)
    z_dq = proj(R_DQ, N_QG)
    store_gate(0)
    store_queries(z_gq, gq_ref[:, 0:N_QG], qg_ref)
    z_kv = proj(R_KV, 2 * LANES)
    store_queries(z_dq, gq_ref[:, N_QG:2 * N_QG], qd_ref)
    z_dk = proj(R_DK, BRANCH_W)
    store_gate(1)
    store_keys(z_kv[:, 0:LANES], gk_ref[:, 0:LANES], kdg_ref, ck_g_ref if emit_cache else None)
    z_dv = proj(R_DV, BRANCH_W)
    store_keys(z_dk, gk_ref[:, LANES:LANES + BRANCH_W], kdd_ref, ck_d_ref if emit_cache else None)
    zl = proj(R_GLA, R_LR - R_GLA)
    z_lgo = proj(R_LGO, BRANCH_W)
    lr = proj(R_LR, LANES)
    store_gate(2)

    ones_rows = jnp.where(lax.broadcasted_iota(jnp.int32, (BF16_ROWS, nt), 0) == 0, 1.0, 0.0).astype(BF16)
    z_gv = z_kv[:, LANES:2 * LANES]
    if emit_cache:
        store_cache(cv_g_ref, 0, z_gv)
        store_cache(cv_d_ref, 0, z_dv)
        for layer, layer_refs in enumerate(prev_refs):
            for dst, src in zip((ck_g_ref, ck_d_ref, cv_g_ref, cv_d_ref), layer_refs):
                for bb in range(bpt):
                    dst[bb, layer] = src[bb * nt:(bb + 1) * nt, :]
    for bb in range(bpt):
        rows = slice(bb * nt, (bb + 1) * nt)
        vt = z_gv[rows, :].T
        for hh in range(GQA_KV_HEADS):
            vtg_ref[bb, hh, 0:HEAD_DIM, :] = vt[hh * HEAD_DIM:(hh + 1) * HEAD_DIM, :].astype(BF16)
            vtg_ref[bb, hh, HEAD_DIM:GQA_VT_ROWS, :] = ones_rows
        for hh in range(DIFF_HEADS):
            vt = z_dv[rows, hh * LANES:(hh + 1) * LANES].T
            vtd_ref[bb, hh, 0:DIFF_DV, :] = vt.astype(BF16)
            vtd_ref[bb, hh, DIFF_DV:DIFF_VT_ROWS, :] = ones_rows

    o = 0
    lq_ref[...] = (zl[:, o:o + GLA_W] * (GLA_DK ** -0.5)).astype(BF16)
    o += GLA_W
    lk_ref[...] = zl[:, o:o + GLA_W].astype(BF16)
    o += GLA_W
    lv_ref[...] = zl[:, o:o + BRANCH_W].astype(BF16)
    o += BRANCH_W
    lgo_ref[...] = z_lgo.astype(BF16)
    lr_hi, lr_mid, _ = _bf16_terms(lr)
    z = (jnp.dot(lr_hi, up_ref[0], preferred_element_type=F32)
         + jnp.dot(lr_mid, up_ref[0], preferred_element_type=F32)
         + jnp.dot(lr_hi, up_ref[1], preferred_element_type=F32)) + ub_ref[...]
    log_sig = jnp.minimum(z, 0.0) - jnp.log(1.0 + jnp.exp(-jnp.abs(z)))
    la_ref[...] = log_sig * (1.0 / GLA_TAU)


def _in_projection(x2d, mod, lw, sw, l, n_batch, n_tok, cache_kv, rope_tabs, emit_cache, prev_caches, tm):
    n_cache = 0 if cache_kv is None else cache_kv[0].shape[2]
    assert n_cache % tm == 0
    n_keys = n_tok + n_cache
    t_total = n_batch * n_tok
    bpt = max(1, tm // n_tok)
    assert bpt == 1 or (mod.shape[0] == 1 and n_cache == 0 and tm == bpt * n_tok)
    nt = tm // bpt
    tpb = n_tok // nt
    cache_tiles = n_cache // tm
    rope = rope_tabs is not None
    own = lambda j: jnp.minimum(j, tpb - 1)
    tile = lambda c: pl.BlockSpec((tm, c), lambda b, j: (b * tpb + own(j), 0))
    in_specs, args = [], []
    if cache_tiles:
        past = lambda j: jnp.maximum(j - tpb, 0)
        in_specs += [
            pl.BlockSpec((1, GQA_KV_HEADS, tm, HEAD_DIM), lambda b, j: (b, 0, past(j), 0)),
            pl.BlockSpec((1, 2 * DIFF_HEADS, tm, HEAD_DIM), lambda b, j: (b, 0, past(j), 0)),
            pl.BlockSpec((1, GQA_KV_HEADS, GQA_VT_ROWS, tm), lambda b, j: (b, 0, 0, past(j))),
            pl.BlockSpec((1, DIFF_HEADS, DIFF_VT_ROWS, tm), lambda b, j: (b, 0, 0, past(j))),
        ]
        args += list(cache_kv)
    in_specs += [
        tile(D_MODEL),
        pl.BlockSpec((1, 6, D_MODEL), lambda b, j: (b if mod.shape[0] > 1 else 0, 0, 0)),
        _full_spec((1, D_MODEL)),
        _layer_spec((N_IN, D_MODEL), l),
        _full_spec((1, 2 * N_QG)),
        _full_spec((1, LANES + BRANCH_W)),
        _full_spec((2 * LANES, 2 * LANES)),
        _full_spec((2, LANES, 2 * GLA_W)),
        _full_spec((1, 2 * GLA_W)),
    ]
    args += [x2d, mod, lw["norm1"], sw["w_in_t"], lw["gq"], lw["gk"], lw["gmat"], lw["up"], lw["ub"]]
    if rope:
        in_specs += [pl.BlockSpec((tm, LANES), lambda b, j: (own(j), 0))] * 2
        args += list(rope_tabs)
    sd = jax.ShapeDtypeStruct
    out_shape = [
        sd((t_total, BRANCH_W), BF16), sd((t_total, BRANCH_W), BF16),
        sd((n_batch, GQA_KV_HEADS, n_keys, HEAD_DIM), BF16), sd((n_batch, 2 * DIFF_HEADS, n_keys, HEAD_DIM), BF16),
        sd((n_batch, GQA_KV_HEADS, GQA_VT_ROWS, n_keys), BF16), sd((n_batch, DIFF_HEADS, DIFF_VT_ROWS, n_keys), BF16),
        sd((t_total, GLA_W), BF16), sd((t_total, GLA_W), BF16),
        sd((t_total, BRANCH_W), BF16), sd((t_total, BRANCH_W), BF16),
        sd((t_total, 2 * GLA_W), F32), sd((t_total, N_BRANCH * D_MODEL), BF16),
    ]
    out_specs = [
        tile(BRANCH_W), tile(BRANCH_W),
        pl.BlockSpec((bpt, GQA_KV_HEADS, nt, HEAD_DIM), lambda b, j: (b, 0, j, 0)),
        pl.BlockSpec((bpt, 2 * DIFF_HEADS, nt, HEAD_DIM), lambda b, j: (b, 0, j, 0)),
        pl.BlockSpec((bpt, GQA_KV_HEADS, GQA_VT_ROWS, nt), lambda b, j: (b, 0, 0, j)),
        pl.BlockSpec((bpt, DIFF_HEADS, DIFF_VT_ROWS, nt), lambda b, j: (b, 0, 0, j)),
        tile(GLA_W), tile(GLA_W), tile(BRANCH_W), tile(BRANCH_W), tile(2 * GLA_W), tile(N_BRANCH * D_MODEL),
    ]
    cache_widths = (LANES, BRANCH_W, LANES, BRANCH_W)
    prev_layers = None
    if emit_cache and prev_caches is None:
        out_shape += [sd((t_total, w), F32) for w in cache_widths]
        out_specs += [tile(w) for w in cache_widths]
    elif emit_cache:
        prev_layers = len(prev_caches)
        n_layers = prev_layers + 1
        for layer_arrays in prev_caches:
            in_specs += [tile(w) for w in cache_widths]
            args += list(layer_arrays)
        out_shape += [sd((n_batch, n_layers, n_tok, w), F32) for w in cache_widths]
        out_specs += [pl.BlockSpec((bpt, n_layers, nt, w), lambda b, j: (b, 0, j, 0)) for w in cache_widths]
    return pl.pallas_call(
        functools.partial(_in_kernel, rope=rope, emit_cache=emit_cache, prev_layers=prev_layers,
                          tiles_per_batch=tpb, cache_tiles=cache_tiles),
        grid=(n_batch // bpt, tpb + cache_tiles),
        in_specs=in_specs,
        out_specs=out_specs,
        out_shape=out_shape,
        compiler_params=pltpu.CompilerParams(
            dimension_semantics=("parallel", "arbitrary"), vmem_limit_bytes=VMEM_LIMIT),
        name="in_projection",
    )(*args)


MAX_COL = 4 * LANES
SCORE_BOUND = 45.0
NORM_MARGIN = 1.05


def _head_queries_t(q_blk, low):
    t = q_blk.astype(F32).T
    return (t[0:HEAD_DIM, :] if low else t[HEAD_DIM:2 * HEAD_DIM, :]).astype(BF16)


def _key_norm2(kd):
    kf = kd.astype(F32)
    return jnp.max(jnp.sum(kf * kf, axis=1, keepdims=True))


def _needs_no_stabiliser(q_norm2, k_norm2):
    return (q_norm2 * k_norm2 <= SCORE_BOUND * SCORE_BOUND).astype(jnp.int32)


KEY_SUB = 512


def _score_units(k_ref, heads_cols):
    tk = k_ref.shape[2]
    sub = min(KEY_SUB, tk)
    return sub, [(kh, vh, s0, c) for s0 in range(0, tk, sub) for kh, vh, cols in heads_cols for c in cols]


def _tile_scores(k_ref, qm_ref, unit, sub, col):
    kh, _, s0, c = unit
    return jnp.dot(k_ref[0, kh, s0:s0 + sub, :], qm_ref[:, c * col:(c + 1) * col], preferred_element_type=F32)


def _plain_tiles(k_ref, vt_ref, qm_ref, acc_ref, heads_cols):
    col = acc_ref.shape[-1]
    sub, units = _score_units(k_ref, heads_cols)
    s_next = _tile_scores(k_ref, qm_ref, units[0], sub, col)
    for i, (_, vh, s0, c) in enumerate(units):
        s = s_next
        if i + 1 < len(units):
            s_next = _tile_scores(k_ref, qm_ref, units[i + 1], sub, col)
        p = jnp.exp2(s).astype(BF16)
        acc_ref[c] += jnp.dot(vt_ref[0, vh, :, s0:s0 + sub], p, preferred_element_type=F32)


def _online_tiles(k_ref, vt_ref, qm_ref, m_ref, acc_ref, heads_cols):
    col = acc_ref.shape[-1]
    sub, units = _score_units(k_ref, heads_cols)
    for unit in units:
        _, vh, s0, c = unit
        s = _tile_scores(k_ref, qm_ref, unit, sub, col)
        m_prev = m_ref[c]
        m_new = jnp.maximum(m_prev, jnp.max(s, axis=0, keepdims=True))
        alpha = jnp.exp2(m_prev - m_new)
        p = jnp.exp2(s - m_new).astype(BF16)
        acc_ref[c] = acc_ref[c] * alpha + jnp.dot(vt_ref[0, vh, :, s0:s0 + sub], p, preferred_element_type=F32)
        m_ref[c] = m_new


def _flash_init(kb_ref, kc_ref, n_key_heads, m_ref, acc_ref, plain_ref):
    k_norm2 = kb_ref[1]
    if kc_ref is not None:
        for kh in range(n_key_heads):
            k_norm2 = jnp.maximum(k_norm2, _key_norm2(kc_ref[0, kh]))
    plain_ref[0] = _needs_no_stabiliser(kb_ref[0], k_norm2)
    m_ref[...] = jnp.full(m_ref.shape, -jnp.inf, F32)
    acc_ref[...] = jnp.zeros(acc_ref.shape, F32)


def _flash_step(k_ref, vt_ref, qm_ref, m_ref, acc_ref, plain_ref, heads_cols):
    @pl.when(plain_ref[0] == 1)
    def _():
        _plain_tiles(k_ref, vt_ref, qm_ref, acc_ref, heads_cols)

    @pl.when(plain_ref[0] != 1)
    def _():
        _online_tiles(k_ref, vt_ref, qm_ref, m_ref, acc_ref, heads_cols)


def _gqa_kernel(*refs, has_cache, tq, hp):
    if has_cache:
        kb_ref, q_ref, k_ref, vt_ref, kc_ref, o_ref, qm_ref, m_ref, acc_ref, plain_ref = refs
    else:
        kb_ref, q_ref, k_ref, vt_ref, o_ref, qm_ref, m_ref, acc_ref, plain_ref = refs
        kc_ref = None
    kt = pl.program_id(3)
    col = acc_ref.shape[-1]
    per_head = tq // col
    per_kv = GQA_GROUP * per_head
    heads_cols = [(hh, hh, range(hh * per_kv, (hh + 1) * per_kv)) for hh in range(hp)]

    @pl.when(kt == 0)
    def _():
        for hh in range(hp):
            for g in range(GQA_GROUP):
                c0 = hh * 2 * LANES + (g // 2) * LANES
                row0 = (hh * GQA_GROUP + g) * tq
                qm_ref[:, row0:row0 + tq] = _head_queries_t(q_ref[0, :, c0:c0 + LANES], g % 2 == 0)
        _flash_init(kb_ref, kc_ref, hp, m_ref, acc_ref, plain_ref)

    _flash_step(k_ref, vt_ref, qm_ref, m_ref, acc_ref, plain_ref, heads_cols)

    @pl.when(kt == pl.num_programs(3) - 1)
    def _():
        for hh in range(hp):
            for part in range(per_head):
                heads = []
                for g in range(GQA_GROUP):
                    acc = acc_ref[hh * per_kv + g * per_head + part]
                    heads.append(acc[0:HEAD_DIM, :] / acc[HEAD_DIM:HEAD_DIM + 1, :])
                o_ref[0, part * col:(part + 1) * col, hh * 2 * LANES:(hh + 1) * 2 * LANES] = (
                    jnp.concatenate(heads, axis=0).T.astype(BF16))


def _kv_specs(key_heads_blk, val_heads_blk, vt_rows, tk, kd_c):
    specs = [pl.BlockSpec((1, key_heads_blk, tk, HEAD_DIM), lambda b, h, qi, kt: (b, h, kt, 0)),
             pl.BlockSpec((1, val_heads_blk, vt_rows, tk), lambda b, h, qi, kt: (b, h, 0, kt))]
    if kd_c is not None:
        specs.append(pl.BlockSpec((1, key_heads_blk, kd_c.shape[2], HEAD_DIM), lambda b, h, qi, kt: (b, h, 0, 0)))
    return specs


def _gqa_attention(q, k_bound, kd, vt, kd_c, n_batch, n_tok, tq, tk, hp):
    has_cache = kd_c is not None
    nk = kd.shape[2] // tk
    rows = hp * GQA_GROUP * tq
    col = min(MAX_COL, tq)
    q_spec = pl.BlockSpec((1, tq, hp * 2 * LANES), lambda b, h, qi, kt: (b, qi, h))
    in_specs = [pl.BlockSpec(memory_space=pltpu.SMEM), q_spec]
    in_specs += _kv_specs(hp, hp, GQA_VT_ROWS, tk, kd_c)
    args = [k_bound, q.reshape(n_batch, n_tok, BRANCH_W), kd, vt]
    if has_cache:
        args.append(kd_c)
    out = pl.pallas_call(
        functools.partial(_gqa_kernel, has_cache=has_cache, tq=tq, hp=hp),
        grid=(n_batch, GQA_KV_HEADS // hp, n_tok // tq, nk),
        in_specs=in_specs,
        out_specs=q_spec,
        out_shape=jax.ShapeDtypeStruct((n_batch, n_tok, BRANCH_W), BF16),
        scratch_shapes=[
            pltpu.VMEM((HEAD_DIM, rows), BF16),
            pltpu.VMEM((rows // col, 1, col), F32),
            pltpu.VMEM((rows // col, GQA_VT_ROWS, col), F32),
            pltpu.SMEM((1,), jnp.int32),
        ],
        compiler_params=pltpu.CompilerParams(
            dimension_semantics=("parallel", "parallel", "parallel", "arbitrary"),
            vmem_limit_bytes=VMEM_LIMIT),
        name="gqa_attention",
    )(*args)
    return out.reshape(n_batch * n_tok, BRANCH_W)


def _diff_kernel(*refs, has_cache, lam_init, hp):
    if has_cache:
        (kb_ref, q_ref, lam_ref, gsub_ref, k_ref, vt_ref, kc_ref, o_ref,
         qm_ref, m_ref, acc_ref, plain_ref) = refs
    else:
        kb_ref, q_ref, lam_ref, gsub_ref, k_ref, vt_ref, o_ref, qm_ref, m_ref, acc_ref, plain_ref = refs
        kc_ref = None
    kt = pl.program_id(3)
    tq = q_ref.shape[1]
    col = acc_ref.shape[-1]
    per_map = tq // col
    heads_cols = [(2 * hh + mm, hh, range((2 * hh + mm) * per_map, (2 * hh + mm + 1) * per_map))
                  for hh in range(hp) for mm in range(2)]

    @pl.when(kt == 0)
    def _():
        for hh in range(hp):
            blk = q_ref[0, :, hh * LANES:(hh + 1) * LANES]
            for mm in range(2):
                row0 = (2 * hh + mm) * tq
                qm_ref[:, row0:row0 + tq] = _head_queries_t(blk, mm == 0)
        _flash_init(kb_ref, kc_ref, 2 * hp, m_ref, acc_ref, plain_ref)

    _flash_step(k_ref, vt_ref, qm_ref, m_ref, acc_ref, plain_ref, heads_cols)

    @pl.when(kt == pl.num_programs(3) - 1)
    def _():
        lp = lam_ref[...]
        lam = (jnp.exp(jnp.sum(lp[0:1] * lp[1:2], axis=-1, keepdims=True))
               - jnp.exp(jnp.sum(lp[2:3] * lp[3:4], axis=-1, keepdims=True)) + lam_init)
        for hh in range(hp):
            for part in range(per_map):
                a0, a1 = acc_ref[2 * hh * per_map + part], acc_ref[(2 * hh + 1) * per_map + part]
                o0 = a0[0:DIFF_DV, :] / a0[DIFF_DV:DIFF_DV + 1, :]
                o1 = a1[0:DIFF_DV, :] / a1[DIFF_DV:DIFF_DV + 1, :]
                d = (o0 - lam * o1).T
                ms = jnp.mean(d * d, axis=-1, keepdims=True)
                o_ref[0, part * col:(part + 1) * col, hh * LANES:(hh + 1) * LANES] = (
                    d * lax.rsqrt(ms + EPS) * gsub_ref[...] * (1.0 - lam_init)).astype(BF16)


def _diff_attention(q, k_bound, lam_p, gsub, kd, vt, kd_c, n_batch, n_tok, tq, tk, hp, lam_init):
    has_cache = kd_c is not None
    nk = kd.shape[2] // tk
    col = min(MAX_COL, tq)
    rows = hp * 2 * tq
    q_spec = pl.BlockSpec((1, tq, hp * LANES), lambda b, h, qi, kt: (b, qi, h))
    in_specs = [
        pl.BlockSpec(memory_space=pltpu.SMEM),
        q_spec,
        pl.BlockSpec((4, HEAD_DIM), lambda b, h, qi, kt: (0, 0)),
        pl.BlockSpec((1, DIFF_DV), lambda b, h, qi, kt: (0, 0)),
    ]
    in_specs += _kv_specs(2 * hp, hp, DIFF_VT_ROWS, tk, kd_c)
    args = [k_bound, q.reshape(n_batch, n_tok, BRANCH_W), lam_p, gsub, kd, vt]
    if has_cache:
        args.append(kd_c)
    out = pl.pallas_call(
        functools.partial(_diff_kernel, has_cache=has_cache, lam_init=lam_init, hp=hp),
        grid=(n_batch, DIFF_HEADS // hp, n_tok // tq, nk),
        in_specs=in_specs,
        out_specs=q_spec,
        out_shape=jax.ShapeDtypeStruct((n_batch, n_tok, BRANCH_W), BF16),
        scratch_shapes=[
            pltpu.VMEM((HEAD_DIM, rows), BF16),
            pltpu.VMEM((rows // col, 1, col), F32),
            pltpu.VMEM((rows // col, DIFF_VT_ROWS, col), F32),
            pltpu.SMEM((1,), jnp.int32),
        ],
        compiler_params=pltpu.CompilerParams(
            dimension_semantics=("parallel", "parallel", "parallel", "arbitrary"),
            vmem_limit_bytes=VMEM_LIMIT),
        name="diff_attention",
    )(*args)
    return out.reshape(n_batch * n_tok, BRANCH_W)


def _gla_kernel(*refs, has_s0, n_chunk, bg):
    if has_s0:
        qf, kf, vf, laf, qb, kb, vb, lab, s0_ref, of_ref, ob_ref, sfin_ref, st_ref = refs
    else:
        qf, kf, vf, laf, qb, kb, vb, lab, of_ref, ob_ref, sfin_ref, st_ref = refs
    i = pl.program_id(1)
    ck = GLA_CHUNK

    @pl.when(i == 0)
    def _():
        for bb in range(bg):
            for d in range(2):
                for hd in range(GLA_HEADS):
                    if has_s0:
                        s = s0_ref[bb, d, hd]
                        z = jnp.zeros_like(s)
                        padded = jnp.concatenate([s, z] if hd % 2 == 0 else [z, s], axis=0)
                        st_ref[bb, d, hd] = padded.T
                    else:
                        st_ref[bb, d, hd] = jnp.zeros((GLA_DV, LANES), F32)

    tb = qf.shape[1]
    r = lax.broadcasted_iota(jnp.int32, (tb, tb), 0)
    c = lax.broadcasted_iota(jnp.int32, (tb, tb), 1)
    lane = lax.broadcasted_iota(jnp.int32, (tb, LANES), 1)
    chunk_id = lax.broadcasted_iota(jnp.int32, (tb, GLA_W), 0) // ck
    zero_row = jnp.zeros((1, GLA_W), F32)
    streams = []
    for bb in range(bg):
        streams.append((bb, 0, qf.at[bb], kf.at[bb], vf.at[bb], laf.at[bb], of_ref.at[bb], c <= r))
        streams.append((bb, 1, qb.at[bb], kb.at[bb], vb.at[bb], lab.at[bb], ob_ref.at[bb], c >= r))
    nt = (((1,), (1,)), ((), ()))
    g_all = [sum(jnp.dot(tri.astype(F32).astype(BF16), part, preferred_element_type=F32)
                 for part in _bf16_terms(la_r[...]))
             for (_, _, _, _, _, la_r, _, tri) in streams]
    prepared = []
    for g, (bb, d, q_r, k_r, v_r, la_r, o_r, tri) in zip(g_all, streams):
        if d == 0:
            bounds = [zero_row] + [g[ck * j - 1:ck * j, :] for j in range(1, n_chunk)]
            g_end = g[tb - 1:tb, :]
        else:
            bounds = [g[ck * (j + 1):ck * (j + 1) + 1, :] for j in range(n_chunk - 1)] + [zero_row]
            g_end = g[0:1, :]
        b_rows = jnp.concatenate([jnp.broadcast_to(b, (ck, GLA_W)) for b in bounds], axis=0)
        q = q_r[...].astype(F32)
        k = k_r[...].astype(F32)
        q_dec = q * jnp.exp(g - b_rows)
        q_glob = q * jnp.exp(g)
        k_end = k * jnp.exp(g_end - g)
        k_rel = []
        for j in range(n_chunk):
            reach = (chunk_id <= j) if d == 0 else (chunk_id >= j)
            k_rel.append((k * jnp.exp(jnp.where(reach, bounds[j] - g, 0.0))).astype(BF16))
        prepared.append((q_dec, q_glob, k_end, k_rel, jnp.exp(g_end), v_r[...]))

    partial = []
    for (bb, d, *_), (q_dec, q_glob, k_end, k_rel, decay, v) in zip(streams, prepared):
        per_head = []
        for hd in range(GLA_HEADS):
            pair = slice((hd // 2) * LANES, (hd // 2 + 1) * LANES)
            hv = slice(hd * GLA_DV, (hd + 1) * GLA_DV)
            keep = (lane < GLA_DK) if hd % 2 == 0 else (lane >= GLA_DK)
            qd_m = jnp.where(keep, q_dec[:, pair], 0.0).astype(BF16)
            qg_m = jnp.where(keep, q_glob[:, pair], 0.0).astype(BF16)
            ke_m = jnp.where(keep, k_end[:, pair], 0.0).astype(BF16)
            a_rows = [lax.dot_general(qd_m[ck * j:ck * (j + 1), :], k_rel[j][:, pair], nt,
                                      preferred_element_type=F32) for j in range(n_chunk)]
            s_t = st_ref[bb, d, hd]
            o_state = lax.dot_general(qg_m, s_t.astype(BF16), nt, preferred_element_type=F32)
            ds_t = lax.dot_general(v[:, hv], ke_m, (((0,), (0,)), ((), ())), preferred_element_type=F32)
            st_ref[bb, d, hd] = s_t * decay[:, pair] + ds_t
            per_head.append((a_rows, o_state))
        partial.append(per_head)

    for (bb, d, _, _, _, _, o_r, tri), prep, per_head in zip(streams, prepared, partial):
        v = prep[5]
        for hd in range(GLA_HEADS):
            hv = slice(hd * GLA_DV, (hd + 1) * GLA_DV)
            a_rows, o_state = per_head[hd]
            a = jnp.where(tri, jnp.concatenate(a_rows, axis=0), 0.0).astype(BF16)
            o_r[:, hv] = (jnp.dot(a, v[:, hv], preferred_element_type=F32) + o_state).astype(BF16)

    @pl.when(i == pl.num_programs(1) - 1)
    def _():
        for bb in range(bg):
            for d in range(2):
                for hd in range(GLA_HEADS):
                    t = st_ref[bb, d, hd].T
                    sfin_ref[bb, d, hd] = t[(hd % 2) * GLA_DK:(hd % 2 + 1) * GLA_DK, :]


def _gla(lq, lk, lv, la, s0, n_batch, n_tok, tb, bg):
    nb = n_tok // tb
    has_s0 = s0 is not None
    as3d = lambda x: x.reshape(n_batch, n_tok, x.shape[-1])
    fwd = lambda b, i: (b, i, 0)
    bwd = lambda b, i: (b, nb - 1 - i, 0)
    bwd_la = lambda b, i: (b, nb - 1 - i, 1)
    in_specs = [
        pl.BlockSpec((bg, tb, GLA_W), fwd), pl.BlockSpec((bg, tb, GLA_W), fwd),
        pl.BlockSpec((bg, tb, BRANCH_W), fwd), pl.BlockSpec((bg, tb, GLA_W), fwd),
        pl.BlockSpec((bg, tb, GLA_W), bwd), pl.BlockSpec((bg, tb, GLA_W), bwd),
        pl.BlockSpec((bg, tb, BRANCH_W), bwd), pl.BlockSpec((bg, tb, GLA_W), bwd_la),
    ]
    args = [as3d(lq), as3d(lk), as3d(lv), as3d(la)] * 2
    state_spec = pl.BlockSpec((bg, 2, GLA_HEADS, GLA_DK, GLA_DV), lambda b, i: (b, 0, 0, 0, 0))
    if has_s0:
        in_specs.append(state_spec)
        args.append(s0)
    o_f, o_b, s_fin = pl.pallas_call(
        functools.partial(_gla_kernel, has_s0=has_s0, n_chunk=tb // GLA_CHUNK, bg=bg),
        grid=(n_batch // bg, nb),
        in_specs=in_specs,
        out_specs=[pl.BlockSpec((bg, tb, BRANCH_W), fwd), pl.BlockSpec((bg, tb, BRANCH_W), bwd), state_spec],
        out_shape=[
            jax.ShapeDtypeStruct((n_batch, n_tok, BRANCH_W), BF16),
            jax.ShapeDtypeStruct((n_batch, n_tok, BRANCH_W), BF16),
            jax.ShapeDtypeStruct((n_batch, 2, GLA_HEADS, GLA_DK, GLA_DV), F32),
        ],
        scratch_shapes=[pltpu.VMEM((bg, 2, GLA_HEADS, GLA_DV, LANES), F32)],
        compiler_params=pltpu.CompilerParams(
            dimension_semantics=("parallel", "arbitrary"), vmem_limit_bytes=VMEM_LIMIT),
        name="gla",
    )(*args)
    flat = lambda x: x.reshape(n_batch * n_tok, BRANCH_W)
    return flat(o_f), flat(o_b), s_fin


FFN_CHUNK = 256


def _merge_ffn_kernel(x_ref, mod_ref, og_ref, of_ref, ob_ref, lgo_ref, od_ref, gate_ref,
                      gout_ref, wb_ref, wo_ref, g2_ref, wi_ref, wd_ref, y_ref):
    o_gla = of_ref[...].astype(F32) + ob_ref[...].astype(F32)
    gla_parts = []
    for hd in range(GLA_HEADS):
        blk = o_gla[:, hd * GLA_DV:(hd + 1) * GLA_DV]
        ms = jnp.mean(blk * blk, axis=-1, keepdims=True)
        gla_parts.append(blk * lax.rsqrt(ms + EPS) * gout_ref[...])
    gla = jnp.concatenate(gla_parts, axis=1) * _silu(lgo_ref[...].astype(F32))
    branches = (og_ref[...], gla.astype(BF16), od_ref[...])
    mixed = None
    for b, ob in enumerate(branches):
        y = jnp.dot(ob, wb_ref[b], preferred_element_type=F32)
        y = y * gate_ref[:, b * D_MODEL:(b + 1) * D_MODEL].astype(F32)
        mixed = y if mixed is None else mixed + y
    out = jnp.dot(mixed.astype(BF16), wo_ref[...], preferred_element_type=F32)
    x = x_ref[...] + mod_ref[0, 2:3, :] * out

    ms = jnp.mean(x * x, axis=-1, keepdims=True)
    h = x * lax.rsqrt(ms + EPS) * g2_ref[...]
    h = h * (1.0 + mod_ref[0, 4:5, :]) + mod_ref[0, 3:4, :]
    hb = h.astype(BF16)

    def up_proj(c0):
        a = jnp.dot(hb, wi_ref[:, c0:c0 + FFN_CHUNK], preferred_element_type=F32)
        u = jnp.dot(hb, wi_ref[:, FFN_HIDDEN + c0:FFN_HIDDEN + c0 + FFN_CHUNK], preferred_element_type=F32)
        return a, u

    acc = None
    chunks = list(range(0, FFN_HIDDEN, FFN_CHUNK))
    nxt = up_proj(chunks[0])
    for i, c0 in enumerate(chunks):
        a, u = nxt
        if i + 1 < len(chunks):
            nxt = up_proj(chunks[i + 1])
        act = (_silu(a) * u).astype(BF16)
        part = jnp.dot(act, wd_ref[c0:c0 + FFN_CHUNK, :], preferred_element_type=F32)
        acc = part if acc is None else acc + part
    y_ref[...] = x + mod_ref[0, 5:6, :] * acc


def _merge_ffn(x2d, mod, og, o_f, o_b, lgo, od, gates, lw, sw, l, tiles_per_mod, tm):
    t_total = x2d.shape[0]
    tile = lambda c: pl.BlockSpec((tm, c), lambda i: (i, 0))
    return pl.pallas_call(
        _merge_ffn_kernel,
        grid=(t_total // tm,),
        in_specs=[
            tile(D_MODEL),
            pl.BlockSpec((1, 6, D_MODEL), lambda i: (i // tiles_per_mod, 0, 0)),
            tile(BRANCH_W), tile(BRANCH_W), tile(BRANCH_W), tile(BRANCH_W), tile(BRANCH_W),
            tile(N_BRANCH * D_MODEL),
            _full_spec((1, GLA_DV)),
            _layer_spec((N_BRANCH, BRANCH_W, D_MODEL), l),
            _layer_spec((D_MODEL, D_MODEL), l),
            _full_spec((1, D_MODEL)),
            _layer_spec((D_MODEL, 2 * FFN_HIDDEN), l),
            _layer_spec((FFN_HIDDEN, D_MODEL), l),
        ],
        out_specs=tile(D_MODEL),
        out_shape=jax.ShapeDtypeStruct((t_total, D_MODEL), F32),
        compiler_params=pltpu.CompilerParams(
            dimension_semantics=("parallel",), vmem_limit_bytes=VMEM_LIMIT),
        name="merge_ffn",
    )(x2d, mod, og, o_f, o_b, lgo, od, gates, lw["gout"], sw["w_branch"], sw["w_out"],
      lw["norm2"], sw["w_ffn_in"], sw["w_ffn_out"])


def _shared_weights(p):
    return {
        "w_in_t": jnp.swapaxes(p["w_in"], 1, 2).astype(BF16),
        "w_branch": p["w_branch"].astype(BF16), "w_out": p["w_out"].astype(BF16),
        "w_ffn_in": p["w_ffn_in"].astype(BF16), "w_ffn_out": p["w_ffn_out"].astype(BF16),
    }


def _layer_weights(l, p):
    scale = HEAD_DIM ** -0.5 * math.log2(math.e)
    head_bound = lambda g, s: HEAD_DIM * (NORM_MARGIN * s) ** 2 * jnp.max(g * g)
    norm_bounds = lambda gq_, gk_: jnp.stack([head_bound(gq_, scale), head_bound(gk_, 1.0)])
    gq_row = jnp.concatenate([jnp.tile(p["gqa_q_norm"][l], GQA_HEADS),
                              jnp.tile(p["diff_q_norm"][l], 2 * DIFF_HEADS)]) * scale
    gk_row = jnp.concatenate([jnp.tile(p["gqa_k_norm"][l], GQA_KV_HEADS),
                              jnp.tile(p["diff_k_norm"][l], 2 * DIFF_HEADS)])
    idx = jnp.arange(2 * LANES) // HEAD_DIM
    gmat = jnp.where(idx[:, None] == idx[None, :], 1.0 / HEAD_DIM, 0.0).astype(BF16)
    up = jnp.zeros((LANES, 2 * GLA_W), F32)
    up = up.at[0:GLA_RANK, 0:GLA_W].set(p["gla_alpha_up"][l, 0])
    up = up.at[GLA_RANK:2 * GLA_RANK, GLA_W:].set(p["gla_alpha_up"][l, 1])
    ub = p["gla_alpha_bias"][l].reshape(1, 2 * GLA_W)
    up_hi = up.astype(BF16)
    up = jnp.stack([up_hi, (up - up_hi.astype(F32)).astype(BF16)])
    return {
        "gq": gq_row.reshape(1, 2 * N_QG), "gk": gk_row.reshape(1, LANES + BRANCH_W), "gmat": gmat,
        "up": up, "ub": ub,
        "kb_g": norm_bounds(p["gqa_q_norm"][l], p["gqa_k_norm"][l]),
        "kb_d": norm_bounds(p["diff_q_norm"][l], p["diff_k_norm"][l]),
        "norm1": p["norm1"][l].reshape(1, D_MODEL), "norm2": p["norm2"][l].reshape(1, D_MODEL),
        "gout": p["gla_out_norm"][l].reshape(1, GLA_DV),
        "gsub": p["diff_sub_norm"][l].reshape(1, DIFF_DV),
        "lam": p["diff_lambda"][l],
    }


def _rope_tables(n_tokens):
    n_rows = n_tokens // GRID_W
    row = jnp.repeat(jnp.arange(n_rows, dtype=F32), GRID_W)
    col = jnp.tile(jnp.arange(GRID_W, dtype=F32), n_rows)
    n_freq = HEAD_DIM // 4
    freqs = ROPE_THETA ** (-jnp.arange(n_freq, dtype=F32) / n_freq)
    ar, ac = row[:, None] * freqs, col[:, None] * freqs
    cos = jnp.concatenate([jnp.cos(ar), jnp.cos(ar), jnp.cos(ac), jnp.cos(ac)], axis=1)
    sin = jnp.concatenate([-jnp.sin(ar), jnp.sin(ar), -jnp.sin(ac), jnp.sin(ac)], axis=1)
    return jnp.tile(cos, (1, LANES // HEAD_DIM)), jnp.tile(sin, (1, LANES // HEAD_DIM))


def _with_ones_rows(v_t):
    lead = v_t.shape[:-2]
    s = v_t.shape[-1]
    ones = jnp.ones(lead + (1, s), v_t.dtype)
    zeros = jnp.zeros(lead + (BF16_ROWS - 1, s), v_t.dtype)
    return jnp.concatenate([v_t, ones, zeros], axis=-2).astype(BF16)


def _cache_layouts(l, cache_gqa_k, cache_gqa_v, cache_diff_k, cache_diff_v):
    b = cache_gqa_k.shape[0]
    gk = jnp.transpose(cache_gqa_k[:, l], (0, 2, 1, 3))
    kd_g = gk.astype(BF16)
    vt_g = _with_ones_rows(jnp.transpose(cache_gqa_v[:, l], (0, 2, 3, 1)))
    dk = jnp.transpose(cache_diff_k[:, l], (0, 2, 3, 1, 4)).reshape(b, 2 * DIFF_HEADS, PAST_LEN, HEAD_DIM)
    kd_d = dk.astype(BF16)
    vt_d = _with_ones_rows(jnp.transpose(cache_diff_v[:, l], (0, 2, 3, 1)))
    return kd_g, vt_g, kd_d, vt_d


def _run_layer(x2d, mod, lw, sw, l, n_batch, n_tok, rope_tabs, ctx, emit_cache, prev_caches, cfg):
    tiles_per_mod = lambda tm: (n_tok // tm) if mod.shape[0] > 1 else (n_batch * n_tok // tm)
    if ctx is None:
        cache_kv = kd_gc = kd_dc = s0 = None
    else:
        kd_gc, vt_gc, kd_dc, vt_dc, s0 = ctx
        cache_kv = (kd_gc, kd_dc, vt_gc, vt_dc)
    outs = _in_projection(x2d, mod, lw, sw, l, n_batch, n_tok, cache_kv, rope_tabs, emit_cache, prev_caches,
                          cfg["tm_in"])
    qg, qd, kdg, kdd, vtg, vtd, lq, lk, lv, lgo, la, gates = outs[:12]
    og = _gqa_attention(qg, lw["kb_g"], kdg, vtg, kd_gc, n_batch, n_tok, cfg["tq_gqa"], cfg["tk"],
                        cfg["hp_gqa"])
    lam_init = 0.8 - 0.6 * math.exp(-0.3 * l)
    od = _diff_attention(qd, lw["kb_d"], lw["lam"], lw["gsub"], kdd, vtd, kd_dc, n_batch, n_tok,
                         cfg["tq_diff"], cfg["tk"], cfg["hp_diff"], lam_init)
    o_f, o_b, s_fin = _gla(lq, lk, lv, la, s0, n_batch, n_tok, cfg["tb_gla"], cfg["bg_gla"])
    x2d = _merge_ffn(x2d, mod, og, o_f, o_b, lgo, od, gates, lw, sw, l, tiles_per_mod(cfg["tm"]), cfg["tm"])
    cache = tuple(outs[12:]) + (s_fin,) if emit_cache else None
    return x2d, cache


PROMPT_CFG = dict(tm_in=512, tq_gqa=256, tq_diff=256, tk=256, hp_gqa=2, hp_diff=4, tb_gla=256, bg_gla=2, tm=512)
SAMPLE_CFG = dict(tm_in=512, tq_gqa=1024, tq_diff=2048, tk=1536, hp_gqa=1, hp_diff=1, tb_gla=256, bg_gla=2, tm=512)


def kernel(x_prompt, x_sample, c, cache_gqa_k, cache_gqa_v, state_gla, cache_diff_k, cache_diff_v, c_ctx, w_mod, b_mod, norm1, norm2, w_in, gqa_q_norm, gqa_k_norm, gla_alpha_up, gla_alpha_bias, gla_out_norm, diff_q_norm, diff_k_norm, diff_lambda, diff_sub_norm, w_branch, w_out, w_ffn_in, w_ffn_out):
    p = {
        "norm1": norm1, "norm2": norm2, "w_in": w_in, "gqa_q_norm": gqa_q_norm, "gqa_k_norm": gqa_k_norm,
        "gla_alpha_up": gla_alpha_up, "gla_alpha_bias": gla_alpha_bias, "gla_out_norm": gla_out_norm,
        "diff_q_norm": diff_q_norm, "diff_k_norm": diff_k_norm, "diff_lambda": diff_lambda,
        "diff_sub_norm": diff_sub_norm, "w_branch": w_branch, "w_out": w_out,
        "w_ffn_in": w_ffn_in, "w_ffn_out": w_ffn_out,
    }
    n_ctx_b, n_ctx = x_prompt.shape[:2]
    n_lat_b, n_lat = x_sample.shape[:2]
    cond_rows = jnp.concatenate(
        [c_ctx[None, :], c, jnp.zeros((F32_ROWS - 1 - n_lat_b, D_MODEL), F32)], axis=0)
    mod_all = _modulation(cond_rows, w_mod, b_mod)
    weights = [_layer_weights(l, p) for l in range(DEPTH)]
    shared = _shared_weights(p)

    y = x_prompt.reshape(n_ctx_b * n_ctx, D_MODEL)
    caches = []
    for l in range(DEPTH):
        mod = mod_all[l, 0:1].reshape(1, 6, D_MODEL)
        prev = [cc[:4] for cc in caches] if l == DEPTH - 1 else None
        y, cache = _run_layer(y, mod, weights[l], shared, l, n_ctx_b, n_ctx, None, None, True, prev, PROMPT_CFG)
        caches.append(cache)
    y_prompt = y.reshape(n_ctx_b, n_ctx, D_MODEL)
    stacked = caches[-1]
    new_gqa_k = stacked[0].reshape(n_ctx_b, DEPTH, n_ctx, GQA_KV_HEADS, HEAD_DIM)
    new_diff_k = stacked[1].reshape(n_ctx_b, DEPTH, n_ctx, DIFF_HEADS, 2, HEAD_DIM)
    new_gqa_v = stacked[2].reshape(n_ctx_b, DEPTH, n_ctx, GQA_KV_HEADS, HEAD_DIM)
    new_diff_v = stacked[3].reshape(n_ctx_b, DEPTH, n_ctx, DIFF_HEADS, DIFF_DV)
    new_state_gla = jnp.stack([cc[4] for cc in caches], axis=1)

    rope_tabs = _rope_tables(n_lat)
    y = x_sample.reshape(n_lat_b * n_lat, D_MODEL)
    for l in range(DEPTH):
        mod = mod_all[l, 1:1 + n_lat_b].reshape(n_lat_b, 6, D_MODEL)
        ctx = _cache_layouts(l, cache_gqa_k, cache_gqa_v, cache_diff_k, cache_diff_v) + (state_gla[:, l],)
        y, _ = _run_layer(y, mod, weights[l], shared, l, n_lat_b, n_lat, rope_tabs, ctx, False, None, SAMPLE_CFG)
    y_sample = y.reshape(n_lat_b, n_lat, D_MODEL)
    return (y_prompt, y_sample, new_gqa_k, new_gqa_v, new_state_gla, new_diff_k, new_diff_v)
```

```python
import functools
import math

import jax
import jax.numpy as jnp
from jax import lax
from jax.experimental import pallas as pl
from jax.experimental.pallas import tpu as pltpu

D_MODEL = 1024
DEPTH = 2
PAST_LEN = 512
GRID_W = 64
HEAD_DIM = 64
GQA_HEADS = 8
GQA_KV_HEADS = 2
GQA_GROUP = GQA_HEADS // GQA_KV_HEADS
GLA_HEADS = 4
GLA_DK = 64
GLA_DV = 128
GLA_RANK = 16
GLA_TAU = 16.0
GLA_CHUNK = 64
DIFF_HEADS = 4
DIFF_DV = 2 * HEAD_DIM
N_BRANCH = 3
BRANCH_W = 512
FFN_HIDDEN = ((8 * D_MODEL + 3 * 256 - 1) // (3 * 256)) * 256
ROPE_THETA = 10000.0
EPS = 1e-6

LANES = 128
BF16_ROWS = 16
F32_ROWS = 8
VMEM_LIMIT = 56 * 1024 * 1024

F32 = jnp.float32
BF16 = jnp.bfloat16

R_GQ = 0
R_KV = R_GQ + GQA_HEADS * HEAD_DIM
R_GLA = R_KV + 2 * GQA_KV_HEADS * HEAD_DIM
R_LR = R_GLA + 2 * GLA_HEADS * GLA_DK + GLA_HEADS * GLA_DV
R_LGO = R_LR + 2 * GLA_RANK
R_DQ = R_LGO + GLA_HEADS * GLA_DV
R_DK = R_DQ + DIFF_HEADS * 2 * HEAD_DIM
R_DV = R_DK + DIFF_HEADS * 2 * HEAD_DIM
R_MG = R_DV + DIFF_HEADS * DIFF_DV
N_IN = R_MG + N_BRANCH * D_MODEL
NT_DIMS = (((1,), (1,)), ((), ()))
N_QG = GQA_HEADS * HEAD_DIM
GLA_W = GLA_HEADS * GLA_DK
GQA_VT_ROWS = HEAD_DIM + BF16_ROWS
DIFF_VT_ROWS = DIFF_DV + BF16_ROWS


def _sigmoid(x):
    return 0.5 * jnp.tanh(0.5 * x) + 0.5


def _silu(x):
    return x * _sigmoid(x)


def _bf16_terms(x):
    hi = x.astype(BF16)
    r1 = x - hi.astype(F32)
    mid = r1.astype(BF16)
    lo = (r1 - mid.astype(F32)).astype(BF16)
    return hi, mid, lo


def _full_spec(shape):
    n = len(shape)
    return pl.BlockSpec(shape, lambda *_: (0,) * n, pipeline_mode=pl.Buffered(1))


def _layer_spec(shape, l):
    n = len(shape)
    return pl.BlockSpec((None,) + tuple(shape), lambda *_: (l,) + (0,) * n, pipeline_mode=pl.Buffered(1))


def _mod_kernel(cond_ref, w_ref, b_ref, o_ref):
    s_hi, s_mid, _ = _bf16_terms(_silu(cond_ref[...]))
    lhs = jnp.concatenate([s_hi, s_mid], axis=0)
    w = w_ref[0]
    w_hi = w.astype(BF16)
    w_mid = (w - w_hi.astype(F32)).astype(BF16)
    r = jnp.dot(lhs, w_hi, preferred_element_type=F32) + jnp.dot(lhs, w_mid, preferred_element_type=F32)
    n = cond_ref.shape[0]
    o_ref[0] = r[0:n] + r[n:2 * n] + b_ref[0]


def _modulation(cond_rows, w_mod, b_mod):
    tn = 1536
    n_out = 6 * D_MODEL
    return pl.pallas_call(
        _mod_kernel,
        grid=(DEPTH, n_out // tn),
        in_specs=[
            pl.BlockSpec((F32_ROWS, D_MODEL), lambda l, j: (0, 0)),
            pl.BlockSpec((1, D_MODEL, tn), lambda l, j: (l, 0, j)),
            pl.BlockSpec((1, 1, tn), lambda l, j: (l, 0, j)),
        ],
        out_specs=pl.BlockSpec((1, F32_ROWS, tn), lambda l, j: (l, 0, j)),
        out_shape=jax.ShapeDtypeStruct((DEPTH, F32_ROWS, n_out), F32),
        compiler_params=pltpu.CompilerParams(
            dimension_semantics=("parallel", "parallel"), vmem_limit_bytes=VMEM_LIMIT),
        name="modulation",
    )(cond_rows, w_mod, b_mod.reshape(DEPTH, 1, n_out))


def _head_norm(z, gain, gmat):
    outs = []
    n = z.shape[1]
    for c0 in range(0, n, 2 * LANES):
        w = min(2 * LANES, n - c0)
        zz = z[:, c0:c0 + w]
        ms = jnp.dot((zz * zz).astype(BF16), gmat[0:w, 0:w], preferred_element_type=F32)
        outs.append(zz * lax.rsqrt(ms + EPS) * gain[:, c0:c0 + w])
    return outs


def _rope_block(zb, cos, sin, first):
    partner = jnp.where(first, pltpu.roll(zb, LANES - 16, 1), pltpu.roll(zb, 16, 1))
    return zb * cos + partner * sin


N_IN_BASE = 9


def _in_kernel(*refs, rope, emit_cache, prev_layers, tiles_per_batch, cache_tiles):
    if not cache_tiles:
        _in_tile(refs, rope, emit_cache, prev_layers)
        return
    kcg_ref, kcd_ref, vcg_ref, vcd_ref = refs[:4]
    j = pl.program_id(1)

    @pl.when(j < tiles_per_batch)
    def _():
        _in_tile(refs[4:], rope, emit_cache, prev_layers)

    @pl.when(j >= tiles_per_batch)
    def _():
        n_in = 4 + N_IN_BASE + (2 if rope else 0)
        kdg_ref, kdd_ref, vtg_ref, vtd_ref = refs[n_in + 2:n_in + 6]
        kdg_ref[...] = kcg_ref[...]
        kdd_ref[...] = kcd_ref[...]
        vtg_ref[...] = vcg_ref[...]
        vtd_ref[...] = vcd_ref[...]


def _in_tile(refs, rope, emit_cache, prev_layers):
    it = iter(refs)
    (x_ref, mod_ref, g1_ref, wt_ref, gq_ref, gk_ref, gmat_ref, up_ref, ub_ref) = (
        next(it) for _ in range(N_IN_BASE))
    if rope:
        cos_ref, sin_ref = next(it), next(it)
    prev_refs = [[next(it) for _ in range(4)] for _ in range(prev_layers or 0)]
    (qg_ref, qd_ref, kdg_ref, kdd_ref, vtg_ref, vtd_ref,
     lq_ref, lk_ref, lv_ref, lgo_ref, la_ref, gate_ref) = (next(it) for _ in range(12))
    if emit_cache:
        ck_g_ref, ck_d_ref, cv_g_ref, cv_d_ref = (next(it) for _ in range(4))

    tm = x_ref.shape[0]
    bpt = kdg_ref.shape[0]
    nt = tm // bpt
    x = x_ref[...]
    ms = jnp.mean(x * x, axis=-1, keepdims=True)
    h = x * lax.rsqrt(ms + EPS) * g1_ref[...]
    h = h * (1.0 + mod_ref[0, 1:2, :]) + mod_ref[0, 0:1, :]
    hb = h.astype(BF16)

    def proj(r0, n):
        return lax.dot_general(hb, wt_ref[r0:r0 + n, :], NT_DIMS, preferred_element_type=F32)

    lane = lax.broadcasted_iota(jnp.int32, (tm, LANES), 1)
    first16 = (lane % 32) < 16
    gmat = gmat_ref[...]
    if rope:
        cos, sin = cos_ref[...], sin_ref[...]

    def blocks128(parts):
        out = []
        for p in parts:
            for c in range(0, p.shape[1], LANES):
                out.append(p[:, c:c + LANES])
        return out

    def store_queries(z, gain, dst):
        for j, zb in enumerate(blocks128(_head_norm(z, gain, gmat))):
            if rope:
                zb = _rope_block(zb, cos, sin, first16)
            dst[:, j * LANES:(j + 1) * LANES] = zb.astype(BF16)

    def store_cache(dst, c0, value):
        w = value.shape[1]
        if prev_layers is None:
            dst[:, c0:c0 + w] = value
        else:
            for bb in range(bpt):
                dst[bb, prev_layers, :, c0:c0 + w] = value[bb * nt:(bb + 1) * nt, :]

    def store_keys(z, gain, dst, cache_dst):
        for j, zb in enumerate(blocks128(_head_norm(z, gain, gmat))):
            if rope:
                zb = _rope_block(zb, cos, sin, first16)
            if emit_cache:
                store_cache(cache_dst, j * LANES, zb)
            sw = pltpu.roll(zb, HEAD_DIM, 1)
            for bb in range(bpt):
                dst[bb, 2 * j] = zb[bb * nt:(bb + 1) * nt, 0:HEAD_DIM].astype(BF16)
                dst[bb, 2 * j + 1] = sw[bb * nt:(bb + 1) * nt, 0:HEAD_DIM].astype(BF16)

    def store_gate(b):
        gate_ref[:, b * D_MODEL:(b + 1) * D_MODEL] = _sigmoid(proj(R_MG + b * D_MODEL, D_MODEL)).astype(BF16)

    z_gq = proj(R_GQ, N_QG)
    z_dq = proj(R_DQ, N_QG)
    store_gate(0)
    store_queries(z_gq, gq_ref[:, 0:N_QG], qg_ref)
    z_kv = proj(R_KV, 2 * LANES)
    store_queries(z_dq, gq_ref[:, N_QG:2 * N_QG], qd_ref)
    z_dk = proj(R_DK, BRANCH_W)
    store_gate(1)
    store_keys(z_kv[:, 0:LANES], gk_ref[:, 0:LANES], kdg_ref, ck_g_ref if emit_cache else None)
    z_dv = proj(R_DV, BRANCH_W)
    store_keys(z_dk, gk_ref[:, LANES:LANES + BRANCH_W], kdd_ref, ck_d_ref if emit_cache else None)
    zl = proj(R_GLA, R_LR - R_GLA)
    z_lgo = proj(R_LGO, BRANCH_W)
    lr = proj(R_LR, LANES)
    store_gate(2)

    ones_rows = jnp.where(lax.broadcasted_iota(jnp.int32, (BF16_ROWS, nt), 0) == 0, 1.0, 0.0).astype(BF16)
    z_gv = z_kv[:, LANES:2 * LANES]
    if emit_cache:
        store_cache(cv_g_ref, 0, z_gv)
        store_cache(cv_d_ref, 0, z_dv)
        for layer, layer_refs in enumerate(prev_refs):
            for dst, src in zip((ck_g_ref, ck_d_ref, cv_g_ref, cv_d_ref), layer_refs):
                for bb in range(bpt):
                    dst[bb, layer] = src[bb * nt:(bb + 1) * nt, :]
    for bb in range(bpt):
        rows = slice(bb * nt, (bb + 1) * nt)
        vt = z_gv[rows, :].T
        for hh in range(GQA_KV_HEADS):
            vtg_ref[bb, hh, 0:HEAD_DIM, :] = vt[hh * HEAD_DIM:(hh + 1) * HEAD_DIM, :].astype(BF16)
            vtg_ref[bb, hh, HEAD_DIM:GQA_VT_ROWS, :] = ones_rows
        for hh in range(DIFF_HEADS):
            vt = z_dv[rows, hh * LANES:(hh + 1) * LANES].T
            vtd_ref[bb, hh, 0:DIFF_DV, :] = vt.astype(BF16)
            vtd_ref[bb, hh, DIFF_DV:DIFF_VT_ROWS, :] = ones_rows

    o = 0
    lq_ref[...] = (zl[:, o:o + GLA_W] * (GLA_DK ** -0.5)).astype(BF16)
    o += GLA_W
    lk_ref[...] = zl[:, o:o + GLA_W].astype(BF16)
    o += GLA_W
    lv_ref[...] = zl[:, o:o + BRANCH_W].astype(BF16)
    o += BRANCH_W
    lgo_ref[...] = z_lgo.astype(BF16)
    lr_hi, lr_mid, _ = _bf16_terms(lr)
    z = (jnp.dot(lr_hi, up_ref[0], preferred_element_type=F32)
         + jnp.dot(lr_mid, up_ref[0], preferred_element_type=F32)
         + jnp.dot(lr_hi, up_ref[1], preferred_element_type=F32)) + ub_ref[...]
    log_sig = jnp.minimum(z, 0.0) - jnp.log(1.0 + jnp.exp(-jnp.abs(z)))
    la_ref[...] = log_sig * (1.0 / GLA_TAU)


def _in_projection(x2d, mod, lw, sw, l, n_batch, n_tok, cache_kv, rope_tabs, emit_cache, prev_caches, tm):
    n_cache = 0 if cache_kv is None else cache_kv[0].shape[2]
    assert n_cache % tm == 0
    n_keys = n_tok + n_cache
    t_total = n_batch * n_tok
    bpt = max(1, tm // n_tok)
    assert bpt == 1 or (mod.shape[0] == 1 and n_cache == 0 and tm == bpt * n_tok)
    nt = tm // bpt
    tpb = n_tok // nt
    cache_tiles = n_cache // tm
    rope = rope_tabs is not None
    own = lambda j: jnp.minimum(j, tpb - 1)
    tile = lambda c: pl.BlockSpec((tm, c), lambda b, j: (b * tpb + own(j), 0))
    in_specs, args = [], []
    if cache_tiles:
        past = lambda j: jnp.maximum(j - tpb, 0)
        in_specs += [
            pl.BlockSpec((1, GQA_KV_HEADS, tm, HEAD_DIM), lambda b, j: (b, 0, past(j), 0)),
            pl.BlockSpec((1, 2 * DIFF_HEADS, tm, HEAD_DIM), lambda b, j: (b, 0, past(j), 0)),
            pl.BlockSpec((1, GQA_KV_HEADS, GQA_VT_ROWS, tm), lambda b, j: (b, 0, 0, past(j))),
            pl.BlockSpec((1, DIFF_HEADS, DIFF_VT_ROWS, tm), lambda b, j: (b, 0, 0, past(j))),
        ]
        args += list(cache_kv)
    in_specs += [
        tile(D_MODEL),
        pl.BlockSpec((1, 6, D_MODEL), lambda b, j: (b if mod.shape[0] > 1 else 0, 0, 0)),
        _full_spec((1, D_MODEL)),
        _layer_spec((N_IN, D_MODEL), l),
        _full_spec((1, 2 * N_QG)),
        _full_spec((1, LANES + BRANCH_W)),
        _full_spec((2 * LANES, 2 * LANES)),
        _full_spec((2, LANES, 2 * GLA_W)),
        _full_spec((1, 2 * GLA_W)),
    ]
    args += [x2d, mod, lw["norm1"], sw["w_in_t"], lw["gq"], lw["gk"], lw["gmat"], lw["up"], lw["ub"]]
    if rope:
        in_specs += [pl.BlockSpec((tm, LANES), lambda b, j: (own(j), 0))] * 2
        args += list(rope_tabs)
    sd = jax.ShapeDtypeStruct
    out_shape = [
        sd((t_total, BRANCH_W), BF16), sd((t_total, BRANCH_W), BF16),
        sd((n_batch, GQA_KV_HEADS, n_keys, HEAD_DIM), BF16), sd((n_batch, 2 * DIFF_HEADS, n_keys, HEAD_DIM), BF16),
        sd((n_batch, GQA_KV_HEADS, GQA_VT_ROWS, n_keys), BF16), sd((n_batch, DIFF_HEADS, DIFF_VT_ROWS, n_keys), BF16),
        sd((t_total, GLA_W), BF16), sd((t_total, GLA_W), BF16),
        sd((t_total, BRANCH_W), BF16), sd((t_total, BRANCH_W), BF16),
        sd((t_total, 2 * GLA_W), F32), sd((t_total, N_BRANCH * D_MODEL), BF16),
    ]
    out_specs = [
        tile(BRANCH_W), tile(BRANCH_W),
        pl.BlockSpec((bpt, GQA_KV_HEADS, nt, HEAD_DIM), lambda b, j: (b, 0, j, 0)),
        pl.BlockSpec((bpt, 2 * DIFF_HEADS, nt, HEAD_DIM), lambda b, j: (b, 0, j, 0)),
        pl.BlockSpec((bpt, GQA_KV_HEADS, GQA_VT_ROWS, nt), lambda b, j: (b, 0, 0, j)),
        pl.BlockSpec((bpt, DIFF_HEADS, DIFF_VT_ROWS, nt), lambda b, j: (b, 0, 0, j)),
        tile(GLA_W), tile(GLA_W), tile(BRANCH_W), tile(BRANCH_W), tile(2 * GLA_W), tile(N_BRANCH * D_MODEL),
    ]
    cache_widths = (LANES, BRANCH_W, LANES, BRANCH_W)
    prev_layers = None
    if emit_cache and prev_caches is None:
        out_shape += [sd((t_total, w), F32) for w in cache_widths]
        out_specs += [tile(w) for w in cache_widths]
    elif emit_cache:
        prev_layers = len(prev_caches)
        n_layers = prev_layers + 1
        for layer_arrays in prev_caches:
            in_specs += [tile(w) for w in cache_widths]
            args += list(layer_arrays)
        out_shape += [sd((n_batch, n_layers, n_tok, w), F32) for w in cache_widths]
        out_specs += [pl.BlockSpec((bpt, n_layers, nt, w), lambda b, j: (b, 0, j, 0)) for w in cache_widths]
    return pl.pallas_call(
        functools.partial(_in_kernel, rope=rope, emit_cache=emit_cache, prev_layers=prev_layers,
                          tiles_per_batch=tpb, cache_tiles=cache_tiles),
        grid=(n_batch // bpt, tpb + cache_tiles),
        in_specs=in_specs,
        out_specs=out_specs,
        out_shape=out_shape,
        compiler_params=pltpu.CompilerParams(
            dimension_semantics=("parallel", "arbitrary"), vmem_limit_bytes=VMEM_LIMIT),
        name="in_projection",
    )(*args)


MAX_COL = 4 * LANES
SCORE_BOUND = 45.0
NORM_MARGIN = 1.05


def _head_queries_t(q_blk, low):
    t = q_blk.astype(F32).T
    return (t[0:HEAD_DIM, :] if low else t[HEAD_DIM:2 * HEAD_DIM, :]).astype(BF16)


def _key_norm2(kd):
    kf = kd.astype(F32)
    return jnp.max(jnp.sum(kf * kf, axis=1, keepdims=True))


def _needs_no_stabiliser(q_norm2, k_norm2):
    return (q_norm2 * k_norm2 <= SCORE_BOUND * SCORE_BOUND).astype(jnp.int32)


def _score_units(heads_cols):
    return [(kh, vh, c) for kh, vh, cols in heads_cols for c in cols]


def _tile_scores(k_ref, qm_ref, unit, col):
    kh, _, c = unit
    return jnp.dot(k_ref[0, kh], qm_ref[:, c * col:(c + 1) * col], preferred_element_type=F32)


def _plain_tiles(k_ref, vt_ref, qm_ref, acc_ref, heads_cols):
    col = acc_ref.shape[-1]
    units = _score_units(heads_cols)
    s_next = _tile_scores(k_ref, qm_ref, units[0], col)
    for i, (_, vh, c) in enumerate(units):
        s = s_next
        if i + 1 < len(units):
            s_next = _tile_scores(k_ref, qm_ref, units[i + 1], col)
        p = jnp.exp2(s).astype(BF16)
        acc_ref[c] += jnp.dot(vt_ref[0, vh], p, preferred_element_type=F32)


def _online_tiles(k_ref, vt_ref, qm_ref, m_ref, acc_ref, heads_cols):
    col = acc_ref.shape[-1]
    for unit in _score_units(heads_cols):
        _, vh, c = unit
        s = _tile_scores(k_ref, qm_ref, unit, col)
        m_prev = m_ref[c]
        m_new = jnp.maximum(m_prev, jnp.max(s, axis=0, keepdims=True))
        alpha = jnp.exp2(m_prev - m_new)
        p = jnp.exp2(s - m_new).astype(BF16)
        acc_ref[c] = acc_ref[c] * alpha + jnp.dot(vt_ref[0, vh], p, preferred_element_type=F32)
        m_ref[c] = m_new


def _flash_init(kb_ref, kc_ref, n_key_heads, m_ref, acc_ref, plain_ref):
    k_norm2 = kb_ref[1]
    if kc_ref is not None:
        for kh in range(n_key_heads):
            k_norm2 = jnp.maximum(k_norm2, _key_norm2(kc_ref[0, kh]))
    plain_ref[0] = _needs_no_stabiliser(kb_ref[0], k_norm2)
    m_ref[...] = jnp.full(m_ref.shape, -jnp.inf, F32)
    acc_ref[...] = jnp.zeros(acc_ref.shape, F32)


def _flash_step(k_ref, vt_ref, qm_ref, m_ref, acc_ref, plain_ref, heads_cols):
    @pl.when(plain_ref[0] == 1)
    def _():
        _plain_tiles(k_ref, vt_ref, qm_ref, acc_ref, heads_cols)

    @pl.when(plain_ref[0] != 1)
    def _():
        _online_tiles(k_ref, vt_ref, qm_ref, m_ref, acc_ref, heads_cols)


def _gqa_kernel(*refs, has_cache, tq, hp):
    if has_cache:
        kb_ref, q_ref, k_ref, vt_ref, kc_ref, o_ref, qm_ref, m_ref, acc_ref, plain_ref = refs
    else:
        kb_ref, q_ref, k_ref, vt_ref, o_ref, qm_ref, m_ref, acc_ref, plain_ref = refs
        kc_ref = None
    kt = pl.program_id(3)
    col = acc_ref.shape[-1]
    per_head = tq // col
    per_kv = GQA_GROUP * per_head
    heads_cols = [(hh, hh, range(hh * per_kv, (hh + 1) * per_kv)) for hh in range(hp)]

    @pl.when(kt == 0)
    def _():
        for hh in range(hp):
            for g in range(GQA_GROUP):
                c0 = hh * 2 * LANES + (g // 2) * LANES
                row0 = (hh * GQA_GROUP + g) * tq
                qm_ref[:, row0:row0 + tq] = _head_queries_t(q_ref[0, :, c0:c0 + LANES], g % 2 == 0)
        _flash_init(kb_ref, kc_ref, hp, m_ref, acc_ref, plain_ref)

    _flash_step(k_ref, vt_ref, qm_ref, m_ref, acc_ref, plain_ref, heads_cols)

    @pl.when(kt == pl.num_programs(3) - 1)
    def _():
        for hh in range(hp):
            for part in range(per_head):
                heads = []
                for g in range(GQA_GROUP):
                    acc = acc_ref[hh * per_kv + g * per_head + part]
                    heads.append(acc[0:HEAD_DIM, :] / acc[HEAD_DIM:HEAD_DIM + 1, :])
                o_ref[0, part * col:(part + 1) * col, hh * 2 * LANES:(hh + 1) * 2 * LANES] = (
                    jnp.concatenate(heads, axis=0).T.astype(BF16))


def _kv_specs(key_heads_blk, val_heads_blk, vt_rows, tk, kd_c):
    specs = [pl.BlockSpec((1, key_heads_blk, tk, HEAD_DIM), lambda b, h, qi, kt: (b, h, kt, 0)),
             pl.BlockSpec((1, val_heads_blk, vt_rows, tk), lambda b, h, qi, kt: (b, h, 0, kt))]
    if kd_c is not None:
        specs.append(pl.BlockSpec((1, key_heads_blk, kd_c.shape[2], HEAD_DIM), lambda b, h, qi, kt: (b, h, 0, 0)))
    return specs


def _gqa_attention(q, k_bound, kd, vt, kd_c, n_batch, n_tok, tq, tk, hp):
    has_cache = kd_c is not None
    nk = kd.shape[2] // tk
    rows = hp * GQA_GROUP * tq
    col = min(MAX_COL, tq)
    q_spec = pl.BlockSpec((1, tq, hp * 2 * LANES), lambda b, h, qi, kt: (b, qi, h))
    in_specs = [pl.BlockSpec(memory_space=pltpu.SMEM), q_spec]
    in_specs += _kv_specs(hp, hp, GQA_VT_ROWS, tk, kd_c)
    args = [k_bound, q.reshape(n_batch, n_tok, BRANCH_W), kd, vt]
    if has_cache:
        args.append(kd_c)
    out = pl.pallas_call(
        functools.partial(_gqa_kernel, has_cache=has_cache, tq=tq, hp=hp),
        grid=(n_batch, GQA_KV_HEADS // hp, n_tok // tq, nk),
        in_specs=in_specs,
        out_specs=q_spec,
        out_shape=jax.ShapeDtypeStruct((n_batch, n_tok, BRANCH_W), BF16),
        scratch_shapes=[
            pltpu.VMEM((HEAD_DIM, rows), BF16),
            pltpu.VMEM((rows // col, 1, col), F32),
            pltpu.VMEM((rows // col, GQA_VT_ROWS, col), F32),
            pltpu.SMEM((1,), jnp.int32),
        ],
        compiler_params=pltpu.CompilerParams(
            dimension_semantics=("parallel", "parallel", "parallel", "arbitrary"),
            vmem_limit_bytes=VMEM_LIMIT),
        name="gqa_attention",
    )(*args)
    return out.reshape(n_batch * n_tok, BRANCH_W)


def _diff_kernel(*refs, has_cache, lam_init, hp):
    if has_cache:
        (kb_ref, q_ref, lam_ref, gsub_ref, k_ref, vt_ref, kc_ref, o_ref,
         qm_ref, m_ref, acc_ref, plain_ref) = refs
    else:
        kb_ref, q_ref, lam_ref, gsub_ref, k_ref, vt_ref, o_ref, qm_ref, m_ref, acc_ref, plain_ref = refs
        kc_ref = None
    kt = pl.program_id(3)
    tq = q_ref.shape[1]
    col = acc_ref.shape[-1]
    per_map = tq // col
    heads_cols = [(2 * hh + mm, hh, range((2 * hh + mm) * per_map, (2 * hh + mm + 1) * per_map))
                  for hh in range(hp) for mm in range(2)]

    @pl.when(kt == 0)
    def _():
        for hh in range(hp):
            blk = q_ref[0, :, hh * LANES:(hh + 1) * LANES]
            for mm in range(2):
                row0 = (2 * hh + mm) * tq
                qm_ref[:, row0:row0 + tq] = _head_queries_t(blk, mm == 0)
        _flash_init(kb_ref, kc_ref, 2 * hp, m_ref, acc_ref, plain_ref)

    _flash_step(k_ref, vt_ref, qm_ref, m_ref, acc_ref, plain_ref, heads_cols)

    @pl.when(kt == pl.num_programs(3) - 1)
    def _():
        lp = lam_ref[...]
        lam = (jnp.exp(jnp.sum(lp[0:1] * lp[1:2], axis=-1, keepdims=True))
               - jnp.exp(jnp.sum(lp[2:3] * lp[3:4], axis=-1, keepdims=True)) + lam_init)
        for hh in range(hp):
            for part in range(per_map):
                a0, a1 = acc_ref[2 * hh * per_map + part], acc_ref[(2 * hh + 1) * per_map + part]
                o0 = a0[0:DIFF_DV, :] / a0[DIFF_DV:DIFF_DV + 1, :]
                o1 = a1[0:DIFF_DV, :] / a1[DIFF_DV:DIFF_DV + 1, :]
                d = (o0 - lam * o1).T
                ms = jnp.mean(d * d, axis=-1, keepdims=True)
                o_ref[0, part * col:(part + 1) * col, hh * LANES:(hh + 1) * LANES] = (
                    d * lax.rsqrt(ms + EPS) * gsub_ref[...] * (1.0 - lam_init)).astype(BF16)


def _diff_attention(q, k_bound, lam_p, gsub, kd, vt, kd_c, n_batch, n_tok, tq, tk, hp, lam_init):
    has_cache = kd_c is not None
    nk = kd.shape[2] // tk
    col = min(MAX_COL, tq)
    rows = hp * 2 * tq
    q_spec = pl.BlockSpec((1, tq, hp * LANES), lambda b, h, qi, kt: (b, qi, h))
    in_specs = [
        pl.BlockSpec(memory_space=pltpu.SMEM),
        q_spec,
        pl.BlockSpec((4, HEAD_DIM), lambda b, h, qi, kt: (0, 0)),
        pl.BlockSpec((1, DIFF_DV), lambda b, h, qi, kt: (0, 0)),
    ]
    in_specs += _kv_specs(2 * hp, hp, DIFF_VT_ROWS, tk, kd_c)
    args = [k_bound, q.reshape(n_batch, n_tok, BRANCH_W), lam_p, gsub, kd, vt]
    if has_cache:
        args.append(kd_c)
    out = pl.pallas_call(
        functools.partial(_diff_kernel, has_cache=has_cache, lam_init=lam_init, hp=hp),
        grid=(n_batch, DIFF_HEADS // hp, n_tok // tq, nk),
        in_specs=in_specs,
        out_specs=q_spec,
        out_shape=jax.ShapeDtypeStruct((n_batch, n_tok, BRANCH_W), BF16),
        scratch_shapes=[
            pltpu.VMEM((HEAD_DIM, rows), BF16),
            pltpu.VMEM((rows // col, 1, col), F32),
            pltpu.VMEM((rows // col, DIFF_VT_ROWS, col), F32),
            pltpu.SMEM((1,), jnp.int32),
        ],
        compiler_params=pltpu.CompilerParams(
            dimension_semantics=("parallel", "parallel", "parallel", "arbitrary"),
            vmem_limit_bytes=VMEM_LIMIT),
        name="diff_attention",
    )(*args)
    return out.reshape(n_batch * n_tok, BRANCH_W)


def _gla_kernel(*refs, has_s0, n_chunk, bg):
    if has_s0:
        qf, kf, vf, laf, qb, kb, vb, lab, s0_ref, of_ref, ob_ref, sfin_ref, st_ref = refs
    else:
        qf, kf, vf, laf, qb, kb, vb, lab, of_ref, ob_ref, sfin_ref, st_ref = refs
    i = pl.program_id(1)
    ck = GLA_CHUNK

    @pl.when(i == 0)
    def _():
        for bb in range(bg):
            for d in range(2):
                for hd in range(GLA_HEADS):
                    if has_s0:
                        s = s0_ref[bb, d, hd]
                        z = jnp.zeros_like(s)
                        padded = jnp.concatenate([s, z] if hd % 2 == 0 else [z, s], axis=0)
                        st_ref[bb, d, hd] = padded.T
                    else:
                        st_ref[bb, d, hd] = jnp.zeros((GLA_DV, LANES), F32)

    tb = qf.shape[1]
    r = lax.broadcasted_iota(jnp.int32, (tb, tb), 0)
    c = lax.broadcasted_iota(jnp.int32, (tb, tb), 1)
    lane = lax.broadcasted_iota(jnp.int32, (tb, LANES), 1)
    chunk_id = lax.broadcasted_iota(jnp.int32, (tb, GLA_W), 0) // ck
    zero_row = jnp.zeros((1, GLA_W), F32)
    streams = []
    for bb in range(bg):
        streams.append((bb, 0, qf.at[bb], kf.at[bb], vf.at[bb], laf.at[bb], of_ref.at[bb], c <= r))
        streams.append((bb, 1, qb.at[bb], kb.at[bb], vb.at[bb], lab.at[bb], ob_ref.at[bb], c >= r))
    nt = (((1,), (1,)), ((), ()))
    g_all = [sum(jnp.dot(tri.astype(F32).astype(BF16), part, preferred_element_type=F32)
                 for part in _bf16_terms(la_r[...]))
             for (_, _, _, _, _, la_r, _, tri) in streams]
    prepared = []
    for g, (bb, d, q_r, k_r, v_r, la_r, o_r, tri) in zip(g_all, streams):
        if d == 0:
            bounds = [zero_row] + [g[ck * j - 1:ck * j, :] for j in range(1, n_chunk)]
            g_end = g[tb - 1:tb, :]
        else:
            bounds = [g[ck * (j + 1):ck * (j + 1) + 1, :] for j in range(n_chunk - 1)] + [zero_row]
            g_end = g[0:1, :]
        b_rows = jnp.concatenate([jnp.broadcast_to(b, (ck, GLA_W)) for b in bounds], axis=0)
        q = q_r[...].astype(F32)
        k = k_r[...].astype(F32)
        q_dec = q * jnp.exp(g - b_rows)
        q_glob = q * jnp.exp(g)
        k_end = k * jnp.exp(g_end - g)
        k_rel = []
        for j in range(n_chunk):
            reach = (chunk_id <= j) if d == 0 else (chunk_id >= j)
            k_rel.append((k * jnp.exp(jnp.where(reach, bounds[j] - g, 0.0))).astype(BF16))
        prepared.append((q_dec, q_glob, k_end, k_rel, jnp.exp(g_end), v_r[...]))

    partial = []
    for (bb, d, *_), (q_dec, q_glob, k_end, k_rel, decay, v) in zip(streams, prepared):
        per_head = []
        for hd in range(GLA_HEADS):
            pair = slice((hd // 2) * LANES, (hd // 2 + 1) * LANES)
            hv = slice(hd * GLA_DV, (hd + 1) * GLA_DV)
            keep = (lane < GLA_DK) if hd % 2 == 0 else (lane >= GLA_DK)
            qd_m = jnp.where(keep, q_dec[:, pair], 0.0).astype(BF16)
            qg_m = jnp.where(keep, q_glob[:, pair], 0.0).astype(BF16)
            ke_m = jnp.where(keep, k_end[:, pair], 0.0).astype(BF16)
            a_rows = [lax.dot_general(qd_m[ck * j:ck * (j + 1), :], k_rel[j][:, pair], nt,
                                      preferred_element_type=F32) for j in range(n_chunk)]
            s_t = st_ref[bb, d, hd]
            o_state = lax.dot_general(qg_m, s_t.astype(BF16), nt, preferred_element_type=F32)
            ds_t = lax.dot_general(v[:, hv], ke_m, (((0,), (0,)), ((), ())), preferred_element_type=F32)
            st_ref[bb, d, hd] = s_t * decay[:, pair] + ds_t
            per_head.append((a_rows, o_state))
        partial.append(per_head)

    for (bb, d, _, _, _, _, o_r, tri), prep, per_head in zip(streams, prepared, partial):
        v = prep[5]
        for hd in range(GLA_HEADS):
            hv = slice(hd * GLA_DV, (hd + 1) * GLA_DV)
            a_rows, o_state = per_head[hd]
            a = jnp.where(tri, jnp.concatenate(a_rows, axis=0), 0.0).astype(BF16)
            o_r[:, hv] = (jnp.dot(a, v[:, hv], preferred_element_type=F32) + o_state).astype(BF16)

    @pl.when(i == pl.num_programs(1) - 1)
    def _():
        for bb in range(bg):
            for d in range(2):
                for hd in range(GLA_HEADS):
                    t = st_ref[bb, d, hd].T
                    sfin_ref[bb, d, hd] = t[(hd % 2) * GLA_DK:(hd % 2 + 1) * GLA_DK, :]


def _gla(lq, lk, lv, la, s0, n_batch, n_tok, tb, bg):
    nb = n_tok // tb
    has_s0 = s0 is not None
    as3d = lambda x: x.reshape(n_batch, n_tok, x.shape[-1])
    fwd = lambda b, i: (b, i, 0)
    bwd = lambda b, i: (b, nb - 1 - i, 0)
    bwd_la = lambda b, i: (b, nb - 1 - i, 1)
    in_specs = [
        pl.BlockSpec((bg, tb, GLA_W), fwd), pl.BlockSpec((bg, tb, GLA_W), fwd),
        pl.BlockSpec((bg, tb, BRANCH_W), fwd), pl.BlockSpec((bg, tb, GLA_W), fwd),
        pl.BlockSpec((bg, tb, GLA_W), bwd), pl.BlockSpec((bg, tb, GLA_W), bwd),
        pl.BlockSpec((bg, tb, BRANCH_W), bwd), pl.BlockSpec((bg, tb, GLA_W), bwd_la),
    ]
    args = [as3d(lq), as3d(lk), as3d(lv), as3d(la)] * 2
    state_spec = pl.BlockSpec((bg, 2, GLA_HEADS, GLA_DK, GLA_DV), lambda b, i: (b, 0, 0, 0, 0))
    if has_s0:
        in_specs.append(state_spec)
        args.append(s0)
    o_f, o_b, s_fin = pl.pallas_call(
        functools.partial(_gla_kernel, has_s0=has_s0, n_chunk=tb // GLA_CHUNK, bg=bg),
        grid=(n_batch // bg, nb),
        in_specs=in_specs,
        out_specs=[pl.BlockSpec((bg, tb, BRANCH_W), fwd), pl.BlockSpec((bg, tb, BRANCH_W), bwd), state_spec],
        out_shape=[
            jax.ShapeDtypeStruct((n_batch, n_tok, BRANCH_W), BF16),
            jax.ShapeDtypeStruct((n_batch, n_tok, BRANCH_W), BF16),
            jax.ShapeDtypeStruct((n_batch, 2, GLA_HEADS, GLA_DK, GLA_DV), F32),
        ],
        scratch_shapes=[pltpu.VMEM((bg, 2, GLA_HEADS, GLA_DV, LANES), F32)],
        compiler_params=pltpu.CompilerParams(
            dimension_semantics=("parallel", "arbitrary"), vmem_limit_bytes=VMEM_LIMIT),
        name="gla",
    )(*args)
    flat = lambda x: x.reshape(n_batch * n_tok, BRANCH_W)
    return flat(o_f), flat(o_b), s_fin


FFN_CHUNK = 256


def _merge_ffn_kernel(x_ref, mod_ref, og_ref, of_ref, ob_ref, lgo_ref, od_ref, gate_ref,
                      gout_ref, wb_ref, wo_ref, g2_ref, wi_ref, wd_ref, y_ref):
    o_gla = of_ref[...].astype(F32) + ob_ref[...].astype(F32)
    gla_parts = []
    for hd in range(GLA_HEADS):
        blk = o_gla[:, hd * GLA_DV:(hd + 1) * GLA_DV]
        ms = jnp.mean(blk * blk, axis=-1, keepdims=True)
        gla_parts.append(blk * lax.rsqrt(ms + EPS) * gout_ref[...])
    gla = jnp.concatenate(gla_parts, axis=1) * _silu(lgo_ref[...].astype(F32))
    branches = (og_ref[...], gla.astype(BF16), od_ref[...])
    mixed = None
    for b, ob in enumerate(branches):
        y = jnp.dot(ob, wb_ref[b], preferred_element_type=F32)
        y = y * gate_ref[:, b * D_MODEL:(b + 1) * D_MODEL].astype(F32)
        mixed = y if mixed is None else mixed + y
    out = jnp.dot(mixed.astype(BF16), wo_ref[...], preferred_element_type=F32)
    x = x_ref[...] + mod_ref[0, 2:3, :] * out

    ms = jnp.mean(x * x, axis=-1, keepdims=True)
    h = x * lax.rsqrt(ms + EPS) * g2_ref[...]
    h = h * (1.0 + mod_ref[0, 4:5, :]) + mod_ref[0, 3:4, :]
    hb = h.astype(BF16)

    def up_proj(c0):
        a = jnp.dot(hb, wi_ref[:, c0:c0 + FFN_CHUNK], preferred_element_type=F32)
        u = jnp.dot(hb, wi_ref[:, FFN_HIDDEN + c0:FFN_HIDDEN + c0 + FFN_CHUNK], preferred_element_type=F32)
        return a, u

    acc = None
    chunks = list(range(0, FFN_HIDDEN, FFN_CHUNK))
    nxt = up_proj(chunks[0])
    for i, c0 in enumerate(chunks):
        a, u = nxt
        if i + 1 < len(chunks):
            nxt = up_proj(chunks[i + 1])
        act = (_silu(a) * u).astype(BF16)
        part = jnp.dot(act, wd_ref[c0:c0 + FFN_CHUNK, :], preferred_element_type=F32)
        acc = part if acc is None else acc + part
    y_ref[...] = x + mod_ref[0, 5:6, :] * acc


def _merge_ffn(x2d, mod, og, o_f, o_b, lgo, od, gates, lw, sw, l, tiles_per_mod, tm):
    t_total = x2d.shape[0]
    tile = lambda c: pl.BlockSpec((tm, c), lambda i: (i, 0))
    return pl.pallas_call(
        _merge_ffn_kernel,
        grid=(t_total // tm,),
        in_specs=[
            tile(D_MODEL),
            pl.BlockSpec((1, 6, D_MODEL), lambda i: (i // tiles_per_mod, 0, 0)),
            tile(BRANCH_W), tile(BRANCH_W), tile(BRANCH_W), tile(BRANCH_W), tile(BRANCH_W),
            tile(N_BRANCH * D_MODEL),
            _full_spec((1, GLA_DV)),
            _layer_spec((N_BRANCH, BRANCH_W, D_MODEL), l),
            _layer_spec((D_MODEL, D_MODEL), l),
            _full_spec((1, D_MODEL)),
            _layer_spec((D_MODEL, 2 * FFN_HIDDEN), l),
            _layer_spec((FFN_HIDDEN, D_MODEL), l),
        ],
        out_specs=tile(D_MODEL),
        out_shape=jax.ShapeDtypeStruct((t_total, D_MODEL), F32),
        compiler_params=pltpu.CompilerParams(
            dimension_semantics=("parallel",), vmem_limit_bytes=VMEM_LIMIT),
        name="merge_ffn",
    )(x2d, mod, og, o_f, o_b, lgo, od, gates, lw["gout"], sw["w_branch"], sw["w_out"],
      lw["norm2"], sw["w_ffn_in"], sw["w_ffn_out"])


def _shared_weights(p):
    return {
        "w_in_t": jnp.swapaxes(p["w_in"], 1, 2).astype(BF16),
        "w_branch": p["w_branch"].astype(BF16), "w_out": p["w_out"].astype(BF16),
        "w_ffn_in": p["w_ffn_in"].astype(BF16), "w_ffn_out": p["w_ffn_out"].astype(BF16),
    }


def _layer_weights(l, p):
    scale = HEAD_DIM ** -0.5 * math.log2(math.e)
    head_bound = lambda g, s: HEAD_DIM * (NORM_MARGIN * s) ** 2 * jnp.max(g * g)
    norm_bounds = lambda gq_, gk_: jnp.stack([head_bound(gq_, scale), head_bound(gk_, 1.0)])
    gq_row = jnp.concatenate([jnp.tile(p["gqa_q_norm"][l], GQA_HEADS),
                              jnp.tile(p["diff_q_norm"][l], 2 * DIFF_HEADS)]) * scale
    gk_row = jnp.concatenate([jnp.tile(p["gqa_k_norm"][l], GQA_KV_HEADS),
                              jnp.tile(p["diff_k_norm"][l], 2 * DIFF_HEADS)])
    idx = jnp.arange(2 * LANES) // HEAD_DIM
    gmat = jnp.where(idx[:, None] == idx[None, :], 1.0 / HEAD_DIM, 0.0).astype(BF16)
    up = jnp.zeros((LANES, 2 * GLA_W), F32)
    up = up.at[0:GLA_RANK, 0:GLA_W].set(p["gla_alpha_up"][l, 0])
    up = up.at[GLA_RANK:2 * GLA_RANK, GLA_W:].set(p["gla_alpha_up"][l, 1])
    ub = p["gla_alpha_bias"][l].reshape(1, 2 * GLA_W)
    up_hi = up.astype(BF16)
    up = jnp.stack([up_hi, (up - up_hi.astype(F32)).astype(BF16)])
    return {
        "gq": gq_row.reshape(1, 2 * N_QG), "gk": gk_row.reshape(1, LANES + BRANCH_W), "gmat": gmat,
        "up": up, "ub": ub,
        "kb_g": norm_bounds(p["gqa_q_norm"][l], p["gqa_k_norm"][l]),
        "kb_d": norm_bounds(p["diff_q_norm"][l], p["diff_k_norm"][l]),
        "norm1": p["norm1"][l].reshape(1, D_MODEL), "norm2": p["norm2"][l].reshape(1, D_MODEL),
        "gout": p["gla_out_norm"][l].reshape(1, GLA_DV),
        "gsub": p["diff_sub_norm"][l].reshape(1, DIFF_DV),
        "lam": p["diff_lambda"][l],
    }


def _rope_tables(n_tokens):
    n_rows = n_tokens // GRID_W
    row = jnp.repeat(jnp.arange(n_rows, dtype=F32), GRID_W)
    col = jnp.tile(jnp.arange(GRID_W, dtype=F32), n_rows)
    n_freq = HEAD_DIM // 4
    freqs = ROPE_THETA ** (-jnp.arange(n_freq, dtype=F32) / n_freq)
    ar, ac = row[:, None] * freqs, col[:, None] * freqs
    cos = jnp.concatenate([jnp.cos(ar), jnp.cos(ar), jnp.cos(ac), jnp.cos(ac)], axis=1)
    sin = jnp.concatenate([-jnp.sin(ar), jnp.sin(ar), -jnp.sin(ac), jnp.sin(ac)], axis=1)
    return jnp.tile(cos, (1, LANES // HEAD_DIM)), jnp.tile(sin, (1, LANES // HEAD_DIM))


def _with_ones_rows(v_t):
    lead = v_t.shape[:-2]
    s = v_t.shape[-1]
    ones = jnp.ones(lead + (1, s), v_t.dtype)
    zeros = jnp.zeros(lead + (BF16_ROWS - 1, s), v_t.dtype)
    return jnp.concatenate([v_t, ones, zeros], axis=-2).astype(BF16)


def _cache_layouts(l, cache_gqa_k, cache_gqa_v, cache_diff_k, cache_diff_v):
    b = cache_gqa_k.shape[0]
    gk = jnp.transpose(cache_gqa_k[:, l], (0, 2, 1, 3))
    kd_g = gk.astype(BF16)
    vt_g = _with_ones_rows(jnp.transpose(cache_gqa_v[:, l], (0, 2, 3, 1)))
    dk = jnp.transpose(cache_diff_k[:, l], (0, 2, 3, 1, 4)).reshape(b, 2 * DIFF_HEADS, PAST_LEN, HEAD_DIM)
    kd_d = dk.astype(BF16)
    vt_d = _with_ones_rows(jnp.transpose(cache_diff_v[:, l], (0, 2, 3, 1)))
    return kd_g, vt_g, kd_d, vt_d


def _run_layer(x2d, mod, lw, sw, l, n_batch, n_tok, rope_tabs, ctx, emit_cache, prev_caches, cfg):
    tiles_per_mod = lambda tm: (n_tok // tm) if mod.shape[0] > 1 else (n_batch * n_tok // tm)
    if ctx is None:
        cache_kv = kd_gc = kd_dc = s0 = None
    else:
        kd_gc, vt_gc, kd_dc, vt_dc, s0 = ctx
        cache_kv = (kd_gc, kd_dc, vt_gc, vt_dc)
    outs = _in_projection(x2d, mod, lw, sw, l, n_batch, n_tok, cache_kv, rope_tabs, emit_cache, prev_caches,
                          cfg["tm_in"])
    qg, qd, kdg, kdd, vtg, vtd, lq, lk, lv, lgo, la, gates = outs[:12]
    og = _gqa_attention(qg, lw["kb_g"], kdg, vtg, kd_gc, n_batch, n_tok, cfg["tq_gqa"], cfg["tk"],
                        cfg["hp_gqa"])
    lam_init = 0.8 - 0.6 * math.exp(-0.3 * l)
    od = _diff_attention(qd, lw["kb_d"], lw["lam"], lw["gsub"], kdd, vtd, kd_dc, n_batch, n_tok,
                         cfg["tq_diff"], cfg["tk"], cfg["hp_diff"], lam_init)
    o_f, o_b, s_fin = _gla(lq, lk, lv, la, s0, n_batch, n_tok, cfg["tb_gla"], cfg["bg_gla"])
    x2d = _merge_ffn(x2d, mod, og, o_f, o_b, lgo, od, gates, lw, sw, l, tiles_per_mod(cfg["tm"]), cfg["tm"])
    cache = tuple(outs[12:]) + (s_fin,) if emit_cache else None
    return x2d, cache


PROMPT_CFG = dict(tm_in=512, tq_gqa=256, tq_diff=256, tk=256, hp_gqa=2, hp_diff=4, tb_gla=256, bg_gla=2, tm=512)
SAMPLE_CFG = dict(tm_in=512, tq_gqa=1024, tq_diff=2048, tk=1536, hp_gqa=1, hp_diff=1, tb_gla=256, bg_gla=2, tm=512)


def kernel(x_prompt, x_sample, c, cache_gqa_k, cache_gqa_v, state_gla, cache_diff_k, cache_diff_v, c_ctx, w_mod, b_mod, norm1, norm2, w_in, gqa_q_norm, gqa_k_norm, gla_alpha_up, gla_alpha_bias, gla_out_norm, diff_q_norm, diff_k_norm, diff_lambda, diff_sub_norm, w_branch, w_out, w_ffn_in, w_ffn_out):
    p = {
        "norm1": norm1, "norm2": norm2, "w_in": w_in, "gqa_q_norm": gqa_q_norm, "gqa_k_norm": gqa_k_norm,
        "gla_alpha_up": gla_alpha_up, "gla_alpha_bias": gla_alpha_bias, "gla_out_norm": gla_out_norm,
        "diff_q_norm": diff_q_norm, "diff_k_norm": diff_k_norm, "diff_lambda": diff_lambda,
        "diff_sub_norm": diff_sub_norm, "w_branch": w_branch, "w_out": w_out,
        "w_ffn_in": w_ffn_in, "w_ffn_out": w_ffn_out,
    }
    n_ctx_b, n_ctx = x_prompt.shape[:2]
    n_lat_b, n_lat = x_sample.shape[:2]
    cond_rows = jnp.concatenate(
        [c_ctx[None, :], c, jnp.zeros((F32_ROWS - 1 - n_lat_b, D_MODEL), F32)], axis=0)
    mod_all = _modulation(cond_rows, w_mod, b_mod)
    weights = [_layer_weights(l, p) for l in range(DEPTH)]
    shared = _shared_weights(p)

    y = x_prompt.reshape(n_ctx_b * n_ctx, D_MODEL)
    caches = []
    for l in range(DEPTH):
        mod = mod_all[l, 0:1].reshape(1, 6, D_MODEL)
        prev = [cc[:4] for cc in caches] if l == DEPTH - 1 else None
        y, cache = _run_layer(y, mod, weights[l], shared, l, n_ctx_b, n_ctx, None, None, True, prev, PROMPT_CFG)
        caches.append(cache)
    y_prompt = y.reshape(n_ctx_b, n_ctx, D_MODEL)
    stacked = caches[-1]
    new_gqa_k = stacked[0].reshape(n_ctx_b, DEPTH, n_ctx, GQA_KV_HEADS, HEAD_DIM)
    new_diff_k = stacked[1].reshape(n_ctx_b, DEPTH, n_ctx, DIFF_HEADS, 2, HEAD_DIM)
    new_gqa_v = stacked[2].reshape(n_ctx_b, DEPTH, n_ctx, GQA_KV_HEADS, HEAD_DIM)
    new_diff_v = stacked[3].reshape(n_ctx_b, DEPTH, n_ctx, DIFF_HEADS, DIFF_DV)
    new_state_gla = jnp.stack([cc[4] for cc in caches], axis=1)

    rope_tabs = _rope_tables(n_lat)
    y = x_sample.reshape(n_lat_b * n_lat, D_MODEL)
    for l in range(DEPTH):
        mod = mod_all[l, 1:1 + n_lat_b].reshape(n_lat_b, 6, D_MODEL)
        ctx = _cache_layouts(l, cache_gqa_k, cache_gqa_v, cache_diff_k, cache_diff_v) + (state_gla[:, l],)
        y, _ = _run_layer(y, mod, weights[l], shared, l, n_lat_b, n_lat, rope_tabs, ctx, False, None, SAMPLE_CFG)
    y_sample = y.reshape(n_lat_b, n_lat, D_MODEL)
    return (y_prompt, y_sample, new_gqa_k, new_gqa_v, new_state_gla, new_diff_k, new_diff_v)
```

```python
import functools
import math

import jax
import jax.numpy as jnp
from jax import lax
from jax.experimental import pallas as pl
from jax.experimental.pallas import tpu as pltpu

D_MODEL = 1024
DEPTH = 2
PAST_LEN = 512
GRID_W = 64
HEAD_DIM = 64
GQA_HEADS = 8
GQA_KV_HEADS = 2
GQA_GROUP = GQA_HEADS // GQA_KV_HEADS
GLA_HEADS = 4
GLA_DK = 64
GLA_DV = 128
GLA_RANK = 16
GLA_TAU = 16.0
GLA_CHUNK = 64
DIFF_HEADS = 4
DIFF_DV = 2 * HEAD_DIM
N_BRANCH = 3
BRANCH_W = 512
FFN_HIDDEN = ((8 * D_MODEL + 3 * 256 - 1) // (3 * 256)) * 256
ROPE_THETA = 10000.0
EPS = 1e-6

LANES = 128
BF16_ROWS = 16
F32_ROWS = 8
VMEM_LIMIT = 56 * 1024 * 1024

F32 = jnp.float32
BF16 = jnp.bfloat16

R_GQ = 0
R_KV = R_GQ + GQA_HEADS * HEAD_DIM
R_GLA = R_KV + 2 * GQA_KV_HEADS * HEAD_DIM
R_LR = R_GLA + 2 * GLA_HEADS * GLA_DK + GLA_HEADS * GLA_DV
R_LGO = R_LR + 2 * GLA_RANK
R_DQ = R_LGO + GLA_HEADS * GLA_DV
R_DK = R_DQ + DIFF_HEADS * 2 * HEAD_DIM
R_DV = R_DK + DIFF_HEADS * 2 * HEAD_DIM
R_MG = R_DV + DIFF_HEADS * DIFF_DV
N_IN = R_MG + N_BRANCH * D_MODEL
NT_DIMS = (((1,), (1,)), ((), ()))
N_QG = GQA_HEADS * HEAD_DIM
GLA_W = GLA_HEADS * GLA_DK
GQA_VT_ROWS = HEAD_DIM + BF16_ROWS
DIFF_VT_ROWS = DIFF_DV + BF16_ROWS


def _sigmoid(x):
    return 0.5 * jnp.tanh(0.5 * x) + 0.5


def _silu(x):
    return x * _sigmoid(x)


def _bf16_terms(x):
    hi = x.astype(BF16)
    r1 = x - hi.astype(F32)
    mid = r1.astype(BF16)
    lo = (r1 - mid.astype(F32)).astype(BF16)
    return hi, mid, lo


def _full_spec(shape):
    n = len(shape)
    return pl.BlockSpec(shape, lambda *_: (0,) * n, pipeline_mode=pl.Buffered(1))


def _layer_spec(shape, l):
    n = len(shape)
    return pl.BlockSpec((None,) + tuple(shape), lambda *_: (l,) + (0,) * n, pipeline_mode=pl.Buffered(1))


def _mod_kernel(cond_ref, w_ref, b_ref, o_ref):
    s_hi, s_mid, _ = _bf16_terms(_silu(cond_ref[...]))
    lhs = jnp.concatenate([s_hi, s_mid], axis=0)
    w = w_ref[0]
    w_hi = w.astype(BF16)
    w_mid = (w - w_hi.astype(F32)).astype(BF16)
    r = jnp.dot(lhs, w_hi, preferred_element_type=F32) + jnp.dot(lhs, w_mid, preferred_element_type=F32)
    n = cond_ref.shape[0]
    o_ref[0] = r[0:n] + r[n:2 * n] + b_ref[0]


def _modulation(cond_rows, w_mod, b_mod):
    tn = 1536
    n_out = 6 * D_MODEL
    return pl.pallas_call(
        _mod_kernel,
        grid=(DEPTH, n_out // tn),
        in_specs=[
            pl.BlockSpec((F32_ROWS, D_MODEL), lambda l, j: (0, 0)),
            pl.BlockSpec((1, D_MODEL, tn), lambda l, j: (l, 0, j)),
            pl.BlockSpec((1, 1, tn), lambda l, j: (l, 0, j)),
        ],
        out_specs=pl.BlockSpec((1, F32_ROWS, tn), lambda l, j: (l, 0, j)),
        out_shape=jax.ShapeDtypeStruct((DEPTH, F32_ROWS, n_out), F32),
        compiler_params=pltpu.CompilerParams(
            dimension_semantics=("parallel", "parallel"), vmem_limit_bytes=VMEM_LIMIT),
        name="modulation",
    )(cond_rows, w_mod, b_mod.reshape(DEPTH, 1, n_out))


def _head_norm(z, gain, gmat):
    outs = []
    n = z.shape[1]
    for c0 in range(0, n, 2 * LANES):
        w = min(2 * LANES, n - c0)
        zz = z[:, c0:c0 + w]
        ms = jnp.dot((zz * zz).astype(BF16), gmat[0:w, 0:w], preferred_element_type=F32)
        outs.append(zz * lax.rsqrt(ms + EPS) * gain[:, c0:c0 + w])
    return outs


def _rope_block(zb, cos, sin, first):
    partner = jnp.where(first, pltpu.roll(zb, LANES - 16, 1), pltpu.roll(zb, 16, 1))
    return zb * cos + partner * sin


N_IN_BASE = 9


def _in_kernel(*refs, rope, emit_cache, prev_layers, tiles_per_batch, cache_tiles):
    if not cache_tiles:
        _in_tile(refs, rope, emit_cache, prev_layers)
        return
    kcg_ref, kcd_ref, vcg_ref, vcd_ref = refs[:4]
    j = pl.program_id(1)

    @pl.when(j < tiles_per_batch)
    def _():
        _in_tile(refs[4:], rope, emit_cache, prev_layers)

    @pl.when(j >= tiles_per_batch)
    def _():
        n_in = 4 + N_IN_BASE + (2 if rope else 0)
        kdg_ref, kdd_ref, vtg_ref, vtd_ref = refs[n_in + 2:n_in + 6]
        kdg_ref[...] = kcg_ref[...]
        kdd_ref[...] = kcd_ref[...]
        vtg_ref[...] = vcg_ref[...]
        vtd_ref[...] = vcd_ref[...]


def _in_tile(refs, rope, emit_cache, prev_layers):
    it = iter(refs)
    (x_ref, mod_ref, g1_ref, wt_ref, gq_ref, gk_ref, gmat_ref, up_ref, ub_ref) = (
        next(it) for _ in range(N_IN_BASE))
    if rope:
        cos_ref, sin_ref = next(it), next(it)
    prev_refs = [[next(it) for _ in range(4)] for _ in range(prev_layers or 0)]
    (qg_ref, qd_ref, kdg_ref, kdd_ref, vtg_ref, vtd_ref,
     lq_ref, lk_ref, lv_ref, lgo_ref, la_ref, gate_ref) = (next(it) for _ in range(12))
    if emit_cache:
        ck_g_ref, ck_d_ref, cv_g_ref, cv_d_ref = (next(it) for _ in range(4))

    tm = x_ref.shape[0]
    bpt = kdg_ref.shape[0]
    nt = tm // bpt
    x = x_ref[...]
    ms = jnp.mean(x * x, axis=-1, keepdims=True)
    h = x * lax.rsqrt(ms + EPS) * g1_ref[...]
    h = h * (1.0 + mod_ref[0, 1:2, :]) + mod_ref[0, 0:1, :]
    hb = h.astype(BF16)

    def proj(r0, n):
        return lax.dot_general(hb, wt_ref[r0:r0 + n, :], NT_DIMS, preferred_element_type=F32)

    lane = lax.broadcasted_iota(jnp.int32, (tm, LANES), 1)
    first16 = (lane % 32) < 16
    gmat = gmat_ref[...]
    if rope:
        cos, sin = cos_ref[...], sin_ref[...]

    def blocks128(parts):
        out = []
        for p in parts:
            for c in range(0, p.shape[1], LANES):
                out.append(p[:, c:c + LANES])
        return out

    def store_queries(z, gain, dst):
        for j, zb in enumerate(blocks128(_head_norm(z, gain, gmat))):
            if rope:
                zb = _rope_block(zb, cos, sin, first16)
            dst[:, j * LANES:(j + 1) * LANES] = zb.astype(BF16)

    def store_cache(dst, c0, value):
        w = value.shape[1]
        if prev_layers is None:
            dst[:, c0:c0 + w] = value
        else:
            for bb in range(bpt):
                dst[bb, prev_layers, :, c0:c0 + w] = value[bb * nt:(bb + 1) * nt, :]

    def store_keys(z, gain, dst, cache_dst):
        for j, zb in enumerate(blocks128(_head_norm(z, gain, gmat))):
            if rope:
                zb = _rope_block(zb, cos, sin, first16)
            if emit_cache:
                store_cache(cache_dst, j * LANES, zb)
            sw = pltpu.roll(zb, HEAD_DIM, 1)
            for bb in range(bpt):
                dst[bb, 2 * j] = zb[bb * nt:(bb + 1) * nt, 0:HEAD_DIM].astype(BF16)
                dst[bb, 2 * j + 1] = sw[bb * nt:(bb + 1) * nt, 0:HEAD_DIM].astype(BF16)

    def store_gate(b):
        gate_ref[:, b * D_MODEL:(b + 1) * D_MODEL] = _sigmoid(proj(R_MG + b * D_MODEL, D_MODEL)).astype(BF16)

    z_gq = proj(R_GQ, N_QG)
    z_dq = proj(R_DQ, N_QG)
    store_gate(0)
    store_queries(z_gq, gq_ref[:, 0:N_QG], qg_ref)
    z_kv = proj(R_KV, 2 * LANES)
    store_queries(z_dq, gq_ref[:, N_QG:2 * N_QG], qd_ref)
    z_dk = proj(R_DK, BRANCH_W)
    store_gate(1)
    store_keys(z_kv[:, 0:LANES], gk_ref[:, 0:LANES], kdg_ref, ck_g_ref if emit_cache else None)
    z_dv = proj(R_DV, BRANCH_W)
    store_keys(z_dk, gk_ref[:, LANES:LANES + BRANCH_W], kdd_ref, ck_d_ref if emit_cache else None)
    zl = proj(R_GLA, R_LR - R_GLA)
    z_lgo = proj(R_LGO, BRANCH_W)
    lr = proj(R_LR, LANES)
    store_gate(2)

    ones_rows = jnp.where(lax.broadcasted_iota(jnp.int32, (BF16_ROWS, nt), 0) == 0, 1.0, 0.0).astype(BF16)
    z_gv = z_kv[:, LANES:2 * LANES]
    if emit_cache:
        store_cache(cv_g_ref, 0, z_gv)
        store_cache(cv_d_ref, 0, z_dv)
        for layer, layer_refs in enumerate(prev_refs):
            for dst, src in zip((ck_g_ref, ck_d_ref, cv_g_ref, cv_d_ref), layer_refs):
                for bb in range(bpt):
                    dst[bb, layer] = src[bb * nt:(bb + 1) * nt, :]
    for bb in range(bpt):
        rows = slice(bb * nt, (bb + 1) * nt)
        vt = z_gv[rows, :].T
        for hh in range(GQA_KV_HEADS):
            vtg_ref[bb, hh, 0:HEAD_DIM, :] = vt[hh * HEAD_DIM:(hh + 1) * HEAD_DIM, :].astype(BF16)
            vtg_ref[bb, hh, HEAD_DIM:GQA_VT_ROWS, :] = ones_rows
        for hh in range(DIFF_HEADS):
            vt = z_dv[rows, hh * LANES:(hh + 1) * LANES].T
            vtd_ref[bb, hh, 0:DIFF_DV, :] = vt.astype(BF16)
            vtd_ref[bb, hh, DIFF_DV:DIFF_VT_ROWS, :] = ones_rows

    o = 0
    lq_ref[...] = (zl[:, o:o + GLA_W] * (GLA_DK ** -0.5)).astype(BF16)
    o += GLA_W
    lk_ref[...] = zl[:, o:o + GLA_W].astype(BF16)
    o += GLA_W
    lv_ref[...] = zl[:, o:o + BRANCH_W].astype(BF16)
    o += BRANCH_W
    lgo_ref[...] = z_lgo.astype(BF16)
    lr_hi, lr_mid, _ = _bf16_terms(lr)
    z = (jnp.dot(lr_hi, up_ref[0], preferred_element_type=F32)
         + jnp.dot(lr_mid, up_ref[0], preferred_element_type=F32)
         + jnp.dot(lr_hi, up_ref[1], preferred_element_type=F32)) + ub_ref[...]
    log_sig = jnp.minimum(z, 0.0) - jnp.log(1.0 + jnp.exp(-jnp.abs(z)))
    la_ref[...] = log_sig * (1.0 / GLA_TAU)


def _in_projection(x2d, mod, lw, sw, l, n_batch, n_tok, cache_kv, rope_tabs, emit_cache, prev_caches, tm):
    n_cache = 0 if cache_kv is None else cache_kv[0].shape[2]
    assert n_cache % tm == 0
    n_keys = n_tok + n_cache
    t_total = n_batch * n_tok
    bpt = max(1, tm // n_tok)
    assert bpt == 1 or (mod.shape[0] == 1 and n_cache == 0 and tm == bpt * n_tok)
    nt = tm // bpt
    tpb = n_tok // nt
    cache_tiles = n_cache // tm
    rope = rope_tabs is not None
    own = lambda j: jnp.minimum(j, tpb - 1)
    tile = lambda c: pl.BlockSpec((tm, c), lambda b, j: (b * tpb + own(j), 0))
    in_specs, args = [], []
    if cache_tiles:
        past = lambda j: jnp.maximum(j - tpb, 0)
        in_specs += [
            pl.BlockSpec((1, GQA_KV_HEADS, tm, HEAD_DIM), lambda b, j: (b, 0, past(j), 0)),
            pl.BlockSpec((1, 2 * DIFF_HEADS, tm, HEAD_DIM), lambda b, j: (b, 0, past(j), 0)),
            pl.BlockSpec((1, GQA_KV_HEADS, GQA_VT_ROWS, tm), lambda b, j: (b, 0, 0, past(j))),
            pl.BlockSpec((1, DIFF_HEADS, DIFF_VT_ROWS, tm), lambda b, j: (b, 0, 0, past(j))),
        ]
        args += list(cache_kv)
    in_specs += [
        tile(D_MODEL),
        pl.BlockSpec((1, 6, D_MODEL), lambda b, j: (b if mod.shape[0] > 1 else 0, 0, 0)),
        _full_spec((1, D_MODEL)),
        _layer_spec((N_IN, D_MODEL), l),
        _full_spec((1, 2 * N_QG)),
        _full_spec((1, LANES + BRANCH_W)),
        _full_spec((2 * LANES, 2 * LANES)),
        _full_spec((2, LANES, 2 * GLA_W)),
        _full_spec((1, 2 * GLA_W)),
    ]
    args += [x2d, mod, lw["norm1"], sw["w_in_t"], lw["gq"], lw["gk"], lw["gmat"], lw["up"], lw["ub"]]
    if rope:
        in_specs += [pl.BlockSpec((tm, LANES), lambda b, j: (own(j), 0))] * 2
        args += list(rope_tabs)
    sd = jax.ShapeDtypeStruct
    out_shape = [
        sd((t_total, BRANCH_W), BF16), sd((t_total, BRANCH_W), BF16),
        sd((n_batch, GQA_KV_HEADS, n_keys, HEAD_DIM), BF16), sd((n_batch, 2 * DIFF_HEADS, n_keys, HEAD_DIM), BF16),
        sd((n_batch, GQA_KV_HEADS, GQA_VT_ROWS, n_keys), BF16), sd((n_batch, DIFF_HEADS, DIFF_VT_ROWS, n_keys), BF16),
        sd((t_total, GLA_W), BF16), sd((t_total, GLA_W), BF16),
        sd((t_total, BRANCH_W), BF16), sd((t_total, BRANCH_W), BF16),
        sd((t_total, 2 * GLA_W), F32), sd((t_total, N_BRANCH * D_MODEL), BF16),
    ]
    out_specs = [
        tile(BRANCH_W), tile(BRANCH_W),
        pl.BlockSpec((bpt, GQA_KV_HEADS, nt, HEAD_DIM), lambda b, j: (b, 0, j, 0)),
        pl.BlockSpec((bpt, 2 * DIFF_HEADS, nt, HEAD_DIM), lambda b, j: (b, 0, j, 0)),
        pl.BlockSpec((bpt, GQA_KV_HEADS, GQA_VT_ROWS, nt), lambda b, j: (b, 0, 0, j)),
        pl.BlockSpec((bpt, DIFF_HEADS, DIFF_VT_ROWS, nt), lambda b, j: (b, 0, 0, j)),
        tile(GLA_W), tile(GLA_W), tile(BRANCH_W), tile(BRANCH_W), tile(2 * GLA_W), tile(N_BRANCH * D_MODEL),
    ]
    cache_widths = (LANES, BRANCH_W, LANES, BRANCH_W)
    prev_layers = None
    if emit_cache and prev_caches is None:
        out_shape += [sd((t_total, w), F32) for w in cache_widths]
        out_specs += [tile(w) for w in cache_widths]
    elif emit_cache:
        prev_layers = len(prev_caches)
        n_layers = prev_layers + 1
        for layer_arrays in prev_caches:
            in_specs += [tile(w) for w in cache_widths]
            args += list(layer_arrays)
        out_shape += [sd((n_batch, n_layers, n_tok, w), F32) for w in cache_widths]
        out_specs += [pl.BlockSpec((bpt, n_layers, nt, w), lambda b, j: (b, 0, j, 0)) for w in cache_widths]
    return pl.pallas_call(
        functools.partial(_in_kernel, rope=rope, emit_cache=emit_cache, prev_layers=prev_layers,
                          tiles_per_batch=tpb, cache_tiles=cache_tiles),
        grid=(n_batch // bpt, tpb + cache_tiles),
        in_specs=in_specs,
        out_specs=out_specs,
        out_shape=out_shape,
        compiler_params=pltpu.CompilerParams(
            dimension_semantics=("parallel", "arbitrary"), vmem_limit_bytes=VMEM_LIMIT),
        name="in_projection",
    )(*args)


MAX_COL = 4 * LANES
SCORE_BOUND = 45.0
NORM_MARGIN = 1.05
SCORES_AHEAD_BYTES = 4 * 1024 * 1024


def _head_queries_t(q_blk, low):
    t = q_blk.astype(F32).T
    return (t[0:HEAD_DIM, :] if low else t[HEAD_DIM:2 * HEAD_DIM, :]).astype(BF16)


def _key_norm2(kd):
    kf = kd.astype(F32)
    return jnp.max(jnp.sum(kf * kf, axis=1, keepdims=True))


def _needs_no_stabiliser(q_norm2, k_norm2):
    return (q_norm2 * k_norm2 <= SCORE_BOUND * SCORE_BOUND).astype(jnp.int32)


def _score_units(heads_cols):
    return [(kh, vh, c) for kh, vh, cols in heads_cols for c in cols]


def _tile_scores(k_ref, qm_ref, unit, col):
    kh, _, c = unit
    return jnp.dot(k_ref[0, kh], qm_ref[:, c * col:(c + 1) * col], preferred_element_type=F32)


def _plain_tiles(k_ref, vt_ref, qm_ref, acc_ref, heads_cols):
    col = acc_ref.shape[-1]
    units = _score_units(heads_cols)
    tile_bytes = k_ref.shape[2] * col * 4
    ahead = max(1, min(len(units), SCORES_AHEAD_BYTES // tile_bytes))
    scores = [_tile_scores(k_ref, qm_ref, unit, col) for unit in units[:ahead]]
    for i, (_, vh, c) in enumerate(units):
        if i + ahead < len(units):
            scores.append(_tile_scores(k_ref, qm_ref, units[i + ahead], col))
        p = jnp.exp2(scores[i]).astype(BF16)
        acc_ref[c] += jnp.dot(vt_ref[0, vh], p, preferred_element_type=F32)


def _online_tiles(k_ref, vt_ref, qm_ref, m_ref, acc_ref, heads_cols):
    col = acc_ref.shape[-1]
    for unit in _score_units(heads_cols):
        _, vh, c = unit
        s = _tile_scores(k_ref, qm_ref, unit, col)
        m_prev = m_ref[c]
        m_new = jnp.maximum(m_prev, jnp.max(s, axis=0, keepdims=True))
        alpha = jnp.exp2(m_prev - m_new)
        p = jnp.exp2(s - m_new).astype(BF16)
        acc_ref[c] = acc_ref[c] * alpha + jnp.dot(vt_ref[0, vh], p, preferred_element_type=F32)
        m_ref[c] = m_new


def _flash_init(kb_ref, kc_ref, n_key_heads, m_ref, acc_ref, plain_ref):
    k_norm2 = kb_ref[1]
    if kc_ref is not None:
        for kh in range(n_key_heads):
            k_norm2 = jnp.maximum(k_norm2, _key_norm2(kc_ref[0, kh]))
    plain_ref[0] = _needs_no_stabiliser(kb_ref[0], k_norm2)
    m_ref[...] = jnp.full(m_ref.shape, -jnp.inf, F32)
    acc_ref[...] = jnp.zeros(acc_ref.shape, F32)


def _flash_step(k_ref, vt_ref, qm_ref, m_ref, acc_ref, plain_ref, heads_cols):
    @pl.when(plain_ref[0] == 1)
    def _():
        _plain_tiles(k_ref, vt_ref, qm_ref, acc_ref, heads_cols)

    @pl.when(plain_ref[0] != 1)
    def _():
        _online_tiles(k_ref, vt_ref, qm_ref, m_ref, acc_ref, heads_cols)


def _gqa_kernel(*refs, has_cache, tq, hp):
    if has_cache:
        kb_ref, q_ref, k_ref, vt_ref, kc_ref, o_ref, qm_ref, m_ref, acc_ref, plain_ref = refs
    else:
        kb_ref, q_ref, k_ref, vt_ref, o_ref, qm_ref, m_ref, acc_ref, plain_ref = refs
        kc_ref = None
    kt = pl.program_id(3)
    col = acc_ref.shape[-1]
    per_head = tq // col
    per_kv = GQA_GROUP * per_head
    heads_cols = [(hh, hh, range(hh * per_kv, (hh + 1) * per_kv)) for hh in range(hp)]

    @pl.when(kt == 0)
    def _():
        for hh in range(hp):
            for g in range(GQA_GROUP):
                c0 = hh * 2 * LANES + (g // 2) * LANES
                row0 = (hh * GQA_GROUP + g) * tq
                qm_ref[:, row0:row0 + tq] = _head_queries_t(q_ref[0, :, c0:c0 + LANES], g % 2 == 0)
        _flash_init(kb_ref, kc_ref, hp, m_ref, acc_ref, plain_ref)

    _flash_step(k_ref, vt_ref, qm_ref, m_ref, acc_ref, plain_ref, heads_cols)

    @pl.when(kt == pl.num_programs(3) - 1)
    def _():
        for hh in range(hp):
            for part in range(per_head):
                heads = []
                for g in range(GQA_GROUP):
                    acc = acc_ref[hh * per_kv + g * per_head + part]
                    heads.append(acc[0:HEAD_DIM, :] / acc[HEAD_DIM:HEAD_DIM + 1, :])
                o_ref[0, part * col:(part + 1) * col, hh * 2 * LANES:(hh + 1) * 2 * LANES] = (
                    jnp.concatenate(heads, axis=0).T.astype(BF16))


def _kv_specs(key_heads_blk, val_heads_blk, vt_rows, tk, kd_c):
    specs = [pl.BlockSpec((1, key_heads_blk, tk, HEAD_DIM), lambda b, h, qi, kt: (b, h, kt, 0)),
             pl.BlockSpec((1, val_heads_blk, vt_rows, tk), lambda b, h, qi, kt: (b, h, 0, kt))]
    if kd_c is not None:
        specs.append(pl.BlockSpec((1, key_heads_blk, kd_c.shape[2], HEAD_DIM), lambda b, h, qi, kt: (b, h, 0, 0)))
    return specs


def _gqa_attention(q, k_bound, kd, vt, kd_c, n_batch, n_tok, tq, tk, hp):
    has_cache = kd_c is not None
    nk = kd.shape[2] // tk
    rows = hp * GQA_GROUP * tq
    col = min(MAX_COL, tq)
    q_spec = pl.BlockSpec((1, tq, hp * 2 * LANES), lambda b, h, qi, kt: (b, qi, h))
    in_specs = [pl.BlockSpec(memory_space=pltpu.SMEM), q_spec]
    in_specs += _kv_specs(hp, hp, GQA_VT_ROWS, tk, kd_c)
    args = [k_bound, q.reshape(n_batch, n_tok, BRANCH_W), kd, vt]
    if has_cache:
        args.append(kd_c)
    out = pl.pallas_call(
        functools.partial(_gqa_kernel, has_cache=has_cache, tq=tq, hp=hp),
        grid=(n_batch, GQA_KV_HEADS // hp, n_tok // tq, nk),
        in_specs=in_specs,
        out_specs=q_spec,
        out_shape=jax.ShapeDtypeStruct((n_batch, n_tok, BRANCH_W), BF16),
        scratch_shapes=[
            pltpu.VMEM((HEAD_DIM, rows), BF16),
            pltpu.VMEM((rows // col, 1, col), F32),
            pltpu.VMEM((rows // col, GQA_VT_ROWS, col), F32),
            pltpu.SMEM((1,), jnp.int32),
        ],
        compiler_params=pltpu.CompilerParams(
            dimension_semantics=("parallel", "parallel", "parallel", "arbitrary"),
            vmem_limit_bytes=VMEM_LIMIT),
        name="gqa_attention",
    )(*args)
    return out.reshape(n_batch * n_tok, BRANCH_W)


def _diff_kernel(*refs, has_cache, lam_init, hp):
    if has_cache:
        (kb_ref, q_ref, lam_ref, gsub_ref, k_ref, vt_ref, kc_ref, o_ref,
         qm_ref, m_ref, acc_ref, plain_ref) = refs
    else:
        kb_ref, q_ref, lam_ref, gsub_ref, k_ref, vt_ref, o_ref, qm_ref, m_ref, acc_ref, plain_ref = refs
        kc_ref = None
    kt = pl.program_id(3)
    tq = q_ref.shape[1]
    col = acc_ref.shape[-1]
    per_map = tq // col
    heads_cols = [(2 * hh + mm, hh, range((2 * hh + mm) * per_map, (2 * hh + mm + 1) * per_map))
                  for hh in range(hp) for mm in range(2)]

    @pl.when(kt == 0)
    def _():
        for hh in range(hp):
            blk = q_ref[0, :, hh * LANES:(hh + 1) * LANES]
            for mm in range(2):
                row0 = (2 * hh + mm) * tq
                qm_ref[:, row0:row0 + tq] = _head_queries_t(blk, mm == 0)
        _flash_init(kb_ref, kc_ref, 2 * hp, m_ref, acc_ref, plain_ref)

    _flash_step(k_ref, vt_ref, qm_ref, m_ref, acc_ref, plain_ref, heads_cols)

    @pl.when(kt == pl.num_programs(3) - 1)
    def _():
        lp = lam_ref[...]
        lam = (jnp.exp(jnp.sum(lp[0:1] * lp[1:2], axis=-1, keepdims=True))
               - jnp.exp(jnp.sum(lp[2:3] * lp[3:4], axis=-1, keepdims=True)) + lam_init)
        for hh in range(hp):
            for part in range(per_map):
                a0, a1 = acc_ref[2 * hh * per_map + part], acc_ref[(2 * hh + 1) * per_map + part]
                o0 = a0[0:DIFF_DV, :] / a0[DIFF_DV:DIFF_DV + 1, :]
                o1 = a1[0:DIFF_DV, :] / a1[DIFF_DV:DIFF_DV + 1, :]
                d = (o0 - lam * o1).T
                ms = jnp.mean(d * d, axis=-1, keepdims=True)
                o_ref[0, part * col:(part + 1) * col, hh * LANES:(hh + 1) * LANES] = (
                    d * lax.rsqrt(ms + EPS) * gsub_ref[...] * (1.0 - lam_init)).astype(BF16)


def _diff_attention(q, k_bound, lam_p, gsub, kd, vt, kd_c, n_batch, n_tok, tq, tk, hp, lam_init):
    has_cache = kd_c is not None
    nk = kd.shape[2] // tk
    col = min(MAX_COL, tq)
    rows = hp * 2 * tq
    q_spec = pl.BlockSpec((1, tq, hp * LANES), lambda b, h, qi, kt: (b, qi, h))
    in_specs = [
        pl.BlockSpec(memory_space=pltpu.SMEM),
        q_spec,
        pl.BlockSpec((4, HEAD_DIM), lambda b, h, qi, kt: (0, 0)),
        pl.BlockSpec((1, DIFF_DV), lambda b, h, qi, kt: (0, 0)),
    ]
    in_specs += _kv_specs(2 * hp, hp, DIFF_VT_ROWS, tk, kd_c)
    args = [k_bound, q.reshape(n_batch, n_tok, BRANCH_W), lam_p, gsub, kd, vt]
    if has_cache:
        args.append(kd_c)
    out = pl.pallas_call(
        functools.partial(_diff_kernel, has_cache=has_cache, lam_init=lam_init, hp=hp),
        grid=(n_batch, DIFF_HEADS // hp, n_tok // tq, nk),
        in_specs=in_specs,
        out_specs=q_spec,
        out_shape=jax.ShapeDtypeStruct((n_batch, n_tok, BRANCH_W), BF16),
        scratch_shapes=[
            pltpu.VMEM((HEAD_DIM, rows), BF16),
            pltpu.VMEM((rows // col, 1, col), F32),
            pltpu.VMEM((rows // col, DIFF_VT_ROWS, col), F32),
            pltpu.SMEM((1,), jnp.int32),
        ],
        compiler_params=pltpu.CompilerParams(
            dimension_semantics=("parallel", "parallel", "parallel", "arbitrary"),
            vmem_limit_bytes=VMEM_LIMIT),
        name="diff_attention",
    )(*args)
    return out.reshape(n_batch * n_tok, BRANCH_W)


def _gla_kernel(*refs, has_s0, n_chunk, bg):
    if has_s0:
        qf, kf, vf, laf, qb, kb, vb, lab, s0_ref, of_ref, ob_ref, sfin_ref, st_ref = refs
    else:
        qf, kf, vf, laf, qb, kb, vb, lab, of_ref, ob_ref, sfin_ref, st_ref = refs
    i = pl.program_id(1)
    ck = GLA_CHUNK

    @pl.when(i == 0)
    def _():
        for bb in range(bg):
            for d in range(2):
                for hd in range(GLA_HEADS):
                    if has_s0:
                        s = s0_ref[bb, d, hd]
                        z = jnp.zeros_like(s)
                        padded = jnp.concatenate([s, z] if hd % 2 == 0 else [z, s], axis=0)
                        st_ref[bb, d, hd] = padded.T
                    else:
                        st_ref[bb, d, hd] = jnp.zeros((GLA_DV, LANES), F32)

    tb = qf.shape[1]
    r = lax.broadcasted_iota(jnp.int32, (tb, tb), 0)
    c = lax.broadcasted_iota(jnp.int32, (tb, tb), 1)
    lane = lax.broadcasted_iota(jnp.int32, (tb, LANES), 1)
    chunk_id = lax.broadcasted_iota(jnp.int32, (tb, GLA_W), 0) // ck
    zero_row = jnp.zeros((1, GLA_W), F32)
    streams = []
    for bb in range(bg):
        streams.append((bb, 0, qf.at[bb], kf.at[bb], vf.at[bb], laf.at[bb], of_ref.at[bb], c <= r))
        streams.append((bb, 1, qb.at[bb], kb.at[bb], vb.at[bb], lab.at[bb], ob_ref.at[bb], c >= r))
    nt = (((1,), (1,)), ((), ()))
    g_all = [sum(jnp.dot(tri.astype(F32).astype(BF16), part, preferred_element_type=F32)
                 for part in _bf16_terms(la_r[...]))
             for (_, _, _, _, _, la_r, _, tri) in streams]
    prepared = []
    for g, (bb, d, q_r, k_r, v_r, la_r, o_r, tri) in zip(g_all, streams):
        if d == 0:
            bounds = [zero_row] + [g[ck * j - 1:ck * j, :] for j in range(1, n_chunk)]
            g_end = g[tb - 1:tb, :]
        else:
            bounds = [g[ck * (j + 1):ck * (j + 1) + 1, :] for j in range(n_chunk - 1)] + [zero_row]
            g_end = g[0:1, :]
        b_rows = jnp.concatenate([jnp.broadcast_to(b, (ck, GLA_W)) for b in bounds], axis=0)
        q = q_r[...].astype(F32)
        k = k_r[...].astype(F32)
        q_dec = q * jnp.exp(g - b_rows)
        q_glob = q * jnp.exp(g)
        k_end = k * jnp.exp(g_end - g)
        k_rel = []
        for j in range(n_chunk):
            reach = (chunk_id <= j) if d == 0 else (chunk_id >= j)
            k_rel.append((k * jnp.exp(jnp.where(reach, bounds[j] - g, 0.0))).astype(BF16))
        prepared.append((q_dec, q_glob, k_end, k_rel, jnp.exp(g_end), v_r[...]))

    partial = []
    for (bb, d, *_), (q_dec, q_glob, k_end, k_rel, decay, v) in zip(streams, prepared):
        per_head = []
        for hd in range(GLA_HEADS):
            pair = slice((hd // 2) * LANES, (hd // 2 + 1) * LANES)
            hv = slice(hd * GLA_DV, (hd + 1) * GLA_DV)
            keep = (lane < GLA_DK) if hd % 2 == 0 else (lane >= GLA_DK)
            qd_m = jnp.where(keep, q_dec[:, pair], 0.0).astype(BF16)
            qg_m = jnp.where(keep, q_glob[:, pair], 0.0).astype(BF16)
            ke_m = jnp.where(keep, k_end[:, pair], 0.0).astype(BF16)
            a_rows = [lax.dot_general(qd_m[ck * j:ck * (j + 1), :], k_rel[j][:, pair], nt,
                                      preferred_element_type=F32) for j in range(n_chunk)]
            s_t = st_ref[bb, d, hd]
            o_state = lax.dot_general(qg_m, s_t.astype(BF16), nt, preferred_element_type=F32)
            ds_t = lax.dot_general(v[:, hv], ke_m, (((0,), (0,)), ((), ())), preferred_element_type=F32)
            st_ref[bb, d, hd] = s_t * decay[:, pair] + ds_t
            per_head.append((a_rows, o_state))
        partial.append(per_head)

    for (bb, d, _, _, _, _, o_r, tri), prep, per_head in zip(streams, prepared, partial):
        v = prep[5]
        for hd in range(GLA_HEADS):
            hv = slice(hd * GLA_DV, (hd + 1) * GLA_DV)
            a_rows, o_state = per_head[hd]
            a = jnp.where(tri, jnp.concatenate(a_rows, axis=0), 0.0).astype(BF16)
            o_r[:, hv] = (jnp.dot(a, v[:, hv], preferred_element_type=F32) + o_state).astype(BF16)

    @pl.when(i == pl.num_programs(1) - 1)
    def _():
        for bb in range(bg):
            for d in range(2):
                for hd in range(GLA_HEADS):
                    t = st_ref[bb, d, hd].T
                    sfin_ref[bb, d, hd] = t[(hd % 2) * GLA_DK:(hd % 2 + 1) * GLA_DK, :]


def _gla(lq, lk, lv, la, s0, n_batch, n_tok, tb, bg):
    nb = n_tok // tb
    has_s0 = s0 is not None
    as3d = lambda x: x.reshape(n_batch, n_tok, x.shape[-1])
    fwd = lambda b, i: (b, i, 0)
    bwd = lambda b, i: (b, nb - 1 - i, 0)
    bwd_la = lambda b, i: (b, nb - 1 - i, 1)
    in_specs = [
        pl.BlockSpec((bg, tb, GLA_W), fwd), pl.BlockSpec((bg, tb, GLA_W), fwd),
        pl.BlockSpec((bg, tb, BRANCH_W), fwd), pl.BlockSpec((bg, tb, GLA_W), fwd),
        pl.BlockSpec((bg, tb, GLA_W), bwd), pl.BlockSpec((bg, tb, GLA_W), bwd),
        pl.BlockSpec((bg, tb, BRANCH_W), bwd), pl.BlockSpec((bg, tb, GLA_W), bwd_la),
    ]
    args = [as3d(lq), as3d(lk), as3d(lv), as3d(la)] * 2
    state_spec = pl.BlockSpec((bg, 2, GLA_HEADS, GLA_DK, GLA_DV), lambda b, i: (b, 0, 0, 0, 0))
    if has_s0:
        in_specs.append(state_spec)
        args.append(s0)
    o_f, o_b, s_fin = pl.pallas_call(
        functools.partial(_gla_kernel, has_s0=has_s0, n_chunk=tb // GLA_CHUNK, bg=bg),
        grid=(n_batch // bg, nb),
        in_specs=in_specs,
        out_specs=[pl.BlockSpec((bg, tb, BRANCH_W), fwd), pl.BlockSpec((bg, tb, BRANCH_W), bwd), state_spec],
        out_shape=[
            jax.ShapeDtypeStruct((n_batch, n_tok, BRANCH_W), BF16),
            jax.ShapeDtypeStruct((n_batch, n_tok, BRANCH_W), BF16),
            jax.ShapeDtypeStruct((n_batch, 2, GLA_HEADS, GLA_DK, GLA_DV), F32),
        ],
        scratch_shapes=[pltpu.VMEM((bg, 2, GLA_HEADS, GLA_DV, LANES), F32)],
        compiler_params=pltpu.CompilerParams(
            dimension_semantics=("parallel", "arbitrary"), vmem_limit_bytes=VMEM_LIMIT),
        name="gla",
    )(*args)
    flat = lambda x: x.reshape(n_batch * n_tok, BRANCH_W)
    return flat(o_f), flat(o_b), s_fin


FFN_CHUNK = 256


def _merge_ffn_kernel(x_ref, mod_ref, og_ref, of_ref, ob_ref, lgo_ref, od_ref, gate_ref,
                      gout_ref, wb_ref, wo_ref, g2_ref, wi_ref, wd_ref, y_ref):
    o_gla = of_ref[...].astype(F32) + ob_ref[...].astype(F32)
    gla_parts = []
    for hd in range(GLA_HEADS):
        blk = o_gla[:, hd * GLA_DV:(hd + 1) * GLA_DV]
        ms = jnp.mean(blk * blk, axis=-1, keepdims=True)
        gla_parts.append(blk * lax.rsqrt(ms + EPS) * gout_ref[...])
    gla = jnp.concatenate(gla_parts, axis=1) * _silu(lgo_ref[...].astype(F32))
    branches = (og_ref[...], gla.astype(BF16), od_ref[...])
    mixed = None
    for b, ob in enumerate(branches):
        y = jnp.dot(ob, wb_ref[b], preferred_element_type=F32)
        y = y * gate_ref[:, b * D_MODEL:(b + 1) * D_MODEL].astype(F32)
        mixed = y if mixed is None else mixed + y
    out = jnp.dot(mixed.astype(BF16), wo_ref[...], preferred_element_type=F32)
    x = x_ref[...] + mod_ref[0, 2:3, :] * out

    ms = jnp.mean(x * x, axis=-1, keepdims=True)
    h = x * lax.rsqrt(ms + EPS) * g2_ref[...]
    h = h * (1.0 + mod_ref[0, 4:5, :]) + mod_ref[0, 3:4, :]
    hb = h.astype(BF16)

    def up_proj(c0):
        a = jnp.dot(hb, wi_ref[:, c0:c0 + FFN_CHUNK], preferred_element_type=F32)
        u = jnp.dot(hb, wi_ref[:, FFN_HIDDEN + c0:FFN_HIDDEN + c0 + FFN_CHUNK], preferred_element_type=F32)
        return a, u

    acc = None
    chunks = list(range(0, FFN_HIDDEN, FFN_CHUNK))
    nxt = up_proj(chunks[0])
    for i, c0 in enumerate(chunks):
        a, u = nxt
        if i + 1 < len(chunks):
            nxt = up_proj(chunks[i + 1])
        act = (_silu(a) * u).astype(BF16)
        part = jnp.dot(act, wd_ref[c0:c0 + FFN_CHUNK, :], preferred_element_type=F32)
        acc = part if acc is None else acc + part
    y_ref[...] = x + mod_ref[0, 5:6, :] * acc


def _merge_ffn(x2d, mod, og, o_f, o_b, lgo, od, gates, lw, sw, l, tiles_per_mod, tm):
    t_total = x2d.shape[0]
    tile = lambda c: pl.BlockSpec((tm, c), lambda i: (i, 0))
    return pl.pallas_call(
        _merge_ffn_kernel,
        grid=(t_total // tm,),
        in_specs=[
            tile(D_MODEL),
            pl.BlockSpec((1, 6, D_MODEL), lambda i: (i // tiles_per_mod, 0, 0)),
            tile(BRANCH_W), tile(BRANCH_W), tile(BRANCH_W), tile(BRANCH_W), tile(BRANCH_W),
            tile(N_BRANCH * D_MODEL),
            _full_spec((1, GLA_DV)),
            _layer_spec((N_BRANCH, BRANCH_W, D_MODEL), l),
            _layer_spec((D_MODEL, D_MODEL), l),
            _full_spec((1, D_MODEL)),
            _layer_spec((D_MODEL, 2 * FFN_HIDDEN), l),
            _layer_spec((FFN_HIDDEN, D_MODEL), l),
        ],
        out_specs=tile(D_MODEL),
        out_shape=jax.ShapeDtypeStruct((t_total, D_MODEL), F32),
        compiler_params=pltpu.CompilerParams(
            dimension_semantics=("parallel",), vmem_limit_bytes=VMEM_LIMIT),
        name="merge_ffn",
    )(x2d, mod, og, o_f, o_b, lgo, od, gates, lw["gout"], sw["w_branch"], sw["w_out"],
      lw["norm2"], sw["w_ffn_in"], sw["w_ffn_out"])


def _shared_weights(p):
    return {
        "w_in_t": jnp.swapaxes(p["w_in"], 1, 2).astype(BF16),
        "w_branch": p["w_branch"].astype(BF16), "w_out": p["w_out"].astype(BF16),
        "w_ffn_in": p["w_ffn_in"].astype(BF16), "w_ffn_out": p["w_ffn_out"].astype(BF16),
    }


def _layer_weights(l, p):
    scale = HEAD_DIM ** -0.5 * math.log2(math.e)
    head_bound = lambda g, s: HEAD_DIM * (NORM_MARGIN * s) ** 2 * jnp.max(g * g)
    norm_bounds = lambda gq_, gk_: jnp.stack([head_bound(gq_, scale), head_bound(gk_, 1.0)])
    gq_row = jnp.concatenate([jnp.tile(p["gqa_q_norm"][l], GQA_HEADS),
                              jnp.tile(p["diff_q_norm"][l], 2 * DIFF_HEADS)]) * scale
    gk_row = jnp.concatenate([jnp.tile(p["gqa_k_norm"][l], GQA_KV_HEADS),
                              jnp.tile(p["diff_k_norm"][l], 2 * DIFF_HEADS)])
    idx = jnp.arange(2 * LANES) // HEAD_DIM
    gmat = jnp.where(idx[:, None] == idx[None, :], 1.0 / HEAD_DIM, 0.0).astype(BF16)
    up = jnp.zeros((LANES, 2 * GLA_W), F32)
    up = up.at[0:GLA_RANK, 0:GLA_W].set(p["gla_alpha_up"][l, 0])
    up = up.at[GLA_RANK:2 * GLA_RANK, GLA_W:].set(p["gla_alpha_up"][l, 1])
    ub = p["gla_alpha_bias"][l].reshape(1, 2 * GLA_W)
    up_hi = up.astype(BF16)
    up = jnp.stack([up_hi, (up - up_hi.astype(F32)).astype(BF16)])
    return {
        "gq": gq_row.reshape(1, 2 * N_QG), "gk": gk_row.reshape(1, LANES + BRANCH_W), "gmat": gmat,
        "up": up, "ub": ub,
        "kb_g": norm_bounds(p["gqa_q_norm"][l], p["gqa_k_norm"][l]),
        "kb_d": norm_bounds(p["diff_q_norm"][l], p["diff_k_norm"][l]),
        "norm1": p["norm1"][l].reshape(1, D_MODEL), "norm2": p["norm2"][l].reshape(1, D_MODEL),
        "gout": p["gla_out_norm"][l].reshape(1, GLA_DV),
        "gsub": p["diff_sub_norm"][l].reshape(1, DIFF_DV),
        "lam": p["diff_lambda"][l],
    }


def _rope_tables(n_tokens):
    n_rows = n_tokens // GRID_W
    row = jnp.repeat(jnp.arange(n_rows, dtype=F32), GRID_W)
    col = jnp.tile(jnp.arange(GRID_W, dtype=F32), n_rows)
    n_freq = HEAD_DIM // 4
    freqs = ROPE_THETA ** (-jnp.arange(n_freq, dtype=F32) / n_freq)
    ar, ac = row[:, None] * freqs, col[:, None] * freqs
    cos = jnp.concatenate([jnp.cos(ar), jnp.cos(ar), jnp.cos(ac), jnp.cos(ac)], axis=1)
    sin = jnp.concatenate([-jnp.sin(ar), jnp.sin(ar), -jnp.sin(ac), jnp.sin(ac)], axis=1)
    return jnp.tile(cos, (1, LANES // HEAD_DIM)), jnp.tile(sin, (1, LANES // HEAD_DIM))


def _with_ones_rows(v_t):
    lead = v_t.shape[:-2]
    s = v_t.shape[-1]
    ones = jnp.ones(lead + (1, s), v_t.dtype)
    zeros = jnp.zeros(lead + (BF16_ROWS - 1, s), v_t.dtype)
    return jnp.concatenate([v_t, ones, zeros], axis=-2).astype(BF16)


def _cache_layouts(l, cache_gqa_k, cache_gqa_v, cache_diff_k, cache_diff_v):
    b = cache_gqa_k.shape[0]
    gk = jnp.transpose(cache_gqa_k[:, l], (0, 2, 1, 3))
    kd_g = gk.astype(BF16)
    vt_g = _with_ones_rows(jnp.transpose(cache_gqa_v[:, l], (0, 2, 3, 1)))
    dk = jnp.transpose(cache_diff_k[:, l], (0, 2, 3, 1, 4)).reshape(b, 2 * DIFF_HEADS, PAST_LEN, HEAD_DIM)
    kd_d = dk.astype(BF16)
    vt_d = _with_ones_rows(jnp.transpose(cache_diff_v[:, l], (0, 2, 3, 1)))
    return kd_g, vt_g, kd_d, vt_d


def _run_layer(x2d, mod, lw, sw, l, n_batch, n_tok, rope_tabs, ctx, emit_cache, prev_caches, cfg):
    tiles_per_mod = lambda tm: (n_tok // tm) if mod.shape[0] > 1 else (n_batch * n_tok // tm)
    if ctx is None:
        cache_kv = kd_gc = kd_dc = s0 = None
    else:
        kd_gc, vt_gc, kd_dc, vt_dc, s0 = ctx
        cache_kv = (kd_gc, kd_dc, vt_gc, vt_dc)
    outs = _in_projection(x2d, mod, lw, sw, l, n_batch, n_tok, cache_kv, rope_tabs, emit_cache, prev_caches,
                          cfg["tm_in"])
    qg, qd, kdg, kdd, vtg, vtd, lq, lk, lv, lgo, la, gates = outs[:12]
    og = _gqa_attention(qg, lw["kb_g"], kdg, vtg, kd_gc, n_batch, n_tok, cfg["tq_gqa"], cfg["tk"],
                        cfg["hp_gqa"])
    lam_init = 0.8 - 0.6 * math.exp(-0.3 * l)
    od = _diff_attention(qd, lw["kb_d"], lw["lam"], lw["gsub"], kdd, vtd, kd_dc, n_batch, n_tok,
                         cfg["tq_diff"], cfg["tk"], cfg["hp_diff"], lam_init)
    o_f, o_b, s_fin = _gla(lq, lk, lv, la, s0, n_batch, n_tok, cfg["tb_gla"], cfg["bg_gla"])
    x2d = _merge_ffn(x2d, mod, og, o_f, o_b, lgo, od, gates, lw, sw, l, tiles_per_mod(cfg["tm"]), cfg["tm"])
    cache = tuple(outs[12:]) + (s_fin,) if emit_cache else None
    return x2d, cache


PROMPT_CFG = dict(tm_in=512, tq_gqa=256, tq_diff=256, tk=256, hp_gqa=2, hp_diff=4, tb_gla=256, bg_gla=2, tm=512)
SAMPLE_CFG = dict(tm_in=512, tq_gqa=1024, tq_diff=2048, tk=1536, hp_gqa=1, hp_diff=1, tb_gla=256, bg_gla=2, tm=512)


def kernel(x_prompt, x_sample, c, cache_gqa_k, cache_gqa_v, state_gla, cache_diff_k, cache_diff_v, c_ctx, w_mod, b_mod, norm1, norm2, w_in, gqa_q_norm, gqa_k_norm, gla_alpha_up, gla_alpha_bias, gla_out_norm, diff_q_norm, diff_k_norm, diff_lambda, diff_sub_norm, w_branch, w_out, w_ffn_in, w_ffn_out):
    p = {
        "norm1": norm1, "norm2": norm2, "w_in": w_in, "gqa_q_norm": gqa_q_norm, "gqa_k_norm": gqa_k_norm,
        "gla_alpha_up": gla_alpha_up, "gla_alpha_bias": gla_alpha_bias, "gla_out_norm": gla_out_norm,
        "diff_q_norm": diff_q_norm, "diff_k_norm": diff_k_norm, "diff_lambda": diff_lambda,
        "diff_sub_norm": diff_sub_norm, "w_branch": w_branch, "w_out": w_out,
        "w_ffn_in": w_ffn_in, "w_ffn_out": w_ffn_out,
    }
    n_ctx_b, n_ctx = x_prompt.shape[:2]
    n_lat_b, n_lat = x_sample.shape[:2]
    cond_rows = jnp.concatenate(
        [c_ctx[None, :], c, jnp.zeros((F32_ROWS - 1 - n_lat_b, D_MODEL), F32)], axis=0)
    mod_all = _modulation(cond_rows, w_mod, b_mod)
    weights = [_layer_weights(l, p) for l in range(DEPTH)]
    shared = _shared_weights(p)

    y = x_prompt.reshape(n_ctx_b * n_ctx, D_MODEL)
    caches = []
    for l in range(DEPTH):
        mod = mod_all[l, 0:1].reshape(1, 6, D_MODEL)
        prev = [cc[:4] for cc in caches] if l == DEPTH - 1 else None
        y, cache = _run_layer(y, mod, weights[l], shared, l, n_ctx_b, n_ctx, None, None, True, prev, PROMPT_CFG)
        caches.append(cache)
    y_prompt = y.reshape(n_ctx_b, n_ctx, D_MODEL)
    stacked = caches[-1]
    new_gqa_k = stacked[0].reshape(n_ctx_b, DEPTH, n_ctx, GQA_KV_HEADS, HEAD_DIM)
    new_diff_k = stacked[1].reshape(n_ctx_b, DEPTH, n_ctx, DIFF_HEADS, 2, HEAD_DIM)
    new_gqa_v = stacked[2].reshape(n_ctx_b, DEPTH, n_ctx, GQA_KV_HEADS, HEAD_DIM)
    new_diff_v = stacked[3].reshape(n_ctx_b, DEPTH, n_ctx, DIFF_HEADS, DIFF_DV)
    new_state_gla = jnp.stack([cc[4] for cc in caches], axis=1)

    rope_tabs = _rope_tables(n_lat)
    y = x_sample.reshape(n_lat_b * n_lat, D_MODEL)
    for l in range(DEPTH):
        mod = mod_all[l, 1:1 + n_lat_b].reshape(n_lat_b, 6, D_MODEL)
        ctx = _cache_layouts(l, cache_gqa_k, cache_gqa_v, cache_diff_k, cache_diff_v) + (state_gla[:, l],)
        y, _ = _run_layer(y, mod, weights[l], shared, l, n_lat_b, n_lat, rope_tabs, ctx, False, None, SAMPLE_CFG)
    y_sample = y.reshape(n_lat_b, n_lat, D_MODEL)
    return (y_prompt, y_sample, new_gqa_k, new_gqa_v, new_state_gla, new_diff_k, new_diff_v)
```

```python
import functools
import math

import jax
import jax.numpy as jnp
from jax import lax
from jax.experimental import pallas as pl
from jax.experimental.pallas import tpu as pltpu

D_MODEL = 1024
DEPTH = 2
PAST_LEN = 512
GRID_W = 64
HEAD_DIM = 64
GQA_HEADS = 8
GQA_KV_HEADS = 2
GQA_GROUP = GQA_HEADS // GQA_KV_HEADS
GLA_HEADS = 4
GLA_DK = 64
GLA_DV = 128
GLA_RANK = 16
GLA_TAU = 16.0
GLA_CHUNK = 64
DIFF_HEADS = 4
DIFF_DV = 2 * HEAD_DIM
N_BRANCH = 3
BRANCH_W = 512
FFN_HIDDEN = ((8 * D_MODEL + 3 * 256 - 1) // (3 * 256)) * 256
ROPE_THETA = 10000.0
EPS = 1e-6

LANES = 128
BF16_ROWS = 16
F32_ROWS = 8
VMEM_LIMIT = 56 * 1024 * 1024

F32 = jnp.float32
BF16 = jnp.bfloat16

R_GQ = 0
R_KV = R_GQ + GQA_HEADS * HEAD_DIM
R_GLA = R_KV + 2 * GQA_KV_HEADS * HEAD_DIM
R_LR = R_GLA + 2 * GLA_HEADS * GLA_DK + GLA_HEADS * GLA_DV
R_LGO = R_LR + 2 * GLA_RANK
R_DQ = R_LGO + GLA_HEADS * GLA_DV
R_DK = R_DQ + DIFF_HEADS * 2 * HEAD_DIM
R_DV = R_DK + DIFF_HEADS * 2 * HEAD_DIM
R_MG = R_DV + DIFF_HEADS * DIFF_DV
N_IN = R_MG + N_BRANCH * D_MODEL
NT_DIMS = (((1,), (1,)), ((), ()))
N_QG = GQA_HEADS * HEAD_DIM
GLA_W = GLA_HEADS * GLA_DK
GQA_VT_ROWS = HEAD_DIM + BF16_ROWS
DIFF_VT_ROWS = DIFF_DV + BF16_ROWS


def _sigmoid(x):
    return 0.5 * jnp.tanh(0.5 * x) + 0.5


def _silu(x):
    return x * _sigmoid(x)


def _bf16_terms(x):
    hi = x.astype(BF16)
    r1 = x - hi.astype(F32)
    mid = r1.astype(BF16)
    lo = (r1 - mid.astype(F32)).astype(BF16)
    return hi, mid, lo


def _full_spec(shape):
    n = len(shape)
    return pl.BlockSpec(shape, lambda *_: (0,) * n, pipeline_mode=pl.Buffered(1))


def _layer_spec(shape, l):
    n = len(shape)
    return pl.BlockSpec((None,) + tuple(shape), lambda *_: (l,) + (0,) * n, pipeline_mode=pl.Buffered(1))


def _mod_kernel(cond_ref, w_ref, b_ref, o_ref):
    s_hi, s_mid, _ = _bf16_terms(_silu(cond_ref[...]))
    lhs = jnp.concatenate([s_hi, s_mid], axis=0)
    w = w_ref[0]
    w_hi = w.astype(BF16)
    w_mid = (w - w_hi.astype(F32)).astype(BF16)
    r = jnp.dot(lhs, w_hi, preferred_element_type=F32) + jnp.dot(lhs, w_mid, preferred_element_type=F32)
    n = cond_ref.shape[0]
    o_ref[0] = r[0:n] + r[n:2 * n] + b_ref[0]


def _modulation(cond_rows, w_mod, b_mod):
    tn = 1536
    n_out = 6 * D_MODEL
    return pl.pallas_call(
        _mod_kernel,
        grid=(DEPTH, n_out // tn),
        in_specs=[
            pl.BlockSpec((F32_ROWS, D_MODEL), lambda l, j: (0, 0)),
            pl.BlockSpec((1, D_MODEL, tn), lambda l, j: (l, 0, j)),
            pl.BlockSpec((1, 1, tn), lambda l, j: (l, 0, j)),
        ],
        out_specs=pl.BlockSpec((1, F32_ROWS, tn), lambda l, j: (l, 0, j)),
        out_shape=jax.ShapeDtypeStruct((DEPTH, F32_ROWS, n_out), F32),
        compiler_params=pltpu.CompilerParams(
            dimension_semantics=("parallel", "parallel"), vmem_limit_bytes=VMEM_LIMIT),
        name="modulation",
    )(cond_rows, w_mod, b_mod.reshape(DEPTH, 1, n_out))


def _head_norm(z, gain, gmat):
    outs = []
    n = z.shape[1]
    for c0 in range(0, n, 2 * LANES):
        w = min(2 * LANES, n - c0)
        zz = z[:, c0:c0 + w]
        ms = jnp.dot((zz * zz).astype(BF16), gmat[0:w, 0:w], preferred_element_type=F32)
        outs.append(zz * lax.rsqrt(ms + EPS) * gain[:, c0:c0 + w])
    return outs


def _rope_block(zb, cos, sin, first):
    partner = jnp.where(first, pltpu.roll(zb, LANES - 16, 1), pltpu.roll(zb, 16, 1))
    return zb * cos + partner * sin


N_IN_BASE = 9


def _in_kernel(*refs, rope, emit_cache, prev_layers, tiles_per_batch, cache_tiles):
    if not cache_tiles:
        _in_tile(refs, rope, emit_cache, prev_layers)
        return
    kcg_ref, kcd_ref, vcg_ref, vcd_ref = refs[:4]
    j = pl.program_id(1)

    @pl.when(j < tiles_per_batch)
    def _():
        _in_tile(refs[4:], rope, emit_cache, prev_layers)

    @pl.when(j >= tiles_per_batch)
    def _():
        n_in = 4 + N_IN_BASE + (2 if rope else 0)
        kdg_ref, kdd_ref, vtg_ref, vtd_ref = refs[n_in + 2:n_in + 6]
        kdg_ref[...] = kcg_ref[...]
        kdd_ref[...] = kcd_ref[...]
        vtg_ref[...] = vcg_ref[...]
        vtd_ref[...] = vcd_ref[...]


def _in_tile(refs, rope, emit_cache, prev_layers):
    it = iter(refs)
    (x_ref, mod_ref, g1_ref, wt_ref, gq_ref, gk_ref, gmat_ref, up_ref, ub_ref) = (
        next(it) for _ in range(N_IN_BASE))
    if rope:
        cos_ref, sin_ref = next(it), next(it)
    prev_refs = [[next(it) for _ in range(4)] for _ in range(prev_layers or 0)]
    (qg_ref, qd_ref, kdg_ref, kdd_ref, vtg_ref, vtd_ref,
     lq_ref, lk_ref, lv_ref, lgo_ref, la_ref, gate_ref) = (next(it) for _ in range(12))
    if emit_cache:
        ck_g_ref, ck_d_ref, cv_g_ref, cv_d_ref = (next(it) for _ in range(4))

    tm = x_ref.shape[0]
    bpt = kdg_ref.shape[0]
    nt = tm // bpt
    x = x_ref[...]
    ms = jnp.mean(x * x, axis=-1, keepdims=True)
    gain = g1_ref[...] * (1.0 + mod_ref[0, 1:2, :])
    hb = (x * lax.rsqrt(ms + EPS) * gain + mod_ref[0, 0:1, :]).astype(BF16)

    def proj(r0, n):
        return lax.dot_general(hb, wt_ref[r0:r0 + n, :], NT_DIMS, preferred_element_type=F32)

    lane = lax.broadcasted_iota(jnp.int32, (tm, LANES), 1)
    first16 = (lane % 32) < 16
    gmat = gmat_ref[...]
    if rope:
        cos, sin = cos_ref[...], sin_ref[...]

    def blocks128(parts):
        out = []
        for p in parts:
            for c in range(0, p.shape[1], LANES):
                out.append(p[:, c:c + LANES])
        return out

    def store_queries(z, gain, dst):
        for j, zb in enumerate(blocks128(_head_norm(z, gain, gmat))):
            if rope:
                zb = _rope_block(zb, cos, sin, first16)
            dst[:, j * LANES:(j + 1) * LANES] = zb.astype(BF16)

    def store_cache(dst, c0, value):
        w = value.shape[1]
        if prev_layers is None:
            dst[:, c0:c0 + w] = value
        else:
            for bb in range(bpt):
                dst[bb, prev_layers, :, c0:c0 + w] = value[bb * nt:(bb + 1) * nt, :]

    def store_keys(z, gain, dst, cache_dst):
        for j, zb in enumerate(blocks128(_head_norm(z, gain, gmat))):
            if rope:
                zb = _rope_block(zb, cos, sin, first16)
            if emit_cache:
                store_cache(cache_dst, j * LANES, zb)
            sw = pltpu.roll(zb, HEAD_DIM, 1)
            for bb in range(bpt):
                dst[bb, 2 * j] = zb[bb * nt:(bb + 1) * nt, 0:HEAD_DIM].astype(BF16)
                dst[bb, 2 * j + 1] = sw[bb * nt:(bb + 1) * nt, 0:HEAD_DIM].astype(BF16)

    def store_gate(b):
        gate_ref[:, b * D_MODEL:(b + 1) * D_MODEL] = _sigmoid(proj(R_MG + b * D_MODEL, D_MODEL)).astype(BF16)

    lr = proj(R_LR, LANES)
    z_gq = proj(R_GQ, N_QG)
    lr_hi, lr_mid, _ = _bf16_terms(lr)
    z_la = (jnp.dot(lr_hi, up_ref[0], preferred_element_type=F32)
            + jnp.dot(lr_mid, up_ref[0], preferred_element_type=F32)
            + jnp.dot(lr_hi, up_ref[1], preferred_element_type=F32)) + ub_ref[...]
    z_dq = proj(R_DQ, N_QG)
    log_sig = jnp.minimum(z_la, 0.0) - jnp.log(1.0 + jnp.exp(-jnp.abs(z_la)))
    la_ref[...] = log_sig * (1.0 / GLA_TAU)
    store_gate(0)
    store_queries(z_gq, gq_ref[:, 0:N_QG], qg_ref)
    z_kv = proj(R_KV, 2 * LANES)
    store_queries(z_dq, gq_ref[:, N_QG:2 * N_QG], qd_ref)
    z_dk = proj(R_DK, BRANCH_W)
    store_gate(1)
    store_keys(z_kv[:, 0:LANES], gk_ref[:, 0:LANES], kdg_ref, ck_g_ref if emit_cache else None)
    z_dv = proj(R_DV, BRANCH_W)
    store_keys(z_dk, gk_ref[:, LANES:LANES + BRANCH_W], kdd_ref, ck_d_ref if emit_cache else None)
    store_gate(2)
    zl = proj(R_GLA, R_LR - R_GLA)
    z_lgo = proj(R_LGO, BRANCH_W)

    ones_rows = jnp.where(lax.broadcasted_iota(jnp.int32, (BF16_ROWS, nt), 0) == 0, 1.0, 0.0).astype(BF16)
    z_gv = z_kv[:, LANES:2 * LANES]
    if emit_cache:
        store_cache(cv_g_ref, 0, z_gv)
        store_cache(cv_d_ref, 0, z_dv)
        for layer, layer_refs in enumerate(prev_refs):
            for dst, src in zip((ck_g_ref, ck_d_ref, cv_g_ref, cv_d_ref), layer_refs):
                for bb in range(bpt):
                    dst[bb, layer] = src[bb * nt:(bb + 1) * nt, :]
    for bb in range(bpt):
        rows = slice(bb * nt, (bb + 1) * nt)
        vt = z_gv[rows, :].T
        for hh in range(GQA_KV_HEADS):
            vtg_ref[bb, hh, 0:HEAD_DIM, :] = vt[hh * HEAD_DIM:(hh + 1) * HEAD_DIM, :].astype(BF16)
            vtg_ref[bb, hh, HEAD_DIM:GQA_VT_ROWS, :] = ones_rows
        for hh in range(DIFF_HEADS):
            vt = z_dv[rows, hh * LANES:(hh + 1) * LANES].T
            vtd_ref[bb, hh, 0:DIFF_DV, :] = vt.astype(BF16)
            vtd_ref[bb, hh, DIFF_DV:DIFF_VT_ROWS, :] = ones_rows

    o = 0
    lq_ref[...] = (zl[:, o:o + GLA_W] * (GLA_DK ** -0.5)).astype(BF16)
    o += GLA_W
    lk_ref[...] = zl[:, o:o + GLA_W].astype(BF16)
    o += GLA_W
    lv_ref[...] = zl[:, o:o + BRANCH_W].astype(BF16)
    o += BRANCH_W
    lgo_ref[...] = z_lgo.astype(BF16)


def _in_projection(x2d, mod, lw, sw, l, n_batch, n_tok, cache_kv, rope_tabs, emit_cache, prev_caches, tm):
    n_cache = 0 if cache_kv is None else cache_kv[0].shape[2]
    assert n_cache % tm == 0
    n_keys = n_tok + n_cache
    t_total = n_batch * n_tok
    bpt = max(1, tm // n_tok)
    assert bpt == 1 or (mod.shape[0] == 1 and n_cache == 0 and tm == bpt * n_tok)
    nt = tm // bpt
    tpb = n_tok // nt
    cache_tiles = n_cache // tm
    rope = rope_tabs is not None
    own = lambda j: jnp.minimum(j, tpb - 1)
    tile = lambda c: pl.BlockSpec((tm, c), lambda b, j: (b * tpb + own(j), 0))
    in_specs, args = [], []
    if cache_tiles:
        past = lambda j: jnp.maximum(j - tpb, 0)
        in_specs += [
            pl.BlockSpec((1, GQA_KV_HEADS, tm, HEAD_DIM), lambda b, j: (b, 0, past(j), 0)),
            pl.BlockSpec((1, 2 * DIFF_HEADS, tm, HEAD_DIM), lambda b, j: (b, 0, past(j), 0)),
            pl.BlockSpec((1, GQA_KV_HEADS, GQA_VT_ROWS, tm), lambda b, j: (b, 0, 0, past(j))),
            pl.BlockSpec((1, DIFF_HEADS, DIFF_VT_ROWS, tm), lambda b, j: (b, 0, 0, past(j))),
        ]
        args += list(cache_kv)
    in_specs += [
        tile(D_MODEL),
        pl.BlockSpec((1, 6, D_MODEL), lambda b, j: (b if mod.shape[0] > 1 else 0, 0, 0)),
        _full_spec((1, D_MODEL)),
        _layer_spec((N_IN, D_MODEL), l),
        _full_spec((1, 2 * N_QG)),
        _full_spec((1, LANES + BRANCH_W)),
        _full_spec((2 * LANES, 2 * LANES)),
        _full_spec((2, LANES, 2 * GLA_W)),
        _full_spec((1, 2 * GLA_W)),
    ]
    args += [x2d, mod, lw["norm1"], sw["w_in_t"], lw["gq"], lw["gk"], lw["gmat"], lw["up"], lw["ub"]]
    if rope:
        in_specs += [pl.BlockSpec((tm, LANES), lambda b, j: (own(j), 0))] * 2
        args += list(rope_tabs)
    sd = jax.ShapeDtypeStruct
    out_shape = [
        sd((t_total, BRANCH_W), BF16), sd((t_total, BRANCH_W), BF16),
        sd((n_batch, GQA_KV_HEADS, n_keys, HEAD_DIM), BF16), sd((n_batch, 2 * DIFF_HEADS, n_keys, HEAD_DIM), BF16),
        sd((n_batch, GQA_KV_HEADS, GQA_VT_ROWS, n_keys), BF16), sd((n_batch, DIFF_HEADS, DIFF_VT_ROWS, n_keys), BF16),
        sd((t_total, GLA_W), BF16), sd((t_total, GLA_W), BF16),
        sd((t_total, BRANCH_W), BF16), sd((t_total, BRANCH_W), BF16),
        sd((t_total, 2 * GLA_W), F32), sd((t_total, N_BRANCH * D_MODEL), BF16),
    ]
    out_specs = [
        tile(BRANCH_W), tile(BRANCH_W),
        pl.BlockSpec((bpt, GQA_KV_HEADS, nt, HEAD_DIM), lambda b, j: (b, 0, j, 0)),
        pl.BlockSpec((bpt, 2 * DIFF_HEADS, nt, HEAD_DIM), lambda b, j: (b, 0, j, 0)),
        pl.BlockSpec((bpt, GQA_KV_HEADS, GQA_VT_ROWS, nt), lambda b, j: (b, 0, 0, j)),
        pl.BlockSpec((bpt, DIFF_HEADS, DIFF_VT_ROWS, nt), lambda b, j: (b, 0, 0, j)),
        tile(GLA_W), tile(GLA_W), tile(BRANCH_W), tile(BRANCH_W), tile(2 * GLA_W), tile(N_BRANCH * D_MODEL),
    ]
    cache_widths = (LANES, BRANCH_W, LANES, BRANCH_W)
    prev_layers = None
    if emit_cache and prev_caches is None:
        out_shape += [sd((t_total, w), F32) for w in cache_widths]
        out_specs += [tile(w) for w in cache_widths]
    elif emit_cache:
        prev_layers = len(prev_caches)
        n_layers = prev_layers + 1
        for layer_arrays in prev_caches:
            in_specs += [tile(w) for w in cache_widths]
            args += list(layer_arrays)
        out_shape += [sd((n_batch, n_layers, n_tok, w), F32) for w in cache_widths]
        out_specs += [pl.BlockSpec((bpt, n_layers, nt, w), lambda b, j: (b, 0, j, 0)) for w in cache_widths]
    return pl.pallas_call(
        functools.partial(_in_kernel, rope=rope, emit_cache=emit_cache, prev_layers=prev_layers,
                          tiles_per_batch=tpb, cache_tiles=cache_tiles),
        grid=(n_batch // bpt, tpb + cache_tiles),
        in_specs=in_specs,
        out_specs=out_specs,
        out_shape=out_shape,
        compiler_params=pltpu.CompilerParams(
            dimension_semantics=("parallel", "arbitrary"), vmem_limit_bytes=VMEM_LIMIT),
        name="in_projection",
    )(*args)


MAX_COL = 4 * LANES
SCORE_BOUND = 45.0
NORM_MARGIN = 1.05
SCORES_AHEAD_BYTES = 4 * 1024 * 1024


def _head_queries_t(q_blk, low):
    t = q_blk.astype(F32).T
    return (t[0:HEAD_DIM, :] if low else t[HEAD_DIM:2 * HEAD_DIM, :]).astype(BF16)


def _key_norm2(kd):
    kf = kd.astype(F32)
    return jnp.max(jnp.sum(kf * kf, axis=1, keepdims=True))


def _needs_no_stabiliser(q_norm2, k_norm2):
    return (q_norm2 * k_norm2 <= SCORE_BOUND * SCORE_BOUND).astype(jnp.int32)


def _score_units(heads_cols):
    return [(be, kh, vh, c) for be, kh, vh, cols in heads_cols for c in cols]


def _tile_scores(k_ref, qm_ref, unit, col):
    be, kh, _, c = unit
    return jnp.dot(k_ref[be, kh], qm_ref[:, c * col:(c + 1) * col], preferred_element_type=F32)


def _plain_tiles(k_ref, vt_ref, qm_ref, acc_ref, heads_cols):
    col = acc_ref.shape[-1]
    units = _score_units(heads_cols)
    tile_bytes = k_ref.shape[2] * col * 4
    ahead = max(1, min(len(units), SCORES_AHEAD_BYTES // tile_bytes))
    scores = [_tile_scores(k_ref, qm_ref, unit, col) for unit in units[:ahead]]
    for i, (be, _, vh, c) in enumerate(units):
        if i + ahead < len(units):
            scores.append(_tile_scores(k_ref, qm_ref, units[i + ahead], col))
        p = jnp.exp2(scores[i]).astype(BF16)
        acc_ref[c] += jnp.dot(vt_ref[be, vh], p, preferred_element_type=F32)


def _online_tiles(k_ref, vt_ref, qm_ref, m_ref, acc_ref, heads_cols):
    col = acc_ref.shape[-1]
    for unit in _score_units(heads_cols):
        be, _, vh, c = unit
        s = _tile_scores(k_ref, qm_ref, unit, col)
        m_prev = m_ref[c]
        m_new = jnp.maximum(m_prev, jnp.max(s, axis=0, keepdims=True))
        alpha = jnp.exp2(m_prev - m_new)
        p = jnp.exp2(s - m_new).astype(BF16)
        acc_ref[c] = acc_ref[c] * alpha + jnp.dot(vt_ref[be, vh], p, preferred_element_type=F32)
        m_ref[c] = m_new


def _flash_init(kb_ref, kc_ref, n_key_heads, m_ref, acc_ref, plain_ref):
    k_norm2 = kb_ref[1]
    if kc_ref is not None:
        for be in range(kc_ref.shape[0]):
            for kh in range(n_key_heads):
                k_norm2 = jnp.maximum(k_norm2, _key_norm2(kc_ref[be, kh]))
    plain_ref[0] = _needs_no_stabiliser(kb_ref[0], k_norm2)
    m_ref[...] = jnp.full(m_ref.shape, -jnp.inf, F32)
    acc_ref[...] = jnp.zeros(acc_ref.shape, F32)


def _flash_step(k_ref, vt_ref, qm_ref, m_ref, acc_ref, plain_ref, heads_cols):
    @pl.when(plain_ref[0] == 1)
    def _():
        _plain_tiles(k_ref, vt_ref, qm_ref, acc_ref, heads_cols)

    @pl.when(plain_ref[0] != 1)
    def _():
        _online_tiles(k_ref, vt_ref, qm_ref, m_ref, acc_ref, heads_cols)


def _gqa_kernel(*refs, has_cache, tq, hp):
    if has_cache:
        kb_ref, q_ref, k_ref, vt_ref, kc_ref, o_ref, qm_ref, m_ref, acc_ref, plain_ref = refs
    else:
        kb_ref, q_ref, k_ref, vt_ref, o_ref, qm_ref, m_ref, acc_ref, plain_ref = refs
        kc_ref = None
    kt = pl.program_id(3)
    col = acc_ref.shape[-1]
    nb = q_ref.shape[0]
    per_head = tq // col
    per_kv = GQA_GROUP * per_head
    first = lambda be, hh: (be * hp + hh) * per_kv
    heads_cols = [(be, hh, hh, range(first(be, hh), first(be, hh) + per_kv))
                  for be in range(nb) for hh in range(hp)]

    @pl.when(kt == 0)
    def _():
        for be in range(nb):
            for hh in range(hp):
                for g in range(GQA_GROUP):
                    c0 = hh * 2 * LANES + (g // 2) * LANES
                    row0 = (first(be, hh) + g * per_head) * col
                    qm_ref[:, row0:row0 + tq] = _head_queries_t(q_ref[be, :, c0:c0 + LANES], g % 2 == 0)
        _flash_init(kb_ref, kc_ref, hp, m_ref, acc_ref, plain_ref)

    _flash_step(k_ref, vt_ref, qm_ref, m_ref, acc_ref, plain_ref, heads_cols)

    @pl.when(kt == pl.num_programs(3) - 1)
    def _():
        for be in range(nb):
            for hh in range(hp):
                for part in range(per_head):
                    heads = []
                    for g in range(GQA_GROUP):
                        acc = acc_ref[first(be, hh) + g * per_head + part]
                        heads.append(acc[0:HEAD_DIM, :] / acc[HEAD_DIM:HEAD_DIM + 1, :])
                    o_ref[be, part * col:(part + 1) * col, hh * 2 * LANES:(hh + 1) * 2 * LANES] = (
                        jnp.concatenate(heads, axis=0).T.astype(BF16))


def _kv_specs(nb, key_heads_blk, val_heads_blk, vt_rows, tk, kd_c):
    specs = [pl.BlockSpec((nb, key_heads_blk, tk, HEAD_DIM), lambda b, h, qi, kt: (b, h, kt, 0)),
             pl.BlockSpec((nb, val_heads_blk, vt_rows, tk), lambda b, h, qi, kt: (b, h, 0, kt))]
    if kd_c is not None:
        specs.append(pl.BlockSpec((nb, key_heads_blk, kd_c.shape[2], HEAD_DIM), lambda b, h, qi, kt: (b, h, 0, 0)))
    return specs


def _gqa_attention(q, k_bound, kd, vt, kd_c, n_batch, n_tok, tq, tk, hp, nb):
    has_cache = kd_c is not None
    nk = kd.shape[2] // tk
    rows = nb * hp * GQA_GROUP * tq
    col = min(MAX_COL, tq)
    q_spec = pl.BlockSpec((nb, tq, hp * 2 * LANES), lambda b, h, qi, kt: (b, qi, h))
    in_specs = [pl.BlockSpec(memory_space=pltpu.SMEM), q_spec]
    in_specs += _kv_specs(nb, hp, hp, GQA_VT_ROWS, tk, kd_c)
    args = [k_bound, q.reshape(n_batch, n_tok, BRANCH_W), kd, vt]
    if has_cache:
        args.append(kd_c)
    out = pl.pallas_call(
        functools.partial(_gqa_kernel, has_cache=has_cache, tq=tq, hp=hp),
        grid=(n_batch // nb, GQA_KV_HEADS // hp, n_tok // tq, nk),
        in_specs=in_specs,
        out_specs=q_spec,
        out_shape=jax.ShapeDtypeStruct((n_batch, n_tok, BRANCH_W), BF16),
        scratch_shapes=[
            pltpu.VMEM((HEAD_DIM, rows), BF16),
            pltpu.VMEM((rows // col, 1, col), F32),
            pltpu.VMEM((rows // col, GQA_VT_ROWS, col), F32),
            pltpu.SMEM((1,), jnp.int32),
        ],
        compiler_params=pltpu.CompilerParams(
            dimension_semantics=("parallel", "parallel", "parallel", "arbitrary"),
            vmem_limit_bytes=VMEM_LIMIT),
        name="gqa_attention",
    )(*args)
    return out.reshape(n_batch * n_tok, BRANCH_W)


def _diff_kernel(*refs, has_cache, lam_init, hp):
    if has_cache:
        (kb_ref, q_ref, lam_ref, gsub_ref, k_ref, vt_ref, kc_ref, o_ref,
         qm_ref, m_ref, acc_ref, plain_ref) = refs
    else:
        kb_ref, q_ref, lam_ref, gsub_ref, k_ref, vt_ref, o_ref, qm_ref, m_ref, acc_ref, plain_ref = refs
        kc_ref = None
    kt = pl.program_id(3)
    tq = q_ref.shape[1]
    col = acc_ref.shape[-1]
    nb = q_ref.shape[0]
    per_map = tq // col
    first = lambda be, hh, mm: ((be * hp + hh) * 2 + mm) * per_map
    heads_cols = [(be, 2 * hh + mm, hh, range(first(be, hh, mm), first(be, hh, mm) + per_map))
                  for be in range(nb) for hh in range(hp) for mm in range(2)]

    @pl.when(kt == 0)
    def _():
        for be in range(nb):
            for hh in range(hp):
                blk = q_ref[be, :, hh * LANES:(hh + 1) * LANES]
                for mm in range(2):
                    row0 = first(be, hh, mm) * col
                    qm_ref[:, row0:row0 + tq] = _head_queries_t(blk, mm == 0)
        _flash_init(kb_ref, kc_ref, 2 * hp, m_ref, acc_ref, plain_ref)

    _flash_step(k_ref, vt_ref, qm_ref, m_ref, acc_ref, plain_ref, heads_cols)

    @pl.when(kt == pl.num_programs(3) - 1)
    def _():
        lp = lam_ref[...]
        lam = (jnp.exp(jnp.sum(lp[0:1] * lp[1:2], axis=-1, keepdims=True))
               - jnp.exp(jnp.sum(lp[2:3] * lp[3:4], axis=-1, keepdims=True)) + lam_init)
        for be in range(nb):
            for hh in range(hp):
                for part in range(per_map):
                    a0, a1 = acc_ref[first(be, hh, 0) + part], acc_ref[first(be, hh, 1) + part]
                    o0 = a0[0:DIFF_DV, :] / a0[DIFF_DV:DIFF_DV + 1, :]
                    o1 = a1[0:DIFF_DV, :] / a1[DIFF_DV:DIFF_DV + 1, :]
                    d = (o0 - lam * o1).T
                    ms = jnp.mean(d * d, axis=-1, keepdims=True)
                    o_ref[be, part * col:(part + 1) * col, hh * LANES:(hh + 1) * LANES] = (
                        d * lax.rsqrt(ms + EPS) * gsub_ref[...] * (1.0 - lam_init)).astype(BF16)


def _diff_attention(q, k_bound, lam_p, gsub, kd, vt, kd_c, n_batch, n_tok, tq, tk, hp, nb, lam_init):
    has_cache = kd_c is not None
    nk = kd.shape[2] // tk
    col = min(MAX_COL, tq)
    rows = nb * hp * 2 * tq
    q_spec = pl.BlockSpec((nb, tq, hp * LANES), lambda b, h, qi, kt: (b, qi, h))
    in_specs = [
        pl.BlockSpec(memory_space=pltpu.SMEM),
        q_spec,
        pl.BlockSpec((4, HEAD_DIM), lambda b, h, qi, kt: (0, 0)),
        pl.BlockSpec((1, DIFF_DV), lambda b, h, qi, kt: (0, 0)),
    ]
    in_specs += _kv_specs(nb, 2 * hp, hp, DIFF_VT_ROWS, tk, kd_c)
    args = [k_bound, q.reshape(n_batch, n_tok, BRANCH_W), lam_p, gsub, kd, vt]
    if has_cache:
        args.append(kd_c)
    out = pl.pallas_call(
        functools.partial(_diff_kernel, has_cache=has_cache, lam_init=lam_init, hp=hp),
        grid=(n_batch // nb, DIFF_HEADS // hp, n_tok // tq, nk),
        in_specs=in_specs,
        out_specs=q_spec,
        out_shape=jax.ShapeDtypeStruct((n_batch, n_tok, BRANCH_W), BF16),
        scratch_shapes=[
            pltpu.VMEM((HEAD_DIM, rows), BF16),
            pltpu.VMEM((rows // col, 1, col), F32),
            pltpu.VMEM((rows // col, DIFF_VT_ROWS, col), F32),
            pltpu.SMEM((1,), jnp.int32),
        ],
        compiler_params=pltpu.CompilerParams(
            dimension_semantics=("parallel", "parallel", "parallel", "arbitrary"),
            vmem_limit_bytes=VMEM_LIMIT),
        name="diff_attention",
    )(*args)
    return out.reshape(n_batch * n_tok, BRANCH_W)


def _gla_kernel(*refs, has_s0, n_chunk, bg):
    if has_s0:
        qf, kf, vf, laf, qb, kb, vb, lab, s0_ref, of_ref, ob_ref, sfin_ref, st_ref = refs
    else:
        qf, kf, vf, laf, qb, kb, vb, lab, of_ref, ob_ref, sfin_ref, st_ref = refs
    i = pl.program_id(1)
    ck = GLA_CHUNK

    @pl.when(i == 0)
    def _():
        for bb in range(bg):
            for d in range(2):
                for hd in range(GLA_HEADS):
                    if has_s0:
                        s = s0_ref[bb, d, hd]
                        z = jnp.zeros_like(s)
                        padded = jnp.concatenate([s, z] if hd % 2 == 0 else [z, s], axis=0)
                        st_ref[bb, d, hd] = padded.T
                    else:
                        st_ref[bb, d, hd] = jnp.zeros((GLA_DV, LANES), F32)

    tb = qf.shape[1]
    r = lax.broadcasted_iota(jnp.int32, (tb, tb), 0)
    c = lax.broadcasted_iota(jnp.int32, (tb, tb), 1)
    lane = lax.broadcasted_iota(jnp.int32, (tb, LANES), 1)
    chunk_id = lax.broadcasted_iota(jnp.int32, (tb, GLA_W), 0) // ck
    zero_row = jnp.zeros((1, GLA_W), F32)
    streams = []
    for bb in range(bg):
        streams.append((bb, 0, qf.at[bb], kf.at[bb], vf.at[bb], laf.at[bb], of_ref.at[bb], c <= r))
        streams.append((bb, 1, qb.at[bb], kb.at[bb], vb.at[bb], lab.at[bb], ob_ref.at[bb], c >= r))
    nt = (((1,), (1,)), ((), ()))
    g_all = [sum(jnp.dot(tri.astype(F32).astype(BF16), part, preferred_element_type=F32)
                 for part in _bf16_terms(la_r[...]))
             for (_, _, _, _, _, la_r, _, tri) in streams]
    prepared = []
    for g, (bb, d, q_r, k_r, v_r, la_r, o_r, tri) in zip(g_all, streams):
        if d == 0:
            bounds = [zero_row] + [g[ck * j - 1:ck * j, :] for j in range(1, n_chunk)]
            g_end = g[tb - 1:tb, :]
        else:
            bounds = [g[ck * (j + 1):ck * (j + 1) + 1, :] for j in range(n_chunk - 1)] + [zero_row]
            g_end = g[0:1, :]
        b_rows = jnp.concatenate([jnp.broadcast_to(b, (ck, GLA_W)) for b in bounds], axis=0)
        q = q_r[...].astype(F32)
        k = k_r[...].astype(F32)
        q_dec = q * jnp.exp(g - b_rows)
        q_glob = q * jnp.exp(g)
        k_end = k * jnp.exp(g_end - g)
        k_rel = []
        for j in range(n_chunk):
            reach = (chunk_id <= j) if d == 0 else (chunk_id >= j)
            k_rel.append((k * jnp.exp(jnp.where(reach, bounds[j] - g, 0.0))).astype(BF16))
        prepared.append((q_dec, q_glob, k_end, k_rel, jnp.exp(g_end), v_r[...]))

    partial = []
    for (bb, d, *_), (q_dec, q_glob, k_end, k_rel, decay, v) in zip(streams, prepared):
        per_head = []
        for hd in range(GLA_HEADS):
            pair = slice((hd // 2) * LANES, (hd // 2 + 1) * LANES)
            hv = slice(hd * GLA_DV, (hd + 1) * GLA_DV)
            keep = (lane < GLA_DK) if hd % 2 == 0 else (lane >= GLA_DK)
            qd_m = jnp.where(keep, q_dec[:, pair], 0.0).astype(BF16)
            qg_m = jnp.where(keep, q_glob[:, pair], 0.0).astype(BF16)
            ke_m = jnp.where(keep, k_end[:, pair], 0.0).astype(BF16)
            a_rows = [lax.dot_general(qd_m[ck * j:ck * (j + 1), :], k_rel[j][:, pair], nt,
                                      preferred_element_type=F32) for j in range(n_chunk)]
            s_t = st_ref[bb, d, hd]
            o_state = lax.dot_general(qg_m, s_t.astype(BF16), nt, preferred_element_type=F32)
            ds_t = lax.dot_general(v[:, hv], ke_m, (((0,), (0,)), ((), ())), preferred_element_type=F32)
            st_ref[bb, d, hd] = s_t * decay[:, pair] + ds_t
            per_head.append((a_rows, o_state))
        partial.append(per_head)

    for (bb, d, _, _, _, _, o_r, tri), prep, per_head in zip(streams, prepared, partial):
        v = prep[5]
        for hd in range(GLA_HEADS):
            hv = slice(hd * GLA_DV, (hd + 1) * GLA_DV)
            a_rows, o_state = per_head[hd]
            a = jnp.where(tri, jnp.concatenate(a_rows, axis=0), 0.0).astype(BF16)
            o_r[:, hv] = (jnp.dot(a, v[:, hv], preferred_element_type=F32) + o_state).astype(BF16)

    @pl.when(i == pl.num_programs(1) - 1)
    def _():
        for bb in range(bg):
            for d in range(2):
                for hd in range(GLA_HEADS):
                    t = st_ref[bb, d, hd].T
                    sfin_ref[bb, d, hd] = t[(hd % 2) * GLA_DK:(hd % 2 + 1) * GLA_DK, :]


def _gla(lq, lk, lv, la, s0, n_batch, n_tok, tb, bg):
    nb = n_tok // tb
    has_s0 = s0 is not None
    as3d = lambda x: x.reshape(n_batch, n_tok, x.shape[-1])
    fwd = lambda b, i: (b, i, 0)
    bwd = lambda b, i: (b, nb - 1 - i, 0)
    bwd_la = lambda b, i: (b, nb - 1 - i, 1)
    in_specs = [
        pl.BlockSpec((bg, tb, GLA_W), fwd), pl.BlockSpec((bg, tb, GLA_W), fwd),
        pl.BlockSpec((bg, tb, BRANCH_W), fwd), pl.BlockSpec((bg, tb, GLA_W), fwd),
        pl.BlockSpec((bg, tb, GLA_W), bwd), pl.BlockSpec((bg, tb, GLA_W), bwd),
        pl.BlockSpec((bg, tb, BRANCH_W), bwd), pl.BlockSpec((bg, tb, GLA_W), bwd_la),
    ]
    args = [as3d(lq), as3d(lk), as3d(lv), as3d(la)] * 2
    state_spec = pl.BlockSpec((bg, 2, GLA_HEADS, GLA_DK, GLA_DV), lambda b, i: (b, 0, 0, 0, 0))
    if has_s0:
        in_specs.append(state_spec)
        args.append(s0)
    o_f, o_b, s_fin = pl.pallas_call(
        functools.partial(_gla_kernel, has_s0=has_s0, n_chunk=tb // GLA_CHUNK, bg=bg),
        grid=(n_batch // bg, nb),
        in_specs=in_specs,
        out_specs=[pl.BlockSpec((bg, tb, BRANCH_W), fwd), pl.BlockSpec((bg, tb, BRANCH_W), bwd), state_spec],
        out_shape=[
            jax.ShapeDtypeStruct((n_batch, n_tok, BRANCH_W), BF16),
            jax.ShapeDtypeStruct((n_batch, n_tok, BRANCH_W), BF16),
            jax.ShapeDtypeStruct((n_batch, 2, GLA_HEADS, GLA_DK, GLA_DV), F32),
        ],
        scratch_shapes=[pltpu.VMEM((bg, 2, GLA_HEADS, GLA_DV, LANES), F32)],
        compiler_params=pltpu.CompilerParams(
            dimension_semantics=("parallel", "arbitrary"), vmem_limit_bytes=VMEM_LIMIT),
        name="gla",
    )(*args)
    flat = lambda x: x.reshape(n_batch * n_tok, BRANCH_W)
    return flat(o_f), flat(o_b), s_fin


FFN_CHUNK = 256


def _merge_ffn_kernel(x_ref, mod_ref, og_ref, of_ref, ob_ref, lgo_ref, od_ref, gate_ref,
                      gout_ref, wb_ref, wo_ref, g2_ref, wi_ref, wd_ref, y_ref):
    o_gla = of_ref[...].astype(F32) + ob_ref[...].astype(F32)
    gla_parts = []
    for hd in range(GLA_HEADS):
        blk = o_gla[:, hd * GLA_DV:(hd + 1) * GLA_DV]
        ms = jnp.mean(blk * blk, axis=-1, keepdims=True)
        gla_parts.append(blk * lax.rsqrt(ms + EPS) * gout_ref[...])
    gla = jnp.concatenate(gla_parts, axis=1) * _silu(lgo_ref[...].astype(F32))
    branches = (og_ref[...], gla.astype(BF16), od_ref[...])
    mixed = None
    for b, ob in enumerate(branches):
        y = jnp.dot(ob, wb_ref[b], preferred_element_type=F32)
        y = y * gate_ref[:, b * D_MODEL:(b + 1) * D_MODEL].astype(F32)
        mixed = y if mixed is None else mixed + y
    out = jnp.dot(mixed.astype(BF16), wo_ref[...], preferred_element_type=F32)
    x = x_ref[...] + mod_ref[0, 2:3, :] * out

    ms = jnp.mean(x * x, axis=-1, keepdims=True)
    gain = g2_ref[...] * (1.0 + mod_ref[0, 4:5, :])
    hb = (x * lax.rsqrt(ms + EPS) * gain + mod_ref[0, 3:4, :]).astype(BF16)

    def up_proj(c0):
        a = jnp.dot(hb, wi_ref[:, c0:c0 + FFN_CHUNK], preferred_element_type=F32)
        u = jnp.dot(hb, wi_ref[:, FFN_HIDDEN + c0:FFN_HIDDEN + c0 + FFN_CHUNK], preferred_element_type=F32)
        return a, u

    acc = None
    chunks = list(range(0, FFN_HIDDEN, FFN_CHUNK))
    nxt = up_proj(chunks[0])
    for i, c0 in enumerate(chunks):
        a, u = nxt
        if i + 1 < len(chunks):
            nxt = up_proj(chunks[i + 1])
        act = (_silu(a) * u).astype(BF16)
        part = jnp.dot(act, wd_ref[c0:c0 + FFN_CHUNK, :], preferred_element_type=F32)
        acc = part if acc is None else acc + part
    y_ref[...] = x + mod_ref[0, 5:6, :] * acc


def _merge_ffn(x2d, mod, og, o_f, o_b, lgo, od, gates, lw, sw, l, tiles_per_mod, tm):
    t_total = x2d.shape[0]
    tile = lambda c: pl.BlockSpec((tm, c), lambda i: (i, 0))
    return pl.pallas_call(
        _merge_ffn_kernel,
        grid=(t_total // tm,),
        in_specs=[
            tile(D_MODEL),
            pl.BlockSpec((1, 6, D_MODEL), lambda i: (i // tiles_per_mod, 0, 0)),
            tile(BRANCH_W), tile(BRANCH_W), tile(BRANCH_W), tile(BRANCH_W), tile(BRANCH_W),
            tile(N_BRANCH * D_MODEL),
            _full_spec((1, GLA_DV)),
            _layer_spec((N_BRANCH, BRANCH_W, D_MODEL), l),
            _layer_spec((D_MODEL, D_MODEL), l),
            _full_spec((1, D_MODEL)),
            _layer_spec((D_MODEL, 2 * FFN_HIDDEN), l),
            _layer_spec((FFN_HIDDEN, D_MODEL), l),
        ],
        out_specs=tile(D_MODEL),
        out_shape=jax.ShapeDtypeStruct((t_total, D_MODEL), F32),
        compiler_params=pltpu.CompilerParams(
            dimension_semantics=("parallel",), vmem_limit_bytes=VMEM_LIMIT),
        name="merge_ffn",
    )(x2d, mod, og, o_f, o_b, lgo, od, gates, lw["gout"], sw["w_branch"], sw["w_out"],
      lw["norm2"], sw["w_ffn_in"], sw["w_ffn_out"])


def _shared_weights(p):
    return {
        "w_in_t": jnp.swapaxes(p["w_in"], 1, 2).astype(BF16),
        "w_branch": p["w_branch"].astype(BF16), "w_out": p["w_out"].astype(BF16),
        "w_ffn_in": p["w_ffn_in"].astype(BF16), "w_ffn_out": p["w_ffn_out"].astype(BF16),
    }


def _layer_weights(l, p):
    scale = HEAD_DIM ** -0.5 * math.log2(math.e)
    head_bound = lambda g, s: HEAD_DIM * (NORM_MARGIN * s) ** 2 * jnp.max(g * g)
    norm_bounds = lambda gq_, gk_: jnp.stack([head_bound(gq_, scale), head_bound(gk_, 1.0)])
    gq_row = jnp.concatenate([jnp.tile(p["gqa_q_norm"][l], GQA_HEADS),
                              jnp.tile(p["diff_q_norm"][l], 2 * DIFF_HEADS)]) * scale
    gk_row = jnp.concatenate([jnp.tile(p["gqa_k_norm"][l], GQA_KV_HEADS),
                              jnp.tile(p["diff_k_norm"][l], 2 * DIFF_HEADS)])
    idx = jnp.arange(2 * LANES) // HEAD_DIM
    gmat = jnp.where(idx[:, None] == idx[None, :], 1.0 / HEAD_DIM, 0.0).astype(BF16)
    up = jnp.zeros((LANES, 2 * GLA_W), F32)
    up = up.at[0:GLA_RANK, 0:GLA_W].set(p["gla_alpha_up"][l, 0])
    up = up.at[GLA_RANK:2 * GLA_RANK, GLA_W:].set(p["gla_alpha_up"][l, 1])
    ub = p["gla_alpha_bias"][l].reshape(1, 2 * GLA_W)
    up_hi = up.astype(BF16)
    up = jnp.stack([up_hi, (up - up_hi.astype(F32)).astype(BF16)])
    return {
        "gq": gq_row.reshape(1, 2 * N_QG), "gk": gk_row.reshape(1, LANES + BRANCH_W), "gmat": gmat,
        "up": up, "ub": ub,
        "kb_g": norm_bounds(p["gqa_q_norm"][l], p["gqa_k_norm"][l]),
        "kb_d": norm_bounds(p["diff_q_norm"][l], p["diff_k_norm"][l]),
        "norm1": p["norm1"][l].reshape(1, D_MODEL), "norm2": p["norm2"][l].reshape(1, D_MODEL),
        "gout": p["gla_out_norm"][l].reshape(1, GLA_DV),
        "gsub": p["diff_sub_norm"][l].reshape(1, DIFF_DV),
        "lam": p["diff_lambda"][l],
    }


def _rope_tables(n_tokens):
    n_rows = n_tokens // GRID_W
    row = jnp.repeat(jnp.arange(n_rows, dtype=F32), GRID_W)
    col = jnp.tile(jnp.arange(GRID_W, dtype=F32), n_rows)
    n_freq = HEAD_DIM // 4
    freqs = ROPE_THETA ** (-jnp.arange(n_freq, dtype=F32) / n_freq)
    ar, ac = row[:, None] * freqs, col[:, None] * freqs
    cos = jnp.concatenate([jnp.cos(ar), jnp.cos(ar), jnp.cos(ac), jnp.cos(ac)], axis=1)
    sin = jnp.concatenate([-jnp.sin(ar), jnp.sin(ar), -jnp.sin(ac), jnp.sin(ac)], axis=1)
    return jnp.tile(cos, (1, LANES // HEAD_DIM)), jnp.tile(sin, (1, LANES // HEAD_DIM))


def _with_ones_rows(v_t):
    lead = v_t.shape[:-2]
    s = v_t.shape[-1]
    ones = jnp.ones(lead + (1, s), v_t.dtype)
    zeros = jnp.zeros(lead + (BF16_ROWS - 1, s), v_t.dtype)
    return jnp.concatenate([v_t, ones, zeros], axis=-2).astype(BF16)


def _cache_layouts(l, cache_gqa_k, cache_gqa_v, cache_diff_k, cache_diff_v):
    b = cache_gqa_k.shape[0]
    gk = jnp.transpose(cache_gqa_k[:, l], (0, 2, 1, 3))
    kd_g = gk.astype(BF16)
    vt_g = _with_ones_rows(jnp.transpose(cache_gqa_v[:, l], (0, 2, 3, 1)))
    dk = jnp.transpose(cache_diff_k[:, l], (0, 2, 3, 1, 4)).reshape(b, 2 * DIFF_HEADS, PAST_LEN, HEAD_DIM)
    kd_d = dk.astype(BF16)
    vt_d = _with_ones_rows(jnp.transpose(cache_diff_v[:, l], (0, 2, 3, 1)))
    return kd_g, vt_g, kd_d, vt_d


def _run_layer(x2d, mod, lw, sw, l, n_batch, n_tok, rope_tabs, ctx, emit_cache, prev_caches, cfg):
    tiles_per_mod = lambda tm: (n_tok // tm) if mod.shape[0] > 1 else (n_batch * n_tok // tm)
    if ctx is None:
        cache_kv = kd_gc = kd_dc = s0 = None
    else:
        kd_gc, vt_gc, kd_dc, vt_dc, s0 = ctx
        cache_kv = (kd_gc, kd_dc, vt_gc, vt_dc)
    outs = _in_projection(x2d, mod, lw, sw, l, n_batch, n_tok, cache_kv, rope_tabs, emit_cache, prev_caches,
                          cfg["tm_in"])
    qg, qd, kdg, kdd, vtg, vtd, lq, lk, lv, lgo, la, gates = outs[:12]
    og = _gqa_attention(qg, lw["kb_g"], kdg, vtg, kd_gc, n_batch, n_tok, cfg["tq_gqa"], cfg["tk"],
                        cfg["hp_gqa"], cfg["nb_attn"])
    lam_init = 0.8 - 0.6 * math.exp(-0.3 * l)
    od = _diff_attention(qd, lw["kb_d"], lw["lam"], lw["gsub"], kdd, vtd, kd_dc, n_batch, n_tok,
                         cfg["tq_diff"], cfg["tk"], cfg["hp_diff"], cfg["nb_attn"], lam_init)
    o_f, o_b, s_fin = _gla(lq, lk, lv, la, s0, n_batch, n_tok, cfg["tb_gla"], cfg["bg_gla"])
    x2d = _merge_ffn(x2d, mod, og, o_f, o_b, lgo, od, gates, lw, sw, l, tiles_per_mod(cfg["tm"]), cfg["tm"])
    cache = tuple(outs[12:]) + (s_fin,) if emit_cache else None
    return x2d, cache


PROMPT_CFG = dict(tm_in=512, tq_gqa=256, tq_diff=256, tk=256, hp_gqa=2, hp_diff=4, nb_attn=2, tb_gla=256, bg_gla=2,
                  tm=512)
SAMPLE_CFG = dict(tm_in=512, tq_gqa=1024, tq_diff=2048, tk=1536, hp_gqa=1, hp_diff=1, nb_attn=1, tb_gla=256, bg_gla=2,
                  tm=512)


def kernel(x_prompt, x_sample, c, cache_gqa_k, cache_gqa_v, state_gla, cache_diff_k, cache_diff_v, c_ctx, w_mod, b_mod, norm1, norm2, w_in, gqa_q_norm, gqa_k_norm, gla_alpha_up, gla_alpha_bias, gla_out_norm, diff_q_norm, diff_k_norm, diff_lambda, diff_sub_norm, w_branch, w_out, w_ffn_in, w_ffn_out):
    p = {
        "norm1": norm1, "norm2": norm2, "w_in": w_in, "gqa_q_norm": gqa_q_norm, "gqa_k_norm": gqa_k_norm,
        "gla_alpha_up": gla_alpha_up, "gla_alpha_bias": gla_alpha_bias, "gla_out_norm": gla_out_norm,
        "diff_q_norm": diff_q_norm, "diff_k_norm": diff_k_norm, "diff_lambda": diff_lambda,
        "diff_sub_norm": diff_sub_norm, "w_branch": w_branch, "w_out": w_out,
        "w_ffn_in": w_ffn_in, "w_ffn_out": w_ffn_out,
    }
    n_ctx_b, n_ctx = x_prompt.shape[:2]
    n_lat_b, n_lat = x_sample.shape[:2]
    cond_rows = jnp.concatenate(
        [c_ctx[None, :], c, jnp.zeros((F32_ROWS - 1 - n_lat_b, D_MODEL), F32)], axis=0)
    mod_all = _modulation(cond_rows, w_mod, b_mod)
    weights = [_layer_weights(l, p) for l in range(DEPTH)]
    shared = _shared_weights(p)

    y = x_prompt.reshape(n_ctx_b * n_ctx, D_MODEL)
    caches = []
    for l in range(DEPTH):
        mod = mod_all[l, 0:1].reshape(1, 6, D_MODEL)
        prev = [cc[:4] for cc in caches] if l == DEPTH - 1 else None
        y, cache = _run_layer(y, mod, weights[l], shared, l, n_ctx_b, n_ctx, None, None, True, prev, PROMPT_CFG)
        caches.append(cache)
    y_prompt = y.reshape(n_ctx_b, n_ctx, D_MODEL)
    stacked = caches[-1]
    new_gqa_k = stacked[0].reshape(n_ctx_b, DEPTH, n_ctx, GQA_KV_HEADS, HEAD_DIM)
    new_diff_k = stacked[1].reshape(n_ctx_b, DEPTH, n_ctx, DIFF_HEADS, 2, HEAD_DIM)
    new_gqa_v = stacked[2].reshape(n_ctx_b, DEPTH, n_ctx, GQA_KV_HEADS, HEAD_DIM)
    new_diff_v = stacked[3].reshape(n_ctx_b, DEPTH, n_ctx, DIFF_HEADS, DIFF_DV)
    new_state_gla = jnp.stack([cc[4] for cc in caches], axis=1)

    rope_tabs = _rope_tables(n_lat)
    y = x_sample.reshape(n_lat_b * n_lat, D_MODEL)
    for l in range(DEPTH):
        mod = mod_all[l, 1:1 + n_lat_b].reshape(n_lat_b, 6, D_MODEL)
        ctx = _cache_layouts(l, cache_gqa_k, cache_gqa_v, cache_diff_k, cache_diff_v) + (state_gla[:, l],)
        y, _ = _run_layer(y, mod, weights[l], shared, l, n_lat_b, n_lat, rope_tabs, ctx, False, None, SAMPLE_CFG)
    y_sample = y.reshape(n_lat_b, n_lat, D_MODEL)
    return (y_prompt, y_sample, new_gqa_k, new_gqa_v, new_state_gla, new_diff_k, new_diff_v)
```

```python
import functools
import math

import jax
import jax.numpy as jnp
from jax import lax
from jax.experimental import pallas as pl
from jax.experimental.pallas import tpu as pltpu

D_MODEL = 1024
DEPTH = 2
PAST_LEN = 512
GRID_W = 64
HEAD_DIM = 64
GQA_HEADS = 8
GQA_KV_HEADS = 2
GQA_GROUP = GQA_HEADS // GQA_KV_HEADS
GLA_HEADS = 4
GLA_DK = 64
GLA_DV = 128
GLA_RANK = 16
GLA_TAU = 16.0
GLA_CHUNK = 64
DIFF_HEADS = 4
DIFF_DV = 2 * HEAD_DIM
N_BRANCH = 3
BRANCH_W = 512
FFN_HIDDEN = ((8 * D_MODEL + 3 * 256 - 1) // (3 * 256)) * 256
ROPE_THETA = 10000.0
EPS = 1e-6

LANES = 128
BF16_ROWS = 16
F32_ROWS = 8
VMEM_LIMIT = 56 * 1024 * 1024

F32 = jnp.float32
BF16 = jnp.bfloat16

R_GQ = 0
R_KV = R_GQ + GQA_HEADS * HEAD_DIM
R_GLA = R_KV + 2 * GQA_KV_HEADS * HEAD_DIM
R_LR = R_GLA + 2 * GLA_HEADS * GLA_DK + GLA_HEADS * GLA_DV
R_LGO = R_LR + 2 * GLA_RANK
R_DQ = R_LGO + GLA_HEADS * GLA_DV
R_DK = R_DQ + DIFF_HEADS * 2 * HEAD_DIM
R_DV = R_DK + DIFF_HEADS * 2 * HEAD_DIM
R_MG = R_DV + DIFF_HEADS * DIFF_DV
N_IN = R_MG + N_BRANCH * D_MODEL
NT_DIMS = (((1,), (1,)), ((), ()))
N_QG = GQA_HEADS * HEAD_DIM
GLA_W = GLA_HEADS * GLA_DK
GQA_VT_ROWS = HEAD_DIM + BF16_ROWS
DIFF_VT_ROWS = DIFF_DV + BF16_ROWS


def _sigmoid(x):
    return 0.5 * jnp.tanh(0.5 * x) + 0.5


def _silu(x):
    return x * _sigmoid(x)


def _bf16_terms(x):
    hi = x.astype(BF16)
    r1 = x - hi.astype(F32)
    mid = r1.astype(BF16)
    lo = (r1 - mid.astype(F32)).astype(BF16)
    return hi, mid, lo


def _full_spec(shape):
    n = len(shape)
    return pl.BlockSpec(shape, lambda *_: (0,) * n, pipeline_mode=pl.Buffered(1))


def _layer_spec(shape, l):
    n = len(shape)
    return pl.BlockSpec((None,) + tuple(shape), lambda *_: (l,) + (0,) * n, pipeline_mode=pl.Buffered(1))


def _mod_kernel(cond_ref, w_ref, b_ref, o_ref):
    s_hi, s_mid, _ = _bf16_terms(_silu(cond_ref[...]))
    lhs = jnp.concatenate([s_hi, s_mid], axis=0)
    w = w_ref[0]
    w_hi = w.astype(BF16)
    w_mid = (w - w_hi.astype(F32)).astype(BF16)
    r = jnp.dot(lhs, w_hi, preferred_element_type=F32) + jnp.dot(lhs, w_mid, preferred_element_type=F32)
    n = cond_ref.shape[0]
    o_ref[0] = r[0:n] + r[n:2 * n] + b_ref[0]


def _modulation(cond_rows, w_mod, b_mod):
    tn = 1536
    n_out = 6 * D_MODEL
    return pl.pallas_call(
        _mod_kernel,
        grid=(DEPTH, n_out // tn),
        in_specs=[
            pl.BlockSpec((F32_ROWS, D_MODEL), lambda l, j: (0, 0)),
            pl.BlockSpec((1, D_MODEL, tn), lambda l, j: (l, 0, j)),
            pl.BlockSpec((1, 1, tn), lambda l, j: (l, 0, j)),
        ],
        out_specs=pl.BlockSpec((1, F32_ROWS, tn), lambda l, j: (l, 0, j)),
        out_shape=jax.ShapeDtypeStruct((DEPTH, F32_ROWS, n_out), F32),
        compiler_params=pltpu.CompilerParams(
            dimension_semantics=("parallel", "parallel"), vmem_limit_bytes=VMEM_LIMIT),
        name="modulation",
    )(cond_rows, w_mod, b_mod.reshape(DEPTH, 1, n_out))


def _head_norm(z, gain, gmat):
    outs = []
    n = z.shape[1]
    for c0 in range(0, n, 2 * LANES):
        w = min(2 * LANES, n - c0)
        zz = z[:, c0:c0 + w]
        ms = jnp.dot((zz * zz).astype(BF16), gmat[0:w, 0:w], preferred_element_type=F32)
        outs.append(zz * lax.rsqrt(ms + EPS) * gain[:, c0:c0 + w])
    return outs


def _rope_block(zb, cos, sin, first):
    partner = jnp.where(first, pltpu.roll(zb, LANES - 16, 1), pltpu.roll(zb, 16, 1))
    return zb * cos + partner * sin


N_IN_BASE = 9


def _in_kernel(*refs, rope, emit_cache, prev_layers, tiles_per_batch, cache_tiles):
    if not cache_tiles:
        _in_tile(refs, rope, emit_cache, prev_layers)
        return
    kcg_ref, kcd_ref, vcg_ref, vcd_ref = refs[:4]
    j = pl.program_id(1)

    @pl.when(j < tiles_per_batch)
    def _():
        _in_tile(refs[4:], rope, emit_cache, prev_layers)

    @pl.when(j >= tiles_per_batch)
    def _():
        n_in = 4 + N_IN_BASE + (2 if rope else 0)
        kdg_ref, kdd_ref, vtg_ref, vtd_ref = refs[n_in + 2:n_in + 6]
        kdg_ref[...] = kcg_ref[...]
        kdd_ref[...] = kcd_ref[...]
        vtg_ref[...] = vcg_ref[...]
        vtd_ref[...] = vcd_ref[...]


def _in_tile(refs, rope, emit_cache, prev_layers):
    it = iter(refs)
    (x_ref, mod_ref, g1_ref, wt_ref, gq_ref, gk_ref, gmat_ref, up_ref, ub_ref) = (
        next(it) for _ in range(N_IN_BASE))
    if rope:
        cos_ref, sin_ref = next(it), next(it)
    prev_refs = [[next(it) for _ in range(4)] for _ in range(prev_layers or 0)]
    (qg_ref, qd_ref, kdg_ref, kdd_ref, vtg_ref, vtd_ref,
     lq_ref, lk_ref, lv_ref, lgo_ref, la_ref, gate_ref) = (next(it) for _ in range(12))
    if emit_cache:
        ck_g_ref, ck_d_ref, cv_g_ref, cv_d_ref = (next(it) for _ in range(4))

    tm = x_ref.shape[0]
    bpt = kdg_ref.shape[0]
    nt = tm // bpt
    x = x_ref[...]
    ms = jnp.mean(x * x, axis=-1, keepdims=True)
    gain = g1_ref[...] * (1.0 + mod_ref[0, 1:2, :])
    hb = (x * lax.rsqrt(ms + EPS) * gain + mod_ref[0, 0:1, :]).astype(BF16)

    def proj(r0, n):
        return lax.dot_general(hb, wt_ref[r0:r0 + n, :], NT_DIMS, preferred_element_type=F32)

    lane = lax.broadcasted_iota(jnp.int32, (tm, LANES), 1)
    first16 = (lane % 32) < 16
    gmat = gmat_ref[...]
    if rope:
        cos, sin = cos_ref[...], sin_ref[...]

    def blocks128(parts):
        out = []
        for p in parts:
            for c in range(0, p.shape[1], LANES):
                out.append(p[:, c:c + LANES])
        return out

    def store_queries(z, gain, dst):
        for j, zb in enumerate(blocks128(_head_norm(z, gain, gmat))):
            if rope:
                zb = _rope_block(zb, cos, sin, first16)
            dst[:, j * LANES:(j + 1) * LANES] = zb.astype(BF16)

    def store_cache(dst, c0, value):
        w = value.shape[1]
        if prev_layers is None:
            dst[:, c0:c0 + w] = value
        else:
            for bb in range(bpt):
                dst[bb, prev_layers, :, c0:c0 + w] = value[bb * nt:(bb + 1) * nt, :]

    def store_keys(z, gain, dst, cache_dst):
        for j, zb in enumerate(blocks128(_head_norm(z, gain, gmat))):
            if rope:
                zb = _rope_block(zb, cos, sin, first16)
            if emit_cache:
                store_cache(cache_dst, j * LANES, zb)
            sw = pltpu.roll(zb, HEAD_DIM, 1)
            for bb in range(bpt):
                dst[bb, 2 * j] = zb[bb * nt:(bb + 1) * nt, 0:HEAD_DIM].astype(BF16)
                dst[bb, 2 * j + 1] = sw[bb * nt:(bb + 1) * nt, 0:HEAD_DIM].astype(BF16)

    def store_gate(b):
        gate_ref[:, b * D_MODEL:(b + 1) * D_MODEL] = _sigmoid(proj(R_MG + b * D_MODEL, D_MODEL)).astype(BF16)

    lr = proj(R_LR, LANES)
    z_gq = proj(R_GQ, N_QG)
    lr_hi, lr_mid, _ = _bf16_terms(lr)
    z_la = (jnp.dot(lr_hi, up_ref[0], preferred_element_type=F32)
            + jnp.dot(lr_mid, up_ref[0], preferred_element_type=F32)
            + jnp.dot(lr_hi, up_ref[1], preferred_element_type=F32)) + ub_ref[...]
    z_dq = proj(R_DQ, N_QG)
    log_sig = jnp.minimum(z_la, 0.0) - jnp.log(1.0 + jnp.exp(-jnp.abs(z_la)))
    la_ref[...] = log_sig * (1.0 / GLA_TAU)
    store_gate(0)
    store_queries(z_gq, gq_ref[:, 0:N_QG], qg_ref)
    z_kv = proj(R_KV, 2 * LANES)
    store_queries(z_dq, gq_ref[:, N_QG:2 * N_QG], qd_ref)
    z_dk = proj(R_DK, BRANCH_W)
    store_gate(1)
    store_keys(z_kv[:, 0:LANES], gk_ref[:, 0:LANES], kdg_ref, ck_g_ref if emit_cache else None)
    z_dv = proj(R_DV, BRANCH_W)
    store_keys(z_dk, gk_ref[:, LANES:LANES + BRANCH_W], kdd_ref, ck_d_ref if emit_cache else None)
    store_gate(2)
    zl = proj(R_GLA, R_LR - R_GLA)
    z_lgo = proj(R_LGO, BRANCH_W)

    ones_rows = jnp.where(lax.broadcasted_iota(jnp.int32, (BF16_ROWS, nt), 0) == 0, 1.0, 0.0).astype(BF16)
    z_gv = z_kv[:, LANES:2 * LANES]
    if emit_cache:
        store_cache(cv_g_ref, 0, z_gv)
        store_cache(cv_d_ref, 0, z_dv)
        for layer, layer_refs in enumerate(prev_refs):
            for dst, src in zip((ck_g_ref, ck_d_ref, cv_g_ref, cv_d_ref), layer_refs):
                for bb in range(bpt):
                    dst[bb, layer] = src[bb * nt:(bb + 1) * nt, :]
    for bb in range(bpt):
        rows = slice(bb * nt, (bb + 1) * nt)
        vt = z_gv[rows, :].T
        for hh in range(GQA_KV_HEADS):
            vtg_ref[bb, hh, 0:HEAD_DIM, :] = vt[hh * HEAD_DIM:(hh + 1) * HEAD_DIM, :].astype(BF16)
            vtg_ref[bb, hh, HEAD_DIM:GQA_VT_ROWS, :] = ones_rows
        for hh in range(DIFF_HEADS):
            vt = z_dv[rows, hh * LANES:(hh + 1) * LANES].T
            vtd_ref[bb, hh, 0:DIFF_DV, :] = vt.astype(BF16)
            vtd_ref[bb, hh, DIFF_DV:DIFF_VT_ROWS, :] = ones_rows

    o = 0
    lq_ref[...] = (zl[:, o:o + GLA_W] * (GLA_DK ** -0.5)).astype(BF16)
    o += GLA_W
    lk_ref[...] = zl[:, o:o + GLA_W].astype(BF16)
    o += GLA_W
    lv_ref[...] = zl[:, o:o + BRANCH_W].astype(BF16)
    o += BRANCH_W
    lgo_ref[...] = z_lgo.astype(BF16)


def _in_projection(x2d, mod, lw, sw, l, n_batch, n_tok, cache_kv, rope_tabs, emit_cache, prev_caches, tm):
    n_cache = 0 if cache_kv is None else cache_kv[0].shape[2]
    assert n_cache % tm == 0
    n_keys = n_tok + n_cache
    t_total = n_batch * n_tok
    bpt = max(1, tm // n_tok)
    assert bpt == 1 or (mod.shape[0] == 1 and n_cache == 0 and tm == bpt * n_tok)
    nt = tm // bpt
    tpb = n_tok // nt
    cache_tiles = n_cache // tm
    rope = rope_tabs is not None
    own = lambda j: jnp.minimum(j, tpb - 1)
    tile = lambda c: pl.BlockSpec((tm, c), lambda b, j: (b * tpb + own(j), 0))
    in_specs, args = [], []
    if cache_tiles:
        past = lambda j: jnp.maximum(j - tpb, 0)
        in_specs += [
            pl.BlockSpec((1, GQA_KV_HEADS, tm, HEAD_DIM), lambda b, j: (b, 0, past(j), 0)),
            pl.BlockSpec((1, 2 * DIFF_HEADS, tm, HEAD_DIM), lambda b, j: (b, 0, past(j), 0)),
            pl.BlockSpec((1, GQA_KV_HEADS, GQA_VT_ROWS, tm), lambda b, j: (b, 0, 0, past(j))),
            pl.BlockSpec((1, DIFF_HEADS, DIFF_VT_ROWS, tm), lambda b, j: (b, 0, 0, past(j))),
        ]
        args += list(cache_kv)
    in_specs += [
        tile(D_MODEL),
        pl.BlockSpec((1, 6, D_MODEL), lambda b, j: (b if mod.shape[0] > 1 else 0, 0, 0)),
        _full_spec((1, D_MODEL)),
        _layer_spec((N_IN, D_MODEL), l),
        _full_spec((1, 2 * N_QG)),
        _full_spec((1, LANES + BRANCH_W)),
        _full_spec((2 * LANES, 2 * LANES)),
        _full_spec((2, LANES, 2 * GLA_W)),
        _full_spec((1, 2 * GLA_W)),
    ]
    args += [x2d, mod, lw["norm1"], sw["w_in_t"], lw["gq"], lw["gk"], lw["gmat"], lw["up"], lw["ub"]]
    if rope:
        in_specs += [pl.BlockSpec((tm, LANES), lambda b, j: (own(j), 0))] * 2
        args += list(rope_tabs)
    sd = jax.ShapeDtypeStruct
    out_shape = [
        sd((t_total, BRANCH_W), BF16), sd((t_total, BRANCH_W), BF16),
        sd((n_batch, GQA_KV_HEADS, n_keys, HEAD_DIM), BF16), sd((n_batch, 2 * DIFF_HEADS, n_keys, HEAD_DIM), BF16),
        sd((n_batch, GQA_KV_HEADS, GQA_VT_ROWS, n_keys), BF16), sd((n_batch, DIFF_HEADS, DIFF_VT_ROWS, n_keys), BF16),
        sd((t_total, GLA_W), BF16), sd((t_total, GLA_W), BF16),
        sd((t_total, BRANCH_W), BF16), sd((t_total, BRANCH_W), BF16),
        sd((t_total, 2 * GLA_W), F32), sd((t_total, N_BRANCH * D_MODEL), BF16),
    ]
    out_specs = [
        tile(BRANCH_W), tile(BRANCH_W),
        pl.BlockSpec((bpt, GQA_KV_HEADS, nt, HEAD_DIM), lambda b, j: (b, 0, j, 0)),
        pl.BlockSpec((bpt, 2 * DIFF_HEADS, nt, HEAD_DIM), lambda b, j: (b, 0, j, 0)),
        pl.BlockSpec((bpt, GQA_KV_HEADS, GQA_VT_ROWS, nt), lambda b, j: (b, 0, 0, j)),
        pl.BlockSpec((bpt, DIFF_HEADS, DIFF_VT_ROWS, nt), lambda b, j: (b, 0, 0, j)),
        tile(GLA_W), tile(GLA_W), tile(BRANCH_W), tile(BRANCH_W), tile(2 * GLA_W), tile(N_BRANCH * D_MODEL),
    ]
    cache_widths = (LANES, BRANCH_W, LANES, BRANCH_W)
    prev_layers = None
    if emit_cache and prev_caches is None:
        out_shape += [sd((t_total, w), F32) for w in cache_widths]
        out_specs += [tile(w) for w in cache_widths]
    elif emit_cache:
        prev_layers = len(prev_caches)
        n_layers = prev_layers + 1
        for layer_arrays in prev_caches:
            in_specs += [tile(w) for w in cache_widths]
            args += list(layer_arrays)
        out_shape += [sd((n_batch, n_layers, n_tok, w), F32) for w in cache_widths]
        out_specs += [pl.BlockSpec((bpt, n_layers, nt, w), lambda b, j: (b, 0, j, 0)) for w in cache_widths]
    return pl.pallas_call(
        functools.partial(_in_kernel, rope=rope, emit_cache=emit_cache, prev_layers=prev_layers,
                          tiles_per_batch=tpb, cache_tiles=cache_tiles),
        grid=(n_batch // bpt, tpb + cache_tiles),
        in_specs=in_specs,
        out_specs=out_specs,
        out_shape=out_shape,
        compiler_params=pltpu.CompilerParams(
            dimension_semantics=("parallel", "arbitrary"), vmem_limit_bytes=VMEM_LIMIT),
        name="in_projection",
    )(*args)


MAX_COL = 4 * LANES
SCORE_BOUND = 45.0
NORM_MARGIN = 1.05
SCORES_AHEAD_BYTES = 4 * 1024 * 1024


def _head_queries_t(q_blk, low):
    t = q_blk.astype(F32).T
    return (t[0:HEAD_DIM, :] if low else t[HEAD_DIM:2 * HEAD_DIM, :]).astype(BF16)


def _key_norm2(kd):
    kf = kd.astype(F32)
    return jnp.max(jnp.sum(kf * kf, axis=1, keepdims=True))


def _needs_no_stabiliser(q_norm2, k_norm2):
    return (q_norm2 * k_norm2 <= SCORE_BOUND * SCORE_BOUND).astype(jnp.int32)


def _score_units(heads_cols):
    return [(be, kh, vh, c) for be, kh, vh, cols in heads_cols for c in cols]


def _tile_scores(k_ref, qm_ref, unit, col):
    be, kh, _, c = unit
    return jnp.dot(k_ref[be, kh], qm_ref[:, c * col:(c + 1) * col], preferred_element_type=F32)


def _plain_tiles(k_ref, vt_ref, qm_ref, acc_ref, heads_cols):
    col = acc_ref.shape[-1]
    units = _score_units(heads_cols)
    tile_bytes = k_ref.shape[2] * col * 4
    ahead = max(1, min(len(units), SCORES_AHEAD_BYTES // tile_bytes))
    scores = [_tile_scores(k_ref, qm_ref, unit, col) for unit in units[:ahead]]
    for i, (be, _, vh, c) in enumerate(units):
        if i + ahead < len(units):
            scores.append(_tile_scores(k_ref, qm_ref, units[i + ahead], col))
        p = jnp.exp2(scores[i]).astype(BF16)
        acc_ref[c] += jnp.dot(vt_ref[be, vh], p, preferred_element_type=F32)


def _online_tiles(k_ref, vt_ref, qm_ref, m_ref, acc_ref, heads_cols):
    col = acc_ref.shape[-1]
    for unit in _score_units(heads_cols):
        be, _, vh, c = unit
        s = _tile_scores(k_ref, qm_ref, unit, col)
        m_prev = m_ref[c]
        m_new = jnp.maximum(m_prev, jnp.max(s, axis=0, keepdims=True))
        alpha = jnp.exp2(m_prev - m_new)
        p = jnp.exp2(s - m_new).astype(BF16)
        acc_ref[c] = acc_ref[c] * alpha + jnp.dot(vt_ref[be, vh], p, preferred_element_type=F32)
        m_ref[c] = m_new


def _flash_init(kb_ref, kc_ref, n_key_heads, m_ref, acc_ref, plain_ref):
    k_norm2 = kb_ref[1]
    if kc_ref is not None:
        for be in range(kc_ref.shape[0]):
            for kh in range(n_key_heads):
                k_norm2 = jnp.maximum(k_norm2, _key_norm2(kc_ref[be, kh]))
    plain_ref[0] = _needs_no_stabiliser(kb_ref[0], k_norm2)
    m_ref[...] = jnp.full(m_ref.shape, -jnp.inf, F32)
    acc_ref[...] = jnp.zeros(acc_ref.shape, F32)


def _flash_step(k_ref, vt_ref, qm_ref, m_ref, acc_ref, plain_ref, heads_cols):
    @pl.when(plain_ref[0] == 1)
    def _():
        _plain_tiles(k_ref, vt_ref, qm_ref, acc_ref, heads_cols)

    @pl.when(plain_ref[0] != 1)
    def _():
        _online_tiles(k_ref, vt_ref, qm_ref, m_ref, acc_ref, heads_cols)


def _gqa_kernel(*refs, has_cache, tq, hp):
    if has_cache:
        kb_ref, q_ref, k_ref, vt_ref, kc_ref, o_ref, qm_ref, m_ref, acc_ref, plain_ref = refs
    else:
        kb_ref, q_ref, k_ref, vt_ref, o_ref, qm_ref, m_ref, acc_ref, plain_ref = refs
        kc_ref = None
    kt = pl.program_id(3)
    col = acc_ref.shape[-1]
    nb = q_ref.shape[0]
    per_head = tq // col
    per_kv = GQA_GROUP * per_head
    first = lambda be, hh: (be * hp + hh) * per_kv
    heads_cols = [(be, hh, hh, range(first(be, hh), first(be, hh) + per_kv))
                  for be in range(nb) for hh in range(hp)]

    @pl.when(kt == 0)
    def _():
        for be in range(nb):
            for hh in range(hp):
                for g in range(GQA_GROUP):
                    c0 = hh * 2 * LANES + (g // 2) * LANES
                    row0 = (first(be, hh) + g * per_head) * col
                    qm_ref[:, row0:row0 + tq] = _head_queries_t(q_ref[be, :, c0:c0 + LANES], g % 2 == 0)
        _flash_init(kb_ref, kc_ref, hp, m_ref, acc_ref, plain_ref)

    _flash_step(k_ref, vt_ref, qm_ref, m_ref, acc_ref, plain_ref, heads_cols)

    @pl.when(kt == pl.num_programs(3) - 1)
    def _():
        for be in range(nb):
            for hh in range(hp):
                for part in range(per_head):
                    heads = []
                    for g in range(GQA_GROUP):
                        acc = acc_ref[first(be, hh) + g * per_head + part]
                        heads.append(acc[0:HEAD_DIM, :] / acc[HEAD_DIM:HEAD_DIM + 1, :])
                    o_ref[be, part * col:(part + 1) * col, hh * 2 * LANES:(hh + 1) * 2 * LANES] = (
                        jnp.concatenate(heads, axis=0).T.astype(BF16))


def _kv_specs(nb, key_heads_blk, val_heads_blk, vt_rows, tk, kd_c):
    specs = [pl.BlockSpec((nb, key_heads_blk, tk, HEAD_DIM), lambda b, h, qi, kt: (b, h, kt, 0)),
             pl.BlockSpec((nb, val_heads_blk, vt_rows, tk), lambda b, h, qi, kt: (b, h, 0, kt))]
    if kd_c is not None:
        specs.append(pl.BlockSpec((nb, key_heads_blk, kd_c.shape[2], HEAD_DIM), lambda b, h, qi, kt: (b, h, 0, 0)))
    return specs


def _gqa_attention(q, k_bound, kd, vt, kd_c, n_batch, n_tok, tq, tk, hp, nb):
    has_cache = kd_c is not None
    nk = kd.shape[2] // tk
    rows = nb * hp * GQA_GROUP * tq
    col = min(MAX_COL, tq)
    q_spec = pl.BlockSpec((nb, tq, hp * 2 * LANES), lambda b, h, qi, kt: (b, qi, h))
    in_specs = [pl.BlockSpec(memory_space=pltpu.SMEM), q_spec]
    in_specs += _kv_specs(nb, hp, hp, GQA_VT_ROWS, tk, kd_c)
    args = [k_bound, q.reshape(n_batch, n_tok, BRANCH_W), kd, vt]
    if has_cache:
        args.append(kd_c)
    out = pl.pallas_call(
        functools.partial(_gqa_kernel, has_cache=has_cache, tq=tq, hp=hp),
        grid=(n_batch // nb, GQA_KV_HEADS // hp, n_tok // tq, nk),
        in_specs=in_specs,
        out_specs=q_spec,
        out_shape=jax.ShapeDtypeStruct((n_batch, n_tok, BRANCH_W), BF16),
        scratch_shapes=[
            pltpu.VMEM((HEAD_DIM, rows), BF16),
            pltpu.VMEM((rows // col, 1, col), F32),
            pltpu.VMEM((rows // col, GQA_VT_ROWS, col), F32),
            pltpu.SMEM((1,), jnp.int32),
        ],
        compiler_params=pltpu.CompilerParams(
            dimension_semantics=("parallel", "parallel", "parallel", "arbitrary"),
            vmem_limit_bytes=VMEM_LIMIT),
        name="gqa_attention",
    )(*args)
    return out.reshape(n_batch * n_tok, BRANCH_W)


def _diff_kernel(*refs, has_cache, lam_init, hp):
    if has_cache:
        (kb_ref, q_ref, lam_ref, gsub_ref, k_ref, vt_ref, kc_ref, o_ref,
         qm_ref, m_ref, acc_ref, plain_ref) = refs
    else:
        kb_ref, q_ref, lam_ref, gsub_ref, k_ref, vt_ref, o_ref, qm_ref, m_ref, acc_ref, plain_ref = refs
        kc_ref = None
    kt = pl.program_id(3)
    tq = q_ref.shape[1]
    col = acc_ref.shape[-1]
    nb = q_ref.shape[0]
    per_map = tq // col
    first = lambda be, hh, mm: ((be * hp + hh) * 2 + mm) * per_map
    heads_cols = [(be, 2 * hh + mm, hh, range(first(be, hh, mm), first(be, hh, mm) + per_map))
                  for be in range(nb) for hh in range(hp) for mm in range(2)]

    @pl.when(kt == 0)
    def _():
        for be in range(nb):
            for hh in range(hp):
                blk = q_ref[be, :, hh * LANES:(hh + 1) * LANES]
                for mm in range(2):
                    row0 = first(be, hh, mm) * col
                    qm_ref[:, row0:row0 + tq] = _head_queries_t(blk, mm == 0)
        _flash_init(kb_ref, kc_ref, 2 * hp, m_ref, acc_ref, plain_ref)

    _flash_step(k_ref, vt_ref, qm_ref, m_ref, acc_ref, plain_ref, heads_cols)

    @pl.when(kt == pl.num_programs(3) - 1)
    def _():
        lp = lam_ref[...]
        lam = (jnp.exp(jnp.sum(lp[0:1] * lp[1:2], axis=-1, keepdims=True))
               - jnp.exp(jnp.sum(lp[2:3] * lp[3:4], axis=-1, keepdims=True)) + lam_init)
        for be in range(nb):
            for hh in range(hp):
                for part in range(per_map):
                    a0, a1 = acc_ref[first(be, hh, 0) + part], acc_ref[first(be, hh, 1) + part]
                    o0 = a0[0:DIFF_DV, :] / a0[DIFF_DV:DIFF_DV + 1, :]
                    o1 = a1[0:DIFF_DV, :] / a1[DIFF_DV:DIFF_DV + 1, :]
                    d = (o0 - lam * o1).T
                    ms = jnp.mean(d * d, axis=-1, keepdims=True)
                    o_ref[be, part * col:(part + 1) * col, hh * LANES:(hh + 1) * LANES] = (
                        d * lax.rsqrt(ms + EPS) * gsub_ref[...] * (1.0 - lam_init)).astype(BF16)


def _diff_attention(q, k_bound, lam_p, gsub, kd, vt, kd_c, n_batch, n_tok, tq, tk, hp, nb, lam_init):
    has_cache = kd_c is not None
    nk = kd.shape[2] // tk
    col = min(MAX_COL, tq)
    rows = nb * hp * 2 * tq
    q_spec = pl.BlockSpec((nb, tq, hp * LANES), lambda b, h, qi, kt: (b, qi, h))
    in_specs = [
        pl.BlockSpec(memory_space=pltpu.SMEM),
        q_spec,
        pl.BlockSpec((4, HEAD_DIM), lambda b, h, qi, kt: (0, 0)),
        pl.BlockSpec((1, DIFF_DV), lambda b, h, qi, kt: (0, 0)),
    ]
    in_specs += _kv_specs(nb, 2 * hp, hp, DIFF_VT_ROWS, tk, kd_c)
    args = [k_bound, q.reshape(n_batch, n_tok, BRANCH_W), lam_p, gsub, kd, vt]
    if has_cache:
        args.append(kd_c)
    out = pl.pallas_call(
        functools.partial(_diff_kernel, has_cache=has_cache, lam_init=lam_init, hp=hp),
        grid=(n_batch // nb, DIFF_HEADS // hp, n_tok // tq, nk),
        in_specs=in_specs,
        out_specs=q_spec,
        out_shape=jax.ShapeDtypeStruct((n_batch, n_tok, BRANCH_W), BF16),
        scratch_shapes=[
            pltpu.VMEM((HEAD_DIM, rows), BF16),
            pltpu.VMEM((rows // col, 1, col), F32),
            pltpu.VMEM((rows // col, DIFF_VT_ROWS, col), F32),
            pltpu.SMEM((1,), jnp.int32),
        ],
        compiler_params=pltpu.CompilerParams(
            dimension_semantics=("parallel", "parallel", "parallel", "arbitrary"),
            vmem_limit_bytes=VMEM_LIMIT),
        name="diff_attention",
    )(*args)
    return out.reshape(n_batch * n_tok, BRANCH_W)


def _gla_kernel(*refs, has_s0, n_chunk, bg):
    if has_s0:
        qf, kf, vf, laf, qb, kb, vb, lab, s0_ref, of_ref, ob_ref, sfin_ref, st_ref = refs
    else:
        qf, kf, vf, laf, qb, kb, vb, lab, of_ref, ob_ref, sfin_ref, st_ref = refs
    i = pl.program_id(1)
    ck = GLA_CHUNK

    @pl.when(i == 0)
    def _():
        for bb in range(bg):
            for d in range(2):
                for hd in range(GLA_HEADS):
                    if has_s0:
                        s = s0_ref[bb, d, hd]
                        z = jnp.zeros_like(s)
                        padded = jnp.concatenate([s, z] if hd % 2 == 0 else [z, s], axis=0)
                        st_ref[bb, d, hd] = padded.T
                    else:
                        st_ref[bb, d, hd] = jnp.zeros((GLA_DV, LANES), F32)

    tb = qf.shape[1]
    r = lax.broadcasted_iota(jnp.int32, (tb, tb), 0)
    c = lax.broadcasted_iota(jnp.int32, (tb, tb), 1)
    lane = lax.broadcasted_iota(jnp.int32, (tb, LANES), 1)
    chunk_id = lax.broadcasted_iota(jnp.int32, (tb, GLA_W), 0) // ck
    zero_row = jnp.zeros((1, GLA_W), F32)
    streams = []
    for bb in range(bg):
        streams.append((bb, 0, qf.at[bb], kf.at[bb], vf.at[bb], laf.at[bb], of_ref.at[bb], c <= r))
        streams.append((bb, 1, qb.at[bb], kb.at[bb], vb.at[bb], lab.at[bb], ob_ref.at[bb], c >= r))
    nt = (((1,), (1,)), ((), ()))
    g_all = [sum(jnp.dot(tri.astype(F32).astype(BF16), part, preferred_element_type=F32)
                 for part in _bf16_terms(la_r[...]))
             for (_, _, _, _, _, la_r, _, tri) in streams]
    prepared = []
    for g, (bb, d, q_r, k_r, v_r, la_r, o_r, tri) in zip(g_all, streams):
        if d == 0:
            bounds = [zero_row] + [g[ck * j - 1:ck * j, :] for j in range(1, n_chunk)]
            g_end = g[tb - 1:tb, :]
        else:
            bounds = [g[ck * (j + 1):ck * (j + 1) + 1, :] for j in range(n_chunk - 1)] + [zero_row]
            g_end = g[0:1, :]
        b_rows = jnp.concatenate([jnp.broadcast_to(b, (ck, GLA_W)) for b in bounds], axis=0)
        q = q_r[...].astype(F32)
        k = k_r[...].astype(F32)
        q_dec = q * jnp.exp(g - b_rows)
        q_glob = q * jnp.exp(g)
        k_end = k * jnp.exp(g_end - g)
        k_rel = []
        for j in range(n_chunk):
            reach = (chunk_id <= j) if d == 0 else (chunk_id >= j)
            k_rel.append((k * jnp.exp(jnp.where(reach, bounds[j] - g, 0.0))).astype(BF16))
        prepared.append((q_dec, q_glob, k_end, k_rel, jnp.exp(g_end), v_r[...]))

    partial = []
    for (bb, d, *_), (q_dec, q_glob, k_end, k_rel, decay, v) in zip(streams, prepared):
        per_head = []
        for hd in range(GLA_HEADS):
            pair = slice((hd // 2) * LANES, (hd // 2 + 1) * LANES)
            hv = slice(hd * GLA_DV, (hd + 1) * GLA_DV)
            keep = (lane < GLA_DK) if hd % 2 == 0 else (lane >= GLA_DK)
            qd_m = jnp.where(keep, q_dec[:, pair], 0.0).astype(BF16)
            qg_m = jnp.where(keep, q_glob[:, pair], 0.0).astype(BF16)
            ke_m = jnp.where(keep, k_end[:, pair], 0.0).astype(BF16)
            a_rows = [lax.dot_general(qd_m[ck * j:ck * (j + 1), :], k_rel[j][:, pair], nt,
                                      preferred_element_type=F32) for j in range(n_chunk)]
            s_t = st_ref[bb, d, hd]
            o_state = lax.dot_general(qg_m, s_t.astype(BF16), nt, preferred_element_type=F32)
            ds_t = lax.dot_general(v[:, hv], ke_m, (((0,), (0,)), ((), ())), preferred_element_type=F32)
            st_ref[bb, d, hd] = s_t * decay[:, pair] + ds_t
            per_head.append((a_rows, o_state))
        partial.append(per_head)

    for (bb, d, _, _, _, _, o_r, tri), prep, per_head in zip(streams, prepared, partial):
        v = prep[5]
        for hd in range(GLA_HEADS):
            hv = slice(hd * GLA_DV, (hd + 1) * GLA_DV)
            a_rows, o_state = per_head[hd]
            a = jnp.where(tri, jnp.concatenate(a_rows, axis=0), 0.0).astype(BF16)
            o_r[:, hv] = (jnp.dot(a, v[:, hv], preferred_element_type=F32) + o_state).astype(BF16)

    @pl.when(i == pl.num_programs(1) - 1)
    def _():
        for bb in range(bg):
            for d in range(2):
                for hd in range(GLA_HEADS):
                    t = st_ref[bb, d, hd].T
                    sfin_ref[bb, d, hd] = t[(hd % 2) * GLA_DK:(hd % 2 + 1) * GLA_DK, :]


def _gla(lq, lk, lv, la, s0, n_batch, n_tok, tb, bg):
    nb = n_tok // tb
    has_s0 = s0 is not None
    as3d = lambda x: x.reshape(n_batch, n_tok, x.shape[-1])
    fwd = lambda b, i: (b, i, 0)
    bwd = lambda b, i: (b, nb - 1 - i, 0)
    bwd_la = lambda b, i: (b, nb - 1 - i, 1)
    in_specs = [
        pl.BlockSpec((bg, tb, GLA_W), fwd), pl.BlockSpec((bg, tb, GLA_W), fwd),
        pl.BlockSpec((bg, tb, BRANCH_W), fwd), pl.BlockSpec((bg, tb, GLA_W), fwd),
        pl.BlockSpec((bg, tb, GLA_W), bwd), pl.BlockSpec((bg, tb, GLA_W), bwd),
        pl.BlockSpec((bg, tb, BRANCH_W), bwd), pl.BlockSpec((bg, tb, GLA_W), bwd_la),
    ]
    args = [as3d(lq), as3d(lk), as3d(lv), as3d(la)] * 2
    state_spec = pl.BlockSpec((bg, 2, GLA_HEADS, GLA_DK, GLA_DV), lambda b, i: (b, 0, 0, 0, 0))
    if has_s0:
        in_specs.append(state_spec)
        args.append(s0)
    o_f, o_b, s_fin = pl.pallas_call(
        functools.partial(_gla_kernel, has_s0=has_s0, n_chunk=tb // GLA_CHUNK, bg=bg),
        grid=(n_batch // bg, nb),
        in_specs=in_specs,
        out_specs=[pl.BlockSpec((bg, tb, BRANCH_W), fwd), pl.BlockSpec((bg, tb, BRANCH_W), bwd), state_spec],
        out_shape=[
            jax.ShapeDtypeStruct((n_batch, n_tok, BRANCH_W), BF16),
            jax.ShapeDtypeStruct((n_batch, n_tok, BRANCH_W), BF16),
            jax.ShapeDtypeStruct((n_batch, 2, GLA_HEADS, GLA_DK, GLA_DV), F32),
        ],
        scratch_shapes=[pltpu.VMEM((bg, 2, GLA_HEADS, GLA_DV, LANES), F32)],
        compiler_params=pltpu.CompilerParams(
            dimension_semantics=("parallel", "arbitrary"), vmem_limit_bytes=VMEM_LIMIT),
        name="gla",
    )(*args)
    flat = lambda x: x.reshape(n_batch * n_tok, BRANCH_W)
    return flat(o_f), flat(o_b), s_fin


FFN_CHUNK = 256


def _merge_ffn_kernel(x_ref, mod_ref, og_ref, of_ref, ob_ref, lgo_ref, od_ref, gate_ref,
                      gout_ref, wb_ref, wo_ref, g2_ref, wi_hbm, wd_hbm, y_ref, wi_ref, wd_ref, sem, *, layer):
    copies = (pltpu.make_async_copy(wi_hbm.at[layer], wi_ref, sem.at[0]),
              pltpu.make_async_copy(wd_hbm.at[layer], wd_ref, sem.at[1]))
    first_step = pl.program_id(0) == 0

    @pl.when(first_step)
    def _():
        for cp in copies:
            cp.start()

    o_gla = of_ref[...].astype(F32) + ob_ref[...].astype(F32)
    gla_parts = []
    for hd in range(GLA_HEADS):
        blk = o_gla[:, hd * GLA_DV:(hd + 1) * GLA_DV]
        ms = jnp.mean(blk * blk, axis=-1, keepdims=True)
        gla_parts.append(blk * lax.rsqrt(ms + EPS) * gout_ref[...])
    gla = jnp.concatenate(gla_parts, axis=1) * _silu(lgo_ref[...].astype(F32))
    branches = (og_ref[...], gla.astype(BF16), od_ref[...])
    mixed = None
    for b, ob in enumerate(branches):
        y = jnp.dot(ob, wb_ref[b], preferred_element_type=F32)
        y = y * gate_ref[:, b * D_MODEL:(b + 1) * D_MODEL].astype(F32)
        mixed = y if mixed is None else mixed + y
    out = jnp.dot(mixed.astype(BF16), wo_ref[...], preferred_element_type=F32)
    x = x_ref[...] + mod_ref[0, 2:3, :] * out

    ms = jnp.mean(x * x, axis=-1, keepdims=True)
    gain = g2_ref[...] * (1.0 + mod_ref[0, 4:5, :])
    hb = (x * lax.rsqrt(ms + EPS) * gain + mod_ref[0, 3:4, :]).astype(BF16)

    @pl.when(first_step)
    def _():
        for cp in copies:
            cp.wait()

    def up_proj(c0):
        a = jnp.dot(hb, wi_ref[:, c0:c0 + FFN_CHUNK], preferred_element_type=F32)
        u = jnp.dot(hb, wi_ref[:, FFN_HIDDEN + c0:FFN_HIDDEN + c0 + FFN_CHUNK], preferred_element_type=F32)
        return a, u

    acc = None
    chunks = list(range(0, FFN_HIDDEN, FFN_CHUNK))
    nxt = up_proj(chunks[0])
    for i, c0 in enumerate(chunks):
        a, u = nxt
        if i + 1 < len(chunks):
            nxt = up_proj(chunks[i + 1])
        act = (_silu(a) * u).astype(BF16)
        part = jnp.dot(act, wd_ref[c0:c0 + FFN_CHUNK, :], preferred_element_type=F32)
        acc = part if acc is None else acc + part
    y_ref[...] = x + mod_ref[0, 5:6, :] * acc


def _merge_ffn(x2d, mod, og, o_f, o_b, lgo, od, gates, lw, sw, l, tiles_per_mod, tm):
    t_total = x2d.shape[0]
    tile = lambda c: pl.BlockSpec((tm, c), lambda i: (i, 0))
    return pl.pallas_call(
        functools.partial(_merge_ffn_kernel, layer=l),
        grid=(t_total // tm,),
        in_specs=[
            tile(D_MODEL),
            pl.BlockSpec((1, 6, D_MODEL), lambda i: (i // tiles_per_mod, 0, 0)),
            tile(BRANCH_W), tile(BRANCH_W), tile(BRANCH_W), tile(BRANCH_W), tile(BRANCH_W),
            tile(N_BRANCH * D_MODEL),
            _full_spec((1, GLA_DV)),
            _layer_spec((N_BRANCH, BRANCH_W, D_MODEL), l),
            _layer_spec((D_MODEL, D_MODEL), l),
            _full_spec((1, D_MODEL)),
            pl.BlockSpec(memory_space=pl.ANY),
            pl.BlockSpec(memory_space=pl.ANY),
        ],
        out_specs=tile(D_MODEL),
        out_shape=jax.ShapeDtypeStruct((t_total, D_MODEL), F32),
        scratch_shapes=[
            pltpu.VMEM((D_MODEL, 2 * FFN_HIDDEN), BF16),
            pltpu.VMEM((FFN_HIDDEN, D_MODEL), BF16),
            pltpu.SemaphoreType.DMA((2,)),
        ],
        compiler_params=pltpu.CompilerParams(
            dimension_semantics=("arbitrary",), vmem_limit_bytes=VMEM_LIMIT),
        name="merge_ffn",
    )(x2d, mod, og, o_f, o_b, lgo, od, gates, lw["gout"], sw["w_branch"], sw["w_out"],
      lw["norm2"], sw["w_ffn_in"], sw["w_ffn_out"])


def _shared_weights(p):
    return {
        "w_in_t": jnp.swapaxes(p["w_in"], 1, 2).astype(BF16),
        "w_branch": p["w_branch"].astype(BF16), "w_out": p["w_out"].astype(BF16),
        "w_ffn_in": p["w_ffn_in"].astype(BF16), "w_ffn_out": p["w_ffn_out"].astype(BF16),
    }


def _layer_weights(l, p):
    scale = HEAD_DIM ** -0.5 * math.log2(math.e)
    head_bound = lambda g, s: HEAD_DIM * (NORM_MARGIN * s) ** 2 * jnp.max(g * g)
    norm_bounds = lambda gq_, gk_: jnp.stack([head_bound(gq_, scale), head_bound(gk_, 1.0)])
    gq_row = jnp.concatenate([jnp.tile(p["gqa_q_norm"][l], GQA_HEADS),
                              jnp.tile(p["diff_q_norm"][l], 2 * DIFF_HEADS)]) * scale
    gk_row = jnp.concatenate([jnp.tile(p["gqa_k_norm"][l], GQA_KV_HEADS),
                              jnp.tile(p["diff_k_norm"][l], 2 * DIFF_HEADS)])
    idx = jnp.arange(2 * LANES) // HEAD_DIM
    gmat = jnp.where(idx[:, None] == idx[None, :], 1.0 / HEAD_DIM, 0.0).astype(BF16)
    up = jnp.zeros((LANES, 2 * GLA_W), F32)
    up = up.at[0:GLA_RANK, 0:GLA_W].set(p["gla_alpha_up"][l, 0])
    up = up.at[GLA_RANK:2 * GLA_RANK, GLA_W:].set(p["gla_alpha_up"][l, 1])
    ub = p["gla_alpha_bias"][l].reshape(1, 2 * GLA_W)
    up_hi = up.astype(BF16)
    up = jnp.stack([up_hi, (up - up_hi.astype(F32)).astype(BF16)])
    return {
        "gq": gq_row.reshape(1, 2 * N_QG), "gk": gk_row.reshape(1, LANES + BRANCH_W), "gmat": gmat,
        "up": up, "ub": ub,
        "kb_g": norm_bounds(p["gqa_q_norm"][l], p["gqa_k_norm"][l]),
        "kb_d": norm_bounds(p["diff_q_norm"][l], p["diff_k_norm"][l]),
        "norm1": p["norm1"][l].reshape(1, D_MODEL), "norm2": p["norm2"][l].reshape(1, D_MODEL),
        "gout": p["gla_out_norm"][l].reshape(1, GLA_DV),
        "gsub": p["diff_sub_norm"][l].reshape(1, DIFF_DV),
        "lam": p["diff_lambda"][l],
    }


def _rope_tables(n_tokens):
    n_rows = n_tokens // GRID_W
    row = jnp.repeat(jnp.arange(n_rows, dtype=F32), GRID_W)
    col = jnp.tile(jnp.arange(GRID_W, dtype=F32), n_rows)
    n_freq = HEAD_DIM // 4
    freqs = ROPE_THETA ** (-jnp.arange(n_freq, dtype=F32) / n_freq)
    ar, ac = row[:, None] * freqs, col[:, None] * freqs
    cos = jnp.concatenate([jnp.cos(ar), jnp.cos(ar), jnp.cos(ac), jnp.cos(ac)], axis=1)
    sin = jnp.concatenate([-jnp.sin(ar), jnp.sin(ar), -jnp.sin(ac), jnp.sin(ac)], axis=1)
    return jnp.tile(cos, (1, LANES // HEAD_DIM)), jnp.tile(sin, (1, LANES // HEAD_DIM))


def _with_ones_rows(v_t):
    lead = v_t.shape[:-2]
    s = v_t.shape[-1]
    ones = jnp.ones(lead + (1, s), v_t.dtype)
    zeros = jnp.zeros(lead + (BF16_ROWS - 1, s), v_t.dtype)
    return jnp.concatenate([v_t, ones, zeros], axis=-2).astype(BF16)


def _cache_layouts(l, cache_gqa_k, cache_gqa_v, cache_diff_k, cache_diff_v):
    b = cache_gqa_k.shape[0]
    gk = jnp.transpose(cache_gqa_k[:, l], (0, 2, 1, 3))
    kd_g = gk.astype(BF16)
    vt_g = _with_ones_rows(jnp.transpose(cache_gqa_v[:, l], (0, 2, 3, 1)))
    dk = jnp.transpose(cache_diff_k[:, l], (0, 2, 3, 1, 4)).reshape(b, 2 * DIFF_HEADS, PAST_LEN, HEAD_DIM)
    kd_d = dk.astype(BF16)
    vt_d = _with_ones_rows(jnp.transpose(cache_diff_v[:, l], (0, 2, 3, 1)))
    return kd_g, vt_g, kd_d, vt_d


def _run_layer(x2d, mod, lw, sw, l, n_batch, n_tok, rope_tabs, ctx, emit_cache, prev_caches, cfg):
    tiles_per_mod = lambda tm: (n_tok // tm) if mod.shape[0] > 1 else (n_batch * n_tok // tm)
    if ctx is None:
        cache_kv = kd_gc = kd_dc = s0 = None
    else:
        kd_gc, vt_gc, kd_dc, vt_dc, s0 = ctx
        cache_kv = (kd_gc, kd_dc, vt_gc, vt_dc)
    outs = _in_projection(x2d, mod, lw, sw, l, n_batch, n_tok, cache_kv, rope_tabs, emit_cache, prev_caches,
                          cfg["tm_in"])
    qg, qd, kdg, kdd, vtg, vtd, lq, lk, lv, lgo, la, gates = outs[:12]
    og = _gqa_attention(qg, lw["kb_g"], kdg, vtg, kd_gc, n_batch, n_tok, cfg["tq_gqa"], cfg["tk"],
                        cfg["hp_gqa"], cfg["nb_attn"])
    lam_init = 0.8 - 0.6 * math.exp(-0.3 * l)
    od = _diff_attention(qd, lw["kb_d"], lw["lam"], lw["gsub"], kdd, vtd, kd_dc, n_batch, n_tok,
                         cfg["tq_diff"], cfg["tk"], cfg["hp_diff"], cfg["nb_attn"], lam_init)
    o_f, o_b, s_fin = _gla(lq, lk, lv, la, s0, n_batch, n_tok, cfg["tb_gla"], cfg["bg_gla"])
    x2d = _merge_ffn(x2d, mod, og, o_f, o_b, lgo, od, gates, lw, sw, l, tiles_per_mod(cfg["tm"]), cfg["tm"])
    cache = tuple(outs[12:]) + (s_fin,) if emit_cache else None
    return x2d, cache


PROMPT_CFG = dict(tm_in=512, tq_gqa=256, tq_diff=256, tk=256, hp_gqa=2, hp_diff=4, nb_attn=2, tb_gla=256, bg_gla=2,
                  tm=512)
SAMPLE_CFG = dict(tm_in=512, tq_gqa=1024, tq_diff=2048, tk=1536, hp_gqa=1, hp_diff=1, nb_attn=1, tb_gla=256, bg_gla=2,
                  tm=512)


def kernel(x_prompt, x_sample, c, cache_gqa_k, cache_gqa_v, state_gla, cache_diff_k, cache_diff_v, c_ctx, w_mod, b_mod, norm1, norm2, w_in, gqa_q_norm, gqa_k_norm, gla_alpha_up, gla_alpha_bias, gla_out_norm, diff_q_norm, diff_k_norm, diff_lambda, diff_sub_norm, w_branch, w_out, w_ffn_in, w_ffn_out):
    p = {
        "norm1": norm1, "norm2": norm2, "w_in": w_in, "gqa_q_norm": gqa_q_norm, "gqa_k_norm": gqa_k_norm,
        "gla_alpha_up": gla_alpha_up, "gla_alpha_bias": gla_alpha_bias, "gla_out_norm": gla_out_norm,
        "diff_q_norm": diff_q_norm, "diff_k_norm": diff_k_norm, "diff_lambda": diff_lambda,
        "diff_sub_norm": diff_sub_norm, "w_branch": w_branch, "w_out": w_out,
        "w_ffn_in": w_ffn_in, "w_ffn_out": w_ffn_out,
    }
    n_ctx_b, n_ctx = x_prompt.shape[:2]
    n_lat_b, n_lat = x_sample.shape[:2]
    cond_rows = jnp.concatenate(
        [c_ctx[None, :], c, jnp.zeros((F32_ROWS - 1 - n_lat_b, D_MODEL), F32)], axis=0)
    mod_all = _modulation(cond_rows, w_mod, b_mod)
    weights = [_layer_weights(l, p) for l in range(DEPTH)]
    shared = _shared_weights(p)

    y = x_prompt.reshape(n_ctx_b * n_ctx, D_MODEL)
    caches = []
    for l in range(DEPTH):
        mod = mod_all[l, 0:1].reshape(1, 6, D_MODEL)
        prev = [cc[:4] for cc in caches] if l == DEPTH - 1 else None
        y, cache = _run_layer(y, mod, weights[l], shared, l, n_ctx_b, n_ctx, None, None, True, prev, PROMPT_CFG)
        caches.append(cache)
    y_prompt = y.reshape(n_ctx_b, n_ctx, D_MODEL)
    stacked = caches[-1]
    new_gqa_k = stacked[0].reshape(n_ctx_b, DEPTH, n_ctx, GQA_KV_HEADS, HEAD_DIM)
    new_diff_k = stacked[1].reshape(n_ctx_b, DEPTH, n_ctx, DIFF_HEADS, 2, HEAD_DIM)
    new_gqa_v = stacked[2].reshape(n_ctx_b, DEPTH, n_ctx, GQA_KV_HEADS, HEAD_DIM)
    new_diff_v = stacked[3].reshape(n_ctx_b, DEPTH, n_ctx, DIFF_HEADS, DIFF_DV)
    new_state_gla = jnp.stack([cc[4] for cc in caches], axis=1)

    rope_tabs = _rope_tables(n_lat)
    y = x_sample.reshape(n_lat_b * n_lat, D_MODEL)
    for l in range(DEPTH):
        mod = mod_all[l, 1:1 + n_lat_b].reshape(n_lat_b, 6, D_MODEL)
        ctx = _cache_layouts(l, cache_gqa_k, cache_gqa_v, cache_diff_k, cache_diff_v) + (state_gla[:, l],)
        y, _ = _run_layer(y, mod, weights[l], shared, l, n_lat_b, n_lat, rope_tabs, ctx, False, None, SAMPLE_CFG)
    y_sample = y.reshape(n_lat_b, n_lat, D_MODEL)
    return (y_prompt, y_sample, new_gqa_k, new_gqa_v, new_state_gla, new_diff_k, new_diff_v)
```
